```python
import jax, jax.numpy as jnp
from jax import lax
import numpy as np


D_MODEL = 1024
BATCH = 16
SEQ = 4096
DEPTH = 1

MIX_WIDTH = D_MODEL
GLA_HEADS = 4
GLA_WIDTH = MIX_WIDTH // 2
GLA_DV = GLA_WIDTH // GLA_HEADS
GLA_DK = GLA_DV // 2
GLA_QK = GLA_HEADS * GLA_DK
GLA_GATE_RANK = 16
GLA_GATE_NORM = 16.0
GDN_HEADS = 4
GDN_WIDTH = MIX_WIDTH - GLA_WIDTH
GDN_DK = GDN_WIDTH // GDN_HEADS
GDN_DV = GDN_DK
CONV_WIDTH = 4
CHUNK = 64
LN_EPS = 1e-5
RMS_EPS = 1e-6
ALPHA = (2.0 * DEPTH) ** 0.25
BETA_INIT = (8.0 * DEPTH) ** -0.25

IN_SIZES = (
    GLA_QK,
    GLA_QK,
    GLA_WIDTH,
    GLA_GATE_RANK,
    GLA_WIDTH,
    3 * GDN_WIDTH,
    GDN_HEADS,
    GDN_HEADS,
    GDN_WIDTH,
)
IN_COLS = sum(IN_SIZES)

kernel_name = 'hymba_gla_gdn_deepnorm_adaln'


def _split_cols(t, sizes):
    idx = []
    acc = 0
    for s in sizes[:-1]:
        acc += s
        idx.append(acc)
    return jnp.split(t, idx, axis=-1)


def layer_norm(u, w, b):
    u32 = u.astype(jnp.float32)
    mu = jnp.mean(u32, axis=-1, keepdims=True)
    var = jnp.mean(jnp.square(u32 - mu), axis=-1, keepdims=True)
    return ((u32 - mu) * lax.rsqrt(var + LN_EPS) * w + b).astype(u.dtype)


def rms_norm(u, w):
    u32 = u.astype(jnp.float32)
    return (u32 * lax.rsqrt(jnp.mean(jnp.square(u32), axis=-1, keepdims=True) + RMS_EPS) * w).astype(u.dtype)


def l2_norm(u):
    u32 = u.astype(jnp.float32)
    return (u32 * lax.rsqrt(jnp.sum(jnp.square(u32), axis=-1, keepdims=True) + RMS_EPS)).astype(u.dtype)


def causal_depthwise_conv(u, w):
    K, C = w.shape
    return lax.conv_general_dilated(u, w[:, None, :].astype(u.dtype), window_strides=(1,),
                                    padding=((K - 1, 0),), dimension_numbers=('NWC', 'WIO', 'NWC'),
                                    feature_group_count=C)


def _to_chunks(t, n_chunks):
    B, T, H = t.shape[:3]
    t = t.reshape((B, n_chunks, CHUNK, H) + t.shape[3:])
    return jnp.moveaxis(t, 3, 1)


def _from_chunks(t):
    B, H, N, C, d = t.shape
    return jnp.moveaxis(t, 1, 3).reshape(B, N * C, H, d)


def gla_chunked(q, k, v, g):
    out_dtype = v.dtype
    B, T, H, dk = q.shape
    dv = v.shape[-1]
    N = T // CHUNK
    q, k, v, g = (_to_chunks(t.astype(jnp.float32), N) for t in (q, k, v, g))
    b = jnp.cumsum(g, axis=3)
    b_ref = b[:, :, :, CHUNK // 2 - 1:CHUNK // 2, :]
    causal = jnp.tril(jnp.ones((CHUNK, CHUNK), dtype=bool))
    att = jnp.einsum('bhnid,bhnjd->bhnij', q * jnp.exp(b - b_ref), k * jnp.exp(b_ref - b))
    att = jnp.where(causal, att, 0.0)
    o_intra = jnp.einsum('bhnij,bhnjv->bhniv', att, v)
    b_last = b[:, :, :, -1, :]
    upd = jnp.einsum('bhncd,bhncv->bhndv', k * jnp.exp(b_last[:, :, :, None, :] - b), v)
    decay = jnp.exp(b_last)

    def step(S, inp):
        d_n, u_n = inp
        return d_n[..., None] * S + u_n, S

    S0 = jnp.zeros((B, H, dk, dv), jnp.float32)
    _, S_prev = lax.scan(step, S0, (jnp.moveaxis(decay, 2, 0), jnp.moveaxis(upd, 2, 0)))
    S_prev = jnp.moveaxis(S_prev, 0, 2)
    o_inter = jnp.einsum('bhncd,bhndv->bhncv', q * jnp.exp(b), S_prev)
    return _from_chunks(o_intra + o_inter).astype(out_dtype)


def gated_delta_chunked(q, k, v, g, beta):
    out_dtype = v.dtype
    B, T, H, dk = q.shape
    dv = v.shape[-1]
    N = T // CHUNK
    q, k, v = (_to_chunks(t.astype(jnp.float32), N) for t in (q, k, v))
    g, beta = (_to_chunks(t.astype(jnp.float32), N) for t in (g, beta))
    d = jnp.cumsum(g, axis=-1)
    causal = jnp.tril(jnp.ones((CHUNK, CHUNK), dtype=bool))
    strict = jnp.tril(jnp.ones((CHUNK, CHUNK), dtype=bool), k=-1)
    L = jnp.exp(jnp.where(causal, d[..., :, None] - d[..., None, :], -jnp.inf))
    k_beta = k * beta[..., None]
    A = jnp.where(strict, jnp.einsum('bhnid,bhnjd->bhnij', k_beta, k) * L, 0.0)
    eye = jnp.eye(CHUNK, dtype=jnp.float32)
    rhs = jnp.concatenate([v * beta[..., None], k_beta * jnp.exp(d)[..., None]], axis=-1)
    sol = lax.linalg.triangular_solve(A + eye, rhs, left_side=True, lower=True, unit_diagonal=True)
    u, w = sol[..., :dv], sol[..., dv:]
    qk = jnp.where(causal, jnp.einsum('bhnid,bhnjd->bhnij', q, k) * L, 0.0)
    q_dec = q * jnp.exp(d)[..., None]
    k_dec = k * jnp.exp(d[..., -1:] - d)[..., None]
    chunk_decay = jnp.exp(d[..., -1])

    def step(S, inp):
        qk_n, q_n, k_n, u_n, w_n, dec_n = inp
        v_new = u_n - jnp.einsum('bhcd,bhdv->bhcv', w_n, S)
        o_n = jnp.einsum('bhcd,bhdv->bhcv', q_n, S) + jnp.einsum('bhij,bhjv->bhiv', qk_n, v_new)
        S = dec_n[..., None, None] * S + jnp.einsum('bhcd,bhcv->bhdv', k_n, v_new)
        return S, o_n

    xs = tuple(jnp.moveaxis(t, 2, 0) for t in (qk, q_dec, k_dec, u, w, chunk_decay))
    S0 = jnp.zeros((B, H, dk, dv), jnp.float32)
    _, o = lax.scan(step, S0, xs)
    return _from_chunks(jnp.moveaxis(o, 0, 2)).astype(out_dtype)


def hybrid_layer(x, c, w_ada, b_ada, w_in, gla_w_gate_up, gla_b_gate, gla_norm_w,
                 gdn_conv_w, gdn_a_log, gdn_dt_bias, gdn_norm_w, w_out, ln_w, ln_b):
    B, T, _ = x.shape
    mod = (c @ w_ada + b_ada)[:, None, :]
    shift, scale, gate = jnp.split(mod, 3, axis=-1)
    h = x * (1.0 + scale) + shift
    proj = h @ w_in
    (gla_q, gla_k, gla_v, gla_lr, gla_og, gdn_qkv, gdn_a, gdn_b, gdn_og) = _split_cols(proj, IN_SIZES)

    q = gla_q.reshape(B, T, GLA_HEADS, GLA_DK) * (GLA_DK ** -0.5)
    k = gla_k.reshape(B, T, GLA_HEADS, GLA_DK)
    v = gla_v.reshape(B, T, GLA_HEADS, GLA_DV)
    z = (gla_lr @ gla_w_gate_up + gla_b_gate).astype(jnp.float32)
    g = (jax.nn.log_sigmoid(z) / GLA_GATE_NORM).reshape(B, T, GLA_HEADS, GLA_DK)
    o_a = rms_norm(gla_chunked(q, k, v, g), gla_norm_w)
    y_a = o_a.reshape(B, T, GLA_WIDTH) * jax.nn.silu(gla_og)

    qkv = jax.nn.silu(causal_depthwise_conv(gdn_qkv, gdn_conv_w))
    q, k, v = jnp.split(qkv, 3, axis=-1)
    q = l2_norm(q.reshape(B, T, GDN_HEADS, GDN_DK)) * (GDN_DK ** -0.5)
    k = l2_norm(k.reshape(B, T, GDN_HEADS, GDN_DK))
    v = v.reshape(B, T, GDN_HEADS, GDN_DV)
    g = -jnp.exp(gdn_a_log.astype(jnp.float32)) * jax.nn.softplus((gdn_a + gdn_dt_bias).astype(jnp.float32))
    beta = jax.nn.sigmoid(gdn_b.astype(jnp.float32))
    o_b = rms_norm(gated_delta_chunked(q, k, v, g, beta), gdn_norm_w)
    y_b = o_b.reshape(B, T, GDN_WIDTH) * jax.nn.silu(gdn_og)

    y = jnp.concatenate([y_a, y_b], axis=-1) @ w_out
    return layer_norm(ALPHA * x + (1.0 + gate) * y, ln_w, ln_b)


def _fwd_setup_inputs(seed: int = 0) -> dict:
    key = jax.random.key(seed)
    ks = jax.random.split(key, 16)
    f32 = jnp.float32
    x = jax.random.normal(ks[0], (BATCH, SEQ, D_MODEL), f32)
    c = jax.random.normal(ks[1], (BATCH, D_MODEL), f32)
    w_ada = jax.random.normal(ks[2], (DEPTH, D_MODEL, 3 * D_MODEL), f32) * (0.1 * D_MODEL ** -0.5)
    b_ada = jax.random.normal(ks[3], (DEPTH, 3 * D_MODEL), f32) * 0.01
    col_scale = jnp.concatenate([
        jnp.ones((2 * GLA_QK,), f32),
        jnp.full((GLA_WIDTH,), BETA_INIT, f32),
        jnp.ones((GLA_GATE_RANK + GLA_WIDTH + 2 * GDN_WIDTH,), f32),
        jnp.full((GDN_WIDTH,), BETA_INIT, f32),
        jnp.ones((2 * GDN_HEADS + GDN_WIDTH,), f32),
    ])
    w_in = jax.random.normal(ks[4], (DEPTH, D_MODEL, IN_COLS), f32) * (D_MODEL ** -0.5) * col_scale
    gla_w_gate_up = jax.random.normal(ks[5], (DEPTH, GLA_GATE_RANK, GLA_QK), f32) * (GLA_GATE_RANK ** -0.5)
    gla_b_gate = jax.random.normal(ks[6], (DEPTH, GLA_QK), f32) * 0.1
    gla_norm_w = 1.0 + 0.01 * jax.random.normal(ks[7], (DEPTH, GLA_DV), f32)
    gdn_conv_w = jax.random.normal(ks[8], (DEPTH, CONV_WIDTH, 3 * GDN_WIDTH), f32) * (CONV_WIDTH ** -0.5)
    gdn_a_log = jnp.log(jax.random.uniform(ks[9], (DEPTH, GDN_HEADS), f32, 1.0, 16.0))
    dt = jnp.exp(jax.random.uniform(ks[10], (DEPTH, GDN_HEADS), f32, np.log(1e-3), np.log(1e-1)))
    gdn_dt_bias = dt + jnp.log(-jnp.expm1(-dt))
    gdn_norm_w = 1.0 + 0.01 * jax.random.normal(ks[11], (DEPTH, GDN_DV), f32)
    w_out = jax.random.normal(ks[12], (DEPTH, MIX_WIDTH, D_MODEL), f32) * (MIX_WIDTH ** -0.5) * BETA_INIT
    ln_w = 1.0 + 0.01 * jax.random.normal(ks[13], (DEPTH, D_MODEL), f32)
    ln_b = 0.01 * jax.random.normal(ks[14], (DEPTH, D_MODEL), f32)
    return {'x': x, 'c': c, 'w_ada': w_ada, 'b_ada': b_ada, 'w_in': w_in,
            'gla_w_gate_up': gla_w_gate_up, 'gla_b_gate': gla_b_gate, 'gla_norm_w': gla_norm_w,
            'gdn_conv_w': gdn_conv_w, 'gdn_a_log': gdn_a_log, 'gdn_dt_bias': gdn_dt_bias,
            'gdn_norm_w': gdn_norm_w, 'w_out': w_out, 'ln_w': ln_w, 'ln_b': ln_b}


def _fwd_reference(x, c, w_ada, b_ada, w_in, gla_w_gate_up, gla_b_gate, gla_norm_w,
              gdn_conv_w, gdn_a_log, gdn_dt_bias, gdn_norm_w, w_out, ln_w, ln_b):
    for layer in range(DEPTH):
        x = hybrid_layer(x, c, w_ada[layer], b_ada[layer], w_in[layer], gla_w_gate_up[layer],
                         gla_b_gate[layer], gla_norm_w[layer], gdn_conv_w[layer], gdn_a_log[layer],
                         gdn_dt_bias[layer], gdn_norm_w[layer], w_out[layer], ln_w[layer], ln_b[layer])
    return x


import jax as _jax
import jax.numpy as _jnp

TWIN_FORMAT = 'train_step'
FWD_PARAMS = ['x', 'c', 'w_ada', 'b_ada', 'w_in', 'gla_w_gate_up', 'gla_b_gate', 'gla_norm_w', 'gdn_conv_w', 'gdn_a_log', 'gdn_dt_bias', 'gdn_norm_w', 'w_out', 'ln_w', 'ln_b']
TWIN_WEIGHTS = ['w_ada', 'b_ada', 'w_in', 'gla_w_gate_up', 'gla_b_gate', 'gla_norm_w', 'gdn_conv_w', 'gdn_a_log', 'gdn_dt_bias', 'gdn_norm_w', 'w_out', 'ln_w', 'ln_b']
TWIN_DIFF_INPUT = 'x'
TWIN_INPUTS = ['x', 'c', 'w_ada', 'b_ada', 'w_in', 'gla_w_gate_up', 'gla_b_gate', 'gla_norm_w', 'gdn_conv_w', 'gdn_a_log', 'gdn_dt_bias', 'gdn_norm_w', 'w_out', 'ln_w', 'ln_b', 'loss_target', 'm_w_ada', 'm_b_ada', 'm_w_in', 'm_gla_w_gate_up', 'm_gla_b_gate', 'm_gla_norm_w', 'm_gdn_conv_w', 'm_gdn_a_log', 'm_gdn_dt_bias', 'm_gdn_norm_w', 'm_w_out', 'm_ln_w', 'm_ln_b', 'v_w_ada', 'v_b_ada', 'v_w_in', 'v_gla_w_gate_up', 'v_gla_b_gate', 'v_gla_norm_w', 'v_gdn_conv_w', 'v_gdn_a_log', 'v_gdn_dt_bias', 'v_gdn_norm_w', 'v_w_out', 'v_ln_w', 'v_ln_b']
TWIN_OUTPUTS = ['loss', 'grad_x', 'grad_w_ada', 'grad_b_ada', 'grad_w_in', 'grad_gla_w_gate_up', 'grad_gla_b_gate', 'grad_gla_norm_w', 'grad_gdn_conv_w', 'grad_gdn_a_log', 'grad_gdn_dt_bias', 'grad_gdn_norm_w', 'grad_w_out', 'grad_ln_w', 'grad_ln_b', 'delta_w_ada', 'delta_b_ada', 'delta_w_in', 'delta_gla_w_gate_up', 'delta_gla_b_gate', 'delta_gla_norm_w', 'delta_gdn_conv_w', 'delta_gdn_a_log', 'delta_gdn_dt_bias', 'delta_gdn_norm_w', 'delta_w_out', 'delta_ln_w', 'delta_ln_b', 'new_m_w_ada', 'new_m_b_ada', 'new_m_w_in', 'new_m_gla_w_gate_up', 'new_m_gla_b_gate', 'new_m_gla_norm_w', 'new_m_gdn_conv_w', 'new_m_gdn_a_log', 'new_m_gdn_dt_bias', 'new_m_gdn_norm_w', 'new_m_w_out', 'new_m_ln_w', 'new_m_ln_b', 'new_v_w_ada', 'new_v_b_ada', 'new_v_w_in', 'new_v_gla_w_gate_up', 'new_v_gla_b_gate', 'new_v_gla_norm_w', 'new_v_gdn_conv_w', 'new_v_gdn_a_log', 'new_v_gdn_dt_bias', 'new_v_gdn_norm_w', 'new_v_w_out', 'new_v_ln_w', 'new_v_ln_b']
TWIN_LEAF_KINDS = {'loss': 'loss', 'grad_x': 'grad_x', 'grad_w_ada': 'grad_w', 'grad_b_ada': 'grad_w', 'grad_w_in': 'grad_w', 'grad_gla_w_gate_up': 'grad_w', 'grad_gla_b_gate': 'grad_w', 'grad_gla_norm_w': 'grad_w', 'grad_gdn_conv_w': 'grad_w', 'grad_gdn_a_log': 'grad_w', 'grad_gdn_dt_bias': 'grad_w', 'grad_gdn_norm_w': 'grad_w', 'grad_w_out': 'grad_w', 'grad_ln_w': 'grad_w', 'grad_ln_b': 'grad_w', 'delta_w_ada': 'delta_w', 'delta_b_ada': 'delta_w', 'delta_w_in': 'delta_w', 'delta_gla_w_gate_up': 'delta_w', 'delta_gla_b_gate': 'delta_w', 'delta_gla_norm_w': 'delta_w', 'delta_gdn_conv_w': 'delta_w', 'delta_gdn_a_log': 'delta_w', 'delta_gdn_dt_bias': 'delta_w', 'delta_gdn_norm_w': 'delta_w', 'delta_w_out': 'delta_w', 'delta_ln_w': 'delta_w', 'delta_ln_b': 'delta_w', 'new_m_w_ada': 'new_m', 'new_m_b_ada': 'new_m', 'new_m_w_in': 'new_m', 'new_m_gla_w_gate_up': 'new_m', 'new_m_gla_b_gate': 'new_m', 'new_m_gla_norm_w': 'new_m', 'new_m_gdn_conv_w': 'new_m', 'new_m_gdn_a_log': 'new_m', 'new_m_gdn_dt_bias': 'new_m', 'new_m_gdn_norm_w': 'new_m', 'new_m_w_out': 'new_m', 'new_m_ln_w': 'new_m', 'new_m_ln_b': 'new_m', 'new_v_w_ada': 'new_v', 'new_v_b_ada': 'new_v', 'new_v_w_in': 'new_v', 'new_v_gla_w_gate_up': 'new_v', 'new_v_gla_b_gate': 'new_v', 'new_v_gla_norm_w': 'new_v', 'new_v_gdn_conv_w': 'new_v', 'new_v_gdn_a_log': 'new_v', 'new_v_gdn_dt_bias': 'new_v', 'new_v_gdn_norm_w': 'new_v', 'new_v_w_out': 'new_v', 'new_v_ln_w': 'new_v', 'new_v_ln_b': 'new_v'}


def _forward(args):
    return _fwd_reference(*[args[k] for k in FWD_PARAMS])


def _output_shape():
    out = _jax.eval_shape(lambda: _forward(_fwd_setup_inputs(0)))
    return out.shape, out.dtype

N_MICROBATCH = 1
ADAM_LR = 0.001
ADAM_B1 = 0.9
ADAM_B2 = 0.999
ADAM_EPS = 1e-08
ADAM_WD = 0.01
ADAM_STEP = 10
PER_EXAMPLE_BATCH_AXIS = {'x': 0, 'c': 0, 'loss_target': 0}
SHARED_INPUTS = []
_WEIGHT_DTYPES = {'w_ada': _jnp.float32, 'b_ada': _jnp.float32, 'w_in': _jnp.float32, 'gla_w_gate_up': _jnp.float32, 'gla_b_gate': _jnp.float32, 'gla_norm_w': _jnp.float32, 'gdn_conv_w': _jnp.float32, 'gdn_a_log': _jnp.float32, 'gdn_dt_bias': _jnp.float32, 'gdn_norm_w': _jnp.float32, 'w_out': _jnp.float32, 'ln_w': _jnp.float32, 'ln_b': _jnp.float32}
MOMENT_SCALE = {'w_ada': 1.298588e-01, 'b_ada': 1.337972e-01, 'w_in': 8.814702e-02, 'gla_w_gate_up': 1.263009e-02, 'gla_b_gate': 4.931549e-02, 'gla_norm_w': 1.523498e-01, 'gdn_conv_w': 5.832889e-02, 'gdn_a_log': 2.762889e-01, 'gdn_dt_bias': 2.779198e-01, 'gdn_norm_w': 1.959214e-01, 'w_out': 1.218601e-01, 'ln_w': 6.388083e+01, 'ln_b': 1.740523e+00}


def _to_microbatches(a, axis):
    t = _jnp.moveaxis(a, axis, 0)
    t = t.reshape((N_MICROBATCH, t.shape[0] // N_MICROBATCH) + t.shape[1:])
    return _jnp.moveaxis(t, 1, axis + 1)


def setup_inputs(seed: int = 0) -> dict:
    inp = _fwd_setup_inputs(seed)
    key = _jax.random.fold_in(_jax.random.key(seed), 7919)
    shape, _ = _output_shape()
    out = dict(inp)
    out["loss_target"] = _jax.random.normal(_jax.random.fold_in(key, 0), shape, _jnp.float32)
    for i, name in enumerate(TWIN_WEIGHTS):
        w = inp[name].astype(_jnp.float32)
        if MOMENT_SCALE is None:
            s = _jnp.sqrt(_jnp.mean(_jnp.square(w)) + 1e-30)
        else:
            s = MOMENT_SCALE[name]
        km, kv = _jax.random.split(_jax.random.fold_in(key, i + 1))
        out[name] = w
        out["m_" + name] = s * _jax.random.normal(km, w.shape, _jnp.float32)
        out["v_" + name] = (s * s) * _jax.random.uniform(kv, w.shape, _jnp.float32, 0.5, 1.5)
    if N_MICROBATCH > 1:
        for name, axis in PER_EXAMPLE_BATCH_AXIS.items():
            out[name] = _to_microbatches(out[name], axis)
    return {'x': out['x'], 'c': out['c'], 'w_ada': out['w_ada'], 'b_ada': out['b_ada'], 'w_in': out['w_in'], 'gla_w_gate_up': out['gla_w_gate_up'], 'gla_b_gate': out['gla_b_gate'], 'gla_norm_w': out['gla_norm_w'], 'gdn_conv_w': out['gdn_conv_w'], 'gdn_a_log': out['gdn_a_log'], 'gdn_dt_bias': out['gdn_dt_bias'], 'gdn_norm_w': out['gdn_norm_w'], 'w_out': out['w_out'], 'ln_w': out['ln_w'], 'ln_b': out['ln_b'], 'loss_target': out['loss_target'], 'm_w_ada': out['m_w_ada'], 'm_b_ada': out['m_b_ada'], 'm_w_in': out['m_w_in'], 'm_gla_w_gate_up': out['m_gla_w_gate_up'], 'm_gla_b_gate': out['m_gla_b_gate'], 'm_gla_norm_w': out['m_gla_norm_w'], 'm_gdn_conv_w': out['m_gdn_conv_w'], 'm_gdn_a_log': out['m_gdn_a_log'], 'm_gdn_dt_bias': out['m_gdn_dt_bias'], 'm_gdn_norm_w': out['m_gdn_norm_w'], 'm_w_out': out['m_w_out'], 'm_ln_w': out['m_ln_w'], 'm_ln_b': out['m_ln_b'], 'v_w_ada': out['v_w_ada'], 'v_b_ada': out['v_b_ada'], 'v_w_in': out['v_w_in'], 'v_gla_w_gate_up': out['v_gla_w_gate_up'], 'v_gla_b_gate': out['v_gla_b_gate'], 'v_gla_norm_w': out['v_gla_norm_w'], 'v_gdn_conv_w': out['v_gdn_conv_w'], 'v_gdn_a_log': out['v_gdn_a_log'], 'v_gdn_dt_bias': out['v_gdn_dt_bias'], 'v_gdn_norm_w': out['v_gdn_norm_w'], 'v_w_out': out['v_w_out'], 'v_ln_w': out['v_ln_w'], 'v_ln_b': out['v_ln_b']}


def _loss(weights, diff, rest, loss_target):
    with _jax.named_scope("forward"):
        args = {**rest, TWIN_DIFF_INPUT: diff, **{k: w.astype(_WEIGHT_DTYPES[k]) for k, w in weights.items()}}
        y = _forward(args)
    with _jax.named_scope("loss_head"):
        err = _jnp.square(y.astype(_jnp.float32) - loss_target)
        return 0.5 * _jnp.sum(_jnp.mean(err, axis=-1)) if err.ndim else 0.5 * err


def _adamw(w, g, m, v):
    m = ADAM_B1 * m + (1.0 - ADAM_B1) * g
    v = ADAM_B2 * v + (1.0 - ADAM_B2) * _jnp.square(g)
    m_hat = m / (1.0 - ADAM_B1 ** ADAM_STEP)
    v_hat = v / (1.0 - ADAM_B2 ** ADAM_STEP)
    delta = -ADAM_LR * (m_hat / (_jnp.sqrt(v_hat) + ADAM_EPS) + ADAM_WD * w)
    return delta, m, v


def reference(x, c, w_ada, b_ada, w_in, gla_w_gate_up, gla_b_gate, gla_norm_w, gdn_conv_w, gdn_a_log, gdn_dt_bias, gdn_norm_w, w_out, ln_w, ln_b, loss_target, m_w_ada, m_b_ada, m_w_in, m_gla_w_gate_up, m_gla_b_gate, m_gla_norm_w, m_gdn_conv_w, m_gdn_a_log, m_gdn_dt_bias, m_gdn_norm_w, m_w_out, m_ln_w, m_ln_b, v_w_ada, v_b_ada, v_w_in, v_gla_w_gate_up, v_gla_b_gate, v_gla_norm_w, v_gdn_conv_w, v_gdn_a_log, v_gdn_dt_bias, v_gdn_norm_w, v_w_out, v_ln_w, v_ln_b):
    given = dict(x=x, c=c, w_ada=w_ada, b_ada=b_ada, w_in=w_in, gla_w_gate_up=gla_w_gate_up, gla_b_gate=gla_b_gate, gla_norm_w=gla_norm_w, gdn_conv_w=gdn_conv_w, gdn_a_log=gdn_a_log, gdn_dt_bias=gdn_dt_bias, gdn_norm_w=gdn_norm_w, w_out=w_out, ln_w=ln_w, ln_b=ln_b, loss_target=loss_target, m_w_ada=m_w_ada, m_b_ada=m_b_ada, m_w_in=m_w_in, m_gla_w_gate_up=m_gla_w_gate_up, m_gla_b_gate=m_gla_b_gate, m_gla_norm_w=m_gla_norm_w, m_gdn_conv_w=m_gdn_conv_w, m_gdn_a_log=m_gdn_a_log, m_gdn_dt_bias=m_gdn_dt_bias, m_gdn_norm_w=m_gdn_norm_w, m_w_out=m_w_out, m_ln_w=m_ln_w, m_ln_b=m_ln_b, v_w_ada=v_w_ada, v_b_ada=v_b_ada, v_w_in=v_w_in, v_gla_w_gate_up=v_gla_w_gate_up, v_gla_b_gate=v_gla_b_gate, v_gla_norm_w=v_gla_norm_w, v_gdn_conv_w=v_gdn_conv_w, v_gdn_a_log=v_gdn_a_log, v_gdn_dt_bias=v_gdn_dt_bias, v_gdn_norm_w=v_gdn_norm_w, v_w_out=v_w_out, v_ln_w=v_ln_w, v_ln_b=v_ln_b)
    weights = {n: given[n] for n in TWIN_WEIGHTS}
    shared = {n: given[n] for n in SHARED_INPUTS}
    per_example = {n: given[n] for n in ['x', 'c']}
    grad_fn = _jax.value_and_grad(_loss, argnums=(0, 1))

    def one_microbatch(ex, loss_target):
        ex = dict(ex)
        diff = ex.pop(TWIN_DIFF_INPUT)
        return grad_fn(weights, diff, {**shared, **ex}, loss_target)

    if N_MICROBATCH == 1:
        loss, (grad_w, grad_x) = one_microbatch(per_example, given["loss_target"])
    else:
        def body(carry, xs):
            loss_sum, grad_sum = carry
            l_k, (gw_k, gx_k) = one_microbatch(xs[0], xs[1])
            with _jax.named_scope("update"):
                return (loss_sum + l_k, _jax.tree.map(_jnp.add, grad_sum, gw_k)), gx_k

        init = (_jnp.zeros((), _jnp.float32), _jax.tree.map(_jnp.zeros_like, weights))
        (loss, grad_w), grad_x = _jax.lax.scan(body, init, (per_example, given["loss_target"]))
    with _jax.named_scope("update"):
        delta_w, new_m, new_v = {}, {}, {}
        for n in TWIN_WEIGHTS:
            delta_w[n], new_m[n], new_v[n] = _adamw(weights[n], grad_w[n], given["m_" + n], given["v_" + n])
    return (loss, grad_x, *[grad_w[n] for n in TWIN_WEIGHTS], *[delta_w[n] for n in TWIN_WEIGHTS],
            *[new_m[n] for n in TWIN_WEIGHTS], *[new_v[n] for n in TWIN_WEIGHTS])
```

```python
import functools

import jax
import jax.numpy as jnp
from jax import lax
from jax.experimental import pallas as pl
from jax.experimental.pallas import tpu as pltpu

F32 = jnp.float32
MXU = jnp.bfloat16
HI = lax.Precision.HIGHEST

D = 1024
NDEV = 8
H = 4
GLA_DK = 64
DV = 128
CHUNK = 64
LN_EPS = 1e-5
RMS_EPS = 1e-6
ALPHA = 2.0 ** 0.25
GATE_NORM = 16.0

W_GLA, W_GQKV, W_GDN = 1664, 1536, 640
PW = W_GLA + W_GQKV + W_GDN
IN_COLS = 3608
SHARD_IN = IN_COLS // NDEV
SHARD_ADA = 3 * D // NDEV
SPW = 3 * D
SMALL_W = 2816

ADAM_LR, ADAM_B1, ADAM_B2, ADAM_EPS, ADAM_WD, ADAM_STEP = 0.001, 0.9, 0.999, 1e-08, 0.01, 10

VMEM_LIMIT = 56 * 1024 * 1024


def _params(sem=None, **kw):
    if sem is not None:
        kw["dimension_semantics"] = sem
    return pltpu.CompilerParams(vmem_limit_bytes=VMEM_LIMIT, **kw)


def _mm(a, b):
    return jnp.dot(a.astype(MXU), b.astype(MXU), preferred_element_type=F32)


def _nt(a, b):
    return lax.dot_general(a.astype(MXU), b.astype(MXU), (((1,), (1,)), ((), ())), preferred_element_type=F32)


def _tn(a, b):
    return lax.dot_general(a.astype(MXU), b.astype(MXU), (((0,), (0,)), ((), ())), preferred_element_type=F32)


@jax.custom_vjp
def bmm(a, b):
    return _mm(a, b)


bmm.defvjp(lambda a, b: (_mm(a, b), (a, b)), lambda r, g: (_nt(g, r[1]), _tn(r[0], g)))


@jax.custom_vjp
def bnt(a, b):
    return _nt(a, b)


bnt.defvjp(lambda a, b: (_nt(a, b), (a, b)), lambda r, g: (_mm(g, r[1]), _tn(g, r[0])))


@jax.custom_vjp
def btn(a, b):
    return _tn(a, b)


btn.defvjp(lambda a, b: (_tn(a, b), (a, b)), lambda r, g: (_nt(r[1], g), _mm(r[0], g)))


def _hmm(a, b):
    return jnp.dot(a, b, precision=HI, preferred_element_type=F32)


def _hnt(a, b):
    return lax.dot_general(a, b, (((1,), (1,)), ((), ())), precision=HI, preferred_element_type=F32)


def _htn(a, b):
    return lax.dot_general(a, b, (((0,), (0,)), ((), ())), precision=HI, preferred_element_type=F32)


@jax.custom_vjp
def unit_lower_solve(a, r1, r2):
    return _solve_fwd(a, r1, r2)[0]


def _solve_fwd(a, r1, r2):
    n = a.shape[0]
    eye = (lax.broadcasted_iota(jnp.int32, (n, n), 0) == lax.broadcasted_iota(jnp.int32, (n, n), 1)).astype(F32)
    p = -a
    t = eye + p
    for _ in range(5):
        p = _hmm(p, p)
        t = t + _hmm(t, p)
    s1, s2 = _hmm(t, r1), _hmm(t, r2)
    return (s1, s2), (t, s1, s2)


def _solve_bwd(res, g):
    t, s1, s2 = res
    d1, d2 = _htn(t, g[0]), _htn(t, g[1])
    return -(_hnt(d1, s1) + _hnt(d2, s2)), d1, d2


unit_lower_solve.defvjp(_solve_fwd, _solve_bwd)


def _iotas(n):
    return lax.broadcasted_iota(jnp.int32, (n, n), 0), lax.broadcasted_iota(jnp.int32, (n, n), 1)


def _col_to_row(col, eye):
    return jnp.sum(jnp.where(eye, col, 0.0), axis=0, keepdims=True)


def _row_to_col(row, eye):
    return jnp.sum(jnp.where(eye, row, 0.0), axis=1, keepdims=True)


def _pick_row(m, i):
    r = lax.broadcasted_iota(jnp.int32, m.shape, 0)
    return jnp.sum(jnp.where(r == i, m, 0.0), axis=0, keepdims=True)


def _rms_gate(o, nw, og):
    on = o * lax.rsqrt(jnp.mean(o * o, axis=-1, keepdims=True) + RMS_EPS) * nw
    return on * jax.nn.silu(og)


def gla_head(q, k, v, lr, og, s, wgu, bg, nw):
    c = q.shape[0]
    r, cc = _iotas(c)
    qs = q * (GLA_DK ** -0.5)
    z = bmm(lr, wgu) + bg
    g = jax.nn.log_sigmoid(z) / GATE_NORM
    b = _hmm((r >= cc).astype(F32), g)
    bref = _pick_row(b, c // 2 - 1)
    blast = _pick_row(b, c - 1)
    att = jnp.where(r >= cc, bnt(qs * jnp.exp(b - bref), k * jnp.exp(bref - b)), 0.0)
    o = bmm(att, v) + bmm(qs * jnp.exp(b), s)
    rk, ck = _iotas(GLA_DK)
    s_new = _row_to_col(jnp.exp(blast), rk == ck) * s + btn(k * jnp.exp(blast - b), v)
    return _rms_gate(o, nw, og), s_new


def gdn_head(cq, ck, cv, a, bb, og, s, alog, dtb, nw):
    c = cq.shape[0]
    r, cc = _iotas(c)
    eye, causal, strict = r == cc, r >= cc, r > cc
    q, k, v = jax.nn.silu(cq), jax.nn.silu(ck), jax.nn.silu(cv)
    q = q * lax.rsqrt(jnp.sum(q * q, axis=-1, keepdims=True) + RMS_EPS) * (DV ** -0.5)
    k = k * lax.rsqrt(jnp.sum(k * k, axis=-1, keepdims=True) + RMS_EPS)
    g = -jnp.exp(alog) * jax.nn.softplus(a + dtb)
    beta = jax.nn.sigmoid(bb)
    d = jnp.sum(jnp.where(causal, _col_to_row(g, eye), 0.0), axis=1, keepdims=True)
    el = jnp.exp(jnp.where(causal, d - _col_to_row(d, eye), -jnp.inf))
    kb = k * beta
    amat = jnp.where(strict, bnt(kb, k) * el, 0.0)
    u, w = unit_lower_solve(amat, v * beta, kb * jnp.exp(d))
    qk = jnp.where(causal, bnt(q, k) * el, 0.0)
    dlast = _pick_row(d, c - 1)
    v_new = u - bmm(w, s)
    o = bmm(q * jnp.exp(d), s) + bmm(qk, v_new)
    s_new = jnp.exp(dlast) * s + btn(k * jnp.exp(dlast - d), v_new)
    return _rms_gate(o, nw, og), s_new


def head_fn(x, y, gate, lnw, lnb, tgt):
    u = ALPHA * x + (1.0 + gate) * y
    mu = jnp.mean(u, axis=-1, keepdims=True)
    var = jnp.mean(jnp.square(u - mu), axis=-1, keepdims=True)
    out = (u - mu) * lax.rsqrt(var + LN_EPS) * lnw + lnb
    err = jnp.square(out - tgt)
    return 0.5 * jnp.sum(jnp.mean(err, axis=-1, keepdims=True), axis=0, keepdims=True)


def _proj_call(x, mod3, wp):
    bsz, t, _ = x.shape
    tm = min(256, t)

    def body(x_ref, mod_ref, w_ref, pg_ref, pq_ref, pd_ref):
        h = x_ref[0] * (1.0 + mod_ref[0, 1:2, :]) + mod_ref[0, 0:1, :]
        p = jnp.dot(h.astype(MXU), w_ref[...], preferred_element_type=F32)
        pg_ref[0] = p[:, :W_GLA]
        pq_ref[0] = p[:, W_GLA:W_GLA + W_GQKV]
        pd_ref[0] = p[:, W_GLA + W_GQKV:]

    tok = lambda w: pl.BlockSpec((1, tm, w), lambda b, i: (b, i, 0))
    return pl.pallas_call(
        body, name="proj", grid=(bsz, t // tm),
        in_specs=[tok(D), pl.BlockSpec((1, 3, D), lambda b, i: (b, 0, 0)), pl.BlockSpec((D, PW), lambda b, i: (0, 0))],
        out_specs=[tok(W_GLA), tok(W_GQKV), tok(W_GDN)],
        out_shape=[jax.ShapeDtypeStruct((bsz, t, w), F32) for w in (W_GLA, W_GQKV, W_GDN)],
        compiler_params=_params(("parallel", "parallel")),
    )(x, mod3, wp)


def _conv_fwd_call(pq, conv_w):
    bsz, t, _ = pq.shape
    tt = min(512, t)
    hb = tt // 8

    def body(x_ref, halo_ref, w_ref, o_ref, buf):
        i = pl.program_id(1)
        buf[0:8, :] = jnp.where(i > 0, halo_ref[0], 0.0)
        buf[8:, :] = x_ref[0]
        acc = w_ref[0:1, :] * buf[pl.ds(5, tt), :]
        for k in range(1, 4):
            acc = acc + w_ref[k:k + 1, :] * buf[pl.ds(5 + k, tt), :]
        o_ref[0] = acc

    return pl.pallas_call(
        body, name="conv_fwd", grid=(bsz, t // tt),
        in_specs=[pl.BlockSpec((1, tt, W_GQKV), lambda b, i: (b, i, 0)),
                  pl.BlockSpec((1, 8, W_GQKV), lambda b, i: (b, jnp.maximum(i * hb - 1, 0), 0)),
                  pl.BlockSpec((4, W_GQKV), lambda b, i: (0, 0))],
        out_specs=pl.BlockSpec((1, tt, W_GQKV), lambda b, i: (b, i, 0)),
        out_shape=jax.ShapeDtypeStruct(pq.shape, F32),
        scratch_shapes=[pltpu.VMEM((tt + 8, W_GQKV), F32)],
        compiler_params=_params(("parallel", "parallel")),
    )(pq, pq, conv_w)


def _conv_bwd_call(dconv, pq, conv_w):
    bsz, t, _ = pq.shape
    tt = min(512, t)
    hb = tt // 8
    nt_ = t // tt

    def body(d_ref, dnext_ref, x_ref, halo_ref, w_ref, din_ref, dw_ref, dbuf, xbuf):
        b, i = pl.program_id(0), pl.program_id(1)

        @pl.when((b == 0) & (i == 0))
        def _():
            dw_ref[...] = jnp.zeros_like(dw_ref)

        dbuf[0:tt, :] = d_ref[0]
        dbuf[tt:, :] = jnp.where(i < nt_ - 1, dnext_ref[0], 0.0)
        xbuf[0:8, :] = jnp.where(i > 0, halo_ref[0], 0.0)
        xbuf[8:, :] = x_ref[0]
        dout = d_ref[0]
        acc = w_ref[0:1, :] * dbuf[pl.ds(3, tt), :]
        dw_ref[0:1, :] += jnp.sum(dout * xbuf[pl.ds(5, tt), :], axis=0, keepdims=True)
        for k in range(1, 4):
            acc = acc + w_ref[k:k + 1, :] * dbuf[pl.ds(3 - k, tt), :]
            dw_ref[k:k + 1, :] += jnp.sum(dout * xbuf[pl.ds(5 + k, tt), :], axis=0, keepdims=True)
        din_ref[0] = acc

    tile = pl.BlockSpec((1, tt, W_GQKV), lambda b, i: (b, i, 0))
    return pl.pallas_call(
        body, name="conv_bwd", grid=(bsz, nt_),
        in_specs=[tile, pl.BlockSpec((1, 8, W_GQKV), lambda b, i: (b, jnp.minimum((i + 1) * hb, t // 8 - 1), 0)),
                  tile, pl.BlockSpec((1, 8, W_GQKV), lambda b, i: (b, jnp.maximum(i * hb - 1, 0), 0)),
                  pl.BlockSpec((4, W_GQKV), lambda b, i: (0, 0))],
        out_specs=[tile, pl.BlockSpec((8, W_GQKV), lambda b, i: (0, 0))],
        out_shape=[jax.ShapeDtypeStruct(pq.shape, F32), jax.ShapeDtypeStruct((8, W_GQKV), F32)],
        scratch_shapes=[pltpu.VMEM((tt + 8, W_GQKV), F32), pltpu.VMEM((tt + 8, W_GQKV), F32)],
        compiler_params=_params(("arbitrary", "arbitrary")),
    )(dconv, dconv, pq, pq, conv_w)


def _gla_specs(nc, rev):
    n_of = (lambda n: nc - 1 - n) if rev else (lambda n: n)
    blk = lambda w, j: pl.BlockSpec((1, CHUNK, w), lambda b, n: (b, n_of(n), j))
    return n_of, [blk(256, 0), blk(256, 1), blk(512, 1), blk(512, 2), blk(128, 12)]


def _full(shape):
    return pl.BlockSpec(shape, lambda b, n: (0,) * len(shape))


def _gla_fwd_call(pg, wgu, bg, nw):
    bsz, t, _ = pg.shape
    nc = t // CHUNK
    _, specs = _gla_specs(nc, False)

    def body(q_ref, k_ref, v_ref, og_ref, lr_ref, wgu_ref, bg_ref, nw_ref, y_ref, sh_ref, s_ref):
        @pl.when(pl.program_id(1) == 0)
        def _():
            s_ref[...] = jnp.zeros_like(s_ref)

        lr = lr_ref[0]
        for h in range(H):
            qs, vs = slice(64 * h, 64 * h + 64), slice(128 * h, 128 * h + 128)
            s = s_ref[h]
            sh_ref[0, 0, h] = s
            y, s_new = gla_head(q_ref[0, :, qs], k_ref[0, :, qs], v_ref[0, :, vs], lr, og_ref[0, :, vs], s,
                                wgu_ref[:, qs], bg_ref[:, qs], nw_ref[...])
            y_ref[0, :, vs] = y
            s_ref[h] = s_new

    return pl.pallas_call(
        body, name="gla_fwd", grid=(bsz, nc),
        in_specs=specs + [_full((128, 256)), _full((1, 256)), _full((1, 128))],
        out_specs=[pl.BlockSpec((1, CHUNK, 512), lambda b, n: (b, n, 0)),
                   pl.BlockSpec((1, 1, H, GLA_DK, DV), lambda b, n: (b, n, 0, 0, 0))],
        out_shape=[jax.ShapeDtypeStruct((bsz, t, 512), F32), jax.ShapeDtypeStruct((bsz, nc, H, GLA_DK, DV), F32)],
        scratch_shapes=[pltpu.VMEM((H, GLA_DK, DV), F32)],
        compiler_params=_params(("parallel", "arbitrary")),
    )(pg, pg, pg, pg, pg, wgu, bg, nw)


def _gla_bwd_call(pg, s_hist, dyin, wgu, bg, nw):
    bsz, t, _ = pg.shape
    nc = t // CHUNK
    n_of, specs = _gla_specs(nc, True)

    def body(q_ref, k_ref, v_ref, og_ref, lr_ref, sh_ref, dy_ref, wgu_ref, bg_ref, nw_ref,
             dp_ref, dwgu_ref, dbg_ref, dnw_ref, ds_ref):
        b, n = pl.program_id(0), pl.program_id(1)

        @pl.when((b == 0) & (n == 0))
        def _():
            dwgu_ref[...] = jnp.zeros_like(dwgu_ref)
            dbg_ref[...] = jnp.zeros_like(dbg_ref)
            dnw_ref[...] = jnp.zeros_like(dnw_ref)

        @pl.when(n == 0)
        def _():
            ds_ref[...] = jnp.zeros_like(ds_ref)

        lr = lr_ref[0]
        dlr = jnp.zeros_like(lr)
        for h in range(H):
            qs, vs = slice(64 * h, 64 * h + 64), slice(128 * h, 128 * h + 128)
            _, vjp = jax.vjp(gla_head, q_ref[0, :, qs], k_ref[0, :, qs], v_ref[0, :, vs], lr, og_ref[0, :, vs],
                             sh_ref[0, 0, h], wgu_ref[:, qs], bg_ref[:, qs], nw_ref[...])
            dq, dk, dv, dlr_h, dog, ds, dwgu, dbg, dnw = vjp((dy_ref[0, :, vs], ds_ref[h]))
            dp_ref[0, :, 64 * h:64 * h + 64] = dq
            dp_ref[0, :, 256 + 64 * h:256 + 64 * h + 64] = dk
            dp_ref[0, :, 512 + 128 * h:512 + 128 * h + 128] = dv
            dp_ref[0, :, 1024 + 128 * h:1024 + 128 * h + 128] = dog
            dlr = dlr + dlr_h
            ds_ref[h] = ds
            dwgu_ref[:, qs] += dwgu
            dbg_ref[:, qs] += dbg
            dnw_ref[...] += dnw
        dp_ref[0, :, 1536:1664] = dlr

    return pl.pallas_call(
        body, name="gla_bwd", grid=(bsz, nc),
        in_specs=specs + [pl.BlockSpec((1, 1, H, GLA_DK, DV), lambda b, n: (b, n_of(n), 0, 0, 0)),
                          pl.BlockSpec((1, CHUNK, 512), lambda b, n: (b, n_of(n), 0)),
                          _full((128, 256)), _full((1, 256)), _full((1, 128))],
        out_specs=[pl.BlockSpec((1, CHUNK, W_GLA), lambda b, n: (b, n_of(n), 0)),
                   _full((128, 256)), _full((1, 256)), _full((1, 128))],
        out_shape=[jax.ShapeDtypeStruct(pg.shape, F32), jax.ShapeDtypeStruct((128, 256), F32),
                   jax.ShapeDtypeStruct((1, 256), F32), jax.ShapeDtypeStruct((1, 128), F32)],
        scratch_shapes=[pltpu.VMEM((H, GLA_DK, DV), F32)],
        compiler_params=_params(("arbitrary", "arbitrary")),
    )(pg, pg, pg, pg, pg, s_hist, dyin, wgu, bg, nw)


def _gdn_specs(nc, rev):
    n_of = (lambda n: nc - 1 - n) if rev else (lambda n: n)
    blk = lambda w, j: pl.BlockSpec((1, CHUNK, w), lambda b, n: (b, n_of(n), j))
    return n_of, [blk(512, 0), blk(512, 1), blk(512, 2), blk(512, 0), blk(128, 4)]


def _gdn_fwd_call(conv, pd, sc, nw):
    bsz, t, _ = conv.shape
    nc = t // CHUNK
    _, specs = _gdn_specs(nc, False)

    def body(q_ref, k_ref, v_ref, og_ref, ab_ref, sc_ref, nw_ref, y_ref, sh_ref, s_ref):
        @pl.when(pl.program_id(1) == 0)
        def _():
            s_ref[...] = jnp.zeros_like(s_ref)

        for h in range(H):
            vs = slice(128 * h, 128 * h + 128)
            s = s_ref[h]
            sh_ref[0, 0, h] = s
            y, s_new = gdn_head(q_ref[0, :, vs], k_ref[0, :, vs], v_ref[0, :, vs], ab_ref[0, :, h:h + 1],
                                ab_ref[0, :, 4 + h:5 + h], og_ref[0, :, vs], s, sc_ref[0:1, h:h + 1],
                                sc_ref[1:2, h:h + 1], nw_ref[...])
            y_ref[0, :, vs] = y
            s_ref[h] = s_new

    return pl.pallas_call(
        body, name="gdn_fwd", grid=(bsz, nc),
        in_specs=specs + [_full((2, 128)), _full((1, 128))],
        out_specs=[pl.BlockSpec((1, CHUNK, 512), lambda b, n: (b, n, 0)),
                   pl.BlockSpec((1, 1, H, DV, DV), lambda b, n: (b, n, 0, 0, 0))],
        out_shape=[jax.ShapeDtypeStruct((bsz, t, 512), F32), jax.ShapeDtypeStruct((bsz, nc, H, DV, DV), F32)],
        scratch_shapes=[pltpu.VMEM((H, DV, DV), F32)],
        compiler_params=_params(("parallel", "arbitrary")),
    )(conv, conv, conv, pd, pd, sc, nw)


def _gdn_bwd_call(conv, pd, s_hist, dyin, sc, nw):
    bsz, t, _ = conv.shape
    nc = t // CHUNK
    n_of, specs = _gdn_specs(nc, True)

    def body(q_ref, k_ref, v_ref, og_ref, ab_ref, sh_ref, dy_ref, sc_ref, nw_ref,
             dc_ref, dpd_ref, dsc_ref, dnw_ref, ds_ref):
        b, n = pl.program_id(0), pl.program_id(1)

        @pl.when((b == 0) & (n == 0))
        def _():
            dsc_ref[...] = jnp.zeros_like(dsc_ref)
            dnw_ref[...] = jnp.zeros_like(dnw_ref)

        @pl.when(n == 0)
        def _():
            ds_ref[...] = jnp.zeros_like(ds_ref)

        dpd_ref[0, :, 512:640] = jnp.zeros((CHUNK, 128), F32)
        for h in range(H):
            vs = slice(128 * h, 128 * h + 128)
            _, vjp = jax.vjp(gdn_head, q_ref[0, :, vs], k_ref[0, :, vs], v_ref[0, :, vs], ab_ref[0, :, h:h + 1],
                             ab_ref[0, :, 4 + h:5 + h], og_ref[0, :, vs], sh_ref[0, 0, h], sc_ref[0:1, h:h + 1],
                             sc_ref[1:2, h:h + 1], nw_ref[...])
            dq, dk, dv, da, db, dog, ds, dalog, ddtb, dnw = vjp((dy_ref[0, :, vs], ds_ref[h]))
            dc_ref[0, :, vs] = dq
            dc_ref[0, :, 512 + 128 * h:512 + 128 * h + 128] = dk
            dc_ref[0, :, 1024 + 128 * h:1024 + 128 * h + 128] = dv
            dpd_ref[0, :, vs] = dog
            dpd_ref[0, :, 512 + h:513 + h] = da
            dpd_ref[0, :, 516 + h:517 + h] = db
            ds_ref[h] = ds
            dsc_ref[0:1, h:h + 1] += dalog
            dsc_ref[1:2, h:h + 1] += ddtb
            dnw_ref[...] += dnw

    return pl.pallas_call(
        body, name="gdn_bwd", grid=(bsz, nc),
        in_specs=specs + [pl.BlockSpec((1, 1, H, DV, DV), lambda b, n: (b, n_of(n), 0, 0, 0)),
                          pl.BlockSpec((1, CHUNK, 512), lambda b, n: (b, n_of(n), 1)),
                          _full((2, 128)), _full((1, 128))],
        out_specs=[pl.BlockSpec((1, CHUNK, W_GQKV), lambda b, n: (b, n_of(n), 0)),
                   pl.BlockSpec((1, CHUNK, W_GDN), lambda b, n: (b, n_of(n), 0)),
                   _full((2, 128)), _full((1, 128))],
        out_shape=[jax.ShapeDtypeStruct(conv.shape, F32), jax.ShapeDtypeStruct(pd.shape, F32),
                   jax.ShapeDtypeStruct((2, 128), F32), jax.ShapeDtypeStruct((1, 128), F32)],
        scratch_shapes=[pltpu.VMEM((H, DV, DV), F32)],
        compiler_params=_params(("arbitrary", "arbitrary")),
    )(conv, conv, conv, pd, pd, s_hist, dyin, sc, nw)


def _head_call(x, ya, yb, wout, mod3, lnw, lnb, tgt):
    bsz, t, _ = x.shape
    tm = min(256, t)

    def body(x_ref, ya_ref, yb_ref, w_ref, mod_ref, lnw_ref, lnb_ref, t_ref,
             dyin_ref, dxa_ref, dgate_ref, dw_ref, dlnw_ref, dlnb_ref, loss_ref):
        b, i = pl.program_id(0), pl.program_id(1)

        @pl.when((b == 0) & (i == 0))
        def _():
            dw_ref[...] = jnp.zeros_like(dw_ref)
            dlnw_ref[...] = jnp.zeros_like(dlnw_ref)
            dlnb_ref[...] = jnp.zeros_like(dlnb_ref)
            loss_ref[...] = jnp.zeros_like(loss_ref)

        @pl.when(i == 0)
        def _():
            dgate_ref[...] = jnp.zeros_like(dgate_ref)

        yin = jnp.concatenate([ya_ref[0], yb_ref[0]], axis=-1).astype(MXU)
        w = w_ref[...]
        y = jnp.dot(yin, w, preferred_element_type=F32)
        loss, vjp = jax.vjp(head_fn, x_ref[0], y, mod_ref[0, 2:3, :], lnw_ref[...], lnb_ref[...], t_ref[0])
        dx, dy, dgate, dlnw, dlnb, _ = vjp(jnp.ones((1, 1), F32))
        dyb = dy.astype(MXU)
        dyin_ref[0] = lax.dot_general(dyb, w, (((1,), (1,)), ((), ())), preferred_element_type=F32)
        dw_ref[...] += lax.dot_general(yin, dyb, (((0,), (0,)), ((), ())), preferred_element_type=F32)
        dxa_ref[0] = dx
        dgate_ref[0] += dgate
        dlnw_ref[...] += dlnw
        dlnb_ref[...] += dlnb
        loss_ref[...] += jnp.broadcast_to(loss, (1, 128))

    tok = lambda w, j=0: pl.BlockSpec((1, tm, w), lambda b, i: (b, i, j))
    row = pl.BlockSpec((1, D), lambda b, i: (0, 0))
    return pl.pallas_call(
        body, name="head", grid=(bsz, t // tm),
        in_specs=[tok(D), tok(512), tok(512), pl.BlockSpec((D, D), lambda b, i: (0, 0)),
                  pl.BlockSpec((1, 3, D), lambda b, i: (b, 0, 0)), row, row, tok(D)],
        out_specs=[tok(D), tok(D), pl.BlockSpec((1, 1, D), lambda b, i: (b, 0, 0)),
                   pl.BlockSpec((D, D), lambda b, i: (0, 0)), row, row, pl.BlockSpec((1, 128), lambda b, i: (0, 0))],
        out_shape=[jax.ShapeDtypeStruct(x.shape, F32), jax.ShapeDtypeStruct(x.shape, F32),
                   jax.ShapeDtypeStruct((bsz, 1, D), F32), jax.ShapeDtypeStruct((D, D), F32),
                   jax.ShapeDtypeStruct((1, D), F32), jax.ShapeDtypeStruct((1, D), F32),
                   jax.ShapeDtypeStruct((1, 128), F32)],
        compiler_params=_params(("arbitrary", "arbitrary")),
    )(x, ya, yb, wout, mod3, lnw, lnb, tgt)


def _dh_call(dpg, dpq, dpd, wp, x, mod3, dxa):
    bsz, t, _ = x.shape
    tm = min(256, t)

    def body(dg_ref, dq_ref, dd_ref, w_ref, x_ref, mod_ref, dxa_ref, gx_ref, dmod_ref):
        @pl.when(pl.program_id(1) == 0)
        def _():
            dmod_ref[...] = jnp.zeros_like(dmod_ref)

        nt = lambda a, lo, hi: lax.dot_general(a.astype(MXU), w_ref[:, lo:hi], (((1,), (1,)), ((), ())),
                                               preferred_element_type=F32)
        dh = nt(dg_ref[0], 0, W_GLA) + nt(dq_ref[0], W_GLA, W_GLA + W_GQKV) + nt(dd_ref[0], W_GLA + W_GQKV, PW)
        gx_ref[0] = dh * (1.0 + mod_ref[0, 1:2, :]) + dxa_ref[0]
        dmod_ref[0, 0:1, :] += jnp.sum(dh, axis=0, keepdims=True)
        dmod_ref[0, 1:2, :] += jnp.sum(dh * x_ref[0], axis=0, keepdims=True)

    tok = lambda w: pl.BlockSpec((1, tm, w), lambda b, i: (b, i, 0))
    return pl.pallas_call(
        body, name="dh", grid=(bsz, t // tm),
        in_specs=[tok(W_GLA), tok(W_GQKV), tok(W_GDN), pl.BlockSpec((D, PW), lambda b, i: (0, 0)), tok(D),
                  pl.BlockSpec((1, 3, D), lambda b, i: (b, 0, 0)), tok(D)],
        out_specs=[tok(D), pl.BlockSpec((1, 2, D), lambda b, i: (b, 0, 0))],
        out_shape=[jax.ShapeDtypeStruct(x.shape, F32), jax.ShapeDtypeStruct((bsz, 2, D), F32)],
        compiler_params=_params(("parallel", "arbitrary")),
    )(dpg, dpq, dpd, wp, x, mod3, dxa)


def _dw_call(x, mod3, dp):
    bsz, t, _ = x.shape
    width = dp.shape[-1]
    tm = min(512, t)

    def body(x_ref, mod_ref, dp_ref, dw_ref):
        @pl.when((pl.program_id(0) == 0) & (pl.program_id(1) == 0))
        def _():
            dw_ref[...] = jnp.zeros_like(dw_ref)

        h = x_ref[0] * (1.0 + mod_ref[0, 1:2, :]) + mod_ref[0, 0:1, :]
        dw_ref[...] += lax.dot_general(h.astype(MXU), dp_ref[0].astype(MXU), (((0,), (0,)), ((), ())),
                                       preferred_element_type=F32)

    tok = lambda w: pl.BlockSpec((1, tm, w), lambda b, i: (b, i, 0))
    return pl.pallas_call(
        body, name=f"dw{width}", grid=(bsz, t // tm),
        in_specs=[tok(D), pl.BlockSpec((1, 3, D), lambda b, i: (b, 0, 0)), tok(width)],
        out_specs=pl.BlockSpec((D, width), lambda b, i: (0, 0)),
        out_shape=jax.ShapeDtypeStruct((D, width), F32),
        compiler_params=_params(("arbitrary", "arbitrary")),
    )(x, mod3, dp)


def _adamw(w, g, m, v):
    m = ADAM_B1 * m + (1.0 - ADAM_B1) * g
    v = ADAM_B2 * v + (1.0 - ADAM_B2) * jnp.square(g)
    m_hat = m / (1.0 - ADAM_B1 ** ADAM_STEP)
    v_hat = v / (1.0 - ADAM_B2 ** ADAM_STEP)
    delta = -ADAM_LR * (m_hat / (jnp.sqrt(v_hat) + ADAM_EPS) + ADAM_WD * w)
    return delta, m, v


def _sum8(ref):
    g = ref[0]
    for j in range(1, NDEV):
        g = g + ref[j]
    return g


def _adam_sum_call(name, g8, w, m, v, rows):
    r, c = w.shape

    def body(g_ref, w_ref, m_ref, v_ref, go_ref, d_ref, mo_ref, vo_ref):
        g = _sum8(g_ref)
        go_ref[...] = g
        d_ref[...], mo_ref[...], vo_ref[...] = _adamw(w_ref[...], g, m_ref[...], v_ref[...])

    blk = pl.BlockSpec((rows, c), lambda i: (i, 0))
    return pl.pallas_call(
        body, name=name, grid=(r // rows,),
        in_specs=[pl.BlockSpec((NDEV, rows, c), lambda i: (0, i, 0)), blk, blk, blk],
        out_specs=[blk] * 4, out_shape=[jax.ShapeDtypeStruct((r, c), F32)] * 4,
        compiler_params=_params(("parallel",)),
    )(g8, w, m, v)


def _adam_ada_call(c_all, dmod_cols, w, m, v):
    def body(c_ref, dm_ref, w_ref, m_ref, v_ref, go_ref, d_ref, mo_ref, vo_ref):
        g = lax.dot_general(c_ref[...].astype(MXU), dm_ref[...].astype(MXU), (((0,), (0,)), ((), ())),
                            preferred_element_type=F32)
        go_ref[...] = g
        d_ref[...], mo_ref[...], vo_ref[...] = _adamw(w_ref[...], g, m_ref[...], v_ref[...])

    return pl.pallas_call(
        body, name="adam_ada", out_shape=[jax.ShapeDtypeStruct(w.shape, F32)] * 4, compiler_params=_params(),
    )(c_all, dmod_cols, w, m, v)


def _adam_small_call(g, w, m, v):
    def body(g_ref, w_ref, m_ref, v_ref, d_ref, mo_ref, vo_ref):
        d_ref[...], mo_ref[...], vo_ref[...] = _adamw(w_ref[...], g_ref[...], m_ref[...], v_ref[...])

    return pl.pallas_call(
        body, name="adam_small", out_shape=[jax.ShapeDtypeStruct(w.shape, F32)] * 3, compiler_params=_params(),
    )(g, w, m, v)


def _small_sum_call(sp_all):
    def body(sp_ref, dmod_ref, sum_ref):
        acc = sp_ref[0, 2:3, :]
        for j in range(1, NDEV):
            acc = acc + sp_ref[j, 2:3, :]
        sum_ref[...] = acc
        for j in range(NDEV):
            dmod_ref[2 * j:2 * j + 2, :] = sp_ref[j, 0:2, :]

    return pl.pallas_call(
        body, name="small_sum", out_shape=[jax.ShapeDtypeStruct((2 * NDEV, SPW), F32), jax.ShapeDtypeStruct((1, SPW), F32)],
        compiler_params=_params(),
    )(sp_all)


def _mesh_pos():
    x, y, c = lax.axis_index("x"), lax.axis_index("y"), lax.axis_index("c")
    return x, y, c, 4 * x + 2 * y + c


def _peer(x, y, c, k):
    px = 1 - x if k & 4 else x
    py = 1 - y if k & 2 else y
    pc = 1 - c if k & 1 else c
    return (px, py, pc), 4 * px + 2 * py + pc


_ANY = pl.BlockSpec(memory_space=pl.ANY)
_VMEM = pl.BlockSpec(memory_space=pltpu.VMEM)


def _gather_call(c8, w_ada, b_sh, w_in, w_out, conv_w, wgu):
    big = [w_in, w_out, conv_w, wgu]
    nb = len(big)

    def body(c_ref, wada_ref, b_ref, *rest):
        srcs, outs = rest[:nb], rest[nb:2 * nb]
        call_ref, mod_ref, modp, send_sems, recv_sems, loc_sems = rest[2 * nb:]
        x, y, c, me = _mesh_pos()

        def remote(src, dst, a, k, dev):
            return pltpu.make_async_remote_copy(src_ref=src, dst_ref=dst, send_sem=send_sems.at[a, k],
                                                recv_sem=recv_sems.at[a, k], device_id=dev,
                                                device_id_type=pl.DeviceIdType.MESH)

        sends = []
        call_ref[me] = c_ref[...]
        for k in range(1, NDEV):
            dev, _ = _peer(x, y, c, k)
            sends.append(remote(c_ref, call_ref.at[me], nb, k, dev))
            sends[-1].start()
        local = [pltpu.make_async_copy(srcs[a], outs[a].at[me], loc_sems.at[a]) for a in range(nb)]
        for cp in local:
            cp.start()
        for k in range(1, NDEV):
            dev, _ = _peer(x, y, c, k)
            for a in range(nb):
                sends.append(remote(srcs[a], outs[a].at[me], a, k, dev))
                sends[-1].start()
        for k in range(1, NDEV):
            dev, pidx = _peer(x, y, c, k)
            remote(c_ref, call_ref.at[pidx], nb, k, dev).wait_recv()
        modp[...] = jnp.dot(call_ref[...].reshape(NDEV * 8, D).astype(MXU), wada_ref[...].astype(MXU),
                            preferred_element_type=F32) + b_ref[...]
        mod_ref[me] = modp[pl.ds(pl.multiple_of(me * 8, 8), 8), :]
        for k in range(1, NDEV):
            dev, pidx = _peer(x, y, c, k)
            sends.append(remote(modp.at[pl.ds(pl.multiple_of(pidx * 8, 8), 8), :], mod_ref.at[me], nb + 1, k, dev))
            sends[-1].start()
        for k in range(1, NDEV):
            dev, pidx = _peer(x, y, c, k)
            for a in range(nb):
                remote(srcs[a], outs[a].at[pidx], a, k, dev).wait_recv()
            remote(modp.at[pl.ds(0, 8), :], mod_ref.at[pidx], nb + 1, k, dev).wait_recv()
        for cp in sends:
            cp.wait_send()
        for cp in local:
            cp.wait()

    out_shape = [jax.ShapeDtypeStruct((NDEV,) + a.shape, a.dtype) for a in big]
    out_shape += [jax.ShapeDtypeStruct((NDEV, 8, D), F32), jax.ShapeDtypeStruct((NDEV, 8, SHARD_ADA), F32)]
    return pl.pallas_call(
        body, name="gather", out_shape=out_shape,
        in_specs=[_VMEM, _VMEM, _VMEM] + [_ANY] * nb,
        out_specs=[_ANY] * nb + [_VMEM, _VMEM],
        scratch_shapes=[pltpu.VMEM((NDEV * 8, SHARD_ADA), F32), pltpu.SemaphoreType.DMA((nb + 2, NDEV)),
                        pltpu.SemaphoreType.DMA((nb + 2, NDEV)), pltpu.SemaphoreType.DMA((nb,))],
        compiler_params=_params(),
    )(c8, w_ada, b_sh, *big)


def _reduce_call(blocks, sp):
    nb = len(blocks)

    def body(sp_ref, *rest):
        srcs, outs = rest[:nb], rest[nb:2 * nb]
        spall_ref, send_sems, recv_sems, loc_sems = rest[2 * nb:]
        x, y, c, me = _mesh_pos()

        def remote(src, dst, a, k, dev):
            return pltpu.make_async_remote_copy(src_ref=src, dst_ref=dst, send_sem=send_sems.at[a, k],
                                                recv_sem=recv_sems.at[a, k], device_id=dev,
                                                device_id_type=pl.DeviceIdType.MESH)

        sends = []
        spall_ref[me] = sp_ref[...]
        local = [pltpu.make_async_copy(srcs[a].at[me], outs[a].at[me], loc_sems.at[a]) for a in range(nb)]
        for cp in local:
            cp.start()
        for k in range(1, NDEV):
            dev, pidx = _peer(x, y, c, k)
            sends.append(remote(sp_ref, spall_ref.at[me], nb, k, dev))
            sends[-1].start()
            for a in range(nb):
                sends.append(remote(srcs[a].at[pidx], outs[a].at[me], a, k, dev))
                sends[-1].start()
        for k in range(1, NDEV):
            dev, pidx = _peer(x, y, c, k)
            remote(sp_ref, spall_ref.at[pidx], nb, k, dev).wait_recv()
            for a in range(nb):
                remote(srcs[a].at[pidx], outs[a].at[pidx], a, k, dev).wait_recv()
        for cp in sends:
            cp.wait_send()
        for cp in local:
            cp.wait()

    return pl.pallas_call(
        body, name="reduce", out_shape=[jax.ShapeDtypeStruct(a.shape, a.dtype) for a in blocks]
        + [jax.ShapeDtypeStruct((NDEV, 8, SPW), F32)],
        in_specs=[_VMEM] + [_ANY] * nb, out_specs=[_ANY] * nb + [_VMEM],
        scratch_shapes=[pltpu.SemaphoreType.DMA((nb + 1, NDEV)), pltpu.SemaphoreType.DMA((nb + 1, NDEV)),
                        pltpu.SemaphoreType.DMA((nb,))],
        compiler_params=_params(),
    )(sp, *blocks)


def _pad_cols(a, n):
    return jnp.pad(a, ((0, 0), (0, n - a.shape[1])))


def _assemble_w(w_full):
    q, k, v, lr, og, gqkv, a, b, dog = jnp.split(w_full, [256, 512, 1024, 1040, 1552, 3088, 3092, 3096], axis=1)
    z = lambda n: jnp.zeros((w_full.shape[0], n), w_full.dtype)
    return jnp.concatenate([q, k, v, og, lr, z(112), gqkv, dog, a, b, z(120)], axis=1)


def _disassemble_dw(dw_gla, dw_gqkv, dw_gdn):
    return jnp.concatenate([dw_gla[:, :1024], dw_gla[:, 1536:1552], dw_gla[:, 1024:1536], dw_gqkv,
                            dw_gdn[:, 512:520], dw_gdn[:, :512]], axis=1)


def local_step(x, mod3, wp, wout, conv_w, wgu_p, bg, gla_nw, sc, gdn_nw, lnw, lnb, tgt):
    pg, pq, pd = _proj_call(x, mod3, wp)
    conv = _conv_fwd_call(pq, conv_w)
    ya, s_gla = _gla_fwd_call(pg, wgu_p, bg, gla_nw)
    yb, s_gdn = _gdn_fwd_call(conv, pd, sc, gdn_nw)
    dyin, dxa, dgate, dwout, dlnw, dlnb, loss = _head_call(x, ya, yb, wout, mod3, lnw, lnb, tgt)
    dpg, dwgu, dbg, dnw_gla = _gla_bwd_call(pg, s_gla, dyin, wgu_p, bg, gla_nw)
    dconv, dpd, dsc, dnw_gdn = _gdn_bwd_call(conv, pd, s_gdn, dyin, sc, gdn_nw)
    dpq, dconv_w = _conv_bwd_call(dconv, pq, conv_w)
    gx, dmod2 = _dh_call(dpg, dpq, dpd, wp, x, mod3, dxa)
    dw_in = _disassemble_dw(_dw_call(x, mod3, dpg), _dw_call(x, mod3, dpq), _dw_call(x, mod3, dpd))
    dmod = jnp.concatenate([dmod2, dgate], axis=1)
    return dict(loss=loss[0, 0], gx=gx, dmod=dmod, dw_in=dw_in, dwout=dwout, dconv_w=dconv_w[:4], dwgu=dwgu[:16],
                dbg=dbg, dnw_gla=dnw_gla, dalog=dsc[0:1, :4], ddtb=dsc[1:2, :4], dnw_gdn=dnw_gdn, dlnw=dlnw, dlnb=dlnb)


def kernel(x, c, w_ada, b_ada, w_in, gla_w_gate_up, gla_b_gate, gla_norm_w, gdn_conv_w, gdn_a_log, gdn_dt_bias, gdn_norm_w, w_out, ln_w, ln_b, loss_target, m_w_ada, m_b_ada, m_w_in, m_gla_w_gate_up, m_gla_b_gate, m_gla_norm_w, m_gdn_conv_w, m_gdn_a_log, m_gdn_dt_bias, m_gdn_norm_w, m_w_out, m_ln_w, m_ln_b, v_w_ada, v_b_ada, v_w_in, v_gla_w_gate_up, v_gla_b_gate, v_gla_norm_w, v_gdn_conv_w, v_gdn_a_log, v_gdn_dt_bias, v_gdn_norm_w, v_w_out, v_ln_w, v_ln_b):
    me = 4 * lax.axis_index("x") + 2 * lax.axis_index("y") + lax.axis_index("c")
    bsz = x.shape[0]

    b_sh = lax.dynamic_slice(b_ada, (0, me * SHARD_ADA), (1, SHARD_ADA))
    c8 = jnp.pad(c, ((0, 8 - bsz), (0, 0)))
    win_all, wout_all, conv_all, wgu_all, c_all, mod_blk = _gather_call(
        c8, w_ada[0], b_sh, w_in[0].astype(MXU), w_out[0].astype(MXU), gdn_conv_w[0], gla_w_gate_up[0])
    wp = _assemble_w(jnp.transpose(win_all, (1, 0, 2)).reshape(D, IN_COLS))
    wout = wout_all.reshape(D, D)
    conv_w = jnp.transpose(conv_all, (1, 0, 2)).reshape(4, W_GQKV)
    wgu_p = jnp.pad(jnp.transpose(wgu_all, (1, 0, 2)).reshape(16, 256), ((0, 112), (0, 0)))
    mod = jnp.transpose(mod_blk[:, :bsz, :], (1, 0, 2)).reshape(bsz, 3 * D)
    mod3 = mod.reshape(bsz, 3, D)
    sc = jnp.concatenate([_pad_cols(gdn_a_log, 128), _pad_cols(gdn_dt_bias, 128)], axis=0)

    g = local_step(x, mod3, wp, wout, conv_w, wgu_p, gla_b_gate, gla_norm_w, sc, gdn_norm_w, ln_w, ln_b, loss_target)

    small = jnp.concatenate([g["dlnw"], g["dlnb"], g["dbg"], g["dnw_gla"], g["dnw_gdn"], _pad_cols(g["dalog"], 128),
                             _pad_cols(g["ddtb"], 128), jnp.full((1, 128), g["loss"], F32), jnp.zeros((1, 128), F32)], axis=1)
    sp = jnp.concatenate([g["dmod"].reshape(bsz, SPW), small, jnp.zeros((8 - bsz - 1, SPW), F32)], axis=0)
    blocks = [jnp.transpose(g["dw_in"].reshape(D, NDEV, SHARD_IN), (1, 0, 2)),
              g["dwout"].reshape(NDEV, D // NDEV, D),
              jnp.transpose(g["dconv_w"].reshape(4, NDEV, W_GQKV // NDEV), (1, 0, 2)),
              jnp.transpose(g["dwgu"].reshape(16, NDEV, 256 // NDEV), (1, 0, 2))]
    r_in, r_out, r_conv, r_gu, sp_all = _reduce_call(blocks, sp)
    dmod_all, sums = _small_sum_call(sp_all)

    g_in, d_in, nm_in, nv_in = _adam_sum_call("adam_in", r_in, w_in[0], m_w_in[0], v_w_in[0], 256)
    g_out, d_out, nm_out, nv_out = _adam_sum_call("adam_out", r_out, w_out[0], m_w_out[0], v_w_out[0], D // NDEV)
    g_conv, d_conv, nm_conv, nv_conv = _adam_sum_call("adam_conv", r_conv, gdn_conv_w[0], m_gdn_conv_w[0], v_gdn_conv_w[0], 4)
    g_gu, d_gu, nm_gu, nv_gu = _adam_sum_call("adam_gu", r_gu, gla_w_gate_up[0], m_gla_w_gate_up[0], v_gla_w_gate_up[0], 16)
    c16 = c_all[:, :bsz, :].reshape(NDEV * bsz, D)
    g_ada, d_ada, nm_ada, nv_ada = _adam_ada_call(c16, lax.dynamic_slice(dmod_all, (0, me * SHARD_ADA), (NDEV * bsz, SHARD_ADA)),
                                                  w_ada[0], m_w_ada[0], v_w_ada[0])

    def pack(b_a, lw, lb, bgt, n1, n2, al, dt):
        return jnp.concatenate([b_a, lw, lb, bgt, n1, n2, _pad_cols(al, 128), _pad_cols(dt, 128)], axis=1).reshape(-1, 128)

    g_small = _bada_and_pack(dmod_all, sums)
    w_s = pack(b_ada, ln_w, ln_b, gla_b_gate, gla_norm_w, gdn_norm_w, gdn_a_log, gdn_dt_bias)
    m_s = pack(m_b_ada, m_ln_w, m_ln_b, m_gla_b_gate, m_gla_norm_w, m_gdn_norm_w, m_gdn_a_log, m_gdn_dt_bias)
    v_s = pack(v_b_ada, v_ln_w, v_ln_b, v_gla_b_gate, v_gla_norm_w, v_gdn_norm_w, v_gdn_a_log, v_gdn_dt_bias)
    d_s, nm_s, nv_s = _adam_small_call(g_small, w_s, m_s, v_s)

    def unpack(p):
        f = p.reshape(1, -1)
        b_a, lw, lb, bgt, n1, n2, al, dt = jnp.split(f, [3072, 4096, 5120, 5376, 5504, 5632, 5760], axis=1)
        return dict(b_ada=b_a, ln_w=lw, ln_b=lb, b_gate=bgt, gla_nw=n1, gdn_nw=n2, a_log=al[:, :4], dt_bias=dt[:, :4])

    gs, ds, ms, vs = unpack(g_small), unpack(d_s), unpack(nm_s), unpack(nv_s)
    loss = sums[0, SMALL_W]

    def group(t_ada, t_in, t_gu, t_conv, t_out, s):
        return [t_ada[None], s["b_ada"], t_in[None], t_gu[None], s["b_gate"], s["gla_nw"], t_conv[None], s["a_log"],
                s["dt_bias"], s["gdn_nw"], t_out[None], s["ln_w"], s["ln_b"]]

    return (loss, g["gx"], *group(g_ada, g_in, g_gu, g_conv, g_out, gs), *group(d_ada, d_in, d_gu, d_conv, d_out, ds),
            *group(nm_ada, nm_in, nm_gu, nm_conv, nm_out, ms), *group(nv_ada, nv_in, nv_gu, nv_conv, nv_out, vs))


def _bada_and_pack(dmod_all, sums):
    n = dmod_all.shape[0]

    def body(dm_ref, s_ref, o_ref):
        acc = dm_ref[0:1, :]
        for j in range(1, n):
            acc = acc + dm_ref[j:j + 1, :]
        o_ref[:, 0:SPW] = acc
        o_ref[:, SPW:SPW + SMALL_W] = s_ref[:, 0:SMALL_W]

    packed = pl.pallas_call(
        body, name="bada_pack", out_shape=jax.ShapeDtypeStruct((1, SPW + SMALL_W), F32), compiler_params=_params(),
    )(dmod_all, sums)
    return packed.reshape(-1, 128)
```

```python
import functools

import jax
import jax.numpy as jnp
from jax import lax
from jax.experimental import pallas as pl
from jax.experimental.pallas import tpu as pltpu

F32 = jnp.float32
MXU = jnp.bfloat16
HI = lax.Precision.HIGHEST

D = 1024
NDEV = 8
H = 4
GLA_DK = 64
DV = 128
CHUNK = 64
LN_EPS = 1e-5
RMS_EPS = 1e-6
ALPHA = 2.0 ** 0.25
GATE_NORM = 16.0

W_GLA, W_GQKV, W_GDN = 1664, 1536, 640
PW = W_GLA + W_GQKV + W_GDN
IN_COLS = 3608
SHARD_IN = IN_COLS // NDEV
SHARD_ADA = 3 * D // NDEV
SPW = 3 * D
SMALL_W = 2816

ADAM_LR, ADAM_B1, ADAM_B2, ADAM_EPS, ADAM_WD, ADAM_STEP = 0.001, 0.9, 0.999, 1e-08, 0.01, 10

VMEM_LIMIT = 56 * 1024 * 1024


def _params(sem=None, **kw):
    if sem is not None:
        kw["dimension_semantics"] = sem
    return pltpu.CompilerParams(vmem_limit_bytes=VMEM_LIMIT, **kw)


_MM = (((2,), (1,)), ((0,), (0,)))
_NT = (((2,), (2,)), ((0,), (0,)))
_TN = (((1,), (1,)), ((0,), (0,)))


def _dg(a, b, dims):
    return lax.dot_general(a.astype(MXU), b.astype(MXU), dims, preferred_element_type=F32)


def _hdg(a, b, dims):
    return lax.dot_general(a, b, dims, precision=HI, preferred_element_type=F32)


@jax.custom_vjp
def bmm(a, b):
    return _dg(a, b, _MM)


bmm.defvjp(lambda a, b: (_dg(a, b, _MM), (a, b)), lambda r, g: (_dg(g, r[1], _NT), _dg(r[0], g, _TN)))


@jax.custom_vjp
def bnt(a, b):
    return _dg(a, b, _NT)


bnt.defvjp(lambda a, b: (_dg(a, b, _NT), (a, b)), lambda r, g: (_dg(g, r[1], _MM), _dg(g, r[0], _TN)))


@jax.custom_vjp
def btn(a, b):
    return _dg(a, b, _TN)


btn.defvjp(lambda a, b: (_dg(a, b, _TN), (a, b)), lambda r, g: (_dg(r[1], g, _NT), _dg(r[0], g, _MM)))


@jax.custom_vjp
def unit_lower_solve(a, r1, r2):
    return _solve_fwd(a, r1, r2)[0]


def _solve_fwd(a, r1, r2):
    n = a.shape[-1]
    r, c = _iotas(n)
    p = -a
    t = (r == c).astype(F32) + p
    for _ in range(5):
        p = _hdg(p, p, _MM)
        t = t + _hdg(t, p, _MM)
    s1, s2 = _hdg(t, r1, _MM), _hdg(t, r2, _MM)
    return (s1, s2), (t, s1, s2)


def _solve_bwd(res, g):
    t, s1, s2 = res
    d1, d2 = _hdg(t, g[0], _TN), _hdg(t, g[1], _TN)
    return -(_hdg(d1, s1, _NT) + _hdg(d2, s2, _NT)), d1, d2


unit_lower_solve.defvjp(_solve_fwd, _solve_bwd)


def _iotas(n):
    return lax.broadcasted_iota(jnp.int32, (n, n), 0), lax.broadcasted_iota(jnp.int32, (n, n), 1)


def _col_to_row(col, eye):
    return jnp.sum(jnp.where(eye, col, 0.0), axis=1, keepdims=True)


def _row_to_col(row, eye):
    return jnp.sum(jnp.where(eye, row, 0.0), axis=2, keepdims=True)


def _pick_row(m, i):
    r = lax.broadcasted_iota(jnp.int32, m.shape, 1)
    return jnp.sum(jnp.where(r == i, m, 0.0), axis=1, keepdims=True)


def _rms_gate(o, nw, og):
    on = o * lax.rsqrt(jnp.mean(o * o, axis=-1, keepdims=True) + RMS_EPS) * nw
    return on * jax.nn.silu(og)


def gla_chunk(q, k, v, lr, og, s, wgu, bg, nw):
    n, c, _ = q.shape
    r, cc = _iotas(c)
    causal = r >= cc
    qs = q * (GLA_DK ** -0.5)
    z = bmm(lr, wgu) + bg
    g = jax.nn.log_sigmoid(z) / GATE_NORM
    b = _hdg(jnp.broadcast_to(causal.astype(F32), (n, c, c)), g, _MM)
    bref = _pick_row(b, c // 2 - 1)
    blast = _pick_row(b, c - 1)
    att = jnp.where(causal, bnt(qs * jnp.exp(b - bref), k * jnp.exp(bref - b)), 0.0)
    o = bmm(att, v) + bmm(qs * jnp.exp(b), s)
    rk, ck = _iotas(GLA_DK)
    s_new = _row_to_col(jnp.exp(blast), rk == ck) * s + btn(k * jnp.exp(blast - b), v)
    return _rms_gate(o, nw, og), s_new


def gdn_chunk(cq, ck, cv, a, bb, og, s, alog, dtb, nw):
    c = cq.shape[1]
    r, cc = _iotas(c)
    eye, causal, strict = r == cc, r >= cc, r > cc
    q, k, v = jax.nn.silu(cq), jax.nn.silu(ck), jax.nn.silu(cv)
    q = q * lax.rsqrt(jnp.sum(q * q, axis=-1, keepdims=True) + RMS_EPS) * (DV ** -0.5)
    k = k * lax.rsqrt(jnp.sum(k * k, axis=-1, keepdims=True) + RMS_EPS)
    g = -jnp.exp(alog) * jax.nn.softplus(a + dtb)
    beta = jax.nn.sigmoid(bb)
    d = jnp.sum(jnp.where(causal, _col_to_row(g, eye), 0.0), axis=2, keepdims=True)
    el = jnp.exp(jnp.where(causal, d - _col_to_row(d, eye), -jnp.inf))
    kb = k * beta
    amat = jnp.where(strict, bnt(kb, k) * el, 0.0)
    u, w = unit_lower_solve(amat, v * beta, kb * jnp.exp(d))
    qk = jnp.where(causal, bnt(q, k) * el, 0.0)
    dlast = _pick_row(d, c - 1)
    v_new = u - bmm(w, s)
    o = bmm(q * jnp.exp(d), s) + bmm(qk, v_new)
    s_new = jnp.exp(dlast) * s + btn(k * jnp.exp(dlast - d), v_new)
    return _rms_gate(o, nw, og), s_new


def head_fn(x, y, gate, lnw, lnb, tgt):
    u = ALPHA * x + (1.0 + gate) * y
    mu = jnp.mean(u, axis=-1, keepdims=True)
    var = jnp.mean(jnp.square(u - mu), axis=-1, keepdims=True)
    out = (u - mu) * lax.rsqrt(var + LN_EPS) * lnw + lnb
    err = jnp.square(out - tgt)
    return 0.5 * jnp.sum(jnp.mean(err, axis=-1, keepdims=True), axis=0, keepdims=True)


def _proj_call(x, mod3, wp):
    bsz, t, _ = x.shape
    tm = min(256, t)

    def body(x_ref, mod_ref, w_ref, pg_ref, pq_ref, pd_ref):
        h = x_ref[0] * (1.0 + mod_ref[0, 1:2, :]) + mod_ref[0, 0:1, :]
        p = jnp.dot(h.astype(MXU), w_ref[...], preferred_element_type=F32)
        pg_ref[0] = p[:, :W_GLA]
        pq_ref[0] = p[:, W_GLA:W_GLA + W_GQKV]
        pd_ref[0] = p[:, W_GLA + W_GQKV:]

    tok = lambda w: pl.BlockSpec((1, tm, w), lambda b, i: (b, i, 0))
    return pl.pallas_call(
        body, name="proj", grid=(bsz, t // tm),
        in_specs=[tok(D), pl.BlockSpec((1, 3, D), lambda b, i: (b, 0, 0)), pl.BlockSpec((D, PW), lambda b, i: (0, 0))],
        out_specs=[tok(W_GLA), tok(W_GQKV), tok(W_GDN)],
        out_shape=[jax.ShapeDtypeStruct((bsz, t, w), F32) for w in (W_GLA, W_GQKV, W_GDN)],
        compiler_params=_params(("parallel", "parallel")),
    )(x, mod3, wp)


def _conv_fwd_call(pq, conv_w):
    bsz, t, _ = pq.shape
    tt = min(512, t)
    hb = tt // 8

    def body(x_ref, halo_ref, w_ref, o_ref, buf):
        i = pl.program_id(1)
        buf[0:8, :] = jnp.where(i > 0, halo_ref[0], 0.0)
        buf[8:, :] = x_ref[0]
        acc = w_ref[0:1, :] * buf[pl.ds(5, tt), :]
        for k in range(1, 4):
            acc = acc + w_ref[k:k + 1, :] * buf[pl.ds(5 + k, tt), :]
        o_ref[0] = acc

    return pl.pallas_call(
        body, name="conv_fwd", grid=(bsz, t // tt),
        in_specs=[pl.BlockSpec((1, tt, W_GQKV), lambda b, i: (b, i, 0)),
                  pl.BlockSpec((1, 8, W_GQKV), lambda b, i: (b, jnp.maximum(i * hb - 1, 0), 0)),
                  pl.BlockSpec((4, W_GQKV), lambda b, i: (0, 0))],
        out_specs=pl.BlockSpec((1, tt, W_GQKV), lambda b, i: (b, i, 0)),
        out_shape=jax.ShapeDtypeStruct(pq.shape, F32),
        scratch_shapes=[pltpu.VMEM((tt + 8, W_GQKV), F32)],
        compiler_params=_params(("parallel", "parallel")),
    )(pq, pq, conv_w)


def _conv_bwd_call(dconv, pq, conv_w):
    bsz, t, _ = pq.shape
    tt = min(512, t)
    hb = tt // 8
    nt_ = t // tt

    def body(d_ref, dnext_ref, x_ref, halo_ref, w_ref, din_ref, dw_ref, dbuf, xbuf):
        b, i = pl.program_id(0), pl.program_id(1)

        @pl.when((b == 0) & (i == 0))
        def _():
            dw_ref[...] = jnp.zeros_like(dw_ref)

        dbuf[0:tt, :] = d_ref[0]
        dbuf[tt:, :] = jnp.where(i < nt_ - 1, dnext_ref[0], 0.0)
        xbuf[0:8, :] = jnp.where(i > 0, halo_ref[0], 0.0)
        xbuf[8:, :] = x_ref[0]
        dout = d_ref[0]
        acc = w_ref[0:1, :] * dbuf[pl.ds(3, tt), :]
        dw_ref[0:1, :] += jnp.sum(dout * xbuf[pl.ds(5, tt), :], axis=0, keepdims=True)
        for k in range(1, 4):
            acc = acc + w_ref[k:k + 1, :] * dbuf[pl.ds(3 - k, tt), :]
            dw_ref[k:k + 1, :] += jnp.sum(dout * xbuf[pl.ds(5 + k, tt), :], axis=0, keepdims=True)
        din_ref[0] = acc

    tile = pl.BlockSpec((1, tt, W_GQKV), lambda b, i: (b, i, 0))
    return pl.pallas_call(
        body, name="conv_bwd", grid=(bsz, nt_),
        in_specs=[tile, pl.BlockSpec((1, 8, W_GQKV), lambda b, i: (b, jnp.minimum((i + 1) * hb, t // 8 - 1), 0)),
                  tile, pl.BlockSpec((1, 8, W_GQKV), lambda b, i: (b, jnp.maximum(i * hb - 1, 0), 0)),
                  pl.BlockSpec((4, W_GQKV), lambda b, i: (0, 0))],
        out_specs=[tile, pl.BlockSpec((8, W_GQKV), lambda b, i: (0, 0))],
        out_shape=[jax.ShapeDtypeStruct(pq.shape, F32), jax.ShapeDtypeStruct((8, W_GQKV), F32)],
        scratch_shapes=[pltpu.VMEM((tt + 8, W_GQKV), F32), pltpu.VMEM((tt + 8, W_GQKV), F32)],
        compiler_params=_params(("arbitrary", "arbitrary")),
    )(dconv, dconv, pq, pq, conv_w)


def _chunk_specs(nc, rev, bsz, cols):
    n_of = (lambda n: nc - 1 - n) if rev else (lambda n: n)
    return n_of, [pl.BlockSpec((bsz, CHUNK, w), lambda n, j=j: (0, n_of(n), j)) for w, j in cols]


def _full(shape):
    return pl.BlockSpec(shape, lambda n: (0,) * len(shape))


def _heads(ref, bsz, width, off=0):
    return jnp.stack([ref[b, :, off + width * h:off + width * (h + 1)] for b in range(bsz) for h in range(H)])


def _per_head(ref, bsz, width, rows=slice(None)):
    return jnp.stack([ref[rows, width * h:width * (h + 1)] for _ in range(bsz) for h in range(H)])


_GLA_COLS = [(256, 0), (256, 1), (512, 1), (512, 2), (128, 12)]


def _gla_args(refs, bsz):
    q_ref, k_ref, v_ref, og_ref, lr_ref, wgu_ref, bg_ref, nw_ref = refs
    lr = jnp.stack([lr_ref[b] for b in range(bsz) for _ in range(H)])
    return (_heads(q_ref, bsz, 64), _heads(k_ref, bsz, 64), _heads(v_ref, bsz, 128), lr, _heads(og_ref, bsz, 128),
            _per_head(wgu_ref, bsz, 64), _per_head(bg_ref, bsz, 64), nw_ref[...])


def _gla_fwd_call(pg, wgu, bg, nw):
    bsz, t, _ = pg.shape
    nc = t // CHUNK
    nh = bsz * H
    _, specs = _chunk_specs(nc, False, bsz, _GLA_COLS)

    def body(q_ref, k_ref, v_ref, og_ref, lr_ref, wgu_ref, bg_ref, nw_ref, y_ref, sh_ref, s_ref):
        @pl.when(pl.program_id(0) == 0)
        def _():
            s_ref[...] = jnp.zeros_like(s_ref)

        q, k, v, lr, og, w, b_, nw_ = _gla_args((q_ref, k_ref, v_ref, og_ref, lr_ref, wgu_ref, bg_ref, nw_ref), bsz)
        s = s_ref[...]
        sh_ref[0] = s
        y, s_new = gla_chunk(q, k, v, lr, og, s, w, b_, nw_)
        s_ref[...] = s_new
        for b in range(bsz):
            for h in range(H):
                y_ref[b, :, 128 * h:128 * h + 128] = y[H * b + h]

    return pl.pallas_call(
        body, name="gla_fwd", grid=(nc,),
        in_specs=specs + [_full((128, 256)), _full((1, 256)), _full((1, 128))],
        out_specs=[pl.BlockSpec((bsz, CHUNK, 512), lambda n: (0, n, 0)),
                   pl.BlockSpec((1, nh, GLA_DK, DV), lambda n: (n, 0, 0, 0))],
        out_shape=[jax.ShapeDtypeStruct((bsz, t, 512), F32), jax.ShapeDtypeStruct((nc, nh, GLA_DK, DV), F32)],
        scratch_shapes=[pltpu.VMEM((nh, GLA_DK, DV), F32)],
        compiler_params=_params(("arbitrary",)),
    )(pg, pg, pg, pg, pg, wgu, bg, nw)


def _gla_bwd_call(pg, s_hist, dyin, wgu, bg, nw):
    bsz, t, _ = pg.shape
    nc = t // CHUNK
    nh = bsz * H
    n_of, specs = _chunk_specs(nc, True, bsz, _GLA_COLS)

    def body(q_ref, k_ref, v_ref, og_ref, lr_ref, sh_ref, dy_ref, wgu_ref, bg_ref, nw_ref,
             dp_ref, dwgu_ref, dbg_ref, dnw_ref, ds_ref):
        @pl.when(pl.program_id(0) == 0)
        def _():
            dwgu_ref[...] = jnp.zeros_like(dwgu_ref)
            dbg_ref[...] = jnp.zeros_like(dbg_ref)
            dnw_ref[...] = jnp.zeros_like(dnw_ref)
            ds_ref[...] = jnp.zeros_like(ds_ref)

        q, k, v, lr, og, w, b_, nw_ = _gla_args((q_ref, k_ref, v_ref, og_ref, lr_ref, wgu_ref, bg_ref, nw_ref), bsz)
        _, vjp = jax.vjp(gla_chunk, q, k, v, lr, og, sh_ref[0], w, b_, nw_)
        dq, dk, dv, dlr, dog, ds, dwgu, dbg, dnw = vjp((_heads(dy_ref, bsz, 128), ds_ref[...]))
        ds_ref[...] = ds
        dnw_ref[...] += dnw
        for b in range(bsz):
            for h in range(H):
                i = H * b + h
                dp_ref[b, :, 64 * h:64 * h + 64] = dq[i]
                dp_ref[b, :, 256 + 64 * h:256 + 64 * h + 64] = dk[i]
                dp_ref[b, :, 512 + 128 * h:512 + 128 * h + 128] = dv[i]
                dp_ref[b, :, 1024 + 128 * h:1024 + 128 * h + 128] = dog[i]
                dwgu_ref[:, 64 * h:64 * h + 64] += dwgu[i]
                dbg_ref[:, 64 * h:64 * h + 64] += dbg[i]
            dp_ref[b, :, 1536:1664] = dlr[H * b] + dlr[H * b + 1] + dlr[H * b + 2] + dlr[H * b + 3]

    return pl.pallas_call(
        body, name="gla_bwd", grid=(nc,),
        in_specs=specs + [pl.BlockSpec((1, nh, GLA_DK, DV), lambda n: (n_of(n), 0, 0, 0)),
                          pl.BlockSpec((bsz, CHUNK, 512), lambda n: (0, n_of(n), 0)),
                          _full((128, 256)), _full((1, 256)), _full((1, 128))],
        out_specs=[pl.BlockSpec((bsz, CHUNK, W_GLA), lambda n: (0, n_of(n), 0)),
                   _full((128, 256)), _full((1, 256)), _full((1, 128))],
        out_shape=[jax.ShapeDtypeStruct(pg.shape, F32), jax.ShapeDtypeStruct((128, 256), F32),
                   jax.ShapeDtypeStruct((1, 256), F32), jax.ShapeDtypeStruct((1, 128), F32)],
        scratch_shapes=[pltpu.VMEM((nh, GLA_DK, DV), F32)],
        compiler_params=_params(("arbitrary",)),
    )(pg, pg, pg, pg, pg, s_hist, dyin, wgu, bg, nw)


_GDN_COLS = [(512, 0), (512, 1), (512, 2), (512, 0), (128, 4)]


def _gdn_args(refs, bsz):
    q_ref, k_ref, v_ref, og_ref, ab_ref, sc_ref, nw_ref = refs
    return (_heads(q_ref, bsz, 128), _heads(k_ref, bsz, 128), _heads(v_ref, bsz, 128), _heads(ab_ref, bsz, 1),
            _heads(ab_ref, bsz, 1, off=H), _heads(og_ref, bsz, 128), _per_head(sc_ref, bsz, 1, slice(0, 1)),
            _per_head(sc_ref, bsz, 1, slice(1, 2)), nw_ref[...])


def _gdn_fwd_call(conv, pd, sc, nw):
    bsz, t, _ = conv.shape
    nc = t // CHUNK
    nh = bsz * H
    _, specs = _chunk_specs(nc, False, bsz, _GDN_COLS)

    def body(q_ref, k_ref, v_ref, og_ref, ab_ref, sc_ref, nw_ref, y_ref, sh_ref, s_ref):
        @pl.when(pl.program_id(0) == 0)
        def _():
            s_ref[...] = jnp.zeros_like(s_ref)

        q, k, v, a, bb, og, alog, dtb, nw_ = _gdn_args((q_ref, k_ref, v_ref, og_ref, ab_ref, sc_ref, nw_ref), bsz)
        s = s_ref[...]
        sh_ref[0] = s
        y, s_new = gdn_chunk(q, k, v, a, bb, og, s, alog, dtb, nw_)
        s_ref[...] = s_new
        for b in range(bsz):
            for h in range(H):
                y_ref[b, :, 128 * h:128 * h + 128] = y[H * b + h]

    return pl.pallas_call(
        body, name="gdn_fwd", grid=(nc,),
        in_specs=specs + [_full((2, 128)), _full((1, 128))],
        out_specs=[pl.BlockSpec((bsz, CHUNK, 512), lambda n: (0, n, 0)),
                   pl.BlockSpec((1, nh, DV, DV), lambda n: (n, 0, 0, 0))],
        out_shape=[jax.ShapeDtypeStruct((bsz, t, 512), F32), jax.ShapeDtypeStruct((nc, nh, DV, DV), F32)],
        scratch_shapes=[pltpu.VMEM((nh, DV, DV), F32)],
        compiler_params=_params(("arbitrary",)),
    )(conv, conv, conv, pd, pd, sc, nw)


def _gdn_bwd_call(conv, pd, s_hist, dyin, sc, nw):
    bsz, t, _ = conv.shape
    nc = t // CHUNK
    nh = bsz * H
    n_of, specs = _chunk_specs(nc, True, bsz, _GDN_COLS)

    def body(q_ref, k_ref, v_ref, og_ref, ab_ref, sh_ref, dy_ref, sc_ref, nw_ref,
             dc_ref, dpd_ref, dsc_ref, dnw_ref, ds_ref):
        @pl.when(pl.program_id(0) == 0)
        def _():
            dsc_ref[...] = jnp.zeros_like(dsc_ref)
            dnw_ref[...] = jnp.zeros_like(dnw_ref)
            ds_ref[...] = jnp.zeros_like(ds_ref)

        q, k, v, a, bb, og, alog, dtb, nw_ = _gdn_args((q_ref, k_ref, v_ref, og_ref, ab_ref, sc_ref, nw_ref), bsz)
        _, vjp = jax.vjp(gdn_chunk, q, k, v, a, bb, og, sh_ref[0], alog, dtb, nw_)
        dq, dk, dv, da, db, dog, ds, dalog, ddtb, dnw = vjp((_heads(dy_ref, bsz, 128), ds_ref[...]))
        ds_ref[...] = ds
        dnw_ref[...] += dnw
        for b in range(bsz):
            dpd_ref[b, :, 512:640] = jnp.zeros((CHUNK, 128), F32)
            for h in range(H):
                i = H * b + h
                dc_ref[b, :, 128 * h:128 * h + 128] = dq[i]
                dc_ref[b, :, 512 + 128 * h:512 + 128 * h + 128] = dk[i]
                dc_ref[b, :, 1024 + 128 * h:1024 + 128 * h + 128] = dv[i]
                dpd_ref[b, :, 128 * h:128 * h + 128] = dog[i]
                dpd_ref[b, :, 512 + h:513 + h] = da[i]
                dpd_ref[b, :, 516 + h:517 + h] = db[i]
                dsc_ref[0:1, h:h + 1] += dalog[i]
                dsc_ref[1:2, h:h + 1] += ddtb[i]

    return pl.pallas_call(
        body, name="gdn_bwd", grid=(nc,),
        in_specs=specs + [pl.BlockSpec((1, nh, DV, DV), lambda n: (n_of(n), 0, 0, 0)),
                          pl.BlockSpec((bsz, CHUNK, 512), lambda n: (0, n_of(n), 1)),
                          _full((2, 128)), _full((1, 128))],
        out_specs=[pl.BlockSpec((bsz, CHUNK, W_GQKV), lambda n: (0, n_of(n), 0)),
                   pl.BlockSpec((bsz, CHUNK, W_GDN), lambda n: (0, n_of(n), 0)),
                   _full((2, 128)), _full((1, 128))],
        out_shape=[jax.ShapeDtypeStruct(conv.shape, F32), jax.ShapeDtypeStruct(pd.shape, F32),
                   jax.ShapeDtypeStruct((2, 128), F32), jax.ShapeDtypeStruct((1, 128), F32)],
        scratch_shapes=[pltpu.VMEM((nh, DV, DV), F32)],
        compiler_params=_params(("arbitrary",)),
    )(conv, conv, conv, pd, pd, s_hist, dyin, sc, nw)


def _head_call(x, ya, yb, wout, mod3, lnw, lnb, tgt):
    bsz, t, _ = x.shape
    tm = min(256, t)

    def body(x_ref, ya_ref, yb_ref, w_ref, mod_ref, lnw_ref, lnb_ref, t_ref,
             dyin_ref, dxa_ref, dgate_ref, dw_ref, dlnw_ref, dlnb_ref, loss_ref):
        b, i = pl.program_id(0), pl.program_id(1)

        @pl.when((b == 0) & (i == 0))
        def _():
            dw_ref[...] = jnp.zeros_like(dw_ref)
            dlnw_ref[...] = jnp.zeros_like(dlnw_ref)
            dlnb_ref[...] = jnp.zeros_like(dlnb_ref)
            loss_ref[...] = jnp.zeros_like(loss_ref)

        @pl.when(i == 0)
        def _():
            dgate_ref[...] = jnp.zeros_like(dgate_ref)

        yin = jnp.concatenate([ya_ref[0], yb_ref[0]], axis=-1).astype(MXU)
        w = w_ref[...]
        y = jnp.dot(yin, w, preferred_element_type=F32)
        loss, vjp = jax.vjp(head_fn, x_ref[0], y, mod_ref[0, 2:3, :], lnw_ref[...], lnb_ref[...], t_ref[0])
        dx, dy, dgate, dlnw, dlnb, _ = vjp(jnp.ones((1, 1), F32))
        dyb = dy.astype(MXU)
        dyin_ref[0] = lax.dot_general(dyb, w, (((1,), (1,)), ((), ())), preferred_element_type=F32)
        dw_ref[...] += lax.dot_general(yin, dyb, (((0,), (0,)), ((), ())), preferred_element_type=F32)
        dxa_ref[0] = dx
        dgate_ref[0] += dgate
        dlnw_ref[...] += dlnw
        dlnb_ref[...] += dlnb
        loss_ref[...] += jnp.broadcast_to(loss, (1, 128))

    tok = lambda w, j=0: pl.BlockSpec((1, tm, w), lambda b, i: (b, i, j))
    row = pl.BlockSpec((1, D), lambda b, i: (0, 0))
    return pl.pallas_call(
        body, name="head", grid=(bsz, t // tm),
        in_specs=[tok(D), tok(512), tok(512), pl.BlockSpec((D, D), lambda b, i: (0, 0)),
                  pl.BlockSpec((1, 3, D), lambda b, i: (b, 0, 0)), row, row, tok(D)],
        out_specs=[tok(D), tok(D), pl.BlockSpec((1, 1, D), lambda b, i: (b, 0, 0)),
                   pl.BlockSpec((D, D), lambda b, i: (0, 0)), row, row, pl.BlockSpec((1, 128), lambda b, i: (0, 0))],
        out_shape=[jax.ShapeDtypeStruct(x.shape, F32), jax.ShapeDtypeStruct(x.shape, F32),
                   jax.ShapeDtypeStruct((bsz, 1, D), F32), jax.ShapeDtypeStruct((D, D), F32),
                   jax.ShapeDtypeStruct((1, D), F32), jax.ShapeDtypeStruct((1, D), F32),
                   jax.ShapeDtypeStruct((1, 128), F32)],
        compiler_params=_params(("arbitrary", "arbitrary")),
    )(x, ya, yb, wout, mod3, lnw, lnb, tgt)


def _dh_call(dpg, dpq, dpd, wp, x, mod3, dxa):
    bsz, t, _ = x.shape
    tm = min(256, t)

    def body(dg_ref, dq_ref, dd_ref, w_ref, x_ref, mod_ref, dxa_ref, gx_ref, dmod_ref):
        @pl.when(pl.program_id(1) == 0)
        def _():
            dmod_ref[...] = jnp.zeros_like(dmod_ref)

        nt = lambda a, lo, hi: lax.dot_general(a.astype(MXU), w_ref[:, lo:hi], (((1,), (1,)), ((), ())),
                                               preferred_element_type=F32)
        dh = nt(dg_ref[0], 0, W_GLA) + nt(dq_ref[0], W_GLA, W_GLA + W_GQKV) + nt(dd_ref[0], W_GLA + W_GQKV, PW)
        gx_ref[0] = dh * (1.0 + mod_ref[0, 1:2, :]) + dxa_ref[0]
        dmod_ref[0, 0:1, :] += jnp.sum(dh, axis=0, keepdims=True)
        dmod_ref[0, 1:2, :] += jnp.sum(dh * x_ref[0], axis=0, keepdims=True)

    tok = lambda w: pl.BlockSpec((1, tm, w), lambda b, i: (b, i, 0))
    return pl.pallas_call(
        body, name="dh", grid=(bsz, t // tm),
        in_specs=[tok(W_GLA), tok(W_GQKV), tok(W_GDN), pl.BlockSpec((D, PW), lambda b, i: (0, 0)), tok(D),
                  pl.BlockSpec((1, 3, D), lambda b, i: (b, 0, 0)), tok(D)],
        out_specs=[tok(D), pl.BlockSpec((1, 2, D), lambda b, i: (b, 0, 0))],
        out_shape=[jax.ShapeDtypeStruct(x.shape, F32), jax.ShapeDtypeStruct((bsz, 2, D), F32)],
        compiler_params=_params(("parallel", "arbitrary")),
    )(dpg, dpq, dpd, wp, x, mod3, dxa)


def _dw_call(x, mod3, dp):
    bsz, t, _ = x.shape
    width = dp.shape[-1]
    tm = min(512, t)

    def body(x_ref, mod_ref, dp_ref, dw_ref):
        @pl.when((pl.program_id(0) == 0) & (pl.program_id(1) == 0))
        def _():
            dw_ref[...] = jnp.zeros_like(dw_ref)

        h = x_ref[0] * (1.0 + mod_ref[0, 1:2, :]) + mod_ref[0, 0:1, :]
        dw_ref[...] += lax.dot_general(h.astype(MXU), dp_ref[0].astype(MXU), (((0,), (0,)), ((), ())),
                                       preferred_element_type=F32)

    tok = lambda w: pl.BlockSpec((1, tm, w), lambda b, i: (b, i, 0))
    return pl.pallas_call(
        body, name=f"dw{width}", grid=(bsz, t // tm),
        in_specs=[tok(D), pl.BlockSpec((1, 3, D), lambda b, i: (b, 0, 0)), tok(width)],
        out_specs=pl.BlockSpec((D, width), lambda b, i: (0, 0)),
        out_shape=jax.ShapeDtypeStruct((D, width), F32),
        compiler_params=_params(("arbitrary", "arbitrary")),
    )(x, mod3, dp)


def _adamw(w, g, m, v):
    m = ADAM_B1 * m + (1.0 - ADAM_B1) * g
    v = ADAM_B2 * v + (1.0 - ADAM_B2) * jnp.square(g)
    m_hat = m / (1.0 - ADAM_B1 ** ADAM_STEP)
    v_hat = v / (1.0 - ADAM_B2 ** ADAM_STEP)
    delta = -ADAM_LR * (m_hat / (jnp.sqrt(v_hat) + ADAM_EPS) + ADAM_WD * w)
    return delta, m, v


def _sum8(ref):
    g = ref[0]
    for j in range(1, NDEV):
        g = g + ref[j]
    return g


def _adam_sum_call(name, g8, w, m, v, rows):
    r, c = w.shape

    def body(g_ref, w_ref, m_ref, v_ref, go_ref, d_ref, mo_ref, vo_ref):
        g = _sum8(g_ref)
        go_ref[...] = g
        d_ref[...], mo_ref[...], vo_ref[...] = _adamw(w_ref[...], g, m_ref[...], v_ref[...])

    blk = pl.BlockSpec((rows, c), lambda i: (i, 0))
    return pl.pallas_call(
        body, name=name, grid=(r // rows,),
        in_specs=[pl.BlockSpec((NDEV, rows, c), lambda i: (0, i, 0)), blk, blk, blk],
        out_specs=[blk] * 4, out_shape=[jax.ShapeDtypeStruct((r, c), F32)] * 4,
        compiler_params=_params(("parallel",)),
    )(g8, w, m, v)


def _adam_ada_call(c_all, dmod_cols, w, m, v):
    def body(c_ref, dm_ref, w_ref, m_ref, v_ref, go_ref, d_ref, mo_ref, vo_ref):
        g = lax.dot_general(c_ref[...].astype(MXU), dm_ref[...].astype(MXU), (((0,), (0,)), ((), ())),
                            preferred_element_type=F32)
        go_ref[...] = g
        d_ref[...], mo_ref[...], vo_ref[...] = _adamw(w_ref[...], g, m_ref[...], v_ref[...])

    return pl.pallas_call(
        body, name="adam_ada", out_shape=[jax.ShapeDtypeStruct(w.shape, F32)] * 4, compiler_params=_params(),
    )(c_all, dmod_cols, w, m, v)


def _adam_small_call(g, w, m, v):
    def body(g_ref, w_ref, m_ref, v_ref, d_ref, mo_ref, vo_ref):
        d_ref[...], mo_ref[...], vo_ref[...] = _adamw(w_ref[...], g_ref[...], m_ref[...], v_ref[...])

    return pl.pallas_call(
        body, name="adam_small", out_shape=[jax.ShapeDtypeStruct(w.shape, F32)] * 3, compiler_params=_params(),
    )(g, w, m, v)


def _small_sum_call(sp_all):
    def body(sp_ref, dmod_ref, sum_ref):
        acc = sp_ref[0, 2:3, :]
        for j in range(1, NDEV):
            acc = acc + sp_ref[j, 2:3, :]
        sum_ref[...] = acc
        for j in range(NDEV):
            dmod_ref[2 * j:2 * j + 2, :] = sp_ref[j, 0:2, :]

    return pl.pallas_call(
        body, name="small_sum", out_shape=[jax.ShapeDtypeStruct((2 * NDEV, SPW), F32), jax.ShapeDtypeStruct((1, SPW), F32)],
        compiler_params=_params(),
    )(sp_all)


def _mesh_pos():
    x, y, c = lax.axis_index("x"), lax.axis_index("y"), lax.axis_index("c")
    return x, y, c, 4 * x + 2 * y + c


def _peer(x, y, c, k):
    px = 1 - x if k & 4 else x
    py = 1 - y if k & 2 else y
    pc = 1 - c if k & 1 else c
    return (px, py, pc), 4 * px + 2 * py + pc


_ANY = pl.BlockSpec(memory_space=pl.ANY)
_VMEM = pl.BlockSpec(memory_space=pltpu.VMEM)


def _gather_call(c8, w_ada, b_sh, w_in, w_out, conv_w, wgu):
    big = [w_in, w_out, conv_w, wgu]
    nb = len(big)

    def body(c_ref, wada_ref, b_ref, *rest):
        srcs, outs = rest[:nb], rest[nb:2 * nb]
        call_ref, mod_ref, modp, send_sems, recv_sems, loc_sems = rest[2 * nb:]
        x, y, c, me = _mesh_pos()

        def remote(src, dst, a, k, dev):
            return pltpu.make_async_remote_copy(src_ref=src, dst_ref=dst, send_sem=send_sems.at[a, k],
                                                recv_sem=recv_sems.at[a, k], device_id=dev,
                                                device_id_type=pl.DeviceIdType.MESH)

        sends = []
        call_ref[me] = c_ref[...]
        for k in range(1, NDEV):
            dev, _ = _peer(x, y, c, k)
            sends.append(remote(c_ref, call_ref.at[me], nb, k, dev))
            sends[-1].start()
        local = [pltpu.make_async_copy(srcs[a], outs[a].at[me], loc_sems.at[a]) for a in range(nb)]
        for cp in local:
            cp.start()
        for k in range(1, NDEV):
            dev, _ = _peer(x, y, c, k)
            for a in range(nb):
                sends.append(remote(srcs[a], outs[a].at[me], a, k, dev))
                sends[-1].start()
        for k in range(1, NDEV):
            dev, pidx = _peer(x, y, c, k)
            remote(c_ref, call_ref.at[pidx], nb, k, dev).wait_recv()
        modp[...] = jnp.dot(call_ref[...].reshape(NDEV * 8, D).astype(MXU), wada_ref[...].astype(MXU),
                            preferred_element_type=F32) + b_ref[...]
        mod_ref[me] = modp[pl.ds(pl.multiple_of(me * 8, 8), 8), :]
        for k in range(1, NDEV):
            dev, pidx = _peer(x, y, c, k)
            sends.append(remote(modp.at[pl.ds(pl.multiple_of(pidx * 8, 8), 8), :], mod_ref.at[me], nb + 1, k, dev))
            sends[-1].start()
        for k in range(1, NDEV):
            dev, pidx = _peer(x, y, c, k)
            for a in range(nb):
                remote(srcs[a], outs[a].at[pidx], a, k, dev).wait_recv()
            remote(modp.at[pl.ds(0, 8), :], mod_ref.at[pidx], nb + 1, k, dev).wait_recv()
        for cp in sends:
            cp.wait_send()
        for cp in local:
            cp.wait()

    out_shape = [jax.ShapeDtypeStruct((NDEV,) + a.shape, a.dtype) for a in big]
    out_shape += [jax.ShapeDtypeStruct((NDEV, 8, D), F32), jax.ShapeDtypeStruct((NDEV, 8, SHARD_ADA), F32)]
    return pl.pallas_call(
        body, name="gather", out_shape=out_shape,
        in_specs=[_VMEM, _VMEM, _VMEM] + [_ANY] * nb,
        out_specs=[_ANY] * nb + [_VMEM, _VMEM],
        scratch_shapes=[pltpu.VMEM((NDEV * 8, SHARD_ADA), F32), pltpu.SemaphoreType.DMA((nb + 2, NDEV)),
                        pltpu.SemaphoreType.DMA((nb + 2, NDEV)), pltpu.SemaphoreType.DMA((nb,))],
        compiler_params=_params(),
    )(c8, w_ada, b_sh, *big)


def _reduce_call(blocks, sp):
    nb = len(blocks)

    def body(sp_ref, *rest):
        srcs, outs = rest[:nb], rest[nb:2 * nb]
        spall_ref, send_sems, recv_sems, loc_sems = rest[2 * nb:]
        x, y, c, me = _mesh_pos()

        def remote(src, dst, a, k, dev):
            return pltpu.make_async_remote_copy(src_ref=src, dst_ref=dst, send_sem=send_sems.at[a, k],
                                                recv_sem=recv_sems.at[a, k], device_id=dev,
                                                device_id_type=pl.DeviceIdType.MESH)

        sends = []
        spall_ref[me] = sp_ref[...]
        local = [pltpu.make_async_copy(srcs[a].at[me], outs[a].at[me], loc_sems.at[a]) for a in range(nb)]
        for cp in local:
            cp.start()
        for k in range(1, NDEV):
            dev, pidx = _peer(x, y, c, k)
            sends.append(remote(sp_ref, spall_ref.at[me], nb, k, dev))
            sends[-1].start()
            for a in range(nb):
                sends.append(remote(srcs[a].at[pidx], outs[a].at[me], a, k, dev))
                sends[-1].start()
        for k in range(1, NDEV):
            dev, pidx = _peer(x, y, c, k)
            remote(sp_ref, spall_ref.at[pidx], nb, k, dev).wait_recv()
            for a in range(nb):
                remote(srcs[a].at[pidx], outs[a].at[pidx], a, k, dev).wait_recv()
        for cp in sends:
            cp.wait_send()
        for cp in local:
            cp.wait()

    return pl.pallas_call(
        body, name="reduce", out_shape=[jax.ShapeDtypeStruct(a.shape, a.dtype) for a in blocks]
        + [jax.ShapeDtypeStruct((NDEV, 8, SPW), F32)],
        in_specs=[_VMEM] + [_ANY] * nb, out_specs=[_ANY] * nb + [_VMEM],
        scratch_shapes=[pltpu.SemaphoreType.DMA((nb + 1, NDEV)), pltpu.SemaphoreType.DMA((nb + 1, NDEV)),
                        pltpu.SemaphoreType.DMA((nb,))],
        compiler_params=_params(),
    )(sp, *blocks)


def _pad_cols(a, n):
    return jnp.pad(a, ((0, 0), (0, n - a.shape[1])))


def _assemble_w(w_full):
    q, k, v, lr, og, gqkv, a, b, dog = jnp.split(w_full, [256, 512, 1024, 1040, 1552, 3088, 3092, 3096], axis=1)
    z = lambda n: jnp.zeros((w_full.shape[0], n), w_full.dtype)
    return jnp.concatenate([q, k, v, og, lr, z(112), gqkv, dog, a, b, z(120)], axis=1)


def _disassemble_dw(dw_gla, dw_gqkv, dw_gdn):
    return jnp.concatenate([dw_gla[:, :1024], dw_gla[:, 1536:1552], dw_gla[:, 1024:1536], dw_gqkv,
                            dw_gdn[:, 512:520], dw_gdn[:, :512]], axis=1)


def local_step(x, mod3, wp, wout, conv_w, wgu_p, bg, gla_nw, sc, gdn_nw, lnw, lnb, tgt):
    pg, pq, pd = _proj_call(x, mod3, wp)
    conv = _conv_fwd_call(pq, conv_w)
    ya, s_gla = _gla_fwd_call(pg, wgu_p, bg, gla_nw)
    yb, s_gdn = _gdn_fwd_call(conv, pd, sc, gdn_nw)
    dyin, dxa, dgate, dwout, dlnw, dlnb, loss = _head_call(x, ya, yb, wout, mod3, lnw, lnb, tgt)
    dpg, dwgu, dbg, dnw_gla = _gla_bwd_call(pg, s_gla, dyin, wgu_p, bg, gla_nw)
    dconv, dpd, dsc, dnw_gdn = _gdn_bwd_call(conv, pd, s_gdn, dyin, sc, gdn_nw)
    dpq, dconv_w = _conv_bwd_call(dconv, pq, conv_w)
    gx, dmod2 = _dh_call(dpg, dpq, dpd, wp, x, mod3, dxa)
    dw_in = _disassemble_dw(_dw_call(x, mod3, dpg), _dw_call(x, mod3, dpq), _dw_call(x, mod3, dpd))
    dmod = jnp.concatenate([dmod2, dgate], axis=1)
    return dict(loss=loss[0, 0], gx=gx, dmod=dmod, dw_in=dw_in, dwout=dwout, dconv_w=dconv_w[:4], dwgu=dwgu[:16],
                dbg=dbg, dnw_gla=dnw_gla, dalog=dsc[0:1, :4], ddtb=dsc[1:2, :4], dnw_gdn=dnw_gdn, dlnw=dlnw, dlnb=dlnb)


def kernel(x, c, w_ada, b_ada, w_in, gla_w_gate_up, gla_b_gate, gla_norm_w, gdn_conv_w, gdn_a_log, gdn_dt_bias, gdn_norm_w, w_out, ln_w, ln_b, loss_target, m_w_ada, m_b_ada, m_w_in, m_gla_w_gate_up, m_gla_b_gate, m_gla_norm_w, m_gdn_conv_w, m_gdn_a_log, m_gdn_dt_bias, m_gdn_norm_w, m_w_out, m_ln_w, m_ln_b, v_w_ada, v_b_ada, v_w_in, v_gla_w_gate_up, v_gla_b_gate, v_gla_norm_w, v_gdn_conv_w, v_gdn_a_log, v_gdn_dt_bias, v_gdn_norm_w, v_w_out, v_ln_w, v_ln_b):
    me = 4 * lax.axis_index("x") + 2 * lax.axis_index("y") + lax.axis_index("c")
    bsz = x.shape[0]

    b_sh = lax.dynamic_slice(b_ada, (0, me * SHARD_ADA), (1, SHARD_ADA))
    c8 = jnp.pad(c, ((0, 8 - bsz), (0, 0)))
    win_all, wout_all, conv_all, wgu_all, c_all, mod_blk = _gather_call(
        c8, w_ada[0], b_sh, w_in[0].astype(MXU), w_out[0].astype(MXU), gdn_conv_w[0], gla_w_gate_up[0])
    wp = _assemble_w(jnp.transpose(win_all, (1, 0, 2)).reshape(D, IN_COLS))
    wout = wout_all.reshape(D, D)
    conv_w = jnp.transpose(conv_all, (1, 0, 2)).reshape(4, W_GQKV)
    wgu_p = jnp.pad(jnp.transpose(wgu_all, (1, 0, 2)).reshape(16, 256), ((0, 112), (0, 0)))
    mod = jnp.transpose(mod_blk[:, :bsz, :], (1, 0, 2)).reshape(bsz, 3 * D)
    mod3 = mod.reshape(bsz, 3, D)
    sc = jnp.concatenate([_pad_cols(gdn_a_log, 128), _pad_cols(gdn_dt_bias, 128)], axis=0)

    g = local_step(x, mod3, wp, wout, conv_w, wgu_p, gla_b_gate, gla_norm_w, sc, gdn_norm_w, ln_w, ln_b, loss_target)

    small = jnp.concatenate([g["dlnw"], g["dlnb"], g["dbg"], g["dnw_gla"], g["dnw_gdn"], _pad_cols(g["dalog"], 128),
                             _pad_cols(g["ddtb"], 128), jnp.full((1, 128), g["loss"], F32), jnp.zeros((1, 128), F32)], axis=1)
    sp = jnp.concatenate([g["dmod"].reshape(bsz, SPW), small, jnp.zeros((8 - bsz - 1, SPW), F32)], axis=0)
    blocks = [jnp.transpose(g["dw_in"].reshape(D, NDEV, SHARD_IN), (1, 0, 2)),
              g["dwout"].reshape(NDEV, D // NDEV, D),
              jnp.transpose(g["dconv_w"].reshape(4, NDEV, W_GQKV // NDEV), (1, 0, 2)),
              jnp.transpose(g["dwgu"].reshape(16, NDEV, 256 // NDEV), (1, 0, 2))]
    r_in, r_out, r_conv, r_gu, sp_all = _reduce_call(blocks, sp)
    dmod_all, sums = _small_sum_call(sp_all)

    g_in, d_in, nm_in, nv_in = _adam_sum_call("adam_in", r_in, w_in[0], m_w_in[0], v_w_in[0], 256)
    g_out, d_out, nm_out, nv_out = _adam_sum_call("adam_out", r_out, w_out[0], m_w_out[0], v_w_out[0], D // NDEV)
    g_conv, d_conv, nm_conv, nv_conv = _adam_sum_call("adam_conv", r_conv, gdn_conv_w[0], m_gdn_conv_w[0], v_gdn_conv_w[0], 4)
    g_gu, d_gu, nm_gu, nv_gu = _adam_sum_call("adam_gu", r_gu, gla_w_gate_up[0], m_gla_w_gate_up[0], v_gla_w_gate_up[0], 16)
    c16 = c_all[:, :bsz, :].reshape(NDEV * bsz, D)
    g_ada, d_ada, nm_ada, nv_ada = _adam_ada_call(c16, lax.dynamic_slice(dmod_all, (0, me * SHARD_ADA), (NDEV * bsz, SHARD_ADA)),
                                                  w_ada[0], m_w_ada[0], v_w_ada[0])

    def pack(b_a, lw, lb, bgt, n1, n2, al, dt):
        return jnp.concatenate([b_a, lw, lb, bgt, n1, n2, _pad_cols(al, 128), _pad_cols(dt, 128)], axis=1).reshape(-1, 128)

    g_small = _bada_and_pack(dmod_all, sums)
    w_s = pack(b_ada, ln_w, ln_b, gla_b_gate, gla_norm_w, gdn_norm_w, gdn_a_log, gdn_dt_bias)
    m_s = pack(m_b_ada, m_ln_w, m_ln_b, m_gla_b_gate, m_gla_norm_w, m_gdn_norm_w, m_gdn_a_log, m_gdn_dt_bias)
    v_s = pack(v_b_ada, v_ln_w, v_ln_b, v_gla_b_gate, v_gla_norm_w, v_gdn_norm_w, v_gdn_a_log, v_gdn_dt_bias)
    d_s, nm_s, nv_s = _adam_small_call(g_small, w_s, m_s, v_s)

    def unpack(p):
        f = p.reshape(1, -1)
        b_a, lw, lb, bgt, n1, n2, al, dt = jnp.split(f, [3072, 4096, 5120, 5376, 5504, 5632, 5760], axis=1)
        return dict(b_ada=b_a, ln_w=lw, ln_b=lb, b_gate=bgt, gla_nw=n1, gdn_nw=n2, a_log=al[:, :4], dt_bias=dt[:, :4])

    gs, ds, ms, vs = unpack(g_small), unpack(d_s), unpack(nm_s), unpack(nv_s)
    loss = sums[0, SMALL_W]

    def group(t_ada, t_in, t_gu, t_conv, t_out, s):
        return [t_ada[None], s["b_ada"], t_in[None], t_gu[None], s["b_gate"], s["gla_nw"], t_conv[None], s["a_log"],
                s["dt_bias"], s["gdn_nw"], t_out[None], s["ln_w"], s["ln_b"]]

    return (loss, g["gx"], *group(g_ada, g_in, g_gu, g_conv, g_out, gs), *group(d_ada, d_in, d_gu, d_conv, d_out, ds),
            *group(nm_ada, nm_in, nm_gu, nm_conv, nm_out, ms), *group(nv_ada, nv_in, nv_gu, nv_conv, nv_out, vs))


def _bada_and_pack(dmod_all, sums):
    n = dmod_all.shape[0]

    def body(dm_ref, s_ref, o_ref):
        acc = dm_ref[0:1, :]
        for j in range(1, n):
            acc = acc + dm_ref[j:j + 1, :]
        o_ref[:, 0:SPW] = acc
        o_ref[:, SPW:SPW + SMALL_W] = s_ref[:, 0:SMALL_W]

    packed = pl.pallas_call(
        body, name="bada_pack", out_shape=jax.ShapeDtypeStruct((1, SPW + SMALL_W), F32), compiler_params=_params(),
    )(dmod_all, sums)
    return packed.reshape(-1, 128)
```

```python
import functools

import jax
import jax.numpy as jnp
from jax import lax
from jax.experimental import pallas as pl
from jax.experimental.pallas import tpu as pltpu

F32 = jnp.float32
MXU = jnp.bfloat16
WIRE = jnp.bfloat16
HI = lax.Precision.HIGH

D = 1024
NDEV = 8
H = 4
GLA_DK = 64
DV = 128
CHUNK = 64
LN_EPS = 1e-5
RMS_EPS = 1e-6
ALPHA = 2.0 ** 0.25
GATE_NORM = 16.0

W_GLA, W_GQKV, W_GDN = 1664, 1536, 640
PW = W_GLA + W_GQKV + W_GDN
IN_COLS = 3608
SHARD_IN = IN_COLS // NDEV
SHARD_ADA = 3 * D // NDEV
SPW = 3 * D
SMALL_W = 2816

ADAM_LR, ADAM_B1, ADAM_B2, ADAM_EPS, ADAM_WD, ADAM_STEP = 0.001, 0.9, 0.999, 1e-08, 0.01, 10

VMEM_LIMIT = 56 * 1024 * 1024


def _params(sem=None, **kw):
    if sem is not None:
        kw["dimension_semantics"] = sem
    return pltpu.CompilerParams(vmem_limit_bytes=VMEM_LIMIT, **kw)


_MM = (((2,), (1,)), ((0,), (0,)))
_NT = (((2,), (2,)), ((0,), (0,)))
_TN = (((1,), (1,)), ((0,), (0,)))


def _dg(a, b, dims):
    return lax.dot_general(a.astype(MXU), b.astype(MXU), dims, preferred_element_type=F32)


def _hdg(a, b, dims):
    return lax.dot_general(a, b, dims, precision=HI, preferred_element_type=F32)


@jax.custom_vjp
def bmm(a, b):
    return _dg(a, b, _MM)


bmm.defvjp(lambda a, b: (_dg(a, b, _MM), (a, b)), lambda r, g: (_dg(g, r[1], _NT), _dg(r[0], g, _TN)))


@jax.custom_vjp
def bnt(a, b):
    return _dg(a, b, _NT)


bnt.defvjp(lambda a, b: (_dg(a, b, _NT), (a, b)), lambda r, g: (_dg(g, r[1], _MM), _dg(g, r[0], _TN)))


@jax.custom_vjp
def btn(a, b):
    return _dg(a, b, _TN)


btn.defvjp(lambda a, b: (_dg(a, b, _TN), (a, b)), lambda r, g: (_dg(r[1], g, _NT), _dg(r[0], g, _MM)))


def unit_lower_inverse(a):
    n = a.shape[-1]
    r, c = _iotas(n)
    p = -a
    t = (r == c).astype(F32) + p
    for _ in range(5):
        p = _hdg(p, p, _MM)
        t = t + _hdg(t, p, _MM)
    return t


@jax.custom_vjp
def unit_lower_solve(a, t, r1, r2):
    return _hdg(t, r1, _MM), _hdg(t, r2, _MM)


def _solve_fwd(a, t, r1, r2):
    s1, s2 = _hdg(t, r1, _MM), _hdg(t, r2, _MM)
    return (s1, s2), (t, s1, s2)


def _solve_bwd(res, g):
    t, s1, s2 = res
    d1, d2 = _hdg(t, g[0], _TN), _hdg(t, g[1], _TN)
    return -(_hdg(d1, s1, _NT) + _hdg(d2, s2, _NT)), jnp.zeros_like(t), d1, d2


unit_lower_solve.defvjp(_solve_fwd, _solve_bwd)


def _iotas(n):
    return lax.broadcasted_iota(jnp.int32, (n, n), 0), lax.broadcasted_iota(jnp.int32, (n, n), 1)


def _col_to_row(col, eye):
    return jnp.sum(jnp.where(eye, col, 0.0), axis=1, keepdims=True)


def _row_to_col(row, eye):
    return jnp.sum(jnp.where(eye, row, 0.0), axis=2, keepdims=True)


def _pick_row(m, i):
    r = lax.broadcasted_iota(jnp.int32, m.shape, 1)
    return jnp.sum(jnp.where(r == i, m, 0.0), axis=1, keepdims=True)


def _rms_gate(o, nw, og):
    on = o * lax.rsqrt(jnp.mean(o * o, axis=-1, keepdims=True) + RMS_EPS) * nw
    return on * jax.nn.silu(og)


def gla_chunk(q, k, v, lr, og, s, wgu, bg, nw):
    n, c, _ = q.shape
    r, cc = _iotas(c)
    causal = r >= cc
    qs = q * (GLA_DK ** -0.5)
    z = bmm(lr, wgu) + bg
    g = jax.nn.log_sigmoid(z) / GATE_NORM
    b = _hdg(jnp.broadcast_to(causal.astype(F32), (n, c, c)), g, _MM)
    bref = _pick_row(b, c // 2 - 1)
    blast = _pick_row(b, c - 1)
    att = jnp.where(causal, bnt(qs * jnp.exp(b - bref), k * jnp.exp(bref - b)), 0.0)
    o = bmm(att, v) + bmm(qs * jnp.exp(b), s)
    rk, ck = _iotas(GLA_DK)
    s_new = _row_to_col(jnp.exp(blast), rk == ck) * s + btn(k * jnp.exp(blast - b), v)
    return _rms_gate(o, nw, og), s_new


def gdn_chunk(cq, ck, cv, a, bb, og, s, alog, dtb, nw, tinv=None):
    c = cq.shape[1]
    r, cc = _iotas(c)
    eye, causal, strict = r == cc, r >= cc, r > cc
    q, k, v = jax.nn.silu(cq), jax.nn.silu(ck), jax.nn.silu(cv)
    q = q * lax.rsqrt(jnp.sum(q * q, axis=-1, keepdims=True) + RMS_EPS) * (DV ** -0.5)
    k = k * lax.rsqrt(jnp.sum(k * k, axis=-1, keepdims=True) + RMS_EPS)
    g = -jnp.exp(alog) * jax.nn.softplus(a + dtb)
    beta = jax.nn.sigmoid(bb)
    d = jnp.sum(jnp.where(causal, _col_to_row(g, eye), 0.0), axis=2, keepdims=True)
    el = jnp.exp(jnp.where(causal, d - _col_to_row(d, eye), -jnp.inf))
    kb = k * beta
    amat = jnp.where(strict, bnt(kb, k) * el, 0.0)
    t = unit_lower_inverse(amat) if tinv is None else tinv
    u, w = unit_lower_solve(amat, t, v * beta, kb * jnp.exp(d))
    qk = jnp.where(causal, bnt(q, k) * el, 0.0)
    dlast = _pick_row(d, c - 1)
    v_new = u - bmm(w, s)
    o = bmm(q * jnp.exp(d), s) + bmm(qk, v_new)
    s_new = jnp.exp(dlast) * s + btn(k * jnp.exp(dlast - d), v_new)
    y = _rms_gate(o, nw, og)
    return (y, s_new, t) if tinv is None else (y, s_new)


def head_fn(x, y, gate, lnw, lnb, tgt):
    u = ALPHA * x + (1.0 + gate) * y
    mu = jnp.mean(u, axis=-1, keepdims=True)
    var = jnp.mean(jnp.square(u - mu), axis=-1, keepdims=True)
    out = (u - mu) * lax.rsqrt(var + LN_EPS) * lnw + lnb
    err = jnp.square(out - tgt)
    return 0.5 * jnp.sum(jnp.mean(err, axis=-1, keepdims=True), axis=0, keepdims=True)


def _proj_call(x, mod3, wp):
    bsz, t, _ = x.shape
    tm = min(256, t)

    def body(x_ref, mod_ref, w_ref, pg_ref, pq_ref, pd_ref):
        h = x_ref[0] * (1.0 + mod_ref[0, 1:2, :]) + mod_ref[0, 0:1, :]
        p = jnp.dot(h.astype(MXU), w_ref[...], preferred_element_type=F32)
        pg_ref[0] = p[:, :W_GLA]
        pq_ref[0] = p[:, W_GLA:W_GLA + W_GQKV]
        pd_ref[0] = p[:, W_GLA + W_GQKV:]

    tok = lambda w: pl.BlockSpec((1, tm, w), lambda b, i: (b, i, 0))
    return pl.pallas_call(
        body, name="proj", grid=(bsz, t // tm),
        in_specs=[tok(D), pl.BlockSpec((1, 3, D), lambda b, i: (b, 0, 0)), pl.BlockSpec((D, PW), lambda b, i: (0, 0))],
        out_specs=[tok(W_GLA), tok(W_GQKV), tok(W_GDN)],
        out_shape=[jax.ShapeDtypeStruct((bsz, t, w), F32) for w in (W_GLA, W_GQKV, W_GDN)],
        compiler_params=_params(("parallel", "parallel")),
    )(x, mod3, wp)


def _conv_fwd_call(pq, conv_w):
    bsz, t, _ = pq.shape
    tt = min(512, t)
    hb = tt // 8

    def body(x_ref, halo_ref, w_ref, o_ref, buf):
        i = pl.program_id(1)
        buf[0:8, :] = jnp.where(i > 0, halo_ref[0], 0.0)
        buf[8:, :] = x_ref[0]
        acc = w_ref[0:1, :] * buf[pl.ds(5, tt), :]
        for k in range(1, 4):
            acc = acc + w_ref[k:k + 1, :] * buf[pl.ds(5 + k, tt), :]
        o_ref[0] = acc

    return pl.pallas_call(
        body, name="conv_fwd", grid=(bsz, t // tt),
        in_specs=[pl.BlockSpec((1, tt, W_GQKV), lambda b, i: (b, i, 0)),
                  pl.BlockSpec((1, 8, W_GQKV), lambda b, i: (b, jnp.maximum(i * hb - 1, 0), 0)),
                  pl.BlockSpec((4, W_GQKV), lambda b, i: (0, 0))],
        out_specs=pl.BlockSpec((1, tt, W_GQKV), lambda b, i: (b, i, 0)),
        out_shape=jax.ShapeDtypeStruct(pq.shape, F32),
        scratch_shapes=[pltpu.VMEM((tt + 8, W_GQKV), F32)],
        compiler_params=_params(("parallel", "parallel")),
    )(pq, pq, conv_w)


def _conv_bwd_call(dconv, pq, conv_w):
    bsz, t, _ = pq.shape
    tt = min(512, t)
    hb = tt // 8
    nt_ = t // tt

    def body(d_ref, dnext_ref, x_ref, halo_ref, w_ref, din_ref, dw_ref, dbuf, xbuf):
        b, i = pl.program_id(0), pl.program_id(1)

        @pl.when((b == 0) & (i == 0))
        def _():
            dw_ref[...] = jnp.zeros_like(dw_ref)

        dbuf[0:tt, :] = d_ref[0]
        dbuf[tt:, :] = jnp.where(i < nt_ - 1, dnext_ref[0], 0.0)
        xbuf[0:8, :] = jnp.where(i > 0, halo_ref[0], 0.0)
        xbuf[8:, :] = x_ref[0]
        dout = d_ref[0]
        acc = w_ref[0:1, :] * dbuf[pl.ds(3, tt), :]
        dw_ref[0:1, :] += jnp.sum(dout * xbuf[pl.ds(5, tt), :], axis=0, keepdims=True)
        for k in range(1, 4):
            acc = acc + w_ref[k:k + 1, :] * dbuf[pl.ds(3 - k, tt), :]
            dw_ref[k:k + 1, :] += jnp.sum(dout * xbuf[pl.ds(5 + k, tt), :], axis=0, keepdims=True)
        din_ref[0] = acc

    tile = pl.BlockSpec((1, tt, W_GQKV), lambda b, i: (b, i, 0))
    return pl.pallas_call(
        body, name="conv_bwd", grid=(bsz, nt_),
        in_specs=[tile, pl.BlockSpec((1, 8, W_GQKV), lambda b, i: (b, jnp.minimum((i + 1) * hb, t // 8 - 1), 0)),
                  tile, pl.BlockSpec((1, 8, W_GQKV), lambda b, i: (b, jnp.maximum(i * hb - 1, 0), 0)),
                  pl.BlockSpec((4, W_GQKV), lambda b, i: (0, 0))],
        out_specs=[tile, pl.BlockSpec((8, W_GQKV), lambda b, i: (0, 0))],
        out_shape=[jax.ShapeDtypeStruct(pq.shape, F32), jax.ShapeDtypeStruct((8, W_GQKV), F32)],
        scratch_shapes=[pltpu.VMEM((tt + 8, W_GQKV), F32), pltpu.VMEM((tt + 8, W_GQKV), F32)],
        compiler_params=_params(("arbitrary", "arbitrary")),
    )(dconv, dconv, pq, pq, conv_w)


def _chunk_specs(nc, rev, bsz, cols):
    n_of = (lambda n: nc - 1 - n) if rev else (lambda n: n)
    return n_of, [pl.BlockSpec((bsz, CHUNK, w), lambda n, j=j: (0, n_of(n), j)) for w, j in cols]


def _full(shape):
    return pl.BlockSpec(shape, lambda n: (0,) * len(shape))


def _heads(ref, bsz, width, off=0):
    return jnp.stack([ref[b, :, off + width * h:off + width * (h + 1)] for b in range(bsz) for h in range(H)])


def _per_head(ref, bsz, width, rows=slice(None)):
    return jnp.stack([ref[rows, width * h:width * (h + 1)] for _ in range(bsz) for h in range(H)])


_GLA_COLS = [(256, 0), (256, 1), (512, 1), (512, 2), (128, 12)]


def _gla_args(refs, bsz):
    q_ref, k_ref, v_ref, og_ref, lr_ref, wgu_ref, bg_ref, nw_ref = refs
    lr = jnp.stack([lr_ref[b] for b in range(bsz) for _ in range(H)])
    return (_heads(q_ref, bsz, 64), _heads(k_ref, bsz, 64), _heads(v_ref, bsz, 128), lr, _heads(og_ref, bsz, 128),
            _per_head(wgu_ref, bsz, 64), _per_head(bg_ref, bsz, 64), nw_ref[...])


def _gla_fwd_call(pg, wgu, bg, nw):
    bsz, t, _ = pg.shape
    nc = t // CHUNK
    nh = bsz * H
    _, specs = _chunk_specs(nc, False, bsz, _GLA_COLS)

    def body(q_ref, k_ref, v_ref, og_ref, lr_ref, wgu_ref, bg_ref, nw_ref, y_ref, sh_ref, s_ref):
        @pl.when(pl.program_id(0) == 0)
        def _():
            s_ref[...] = jnp.zeros_like(s_ref)

        q, k, v, lr, og, w, b_, nw_ = _gla_args((q_ref, k_ref, v_ref, og_ref, lr_ref, wgu_ref, bg_ref, nw_ref), bsz)
        s = s_ref[...]
        sh_ref[0] = s
        y, s_new = gla_chunk(q, k, v, lr, og, s, w, b_, nw_)
        s_ref[...] = s_new
        for b in range(bsz):
            for h in range(H):
                y_ref[b, :, 128 * h:128 * h + 128] = y[H * b + h]

    return pl.pallas_call(
        body, name="gla_fwd", grid=(nc,),
        in_specs=specs + [_full((128, 256)), _full((1, 256)), _full((1, 128))],
        out_specs=[pl.BlockSpec((bsz, CHUNK, 512), lambda n: (0, n, 0)),
                   pl.BlockSpec((1, nh, GLA_DK, DV), lambda n: (n, 0, 0, 0))],
        out_shape=[jax.ShapeDtypeStruct((bsz, t, 512), F32), jax.ShapeDtypeStruct((nc, nh, GLA_DK, DV), F32)],
        scratch_shapes=[pltpu.VMEM((nh, GLA_DK, DV), F32)],
        compiler_params=_params(("arbitrary",)),
    )(pg, pg, pg, pg, pg, wgu, bg, nw)


def _gla_bwd_call(pg, s_hist, dyin, wgu, bg, nw):
    bsz, t, _ = pg.shape
    nc = t // CHUNK
    nh = bsz * H
    n_of, specs = _chunk_specs(nc, True, bsz, _GLA_COLS)

    def body(q_ref, k_ref, v_ref, og_ref, lr_ref, sh_ref, dy_ref, wgu_ref, bg_ref, nw_ref,
             dp_ref, dwgu_ref, dbg_ref, dnw_ref, ds_ref):
        @pl.when(pl.program_id(0) == 0)
        def _():
            dwgu_ref[...] = jnp.zeros_like(dwgu_ref)
            dbg_ref[...] = jnp.zeros_like(dbg_ref)
            dnw_ref[...] = jnp.zeros_like(dnw_ref)
            ds_ref[...] = jnp.zeros_like(ds_ref)

        q, k, v, lr, og, w, b_, nw_ = _gla_args((q_ref, k_ref, v_ref, og_ref, lr_ref, wgu_ref, bg_ref, nw_ref), bsz)
        _, vjp = jax.vjp(gla_chunk, q, k, v, lr, og, sh_ref[0], w, b_, nw_)
        dq, dk, dv, dlr, dog, ds, dwgu, dbg, dnw = vjp((_heads(dy_ref, bsz, 128), ds_ref[...]))
        ds_ref[...] = ds
        dnw_ref[...] += dnw
        for b in range(bsz):
            for h in range(H):
                i = H * b + h
                dp_ref[b, :, 64 * h:64 * h + 64] = dq[i]
                dp_ref[b, :, 256 + 64 * h:256 + 64 * h + 64] = dk[i]
                dp_ref[b, :, 512 + 128 * h:512 + 128 * h + 128] = dv[i]
                dp_ref[b, :, 1024 + 128 * h:1024 + 128 * h + 128] = dog[i]
                dwgu_ref[:, 64 * h:64 * h + 64] += dwgu[i]
                dbg_ref[:, 64 * h:64 * h + 64] += dbg[i]
            dp_ref[b, :, 1536:1664] = dlr[H * b] + dlr[H * b + 1] + dlr[H * b + 2] + dlr[H * b + 3]

    return pl.pallas_call(
        body, name="gla_bwd", grid=(nc,),
        in_specs=specs + [pl.BlockSpec((1, nh, GLA_DK, DV), lambda n: (n_of(n), 0, 0, 0)),
                          pl.BlockSpec((bsz, CHUNK, 512), lambda n: (0, n_of(n), 0)),
                          _full((128, 256)), _full((1, 256)), _full((1, 128))],
        out_specs=[pl.BlockSpec((bsz, CHUNK, W_GLA), lambda n: (0, n_of(n), 0)),
                   _full((128, 256)), _full((1, 256)), _full((1, 128))],
        out_shape=[jax.ShapeDtypeStruct(pg.shape, F32), jax.ShapeDtypeStruct((128, 256), F32),
                   jax.ShapeDtypeStruct((1, 256), F32), jax.ShapeDtypeStruct((1, 128), F32)],
        scratch_shapes=[pltpu.VMEM((nh, GLA_DK, DV), F32)],
        compiler_params=_params(("arbitrary",)),
    )(pg, pg, pg, pg, pg, s_hist, dyin, wgu, bg, nw)


_GDN_COLS = [(512, 0), (512, 1), (512, 2), (512, 0), (128, 4)]


def _gdn_args(refs, bsz):
    q_ref, k_ref, v_ref, og_ref, ab_ref, sc_ref, nw_ref = refs
    return (_heads(q_ref, bsz, 128), _heads(k_ref, bsz, 128), _heads(v_ref, bsz, 128), _heads(ab_ref, bsz, 1),
            _heads(ab_ref, bsz, 1, off=H), _heads(og_ref, bsz, 128), _per_head(sc_ref, bsz, 1, slice(0, 1)),
            _per_head(sc_ref, bsz, 1, slice(1, 2)), nw_ref[...])


def _gdn_fwd_call(conv, pd, sc, nw):
    bsz, t, _ = conv.shape
    nc = t // CHUNK
    nh = bsz * H
    _, specs = _chunk_specs(nc, False, bsz, _GDN_COLS)

    def body(q_ref, k_ref, v_ref, og_ref, ab_ref, sc_ref, nw_ref, y_ref, sh_ref, th_ref, s_ref):
        @pl.when(pl.program_id(0) == 0)
        def _():
            s_ref[...] = jnp.zeros_like(s_ref)

        q, k, v, a, bb, og, alog, dtb, nw_ = _gdn_args((q_ref, k_ref, v_ref, og_ref, ab_ref, sc_ref, nw_ref), bsz)
        s = s_ref[...]
        sh_ref[0] = s
        y, s_new, tinv = gdn_chunk(q, k, v, a, bb, og, s, alog, dtb, nw_)
        th_ref[0] = tinv
        s_ref[...] = s_new
        for b in range(bsz):
            for h in range(H):
                y_ref[b, :, 128 * h:128 * h + 128] = y[H * b + h]

    return pl.pallas_call(
        body, name="gdn_fwd", grid=(nc,),
        in_specs=specs + [_full((2, 128)), _full((1, 128))],
        out_specs=[pl.BlockSpec((bsz, CHUNK, 512), lambda n: (0, n, 0)),
                   pl.BlockSpec((1, nh, DV, DV), lambda n: (n, 0, 0, 0)),
                   pl.BlockSpec((1, nh, CHUNK, CHUNK), lambda n: (n, 0, 0, 0))],
        out_shape=[jax.ShapeDtypeStruct((bsz, t, 512), F32), jax.ShapeDtypeStruct((nc, nh, DV, DV), F32),
                   jax.ShapeDtypeStruct((nc, nh, CHUNK, CHUNK), F32)],
        scratch_shapes=[pltpu.VMEM((nh, DV, DV), F32)],
        compiler_params=_params(("arbitrary",)),
    )(conv, conv, conv, pd, pd, sc, nw)


def _gdn_bwd_call(conv, pd, s_hist, t_hist, dyin, sc, nw):
    bsz, t, _ = conv.shape
    nc = t // CHUNK
    nh = bsz * H
    n_of, specs = _chunk_specs(nc, True, bsz, _GDN_COLS)

    def body(q_ref, k_ref, v_ref, og_ref, ab_ref, sh_ref, th_ref, dy_ref, sc_ref, nw_ref,
             dc_ref, dpd_ref, dsc_ref, dnw_ref, ds_ref):
        @pl.when(pl.program_id(0) == 0)
        def _():
            dsc_ref[...] = jnp.zeros_like(dsc_ref)
            dnw_ref[...] = jnp.zeros_like(dnw_ref)
            ds_ref[...] = jnp.zeros_like(ds_ref)

        q, k, v, a, bb, og, alog, dtb, nw_ = _gdn_args((q_ref, k_ref, v_ref, og_ref, ab_ref, sc_ref, nw_ref), bsz)
        _, vjp = jax.vjp(functools.partial(gdn_chunk, tinv=th_ref[0]), q, k, v, a, bb, og, sh_ref[0], alog, dtb, nw_)
        dq, dk, dv, da, db, dog, ds, dalog, ddtb, dnw = vjp((_heads(dy_ref, bsz, 128), ds_ref[...]))
        ds_ref[...] = ds
        dnw_ref[...] += dnw
        for b in range(bsz):
            dpd_ref[b, :, 512:640] = jnp.zeros((CHUNK, 128), F32)
            for h in range(H):
                i = H * b + h
                dc_ref[b, :, 128 * h:128 * h + 128] = dq[i]
                dc_ref[b, :, 512 + 128 * h:512 + 128 * h + 128] = dk[i]
                dc_ref[b, :, 1024 + 128 * h:1024 + 128 * h + 128] = dv[i]
                dpd_ref[b, :, 128 * h:128 * h + 128] = dog[i]
                dpd_ref[b, :, 512 + h:513 + h] = da[i]
                dpd_ref[b, :, 516 + h:517 + h] = db[i]
                dsc_ref[0:1, h:h + 1] += dalog[i]
                dsc_ref[1:2, h:h + 1] += ddtb[i]

    return pl.pallas_call(
        body, name="gdn_bwd", grid=(nc,),
        in_specs=specs + [pl.BlockSpec((1, nh, DV, DV), lambda n: (n_of(n), 0, 0, 0)),
                          pl.BlockSpec((1, nh, CHUNK, CHUNK), lambda n: (n_of(n), 0, 0, 0)),
                          pl.BlockSpec((bsz, CHUNK, 512), lambda n: (0, n_of(n), 1)),
                          _full((2, 128)), _full((1, 128))],
        out_specs=[pl.BlockSpec((bsz, CHUNK, W_GQKV), lambda n: (0, n_of(n), 0)),
                   pl.BlockSpec((bsz, CHUNK, W_GDN), lambda n: (0, n_of(n), 0)),
                   _full((2, 128)), _full((1, 128))],
        out_shape=[jax.ShapeDtypeStruct(conv.shape, F32), jax.ShapeDtypeStruct(pd.shape, F32),
                   jax.ShapeDtypeStruct((2, 128), F32), jax.ShapeDtypeStruct((1, 128), F32)],
        scratch_shapes=[pltpu.VMEM((nh, DV, DV), F32)],
        compiler_params=_params(("arbitrary",)),
    )(conv, conv, conv, pd, pd, s_hist, t_hist, dyin, sc, nw)


def _head_call(x, ya, yb, wout, mod3, lnw, lnb, tgt):
    bsz, t, _ = x.shape
    tm = min(256, t)

    def body(x_ref, ya_ref, yb_ref, w_ref, mod_ref, lnw_ref, lnb_ref, t_ref,
             dyin_ref, dxa_ref, dgate_ref, dw_ref, dlnw_ref, dlnb_ref, loss_ref):
        b, i = pl.program_id(0), pl.program_id(1)

        @pl.when((b == 0) & (i == 0))
        def _():
            dw_ref[...] = jnp.zeros_like(dw_ref)
            dlnw_ref[...] = jnp.zeros_like(dlnw_ref)
            dlnb_ref[...] = jnp.zeros_like(dlnb_ref)
            loss_ref[...] = jnp.zeros_like(loss_ref)

        @pl.when(i == 0)
        def _():
            dgate_ref[...] = jnp.zeros_like(dgate_ref)

        yin = jnp.concatenate([ya_ref[0], yb_ref[0]], axis=-1).astype(MXU)
        w = w_ref[...]
        y = jnp.dot(yin, w, preferred_element_type=F32)
        loss, vjp = jax.vjp(head_fn, x_ref[0], y, mod_ref[0, 2:3, :], lnw_ref[...], lnb_ref[...], t_ref[0])
        dx, dy, dgate, dlnw, dlnb, _ = vjp(jnp.ones((1, 1), F32))
        dyb = dy.astype(MXU)
        dyin_ref[0] = lax.dot_general(dyb, w, (((1,), (1,)), ((), ())), preferred_element_type=F32)
        dw_ref[...] += lax.dot_general(yin, dyb, (((0,), (0,)), ((), ())), preferred_element_type=F32)
        dxa_ref[0] = dx
        dgate_ref[0] += dgate
        dlnw_ref[...] += dlnw
        dlnb_ref[...] += dlnb
        loss_ref[...] += jnp.broadcast_to(loss, (1, 128))

    tok = lambda w, j=0: pl.BlockSpec((1, tm, w), lambda b, i: (b, i, j))
    row = pl.BlockSpec((1, D), lambda b, i: (0, 0))
    return pl.pallas_call(
        body, name="head", grid=(bsz, t // tm),
        in_specs=[tok(D), tok(512), tok(512), pl.BlockSpec((D, D), lambda b, i: (0, 0)),
                  pl.BlockSpec((1, 3, D), lambda b, i: (b, 0, 0)), row, row, tok(D)],
        out_specs=[tok(D), tok(D), pl.BlockSpec((1, 1, D), lambda b, i: (b, 0, 0)),
                   pl.BlockSpec((D, D), lambda b, i: (0, 0)), row, row, pl.BlockSpec((1, 128), lambda b, i: (0, 0))],
        out_shape=[jax.ShapeDtypeStruct(x.shape, F32), jax.ShapeDtypeStruct(x.shape, F32),
                   jax.ShapeDtypeStruct((bsz, 1, D), F32), jax.ShapeDtypeStruct((D, D), F32),
                   jax.ShapeDtypeStruct((1, D), F32), jax.ShapeDtypeStruct((1, D), F32),
                   jax.ShapeDtypeStruct((1, 128), F32)],
        compiler_params=_params(("arbitrary", "arbitrary")),
    )(x, ya, yb, wout, mod3, lnw, lnb, tgt)


def _dh_call(dpg, dpq, dpd, wp, x, mod3, dxa):
    bsz, t, _ = x.shape
    tm = min(256, t)

    def body(dg_ref, dq_ref, dd_ref, w_ref, x_ref, mod_ref, dxa_ref, gx_ref, dmod_ref):
        @pl.when(pl.program_id(1) == 0)
        def _():
            dmod_ref[...] = jnp.zeros_like(dmod_ref)

        nt = lambda a, lo, hi: lax.dot_general(a.astype(MXU), w_ref[:, lo:hi], (((1,), (1,)), ((), ())),
                                               preferred_element_type=F32)
        dh = nt(dg_ref[0], 0, W_GLA) + nt(dq_ref[0], W_GLA, W_GLA + W_GQKV) + nt(dd_ref[0], W_GLA + W_GQKV, PW)
        gx_ref[0] = dh * (1.0 + mod_ref[0, 1:2, :]) + dxa_ref[0]
        dmod_ref[0, 0:1, :] += jnp.sum(dh, axis=0, keepdims=True)
        dmod_ref[0, 1:2, :] += jnp.sum(dh * x_ref[0], axis=0, keepdims=True)

    tok = lambda w: pl.BlockSpec((1, tm, w), lambda b, i: (b, i, 0))
    return pl.pallas_call(
        body, name="dh", grid=(bsz, t // tm),
        in_specs=[tok(W_GLA), tok(W_GQKV), tok(W_GDN), pl.BlockSpec((D, PW), lambda b, i: (0, 0)), tok(D),
                  pl.BlockSpec((1, 3, D), lambda b, i: (b, 0, 0)), tok(D)],
        out_specs=[tok(D), pl.BlockSpec((1, 2, D), lambda b, i: (b, 0, 0))],
        out_shape=[jax.ShapeDtypeStruct(x.shape, F32), jax.ShapeDtypeStruct((bsz, 2, D), F32)],
        compiler_params=_params(("parallel", "arbitrary")),
    )(dpg, dpq, dpd, wp, x, mod3, dxa)


def _dw_call(x, mod3, dp):
    bsz, t, _ = x.shape
    width = dp.shape[-1]
    tm = min(512, t)

    def body(x_ref, mod_ref, dp_ref, dw_ref):
        @pl.when((pl.program_id(0) == 0) & (pl.program_id(1) == 0))
        def _():
            dw_ref[...] = jnp.zeros_like(dw_ref)

        h = x_ref[0] * (1.0 + mod_ref[0, 1:2, :]) + mod_ref[0, 0:1, :]
        dw_ref[...] += lax.dot_general(h.astype(MXU), dp_ref[0].astype(MXU), (((0,), (0,)), ((), ())),
                                       preferred_element_type=F32)

    tok = lambda w: pl.BlockSpec((1, tm, w), lambda b, i: (b, i, 0))
    return pl.pallas_call(
        body, name=f"dw{width}", grid=(bsz, t // tm),
        in_specs=[tok(D), pl.BlockSpec((1, 3, D), lambda b, i: (b, 0, 0)), tok(width)],
        out_specs=pl.BlockSpec((D, width), lambda b, i: (0, 0)),
        out_shape=jax.ShapeDtypeStruct((D, width), F32),
        compiler_params=_params(("arbitrary", "arbitrary")),
    )(x, mod3, dp)


def _adamw(w, g, m, v):
    m = ADAM_B1 * m + (1.0 - ADAM_B1) * g
    v = ADAM_B2 * v + (1.0 - ADAM_B2) * jnp.square(g)
    m_hat = m / (1.0 - ADAM_B1 ** ADAM_STEP)
    v_hat = v / (1.0 - ADAM_B2 ** ADAM_STEP)
    delta = -ADAM_LR * (m_hat / (jnp.sqrt(v_hat) + ADAM_EPS) + ADAM_WD * w)
    return delta, m, v


def _sum8(ref):
    g = ref[0].astype(F32)
    for j in range(1, NDEV):
        g = g + ref[j].astype(F32)
    return g


def _adam_sum_call(name, g8, w, m, v, rows):
    r, c = w.shape

    def body(g_ref, w_ref, m_ref, v_ref, go_ref, d_ref, mo_ref, vo_ref):
        g = _sum8(g_ref)
        go_ref[...] = g
        d_ref[...], mo_ref[...], vo_ref[...] = _adamw(w_ref[...], g, m_ref[...], v_ref[...])

    blk = pl.BlockSpec((rows, c), lambda i: (i, 0))
    return pl.pallas_call(
        body, name=name, grid=(r // rows,),
        in_specs=[pl.BlockSpec((NDEV, rows, c), lambda i: (0, i, 0)), blk, blk, blk],
        out_specs=[blk] * 4, out_shape=[jax.ShapeDtypeStruct((r, c), F32)] * 4,
        compiler_params=_params(("parallel",)),
    )(g8, w, m, v)


def _adam_ada_call(c_all, dmod_cols, w, m, v):
    def body(c_ref, dm_ref, w_ref, m_ref, v_ref, go_ref, d_ref, mo_ref, vo_ref):
        g = lax.dot_general(c_ref[...].astype(MXU), dm_ref[...].astype(MXU), (((0,), (0,)), ((), ())),
                            preferred_element_type=F32)
        go_ref[...] = g
        d_ref[...], mo_ref[...], vo_ref[...] = _adamw(w_ref[...], g, m_ref[...], v_ref[...])

    return pl.pallas_call(
        body, name="adam_ada", out_shape=[jax.ShapeDtypeStruct(w.shape, F32)] * 4, compiler_params=_params(),
    )(c_all, dmod_cols, w, m, v)


def _adam_small_call(g, w, m, v):
    def body(g_ref, w_ref, m_ref, v_ref, d_ref, mo_ref, vo_ref):
        d_ref[...], mo_ref[...], vo_ref[...] = _adamw(w_ref[...], g_ref[...], m_ref[...], v_ref[...])

    return pl.pallas_call(
        body, name="adam_small", out_shape=[jax.ShapeDtypeStruct(w.shape, F32)] * 3, compiler_params=_params(),
    )(g, w, m, v)


def _small_sum_call(sp_all):
    def body(sp_ref, dmod_ref, sum_ref):
        acc = sp_ref[0, 2:3, :]
        for j in range(1, NDEV):
            acc = acc + sp_ref[j, 2:3, :]
        sum_ref[...] = acc
        for j in range(NDEV):
            dmod_ref[2 * j:2 * j + 2, :] = sp_ref[j, 0:2, :]

    return pl.pallas_call(
        body, name="small_sum", out_shape=[jax.ShapeDtypeStruct((2 * NDEV, SPW), F32), jax.ShapeDtypeStruct((1, SPW), F32)],
        compiler_params=_params(),
    )(sp_all)


def _mesh_pos():
    x, y, c = lax.axis_index("x"), lax.axis_index("y"), lax.axis_index("c")
    return x, y, c, 4 * x + 2 * y + c


def _peer(x, y, c, k):
    px = 1 - x if k & 4 else x
    py = 1 - y if k & 2 else y
    pc = 1 - c if k & 1 else c
    return (px, py, pc), 4 * px + 2 * py + pc


_ANY = pl.BlockSpec(memory_space=pl.ANY)
_VMEM = pl.BlockSpec(memory_space=pltpu.VMEM)


def _gather_call(c8, w_ada, b_sh, w_in, w_out, conv_w, wgu):
    big = [w_in, w_out, conv_w, wgu]
    nb = len(big)

    def body(c_ref, wada_ref, b_ref, *rest):
        srcs, outs = rest[:nb], rest[nb:2 * nb]
        call_ref, mod_ref, modp, send_sems, recv_sems, loc_sems = rest[2 * nb:]
        x, y, c, me = _mesh_pos()

        def remote(src, dst, a, k, dev):
            return pltpu.make_async_remote_copy(src_ref=src, dst_ref=dst, send_sem=send_sems.at[a, k],
                                                recv_sem=recv_sems.at[a, k], device_id=dev,
                                                device_id_type=pl.DeviceIdType.MESH)

        sends = []
        call_ref[me] = c_ref[...]
        for k in range(1, NDEV):
            dev, _ = _peer(x, y, c, k)
            sends.append(remote(c_ref, call_ref.at[me], nb, k, dev))
            sends[-1].start()
        local = [pltpu.make_async_copy(srcs[a], outs[a].at[me], loc_sems.at[a]) for a in range(nb)]
        for cp in local:
            cp.start()
        for k in range(1, NDEV):
            dev, _ = _peer(x, y, c, k)
            for a in range(nb):
                sends.append(remote(srcs[a], outs[a].at[me], a, k, dev))
                sends[-1].start()
        for k in range(1, NDEV):
            dev, pidx = _peer(x, y, c, k)
            remote(c_ref, call_ref.at[pidx], nb, k, dev).wait_recv()
        modp[...] = jnp.dot(call_ref[...].reshape(NDEV * 8, D).astype(MXU), wada_ref[...].astype(MXU),
                            preferred_element_type=F32) + b_ref[...]
        mod_ref[me] = modp[pl.ds(pl.multiple_of(me * 8, 8), 8), :]
        for k in range(1, NDEV):
            dev, pidx = _peer(x, y, c, k)
            sends.append(remote(modp.at[pl.ds(pl.multiple_of(pidx * 8, 8), 8), :], mod_ref.at[me], nb + 1, k, dev))
            sends[-1].start()
        for k in range(1, NDEV):
            dev, pidx = _peer(x, y, c, k)
            for a in range(nb):
                remote(srcs[a], outs[a].at[pidx], a, k, dev).wait_recv()
            remote(modp.at[pl.ds(0, 8), :], mod_ref.at[pidx], nb + 1, k, dev).wait_recv()
        for cp in sends:
            cp.wait_send()
        for cp in local:
            cp.wait()

    out_shape = [jax.ShapeDtypeStruct((NDEV,) + a.shape, a.dtype) for a in big]
    out_shape += [jax.ShapeDtypeStruct((NDEV, 8, D), F32), jax.ShapeDtypeStruct((NDEV, 8, SHARD_ADA), F32)]
    return pl.pallas_call(
        body, name="gather", out_shape=out_shape,
        in_specs=[_VMEM, _VMEM, _VMEM] + [_ANY] * nb,
        out_specs=[_ANY] * nb + [_VMEM, _VMEM],
        scratch_shapes=[pltpu.VMEM((NDEV * 8, SHARD_ADA), F32), pltpu.SemaphoreType.DMA((nb + 2, NDEV)),
                        pltpu.SemaphoreType.DMA((nb + 2, NDEV)), pltpu.SemaphoreType.DMA((nb,))],
        compiler_params=_params(),
    )(c8, w_ada, b_sh, *big)


def _reduce_call(blocks, sp):
    nb = len(blocks)

    def body(sp_ref, *rest):
        srcs, outs = rest[:nb], rest[nb:2 * nb]
        spall_ref, send_sems, recv_sems, loc_sems = rest[2 * nb:]
        x, y, c, me = _mesh_pos()

        def remote(src, dst, a, k, dev):
            return pltpu.make_async_remote_copy(src_ref=src, dst_ref=dst, send_sem=send_sems.at[a, k],
                                                recv_sem=recv_sems.at[a, k], device_id=dev,
                                                device_id_type=pl.DeviceIdType.MESH)

        sends = []
        spall_ref[me] = sp_ref[...]
        local = [pltpu.make_async_copy(srcs[a].at[me], outs[a].at[me], loc_sems.at[a]) for a in range(nb)]
        for cp in local:
            cp.start()
        for k in range(1, NDEV):
            dev, pidx = _peer(x, y, c, k)
            sends.append(remote(sp_ref, spall_ref.at[me], nb, k, dev))
            sends[-1].start()
            for a in range(nb):
                sends.append(remote(srcs[a].at[pidx], outs[a].at[me], a, k, dev))
                sends[-1].start()
        for k in range(1, NDEV):
            dev, pidx = _peer(x, y, c, k)
            remote(sp_ref, spall_ref.at[pidx], nb, k, dev).wait_recv()
            for a in range(nb):
                remote(srcs[a].at[pidx], outs[a].at[pidx], a, k, dev).wait_recv()
        for cp in sends:
            cp.wait_send()
        for cp in local:
            cp.wait()

    return pl.pallas_call(
        body, name="reduce", out_shape=[jax.ShapeDtypeStruct(a.shape, a.dtype) for a in blocks]
        + [jax.ShapeDtypeStruct((NDEV, 8, SPW), F32)],
        in_specs=[_VMEM] + [_ANY] * nb, out_specs=[_ANY] * nb + [_VMEM],
        scratch_shapes=[pltpu.SemaphoreType.DMA((nb + 1, NDEV)), pltpu.SemaphoreType.DMA((nb + 1, NDEV)),
                        pltpu.SemaphoreType.DMA((nb,))],
        compiler_params=_params(),
    )(sp, *blocks)


def _pad_cols(a, n):
    return jnp.pad(a, ((0, 0), (0, n - a.shape[1])))


def _assemble_w(w_full):
    q, k, v, lr, og, gqkv, a, b, dog = jnp.split(w_full, [256, 512, 1024, 1040, 1552, 3088, 3092, 3096], axis=1)
    z = lambda n: jnp.zeros((w_full.shape[0], n), w_full.dtype)
    return jnp.concatenate([q, k, v, og, lr, z(112), gqkv, dog, a, b, z(120)], axis=1)


def _disassemble_dw(dw_gla, dw_gqkv, dw_gdn):
    return jnp.concatenate([dw_gla[:, :1024], dw_gla[:, 1536:1552], dw_gla[:, 1024:1536], dw_gqkv,
                            dw_gdn[:, 512:520], dw_gdn[:, :512]], axis=1)


def local_step(x, mod3, wp, wout, conv_w, wgu_p, bg, gla_nw, sc, gdn_nw, lnw, lnb, tgt):
    pg, pq, pd = _proj_call(x, mod3, wp)
    conv = _conv_fwd_call(pq, conv_w)
    ya, s_gla = _gla_fwd_call(pg, wgu_p, bg, gla_nw)
    yb, s_gdn, t_gdn = _gdn_fwd_call(conv, pd, sc, gdn_nw)
    dyin, dxa, dgate, dwout, dlnw, dlnb, loss = _head_call(x, ya, yb, wout, mod3, lnw, lnb, tgt)
    dpg, dwgu, dbg, dnw_gla = _gla_bwd_call(pg, s_gla, dyin, wgu_p, bg, gla_nw)
    dconv, dpd, dsc, dnw_gdn = _gdn_bwd_call(conv, pd, s_gdn, t_gdn, dyin, sc, gdn_nw)
    dpq, dconv_w = _conv_bwd_call(dconv, pq, conv_w)
    gx, dmod2 = _dh_call(dpg, dpq, dpd, wp, x, mod3, dxa)
    dw_in = _disassemble_dw(_dw_call(x, mod3, dpg), _dw_call(x, mod3, dpq), _dw_call(x, mod3, dpd))
    dmod = jnp.concatenate([dmod2, dgate], axis=1)
    return dict(loss=loss[0, 0], gx=gx, dmod=dmod, dw_in=dw_in, dwout=dwout, dconv_w=dconv_w[:4], dwgu=dwgu[:16],
                dbg=dbg, dnw_gla=dnw_gla, dalog=dsc[0:1, :4], ddtb=dsc[1:2, :4], dnw_gdn=dnw_gdn, dlnw=dlnw, dlnb=dlnb)


def kernel(x, c, w_ada, b_ada, w_in, gla_w_gate_up, gla_b_gate, gla_norm_w, gdn_conv_w, gdn_a_log, gdn_dt_bias, gdn_norm_w, w_out, ln_w, ln_b, loss_target, m_w_ada, m_b_ada, m_w_in, m_gla_w_gate_up, m_gla_b_gate, m_gla_norm_w, m_gdn_conv_w, m_gdn_a_log, m_gdn_dt_bias, m_gdn_norm_w, m_w_out, m_ln_w, m_ln_b, v_w_ada, v_b_ada, v_w_in, v_gla_w_gate_up, v_gla_b_gate, v_gla_norm_w, v_gdn_conv_w, v_gdn_a_log, v_gdn_dt_bias, v_gdn_norm_w, v_w_out, v_ln_w, v_ln_b):
    me = 4 * lax.axis_index("x") + 2 * lax.axis_index("y") + lax.axis_index("c")
    bsz = x.shape[0]

    b_sh = lax.dynamic_slice(b_ada, (0, me * SHARD_ADA), (1, SHARD_ADA))
    c8 = jnp.pad(c, ((0, 8 - bsz), (0, 0)))
    win_all, wout_all, conv_all, wgu_all, c_all, mod_blk = _gather_call(
        c8, w_ada[0], b_sh, w_in[0].astype(MXU), w_out[0].astype(MXU), gdn_conv_w[0], gla_w_gate_up[0])
    wp = _assemble_w(jnp.transpose(win_all, (1, 0, 2)).reshape(D, IN_COLS))
    wout = wout_all.reshape(D, D)
    conv_w = jnp.transpose(conv_all, (1, 0, 2)).reshape(4, W_GQKV)
    wgu_p = jnp.pad(jnp.transpose(wgu_all, (1, 0, 2)).reshape(16, 256), ((0, 112), (0, 0)))
    mod = jnp.transpose(mod_blk[:, :bsz, :], (1, 0, 2)).reshape(bsz, 3 * D)
    mod3 = mod.reshape(bsz, 3, D)
    sc = jnp.concatenate([_pad_cols(gdn_a_log, 128), _pad_cols(gdn_dt_bias, 128)], axis=0)

    g = local_step(x, mod3, wp, wout, conv_w, wgu_p, gla_b_gate, gla_norm_w, sc, gdn_norm_w, ln_w, ln_b, loss_target)

    small = jnp.concatenate([g["dlnw"], g["dlnb"], g["dbg"], g["dnw_gla"], g["dnw_gdn"], _pad_cols(g["dalog"], 128),
                             _pad_cols(g["ddtb"], 128), jnp.full((1, 128), g["loss"], F32), jnp.zeros((1, 128), F32)], axis=1)
    sp = jnp.concatenate([g["dmod"].reshape(bsz, SPW), small, jnp.zeros((8 - bsz - 1, SPW), F32)], axis=0)
    blocks = [jnp.transpose(g["dw_in"].reshape(D, NDEV, SHARD_IN), (1, 0, 2)).astype(WIRE),
              g["dwout"].reshape(NDEV, D // NDEV, D).astype(WIRE),
              jnp.transpose(g["dconv_w"].reshape(4, NDEV, W_GQKV // NDEV), (1, 0, 2)),
              jnp.transpose(g["dwgu"].reshape(16, NDEV, 256 // NDEV), (1, 0, 2))]
    r_in, r_out, r_conv, r_gu, sp_all = _reduce_call(blocks, sp)
    dmod_all, sums = _small_sum_call(sp_all)

    g_in, d_in, nm_in, nv_in = _adam_sum_call("adam_in", r_in, w_in[0], m_w_in[0], v_w_in[0], 256)
    g_out, d_out, nm_out, nv_out = _adam_sum_call("adam_out", r_out, w_out[0], m_w_out[0], v_w_out[0], D // NDEV)
    g_conv, d_conv, nm_conv, nv_conv = _adam_sum_call("adam_conv", r_conv, gdn_conv_w[0], m_gdn_conv_w[0], v_gdn_conv_w[0], 4)
    g_gu, d_gu, nm_gu, nv_gu = _adam_sum_call("adam_gu", r_gu, gla_w_gate_up[0], m_gla_w_gate_up[0], v_gla_w_gate_up[0], 16)
    c16 = c_all[:, :bsz, :].reshape(NDEV * bsz, D)
    g_ada, d_ada, nm_ada, nv_ada = _adam_ada_call(c16, lax.dynamic_slice(dmod_all, (0, me * SHARD_ADA), (NDEV * bsz, SHARD_ADA)),
                                                  w_ada[0], m_w_ada[0], v_w_ada[0])

    def pack(b_a, lw, lb, bgt, n1, n2, al, dt):
        return jnp.concatenate([b_a, lw, lb, bgt, n1, n2, _pad_cols(al, 128), _pad_cols(dt, 128)], axis=1).reshape(-1, 128)

    g_small = _bada_and_pack(dmod_all, sums)
    w_s = pack(b_ada, ln_w, ln_b, gla_b_gate, gla_norm_w, gdn_norm_w, gdn_a_log, gdn_dt_bias)
    m_s = pack(m_b_ada, m_ln_w, m_ln_b, m_gla_b_gate, m_gla_norm_w, m_gdn_norm_w, m_gdn_a_log, m_gdn_dt_bias)
    v_s = pack(v_b_ada, v_ln_w, v_ln_b, v_gla_b_gate, v_gla_norm_w, v_gdn_norm_w, v_gdn_a_log, v_gdn_dt_bias)
    d_s, nm_s, nv_s = _adam_small_call(g_small, w_s, m_s, v_s)

    def unpack(p):
        f = p.reshape(1, -1)
        b_a, lw, lb, bgt, n1, n2, al, dt = jnp.split(f, [3072, 4096, 5120, 5376, 5504, 5632, 5760], axis=1)
        return dict(b_ada=b_a, ln_w=lw, ln_b=lb, b_gate=bgt, gla_nw=n1, gdn_nw=n2, a_log=al[:, :4], dt_bias=dt[:, :4])

    gs, ds, ms, vs = unpack(g_small), unpack(d_s), unpack(nm_s), unpack(nv_s)
    loss = sums[0, SMALL_W]

    def group(t_ada, t_in, t_gu, t_conv, t_out, s):
        return [t_ada[None], s["b_ada"], t_in[None], t_gu[None], s["b_gate"], s["gla_nw"], t_conv[None], s["a_log"],
                s["dt_bias"], s["gdn_nw"], t_out[None], s["ln_w"], s["ln_b"]]

    return (loss, g["gx"], *group(g_ada, g_in, g_gu, g_conv, g_out, gs), *group(d_ada, d_in, d_gu, d_conv, d_out, ds),
            *group(nm_ada, nm_in, nm_gu, nm_conv, nm_out, ms), *group(nv_ada, nv_in, nv_gu, nv_conv, nv_out, vs))


def _bada_and_pack(dmod_all, sums):
    n = dmod_all.shape[0]

    def body(dm_ref, s_ref, o_ref):
        acc = dm_ref[0:1, :]
        for j in range(1, n):
            acc = acc + dm_ref[j:j + 1, :]
        o_ref[:, 0:SPW] = acc
        o_ref[:, SPW:SPW + SMALL_W] = s_ref[:, 0:SMALL_W]

    packed = pl.pallas_call(
        body, name="bada_pack", out_shape=jax.ShapeDtypeStruct((1, SPW + SMALL_W), F32), compiler_params=_params(),
    )(dmod_all, sums)
    return packed.reshape(-1, 128)
```

```python
import functools

import jax
import jax.numpy as jnp
from jax import lax
from jax.experimental import pallas as pl
from jax.experimental.pallas import tpu as pltpu

F32 = jnp.float32
MXU = jnp.bfloat16
WIRE = jnp.bfloat16
HI = lax.Precision.HIGH

D = 1024
NDEV = 8
H = 4
GLA_DK = 64
DV = 128
CHUNK = 64
LN_EPS = 1e-5
RMS_EPS = 1e-6
ALPHA = 2.0 ** 0.25
GATE_NORM = 16.0

W_GLA, W_GQKV, W_GDN = 1664, 1536, 640
PW = W_GLA + W_GQKV + W_GDN
IN_COLS = 3608
SHARD_IN = IN_COLS // NDEV
SHARD_ADA = 3 * D // NDEV
SPW = 3 * D
SMALL_W = 2816

ADAM_LR, ADAM_B1, ADAM_B2, ADAM_EPS, ADAM_WD, ADAM_STEP = 0.001, 0.9, 0.999, 1e-08, 0.01, 10

VMEM_LIMIT = 56 * 1024 * 1024


def _params(sem=None, **kw):
    if sem is not None:
        kw["dimension_semantics"] = sem
    return pltpu.CompilerParams(vmem_limit_bytes=VMEM_LIMIT, **kw)


_MM = (((2,), (1,)), ((0,), (0,)))
_NT = (((2,), (2,)), ((0,), (0,)))
_TN = (((1,), (1,)), ((0,), (0,)))


def _dg(a, b, dims):
    return lax.dot_general(a.astype(MXU), b.astype(MXU), dims, preferred_element_type=F32)


def _hdg(a, b, dims):
    return lax.dot_general(a, b, dims, precision=HI, preferred_element_type=F32)


@jax.custom_vjp
def bmm(a, b):
    return _dg(a, b, _MM)


bmm.defvjp(lambda a, b: (_dg(a, b, _MM), (a, b)), lambda r, g: (_dg(g, r[1], _NT), _dg(r[0], g, _TN)))


@jax.custom_vjp
def bnt(a, b):
    return _dg(a, b, _NT)


bnt.defvjp(lambda a, b: (_dg(a, b, _NT), (a, b)), lambda r, g: (_dg(g, r[1], _MM), _dg(g, r[0], _TN)))


@jax.custom_vjp
def btn(a, b):
    return _dg(a, b, _TN)


btn.defvjp(lambda a, b: (_dg(a, b, _TN), (a, b)), lambda r, g: (_dg(r[1], g, _NT), _dg(r[0], g, _MM)))


def unit_lower_inverse(a):
    n = a.shape[-1]
    r, c = _iotas(n)
    p = -a
    t = (r == c).astype(F32) + p
    for _ in range(5):
        p = _hdg(p, p, _MM)
        t = t + _hdg(t, p, _MM)
    return t


@jax.custom_vjp
def unit_lower_solve(a, t, r1, r2):
    return _hdg(t, r1, _MM), _hdg(t, r2, _MM)


def _solve_fwd(a, t, r1, r2):
    s1, s2 = _hdg(t, r1, _MM), _hdg(t, r2, _MM)
    return (s1, s2), (t, s1, s2)


def _solve_bwd(res, g):
    t, s1, s2 = res
    d1, d2 = _hdg(t, g[0], _TN), _hdg(t, g[1], _TN)
    return -(_hdg(d1, s1, _NT) + _hdg(d2, s2, _NT)), jnp.zeros_like(t), d1, d2


unit_lower_solve.defvjp(_solve_fwd, _solve_bwd)


def _iotas(n):
    return lax.broadcasted_iota(jnp.int32, (n, n), 0), lax.broadcasted_iota(jnp.int32, (n, n), 1)


def _col_to_row(col, eye):
    return jnp.sum(jnp.where(eye, col, 0.0), axis=1, keepdims=True)


def _row_to_col(row, eye):
    return jnp.sum(jnp.where(eye, row, 0.0), axis=2, keepdims=True)


def _pick_row(m, i):
    r = lax.broadcasted_iota(jnp.int32, m.shape, 1)
    return jnp.sum(jnp.where(r == i, m, 0.0), axis=1, keepdims=True)


def _rms_gate(o, nw, og):
    on = o * lax.rsqrt(jnp.mean(o * o, axis=-1, keepdims=True) + RMS_EPS) * nw
    return on * jax.nn.silu(og)


def gla_chunk(q, k, v, lr, og, s, wgu, bg, nw):
    n, c, _ = q.shape
    r, cc = _iotas(c)
    causal = r >= cc
    qs = q * (GLA_DK ** -0.5)
    z = bmm(lr, wgu) + bg
    g = jax.nn.log_sigmoid(z) / GATE_NORM
    b = _hdg(jnp.broadcast_to(causal.astype(F32), (n, c, c)), g, _MM)
    bref = _pick_row(b, c // 2 - 1)
    blast = _pick_row(b, c - 1)
    att = jnp.where(causal, bnt(qs * jnp.exp(b - bref), k * jnp.exp(bref - b)), 0.0)
    o = bmm(att, v) + bmm(qs * jnp.exp(b), s)
    rk, ck = _iotas(GLA_DK)
    s_new = _row_to_col(jnp.exp(blast), rk == ck) * s + btn(k * jnp.exp(blast - b), v)
    return _rms_gate(o, nw, og), s_new


def gdn_chunk(cq, ck, cv, a, bb, og, s, alog, dtb, nw, tinv=None):
    c = cq.shape[1]
    r, cc = _iotas(c)
    eye, causal, strict = r == cc, r >= cc, r > cc
    q, k, v = jax.nn.silu(cq), jax.nn.silu(ck), jax.nn.silu(cv)
    q = q * lax.rsqrt(jnp.sum(q * q, axis=-1, keepdims=True) + RMS_EPS) * (DV ** -0.5)
    k = k * lax.rsqrt(jnp.sum(k * k, axis=-1, keepdims=True) + RMS_EPS)
    g = -jnp.exp(alog) * jax.nn.softplus(a + dtb)
    beta = jax.nn.sigmoid(bb)
    d = jnp.sum(jnp.where(causal, _col_to_row(g, eye), 0.0), axis=2, keepdims=True)
    el = jnp.exp(jnp.where(causal, d - _col_to_row(d, eye), -jnp.inf))
    kb = k * beta
    amat = jnp.where(strict, bnt(kb, k) * el, 0.0)
    t = unit_lower_inverse(amat) if tinv is None else tinv
    u, w = unit_lower_solve(amat, t, v * beta, kb * jnp.exp(d))
    qk = jnp.where(causal, bnt(q, k) * el, 0.0)
    dlast = _pick_row(d, c - 1)
    v_new = u - bmm(w, s)
    o = bmm(q * jnp.exp(d), s) + bmm(qk, v_new)
    s_new = jnp.exp(dlast) * s + btn(k * jnp.exp(dlast - d), v_new)
    y = _rms_gate(o, nw, og)
    return (y, s_new, t) if tinv is None else (y, s_new)


def head_fn(x, y, gate, lnw, lnb, tgt):
    u = ALPHA * x + (1.0 + gate) * y
    mu = jnp.mean(u, axis=-1, keepdims=True)
    var = jnp.mean(jnp.square(u - mu), axis=-1, keepdims=True)
    out = (u - mu) * lax.rsqrt(var + LN_EPS) * lnw + lnb
    err = jnp.square(out - tgt)
    return 0.5 * jnp.sum(jnp.mean(err, axis=-1, keepdims=True), axis=0, keepdims=True)


def _proj_call(x, mod3, wpt):
    bsz, t, _ = x.shape
    tm = min(256, t)

    def body(x_ref, mod_ref, w_ref, pg_ref, pq_ref, pd_ref):
        h = (x_ref[0] * (1.0 + mod_ref[0, 1:2, :]) + mod_ref[0, 0:1, :]).astype(MXU)
        nt = lambda lo, hi: lax.dot_general(h, w_ref[lo:hi, :], (((1,), (1,)), ((), ())), preferred_element_type=F32)
        pg_ref[0] = nt(0, W_GLA)
        pq_ref[0] = nt(W_GLA, W_GLA + W_GQKV)
        pd_ref[0] = nt(W_GLA + W_GQKV, PW)

    tok = lambda w: pl.BlockSpec((1, tm, w), lambda b, i: (b, i, 0))
    return pl.pallas_call(
        body, name="proj", grid=(bsz, t // tm),
        in_specs=[tok(D), pl.BlockSpec((1, 3, D), lambda b, i: (b, 0, 0)), pl.BlockSpec((PW, D), lambda b, i: (0, 0))],
        out_specs=[tok(W_GLA), tok(W_GQKV), tok(W_GDN)],
        out_shape=[jax.ShapeDtypeStruct((bsz, t, w), F32) for w in (W_GLA, W_GQKV, W_GDN)],
        compiler_params=_params(("parallel", "parallel")),
    )(x, mod3, wpt)


def _conv_fwd_call(pq, conv_w):
    bsz, t, _ = pq.shape
    tt = min(512, t)
    hb = tt // 8

    def body(x_ref, halo_ref, w_ref, o_ref, buf):
        i = pl.program_id(1)
        buf[0:8, :] = jnp.where(i > 0, halo_ref[0], 0.0)
        buf[8:, :] = x_ref[0]
        acc = w_ref[0:1, :] * buf[pl.ds(5, tt), :]
        for k in range(1, 4):
            acc = acc + w_ref[k:k + 1, :] * buf[pl.ds(5 + k, tt), :]
        o_ref[0] = acc

    return pl.pallas_call(
        body, name="conv_fwd", grid=(bsz, t // tt),
        in_specs=[pl.BlockSpec((1, tt, W_GQKV), lambda b, i: (b, i, 0)),
                  pl.BlockSpec((1, 8, W_GQKV), lambda b, i: (b, jnp.maximum(i * hb - 1, 0), 0)),
                  pl.BlockSpec((4, W_GQKV), lambda b, i: (0, 0))],
        out_specs=pl.BlockSpec((1, tt, W_GQKV), lambda b, i: (b, i, 0)),
        out_shape=jax.ShapeDtypeStruct(pq.shape, F32),
        scratch_shapes=[pltpu.VMEM((tt + 8, W_GQKV), F32)],
        compiler_params=_params(("parallel", "parallel")),
    )(pq, pq, conv_w)


def _conv_bwd_call(dconv, pq, conv_w):
    bsz, t, _ = pq.shape
    tt = min(512, t)
    hb = tt // 8
    nt_ = t // tt

    def body(d_ref, dnext_ref, x_ref, halo_ref, w_ref, din_ref, dw_ref, dbuf, xbuf):
        b, i = pl.program_id(0), pl.program_id(1)

        @pl.when((b == 0) & (i == 0))
        def _():
            dw_ref[...] = jnp.zeros_like(dw_ref)

        dbuf[0:tt, :] = d_ref[0]
        dbuf[tt:, :] = jnp.where(i < nt_ - 1, dnext_ref[0], 0.0)
        xbuf[0:8, :] = jnp.where(i > 0, halo_ref[0], 0.0)
        xbuf[8:, :] = x_ref[0]
        dout = d_ref[0]
        acc = w_ref[0:1, :] * dbuf[pl.ds(3, tt), :]
        dw_ref[0:1, :] += jnp.sum(dout * xbuf[pl.ds(5, tt), :], axis=0, keepdims=True)
        for k in range(1, 4):
            acc = acc + w_ref[k:k + 1, :] * dbuf[pl.ds(3 - k, tt), :]
            dw_ref[k:k + 1, :] += jnp.sum(dout * xbuf[pl.ds(5 + k, tt), :], axis=0, keepdims=True)
        din_ref[0] = acc.astype(MXU)

    tile = pl.BlockSpec((1, tt, W_GQKV), lambda b, i: (b, i, 0))
    return pl.pallas_call(
        body, name="conv_bwd", grid=(bsz, nt_),
        in_specs=[tile, pl.BlockSpec((1, 8, W_GQKV), lambda b, i: (b, jnp.minimum((i + 1) * hb, t // 8 - 1), 0)),
                  tile, pl.BlockSpec((1, 8, W_GQKV), lambda b, i: (b, jnp.maximum(i * hb - 1, 0), 0)),
                  pl.BlockSpec((4, W_GQKV), lambda b, i: (0, 0))],
        out_specs=[tile, pl.BlockSpec((8, W_GQKV), lambda b, i: (0, 0))],
        out_shape=[jax.ShapeDtypeStruct(pq.shape, MXU), jax.ShapeDtypeStruct((8, W_GQKV), F32)],
        scratch_shapes=[pltpu.VMEM((tt + 8, W_GQKV), F32), pltpu.VMEM((tt + 8, W_GQKV), F32)],
        compiler_params=_params(("arbitrary", "arbitrary")),
    )(dconv, dconv, pq, pq, conv_w)


def _chunk_specs(nc, rev, bsz, cols):
    n_of = (lambda n: nc - 1 - n) if rev else (lambda n: n)
    return n_of, [pl.BlockSpec((bsz, CHUNK, w), lambda n, j=j: (0, n_of(n), j)) for w, j in cols]


def _full(shape):
    return pl.BlockSpec(shape, lambda n: (0,) * len(shape))


def _heads(ref, bsz, width, off=0):
    return jnp.stack([ref[b, :, off + width * h:off + width * (h + 1)] for b in range(bsz) for h in range(H)])


def _per_head(ref, bsz, width, rows=slice(None)):
    return jnp.stack([ref[rows, width * h:width * (h + 1)] for _ in range(bsz) for h in range(H)])


_GLA_COLS = [(256, 0), (256, 1), (512, 1), (512, 2), (128, 12)]


def _gla_args(refs, bsz):
    q_ref, k_ref, v_ref, og_ref, lr_ref, wgu_ref, bg_ref, nw_ref = refs
    lr = jnp.stack([lr_ref[b] for b in range(bsz) for _ in range(H)])
    return (_heads(q_ref, bsz, 64), _heads(k_ref, bsz, 64), _heads(v_ref, bsz, 128), lr, _heads(og_ref, bsz, 128),
            _per_head(wgu_ref, bsz, 64), _per_head(bg_ref, bsz, 64), nw_ref[...])


def _gla_fwd_call(pg, wgu, bg, nw):
    bsz, t, _ = pg.shape
    nc = t // CHUNK
    nh = bsz * H
    _, specs = _chunk_specs(nc, False, bsz, _GLA_COLS)

    def body(q_ref, k_ref, v_ref, og_ref, lr_ref, wgu_ref, bg_ref, nw_ref, y_ref, sh_ref, s_ref):
        @pl.when(pl.program_id(0) == 0)
        def _():
            s_ref[...] = jnp.zeros_like(s_ref)

        q, k, v, lr, og, w, b_, nw_ = _gla_args((q_ref, k_ref, v_ref, og_ref, lr_ref, wgu_ref, bg_ref, nw_ref), bsz)
        s = s_ref[...]
        sh_ref[0] = s
        y, s_new = gla_chunk(q, k, v, lr, og, s, w, b_, nw_)
        s_ref[...] = s_new
        for b in range(bsz):
            for h in range(H):
                y_ref[b, :, 128 * h:128 * h + 128] = y[H * b + h].astype(MXU)

    return pl.pallas_call(
        body, name="gla_fwd", grid=(nc,),
        in_specs=specs + [_full((128, 256)), _full((1, 256)), _full((1, 128))],
        out_specs=[pl.BlockSpec((bsz, CHUNK, 512), lambda n: (0, n, 0)),
                   pl.BlockSpec((1, nh, GLA_DK, DV), lambda n: (n, 0, 0, 0))],
        out_shape=[jax.ShapeDtypeStruct((bsz, t, 512), MXU), jax.ShapeDtypeStruct((nc, nh, GLA_DK, DV), F32)],
        scratch_shapes=[pltpu.VMEM((nh, GLA_DK, DV), F32)],
        compiler_params=_params(("arbitrary",)),
    )(pg, pg, pg, pg, pg, wgu, bg, nw)


def _gla_bwd_call(pg, s_hist, dyin, wgu, bg, nw):
    bsz, t, _ = pg.shape
    nc = t // CHUNK
    nh = bsz * H
    n_of, specs = _chunk_specs(nc, True, bsz, _GLA_COLS)

    def body(q_ref, k_ref, v_ref, og_ref, lr_ref, sh_ref, dy_ref, wgu_ref, bg_ref, nw_ref,
             dp_ref, dwgu_ref, dbg_ref, dnw_ref, ds_ref):
        @pl.when(pl.program_id(0) == 0)
        def _():
            dwgu_ref[...] = jnp.zeros_like(dwgu_ref)
            dbg_ref[...] = jnp.zeros_like(dbg_ref)
            dnw_ref[...] = jnp.zeros_like(dnw_ref)
            ds_ref[...] = jnp.zeros_like(ds_ref)

        q, k, v, lr, og, w, b_, nw_ = _gla_args((q_ref, k_ref, v_ref, og_ref, lr_ref, wgu_ref, bg_ref, nw_ref), bsz)
        _, vjp = jax.vjp(gla_chunk, q, k, v, lr, og, sh_ref[0], w, b_, nw_)
        dq, dk, dv, dlr, dog, ds, dwgu, dbg, dnw = vjp((_heads(dy_ref, bsz, 128), ds_ref[...]))
        ds_ref[...] = ds
        dnw_ref[...] += dnw
        for b in range(bsz):
            for h in range(H):
                i = H * b + h
                dp_ref[b, :, 512 + 128 * h:512 + 128 * h + 128] = dv[i].astype(MXU)
                dp_ref[b, :, 1024 + 128 * h:1024 + 128 * h + 128] = dog[i].astype(MXU)
                dwgu_ref[:, 64 * h:64 * h + 64] += dwgu[i]
                dbg_ref[:, 64 * h:64 * h + 64] += dbg[i]
            for j in range(H // 2):
                dp_ref[b, :, 128 * j:128 * j + 128] = jnp.concatenate(
                    [dq[H * b + 2 * j], dq[H * b + 2 * j + 1]], axis=-1).astype(MXU)
                dp_ref[b, :, 256 + 128 * j:256 + 128 * j + 128] = jnp.concatenate(
                    [dk[H * b + 2 * j], dk[H * b + 2 * j + 1]], axis=-1).astype(MXU)
            dp_ref[b, :, 1536:1664] = (dlr[H * b] + dlr[H * b + 1] + dlr[H * b + 2] + dlr[H * b + 3]).astype(MXU)

    return pl.pallas_call(
        body, name="gla_bwd", grid=(nc,),
        in_specs=specs + [pl.BlockSpec((1, nh, GLA_DK, DV), lambda n: (n_of(n), 0, 0, 0)),
                          pl.BlockSpec((bsz, CHUNK, 512), lambda n: (0, n_of(n), 0)),
                          _full((128, 256)), _full((1, 256)), _full((1, 128))],
        out_specs=[pl.BlockSpec((bsz, CHUNK, W_GLA), lambda n: (0, n_of(n), 0)),
                   _full((128, 256)), _full((1, 256)), _full((1, 128))],
        out_shape=[jax.ShapeDtypeStruct(pg.shape, MXU), jax.ShapeDtypeStruct((128, 256), F32),
                   jax.ShapeDtypeStruct((1, 256), F32), jax.ShapeDtypeStruct((1, 128), F32)],
        scratch_shapes=[pltpu.VMEM((nh, GLA_DK, DV), F32)],
        compiler_params=_params(("arbitrary",)),
    )(pg, pg, pg, pg, pg, s_hist, dyin, wgu, bg, nw)


_GDN_COLS = [(512, 0), (512, 1), (512, 2), (512, 0), (128, 4)]


def _gdn_args(refs, bsz):
    q_ref, k_ref, v_ref, og_ref, ab_ref, sc_ref, nw_ref = refs
    return (_heads(q_ref, bsz, 128), _heads(k_ref, bsz, 128), _heads(v_ref, bsz, 128), _heads(ab_ref, bsz, 1),
            _heads(ab_ref, bsz, 1, off=H), _heads(og_ref, bsz, 128), _per_head(sc_ref, bsz, 1, slice(0, 1)),
            _per_head(sc_ref, bsz, 1, slice(1, 2)), nw_ref[...])


def _gdn_fwd_call(conv, pd, sc, nw):
    bsz, t, _ = conv.shape
    nc = t // CHUNK
    nh = bsz * H
    _, specs = _chunk_specs(nc, False, bsz, _GDN_COLS)

    def body(q_ref, k_ref, v_ref, og_ref, ab_ref, sc_ref, nw_ref, y_ref, sh_ref, th_ref, s_ref):
        @pl.when(pl.program_id(0) == 0)
        def _():
            s_ref[...] = jnp.zeros_like(s_ref)

        q, k, v, a, bb, og, alog, dtb, nw_ = _gdn_args((q_ref, k_ref, v_ref, og_ref, ab_ref, sc_ref, nw_ref), bsz)
        s = s_ref[...]
        sh_ref[0] = s
        y, s_new, tinv = gdn_chunk(q, k, v, a, bb, og, s, alog, dtb, nw_)
        th_ref[0] = tinv
        s_ref[...] = s_new
        for b in range(bsz):
            for h in range(H):
                y_ref[b, :, 128 * h:128 * h + 128] = y[H * b + h].astype(MXU)

    return pl.pallas_call(
        body, name="gdn_fwd", grid=(nc,),
        in_specs=specs + [_full((2, 128)), _full((1, 128))],
        out_specs=[pl.BlockSpec((bsz, CHUNK, 512), lambda n: (0, n, 0)),
                   pl.BlockSpec((1, nh, DV, DV), lambda n: (n, 0, 0, 0)),
                   pl.BlockSpec((1, nh, CHUNK, CHUNK), lambda n: (n, 0, 0, 0))],
        out_shape=[jax.ShapeDtypeStruct((bsz, t, 512), MXU), jax.ShapeDtypeStruct((nc, nh, DV, DV), F32),
                   jax.ShapeDtypeStruct((nc, nh, CHUNK, CHUNK), F32)],
        scratch_shapes=[pltpu.VMEM((nh, DV, DV), F32)],
        compiler_params=_params(("arbitrary",)),
    )(conv, conv, conv, pd, pd, sc, nw)


def _gdn_bwd_call(conv, pd, s_hist, t_hist, dyin, sc, nw):
    bsz, t, _ = conv.shape
    nc = t // CHUNK
    nh = bsz * H
    n_of, specs = _chunk_specs(nc, True, bsz, _GDN_COLS)

    def body(q_ref, k_ref, v_ref, og_ref, ab_ref, sh_ref, th_ref, dy_ref, sc_ref, nw_ref,
             dc_ref, dpd_ref, dsc_ref, dnw_ref, ds_ref):
        @pl.when(pl.program_id(0) == 0)
        def _():
            dsc_ref[...] = jnp.zeros_like(dsc_ref)
            dnw_ref[...] = jnp.zeros_like(dnw_ref)
            ds_ref[...] = jnp.zeros_like(ds_ref)

        q, k, v, a, bb, og, alog, dtb, nw_ = _gdn_args((q_ref, k_ref, v_ref, og_ref, ab_ref, sc_ref, nw_ref), bsz)
        _, vjp = jax.vjp(functools.partial(gdn_chunk, tinv=th_ref[0]), q, k, v, a, bb, og, sh_ref[0], alog, dtb, nw_)
        dq, dk, dv, da, db, dog, ds, dalog, ddtb, dnw = vjp((_heads(dy_ref, bsz, 128), ds_ref[...]))
        ds_ref[...] = ds
        dnw_ref[...] += dnw
        lane = lax.broadcasted_iota(jnp.int32, (CHUNK, 128), 1)
        for b in range(bsz):
            dab = jnp.zeros((CHUNK, 128), F32)
            for h in range(H):
                i = H * b + h
                dc_ref[b, :, 128 * h:128 * h + 128] = dq[i]
                dc_ref[b, :, 512 + 128 * h:512 + 128 * h + 128] = dk[i]
                dc_ref[b, :, 1024 + 128 * h:1024 + 128 * h + 128] = dv[i]
                dpd_ref[b, :, 128 * h:128 * h + 128] = dog[i].astype(MXU)
                dab = dab + jnp.where(lane == h, da[i], 0.0) + jnp.where(lane == H + h, db[i], 0.0)
                dsc_ref[0:1, h:h + 1] += dalog[i]
                dsc_ref[1:2, h:h + 1] += ddtb[i]
            dpd_ref[b, :, 512:640] = dab.astype(MXU)

    return pl.pallas_call(
        body, name="gdn_bwd", grid=(nc,),
        in_specs=specs + [pl.BlockSpec((1, nh, DV, DV), lambda n: (n_of(n), 0, 0, 0)),
                          pl.BlockSpec((1, nh, CHUNK, CHUNK), lambda n: (n_of(n), 0, 0, 0)),
                          pl.BlockSpec((bsz, CHUNK, 512), lambda n: (0, n_of(n), 1)),
                          _full((2, 128)), _full((1, 128))],
        out_specs=[pl.BlockSpec((bsz, CHUNK, W_GQKV), lambda n: (0, n_of(n), 0)),
                   pl.BlockSpec((bsz, CHUNK, W_GDN), lambda n: (0, n_of(n), 0)),
                   _full((2, 128)), _full((1, 128))],
        out_shape=[jax.ShapeDtypeStruct(conv.shape, F32), jax.ShapeDtypeStruct(pd.shape, MXU),
                   jax.ShapeDtypeStruct((2, 128), F32), jax.ShapeDtypeStruct((1, 128), F32)],
        scratch_shapes=[pltpu.VMEM((nh, DV, DV), F32)],
        compiler_params=_params(("arbitrary",)),
    )(conv, conv, conv, pd, pd, s_hist, t_hist, dyin, sc, nw)


def _head_call(x, ya, yb, wout, mod3, lnw, lnb, tgt):
    bsz, t, _ = x.shape
    tm = min(256, t)

    def body(x_ref, ya_ref, yb_ref, w_ref, mod_ref, lnw_ref, lnb_ref, t_ref,
             dyin_ref, dxa_ref, dgate_ref, dw_ref, dlnw_ref, dlnb_ref, loss_ref):
        b, i = pl.program_id(0), pl.program_id(1)

        @pl.when((b == 0) & (i == 0))
        def _():
            dw_ref[...] = jnp.zeros_like(dw_ref)
            dlnw_ref[...] = jnp.zeros_like(dlnw_ref)
            dlnb_ref[...] = jnp.zeros_like(dlnb_ref)
            loss_ref[...] = jnp.zeros_like(loss_ref)

        @pl.when(i == 0)
        def _():
            dgate_ref[...] = jnp.zeros_like(dgate_ref)

        yin = jnp.concatenate([ya_ref[0], yb_ref[0]], axis=-1).astype(MXU)
        w = w_ref[...]
        y = jnp.dot(yin, w, preferred_element_type=F32)
        loss, vjp = jax.vjp(head_fn, x_ref[0], y, mod_ref[0, 2:3, :], lnw_ref[...], lnb_ref[...], t_ref[0])
        dx, dy, dgate, dlnw, dlnb, _ = vjp(jnp.ones((1, 1), F32))
        dyb = dy.astype(MXU)
        dyin_ref[0] = lax.dot_general(dyb, w, (((1,), (1,)), ((), ())), preferred_element_type=F32)
        dw_ref[...] += lax.dot_general(yin, dyb, (((0,), (0,)), ((), ())), preferred_element_type=F32)
        dxa_ref[0] = dx
        dgate_ref[0] += dgate
        dlnw_ref[...] += dlnw
        dlnb_ref[...] += dlnb
        loss_ref[...] += jnp.broadcast_to(loss, (1, 128))

    tok = lambda w, j=0: pl.BlockSpec((1, tm, w), lambda b, i: (b, i, j))
    row = pl.BlockSpec((1, D), lambda b, i: (0, 0))
    return pl.pallas_call(
        body, name="head", grid=(bsz, t // tm),
        in_specs=[tok(D), tok(512), tok(512), pl.BlockSpec((D, D), lambda b, i: (0, 0)),
                  pl.BlockSpec((1, 3, D), lambda b, i: (b, 0, 0)), row, row, tok(D)],
        out_specs=[tok(D), tok(D), pl.BlockSpec((1, 1, D), lambda b, i: (b, 0, 0)),
                   pl.BlockSpec((D, D), lambda b, i: (0, 0)), row, row, pl.BlockSpec((1, 128), lambda b, i: (0, 0))],
        out_shape=[jax.ShapeDtypeStruct(x.shape, F32), jax.ShapeDtypeStruct(x.shape, F32),
                   jax.ShapeDtypeStruct((bsz, 1, D), F32), jax.ShapeDtypeStruct((D, D), F32),
                   jax.ShapeDtypeStruct((1, D), F32), jax.ShapeDtypeStruct((1, D), F32),
                   jax.ShapeDtypeStruct((1, 128), F32)],
        compiler_params=_params(("arbitrary", "arbitrary")),
    )(x, ya, yb, wout, mod3, lnw, lnb, tgt)


def _dh_call(dpg, dpq, dpd, wpt, x, mod3, dxa):
    bsz, t, _ = x.shape
    tm = min(256, t)

    def body(dg_ref, dq_ref, dd_ref, w_ref, x_ref, mod_ref, dxa_ref, gx_ref, dmod_ref):
        @pl.when(pl.program_id(1) == 0)
        def _():
            dmod_ref[...] = jnp.zeros_like(dmod_ref)

        mm = lambda a, lo, hi: jnp.dot(a.astype(MXU), w_ref[lo:hi, :], preferred_element_type=F32)
        dh = mm(dg_ref[0], 0, W_GLA) + mm(dq_ref[0], W_GLA, W_GLA + W_GQKV) + mm(dd_ref[0], W_GLA + W_GQKV, PW)
        gx_ref[0] = dh * (1.0 + mod_ref[0, 1:2, :]) + dxa_ref[0]
        dmod_ref[0, 0:1, :] += jnp.sum(dh, axis=0, keepdims=True)
        dmod_ref[0, 1:2, :] += jnp.sum(dh * x_ref[0], axis=0, keepdims=True)

    tok = lambda w: pl.BlockSpec((1, tm, w), lambda b, i: (b, i, 0))
    return pl.pallas_call(
        body, name="dh", grid=(bsz, t // tm),
        in_specs=[tok(W_GLA), tok(W_GQKV), tok(W_GDN), pl.BlockSpec((PW, D), lambda b, i: (0, 0)), tok(D),
                  pl.BlockSpec((1, 3, D), lambda b, i: (b, 0, 0)), tok(D)],
        out_specs=[tok(D), pl.BlockSpec((1, 2, D), lambda b, i: (b, 0, 0))],
        out_shape=[jax.ShapeDtypeStruct(x.shape, F32), jax.ShapeDtypeStruct((bsz, 2, D), F32)],
        compiler_params=_params(("parallel", "arbitrary")),
    )(dpg, dpq, dpd, wpt, x, mod3, dxa)


def _dw_call(x, mod3, dp):
    bsz, t, _ = x.shape
    width = dp.shape[-1]
    tm = min(512, t)

    def body(x_ref, mod_ref, dp_ref, dw_ref):
        @pl.when((pl.program_id(0) == 0) & (pl.program_id(1) == 0))
        def _():
            dw_ref[...] = jnp.zeros_like(dw_ref)

        h = x_ref[0] * (1.0 + mod_ref[0, 1:2, :]) + mod_ref[0, 0:1, :]
        dw_ref[...] += lax.dot_general(dp_ref[0].astype(MXU), h.astype(MXU), (((0,), (0,)), ((), ())),
                                       preferred_element_type=F32)

    tok = lambda w: pl.BlockSpec((1, tm, w), lambda b, i: (b, i, 0))
    return pl.pallas_call(
        body, name=f"dw{width}", grid=(bsz, t // tm),
        in_specs=[tok(D), pl.BlockSpec((1, 3, D), lambda b, i: (b, 0, 0)), tok(width)],
        out_specs=pl.BlockSpec((width, D), lambda b, i: (0, 0)),
        out_shape=jax.ShapeDtypeStruct((width, D), F32),
        compiler_params=_params(("arbitrary", "arbitrary")),
    )(x, mod3, dp)


def _adamw(w, g, m, v):
    m = ADAM_B1 * m + (1.0 - ADAM_B1) * g
    v = ADAM_B2 * v + (1.0 - ADAM_B2) * jnp.square(g)
    m_hat = m / (1.0 - ADAM_B1 ** ADAM_STEP)
    v_hat = v / (1.0 - ADAM_B2 ** ADAM_STEP)
    delta = -ADAM_LR * (m_hat / (jnp.sqrt(v_hat) + ADAM_EPS) + ADAM_WD * w)
    return delta, m, v


def _sum8(ref):
    g = ref[0].astype(F32)
    for j in range(1, NDEV):
        g = g + ref[j].astype(F32)
    return g


def _adam_sum_call(name, g8, w, m, v, cols):
    r, c = w.shape

    def body(g_ref, w_ref, m_ref, v_ref, go_ref, d_ref, mo_ref, vo_ref):
        g = _sum8(g_ref)
        go_ref[...] = g
        d_ref[...], mo_ref[...], vo_ref[...] = _adamw(w_ref[...], g, m_ref[...], v_ref[...])

    blk = pl.BlockSpec((r, cols), lambda i: (0, i))
    return pl.pallas_call(
        body, name=name, grid=(c // cols,),
        in_specs=[pl.BlockSpec((NDEV, r, cols), lambda i: (0, 0, i)), blk, blk, blk],
        out_specs=[blk] * 4, out_shape=[jax.ShapeDtypeStruct((r, c), F32)] * 4,
        compiler_params=_params(("parallel",)),
    )(g8, w, m, v)


def _adam_ada_call(c_all, dmod_cols, w, m, v):
    def body(c_ref, dm_ref, w_ref, m_ref, v_ref, go_ref, d_ref, mo_ref, vo_ref):
        g = lax.dot_general(c_ref[...].astype(MXU), dm_ref[...].astype(MXU), (((0,), (0,)), ((), ())),
                            preferred_element_type=F32)
        go_ref[...] = g
        d_ref[...], mo_ref[...], vo_ref[...] = _adamw(w_ref[...], g, m_ref[...], v_ref[...])

    return pl.pallas_call(
        body, name="adam_ada", out_shape=[jax.ShapeDtypeStruct(w.shape, F32)] * 4, compiler_params=_params(),
    )(c_all, dmod_cols, w, m, v)


def _adam_small_call(g, w, m, v):
    def body(g_ref, w_ref, m_ref, v_ref, d_ref, mo_ref, vo_ref):
        d_ref[...], mo_ref[...], vo_ref[...] = _adamw(w_ref[...], g_ref[...], m_ref[...], v_ref[...])

    return pl.pallas_call(
        body, name="adam_small", out_shape=[jax.ShapeDtypeStruct(w.shape, F32)] * 3, compiler_params=_params(),
    )(g, w, m, v)


def _small_sum_call(sp_all):
    def body(sp_ref, dmod_ref, sum_ref):
        acc = sp_ref[0, 2:3, :]
        for j in range(1, NDEV):
            acc = acc + sp_ref[j, 2:3, :]
        sum_ref[...] = acc
        for j in range(NDEV):
            dmod_ref[2 * j:2 * j + 2, :] = sp_ref[j, 0:2, :]

    return pl.pallas_call(
        body, name="small_sum", out_shape=[jax.ShapeDtypeStruct((2 * NDEV, SPW), F32), jax.ShapeDtypeStruct((1, SPW), F32)],
        compiler_params=_params(),
    )(sp_all)


def _mesh_pos():
    x, y, c = lax.axis_index("x"), lax.axis_index("y"), lax.axis_index("c")
    return x, y, c, 4 * x + 2 * y + c


def _peer(x, y, c, k):
    px = 1 - x if k & 4 else x
    py = 1 - y if k & 2 else y
    pc = 1 - c if k & 1 else c
    return (px, py, pc), 4 * px + 2 * py + pc


_ANY = pl.BlockSpec(memory_space=pl.ANY)
_VMEM = pl.BlockSpec(memory_space=pltpu.VMEM)


def _gather_call(c8, w_ada, b_sh, w_in, w_out, conv_w, wgu):
    big = [w_in, w_out, conv_w, wgu]
    nb = len(big)

    def body(c_ref, wada_ref, b_ref, *rest):
        srcs, outs = rest[:nb], rest[nb:2 * nb]
        call_ref, mod_ref, modp, send_sems, recv_sems, loc_sems = rest[2 * nb:]
        x, y, c, me = _mesh_pos()

        def remote(src, dst, a, k, dev):
            return pltpu.make_async_remote_copy(src_ref=src, dst_ref=dst, send_sem=send_sems.at[a, k],
                                                recv_sem=recv_sems.at[a, k], device_id=dev,
                                                device_id_type=pl.DeviceIdType.MESH)

        sends = []
        call_ref[me] = c_ref[...]
        for k in range(1, NDEV):
            dev, _ = _peer(x, y, c, k)
            sends.append(remote(c_ref, call_ref.at[me], nb, k, dev))
            sends[-1].start()
        local = [pltpu.make_async_copy(srcs[a], outs[a].at[me], loc_sems.at[a]) for a in range(nb)]
        for cp in local:
            cp.start()
        for k in range(1, NDEV):
            dev, _ = _peer(x, y, c, k)
            for a in range(nb):
                sends.append(remote(srcs[a], outs[a].at[me], a, k, dev))
                sends[-1].start()
        for k in range(1, NDEV):
            dev, pidx = _peer(x, y, c, k)
            remote(c_ref, call_ref.at[pidx], nb, k, dev).wait_recv()
        modp[...] = jnp.dot(call_ref[...].reshape(NDEV * 8, D).astype(MXU), wada_ref[...].astype(MXU),
                            preferred_element_type=F32) + b_ref[...]
        mod_ref[me] = modp[pl.ds(pl.multiple_of(me * 8, 8), 8), :]
        for k in range(1, NDEV):
            dev, pidx = _peer(x, y, c, k)
            sends.append(remote(modp.at[pl.ds(pl.multiple_of(pidx * 8, 8), 8), :], mod_ref.at[me], nb + 1, k, dev))
            sends[-1].start()
        for k in range(1, NDEV):
            dev, pidx = _peer(x, y, c, k)
            for a in range(nb):
                remote(srcs[a], outs[a].at[pidx], a, k, dev).wait_recv()
            remote(modp.at[pl.ds(0, 8), :], mod_ref.at[pidx], nb + 1, k, dev).wait_recv()
        for cp in sends:
            cp.wait_send()
        for cp in local:
            cp.wait()

    out_shape = [jax.ShapeDtypeStruct((NDEV,) + a.shape, a.dtype) for a in big]
    out_shape += [jax.ShapeDtypeStruct((NDEV, 8, D), F32), jax.ShapeDtypeStruct((NDEV, 8, SHARD_ADA), F32)]
    return pl.pallas_call(
        body, name="gather", out_shape=out_shape,
        in_specs=[_VMEM, _VMEM, _VMEM] + [_ANY] * nb,
        out_specs=[_ANY] * nb + [_VMEM, _VMEM],
        scratch_shapes=[pltpu.VMEM((NDEV * 8, SHARD_ADA), F32), pltpu.SemaphoreType.DMA((nb + 2, NDEV)),
                        pltpu.SemaphoreType.DMA((nb + 2, NDEV)), pltpu.SemaphoreType.DMA((nb,))],
        compiler_params=_params(),
    )(c8, w_ada, b_sh, *big)


def _reduce_call(blocks, sp):
    nb = len(blocks)

    def body(sp_ref, *rest):
        srcs, outs = rest[:nb], rest[nb:2 * nb]
        spall_ref, send_sems, recv_sems, loc_sems = rest[2 * nb:]
        x, y, c, me = _mesh_pos()

        def remote(src, dst, a, k, dev):
            return pltpu.make_async_remote_copy(src_ref=src, dst_ref=dst, send_sem=send_sems.at[a, k],
                                                recv_sem=recv_sems.at[a, k], device_id=dev,
                                                device_id_type=pl.DeviceIdType.MESH)

        sends = []
        spall_ref[me] = sp_ref[...]
        local = [pltpu.make_async_copy(srcs[a].at[me], outs[a].at[me], loc_sems.at[a]) for a in range(nb)]
        for cp in local:
            cp.start()
        for k in range(1, NDEV):
            dev, pidx = _peer(x, y, c, k)
            sends.append(remote(sp_ref, spall_ref.at[me], nb, k, dev))
            sends[-1].start()
            for a in range(nb):
                sends.append(remote(srcs[a].at[pidx], outs[a].at[me], a, k, dev))
                sends[-1].start()
        for k in range(1, NDEV):
            dev, pidx = _peer(x, y, c, k)
            remote(sp_ref, spall_ref.at[pidx], nb, k, dev).wait_recv()
            for a in range(nb):
                remote(srcs[a].at[pidx], outs[a].at[pidx], a, k, dev).wait_recv()
        for cp in sends:
            cp.wait_send()
        for cp in local:
            cp.wait()

    return pl.pallas_call(
        body, name="reduce", out_shape=[jax.ShapeDtypeStruct(a.shape, a.dtype) for a in blocks]
        + [jax.ShapeDtypeStruct((NDEV, 8, SPW), F32)],
        in_specs=[_VMEM] + [_ANY] * nb, out_specs=[_ANY] * nb + [_VMEM],
        scratch_shapes=[pltpu.SemaphoreType.DMA((nb + 1, NDEV)), pltpu.SemaphoreType.DMA((nb + 1, NDEV)),
                        pltpu.SemaphoreType.DMA((nb,))],
        compiler_params=_params(),
    )(sp, *blocks)


def _pad_cols(a, n):
    return jnp.pad(a, ((0, 0), (0, n - a.shape[1])))


def _assemble_wt(wt_full):
    q, k, v, lr, og, gqkv, ab, dog = jnp.split(wt_full, [256, 512, 1024, 1040, 1552, 3088, 3096], axis=0)
    z = lambda n: jnp.zeros((n, wt_full.shape[1]), wt_full.dtype)
    return jnp.concatenate([q, k, v, og, lr, z(112), gqkv, dog, ab, z(120)], axis=0)


def _disassemble_dwt(dw_gla, dw_gqkv, dw_gdn):
    return jnp.concatenate([dw_gla[:1024], dw_gla[1536:1552], dw_gla[1024:1536], dw_gqkv,
                            dw_gdn[512:520], dw_gdn[:512]], axis=0)


def local_step(x, mod3, wp, wout, conv_w, wgu_p, bg, gla_nw, sc, gdn_nw, lnw, lnb, tgt):
    pg, pq, pd = _proj_call(x, mod3, wp)
    conv = _conv_fwd_call(pq, conv_w)
    ya, s_gla = _gla_fwd_call(pg, wgu_p, bg, gla_nw)
    yb, s_gdn, t_gdn = _gdn_fwd_call(conv, pd, sc, gdn_nw)
    dyin, dxa, dgate, dwout, dlnw, dlnb, loss = _head_call(x, ya, yb, wout, mod3, lnw, lnb, tgt)
    dpg, dwgu, dbg, dnw_gla = _gla_bwd_call(pg, s_gla, dyin, wgu_p, bg, gla_nw)
    dconv, dpd, dsc, dnw_gdn = _gdn_bwd_call(conv, pd, s_gdn, t_gdn, dyin, sc, gdn_nw)
    dpq, dconv_w = _conv_bwd_call(dconv, pq, conv_w)
    gx, dmod2 = _dh_call(dpg, dpq, dpd, wp, x, mod3, dxa)
    dw_in = _disassemble_dwt(_dw_call(x, mod3, dpg), _dw_call(x, mod3, dpq), _dw_call(x, mod3, dpd))
    dmod = jnp.concatenate([dmod2, dgate], axis=1)
    return dict(loss=loss[0, 0], gx=gx, dmod=dmod, dw_in=dw_in, dwout=dwout, dconv_w=dconv_w[:4], dwgu=dwgu[:16],
                dbg=dbg, dnw_gla=dnw_gla, dalog=dsc[0:1, :4], ddtb=dsc[1:2, :4], dnw_gdn=dnw_gdn, dlnw=dlnw, dlnb=dlnb)


def kernel(x, c, w_ada, b_ada, w_in, gla_w_gate_up, gla_b_gate, gla_norm_w, gdn_conv_w, gdn_a_log, gdn_dt_bias, gdn_norm_w, w_out, ln_w, ln_b, loss_target, m_w_ada, m_b_ada, m_w_in, m_gla_w_gate_up, m_gla_b_gate, m_gla_norm_w, m_gdn_conv_w, m_gdn_a_log, m_gdn_dt_bias, m_gdn_norm_w, m_w_out, m_ln_w, m_ln_b, v_w_ada, v_b_ada, v_w_in, v_gla_w_gate_up, v_gla_b_gate, v_gla_norm_w, v_gdn_conv_w, v_gdn_a_log, v_gdn_dt_bias, v_gdn_norm_w, v_w_out, v_ln_w, v_ln_b):
    me = 4 * lax.axis_index("x") + 2 * lax.axis_index("y") + lax.axis_index("c")
    bsz = x.shape[0]

    b_sh = lax.dynamic_slice(b_ada, (0, me * SHARD_ADA), (1, SHARD_ADA))
    c8 = jnp.pad(c, ((0, 8 - bsz), (0, 0)))
    w_in_t, m_in_t, v_in_t = (jnp.swapaxes(a[0], 0, 1) for a in (w_in, m_w_in, v_w_in))
    win_all, wout_all, conv_all, wgu_all, c_all, mod_blk = _gather_call(
        c8, w_ada[0], b_sh, w_in_t.astype(WIRE), w_out[0].astype(WIRE), gdn_conv_w[0], gla_w_gate_up[0])
    wp = _assemble_wt(win_all.reshape(IN_COLS, D))
    wout = wout_all.reshape(D, D)
    conv_w = jnp.transpose(conv_all, (1, 0, 2)).reshape(4, W_GQKV)
    wgu_p = jnp.pad(jnp.transpose(wgu_all, (1, 0, 2)).reshape(16, 256), ((0, 112), (0, 0)))
    mod = jnp.transpose(mod_blk[:, :bsz, :], (1, 0, 2)).reshape(bsz, 3 * D)
    mod3 = mod.reshape(bsz, 3, D)
    sc = jnp.concatenate([_pad_cols(gdn_a_log, 128), _pad_cols(gdn_dt_bias, 128)], axis=0)

    g = local_step(x, mod3, wp, wout, conv_w, wgu_p, gla_b_gate, gla_norm_w, sc, gdn_norm_w, ln_w, ln_b, loss_target)

    small = jnp.concatenate([g["dlnw"], g["dlnb"], g["dbg"], g["dnw_gla"], g["dnw_gdn"], _pad_cols(g["dalog"], 128),
                             _pad_cols(g["ddtb"], 128), jnp.full((1, 128), g["loss"], F32), jnp.zeros((1, 128), F32)], axis=1)
    sp = jnp.concatenate([g["dmod"].reshape(bsz, SPW), small, jnp.zeros((8 - bsz - 1, SPW), F32)], axis=0)
    blocks = [g["dw_in"].reshape(NDEV, SHARD_IN, D).astype(WIRE),
              g["dwout"].reshape(NDEV, D // NDEV, D).astype(WIRE),
              jnp.transpose(g["dconv_w"].reshape(4, NDEV, W_GQKV // NDEV), (1, 0, 2)),
              jnp.transpose(g["dwgu"].reshape(16, NDEV, 256 // NDEV), (1, 0, 2))]
    r_in, r_out, r_conv, r_gu, sp_all = _reduce_call(blocks, sp)
    dmod_all, sums = _small_sum_call(sp_all)

    g_in, d_in, nm_in, nv_in = (jnp.swapaxes(a, 0, 1) for a in _adam_sum_call("adam_in", r_in, w_in_t, m_in_t, v_in_t, 256))
    g_out, d_out, nm_out, nv_out = _adam_sum_call("adam_out", r_out, w_out[0], m_w_out[0], v_w_out[0], D)
    g_conv, d_conv, nm_conv, nv_conv = _adam_sum_call("adam_conv", r_conv, gdn_conv_w[0], m_gdn_conv_w[0], v_gdn_conv_w[0],
                                                      W_GQKV // NDEV)
    g_gu, d_gu, nm_gu, nv_gu = _adam_sum_call("adam_gu", r_gu, gla_w_gate_up[0], m_gla_w_gate_up[0], v_gla_w_gate_up[0],
                                              256 // NDEV)
    c16 = c_all[:, :bsz, :].reshape(NDEV * bsz, D)
    g_ada, d_ada, nm_ada, nv_ada = _adam_ada_call(c16, lax.dynamic_slice(dmod_all, (0, me * SHARD_ADA), (NDEV * bsz, SHARD_ADA)),
                                                  w_ada[0], m_w_ada[0], v_w_ada[0])

    def pack(b_a, lw, lb, bgt, n1, n2, al, dt):
        return jnp.concatenate([b_a, lw, lb, bgt, n1, n2, _pad_cols(al, 128), _pad_cols(dt, 128)], axis=1).reshape(-1, 128)

    g_small = _bada_and_pack(dmod_all, sums)
    w_s = pack(b_ada, ln_w, ln_b, gla_b_gate, gla_norm_w, gdn_norm_w, gdn_a_log, gdn_dt_bias)
    m_s = pack(m_b_ada, m_ln_w, m_ln_b, m_gla_b_gate, m_gla_norm_w, m_gdn_norm_w, m_gdn_a_log, m_gdn_dt_bias)
    v_s = pack(v_b_ada, v_ln_w, v_ln_b, v_gla_b_gate, v_gla_norm_w, v_gdn_norm_w, v_gdn_a_log, v_gdn_dt_bias)
    d_s, nm_s, nv_s = _adam_small_call(g_small, w_s, m_s, v_s)

    def unpack(p):
        f = p.reshape(1, -1)
        b_a, lw, lb, bgt, n1, n2, al, dt = jnp.split(f, [3072, 4096, 5120, 5376, 5504, 5632, 5760], axis=1)
        return dict(b_ada=b_a, ln_w=lw, ln_b=lb, b_gate=bgt, gla_nw=n1, gdn_nw=n2, a_log=al[:, :4], dt_bias=dt[:, :4])

    gs, ds, ms, vs = unpack(g_small), unpack(d_s), unpack(nm_s), unpack(nv_s)
    loss = sums[0, SMALL_W]

    def group(t_ada, t_in, t_gu, t_conv, t_out, s):
        return [t_ada[None], s["b_ada"], t_in[None], t_gu[None], s["b_gate"], s["gla_nw"], t_conv[None], s["a_log"],
                s["dt_bias"], s["gdn_nw"], t_out[None], s["ln_w"], s["ln_b"]]

    return (loss, g["gx"], *group(g_ada, g_in, g_gu, g_conv, g_out, gs), *group(d_ada, d_in, d_gu, d_conv, d_out, ds),
            *group(nm_ada, nm_in, nm_gu, nm_conv, nm_out, ms), *group(nv_ada, nv_in, nv_gu, nv_conv, nv_out, vs))


def _bada_and_pack(dmod_all, sums):
    n = dmod_all.shape[0]

    def body(dm_ref, s_ref, o_ref):
        acc = dm_ref[0:1, :]
        for j in range(1, n):
            acc = acc + dm_ref[j:j + 1, :]
        o_ref[:, 0:SPW] = acc
        o_ref[:, SPW:SPW + SMALL_W] = s_ref[:, 0:SMALL_W]

    packed = pl.pallas_call(
        body, name="bada_pack", out_shape=jax.ShapeDtypeStruct((1, SPW + SMALL_W), F32), compiler_params=_params(),
    )(dmod_all, sums)
    return packed.reshape(-1, 128)
```

```python
import functools

import jax
import jax.numpy as jnp
from jax import lax
from jax.experimental import pallas as pl
from jax.experimental.pallas import tpu as pltpu

F32 = jnp.float32
MXU = jnp.bfloat16
WIRE = jnp.bfloat16
HI = lax.Precision.HIGH

D = 1024
NDEV = 8
H = 4
GLA_DK = 64
DV = 128
CHUNK = 64
LN_EPS = 1e-5
RMS_EPS = 1e-6
ALPHA = 2.0 ** 0.25
GATE_NORM = 16.0

W_GLA, W_GQKV, W_GDN = 1664, 1536, 640
PW = W_GLA + W_GQKV + W_GDN
IN_COLS = 3608
SHARD_IN = IN_COLS // NDEV
SHARD_ADA = 3 * D // NDEV
SPW = 3 * D
SMALL_W = 2816

ADAM_LR, ADAM_B1, ADAM_B2, ADAM_EPS, ADAM_WD, ADAM_STEP = 0.001, 0.9, 0.999, 1e-08, 0.01, 10

VMEM_LIMIT = 56 * 1024 * 1024


def _params(sem=None, **kw):
    if sem is not None:
        kw["dimension_semantics"] = sem
    return pltpu.CompilerParams(vmem_limit_bytes=VMEM_LIMIT, **kw)


_MM = (((2,), (1,)), ((0,), (0,)))
_NT = (((2,), (2,)), ((0,), (0,)))
_TN = (((1,), (1,)), ((0,), (0,)))


def _dg(a, b, dims):
    return lax.dot_general(a.astype(MXU), b.astype(MXU), dims, preferred_element_type=F32)


def _hdg(a, b, dims):
    return lax.dot_general(a, b, dims, precision=HI, preferred_element_type=F32)


@jax.custom_vjp
def bmm(a, b):
    return _dg(a, b, _MM)


bmm.defvjp(lambda a, b: (_dg(a, b, _MM), (a, b)), lambda r, g: (_dg(g, r[1], _NT), _dg(r[0], g, _TN)))


@jax.custom_vjp
def bnt(a, b):
    return _dg(a, b, _NT)


bnt.defvjp(lambda a, b: (_dg(a, b, _NT), (a, b)), lambda r, g: (_dg(g, r[1], _MM), _dg(g, r[0], _TN)))


@jax.custom_vjp
def btn(a, b):
    return _dg(a, b, _TN)


btn.defvjp(lambda a, b: (_dg(a, b, _TN), (a, b)), lambda r, g: (_dg(r[1], g, _NT), _dg(r[0], g, _MM)))


def unit_lower_inverse(a):
    n = a.shape[-1]
    r, c = _iotas(n)
    p = -a
    t = (r == c).astype(F32) + p
    for _ in range(5):
        p = _hdg(p, p, _MM)
        t = t + _hdg(t, p, _MM)
    return t


@jax.custom_vjp
def unit_lower_solve(a, t, r1, r2):
    return _dg(t, r1, _MM), _dg(t, r2, _MM)


def _solve_fwd(a, t, r1, r2):
    s1, s2 = _dg(t, r1, _MM), _dg(t, r2, _MM)
    return (s1, s2), (t, s1, s2)


def _solve_bwd(res, g):
    t, s1, s2 = res
    d1, d2 = _dg(t, g[0], _TN), _dg(t, g[1], _TN)
    return -(_dg(d1, s1, _NT) + _dg(d2, s2, _NT)), jnp.zeros_like(t), d1, d2


unit_lower_solve.defvjp(_solve_fwd, _solve_bwd)


def _iotas(n):
    return lax.broadcasted_iota(jnp.int32, (n, n), 0), lax.broadcasted_iota(jnp.int32, (n, n), 1)


def _col_to_row(col, eye):
    return jnp.sum(jnp.where(eye, col, 0.0), axis=1, keepdims=True)


def _row_to_col(row, eye):
    return jnp.sum(jnp.where(eye, row, 0.0), axis=2, keepdims=True)


def _pick_row(m, i):
    r = lax.broadcasted_iota(jnp.int32, m.shape, 1)
    return jnp.sum(jnp.where(r == i, m, 0.0), axis=1, keepdims=True)


def _rms_gate(o, nw, og):
    on = o * lax.rsqrt(jnp.mean(o * o, axis=-1, keepdims=True) + RMS_EPS) * nw
    return on * jax.nn.silu(og)


def gla_chunk(q, k, v, lr, og, s, wgu, bg, nw):
    n, c, _ = q.shape
    r, cc = _iotas(c)
    causal = r >= cc
    qs = q * (GLA_DK ** -0.5)
    z = bmm(lr, wgu) + bg
    g = jax.nn.log_sigmoid(z) / GATE_NORM
    b = _hdg(jnp.broadcast_to(causal.astype(F32), (n, c, c)), g, _MM)
    bref = _pick_row(b, c // 2 - 1)
    blast = _pick_row(b, c - 1)
    att = jnp.where(causal, bnt(qs * jnp.exp(b - bref), k * jnp.exp(bref - b)), 0.0)
    o = bmm(att, v) + bmm(qs * jnp.exp(b), s)
    rk, ck = _iotas(GLA_DK)
    s_new = _row_to_col(jnp.exp(blast), rk == ck) * s + btn(k * jnp.exp(blast - b), v)
    return _rms_gate(o, nw, og), s_new


def gdn_chunk(cq, ck, cv, a, bb, og, s, alog, dtb, nw, tinv=None):
    c = cq.shape[1]
    r, cc = _iotas(c)
    eye, causal, strict = r == cc, r >= cc, r > cc
    q, k, v = jax.nn.silu(cq), jax.nn.silu(ck), jax.nn.silu(cv)
    q = q * lax.rsqrt(jnp.sum(q * q, axis=-1, keepdims=True) + RMS_EPS) * (DV ** -0.5)
    k = k * lax.rsqrt(jnp.sum(k * k, axis=-1, keepdims=True) + RMS_EPS)
    g = -jnp.exp(alog) * jax.nn.softplus(a + dtb)
    beta = jax.nn.sigmoid(bb)
    d = jnp.sum(jnp.where(causal, _col_to_row(g, eye), 0.0), axis=2, keepdims=True)
    el = jnp.exp(jnp.where(causal, d - _col_to_row(d, eye), -jnp.inf))
    kb = k * beta
    amat = jnp.where(strict, bnt(kb, k) * el, 0.0)
    t = unit_lower_inverse(amat) if tinv is None else tinv
    u, w = unit_lower_solve(amat, t, v * beta, kb * jnp.exp(d))
    qk = jnp.where(causal, bnt(q, k) * el, 0.0)
    dlast = _pick_row(d, c - 1)
    v_new = u - bmm(w, s)
    o = bmm(q * jnp.exp(d), s) + bmm(qk, v_new)
    s_new = jnp.exp(dlast) * s + btn(k * jnp.exp(dlast - d), v_new)
    y = _rms_gate(o, nw, og)
    return (y, s_new, t) if tinv is None else (y, s_new)


def head_fn(x, y, gate, lnw, lnb, tgt):
    u = ALPHA * x + (1.0 + gate) * y
    mu = jnp.mean(u, axis=-1, keepdims=True)
    var = jnp.mean(jnp.square(u - mu), axis=-1, keepdims=True)
    out = (u - mu) * lax.rsqrt(var + LN_EPS) * lnw + lnb
    err = jnp.square(out - tgt)
    return 0.5 * jnp.sum(jnp.mean(err, axis=-1, keepdims=True), axis=0, keepdims=True)


def _proj_call(x, mod3, wpt):
    bsz, t, _ = x.shape
    tm = min(256, t)

    def body(x_ref, mod_ref, w_ref, pg_ref, pq_ref, pd_ref):
        h = (x_ref[0] * (1.0 + mod_ref[0, 1:2, :]) + mod_ref[0, 0:1, :]).astype(MXU)
        nt = lambda lo, hi: lax.dot_general(h, w_ref[lo:hi, :], (((1,), (1,)), ((), ())), preferred_element_type=F32)
        pg_ref[0] = nt(0, W_GLA)
        pq_ref[0] = nt(W_GLA, W_GLA + W_GQKV)
        pd_ref[0] = nt(W_GLA + W_GQKV, PW)

    tok = lambda w: pl.BlockSpec((1, tm, w), lambda b, i: (b, i, 0))
    return pl.pallas_call(
        body, name="proj", grid=(bsz, t // tm),
        in_specs=[tok(D), pl.BlockSpec((1, 3, D), lambda b, i: (b, 0, 0)), pl.BlockSpec((PW, D), lambda b, i: (0, 0))],
        out_specs=[tok(W_GLA), tok(W_GQKV), tok(W_GDN)],
        out_shape=[jax.ShapeDtypeStruct((bsz, t, w), F32) for w in (W_GLA, W_GQKV, W_GDN)],
        compiler_params=_params(("parallel", "parallel")),
    )(x, mod3, wpt)


def _conv_fwd_call(pq, conv_w):
    bsz, t, _ = pq.shape
    tt = min(512, t)
    hb = tt // 8

    def body(x_ref, halo_ref, w_ref, o_ref, buf):
        i = pl.program_id(1)
        buf[0:8, :] = jnp.where(i > 0, halo_ref[0], 0.0)
        buf[8:, :] = x_ref[0]
        acc = w_ref[0:1, :] * buf[pl.ds(5, tt), :]
        for k in range(1, 4):
            acc = acc + w_ref[k:k + 1, :] * buf[pl.ds(5 + k, tt), :]
        o_ref[0] = acc

    return pl.pallas_call(
        body, name="conv_fwd", grid=(bsz, t // tt),
        in_specs=[pl.BlockSpec((1, tt, W_GQKV), lambda b, i: (b, i, 0)),
                  pl.BlockSpec((1, 8, W_GQKV), lambda b, i: (b, jnp.maximum(i * hb - 1, 0), 0)),
                  pl.BlockSpec((4, W_GQKV), lambda b, i: (0, 0))],
        out_specs=pl.BlockSpec((1, tt, W_GQKV), lambda b, i: (b, i, 0)),
        out_shape=jax.ShapeDtypeStruct(pq.shape, F32),
        scratch_shapes=[pltpu.VMEM((tt + 8, W_GQKV), F32)],
        compiler_params=_params(("parallel", "parallel")),
    )(pq, pq, conv_w)


def _conv_bwd_call(dconv, pq, conv_w):
    bsz, t, _ = pq.shape
    tt = min(512, t)
    hb = tt // 8
    nt_ = t // tt

    def body(d_ref, dnext_ref, x_ref, halo_ref, w_ref, din_ref, dw_ref, dbuf, xbuf):
        b, i = pl.program_id(0), pl.program_id(1)

        @pl.when((b == 0) & (i == 0))
        def _():
            dw_ref[...] = jnp.zeros_like(dw_ref)

        dbuf[0:tt, :] = d_ref[0]
        dbuf[tt:, :] = jnp.where(i < nt_ - 1, dnext_ref[0], 0.0)
        xbuf[0:8, :] = jnp.where(i > 0, halo_ref[0], 0.0)
        xbuf[8:, :] = x_ref[0]
        dout = d_ref[0]
        acc = w_ref[0:1, :] * dbuf[pl.ds(3, tt), :]
        dw_ref[0:1, :] += jnp.sum(dout * xbuf[pl.ds(5, tt), :], axis=0, keepdims=True)
        for k in range(1, 4):
            acc = acc + w_ref[k:k + 1, :] * dbuf[pl.ds(3 - k, tt), :]
            dw_ref[k:k + 1, :] += jnp.sum(dout * xbuf[pl.ds(5 + k, tt), :], axis=0, keepdims=True)
        din_ref[0] = acc.astype(MXU)

    tile = pl.BlockSpec((1, tt, W_GQKV), lambda b, i: (b, i, 0))
    return pl.pallas_call(
        body, name="conv_bwd", grid=(bsz, nt_),
        in_specs=[tile, pl.BlockSpec((1, 8, W_GQKV), lambda b, i: (b, jnp.minimum((i + 1) * hb, t // 8 - 1), 0)),
                  tile, pl.BlockSpec((1, 8, W_GQKV), lambda b, i: (b, jnp.maximum(i * hb - 1, 0), 0)),
                  pl.BlockSpec((4, W_GQKV), lambda b, i: (0, 0))],
        out_specs=[tile, pl.BlockSpec((8, W_GQKV), lambda b, i: (0, 0))],
        out_shape=[jax.ShapeDtypeStruct(pq.shape, MXU), jax.ShapeDtypeStruct((8, W_GQKV), F32)],
        scratch_shapes=[pltpu.VMEM((tt + 8, W_GQKV), F32), pltpu.VMEM((tt + 8, W_GQKV), F32)],
        compiler_params=_params(("arbitrary", "arbitrary")),
    )(dconv, dconv, pq, pq, conv_w)


def _chunk_specs(nc, rev, bsz, cols):
    n_of = (lambda n: nc - 1 - n) if rev else (lambda n: n)
    return n_of, [pl.BlockSpec((bsz, CHUNK, w), lambda n, j=j: (0, n_of(n), j)) for w, j in cols]


def _full(shape):
    return pl.BlockSpec(shape, lambda n: (0,) * len(shape))


def _heads(ref, bsz, width, off=0):
    return jnp.stack([ref[b, :, off + width * h:off + width * (h + 1)] for b in range(bsz) for h in range(H)])


def _per_head(ref, bsz, width, rows=slice(None)):
    return jnp.stack([ref[rows, width * h:width * (h + 1)] for _ in range(bsz) for h in range(H)])


_GLA_COLS = [(256, 0), (256, 1), (512, 1), (512, 2), (128, 12)]


def _gla_args(refs, bsz):
    q_ref, k_ref, v_ref, og_ref, lr_ref, wgu_ref, bg_ref, nw_ref = refs
    lr = jnp.stack([lr_ref[b] for b in range(bsz) for _ in range(H)])
    return (_heads(q_ref, bsz, 64), _heads(k_ref, bsz, 64), _heads(v_ref, bsz, 128), lr, _heads(og_ref, bsz, 128),
            _per_head(wgu_ref, bsz, 64), _per_head(bg_ref, bsz, 64), nw_ref[...])


def _gla_fwd_call(pg, wgu, bg, nw):
    bsz, t, _ = pg.shape
    nc = t // CHUNK
    nh = bsz * H
    _, specs = _chunk_specs(nc, False, bsz, _GLA_COLS)

    def body(q_ref, k_ref, v_ref, og_ref, lr_ref, wgu_ref, bg_ref, nw_ref, y_ref, sh_ref, s_ref):
        @pl.when(pl.program_id(0) == 0)
        def _():
            s_ref[...] = jnp.zeros_like(s_ref)

        q, k, v, lr, og, w, b_, nw_ = _gla_args((q_ref, k_ref, v_ref, og_ref, lr_ref, wgu_ref, bg_ref, nw_ref), bsz)
        s = s_ref[...]
        sh_ref[0] = s
        y, s_new = gla_chunk(q, k, v, lr, og, s, w, b_, nw_)
        s_ref[...] = s_new
        for b in range(bsz):
            for h in range(H):
                y_ref[b, :, 128 * h:128 * h + 128] = y[H * b + h].astype(MXU)

    return pl.pallas_call(
        body, name="gla_fwd", grid=(nc,),
        in_specs=specs + [_full((128, 256)), _full((1, 256)), _full((1, 128))],
        out_specs=[pl.BlockSpec((bsz, CHUNK, 512), lambda n: (0, n, 0)),
                   pl.BlockSpec((1, nh, GLA_DK, DV), lambda n: (n, 0, 0, 0))],
        out_shape=[jax.ShapeDtypeStruct((bsz, t, 512), MXU), jax.ShapeDtypeStruct((nc, nh, GLA_DK, DV), F32)],
        scratch_shapes=[pltpu.VMEM((nh, GLA_DK, DV), F32)],
        compiler_params=_params(("arbitrary",)),
    )(pg, pg, pg, pg, pg, wgu, bg, nw)


def _gla_bwd_call(pg, s_hist, dyin, wgu, bg, nw):
    bsz, t, _ = pg.shape
    nc = t // CHUNK
    nh = bsz * H
    n_of, specs = _chunk_specs(nc, True, bsz, _GLA_COLS)

    def body(q_ref, k_ref, v_ref, og_ref, lr_ref, sh_ref, dy_ref, wgu_ref, bg_ref, nw_ref,
             dp_ref, dwgu_ref, dbg_ref, dnw_ref, ds_ref):
        @pl.when(pl.program_id(0) == 0)
        def _():
            dwgu_ref[...] = jnp.zeros_like(dwgu_ref)
            dbg_ref[...] = jnp.zeros_like(dbg_ref)
            dnw_ref[...] = jnp.zeros_like(dnw_ref)
            ds_ref[...] = jnp.zeros_like(ds_ref)

        q, k, v, lr, og, w, b_, nw_ = _gla_args((q_ref, k_ref, v_ref, og_ref, lr_ref, wgu_ref, bg_ref, nw_ref), bsz)
        _, vjp = jax.vjp(gla_chunk, q, k, v, lr, og, sh_ref[0], w, b_, nw_)
        dq, dk, dv, dlr, dog, ds, dwgu, dbg, dnw = vjp((_heads(dy_ref, bsz, 128), ds_ref[...]))
        ds_ref[...] = ds
        dnw_ref[...] += dnw
        for b in range(bsz):
            for h in range(H):
                i = H * b + h
                dp_ref[b, :, 512 + 128 * h:512 + 128 * h + 128] = dv[i].astype(MXU)
                dp_ref[b, :, 1024 + 128 * h:1024 + 128 * h + 128] = dog[i].astype(MXU)
                dwgu_ref[:, 64 * h:64 * h + 64] += dwgu[i]
                dbg_ref[:, 64 * h:64 * h + 64] += dbg[i]
            for j in range(H // 2):
                dp_ref[b, :, 128 * j:128 * j + 128] = jnp.concatenate(
                    [dq[H * b + 2 * j], dq[H * b + 2 * j + 1]], axis=-1).astype(MXU)
                dp_ref[b, :, 256 + 128 * j:256 + 128 * j + 128] = jnp.concatenate(
                    [dk[H * b + 2 * j], dk[H * b + 2 * j + 1]], axis=-1).astype(MXU)
            dp_ref[b, :, 1536:1664] = (dlr[H * b] + dlr[H * b + 1] + dlr[H * b + 2] + dlr[H * b + 3]).astype(MXU)

    return pl.pallas_call(
        body, name="gla_bwd", grid=(nc,),
        in_specs=specs + [pl.BlockSpec((1, nh, GLA_DK, DV), lambda n: (n_of(n), 0, 0, 0)),
                          pl.BlockSpec((bsz, CHUNK, 512), lambda n: (0, n_of(n), 0)),
                          _full((128, 256)), _full((1, 256)), _full((1, 128))],
        out_specs=[pl.BlockSpec((bsz, CHUNK, W_GLA), lambda n: (0, n_of(n), 0)),
                   _full((128, 256)), _full((1, 256)), _full((1, 128))],
        out_shape=[jax.ShapeDtypeStruct(pg.shape, MXU), jax.ShapeDtypeStruct((128, 256), F32),
                   jax.ShapeDtypeStruct((1, 256), F32), jax.ShapeDtypeStruct((1, 128), F32)],
        scratch_shapes=[pltpu.VMEM((nh, GLA_DK, DV), F32)],
        compiler_params=_params(("arbitrary",)),
    )(pg, pg, pg, pg, pg, s_hist, dyin, wgu, bg, nw)


_GDN_COLS = [(512, 0), (512, 1), (512, 2), (512, 0), (128, 4)]


def _gdn_args(refs, bsz):
    q_ref, k_ref, v_ref, og_ref, ab_ref, sc_ref, nw_ref = refs
    return (_heads(q_ref, bsz, 128), _heads(k_ref, bsz, 128), _heads(v_ref, bsz, 128), _heads(ab_ref, bsz, 1),
            _heads(ab_ref, bsz, 1, off=H), _heads(og_ref, bsz, 128), _per_head(sc_ref, bsz, 1, slice(0, 1)),
            _per_head(sc_ref, bsz, 1, slice(1, 2)), nw_ref[...])


def _gdn_fwd_call(conv, pd, sc, nw):
    bsz, t, _ = conv.shape
    nc = t // CHUNK
    nh = bsz * H
    _, specs = _chunk_specs(nc, False, bsz, _GDN_COLS)

    def body(q_ref, k_ref, v_ref, og_ref, ab_ref, sc_ref, nw_ref, y_ref, sh_ref, th_ref, s_ref):
        @pl.when(pl.program_id(0) == 0)
        def _():
            s_ref[...] = jnp.zeros_like(s_ref)

        q, k, v, a, bb, og, alog, dtb, nw_ = _gdn_args((q_ref, k_ref, v_ref, og_ref, ab_ref, sc_ref, nw_ref), bsz)
        s = s_ref[...]
        sh_ref[0] = s
        y, s_new, tinv = gdn_chunk(q, k, v, a, bb, og, s, alog, dtb, nw_)
        th_ref[0] = tinv
        s_ref[...] = s_new
        for b in range(bsz):
            for h in range(H):
                y_ref[b, :, 128 * h:128 * h + 128] = y[H * b + h].astype(MXU)

    return pl.pallas_call(
        body, name="gdn_fwd", grid=(nc,),
        in_specs=specs + [_full((2, 128)), _full((1, 128))],
        out_specs=[pl.BlockSpec((bsz, CHUNK, 512), lambda n: (0, n, 0)),
                   pl.BlockSpec((1, nh, DV, DV), lambda n: (n, 0, 0, 0)),
                   pl.BlockSpec((1, nh, CHUNK, CHUNK), lambda n: (n, 0, 0, 0))],
        out_shape=[jax.ShapeDtypeStruct((bsz, t, 512), MXU), jax.ShapeDtypeStruct((nc, nh, DV, DV), F32),
                   jax.ShapeDtypeStruct((nc, nh, CHUNK, CHUNK), F32)],
        scratch_shapes=[pltpu.VMEM((nh, DV, DV), F32)],
        compiler_params=_params(("arbitrary",)),
    )(conv, conv, conv, pd, pd, sc, nw)


def _gdn_bwd_call(conv, pd, s_hist, t_hist, dyin, sc, nw):
    bsz, t, _ = conv.shape
    nc = t // CHUNK
    nh = bsz * H
    n_of, specs = _chunk_specs(nc, True, bsz, _GDN_COLS)

    def body(q_ref, k_ref, v_ref, og_ref, ab_ref, sh_ref, th_ref, dy_ref, sc_ref, nw_ref,
             dc_ref, dpd_ref, dsc_ref, dnw_ref, ds_ref):
        @pl.when(pl.program_id(0) == 0)
        def _():
            dsc_ref[...] = jnp.zeros_like(dsc_ref)
            dnw_ref[...] = jnp.zeros_like(dnw_ref)
            ds_ref[...] = jnp.zeros_like(ds_ref)

        q, k, v, a, bb, og, alog, dtb, nw_ = _gdn_args((q_ref, k_ref, v_ref, og_ref, ab_ref, sc_ref, nw_ref), bsz)
        _, vjp = jax.vjp(functools.partial(gdn_chunk, tinv=th_ref[0]), q, k, v, a, bb, og, sh_ref[0], alog, dtb, nw_)
        dq, dk, dv, da, db, dog, ds, dalog, ddtb, dnw = vjp((_heads(dy_ref, bsz, 128), ds_ref[...]))
        ds_ref[...] = ds
        dnw_ref[...] += dnw
        lane = lax.broadcasted_iota(jnp.int32, (CHUNK, 128), 1)
        for b in range(bsz):
            dab = jnp.zeros((CHUNK, 128), F32)
            for h in range(H):
                i = H * b + h
                dc_ref[b, :, 128 * h:128 * h + 128] = dq[i]
                dc_ref[b, :, 512 + 128 * h:512 + 128 * h + 128] = dk[i]
                dc_ref[b, :, 1024 + 128 * h:1024 + 128 * h + 128] = dv[i]
                dpd_ref[b, :, 128 * h:128 * h + 128] = dog[i].astype(MXU)
                dab = dab + jnp.where(lane == h, da[i], 0.0) + jnp.where(lane == H + h, db[i], 0.0)
                dsc_ref[0:1, h:h + 1] += dalog[i]
                dsc_ref[1:2, h:h + 1] += ddtb[i]
            dpd_ref[b, :, 512:640] = dab.astype(MXU)

    return pl.pallas_call(
        body, name="gdn_bwd", grid=(nc,),
        in_specs=specs + [pl.BlockSpec((1, nh, DV, DV), lambda n: (n_of(n), 0, 0, 0)),
                          pl.BlockSpec((1, nh, CHUNK, CHUNK), lambda n: (n_of(n), 0, 0, 0)),
                          pl.BlockSpec((bsz, CHUNK, 512), lambda n: (0, n_of(n), 1)),
                          _full((2, 128)), _full((1, 128))],
        out_specs=[pl.BlockSpec((bsz, CHUNK, W_GQKV), lambda n: (0, n_of(n), 0)),
                   pl.BlockSpec((bsz, CHUNK, W_GDN), lambda n: (0, n_of(n), 0)),
                   _full((2, 128)), _full((1, 128))],
        out_shape=[jax.ShapeDtypeStruct(conv.shape, F32), jax.ShapeDtypeStruct(pd.shape, MXU),
                   jax.ShapeDtypeStruct((2, 128), F32), jax.ShapeDtypeStruct((1, 128), F32)],
        scratch_shapes=[pltpu.VMEM((nh, DV, DV), F32)],
        compiler_params=_params(("arbitrary",)),
    )(conv, conv, conv, pd, pd, s_hist, t_hist, dyin, sc, nw)


def _head_call(x, ya, yb, wout, mod3, lnw, lnb, tgt):
    bsz, t, _ = x.shape
    tm = min(256, t)

    def body(x_ref, ya_ref, yb_ref, w_ref, mod_ref, lnw_ref, lnb_ref, t_ref,
             dyin_ref, dxa_ref, dgate_ref, dw_ref, dlnw_ref, dlnb_ref, loss_ref):
        b, i = pl.program_id(0), pl.program_id(1)

        @pl.when((b == 0) & (i == 0))
        def _():
            dw_ref[...] = jnp.zeros_like(dw_ref)
            dlnw_ref[...] = jnp.zeros_like(dlnw_ref)
            dlnb_ref[...] = jnp.zeros_like(dlnb_ref)
            loss_ref[...] = jnp.zeros_like(loss_ref)

        @pl.when(i == 0)
        def _():
            dgate_ref[...] = jnp.zeros_like(dgate_ref)

        yin = jnp.concatenate([ya_ref[0], yb_ref[0]], axis=-1).astype(MXU)
        w = w_ref[...]
        y = jnp.dot(yin, w, preferred_element_type=F32)
        loss, vjp = jax.vjp(head_fn, x_ref[0], y, mod_ref[0, 2:3, :], lnw_ref[...], lnb_ref[...], t_ref[0])
        dx, dy, dgate, dlnw, dlnb, _ = vjp(jnp.ones((1, 1), F32))
        dyb = dy.astype(MXU)
        dyin_ref[0] = lax.dot_general(dyb, w, (((1,), (1,)), ((), ())), preferred_element_type=F32)
        dw_ref[...] += lax.dot_general(yin, dyb, (((0,), (0,)), ((), ())), preferred_element_type=F32)
        dxa_ref[0] = dx
        dgate_ref[0] += dgate
        dlnw_ref[...] += dlnw
        dlnb_ref[...] += dlnb
        loss_ref[...] += jnp.broadcast_to(loss, (1, 128))

    tok = lambda w, j=0: pl.BlockSpec((1, tm, w), lambda b, i: (b, i, j))
    row = pl.BlockSpec((1, D), lambda b, i: (0, 0))
    return pl.pallas_call(
        body, name="head", grid=(bsz, t // tm),
        in_specs=[tok(D), tok(512), tok(512), pl.BlockSpec((D, D), lambda b, i: (0, 0)),
                  pl.BlockSpec((1, 3, D), lambda b, i: (b, 0, 0)), row, row, tok(D)],
        out_specs=[tok(D), tok(D), pl.BlockSpec((1, 1, D), lambda b, i: (b, 0, 0)),
                   pl.BlockSpec((D, D), lambda b, i: (0, 0)), row, row, pl.BlockSpec((1, 128), lambda b, i: (0, 0))],
        out_shape=[jax.ShapeDtypeStruct(x.shape, F32), jax.ShapeDtypeStruct(x.shape, F32),
                   jax.ShapeDtypeStruct((bsz, 1, D), F32), jax.ShapeDtypeStruct((D, D), F32),
                   jax.ShapeDtypeStruct((1, D), F32), jax.ShapeDtypeStruct((1, D), F32),
                   jax.ShapeDtypeStruct((1, 128), F32)],
        compiler_params=_params(("arbitrary", "arbitrary")),
    )(x, ya, yb, wout, mod3, lnw, lnb, tgt)


def _dh_call(dpg, dpq, dpd, wpt, x, mod3, dxa):
    bsz, t, _ = x.shape
    tm = min(256, t)

    def body(dg_ref, dq_ref, dd_ref, w_ref, x_ref, mod_ref, dxa_ref, gx_ref, dmod_ref):
        @pl.when(pl.program_id(1) == 0)
        def _():
            dmod_ref[...] = jnp.zeros_like(dmod_ref)

        mm = lambda a, lo, hi: jnp.dot(a.astype(MXU), w_ref[lo:hi, :], preferred_element_type=F32)
        dh = mm(dg_ref[0], 0, W_GLA) + mm(dq_ref[0], W_GLA, W_GLA + W_GQKV) + mm(dd_ref[0], W_GLA + W_GQKV, PW)
        gx_ref[0] = dh * (1.0 + mod_ref[0, 1:2, :]) + dxa_ref[0]
        dmod_ref[0, 0:1, :] += jnp.sum(dh, axis=0, keepdims=True)
        dmod_ref[0, 1:2, :] += jnp.sum(dh * x_ref[0], axis=0, keepdims=True)

    tok = lambda w: pl.BlockSpec((1, tm, w), lambda b, i: (b, i, 0))
    return pl.pallas_call(
        body, name="dh", grid=(bsz, t // tm),
        in_specs=[tok(W_GLA), tok(W_GQKV), tok(W_GDN), pl.BlockSpec((PW, D), lambda b, i: (0, 0)), tok(D),
                  pl.BlockSpec((1, 3, D), lambda b, i: (b, 0, 0)), tok(D)],
        out_specs=[tok(D), pl.BlockSpec((1, 2, D), lambda b, i: (b, 0, 0))],
        out_shape=[jax.ShapeDtypeStruct(x.shape, F32), jax.ShapeDtypeStruct((bsz, 2, D), F32)],
        compiler_params=_params(("parallel", "arbitrary")),
    )(dpg, dpq, dpd, wpt, x, mod3, dxa)


def _dw_call(x, mod3, dp):
    bsz, t, _ = x.shape
    width = dp.shape[-1]
    tm = min(512, t)

    def body(x_ref, mod_ref, dp_ref, dw_ref):
        @pl.when((pl.program_id(0) == 0) & (pl.program_id(1) == 0))
        def _():
            dw_ref[...] = jnp.zeros_like(dw_ref)

        h = x_ref[0] * (1.0 + mod_ref[0, 1:2, :]) + mod_ref[0, 0:1, :]
        dw_ref[...] += lax.dot_general(dp_ref[0].astype(MXU), h.astype(MXU), (((0,), (0,)), ((), ())),
                                       preferred_element_type=F32)

    tok = lambda w: pl.BlockSpec((1, tm, w), lambda b, i: (b, i, 0))
    return pl.pallas_call(
        body, name=f"dw{width}", grid=(bsz, t // tm),
        in_specs=[tok(D), pl.BlockSpec((1, 3, D), lambda b, i: (b, 0, 0)), tok(width)],
        out_specs=pl.BlockSpec((width, D), lambda b, i: (0, 0)),
        out_shape=jax.ShapeDtypeStruct((width, D), F32),
        compiler_params=_params(("arbitrary", "arbitrary")),
    )(x, mod3, dp)


def _adamw(w, g, m, v):
    m = ADAM_B1 * m + (1.0 - ADAM_B1) * g
    v = ADAM_B2 * v + (1.0 - ADAM_B2) * jnp.square(g)
    m_hat = m / (1.0 - ADAM_B1 ** ADAM_STEP)
    v_hat = v / (1.0 - ADAM_B2 ** ADAM_STEP)
    delta = -ADAM_LR * (m_hat / (jnp.sqrt(v_hat) + ADAM_EPS) + ADAM_WD * w)
    return delta, m, v


def _sum8(ref):
    g = ref[0].astype(F32)
    for j in range(1, NDEV):
        g = g + ref[j].astype(F32)
    return g


def _adam_sum_call(name, g8, w, m, v, cols):
    r, c = w.shape

    def body(g_ref, w_ref, m_ref, v_ref, go_ref, d_ref, mo_ref, vo_ref):
        g = _sum8(g_ref)
        go_ref[...] = g
        d_ref[...], mo_ref[...], vo_ref[...] = _adamw(w_ref[...], g, m_ref[...], v_ref[...])

    blk = pl.BlockSpec((r, cols), lambda i: (0, i))
    return pl.pallas_call(
        body, name=name, grid=(c // cols,),
        in_specs=[pl.BlockSpec((NDEV, r, cols), lambda i: (0, 0, i)), blk, blk, blk],
        out_specs=[blk] * 4, out_shape=[jax.ShapeDtypeStruct((r, c), F32)] * 4,
        compiler_params=_params(("parallel",)),
    )(g8, w, m, v)


def _adam_ada_call(c_all, dmod_cols, w, m, v):
    def body(c_ref, dm_ref, w_ref, m_ref, v_ref, go_ref, d_ref, mo_ref, vo_ref):
        g = lax.dot_general(c_ref[...].astype(MXU), dm_ref[...].astype(MXU), (((0,), (0,)), ((), ())),
                            preferred_element_type=F32)
        go_ref[...] = g
        d_ref[...], mo_ref[...], vo_ref[...] = _adamw(w_ref[...], g, m_ref[...], v_ref[...])

    return pl.pallas_call(
        body, name="adam_ada", out_shape=[jax.ShapeDtypeStruct(w.shape, F32)] * 4, compiler_params=_params(),
    )(c_all, dmod_cols, w, m, v)


def _adam_small_call(g, w, m, v):
    def body(g_ref, w_ref, m_ref, v_ref, d_ref, mo_ref, vo_ref):
        d_ref[...], mo_ref[...], vo_ref[...] = _adamw(w_ref[...], g_ref[...], m_ref[...], v_ref[...])

    return pl.pallas_call(
        body, name="adam_small", out_shape=[jax.ShapeDtypeStruct(w.shape, F32)] * 3, compiler_params=_params(),
    )(g, w, m, v)


def _small_sum_call(sp_all):
    def body(sp_ref, dmod_ref, sum_ref):
        acc = sp_ref[0, 2:3, :]
        for j in range(1, NDEV):
            acc = acc + sp_ref[j, 2:3, :]
        sum_ref[...] = acc
        for j in range(NDEV):
            dmod_ref[2 * j:2 * j + 2, :] = sp_ref[j, 0:2, :]

    return pl.pallas_call(
        body, name="small_sum", out_shape=[jax.ShapeDtypeStruct((2 * NDEV, SPW), F32), jax.ShapeDtypeStruct((1, SPW), F32)],
        compiler_params=_params(),
    )(sp_all)


def _mesh_pos():
    x, y, c = lax.axis_index("x"), lax.axis_index("y"), lax.axis_index("c")
    return x, y, c, 4 * x + 2 * y + c


def _peer(x, y, c, k):
    px = 1 - x if k & 4 else x
    py = 1 - y if k & 2 else y
    pc = 1 - c if k & 1 else c
    return (px, py, pc), 4 * px + 2 * py + pc


_ANY = pl.BlockSpec(memory_space=pl.ANY)
_VMEM = pl.BlockSpec(memory_space=pltpu.VMEM)


def _gather_call(c8, w_ada, b_sh, w_in, w_out, conv_w, wgu):
    big = [w_in, w_out, conv_w, wgu]
    nb = len(big)

    def body(c_ref, wada_ref, b_ref, *rest):
        srcs, outs = rest[:nb], rest[nb:2 * nb]
        call_ref, mod_ref, modp, send_sems, recv_sems, loc_sems = rest[2 * nb:]
        x, y, c, me = _mesh_pos()

        def remote(src, dst, a, k, dev):
            return pltpu.make_async_remote_copy(src_ref=src, dst_ref=dst, send_sem=send_sems.at[a, k],
                                                recv_sem=recv_sems.at[a, k], device_id=dev,
                                                device_id_type=pl.DeviceIdType.MESH)

        sends = []
        call_ref[me] = c_ref[...]
        for k in range(1, NDEV):
            dev, _ = _peer(x, y, c, k)
            sends.append(remote(c_ref, call_ref.at[me], nb, k, dev))
            sends[-1].start()
        local = [pltpu.make_async_copy(srcs[a], outs[a].at[me], loc_sems.at[a]) for a in range(nb)]
        for cp in local:
            cp.start()
        for k in range(1, NDEV):
            dev, _ = _peer(x, y, c, k)
            for a in range(nb):
                sends.append(remote(srcs[a], outs[a].at[me], a, k, dev))
                sends[-1].start()
        for k in range(1, NDEV):
            dev, pidx = _peer(x, y, c, k)
            remote(c_ref, call_ref.at[pidx], nb, k, dev).wait_recv()
        modp[...] = jnp.dot(call_ref[...].reshape(NDEV * 8, D).astype(MXU), wada_ref[...].astype(MXU),
                            preferred_element_type=F32) + b_ref[...]
        mod_ref[me] = modp[pl.ds(pl.multiple_of(me * 8, 8), 8), :]
        for k in range(1, NDEV):
            dev, pidx = _peer(x, y, c, k)
            sends.append(remote(modp.at[pl.ds(pl.multiple_of(pidx * 8, 8), 8), :], mod_ref.at[me], nb + 1, k, dev))
            sends[-1].start()
        for k in range(1, NDEV):
            dev, pidx = _peer(x, y, c, k)
            for a in range(nb):
                remote(srcs[a], outs[a].at[pidx], a, k, dev).wait_recv()
            remote(modp.at[pl.ds(0, 8), :], mod_ref.at[pidx], nb + 1, k, dev).wait_recv()
        for cp in sends:
            cp.wait_send()
        for cp in local:
            cp.wait()

    out_shape = [jax.ShapeDtypeStruct((NDEV,) + a.shape, a.dtype) for a in big]
    out_shape += [jax.ShapeDtypeStruct((NDEV, 8, D), F32), jax.ShapeDtypeStruct((NDEV, 8, SHARD_ADA), F32)]
    return pl.pallas_call(
        body, name="gather", out_shape=out_shape,
        in_specs=[_VMEM, _VMEM, _VMEM] + [_ANY] * nb,
        out_specs=[_ANY] * nb + [_VMEM, _VMEM],
        scratch_shapes=[pltpu.VMEM((NDEV * 8, SHARD_ADA), F32), pltpu.SemaphoreType.DMA((nb + 2, NDEV)),
                        pltpu.SemaphoreType.DMA((nb + 2, NDEV)), pltpu.SemaphoreType.DMA((nb,))],
        compiler_params=_params(),
    )(c8, w_ada, b_sh, *big)


def _reduce_call(blocks, sp):
    nb = len(blocks)

    def body(sp_ref, *rest):
        srcs, outs = rest[:nb], rest[nb:2 * nb]
        spall_ref, send_sems, recv_sems, loc_sems = rest[2 * nb:]
        x, y, c, me = _mesh_pos()

        def remote(src, dst, a, k, dev):
            return pltpu.make_async_remote_copy(src_ref=src, dst_ref=dst, send_sem=send_sems.at[a, k],
                                                recv_sem=recv_sems.at[a, k], device_id=dev,
                                                device_id_type=pl.DeviceIdType.MESH)

        sends = []
        spall_ref[me] = sp_ref[...]
        local = [pltpu.make_async_copy(srcs[a].at[me], outs[a].at[me], loc_sems.at[a]) for a in range(nb)]
        for cp in local:
            cp.start()
        for k in range(1, NDEV):
            dev, pidx = _peer(x, y, c, k)
            sends.append(remote(sp_ref, spall_ref.at[me], nb, k, dev))
            sends[-1].start()
            for a in range(nb):
                sends.append(remote(srcs[a].at[pidx], outs[a].at[me], a, k, dev))
                sends[-1].start()
        for k in range(1, NDEV):
            dev, pidx = _peer(x, y, c, k)
            remote(sp_ref, spall_ref.at[pidx], nb, k, dev).wait_recv()
            for a in range(nb):
                remote(srcs[a].at[pidx], outs[a].at[pidx], a, k, dev).wait_recv()
        for cp in sends:
            cp.wait_send()
        for cp in local:
            cp.wait()

    return pl.pallas_call(
        body, name="reduce", out_shape=[jax.ShapeDtypeStruct(a.shape, a.dtype) for a in blocks]
        + [jax.ShapeDtypeStruct((NDEV, 8, SPW), F32)],
        in_specs=[_VMEM] + [_ANY] * nb, out_specs=[_ANY] * nb + [_VMEM],
        scratch_shapes=[pltpu.SemaphoreType.DMA((nb + 1, NDEV)), pltpu.SemaphoreType.DMA((nb + 1, NDEV)),
                        pltpu.SemaphoreType.DMA((nb,))],
        compiler_params=_params(),
    )(sp, *blocks)


_HBM = pl.BlockSpec(memory_space=pltpu.HBM)
_SEM = pl.BlockSpec(memory_space=pltpu.SEMAPHORE)
_EFFECT = pltpu.SideEffectType.DATAFLOW_SIDE_EFFECTING


def _xchg_start(blocks, lands):
    nb = len(blocks)

    def body(*refs):
        srcs, dsts = refs[:nb], refs[nb:2 * nb]
        send_sems, recv_sems = refs[2 * nb], refs[2 * nb + 1]
        token = refs[-1]
        x, y, c, me = _mesh_pos()
        for k in range(1, NDEV):
            dev, pidx = _peer(x, y, c, k)
            for a in range(nb):
                pltpu.make_async_remote_copy(src_ref=srcs[a].at[pidx], dst_ref=dsts[a].at[me],
                                             send_sem=send_sems.at[NDEV * a + k], recv_sem=recv_sems.at[NDEV * a + k],
                                             device_id=dev, device_id_type=pl.DeviceIdType.MESH).start()
        token[...] = jnp.zeros_like(token)

    thru = [pltpu.HBM(a.shape, a.dtype) for a in list(blocks) + list(lands)]
    return pl.pallas_call(
        body, name="xchg_start",
        out_shape=(pltpu.SemaphoreType.DMA((nb * NDEV,)), pltpu.SemaphoreType.DMA((nb * NDEV,)), *thru,
                   jax.ShapeDtypeStruct((8, 128), F32)),
        in_specs=[_HBM] * (2 * nb), out_specs=(_SEM, _SEM, *([_HBM] * (2 * nb)), _VMEM),
        input_output_aliases={i: 2 + i for i in range(2 * nb)},
        compiler_params=pltpu.CompilerParams(has_side_effects=_EFFECT),
    )(*[pltpu.with_memory_space_constraint(a, pltpu.HBM) for a in list(blocks) + list(lands)])


def _xchg_wait(send_sems, recv_sems, thru, after):
    nb = len(thru) // 2

    def body(*refs):
        srcs, dsts = refs[:nb], refs[nb:2 * nb]
        send_sems, recv_sems = refs[2 * nb], refs[2 * nb + 1]
        x, y, c, me = _mesh_pos()
        for k in range(1, NDEV):
            dev, pidx = _peer(x, y, c, k)
            for a in range(nb):
                cp = pltpu.make_async_remote_copy(src_ref=srcs[a].at[pidx], dst_ref=dsts[a].at[pidx],
                                                  send_sem=send_sems.at[NDEV * a + k], recv_sem=recv_sems.at[NDEV * a + k],
                                                  device_id=dev, device_id_type=pl.DeviceIdType.MESH)
                cp.wait_send()
                cp.wait_recv()

    out = pl.pallas_call(
        body, name="xchg_wait", out_shape=tuple(pltpu.HBM(a.shape, a.dtype) for a in thru),
        in_specs=[_HBM] * (2 * nb) + [_SEM, _SEM, pl.BlockSpec(memory_space=pl.ANY)], out_specs=tuple([_HBM] * (2 * nb)),
        input_output_aliases={i: i for i in range(2 * nb)},
        compiler_params=pltpu.CompilerParams(has_side_effects=_EFFECT),
    )(*thru, send_sems, recv_sems, after)
    return out[nb:]


def _pad_cols(a, n):
    return jnp.pad(a, ((0, 0), (0, n - a.shape[1])))


def _assemble_wt(wt_full):
    q, k, v, lr, og, gqkv, ab, dog = jnp.split(wt_full, [256, 512, 1024, 1040, 1552, 3088, 3096], axis=0)
    z = lambda n: jnp.zeros((n, wt_full.shape[1]), wt_full.dtype)
    return jnp.concatenate([q, k, v, og, lr, z(112), gqkv, dog, ab, z(120)], axis=0)


def _disassemble_dwt(dw_gla, dw_gqkv, dw_gdn):
    return jnp.concatenate([dw_gla[:1024], dw_gla[1536:1552], dw_gla[1024:1536], dw_gqkv,
                            dw_gdn[512:520], dw_gdn[:512]], axis=0)


def local_grads(x, mod3, wp, wout, conv_w, wgu_p, bg, gla_nw, sc, gdn_nw, lnw, lnb, tgt):
    pg, pq, pd = _proj_call(x, mod3, wp)
    conv = _conv_fwd_call(pq, conv_w)
    ya, s_gla = _gla_fwd_call(pg, wgu_p, bg, gla_nw)
    yb, s_gdn, t_gdn = _gdn_fwd_call(conv, pd, sc, gdn_nw)
    dyin, dxa, dgate, dwout, dlnw, dlnb, loss = _head_call(x, ya, yb, wout, mod3, lnw, lnb, tgt)
    dpg, dwgu, dbg, dnw_gla = _gla_bwd_call(pg, s_gla, dyin, wgu_p, bg, gla_nw)
    dconv, dpd, dsc, dnw_gdn = _gdn_bwd_call(conv, pd, s_gdn, t_gdn, dyin, sc, gdn_nw)
    dpq, dconv_w = _conv_bwd_call(dconv, pq, conv_w)
    dw_in = _disassemble_dwt(_dw_call(x, mod3, dpg), _dw_call(x, mod3, dpq), _dw_call(x, mod3, dpd))
    g = dict(loss=loss[0, 0], dw_in=dw_in, dwout=dwout, dconv_w=dconv_w[:4], dwgu=dwgu[:16], dbg=dbg, dnw_gla=dnw_gla,
             dalog=dsc[0:1, :4], ddtb=dsc[1:2, :4], dnw_gdn=dnw_gdn, dlnw=dlnw, dlnb=dlnb)
    return g, functools.partial(_finish_grad_x, dpg, dpq, dpd, wp, x, dxa, dgate)


def _finish_grad_x(dpg, dpq, dpd, wp, x, dxa, dgate, mod3):
    gx, dmod2 = _dh_call(dpg, dpq, dpd, wp, x, mod3, dxa)
    return gx, jnp.concatenate([dmod2, dgate], axis=1)


def local_step(x, mod3, *args):
    g, finish = local_grads(x, mod3, *args)
    g["gx"], g["dmod"] = finish(mod3)
    return g


def kernel(x, c, w_ada, b_ada, w_in, gla_w_gate_up, gla_b_gate, gla_norm_w, gdn_conv_w, gdn_a_log, gdn_dt_bias, gdn_norm_w, w_out, ln_w, ln_b, loss_target, m_w_ada, m_b_ada, m_w_in, m_gla_w_gate_up, m_gla_b_gate, m_gla_norm_w, m_gdn_conv_w, m_gdn_a_log, m_gdn_dt_bias, m_gdn_norm_w, m_w_out, m_ln_w, m_ln_b, v_w_ada, v_b_ada, v_w_in, v_gla_w_gate_up, v_gla_b_gate, v_gla_norm_w, v_gdn_conv_w, v_gdn_a_log, v_gdn_dt_bias, v_gdn_norm_w, v_w_out, v_ln_w, v_ln_b):
    me = 4 * lax.axis_index("x") + 2 * lax.axis_index("y") + lax.axis_index("c")
    bsz = x.shape[0]

    b_sh = lax.dynamic_slice(b_ada, (0, me * SHARD_ADA), (1, SHARD_ADA))
    c8 = jnp.pad(c, ((0, 8 - bsz), (0, 0)))
    w_in_t, m_in_t, v_in_t = (jnp.swapaxes(a[0], 0, 1) for a in (w_in, m_w_in, v_w_in))
    win_all, wout_all, conv_all, wgu_all, c_all, mod_blk = _gather_call(
        c8, w_ada[0], b_sh, w_in_t.astype(WIRE), w_out[0].astype(WIRE), gdn_conv_w[0], gla_w_gate_up[0])
    wp = _assemble_wt(win_all.reshape(IN_COLS, D))
    wout = wout_all.reshape(D, D)
    conv_w = jnp.transpose(conv_all, (1, 0, 2)).reshape(4, W_GQKV)
    wgu_p = jnp.pad(jnp.transpose(wgu_all, (1, 0, 2)).reshape(16, 256), ((0, 112), (0, 0)))
    mod = jnp.transpose(mod_blk[:, :bsz, :], (1, 0, 2)).reshape(bsz, 3 * D)
    mod3 = mod.reshape(bsz, 3, D)
    sc = jnp.concatenate([_pad_cols(gdn_a_log, 128), _pad_cols(gdn_dt_bias, 128)], axis=0)

    g, finish = local_grads(x, mod3, wp, wout, conv_w, wgu_p, gla_b_gate, gla_norm_w, sc, gdn_norm_w, ln_w, ln_b, loss_target)

    big = [g["dw_in"].reshape(NDEV, SHARD_IN, D).astype(WIRE), g["dwout"].reshape(NDEV, D // NDEV, D).astype(WIRE)]
    lands = [lax.dynamic_update_slice(lax.empty(a.shape, a.dtype), lax.dynamic_slice(a, (me, 0, 0), (1,) + a.shape[1:]),
                                      (me, 0, 0)) for a in big]
    send_sems, recv_sems, *thru, token = _xchg_start(big, lands)
    gx, dmod = finish(mod3 + token[0, 0])
    r_in, r_out = _xchg_wait(send_sems, recv_sems, thru, gx)

    small = jnp.concatenate([g["dlnw"], g["dlnb"], g["dbg"], g["dnw_gla"], g["dnw_gdn"], _pad_cols(g["dalog"], 128),
                             _pad_cols(g["ddtb"], 128), jnp.full((1, 128), g["loss"], F32), jnp.zeros((1, 128), F32)], axis=1)
    sp = jnp.concatenate([dmod.reshape(bsz, SPW), small, jnp.zeros((8 - bsz - 1, SPW), F32)], axis=0)
    blocks = [jnp.transpose(g["dconv_w"].reshape(4, NDEV, W_GQKV // NDEV), (1, 0, 2)),
              jnp.transpose(g["dwgu"].reshape(16, NDEV, 256 // NDEV), (1, 0, 2))]
    r_conv, r_gu, sp_all = _reduce_call(blocks, sp)
    dmod_all, sums = _small_sum_call(sp_all)

    g_in, d_in, nm_in, nv_in = (jnp.swapaxes(a, 0, 1) for a in _adam_sum_call("adam_in", r_in, w_in_t, m_in_t, v_in_t, 256))
    g_out, d_out, nm_out, nv_out = _adam_sum_call("adam_out", r_out, w_out[0], m_w_out[0], v_w_out[0], D)
    g_conv, d_conv, nm_conv, nv_conv = _adam_sum_call("adam_conv", r_conv, gdn_conv_w[0], m_gdn_conv_w[0], v_gdn_conv_w[0],
                                                      W_GQKV // NDEV)
    g_gu, d_gu, nm_gu, nv_gu = _adam_sum_call("adam_gu", r_gu, gla_w_gate_up[0], m_gla_w_gate_up[0], v_gla_w_gate_up[0],
                                              256 // NDEV)
    c16 = c_all[:, :bsz, :].reshape(NDEV * bsz, D)
    g_ada, d_ada, nm_ada, nv_ada = _adam_ada_call(c16, lax.dynamic_slice(dmod_all, (0, me * SHARD_ADA), (NDEV * bsz, SHARD_ADA)),
                                                  w_ada[0], m_w_ada[0], v_w_ada[0])

    def pack(b_a, lw, lb, bgt, n1, n2, al, dt):
        return jnp.concatenate([b_a, lw, lb, bgt, n1, n2, _pad_cols(al, 128), _pad_cols(dt, 128)], axis=1).reshape(-1, 128)

    g_small = _bada_and_pack(dmod_all, sums)
    w_s = pack(b_ada, ln_w, ln_b, gla_b_gate, gla_norm_w, gdn_norm_w, gdn_a_log, gdn_dt_bias)
    m_s = pack(m_b_ada, m_ln_w, m_ln_b, m_gla_b_gate, m_gla_norm_w, m_gdn_norm_w, m_gdn_a_log, m_gdn_dt_bias)
    v_s = pack(v_b_ada, v_ln_w, v_ln_b, v_gla_b_gate, v_gla_norm_w, v_gdn_norm_w, v_gdn_a_log, v_gdn_dt_bias)
    d_s, nm_s, nv_s = _adam_small_call(g_small, w_s, m_s, v_s)

    def unpack(p):
        f = p.reshape(1, -1)
        b_a, lw, lb, bgt, n1, n2, al, dt = jnp.split(f, [3072, 4096, 5120, 5376, 5504, 5632, 5760], axis=1)
        return dict(b_ada=b_a, ln_w=lw, ln_b=lb, b_gate=bgt, gla_nw=n1, gdn_nw=n2, a_log=al[:, :4], dt_bias=dt[:, :4])

    gs, ds, ms, vs = unpack(g_small), unpack(d_s), unpack(nm_s), unpack(nv_s)
    loss = sums[0, SMALL_W]

    def group(t_ada, t_in, t_gu, t_conv, t_out, s):
        return [t_ada[None], s["b_ada"], t_in[None], t_gu[None], s["b_gate"], s["gla_nw"], t_conv[None], s["a_log"],
                s["dt_bias"], s["gdn_nw"], t_out[None], s["ln_w"], s["ln_b"]]

    return (loss, gx, *group(g_ada, g_in, g_gu, g_conv, g_out, gs), *group(d_ada, d_in, d_gu, d_conv, d_out, ds),
            *group(nm_ada, nm_in, nm_gu, nm_conv, nm_out, ms), *group(nv_ada, nv_in, nv_gu, nv_conv, nv_out, vs))


def _bada_and_pack(dmod_all, sums):
    n = dmod_all.shape[0]

    def body(dm_ref, s_ref, o_ref):
        acc = dm_ref[0:1, :]
        for j in range(1, n):
            acc = acc + dm_ref[j:j + 1, :]
        o_ref[:, 0:SPW] = acc
        o_ref[:, SPW:SPW + SMALL_W] = s_ref[:, 0:SMALL_W]

    packed = pl.pallas_call(
        body, name="bada_pack", out_shape=jax.ShapeDtypeStruct((1, SPW + SMALL_W), F32), compiler_params=_params(),
    )(dmod_all, sums)
    return packed.reshape(-1, 128)
```

```python
import functools

import jax
import jax.numpy as jnp
from jax import lax
from jax.experimental import pallas as pl
from jax.experimental.pallas import tpu as pltpu

F32 = jnp.float32
MXU = jnp.bfloat16
WIRE = jnp.bfloat16
HI = lax.Precision.HIGH

D = 1024
NDEV = 8
H = 4
GLA_DK = 64
DV = 128
CHUNK = 64
LN_EPS = 1e-5
RMS_EPS = 1e-6
ALPHA = 2.0 ** 0.25
GATE_NORM = 16.0

W_GLA, W_GQKV, W_GDN = 1664, 1536, 640
PW = W_GLA + W_GQKV + W_GDN
IN_COLS = 3608
SHARD_IN = IN_COLS // NDEV
SHARD_ADA = 3 * D // NDEV
SPW = 3 * D
SMALL_W = 2816

ADAM_LR, ADAM_B1, ADAM_B2, ADAM_EPS, ADAM_WD, ADAM_STEP = 0.001, 0.9, 0.999, 1e-08, 0.01, 10

VMEM_LIMIT = 56 * 1024 * 1024


def _params(sem=None, **kw):
    if sem is not None:
        kw["dimension_semantics"] = sem
    return pltpu.CompilerParams(vmem_limit_bytes=VMEM_LIMIT, **kw)


_MM = (((2,), (1,)), ((0,), (0,)))
_NT = (((2,), (2,)), ((0,), (0,)))
_TN = (((1,), (1,)), ((0,), (0,)))


def _dg(a, b, dims):
    return lax.dot_general(a.astype(MXU), b.astype(MXU), dims, preferred_element_type=F32)


def _hdg(a, b, dims):
    return lax.dot_general(a, b, dims, precision=HI, preferred_element_type=F32)


@jax.custom_vjp
def bmm(a, b):
    return _dg(a, b, _MM)


bmm.defvjp(lambda a, b: (_dg(a, b, _MM), (a, b)), lambda r, g: (_dg(g, r[1], _NT), _dg(r[0], g, _TN)))


@jax.custom_vjp
def bnt(a, b):
    return _dg(a, b, _NT)


bnt.defvjp(lambda a, b: (_dg(a, b, _NT), (a, b)), lambda r, g: (_dg(g, r[1], _MM), _dg(g, r[0], _TN)))


@jax.custom_vjp
def btn(a, b):
    return _dg(a, b, _TN)


btn.defvjp(lambda a, b: (_dg(a, b, _TN), (a, b)), lambda r, g: (_dg(r[1], g, _NT), _dg(r[0], g, _MM)))


def unit_lower_inverse(a):
    n = a.shape[-1]
    r, c = _iotas(n)
    p = -a
    t = (r == c).astype(F32) + p
    for _ in range(5):
        p = _hdg(p, p, _MM)
        t = t + _hdg(t, p, _MM)
    return t


@jax.custom_vjp
def unit_lower_solve(a, t, r1, r2):
    return _dg(t, r1, _MM), _dg(t, r2, _MM)


def _solve_fwd(a, t, r1, r2):
    s1, s2 = _dg(t, r1, _MM), _dg(t, r2, _MM)
    return (s1, s2), (t, s1, s2)


def _solve_bwd(res, g):
    t, s1, s2 = res
    d1, d2 = _dg(t, g[0], _TN), _dg(t, g[1], _TN)
    return -(_dg(d1, s1, _NT) + _dg(d2, s2, _NT)), jnp.zeros_like(t), d1, d2


unit_lower_solve.defvjp(_solve_fwd, _solve_bwd)


def _iotas(n):
    return lax.broadcasted_iota(jnp.int32, (n, n), 0), lax.broadcasted_iota(jnp.int32, (n, n), 1)


def _col_to_row(col, eye):
    return jnp.sum(jnp.where(eye, col, 0.0), axis=1, keepdims=True)


def _row_to_col(row, eye):
    return jnp.sum(jnp.where(eye, row, 0.0), axis=2, keepdims=True)


def _pick_row(m, i):
    r = lax.broadcasted_iota(jnp.int32, m.shape, 1)
    return jnp.sum(jnp.where(r == i, m, 0.0), axis=1, keepdims=True)


def _rms_gate(o, nw, og):
    on = o * lax.rsqrt(jnp.mean(o * o, axis=-1, keepdims=True) + RMS_EPS) * nw
    return on * jax.nn.silu(og)


def gla_chunk(q, k, v, lr, og, s, wgu, bg, nw):
    n, c, _ = q.shape
    r, cc = _iotas(c)
    causal = r >= cc
    qs = q * (GLA_DK ** -0.5)
    z = bmm(lr, wgu) + bg
    g = jax.nn.log_sigmoid(z) / GATE_NORM
    b = _hdg(jnp.broadcast_to(causal.astype(F32), (n, c, c)), g, _MM)
    bref = _pick_row(b, c // 2 - 1)
    blast = _pick_row(b, c - 1)
    att = jnp.where(causal, bnt(qs * jnp.exp(b - bref), k * jnp.exp(bref - b)), 0.0)
    o = bmm(att, v) + bmm(qs * jnp.exp(b), s)
    rk, ck = _iotas(GLA_DK)
    s_new = _row_to_col(jnp.exp(blast), rk == ck) * s + btn(k * jnp.exp(blast - b), v)
    return _rms_gate(o, nw, og), s_new


def gdn_chunk(cq, ck, cv, a, bb, og, s, alog, dtb, nw, tinv=None):
    c = cq.shape[1]
    r, cc = _iotas(c)
    eye, causal, strict = r == cc, r >= cc, r > cc
    q, k, v = jax.nn.silu(cq), jax.nn.silu(ck), jax.nn.silu(cv)
    q = q * lax.rsqrt(jnp.sum(q * q, axis=-1, keepdims=True) + RMS_EPS) * (DV ** -0.5)
    k = k * lax.rsqrt(jnp.sum(k * k, axis=-1, keepdims=True) + RMS_EPS)
    g = -jnp.exp(alog) * jax.nn.softplus(a + dtb)
    beta = jax.nn.sigmoid(bb)
    d = jnp.sum(jnp.where(causal, _col_to_row(g, eye), 0.0), axis=2, keepdims=True)
    el = jnp.exp(jnp.where(causal, d - _col_to_row(d, eye), -jnp.inf))
    kb = k * beta
    amat = jnp.where(strict, bnt(kb, k) * el, 0.0)
    t = unit_lower_inverse(amat) if tinv is None else tinv
    u, w = unit_lower_solve(amat, t, v * beta, kb * jnp.exp(d))
    qk = jnp.where(causal, bnt(q, k) * el, 0.0)
    dlast = _pick_row(d, c - 1)
    v_new = u - bmm(w, s)
    o = bmm(q * jnp.exp(d), s) + bmm(qk, v_new)
    s_new = jnp.exp(dlast) * s + btn(k * jnp.exp(dlast - d), v_new)
    y = _rms_gate(o, nw, og)
    return (y, s_new, t) if tinv is None else (y, s_new)


def head_fn(x, y, gate, lnw, lnb, tgt):
    u = ALPHA * x + (1.0 + gate) * y
    mu = jnp.mean(u, axis=-1, keepdims=True)
    var = jnp.mean(jnp.square(u - mu), axis=-1, keepdims=True)
    out = (u - mu) * lax.rsqrt(var + LN_EPS) * lnw + lnb
    err = jnp.square(out - tgt)
    return 0.5 * jnp.sum(jnp.mean(err, axis=-1, keepdims=True), axis=0, keepdims=True)


def _proj_call(x, mod3, wpt):
    bsz, t, _ = x.shape
    tm = min(256, t)

    def body(x_ref, mod_ref, w_ref, pg_ref, pq_ref, pd_ref):
        h = (x_ref[0] * (1.0 + mod_ref[0, 1:2, :]) + mod_ref[0, 0:1, :]).astype(MXU)
        nt = lambda lo, hi: lax.dot_general(h, w_ref[lo:hi, :], (((1,), (1,)), ((), ())), preferred_element_type=F32)
        pg_ref[0] = nt(0, W_GLA)
        pq_ref[0] = nt(W_GLA, W_GLA + W_GQKV)
        pd_ref[0] = nt(W_GLA + W_GQKV, PW)

    tok = lambda w: pl.BlockSpec((1, tm, w), lambda b, i: (b, i, 0))
    return pl.pallas_call(
        body, name="proj", grid=(bsz, t // tm),
        in_specs=[tok(D), pl.BlockSpec((1, 3, D), lambda b, i: (b, 0, 0)), pl.BlockSpec((PW, D), lambda b, i: (0, 0))],
        out_specs=[tok(W_GLA), tok(W_GQKV), tok(W_GDN)],
        out_shape=[jax.ShapeDtypeStruct((bsz, t, w), F32) for w in (W_GLA, W_GQKV, W_GDN)],
        compiler_params=_params(("parallel", "parallel")),
    )(x, mod3, wpt)


def _conv_fwd_call(pq, conv_w):
    bsz, t, _ = pq.shape
    tt = min(512, t)
    hb = tt // 8

    def body(x_ref, halo_ref, w_ref, o_ref, buf):
        i = pl.program_id(1)
        buf[0:8, :] = jnp.where(i > 0, halo_ref[0], 0.0)
        buf[8:, :] = x_ref[0]
        acc = w_ref[0:1, :] * buf[pl.ds(5, tt), :]
        for k in range(1, 4):
            acc = acc + w_ref[k:k + 1, :] * buf[pl.ds(5 + k, tt), :]
        o_ref[0] = acc

    return pl.pallas_call(
        body, name="conv_fwd", grid=(bsz, t // tt),
        in_specs=[pl.BlockSpec((1, tt, W_GQKV), lambda b, i: (b, i, 0)),
                  pl.BlockSpec((1, 8, W_GQKV), lambda b, i: (b, jnp.maximum(i * hb - 1, 0), 0)),
                  pl.BlockSpec((4, W_GQKV), lambda b, i: (0, 0))],
        out_specs=pl.BlockSpec((1, tt, W_GQKV), lambda b, i: (b, i, 0)),
        out_shape=jax.ShapeDtypeStruct(pq.shape, F32),
        scratch_shapes=[pltpu.VMEM((tt + 8, W_GQKV), F32)],
        compiler_params=_params(("parallel", "parallel")),
    )(pq, pq, conv_w)


def _conv_bwd_call(dconv, pq, conv_w):
    bsz, t, _ = pq.shape
    tt = min(512, t)
    hb = tt // 8
    nt_ = t // tt

    def body(d_ref, dnext_ref, x_ref, halo_ref, w_ref, din_ref, dw_ref, dbuf, xbuf):
        b, i = pl.program_id(0), pl.program_id(1)

        @pl.when((b == 0) & (i == 0))
        def _():
            dw_ref[...] = jnp.zeros_like(dw_ref)

        dbuf[0:tt, :] = d_ref[0]
        dbuf[tt:, :] = jnp.where(i < nt_ - 1, dnext_ref[0], 0.0)
        xbuf[0:8, :] = jnp.where(i > 0, halo_ref[0], 0.0)
        xbuf[8:, :] = x_ref[0]
        dout = d_ref[0]
        acc = w_ref[0:1, :] * dbuf[pl.ds(3, tt), :]
        dw_ref[0:1, :] += jnp.sum(dout * xbuf[pl.ds(5, tt), :], axis=0, keepdims=True)
        for k in range(1, 4):
            acc = acc + w_ref[k:k + 1, :] * dbuf[pl.ds(3 - k, tt), :]
            dw_ref[k:k + 1, :] += jnp.sum(dout * xbuf[pl.ds(5 + k, tt), :], axis=0, keepdims=True)
        din_ref[0] = acc.astype(MXU)

    tile = pl.BlockSpec((1, tt, W_GQKV), lambda b, i: (b, i, 0))
    return pl.pallas_call(
        body, name="conv_bwd", grid=(bsz, nt_),
        in_specs=[tile, pl.BlockSpec((1, 8, W_GQKV), lambda b, i: (b, jnp.minimum((i + 1) * hb, t // 8 - 1), 0)),
                  tile, pl.BlockSpec((1, 8, W_GQKV), lambda b, i: (b, jnp.maximum(i * hb - 1, 0), 0)),
                  pl.BlockSpec((4, W_GQKV), lambda b, i: (0, 0))],
        out_specs=[tile, pl.BlockSpec((8, W_GQKV), lambda b, i: (0, 0))],
        out_shape=[jax.ShapeDtypeStruct(pq.shape, MXU), jax.ShapeDtypeStruct((8, W_GQKV), F32)],
        scratch_shapes=[pltpu.VMEM((tt + 8, W_GQKV), F32), pltpu.VMEM((tt + 8, W_GQKV), F32)],
        compiler_params=_params(("arbitrary", "arbitrary")),
    )(dconv, dconv, pq, pq, conv_w)


def _chunk_specs(nc, rev, bsz, cols):
    n_of = (lambda n: nc - 1 - n) if rev else (lambda n: n)
    return n_of, [pl.BlockSpec((bsz, CHUNK, w), lambda n, j=j: (0, n_of(n), j)) for w, j in cols]


def _full(shape):
    return pl.BlockSpec(shape, lambda n: (0,) * len(shape))


def _heads(ref, bsz, width, off=0):
    return jnp.stack([ref[b, :, off + width * h:off + width * (h + 1)] for b in range(bsz) for h in range(H)])


def _per_head(ref, bsz, width, rows=slice(None)):
    return jnp.stack([ref[rows, width * h:width * (h + 1)] for _ in range(bsz) for h in range(H)])


_GLA_COLS = [(256, 0), (256, 1), (512, 1), (512, 2), (128, 12)]


def _gla_args(refs, bsz):
    q_ref, k_ref, v_ref, og_ref, lr_ref, wgu_ref, bg_ref, nw_ref = refs
    lr = jnp.stack([lr_ref[b] for b in range(bsz) for _ in range(H)])
    return (_heads(q_ref, bsz, 64), _heads(k_ref, bsz, 64), _heads(v_ref, bsz, 128), lr, _heads(og_ref, bsz, 128),
            _per_head(wgu_ref, bsz, 64), _per_head(bg_ref, bsz, 64), nw_ref[...])


def _gla_fwd_call(pg, wgu, bg, nw):
    bsz, t, _ = pg.shape
    nc = t // CHUNK
    nh = bsz * H
    _, specs = _chunk_specs(nc, False, bsz, _GLA_COLS)

    def body(q_ref, k_ref, v_ref, og_ref, lr_ref, wgu_ref, bg_ref, nw_ref, y_ref, sh_ref, s_ref):
        @pl.when(pl.program_id(0) == 0)
        def _():
            s_ref[...] = jnp.zeros_like(s_ref)

        q, k, v, lr, og, w, b_, nw_ = _gla_args((q_ref, k_ref, v_ref, og_ref, lr_ref, wgu_ref, bg_ref, nw_ref), bsz)
        s = s_ref[...]
        sh_ref[0] = s
        y, s_new = gla_chunk(q, k, v, lr, og, s, w, b_, nw_)
        s_ref[...] = s_new
        for b in range(bsz):
            for h in range(H):
                y_ref[b, :, 128 * h:128 * h + 128] = y[H * b + h].astype(MXU)

    return pl.pallas_call(
        body, name="gla_fwd", grid=(nc,),
        in_specs=specs + [_full((128, 256)), _full((1, 256)), _full((1, 128))],
        out_specs=[pl.BlockSpec((bsz, CHUNK, 512), lambda n: (0, n, 0)),
                   pl.BlockSpec((1, nh, GLA_DK, DV), lambda n: (n, 0, 0, 0))],
        out_shape=[jax.ShapeDtypeStruct((bsz, t, 512), MXU), jax.ShapeDtypeStruct((nc, nh, GLA_DK, DV), F32)],
        scratch_shapes=[pltpu.VMEM((nh, GLA_DK, DV), F32)],
        compiler_params=_params(("arbitrary",)),
    )(pg, pg, pg, pg, pg, wgu, bg, nw)


def _gla_bwd_call(pg, s_hist, dyin, wgu, bg, nw):
    bsz, t, _ = pg.shape
    nc = t // CHUNK
    nh = bsz * H
    n_of, specs = _chunk_specs(nc, True, bsz, _GLA_COLS)

    def body(q_ref, k_ref, v_ref, og_ref, lr_ref, sh_ref, dy_ref, wgu_ref, bg_ref, nw_ref,
             dp_ref, dwgu_ref, dbg_ref, dnw_ref, ds_ref):
        @pl.when(pl.program_id(0) == 0)
        def _():
            dwgu_ref[...] = jnp.zeros_like(dwgu_ref)
            dbg_ref[...] = jnp.zeros_like(dbg_ref)
            dnw_ref[...] = jnp.zeros_like(dnw_ref)
            ds_ref[...] = jnp.zeros_like(ds_ref)

        q, k, v, lr, og, w, b_, nw_ = _gla_args((q_ref, k_ref, v_ref, og_ref, lr_ref, wgu_ref, bg_ref, nw_ref), bsz)
        _, vjp = jax.vjp(gla_chunk, q, k, v, lr, og, sh_ref[0], w, b_, nw_)
        dq, dk, dv, dlr, dog, ds, dwgu, dbg, dnw = vjp((_heads(dy_ref, bsz, 128), ds_ref[...]))
        ds_ref[...] = ds
        dnw_ref[...] += dnw
        for b in range(bsz):
            for h in range(H):
                i = H * b + h
                dp_ref[b, :, 512 + 128 * h:512 + 128 * h + 128] = dv[i].astype(MXU)
                dp_ref[b, :, 1024 + 128 * h:1024 + 128 * h + 128] = dog[i].astype(MXU)
                dwgu_ref[:, 64 * h:64 * h + 64] += dwgu[i]
                dbg_ref[:, 64 * h:64 * h + 64] += dbg[i]
            for j in range(H // 2):
                dp_ref[b, :, 128 * j:128 * j + 128] = jnp.concatenate(
                    [dq[H * b + 2 * j], dq[H * b + 2 * j + 1]], axis=-1).astype(MXU)
                dp_ref[b, :, 256 + 128 * j:256 + 128 * j + 128] = jnp.concatenate(
                    [dk[H * b + 2 * j], dk[H * b + 2 * j + 1]], axis=-1).astype(MXU)
            dp_ref[b, :, 1536:1664] = (dlr[H * b] + dlr[H * b + 1] + dlr[H * b + 2] + dlr[H * b + 3]).astype(MXU)

    return pl.pallas_call(
        body, name="gla_bwd", grid=(nc,),
        in_specs=specs + [pl.BlockSpec((1, nh, GLA_DK, DV), lambda n: (n_of(n), 0, 0, 0)),
                          pl.BlockSpec((bsz, CHUNK, 512), lambda n: (0, n_of(n), 0)),
                          _full((128, 256)), _full((1, 256)), _full((1, 128))],
        out_specs=[pl.BlockSpec((bsz, CHUNK, W_GLA), lambda n: (0, n_of(n), 0)),
                   _full((128, 256)), _full((1, 256)), _full((1, 128))],
        out_shape=[jax.ShapeDtypeStruct(pg.shape, MXU), jax.ShapeDtypeStruct((128, 256), F32),
                   jax.ShapeDtypeStruct((1, 256), F32), jax.ShapeDtypeStruct((1, 128), F32)],
        scratch_shapes=[pltpu.VMEM((nh, GLA_DK, DV), F32)],
        compiler_params=_params(("arbitrary",)),
    )(pg, pg, pg, pg, pg, s_hist, dyin, wgu, bg, nw)


_GDN_COLS = [(512, 0), (512, 1), (512, 2), (512, 0), (128, 4)]


def _gdn_args(refs, bsz):
    q_ref, k_ref, v_ref, og_ref, ab_ref, sc_ref, nw_ref = refs
    return (_heads(q_ref, bsz, 128), _heads(k_ref, bsz, 128), _heads(v_ref, bsz, 128), _heads(ab_ref, bsz, 1),
            _heads(ab_ref, bsz, 1, off=H), _heads(og_ref, bsz, 128), _per_head(sc_ref, bsz, 1, slice(0, 1)),
            _per_head(sc_ref, bsz, 1, slice(1, 2)), nw_ref[...])


def _gdn_fwd_call(conv, pd, sc, nw):
    bsz, t, _ = conv.shape
    nc = t // CHUNK
    nh = bsz * H
    _, specs = _chunk_specs(nc, False, bsz, _GDN_COLS)

    def body(q_ref, k_ref, v_ref, og_ref, ab_ref, sc_ref, nw_ref, y_ref, sh_ref, th_ref, s_ref):
        @pl.when(pl.program_id(0) == 0)
        def _():
            s_ref[...] = jnp.zeros_like(s_ref)

        q, k, v, a, bb, og, alog, dtb, nw_ = _gdn_args((q_ref, k_ref, v_ref, og_ref, ab_ref, sc_ref, nw_ref), bsz)
        s = s_ref[...]
        sh_ref[0] = s
        y, s_new, tinv = gdn_chunk(q, k, v, a, bb, og, s, alog, dtb, nw_)
        th_ref[0] = tinv
        s_ref[...] = s_new
        for b in range(bsz):
            for h in range(H):
                y_ref[b, :, 128 * h:128 * h + 128] = y[H * b + h].astype(MXU)

    return pl.pallas_call(
        body, name="gdn_fwd", grid=(nc,),
        in_specs=specs + [_full((2, 128)), _full((1, 128))],
        out_specs=[pl.BlockSpec((bsz, CHUNK, 512), lambda n: (0, n, 0)),
                   pl.BlockSpec((1, nh, DV, DV), lambda n: (n, 0, 0, 0)),
                   pl.BlockSpec((1, nh, CHUNK, CHUNK), lambda n: (n, 0, 0, 0))],
        out_shape=[jax.ShapeDtypeStruct((bsz, t, 512), MXU), jax.ShapeDtypeStruct((nc, nh, DV, DV), F32),
                   jax.ShapeDtypeStruct((nc, nh, CHUNK, CHUNK), F32)],
        scratch_shapes=[pltpu.VMEM((nh, DV, DV), F32)],
        compiler_params=_params(("arbitrary",)),
    )(conv, conv, conv, pd, pd, sc, nw)


def _gdn_bwd_call(conv, pd, s_hist, t_hist, dyin, sc, nw):
    bsz, t, _ = conv.shape
    nc = t // CHUNK
    nh = bsz * H
    n_of, specs = _chunk_specs(nc, True, bsz, _GDN_COLS)

    def body(q_ref, k_ref, v_ref, og_ref, ab_ref, sh_ref, th_ref, dy_ref, sc_ref, nw_ref,
             dc_ref, dpd_ref, dsc_ref, dnw_ref, ds_ref):
        @pl.when(pl.program_id(0) == 0)
        def _():
            dsc_ref[...] = jnp.zeros_like(dsc_ref)
            dnw_ref[...] = jnp.zeros_like(dnw_ref)
            ds_ref[...] = jnp.zeros_like(ds_ref)

        q, k, v, a, bb, og, alog, dtb, nw_ = _gdn_args((q_ref, k_ref, v_ref, og_ref, ab_ref, sc_ref, nw_ref), bsz)
        _, vjp = jax.vjp(functools.partial(gdn_chunk, tinv=th_ref[0]), q, k, v, a, bb, og, sh_ref[0], alog, dtb, nw_)
        dq, dk, dv, da, db, dog, ds, dalog, ddtb, dnw = vjp((_heads(dy_ref, bsz, 128), ds_ref[...]))
        ds_ref[...] = ds
        dnw_ref[...] += dnw
        lane = lax.broadcasted_iota(jnp.int32, (CHUNK, 128), 1)
        for b in range(bsz):
            dab = jnp.zeros((CHUNK, 128), F32)
            for h in range(H):
                i = H * b + h
                dc_ref[b, :, 128 * h:128 * h + 128] = dq[i]
                dc_ref[b, :, 512 + 128 * h:512 + 128 * h + 128] = dk[i]
                dc_ref[b, :, 1024 + 128 * h:1024 + 128 * h + 128] = dv[i]
                dpd_ref[b, :, 128 * h:128 * h + 128] = dog[i].astype(MXU)
                dab = dab + jnp.where(lane == h, da[i], 0.0) + jnp.where(lane == H + h, db[i], 0.0)
                dsc_ref[0:1, h:h + 1] += dalog[i]
                dsc_ref[1:2, h:h + 1] += ddtb[i]
            dpd_ref[b, :, 512:640] = dab.astype(MXU)

    return pl.pallas_call(
        body, name="gdn_bwd", grid=(nc,),
        in_specs=specs + [pl.BlockSpec((1, nh, DV, DV), lambda n: (n_of(n), 0, 0, 0)),
                          pl.BlockSpec((1, nh, CHUNK, CHUNK), lambda n: (n_of(n), 0, 0, 0)),
                          pl.BlockSpec((bsz, CHUNK, 512), lambda n: (0, n_of(n), 1)),
                          _full((2, 128)), _full((1, 128))],
        out_specs=[pl.BlockSpec((bsz, CHUNK, W_GQKV), lambda n: (0, n_of(n), 0)),
                   pl.BlockSpec((bsz, CHUNK, W_GDN), lambda n: (0, n_of(n), 0)),
                   _full((2, 128)), _full((1, 128))],
        out_shape=[jax.ShapeDtypeStruct(conv.shape, F32), jax.ShapeDtypeStruct(pd.shape, MXU),
                   jax.ShapeDtypeStruct((2, 128), F32), jax.ShapeDtypeStruct((1, 128), F32)],
        scratch_shapes=[pltpu.VMEM((nh, DV, DV), F32)],
        compiler_params=_params(("arbitrary",)),
    )(conv, conv, conv, pd, pd, s_hist, t_hist, dyin, sc, nw)


def _head_call(x, ya, yb, wout, mod3, lnw, lnb, tgt):
    bsz, t, _ = x.shape
    tm = min(256, t)

    def body(x_ref, ya_ref, yb_ref, w_ref, mod_ref, lnw_ref, lnb_ref, t_ref,
             dyin_ref, dxa_ref, dgate_ref, dw_ref, dlnw_ref, dlnb_ref, loss_ref):
        b, i = pl.program_id(0), pl.program_id(1)

        @pl.when((b == 0) & (i == 0))
        def _():
            dw_ref[...] = jnp.zeros_like(dw_ref)
            dlnw_ref[...] = jnp.zeros_like(dlnw_ref)
            dlnb_ref[...] = jnp.zeros_like(dlnb_ref)
            loss_ref[...] = jnp.zeros_like(loss_ref)

        @pl.when(i == 0)
        def _():
            dgate_ref[...] = jnp.zeros_like(dgate_ref)

        yin = jnp.concatenate([ya_ref[0], yb_ref[0]], axis=-1).astype(MXU)
        w = w_ref[...]
        y = jnp.dot(yin, w, preferred_element_type=F32)
        loss, vjp = jax.vjp(head_fn, x_ref[0], y, mod_ref[0, 2:3, :], lnw_ref[...], lnb_ref[...], t_ref[0])
        dx, dy, dgate, dlnw, dlnb, _ = vjp(jnp.ones((1, 1), F32))
        dyb = dy.astype(MXU)
        dyin_ref[0] = lax.dot_general(dyb, w, (((1,), (1,)), ((), ())), preferred_element_type=F32)
        dw_ref[...] += lax.dot_general(yin, dyb, (((0,), (0,)), ((), ())), preferred_element_type=F32)
        dxa_ref[0] = dx
        dgate_ref[0] += dgate
        dlnw_ref[...] += dlnw
        dlnb_ref[...] += dlnb
        loss_ref[...] += jnp.broadcast_to(loss, (1, 128))

    tok = lambda w, j=0: pl.BlockSpec((1, tm, w), lambda b, i: (b, i, j))
    row = pl.BlockSpec((1, D), lambda b, i: (0, 0))
    return pl.pallas_call(
        body, name="head", grid=(bsz, t // tm),
        in_specs=[tok(D), tok(512), tok(512), pl.BlockSpec((D, D), lambda b, i: (0, 0)),
                  pl.BlockSpec((1, 3, D), lambda b, i: (b, 0, 0)), row, row, tok(D)],
        out_specs=[tok(D), tok(D), pl.BlockSpec((1, 1, D), lambda b, i: (b, 0, 0)),
                   pl.BlockSpec((D, D), lambda b, i: (0, 0)), row, row, pl.BlockSpec((1, 128), lambda b, i: (0, 0))],
        out_shape=[jax.ShapeDtypeStruct(x.shape, F32), jax.ShapeDtypeStruct(x.shape, F32),
                   jax.ShapeDtypeStruct((bsz, 1, D), F32), jax.ShapeDtypeStruct((D, D), F32),
                   jax.ShapeDtypeStruct((1, D), F32), jax.ShapeDtypeStruct((1, D), F32),
                   jax.ShapeDtypeStruct((1, 128), F32)],
        compiler_params=_params(("arbitrary", "arbitrary")),
    )(x, ya, yb, wout, mod3, lnw, lnb, tgt)


def _dh_call(dpg, dpq, dpd, wpt, x, mod3, dxa):
    bsz, t, _ = x.shape
    tm = min(256, t)

    def body(dg_ref, dq_ref, dd_ref, w_ref, x_ref, mod_ref, dxa_ref, gx_ref, dmod_ref):
        @pl.when(pl.program_id(1) == 0)
        def _():
            dmod_ref[...] = jnp.zeros_like(dmod_ref)

        mm = lambda a, lo, hi: jnp.dot(a.astype(MXU), w_ref[lo:hi, :], preferred_element_type=F32)
        dh = mm(dg_ref[0], 0, W_GLA) + mm(dq_ref[0], W_GLA, W_GLA + W_GQKV) + mm(dd_ref[0], W_GLA + W_GQKV, PW)
        gx_ref[0] = dh * (1.0 + mod_ref[0, 1:2, :]) + dxa_ref[0]
        dmod_ref[0, 0:1, :] += jnp.sum(dh, axis=0, keepdims=True)
        dmod_ref[0, 1:2, :] += jnp.sum(dh * x_ref[0], axis=0, keepdims=True)

    tok = lambda w: pl.BlockSpec((1, tm, w), lambda b, i: (b, i, 0))
    return pl.pallas_call(
        body, name="dh", grid=(bsz, t // tm),
        in_specs=[tok(W_GLA), tok(W_GQKV), tok(W_GDN), pl.BlockSpec((PW, D), lambda b, i: (0, 0)), tok(D),
                  pl.BlockSpec((1, 3, D), lambda b, i: (b, 0, 0)), tok(D)],
        out_specs=[tok(D), pl.BlockSpec((1, 2, D), lambda b, i: (b, 0, 0))],
        out_shape=[jax.ShapeDtypeStruct(x.shape, F32), jax.ShapeDtypeStruct((bsz, 2, D), F32)],
        compiler_params=_params(("parallel", "arbitrary")),
    )(dpg, dpq, dpd, wpt, x, mod3, dxa)


def _dw_call(x, mod3, dp):
    bsz, t, _ = x.shape
    width = dp.shape[-1]
    tm = min(512, t)

    def body(x_ref, mod_ref, dp_ref, dw_ref):
        @pl.when((pl.program_id(0) == 0) & (pl.program_id(1) == 0))
        def _():
            dw_ref[...] = jnp.zeros_like(dw_ref)

        h = x_ref[0] * (1.0 + mod_ref[0, 1:2, :]) + mod_ref[0, 0:1, :]
        dw_ref[...] += lax.dot_general(dp_ref[0].astype(MXU), h.astype(MXU), (((0,), (0,)), ((), ())),
                                       preferred_element_type=F32)

    tok = lambda w: pl.BlockSpec((1, tm, w), lambda b, i: (b, i, 0))
    return pl.pallas_call(
        body, name=f"dw{width}", grid=(bsz, t // tm),
        in_specs=[tok(D), pl.BlockSpec((1, 3, D), lambda b, i: (b, 0, 0)), tok(width)],
        out_specs=pl.BlockSpec((width, D), lambda b, i: (0, 0)),
        out_shape=jax.ShapeDtypeStruct((width, D), F32),
        compiler_params=_params(("arbitrary", "arbitrary")),
    )(x, mod3, dp)


def _adamw(w, g, m, v):
    m = ADAM_B1 * m + (1.0 - ADAM_B1) * g
    v = ADAM_B2 * v + (1.0 - ADAM_B2) * jnp.square(g)
    m_hat = m / (1.0 - ADAM_B1 ** ADAM_STEP)
    v_hat = v / (1.0 - ADAM_B2 ** ADAM_STEP)
    delta = -ADAM_LR * (m_hat / (jnp.sqrt(v_hat) + ADAM_EPS) + ADAM_WD * w)
    return delta, m, v


def _sum8(ref):
    g = ref[0].astype(F32)
    for j in range(1, NDEV):
        g = g + ref[j].astype(F32)
    return g


def _adam_sum_call(name, g8, w, m, v, cols):
    r, c = w.shape

    def body(g_ref, w_ref, m_ref, v_ref, go_ref, d_ref, mo_ref, vo_ref):
        g = _sum8(g_ref)
        go_ref[...] = g
        d_ref[...], mo_ref[...], vo_ref[...] = _adamw(w_ref[...], g, m_ref[...], v_ref[...])

    blk = pl.BlockSpec((r, cols), lambda i: (0, i))
    return pl.pallas_call(
        body, name=name, grid=(c // cols,),
        in_specs=[pl.BlockSpec((NDEV, r, cols), lambda i: (0, 0, i)), blk, blk, blk],
        out_specs=[blk] * 4, out_shape=[jax.ShapeDtypeStruct((r, c), F32)] * 4,
        compiler_params=_params(("parallel",)),
    )(g8, w, m, v)


def _adam_ada_call(c_all, dmod_cols, w, m, v):
    def body(c_ref, dm_ref, w_ref, m_ref, v_ref, go_ref, d_ref, mo_ref, vo_ref):
        g = lax.dot_general(c_ref[...].astype(MXU), dm_ref[...].astype(MXU), (((0,), (0,)), ((), ())),
                            preferred_element_type=F32)
        go_ref[...] = g
        d_ref[...], mo_ref[...], vo_ref[...] = _adamw(w_ref[...], g, m_ref[...], v_ref[...])

    return pl.pallas_call(
        body, name="adam_ada", out_shape=[jax.ShapeDtypeStruct(w.shape, F32)] * 4, compiler_params=_params(),
    )(c_all, dmod_cols, w, m, v)


def _adam_small_call(g, w, m, v):
    def body(g_ref, w_ref, m_ref, v_ref, d_ref, mo_ref, vo_ref):
        d_ref[...], mo_ref[...], vo_ref[...] = _adamw(w_ref[...], g_ref[...], m_ref[...], v_ref[...])

    return pl.pallas_call(
        body, name="adam_small", out_shape=[jax.ShapeDtypeStruct(w.shape, F32)] * 3, compiler_params=_params(),
    )(g, w, m, v)


def _small_sum_call(sp_all):
    def body(sp_ref, dmod_ref, sum_ref):
        acc = sp_ref[0, 2:3, :]
        for j in range(1, NDEV):
            acc = acc + sp_ref[j, 2:3, :]
        sum_ref[...] = acc
        for j in range(NDEV):
            dmod_ref[2 * j:2 * j + 2, :] = sp_ref[j, 0:2, :]

    return pl.pallas_call(
        body, name="small_sum", out_shape=[jax.ShapeDtypeStruct((2 * NDEV, SPW), F32), jax.ShapeDtypeStruct((1, SPW), F32)],
        compiler_params=_params(),
    )(sp_all)


def _mesh_pos():
    x, y, c = lax.axis_index("x"), lax.axis_index("y"), lax.axis_index("c")
    return x, y, c, 4 * x + 2 * y + c


def _peer(x, y, c, k):
    px = 1 - x if k & 4 else x
    py = 1 - y if k & 2 else y
    pc = 1 - c if k & 1 else c
    return (px, py, pc), 4 * px + 2 * py + pc


_ANY = pl.BlockSpec(memory_space=pl.ANY)
_VMEM = pl.BlockSpec(memory_space=pltpu.VMEM)


def _gather_call(c8, w_ada, b_sh, w_in_t):
    C_SEM, W_SEM, MOD_SEM = 0, 1, 2

    def body(c_ref, wada_ref, b_ref, win_ref, wall_ref, call_ref, mod_ref, modp, send_sems, recv_sems, loc_sem):
        x, y, c, me = _mesh_pos()

        def remote(src, dst, a, k, to):
            return pltpu.make_async_remote_copy(src_ref=src, dst_ref=dst, send_sem=send_sems.at[a, k],
                                                recv_sem=recv_sems.at[a, k], device_id=_peer(x, y, c, to)[0],
                                                device_id_type=pl.DeviceIdType.MESH)

        idx = lambda k: _peer(x, y, c, k)[1]
        sends = []
        call_ref[me] = c_ref[...]
        for k in range(1, NDEV):
            sends.append(remote(c_ref, call_ref.at[me], C_SEM, k, k))
            sends[-1].start()
        local = pltpu.make_async_copy(win_ref, wall_ref.at[me], loc_sem)
        local.start()
        for k in (1, 2, 4, 6):
            sends.append(remote(win_ref, wall_ref.at[me], W_SEM, k, k))
            sends[-1].start()
        for k in range(1, NDEV):
            remote(c_ref, call_ref.at[idx(k)], C_SEM, k, k).wait_recv()
        modp[...] = jnp.dot(call_ref[...].reshape(NDEV * 8, D).astype(MXU), wada_ref[...].astype(MXU),
                            preferred_element_type=F32) + b_ref[...]
        mod_ref[me] = modp[pl.ds(pl.multiple_of(me * 8, 8), 8), :]
        for k in range(1, NDEV):
            sends.append(remote(modp.at[pl.ds(pl.multiple_of(idx(k) * 8, 8), 8), :], mod_ref.at[me], MOD_SEM, k, k))
            sends[-1].start()
        for k in (2, 4, 6):
            remote(win_ref, wall_ref.at[idx(k)], W_SEM, k, k).wait_recv()
            sends.append(remote(wall_ref.at[idx(k)], wall_ref.at[idx(k)], W_SEM, k + 1, 1))
            sends[-1].start()
        for k in (1, 3, 5, 7):
            remote(win_ref, wall_ref.at[idx(k)], W_SEM, k, 1).wait_recv()
        for k in range(1, NDEV):
            remote(modp.at[pl.ds(0, 8), :], mod_ref.at[idx(k)], MOD_SEM, k, k).wait_recv()
        for cp in sends:
            cp.wait_send()
        local.wait()

    return pl.pallas_call(
        body, name="gather",
        out_shape=[jax.ShapeDtypeStruct((NDEV,) + w_in_t.shape, w_in_t.dtype), jax.ShapeDtypeStruct((NDEV, 8, D), F32),
                   jax.ShapeDtypeStruct((NDEV, 8, SHARD_ADA), F32)],
        in_specs=[_VMEM, _VMEM, _VMEM, _ANY], out_specs=[_ANY, _VMEM, _VMEM],
        scratch_shapes=[pltpu.VMEM((NDEV * 8, SHARD_ADA), F32), pltpu.SemaphoreType.DMA((3, NDEV)),
                        pltpu.SemaphoreType.DMA((3, NDEV)), pltpu.SemaphoreType.DMA],
        compiler_params=_params(),
    )(c8, w_ada, b_sh, w_in_t)


def _reduce_call(blocks, sp):
    nb = len(blocks)

    def body(sp_ref, *rest):
        srcs, outs = rest[:nb], rest[nb:2 * nb]
        spall_ref, send_sems, recv_sems, loc_sems = rest[2 * nb:]
        x, y, c, me = _mesh_pos()

        def remote(src, dst, a, k, dev):
            return pltpu.make_async_remote_copy(src_ref=src, dst_ref=dst, send_sem=send_sems.at[a, k],
                                                recv_sem=recv_sems.at[a, k], device_id=dev,
                                                device_id_type=pl.DeviceIdType.MESH)

        sends = []
        spall_ref[me] = sp_ref[...]
        local = [pltpu.make_async_copy(srcs[a].at[me], outs[a].at[me], loc_sems.at[a]) for a in range(nb)]
        for cp in local:
            cp.start()
        for k in range(1, NDEV):
            dev, pidx = _peer(x, y, c, k)
            sends.append(remote(sp_ref, spall_ref.at[me], nb, k, dev))
            sends[-1].start()
            for a in range(nb):
                sends.append(remote(srcs[a].at[pidx], outs[a].at[me], a, k, dev))
                sends[-1].start()
        for k in range(1, NDEV):
            dev, pidx = _peer(x, y, c, k)
            remote(sp_ref, spall_ref.at[pidx], nb, k, dev).wait_recv()
            for a in range(nb):
                remote(srcs[a].at[pidx], outs[a].at[pidx], a, k, dev).wait_recv()
        for cp in sends:
            cp.wait_send()
        for cp in local:
            cp.wait()

    return pl.pallas_call(
        body, name="reduce", out_shape=[jax.ShapeDtypeStruct(a.shape, a.dtype) for a in blocks]
        + [jax.ShapeDtypeStruct((NDEV, 8, SPW), F32)],
        in_specs=[_VMEM] + [_ANY] * nb, out_specs=[_ANY] * nb + [_VMEM],
        scratch_shapes=[pltpu.SemaphoreType.DMA((nb + 1, NDEV)), pltpu.SemaphoreType.DMA((nb + 1, NDEV)),
                        pltpu.SemaphoreType.DMA((nb,))],
        compiler_params=_params(),
    )(sp, *blocks)


_HBM = pl.BlockSpec(memory_space=pltpu.HBM)
_SEM = pl.BlockSpec(memory_space=pltpu.SEMAPHORE)
_EFFECT = pltpu.SideEffectType.DATAFLOW_SIDE_EFFECTING


def _xchg_start(name, blocks, lands, gather):
    nb = len(blocks)

    def body(*refs):
        srcs, dsts = refs[:nb], refs[nb:2 * nb]
        send_sems, recv_sems = refs[2 * nb], refs[2 * nb + 1]
        token = refs[-1]
        x, y, c, me = _mesh_pos()
        for k in range(1, NDEV):
            dev, pidx = _peer(x, y, c, k)
            for a in range(nb):
                pltpu.make_async_remote_copy(src_ref=srcs[a] if gather else srcs[a].at[pidx], dst_ref=dsts[a].at[me],
                                             send_sem=send_sems.at[NDEV * a + k], recv_sem=recv_sems.at[NDEV * a + k],
                                             device_id=dev, device_id_type=pl.DeviceIdType.MESH).start()
        token[...] = jnp.zeros_like(token)

    thru = [pltpu.HBM(a.shape, a.dtype) for a in list(blocks) + list(lands)]
    return pl.pallas_call(
        body, name=name,
        out_shape=(pltpu.SemaphoreType.DMA((nb * NDEV,)), pltpu.SemaphoreType.DMA((nb * NDEV,)), *thru,
                   jax.ShapeDtypeStruct((8, 128), F32)),
        in_specs=[_HBM] * (2 * nb), out_specs=(_SEM, _SEM, *([_HBM] * (2 * nb)), _VMEM),
        input_output_aliases={i: 2 + i for i in range(2 * nb)},
        compiler_params=pltpu.CompilerParams(has_side_effects=_EFFECT),
    )(*[pltpu.with_memory_space_constraint(a, pltpu.HBM) for a in list(blocks) + list(lands)])


def _xchg_wait(name, send_sems, recv_sems, thru, after, gather):
    nb = len(thru) // 2

    def body(*refs):
        srcs, dsts = refs[:nb], refs[nb:2 * nb]
        send_sems, recv_sems = refs[2 * nb], refs[2 * nb + 1]
        x, y, c, me = _mesh_pos()
        for k in range(1, NDEV):
            dev, pidx = _peer(x, y, c, k)
            for a in range(nb):
                cp = pltpu.make_async_remote_copy(src_ref=srcs[a] if gather else srcs[a].at[pidx], dst_ref=dsts[a].at[pidx],
                                                  send_sem=send_sems.at[NDEV * a + k], recv_sem=recv_sems.at[NDEV * a + k],
                                                  device_id=dev, device_id_type=pl.DeviceIdType.MESH)
                cp.wait_send()
                cp.wait_recv()

    out = pl.pallas_call(
        body, name=name, out_shape=tuple(pltpu.HBM(a.shape, a.dtype) for a in thru),
        in_specs=[_HBM] * (2 * nb) + [_SEM, _SEM, pl.BlockSpec(memory_space=pl.ANY)], out_specs=tuple([_HBM] * (2 * nb)),
        input_output_aliases={i: i for i in range(2 * nb)},
        compiler_params=pltpu.CompilerParams(has_side_effects=_EFFECT),
    )(*thru, send_sems, recv_sems, after)
    return out[nb:]


def _pad_cols(a, n):
    return jnp.pad(a, ((0, 0), (0, n - a.shape[1])))


def _assemble_wt(wt_full):
    q, k, v, lr, og, gqkv, ab, dog = jnp.split(wt_full, [256, 512, 1024, 1040, 1552, 3088, 3096], axis=0)
    z = lambda n: jnp.zeros((n, wt_full.shape[1]), wt_full.dtype)
    return jnp.concatenate([q, k, v, og, lr, z(112), gqkv, dog, ab, z(120)], axis=0)


def _disassemble_dwt(dw_gla, dw_gqkv, dw_gdn):
    return jnp.concatenate([dw_gla[:1024], dw_gla[1536:1552], dw_gla[1024:1536], dw_gqkv,
                            dw_gdn[512:520], dw_gdn[:512]], axis=0)


def local_grads(x, mod3, wp, late_weights, bg, gla_nw, sc, gdn_nw, lnw, lnb, tgt):
    pg, pq, pd = _proj_call(x, mod3, wp)
    wout, conv_w, wgu_p = late_weights(pq)
    conv = _conv_fwd_call(pq, conv_w)
    ya, s_gla = _gla_fwd_call(pg, wgu_p, bg, gla_nw)
    yb, s_gdn, t_gdn = _gdn_fwd_call(conv, pd, sc, gdn_nw)
    dyin, dxa, dgate, dwout, dlnw, dlnb, loss = _head_call(x, ya, yb, wout, mod3, lnw, lnb, tgt)
    dpg, dwgu, dbg, dnw_gla = _gla_bwd_call(pg, s_gla, dyin, wgu_p, bg, gla_nw)
    dconv, dpd, dsc, dnw_gdn = _gdn_bwd_call(conv, pd, s_gdn, t_gdn, dyin, sc, gdn_nw)
    dpq, dconv_w = _conv_bwd_call(dconv, pq, conv_w)
    dw_in = _disassemble_dwt(_dw_call(x, mod3, dpg), _dw_call(x, mod3, dpq), _dw_call(x, mod3, dpd))
    g = dict(loss=loss[0, 0], dw_in=dw_in, dwout=dwout, dconv_w=dconv_w[:4], dwgu=dwgu[:16], dbg=dbg, dnw_gla=dnw_gla,
             dalog=dsc[0:1, :4], ddtb=dsc[1:2, :4], dnw_gdn=dnw_gdn, dlnw=dlnw, dlnb=dlnb)
    return g, functools.partial(_finish_grad_x, dpg, dpq, dpd, wp, x, dxa, dgate)


def _finish_grad_x(dpg, dpq, dpd, wp, x, dxa, dgate, mod3):
    gx, dmod2 = _dh_call(dpg, dpq, dpd, wp, x, mod3, dxa)
    return gx, jnp.concatenate([dmod2, dgate], axis=1)


def local_step(x, mod3, wp, wout, conv_w, wgu_p, *args):
    g, finish = local_grads(x, mod3, wp, lambda _: (wout, conv_w, wgu_p), *args)
    g["gx"], g["dmod"] = finish(mod3)
    return g


def kernel(x, c, w_ada, b_ada, w_in, gla_w_gate_up, gla_b_gate, gla_norm_w, gdn_conv_w, gdn_a_log, gdn_dt_bias, gdn_norm_w, w_out, ln_w, ln_b, loss_target, m_w_ada, m_b_ada, m_w_in, m_gla_w_gate_up, m_gla_b_gate, m_gla_norm_w, m_gdn_conv_w, m_gdn_a_log, m_gdn_dt_bias, m_gdn_norm_w, m_w_out, m_ln_w, m_ln_b, v_w_ada, v_b_ada, v_w_in, v_gla_w_gate_up, v_gla_b_gate, v_gla_norm_w, v_gdn_conv_w, v_gdn_a_log, v_gdn_dt_bias, v_gdn_norm_w, v_w_out, v_ln_w, v_ln_b):
    me = 4 * lax.axis_index("x") + 2 * lax.axis_index("y") + lax.axis_index("c")
    bsz = x.shape[0]

    b_sh = lax.dynamic_slice(b_ada, (0, me * SHARD_ADA), (1, SHARD_ADA))
    c8 = jnp.pad(c, ((0, 8 - bsz), (0, 0)))
    w_in_t, m_in_t, v_in_t = (jnp.swapaxes(a[0], 0, 1) for a in (w_in, m_w_in, v_w_in))
    win_all, c_all, mod_blk = _gather_call(c8, w_ada[0], b_sh, w_in_t.astype(WIRE))
    wp = _assemble_wt(win_all.reshape(IN_COLS, D))
    mod = jnp.transpose(mod_blk[:, :bsz, :], (1, 0, 2)).reshape(bsz, 3 * D)
    mod3 = mod.reshape(bsz, 3, D)
    sc = jnp.concatenate([_pad_cols(gdn_a_log, 128), _pad_cols(gdn_dt_bias, 128)], axis=0)

    own = lambda a: lax.dynamic_update_slice(lax.empty((NDEV,) + a.shape, a.dtype), a[None], (me,) + (0,) * a.ndim)
    late = [w_out[0].astype(WIRE), gdn_conv_w[0], gla_w_gate_up[0] + 0.0 * mod_blk[0, 0, 0]]
    w_send, w_recv, *w_thru, w_token = _xchg_start("wgather_start", late, [own(a) for a in late], gather=True)

    def late_weights(pq):
        wout_all, conv_all, wgu_all = _xchg_wait("wgather_wait", w_send, w_recv, w_thru, pq, gather=True)
        return (wout_all.reshape(D, D), jnp.transpose(conv_all, (1, 0, 2)).reshape(4, W_GQKV),
                jnp.pad(jnp.transpose(wgu_all, (1, 0, 2)).reshape(16, 256), ((0, 112), (0, 0))))

    g, finish = local_grads(x, mod3 + w_token[0, 0], wp, late_weights, gla_b_gate, gla_norm_w, sc, gdn_norm_w, ln_w, ln_b,
                            loss_target)

    big = [g["dw_in"].reshape(NDEV, SHARD_IN, D).astype(WIRE), g["dwout"].reshape(NDEV, D // NDEV, D).astype(WIRE)]
    lands = [lax.dynamic_update_slice(lax.empty(a.shape, a.dtype), lax.dynamic_slice(a, (me, 0, 0), (1,) + a.shape[1:]),
                                      (me, 0, 0)) for a in big]
    send_sems, recv_sems, *thru, token = _xchg_start("xchg_start", big, lands, gather=False)
    gx, dmod = finish(mod3 + token[0, 0])
    r_in, r_out = _xchg_wait("xchg_wait", send_sems, recv_sems, thru, gx, gather=False)

    small = jnp.concatenate([g["dlnw"], g["dlnb"], g["dbg"], g["dnw_gla"], g["dnw_gdn"], _pad_cols(g["dalog"], 128),
                             _pad_cols(g["ddtb"], 128), jnp.full((1, 128), g["loss"], F32), jnp.zeros((1, 128), F32)], axis=1)
    sp = jnp.concatenate([dmod.reshape(bsz, SPW), small, jnp.zeros((8 - bsz - 1, SPW), F32)], axis=0)
    blocks = [jnp.transpose(g["dconv_w"].reshape(4, NDEV, W_GQKV // NDEV), (1, 0, 2)),
              jnp.transpose(g["dwgu"].reshape(16, NDEV, 256 // NDEV), (1, 0, 2))]
    r_conv, r_gu, sp_all = _reduce_call(blocks, sp)
    dmod_all, sums = _small_sum_call(sp_all)

    g_in, d_in, nm_in, nv_in = (jnp.swapaxes(a, 0, 1) for a in _adam_sum_call("adam_in", r_in, w_in_t, m_in_t, v_in_t, 256))
    g_out, d_out, nm_out, nv_out = _adam_sum_call("adam_out", r_out, w_out[0], m_w_out[0], v_w_out[0], D)
    g_conv, d_conv, nm_conv, nv_conv = _adam_sum_call("adam_conv", r_conv, gdn_conv_w[0], m_gdn_conv_w[0], v_gdn_conv_w[0],
                                                      W_GQKV // NDEV)
    g_gu, d_gu, nm_gu, nv_gu = _adam_sum_call("adam_gu", r_gu, gla_w_gate_up[0], m_gla_w_gate_up[0], v_gla_w_gate_up[0],
                                              256 // NDEV)
    c16 = c_all[:, :bsz, :].reshape(NDEV * bsz, D)
    g_ada, d_ada, nm_ada, nv_ada = _adam_ada_call(c16, lax.dynamic_slice(dmod_all, (0, me * SHARD_ADA), (NDEV * bsz, SHARD_ADA)),
                                                  w_ada[0], m_w_ada[0], v_w_ada[0])

    def pack(b_a, lw, lb, bgt, n1, n2, al, dt):
        return jnp.concatenate([b_a, lw, lb, bgt, n1, n2, _pad_cols(al, 128), _pad_cols(dt, 128)], axis=1).reshape(-1, 128)

    g_small = _bada_and_pack(dmod_all, sums)
    w_s = pack(b_ada, ln_w, ln_b, gla_b_gate, gla_norm_w, gdn_norm_w, gdn_a_log, gdn_dt_bias)
    m_s = pack(m_b_ada, m_ln_w, m_ln_b, m_gla_b_gate, m_gla_norm_w, m_gdn_norm_w, m_gdn_a_log, m_gdn_dt_bias)
    v_s = pack(v_b_ada, v_ln_w, v_ln_b, v_gla_b_gate, v_gla_norm_w, v_gdn_norm_w, v_gdn_a_log, v_gdn_dt_bias)
    d_s, nm_s, nv_s = _adam_small_call(g_small, w_s, m_s, v_s)

    def unpack(p):
        f = p.reshape(1, -1)
        b_a, lw, lb, bgt, n1, n2, al, dt = jnp.split(f, [3072, 4096, 5120, 5376, 5504, 5632, 5760], axis=1)
        return dict(b_ada=b_a, ln_w=lw, ln_b=lb, b_gate=bgt, gla_nw=n1, gdn_nw=n2, a_log=al[:, :4], dt_bias=dt[:, :4])

    gs, ds, ms, vs = unpack(g_small), unpack(d_s), unpack(nm_s), unpack(nv_s)
    loss = sums[0, SMALL_W]

    def group(t_ada, t_in, t_gu, t_conv, t_out, s):
        return [t_ada[None], s["b_ada"], t_in[None], t_gu[None], s["b_gate"], s["gla_nw"], t_conv[None], s["a_log"],
                s["dt_bias"], s["gdn_nw"], t_out[None], s["ln_w"], s["ln_b"]]

    return (loss, gx, *group(g_ada, g_in, g_gu, g_conv, g_out, gs), *group(d_ada, d_in, d_gu, d_conv, d_out, ds),
            *group(nm_ada, nm_in, nm_gu, nm_conv, nm_out, ms), *group(nv_ada, nv_in, nv_gu, nv_conv, nv_out, vs))


def _bada_and_pack(dmod_all, sums):
    n = dmod_all.shape[0]

    def body(dm_ref, s_ref, o_ref):
        acc = dm_ref[0:1, :]
        for j in range(1, n):
            acc = acc + dm_ref[j:j + 1, :]
        o_ref[:, 0:SPW] = acc
        o_ref[:, SPW:SPW + SMALL_W] = s_ref[:, 0:SMALL_W]

    packed = pl.pallas_call(
        body, name="bada_pack", out_shape=jax.ShapeDtypeStruct((1, SPW + SMALL_W), F32), compiler_params=_params(),
    )(dmod_all, sums)
    return packed.reshape(-1, 128)
```

```python
import functools

import jax
import jax.numpy as jnp
from jax import lax
from jax.experimental import pallas as pl
from jax.experimental.pallas import tpu as pltpu

F32 = jnp.float32
MXU = jnp.bfloat16
WIRE = jnp.bfloat16
HI = lax.Precision.HIGH

D = 1024
NDEV = 8
H = 4
GLA_DK = 64
DV = 128
CHUNK = 64
LN_EPS = 1e-5
RMS_EPS = 1e-6
ALPHA = 2.0 ** 0.25
GATE_NORM = 16.0

W_GLA, W_GQKV, W_GDN = 1664, 1536, 640
PW = W_GLA + W_GQKV + W_GDN
IN_COLS = 3608
SHARD_IN = IN_COLS // NDEV
SHARD_ADA = 3 * D // NDEV
SPW = 3 * D
SMALL_W = 2816

ADAM_LR, ADAM_B1, ADAM_B2, ADAM_EPS, ADAM_WD, ADAM_STEP = 0.001, 0.9, 0.999, 1e-08, 0.01, 10

VMEM_LIMIT = 56 * 1024 * 1024


def _params(sem=None, **kw):
    if sem is not None:
        kw["dimension_semantics"] = sem
    return pltpu.CompilerParams(vmem_limit_bytes=VMEM_LIMIT, **kw)


_MM = (((2,), (1,)), ((0,), (0,)))
_NT = (((2,), (2,)), ((0,), (0,)))
_TN = (((1,), (1,)), ((0,), (0,)))


def _dg(a, b, dims):
    return lax.dot_general(a.astype(MXU), b.astype(MXU), dims, preferred_element_type=F32)


def _hdg(a, b, dims):
    return lax.dot_general(a, b, dims, precision=HI, preferred_element_type=F32)


@jax.custom_vjp
def bmm(a, b):
    return _dg(a, b, _MM)


bmm.defvjp(lambda a, b: (_dg(a, b, _MM), (a, b)), lambda r, g: (_dg(g, r[1], _NT), _dg(r[0], g, _TN)))


@jax.custom_vjp
def bnt(a, b):
    return _dg(a, b, _NT)


bnt.defvjp(lambda a, b: (_dg(a, b, _NT), (a, b)), lambda r, g: (_dg(g, r[1], _MM), _dg(g, r[0], _TN)))


@jax.custom_vjp
def btn(a, b):
    return _dg(a, b, _TN)


btn.defvjp(lambda a, b: (_dg(a, b, _TN), (a, b)), lambda r, g: (_dg(r[1], g, _NT), _dg(r[0], g, _MM)))


def unit_lower_inverse(a):
    n = a.shape[-1]
    r, c = _iotas(n)
    p = -a
    t = (r == c).astype(F32) + p
    for _ in range(5):
        p = _hdg(p, p, _MM)
        t = t + _hdg(t, p, _MM)
    return t


@jax.custom_vjp
def unit_lower_solve(a, t, r1, r2):
    return _dg(t, r1, _MM), _dg(t, r2, _MM)


def _solve_fwd(a, t, r1, r2):
    s1, s2 = _dg(t, r1, _MM), _dg(t, r2, _MM)
    return (s1, s2), (t, s1, s2)


def _solve_bwd(res, g):
    t, s1, s2 = res
    d1, d2 = _dg(t, g[0], _TN), _dg(t, g[1], _TN)
    return -(_dg(d1, s1, _NT) + _dg(d2, s2, _NT)), jnp.zeros_like(t), d1, d2


unit_lower_solve.defvjp(_solve_fwd, _solve_bwd)


def _iotas(n):
    return lax.broadcasted_iota(jnp.int32, (n, n), 0), lax.broadcasted_iota(jnp.int32, (n, n), 1)


def _col_to_row(col, eye):
    return jnp.sum(jnp.where(eye, col, 0.0), axis=1, keepdims=True)


def _row_to_col(row, eye):
    return jnp.sum(jnp.where(eye, row, 0.0), axis=2, keepdims=True)


def _pick_row(m, i):
    r = lax.broadcasted_iota(jnp.int32, m.shape, 1)
    return jnp.sum(jnp.where(r == i, m, 0.0), axis=1, keepdims=True)


def _rms_gate(o, nw, og):
    on = o * lax.rsqrt(jnp.mean(o * o, axis=-1, keepdims=True) + RMS_EPS) * nw
    return on * jax.nn.silu(og)


def gla_chunk(q, k, v, lr, og, s, wgu, bg, nw):
    n, c, _ = q.shape
    r, cc = _iotas(c)
    causal = r >= cc
    qs = q * (GLA_DK ** -0.5)
    z = bmm(lr, wgu) + bg
    g = jax.nn.log_sigmoid(z) / GATE_NORM
    b = _hdg(jnp.broadcast_to(causal.astype(F32), (n, c, c)), g, _MM)
    bref = _pick_row(b, c // 2 - 1)
    blast = _pick_row(b, c - 1)
    att = jnp.where(causal, bnt(qs * jnp.exp(b - bref), k * jnp.exp(bref - b)), 0.0)
    o = bmm(att, v) + bmm(qs * jnp.exp(b), s)
    rk, ck = _iotas(GLA_DK)
    s_new = _row_to_col(jnp.exp(blast), rk == ck) * s + btn(k * jnp.exp(blast - b), v)
    return _rms_gate(o, nw, og), s_new


def gdn_chunk(cq, ck, cv, a, bb, og, s, alog, dtb, nw, tinv=None):
    c = cq.shape[1]
    r, cc = _iotas(c)
    eye, causal, strict = r == cc, r >= cc, r > cc
    q, k, v = jax.nn.silu(cq), jax.nn.silu(ck), jax.nn.silu(cv)
    q = q * lax.rsqrt(jnp.sum(q * q, axis=-1, keepdims=True) + RMS_EPS) * (DV ** -0.5)
    k = k * lax.rsqrt(jnp.sum(k * k, axis=-1, keepdims=True) + RMS_EPS)
    g = -jnp.exp(alog) * jax.nn.softplus(a + dtb)
    beta = jax.nn.sigmoid(bb)
    d = jnp.sum(jnp.where(causal, _col_to_row(g, eye), 0.0), axis=2, keepdims=True)
    el = jnp.exp(jnp.where(causal, d - _col_to_row(d, eye), -jnp.inf))
    kb = k * beta
    amat = jnp.where(strict, bnt(kb, k) * el, 0.0)
    t = unit_lower_inverse(amat) if tinv is None else tinv
    u, w = unit_lower_solve(amat, t, v * beta, kb * jnp.exp(d))
    qk = jnp.where(causal, bnt(q, k) * el, 0.0)
    dlast = _pick_row(d, c - 1)
    v_new = u - bmm(w, s)
    o = bmm(q * jnp.exp(d), s) + bmm(qk, v_new)
    s_new = jnp.exp(dlast) * s + btn(k * jnp.exp(dlast - d), v_new)
    y = _rms_gate(o, nw, og)
    return (y, s_new, t) if tinv is None else (y, s_new)


def head_fn(x, y, gate, lnw, lnb, tgt):
    u = ALPHA * x + (1.0 + gate) * y
    mu = jnp.mean(u, axis=-1, keepdims=True)
    var = jnp.mean(jnp.square(u - mu), axis=-1, keepdims=True)
    out = (u - mu) * lax.rsqrt(var + LN_EPS) * lnw + lnb
    err = jnp.square(out - tgt)
    return 0.5 * jnp.sum(jnp.mean(err, axis=-1, keepdims=True), axis=0, keepdims=True)


def _proj_call(x, mod3, wpt):
    bsz, t, _ = x.shape
    tm = min(512, t)

    def body(x_ref, mod_ref, w_ref, pg_ref, pq_ref, pd_ref):
        h = (x_ref[0] * (1.0 + mod_ref[0, 1:2, :]) + mod_ref[0, 0:1, :]).astype(MXU)
        nt = lambda lo, hi: lax.dot_general(h, w_ref[lo:hi, :], (((1,), (1,)), ((), ())), preferred_element_type=F32)
        pg_ref[0] = nt(0, W_GLA)
        pq_ref[0] = nt(W_GLA, W_GLA + W_GQKV)
        pd_ref[0] = nt(W_GLA + W_GQKV, PW)

    tok = lambda w: pl.BlockSpec((1, tm, w), lambda b, i: (b, i, 0))
    return pl.pallas_call(
        body, name="proj", grid=(bsz, t // tm),
        in_specs=[tok(D), pl.BlockSpec((1, 3, D), lambda b, i: (b, 0, 0)), pl.BlockSpec((PW, D), lambda b, i: (0, 0))],
        out_specs=[tok(W_GLA), tok(W_GQKV), tok(W_GDN)],
        out_shape=[jax.ShapeDtypeStruct((bsz, t, w), F32) for w in (W_GLA, W_GQKV, W_GDN)],
        compiler_params=_params(("parallel", "parallel")),
    )(x, mod3, wpt)


_CONV_ROWS = 16


def _conv_fwd_call(pq, conv_w):
    bsz, t, _ = pq.shape
    tt = min(512, t)
    hb = tt // 8

    def body(x_ref, halo_ref, w_ref, o_ref, buf):
        i = pl.program_id(1)
        buf[0:8, :] = jnp.where(i > 0, halo_ref[0], 0.0)
        buf[8:, :] = x_ref[0]
        for j in range(W_GQKV // 128):
            ls = slice(128 * j, 128 * j + 128)
            wj = [w_ref[k:k + 1, ls] for k in range(4)]

            def rows(r, carry):
                base = pl.multiple_of(r * _CONV_ROWS, _CONV_ROWS)
                win = buf[pl.ds(base, _CONV_ROWS + 8), ls]
                acc = wj[0] * win[5:5 + _CONV_ROWS, :]
                for k in range(1, 4):
                    acc = acc + wj[k] * win[5 + k:5 + k + _CONV_ROWS, :]
                o_ref[0, pl.ds(base, _CONV_ROWS), ls] = acc
                return carry

            lax.fori_loop(0, tt // _CONV_ROWS, rows, 0, unroll=4)

    return pl.pallas_call(
        body, name="conv_fwd", grid=(bsz, t // tt),
        in_specs=[pl.BlockSpec((1, tt, W_GQKV), lambda b, i: (b, i, 0)),
                  pl.BlockSpec((1, 8, W_GQKV), lambda b, i: (b, jnp.maximum(i * hb - 1, 0), 0)),
                  pl.BlockSpec((4, W_GQKV), lambda b, i: (0, 0))],
        out_specs=pl.BlockSpec((1, tt, W_GQKV), lambda b, i: (b, i, 0)),
        out_shape=jax.ShapeDtypeStruct(pq.shape, F32),
        scratch_shapes=[pltpu.VMEM((tt + 8, W_GQKV), F32)],
        compiler_params=_params(("parallel", "parallel")),
    )(pq, pq, conv_w)


def _conv_bwd_call(dconv, pq, conv_w):
    bsz, t, _ = pq.shape
    tt = min(512, t)
    hb = tt // 8
    nt_ = t // tt

    def body(d_ref, dnext_ref, x_ref, w_ref, din_ref, dw_ref, dbuf):
        b, i = pl.program_id(0), pl.program_id(1)

        @pl.when((b == 0) & (i == 0))
        def _():
            dw_ref[...] = jnp.zeros_like(dw_ref)

        dbuf[0:tt, :] = d_ref[0]
        dbuf[tt:, :] = jnp.where(i < nt_ - 1, dnext_ref[0], 0.0)
        xin = x_ref[0]
        acc = None
        for k in range(4):
            dsh = dbuf[pl.ds(3 - k, tt), :]
            acc = w_ref[k:k + 1, :] * dsh if acc is None else acc + w_ref[k:k + 1, :] * dsh
            dw_ref[k:k + 1, :] += jnp.sum(xin * dsh, axis=0, keepdims=True)
        din_ref[0] = acc.astype(MXU)

    tile = pl.BlockSpec((1, tt, W_GQKV), lambda b, i: (b, i, 0))
    return pl.pallas_call(
        body, name="conv_bwd", grid=(bsz, nt_),
        in_specs=[tile, pl.BlockSpec((1, 8, W_GQKV), lambda b, i: (b, jnp.minimum((i + 1) * hb, t // 8 - 1), 0)),
                  tile, pl.BlockSpec((4, W_GQKV), lambda b, i: (0, 0))],
        out_specs=[tile, pl.BlockSpec((8, W_GQKV), lambda b, i: (0, 0))],
        out_shape=[jax.ShapeDtypeStruct(pq.shape, MXU), jax.ShapeDtypeStruct((8, W_GQKV), F32)],
        scratch_shapes=[pltpu.VMEM((tt + 8, W_GQKV), F32)],
        compiler_params=_params(("arbitrary", "arbitrary")),
    )(dconv, dconv, pq, conv_w)


def _chunk_specs(nc, rev, bsz, cols):
    n_of = (lambda n: nc - 1 - n) if rev else (lambda n: n)
    return n_of, [pl.BlockSpec((bsz, CHUNK, w), lambda n, j=j: (0, n_of(n), j)) for w, j in cols]


def _full(shape):
    return pl.BlockSpec(shape, lambda n: (0,) * len(shape))


def _heads(ref, bsz, width, off=0):
    return jnp.stack([ref[b, :, off + width * h:off + width * (h + 1)] for b in range(bsz) for h in range(H)])


def _per_head(ref, bsz, width, rows=slice(None)):
    return jnp.stack([ref[rows, width * h:width * (h + 1)] for _ in range(bsz) for h in range(H)])


_GLA_COLS = [(256, 0), (256, 1), (512, 1), (512, 2), (128, 12)]


def _gla_args(refs, bsz):
    q_ref, k_ref, v_ref, og_ref, lr_ref, wgu_ref, bg_ref, nw_ref = refs
    lr = jnp.stack([lr_ref[b] for b in range(bsz) for _ in range(H)])
    return (_heads(q_ref, bsz, 64), _heads(k_ref, bsz, 64), _heads(v_ref, bsz, 128), lr, _heads(og_ref, bsz, 128),
            _per_head(wgu_ref, bsz, 64), _per_head(bg_ref, bsz, 64), nw_ref[...])


def _gla_fwd_call(pg, wgu, bg, nw):
    bsz, t, _ = pg.shape
    nc = t // CHUNK
    nh = bsz * H
    _, specs = _chunk_specs(nc, False, bsz, _GLA_COLS)

    def body(q_ref, k_ref, v_ref, og_ref, lr_ref, wgu_ref, bg_ref, nw_ref, y_ref, sh_ref, s_ref):
        @pl.when(pl.program_id(0) == 0)
        def _():
            s_ref[...] = jnp.zeros_like(s_ref)

        q, k, v, lr, og, w, b_, nw_ = _gla_args((q_ref, k_ref, v_ref, og_ref, lr_ref, wgu_ref, bg_ref, nw_ref), bsz)
        s = s_ref[...]
        sh_ref[0] = s
        y, s_new = gla_chunk(q, k, v, lr, og, s, w, b_, nw_)
        s_ref[...] = s_new
        for b in range(bsz):
            for h in range(H):
                y_ref[b, :, 128 * h:128 * h + 128] = y[H * b + h].astype(MXU)

    return pl.pallas_call(
        body, name="gla_fwd", grid=(nc,),
        in_specs=specs + [_full((128, 256)), _full((1, 256)), _full((1, 128))],
        out_specs=[pl.BlockSpec((bsz, CHUNK, 512), lambda n: (0, n, 0)),
                   pl.BlockSpec((1, nh, GLA_DK, DV), lambda n: (n, 0, 0, 0))],
        out_shape=[jax.ShapeDtypeStruct((bsz, t, 512), MXU), jax.ShapeDtypeStruct((nc, nh, GLA_DK, DV), F32)],
        scratch_shapes=[pltpu.VMEM((nh, GLA_DK, DV), F32)],
        compiler_params=_params(("arbitrary",)),
    )(pg, pg, pg, pg, pg, wgu, bg, nw)


def _gla_bwd_call(pg, s_hist, dyin, wgu, bg, nw):
    bsz, t, _ = pg.shape
    nc = t // CHUNK
    nh = bsz * H
    n_of, specs = _chunk_specs(nc, True, bsz, _GLA_COLS)

    def body(q_ref, k_ref, v_ref, og_ref, lr_ref, sh_ref, dy_ref, wgu_ref, bg_ref, nw_ref,
             dp_ref, dwgu_ref, dbg_ref, dnw_ref, ds_ref):
        @pl.when(pl.program_id(0) == 0)
        def _():
            dwgu_ref[...] = jnp.zeros_like(dwgu_ref)
            dbg_ref[...] = jnp.zeros_like(dbg_ref)
            dnw_ref[...] = jnp.zeros_like(dnw_ref)
            ds_ref[...] = jnp.zeros_like(ds_ref)

        q, k, v, lr, og, w, b_, nw_ = _gla_args((q_ref, k_ref, v_ref, og_ref, lr_ref, wgu_ref, bg_ref, nw_ref), bsz)
        _, vjp = jax.vjp(gla_chunk, q, k, v, lr, og, sh_ref[0], w, b_, nw_)
        dq, dk, dv, dlr, dog, ds, dwgu, dbg, dnw = vjp((_heads(dy_ref, bsz, 128), ds_ref[...]))
        ds_ref[...] = ds
        dnw_ref[...] += dnw
        for b in range(bsz):
            for h in range(H):
                i = H * b + h
                dp_ref[b, :, 512 + 128 * h:512 + 128 * h + 128] = dv[i].astype(MXU)
                dp_ref[b, :, 1024 + 128 * h:1024 + 128 * h + 128] = dog[i].astype(MXU)
                dwgu_ref[:, 64 * h:64 * h + 64] += dwgu[i]
                dbg_ref[:, 64 * h:64 * h + 64] += dbg[i]
            for j in range(H // 2):
                dp_ref[b, :, 128 * j:128 * j + 128] = jnp.concatenate(
                    [dq[H * b + 2 * j], dq[H * b + 2 * j + 1]], axis=-1).astype(MXU)
                dp_ref[b, :, 256 + 128 * j:256 + 128 * j + 128] = jnp.concatenate(
                    [dk[H * b + 2 * j], dk[H * b + 2 * j + 1]], axis=-1).astype(MXU)
            dp_ref[b, :, 1536:1664] = (dlr[H * b] + dlr[H * b + 1] + dlr[H * b + 2] + dlr[H * b + 3]).astype(MXU)

    return pl.pallas_call(
        body, name="gla_bwd", grid=(nc,),
        in_specs=specs + [pl.BlockSpec((1, nh, GLA_DK, DV), lambda n: (n_of(n), 0, 0, 0)),
                          pl.BlockSpec((bsz, CHUNK, 512), lambda n: (0, n_of(n), 0)),
                          _full((128, 256)), _full((1, 256)), _full((1, 128))],
        out_specs=[pl.BlockSpec((bsz, CHUNK, W_GLA), lambda n: (0, n_of(n), 0)),
                   _full((128, 256)), _full((1, 256)), _full((1, 128))],
        out_shape=[jax.ShapeDtypeStruct(pg.shape, MXU), jax.ShapeDtypeStruct((128, 256), F32),
                   jax.ShapeDtypeStruct((1, 256), F32), jax.ShapeDtypeStruct((1, 128), F32)],
        scratch_shapes=[pltpu.VMEM((nh, GLA_DK, DV), F32)],
        compiler_params=_params(("arbitrary",)),
    )(pg, pg, pg, pg, pg, s_hist, dyin, wgu, bg, nw)


_GDN_COLS = [(512, 0), (512, 1), (512, 2), (512, 0), (128, 4)]


def _gdn_args(refs, bsz):
    q_ref, k_ref, v_ref, og_ref, ab_ref, sc_ref, nw_ref = refs
    return (_heads(q_ref, bsz, 128), _heads(k_ref, bsz, 128), _heads(v_ref, bsz, 128), _heads(ab_ref, bsz, 1),
            _heads(ab_ref, bsz, 1, off=H), _heads(og_ref, bsz, 128), _per_head(sc_ref, bsz, 1, slice(0, 1)),
            _per_head(sc_ref, bsz, 1, slice(1, 2)), nw_ref[...])


def _gdn_fwd_call(conv, pd, sc, nw):
    bsz, t, _ = conv.shape
    nc = t // CHUNK
    nh = bsz * H
    _, specs = _chunk_specs(nc, False, bsz, _GDN_COLS)

    def body(q_ref, k_ref, v_ref, og_ref, ab_ref, sc_ref, nw_ref, y_ref, sh_ref, th_ref, s_ref):
        @pl.when(pl.program_id(0) == 0)
        def _():
            s_ref[...] = jnp.zeros_like(s_ref)

        q, k, v, a, bb, og, alog, dtb, nw_ = _gdn_args((q_ref, k_ref, v_ref, og_ref, ab_ref, sc_ref, nw_ref), bsz)
        s = s_ref[...]
        sh_ref[0] = s
        y, s_new, tinv = gdn_chunk(q, k, v, a, bb, og, s, alog, dtb, nw_)
        th_ref[0] = tinv
        s_ref[...] = s_new
        for b in range(bsz):
            for h in range(H):
                y_ref[b, :, 128 * h:128 * h + 128] = y[H * b + h].astype(MXU)

    return pl.pallas_call(
        body, name="gdn_fwd", grid=(nc,),
        in_specs=specs + [_full((2, 128)), _full((1, 128))],
        out_specs=[pl.BlockSpec((bsz, CHUNK, 512), lambda n: (0, n, 0)),
                   pl.BlockSpec((1, nh, DV, DV), lambda n: (n, 0, 0, 0)),
                   pl.BlockSpec((1, nh, CHUNK, CHUNK), lambda n: (n, 0, 0, 0))],
        out_shape=[jax.ShapeDtypeStruct((bsz, t, 512), MXU), jax.ShapeDtypeStruct((nc, nh, DV, DV), F32),
                   jax.ShapeDtypeStruct((nc, nh, CHUNK, CHUNK), F32)],
        scratch_shapes=[pltpu.VMEM((nh, DV, DV), F32)],
        compiler_params=_params(("arbitrary",)),
    )(conv, conv, conv, pd, pd, sc, nw)


def _gdn_bwd_call(conv, pd, s_hist, t_hist, dyin, sc, nw):
    bsz, t, _ = conv.shape
    nc = t // CHUNK
    nh = bsz * H
    n_of, specs = _chunk_specs(nc, True, bsz, _GDN_COLS)

    def body(q_ref, k_ref, v_ref, og_ref, ab_ref, sh_ref, th_ref, dy_ref, sc_ref, nw_ref,
             dc_ref, dpd_ref, dsc_ref, dnw_ref, ds_ref):
        @pl.when(pl.program_id(0) == 0)
        def _():
            dsc_ref[...] = jnp.zeros_like(dsc_ref)
            dnw_ref[...] = jnp.zeros_like(dnw_ref)
            ds_ref[...] = jnp.zeros_like(ds_ref)

        q, k, v, a, bb, og, alog, dtb, nw_ = _gdn_args((q_ref, k_ref, v_ref, og_ref, ab_ref, sc_ref, nw_ref), bsz)
        _, vjp = jax.vjp(functools.partial(gdn_chunk, tinv=th_ref[0]), q, k, v, a, bb, og, sh_ref[0], alog, dtb, nw_)
        dq, dk, dv, da, db, dog, ds, dalog, ddtb, dnw = vjp((_heads(dy_ref, bsz, 128), ds_ref[...]))
        ds_ref[...] = ds
        dnw_ref[...] += dnw
        lane = lax.broadcasted_iota(jnp.int32, (CHUNK, 128), 1)
        for b in range(bsz):
            dab = jnp.zeros((CHUNK, 128), F32)
            for h in range(H):
                i = H * b + h
                dc_ref[b, :, 128 * h:128 * h + 128] = dq[i]
                dc_ref[b, :, 512 + 128 * h:512 + 128 * h + 128] = dk[i]
                dc_ref[b, :, 1024 + 128 * h:1024 + 128 * h + 128] = dv[i]
                dpd_ref[b, :, 128 * h:128 * h + 128] = dog[i].astype(MXU)
                dab = dab + jnp.where(lane == h, da[i], 0.0) + jnp.where(lane == H + h, db[i], 0.0)
                dsc_ref[0:1, h:h + 1] += dalog[i]
                dsc_ref[1:2, h:h + 1] += ddtb[i]
            dpd_ref[b, :, 512:640] = dab.astype(MXU)

    return pl.pallas_call(
        body, name="gdn_bwd", grid=(nc,),
        in_specs=specs + [pl.BlockSpec((1, nh, DV, DV), lambda n: (n_of(n), 0, 0, 0)),
                          pl.BlockSpec((1, nh, CHUNK, CHUNK), lambda n: (n_of(n), 0, 0, 0)),
                          pl.BlockSpec((bsz, CHUNK, 512), lambda n: (0, n_of(n), 1)),
                          _full((2, 128)), _full((1, 128))],
        out_specs=[pl.BlockSpec((bsz, CHUNK, W_GQKV), lambda n: (0, n_of(n), 0)),
                   pl.BlockSpec((bsz, CHUNK, W_GDN), lambda n: (0, n_of(n), 0)),
                   _full((2, 128)), _full((1, 128))],
        out_shape=[jax.ShapeDtypeStruct(conv.shape, F32), jax.ShapeDtypeStruct(pd.shape, MXU),
                   jax.ShapeDtypeStruct((2, 128), F32), jax.ShapeDtypeStruct((1, 128), F32)],
        scratch_shapes=[pltpu.VMEM((nh, DV, DV), F32)],
        compiler_params=_params(("arbitrary",)),
    )(conv, conv, conv, pd, pd, s_hist, t_hist, dyin, sc, nw)


def _head_call(x, ya, yb, wout, mod3, lnw, lnb, tgt):
    bsz, t, _ = x.shape
    tm = min(512, t)
    rows = min(256, tm)

    def body(x_ref, ya_ref, yb_ref, w_ref, mod_ref, lnw_ref, lnb_ref, t_ref,
             dyin_ref, dxa_ref, dgate_ref, dw_ref, dlnw_ref, dlnb_ref, loss_ref):
        b, i = pl.program_id(0), pl.program_id(1)

        @pl.when((b == 0) & (i == 0))
        def _():
            dw_ref[...] = jnp.zeros_like(dw_ref)
            dlnw_ref[...] = jnp.zeros_like(dlnw_ref)
            dlnb_ref[...] = jnp.zeros_like(dlnb_ref)
            loss_ref[...] = jnp.zeros_like(loss_ref)

        @pl.when(i == 0)
        def _():
            dgate_ref[...] = jnp.zeros_like(dgate_ref)

        w = w_ref[...]
        parts = [slice(p * rows, (p + 1) * rows) for p in range(tm // rows)]
        yin = [jnp.concatenate([ya_ref[0, rs, :], yb_ref[0, rs, :]], axis=-1).astype(MXU) for rs in parts]
        y = [jnp.dot(yi, w, preferred_element_type=F32) for yi in yin]
        for rs, yi, y_p in zip(parts, yin, y):
            loss, vjp = jax.vjp(head_fn, x_ref[0, rs, :], y_p, mod_ref[0, 2:3, :], lnw_ref[...], lnb_ref[...], t_ref[0, rs, :])
            dx, dy, dgate, dlnw, dlnb, _ = vjp(jnp.ones((1, 1), F32))
            dyb = dy.astype(MXU)
            dyin_ref[0, rs, :] = lax.dot_general(dyb, w, (((1,), (1,)), ((), ())), preferred_element_type=F32)
            dw_ref[...] += lax.dot_general(yi, dyb, (((0,), (0,)), ((), ())), preferred_element_type=F32)
            dxa_ref[0, rs, :] = dx
            dgate_ref[0] += dgate
            dlnw_ref[...] += dlnw
            dlnb_ref[...] += dlnb
            loss_ref[...] += jnp.broadcast_to(loss, (1, 128))

    tok = lambda w, j=0: pl.BlockSpec((1, tm, w), lambda b, i: (b, i, j))
    row = pl.BlockSpec((1, D), lambda b, i: (0, 0))
    return pl.pallas_call(
        body, name="head", grid=(bsz, t // tm),
        in_specs=[tok(D), tok(512), tok(512), pl.BlockSpec((D, D), lambda b, i: (0, 0)),
                  pl.BlockSpec((1, 3, D), lambda b, i: (b, 0, 0)), row, row, tok(D)],
        out_specs=[tok(D), tok(D), pl.BlockSpec((1, 1, D), lambda b, i: (b, 0, 0)),
                   pl.BlockSpec((D, D), lambda b, i: (0, 0)), row, row, pl.BlockSpec((1, 128), lambda b, i: (0, 0))],
        out_shape=[jax.ShapeDtypeStruct(x.shape, F32), jax.ShapeDtypeStruct(x.shape, F32),
                   jax.ShapeDtypeStruct((bsz, 1, D), F32), jax.ShapeDtypeStruct((D, D), F32),
                   jax.ShapeDtypeStruct((1, D), F32), jax.ShapeDtypeStruct((1, D), F32),
                   jax.ShapeDtypeStruct((1, 128), F32)],
        compiler_params=_params(("arbitrary", "arbitrary")),
    )(x, ya, yb, wout, mod3, lnw, lnb, tgt)


def _dh_call(dpg, dpq, dpd, wpt, x, mod3, dxa):
    bsz, t, _ = x.shape
    tm = min(512, t)

    def body(dg_ref, dq_ref, dd_ref, w_ref, x_ref, mod_ref, dxa_ref, gx_ref, dmod_ref):
        @pl.when(pl.program_id(1) == 0)
        def _():
            dmod_ref[...] = jnp.zeros_like(dmod_ref)

        mm = lambda a, lo, hi: jnp.dot(a.astype(MXU), w_ref[lo:hi, :], preferred_element_type=F32)
        dh = mm(dg_ref[0], 0, W_GLA) + mm(dq_ref[0], W_GLA, W_GLA + W_GQKV) + mm(dd_ref[0], W_GLA + W_GQKV, PW)
        gx_ref[0] = dh * (1.0 + mod_ref[0, 1:2, :]) + dxa_ref[0]
        dmod_ref[0, 0:1, :] += jnp.sum(dh, axis=0, keepdims=True)
        dmod_ref[0, 1:2, :] += jnp.sum(dh * x_ref[0], axis=0, keepdims=True)

    tok = lambda w: pl.BlockSpec((1, tm, w), lambda b, i: (b, i, 0))
    return pl.pallas_call(
        body, name="dh", grid=(bsz, t // tm),
        in_specs=[tok(W_GLA), tok(W_GQKV), tok(W_GDN), pl.BlockSpec((PW, D), lambda b, i: (0, 0)), tok(D),
                  pl.BlockSpec((1, 3, D), lambda b, i: (b, 0, 0)), tok(D)],
        out_specs=[tok(D), pl.BlockSpec((1, 2, D), lambda b, i: (b, 0, 0))],
        out_shape=[jax.ShapeDtypeStruct(x.shape, F32), jax.ShapeDtypeStruct((bsz, 2, D), F32)],
        compiler_params=_params(("parallel", "arbitrary")),
    )(dpg, dpq, dpd, wpt, x, mod3, dxa)


def _dw_call(x, mod3, dp):
    bsz, t, _ = x.shape
    width = dp.shape[-1]
    tm = min(512, t)

    def body(x_ref, mod_ref, dp_ref, dw_ref):
        @pl.when((pl.program_id(0) == 0) & (pl.program_id(1) == 0))
        def _():
            dw_ref[...] = jnp.zeros_like(dw_ref)

        h = x_ref[0] * (1.0 + mod_ref[0, 1:2, :]) + mod_ref[0, 0:1, :]
        dw_ref[...] += lax.dot_general(dp_ref[0].astype(MXU), h.astype(MXU), (((0,), (0,)), ((), ())),
                                       preferred_element_type=F32)

    tok = lambda w: pl.BlockSpec((1, tm, w), lambda b, i: (b, i, 0))
    return pl.pallas_call(
        body, name=f"dw{width}", grid=(bsz, t // tm),
        in_specs=[tok(D), pl.BlockSpec((1, 3, D), lambda b, i: (b, 0, 0)), tok(width)],
        out_specs=pl.BlockSpec((width, D), lambda b, i: (0, 0)),
        out_shape=jax.ShapeDtypeStruct((width, D), F32),
        compiler_params=_params(("arbitrary", "arbitrary")),
    )(x, mod3, dp)


def _adamw(w, g, m, v):
    m = ADAM_B1 * m + (1.0 - ADAM_B1) * g
    v = ADAM_B2 * v + (1.0 - ADAM_B2) * jnp.square(g)
    m_hat = m / (1.0 - ADAM_B1 ** ADAM_STEP)
    v_hat = v / (1.0 - ADAM_B2 ** ADAM_STEP)
    delta = -ADAM_LR * (m_hat / (jnp.sqrt(v_hat) + ADAM_EPS) + ADAM_WD * w)
    return delta, m, v


def _sum8(ref):
    g = ref[0].astype(F32)
    for j in range(1, NDEV):
        g = g + ref[j].astype(F32)
    return g


def _adam_sum_call(name, g8, w, m, v, cols):
    r, c = w.shape

    def body(g_ref, w_ref, m_ref, v_ref, go_ref, d_ref, mo_ref, vo_ref):
        g = _sum8(g_ref)
        go_ref[...] = g
        d_ref[...], mo_ref[...], vo_ref[...] = _adamw(w_ref[...], g, m_ref[...], v_ref[...])

    blk = pl.BlockSpec((r, cols), lambda i: (0, i))
    return pl.pallas_call(
        body, name=name, grid=(c // cols,),
        in_specs=[pl.BlockSpec((NDEV, r, cols), lambda i: (0, 0, i)), blk, blk, blk],
        out_specs=[blk] * 4, out_shape=[jax.ShapeDtypeStruct((r, c), F32)] * 4,
        compiler_params=_params(("parallel",)),
    )(g8, w, m, v)


def _adam_ada_call(c_all, dmod_cols, w, m, v):
    def body(c_ref, dm_ref, w_ref, m_ref, v_ref, go_ref, d_ref, mo_ref, vo_ref):
        g = lax.dot_general(c_ref[...].astype(MXU), dm_ref[...].astype(MXU), (((0,), (0,)), ((), ())),
                            preferred_element_type=F32)
        go_ref[...] = g
        d_ref[...], mo_ref[...], vo_ref[...] = _adamw(w_ref[...], g, m_ref[...], v_ref[...])

    return pl.pallas_call(
        body, name="adam_ada", out_shape=[jax.ShapeDtypeStruct(w.shape, F32)] * 4, compiler_params=_params(),
    )(c_all, dmod_cols, w, m, v)


def _adam_small_call(g, w, m, v):
    def body(g_ref, w_ref, m_ref, v_ref, d_ref, mo_ref, vo_ref):
        d_ref[...], mo_ref[...], vo_ref[...] = _adamw(w_ref[...], g_ref[...], m_ref[...], v_ref[...])

    return pl.pallas_call(
        body, name="adam_small", out_shape=[jax.ShapeDtypeStruct(w.shape, F32)] * 3, compiler_params=_params(),
    )(g, w, m, v)


def _small_sum_call(sp_all):
    def body(sp_ref, dmod_ref, sum_ref):
        acc = sp_ref[0, 2:3, :]
        for j in range(1, NDEV):
            acc = acc + sp_ref[j, 2:3, :]
        sum_ref[...] = acc
        for j in range(NDEV):
            dmod_ref[2 * j:2 * j + 2, :] = sp_ref[j, 0:2, :]

    return pl.pallas_call(
        body, name="small_sum", out_shape=[jax.ShapeDtypeStruct((2 * NDEV, SPW), F32), jax.ShapeDtypeStruct((1, SPW), F32)],
        compiler_params=_params(),
    )(sp_all)


def _mesh_pos():
    x, y, c = lax.axis_index("x"), lax.axis_index("y"), lax.axis_index("c")
    return x, y, c, 4 * x + 2 * y + c


def _peer(x, y, c, k):
    px = 1 - x if k & 4 else x
    py = 1 - y if k & 2 else y
    pc = 1 - c if k & 1 else c
    return (px, py, pc), 4 * px + 2 * py + pc


_ANY = pl.BlockSpec(memory_space=pl.ANY)
_VMEM = pl.BlockSpec(memory_space=pltpu.VMEM)


def _gather_call(c8, w_ada, b_sh, w_in_t):
    C_SEM, W_SEM, MOD_SEM = 0, 1, 2

    def body(c_ref, wada_ref, b_ref, win_ref, wall_ref, call_ref, mod_ref, modp, send_sems, recv_sems, loc_sem):
        x, y, c, me = _mesh_pos()

        def remote(src, dst, a, k, to):
            return pltpu.make_async_remote_copy(src_ref=src, dst_ref=dst, send_sem=send_sems.at[a, k],
                                                recv_sem=recv_sems.at[a, k], device_id=_peer(x, y, c, to)[0],
                                                device_id_type=pl.DeviceIdType.MESH)

        idx = lambda k: _peer(x, y, c, k)[1]
        sends = []
        call_ref[me] = c_ref[...]
        for k in range(1, NDEV):
            sends.append(remote(c_ref, call_ref.at[me], C_SEM, k, k))
            sends[-1].start()
        local = pltpu.make_async_copy(win_ref, wall_ref.at[me], loc_sem)
        local.start()
        for k in (1, 2, 4, 6):
            sends.append(remote(win_ref, wall_ref.at[me], W_SEM, k, k))
            sends[-1].start()
        for k in range(1, NDEV):
            remote(c_ref, call_ref.at[idx(k)], C_SEM, k, k).wait_recv()
        modp[...] = jnp.dot(call_ref[...].reshape(NDEV * 8, D).astype(MXU), wada_ref[...].astype(MXU),
                            preferred_element_type=F32) + b_ref[...]
        mod_ref[me] = modp[pl.ds(pl.multiple_of(me * 8, 8), 8), :]
        for k in range(1, NDEV):
            sends.append(remote(modp.at[pl.ds(pl.multiple_of(idx(k) * 8, 8), 8), :], mod_ref.at[me], MOD_SEM, k, k))
            sends[-1].start()
        for k in (2, 4, 6):
            remote(win_ref, wall_ref.at[idx(k)], W_SEM, k, k).wait_recv()
            sends.append(remote(wall_ref.at[idx(k)], wall_ref.at[idx(k)], W_SEM, k + 1, 1))
            sends[-1].start()
        for k in (1, 3, 5, 7):
            remote(win_ref, wall_ref.at[idx(k)], W_SEM, k, 1).wait_recv()
        for k in range(1, NDEV):
            remote(modp.at[pl.ds(0, 8), :], mod_ref.at[idx(k)], MOD_SEM, k, k).wait_recv()
        for cp in sends:
            cp.wait_send()
        local.wait()

    return pl.pallas_call(
        body, name="gather",
        out_shape=[jax.ShapeDtypeStruct((NDEV,) + w_in_t.shape, w_in_t.dtype), jax.ShapeDtypeStruct((NDEV, 8, D), F32),
                   jax.ShapeDtypeStruct((NDEV, 8, SHARD_ADA), F32)],
        in_specs=[_VMEM, _VMEM, _VMEM, _ANY], out_specs=[_ANY, _VMEM, _VMEM],
        scratch_shapes=[pltpu.VMEM((NDEV * 8, SHARD_ADA), F32), pltpu.SemaphoreType.DMA((3, NDEV)),
                        pltpu.SemaphoreType.DMA((3, NDEV)), pltpu.SemaphoreType.DMA],
        compiler_params=_params(),
    )(c8, w_ada, b_sh, w_in_t)


def _reduce_call(blocks, sp):
    nb = len(blocks)

    def body(sp_ref, *rest):
        srcs, outs = rest[:nb], rest[nb:2 * nb]
        spall_ref, send_sems, recv_sems, loc_sems = rest[2 * nb:]
        x, y, c, me = _mesh_pos()

        def remote(src, dst, a, k, dev):
            return pltpu.make_async_remote_copy(src_ref=src, dst_ref=dst, send_sem=send_sems.at[a, k],
                                                recv_sem=recv_sems.at[a, k], device_id=dev,
                                                device_id_type=pl.DeviceIdType.MESH)

        sends = []
        spall_ref[me] = sp_ref[...]
        local = [pltpu.make_async_copy(srcs[a].at[me], outs[a].at[me], loc_sems.at[a]) for a in range(nb)]
        for cp in local:
            cp.start()
        for k in range(1, NDEV):
            dev, pidx = _peer(x, y, c, k)
            sends.append(remote(sp_ref, spall_ref.at[me], nb, k, dev))
            sends[-1].start()
            for a in range(nb):
                sends.append(remote(srcs[a].at[pidx], outs[a].at[me], a, k, dev))
                sends[-1].start()
        for k in range(1, NDEV):
            dev, pidx = _peer(x, y, c, k)
            remote(sp_ref, spall_ref.at[pidx], nb, k, dev).wait_recv()
            for a in range(nb):
                remote(srcs[a].at[pidx], outs[a].at[pidx], a, k, dev).wait_recv()
        for cp in sends:
            cp.wait_send()
        for cp in local:
            cp.wait()

    return pl.pallas_call(
        body, name="reduce", out_shape=[jax.ShapeDtypeStruct(a.shape, a.dtype) for a in blocks]
        + [jax.ShapeDtypeStruct((NDEV, 8, SPW), F32)],
        in_specs=[_VMEM] + [_ANY] * nb, out_specs=[_ANY] * nb + [_VMEM],
        scratch_shapes=[pltpu.SemaphoreType.DMA((nb + 1, NDEV)), pltpu.SemaphoreType.DMA((nb + 1, NDEV)),
                        pltpu.SemaphoreType.DMA((nb,))],
        compiler_params=_params(),
    )(sp, *blocks)


_HBM = pl.BlockSpec(memory_space=pltpu.HBM)
_SEM = pl.BlockSpec(memory_space=pltpu.SEMAPHORE)
_EFFECT = pltpu.SideEffectType.DATAFLOW_SIDE_EFFECTING


def _xchg_start(name, blocks, lands, gather):
    nb = len(blocks)

    def body(*refs):
        srcs, dsts = refs[:nb], refs[nb:2 * nb]
        send_sems, recv_sems = refs[2 * nb], refs[2 * nb + 1]
        token = refs[-1]
        x, y, c, me = _mesh_pos()
        for k in range(1, NDEV):
            dev, pidx = _peer(x, y, c, k)
            for a in range(nb):
                pltpu.make_async_remote_copy(src_ref=srcs[a] if gather else srcs[a].at[pidx], dst_ref=dsts[a].at[me],
                                             send_sem=send_sems.at[NDEV * a + k], recv_sem=recv_sems.at[NDEV * a + k],
                                             device_id=dev, device_id_type=pl.DeviceIdType.MESH).start()
        token[...] = jnp.zeros_like(token)

    thru = [pltpu.HBM(a.shape, a.dtype) for a in list(blocks) + list(lands)]
    return pl.pallas_call(
        body, name=name,
        out_shape=(pltpu.SemaphoreType.DMA((nb * NDEV,)), pltpu.SemaphoreType.DMA((nb * NDEV,)), *thru,
                   jax.ShapeDtypeStruct((8, 128), F32)),
        in_specs=[_HBM] * (2 * nb), out_specs=(_SEM, _SEM, *([_HBM] * (2 * nb)), _VMEM),
        input_output_aliases={i: 2 + i for i in range(2 * nb)},
        compiler_params=pltpu.CompilerParams(has_side_effects=_EFFECT),
    )(*[pltpu.with_memory_space_constraint(a, pltpu.HBM) for a in list(blocks) + list(lands)])


def _xchg_wait(name, send_sems, recv_sems, thru, after, gather):
    nb = len(thru) // 2

    def body(*refs):
        srcs, dsts = refs[:nb], refs[nb:2 * nb]
        send_sems, recv_sems = refs[2 * nb], refs[2 * nb + 1]
        x, y, c, me = _mesh_pos()
        for k in range(1, NDEV):
            dev, pidx = _peer(x, y, c, k)
            for a in range(nb):
                cp = pltpu.make_async_remote_copy(src_ref=srcs[a] if gather else srcs[a].at[pidx], dst_ref=dsts[a].at[pidx],
                                                  send_sem=send_sems.at[NDEV * a + k], recv_sem=recv_sems.at[NDEV * a + k],
                                                  device_id=dev, device_id_type=pl.DeviceIdType.MESH)
                cp.wait_send()
                cp.wait_recv()

    out = pl.pallas_call(
        body, name=name, out_shape=tuple(pltpu.HBM(a.shape, a.dtype) for a in thru),
        in_specs=[_HBM] * (2 * nb) + [_SEM, _SEM, pl.BlockSpec(memory_space=pl.ANY)], out_specs=tuple([_HBM] * (2 * nb)),
        input_output_aliases={i: i for i in range(2 * nb)},
        compiler_params=pltpu.CompilerParams(has_side_effects=_EFFECT),
    )(*thru, send_sems, recv_sems, after)
    return out[nb:]


def _pad_cols(a, n):
    return jnp.pad(a, ((0, 0), (0, n - a.shape[1])))


def _assemble_wt(wt_full):
    q, k, v, lr, og, gqkv, ab, dog = jnp.split(wt_full, [256, 512, 1024, 1040, 1552, 3088, 3096], axis=0)
    z = lambda n: jnp.zeros((n, wt_full.shape[1]), wt_full.dtype)
    return jnp.concatenate([q, k, v, og, lr, z(112), gqkv, dog, ab, z(120)], axis=0)


def _disassemble_dwt(dw_gla, dw_gqkv, dw_gdn):
    return jnp.concatenate([dw_gla[:1024], dw_gla[1536:1552], dw_gla[1024:1536], dw_gqkv,
                            dw_gdn[512:520], dw_gdn[:512]], axis=0)


def local_grads(x, mod3, wp, late_weights, bg, gla_nw, sc, gdn_nw, lnw, lnb, tgt):
    pg, pq, pd = _proj_call(x, mod3, wp)
    wout, conv_w, wgu_p = late_weights(pq)
    conv = _conv_fwd_call(pq, conv_w)
    ya, s_gla = _gla_fwd_call(pg, wgu_p, bg, gla_nw)
    yb, s_gdn, t_gdn = _gdn_fwd_call(conv, pd, sc, gdn_nw)
    dyin, dxa, dgate, dwout, dlnw, dlnb, loss = _head_call(x, ya, yb, wout, mod3, lnw, lnb, tgt)
    dpg, dwgu, dbg, dnw_gla = _gla_bwd_call(pg, s_gla, dyin, wgu_p, bg, gla_nw)
    dconv, dpd, dsc, dnw_gdn = _gdn_bwd_call(conv, pd, s_gdn, t_gdn, dyin, sc, gdn_nw)
    dpq, dconv_w = _conv_bwd_call(dconv, pq, conv_w)
    dw_in = _disassemble_dwt(_dw_call(x, mod3, dpg), _dw_call(x, mod3, dpq), _dw_call(x, mod3, dpd))
    g = dict(loss=loss[0, 0], dw_in=dw_in, dwout=dwout, dconv_w=dconv_w[:4], dwgu=dwgu[:16], dbg=dbg, dnw_gla=dnw_gla,
             dalog=dsc[0:1, :4], ddtb=dsc[1:2, :4], dnw_gdn=dnw_gdn, dlnw=dlnw, dlnb=dlnb)
    return g, functools.partial(_finish_grad_x, dpg, dpq, dpd, wp, x, dxa, dgate)


def _finish_grad_x(dpg, dpq, dpd, wp, x, dxa, dgate, mod3):
    gx, dmod2 = _dh_call(dpg, dpq, dpd, wp, x, mod3, dxa)
    return gx, jnp.concatenate([dmod2, dgate], axis=1)


def local_step(x, mod3, wp, wout, conv_w, wgu_p, *args):
    g, finish = local_grads(x, mod3, wp, lambda _: (wout, conv_w, wgu_p), *args)
    g["gx"], g["dmod"] = finish(mod3)
    return g


def kernel(x, c, w_ada, b_ada, w_in, gla_w_gate_up, gla_b_gate, gla_norm_w, gdn_conv_w, gdn_a_log, gdn_dt_bias, gdn_norm_w, w_out, ln_w, ln_b, loss_target, m_w_ada, m_b_ada, m_w_in, m_gla_w_gate_up, m_gla_b_gate, m_gla_norm_w, m_gdn_conv_w, m_gdn_a_log, m_gdn_dt_bias, m_gdn_norm_w, m_w_out, m_ln_w, m_ln_b, v_w_ada, v_b_ada, v_w_in, v_gla_w_gate_up, v_gla_b_gate, v_gla_norm_w, v_gdn_conv_w, v_gdn_a_log, v_gdn_dt_bias, v_gdn_norm_w, v_w_out, v_ln_w, v_ln_b):
    me = 4 * lax.axis_index("x") + 2 * lax.axis_index("y") + lax.axis_index("c")
    bsz = x.shape[0]

    b_sh = lax.dynamic_slice(b_ada, (0, me * SHARD_ADA), (1, SHARD_ADA))
    c8 = jnp.pad(c, ((0, 8 - bsz), (0, 0)))
    w_in_t, m_in_t, v_in_t = (jnp.swapaxes(a[0], 0, 1) for a in (w_in, m_w_in, v_w_in))
    win_all, c_all, mod_blk = _gather_call(c8, w_ada[0], b_sh, w_in_t.astype(WIRE))
    wp = _assemble_wt(win_all.reshape(IN_COLS, D))
    mod = jnp.transpose(mod_blk[:, :bsz, :], (1, 0, 2)).reshape(bsz, 3 * D)
    mod3 = mod.reshape(bsz, 3, D)
    sc = jnp.concatenate([_pad_cols(gdn_a_log, 128), _pad_cols(gdn_dt_bias, 128)], axis=0)

    own = lambda a: lax.dynamic_update_slice(lax.empty((NDEV,) + a.shape, a.dtype), a[None], (me,) + (0,) * a.ndim)
    late = [w_out[0].astype(WIRE), gdn_conv_w[0], gla_w_gate_up[0] + 0.0 * mod_blk[0, 0, 0]]
    w_send, w_recv, *w_thru, w_token = _xchg_start("wgather_start", late, [own(a) for a in late], gather=True)

    def late_weights(pq):
        wout_all, conv_all, wgu_all = _xchg_wait("wgather_wait", w_send, w_recv, w_thru, pq, gather=True)
        return (wout_all.reshape(D, D), jnp.transpose(conv_all, (1, 0, 2)).reshape(4, W_GQKV),
                jnp.pad(jnp.transpose(wgu_all, (1, 0, 2)).reshape(16, 256), ((0, 112), (0, 0))))

    g, finish = local_grads(x, mod3 + w_token[0, 0], wp, late_weights, gla_b_gate, gla_norm_w, sc, gdn_norm_w, ln_w, ln_b,
                            loss_target)

    big = [g["dw_in"].reshape(NDEV, SHARD_IN, D).astype(WIRE), g["dwout"].reshape(NDEV, D // NDEV, D).astype(WIRE)]
    lands = [lax.dynamic_update_slice(lax.empty(a.shape, a.dtype), lax.dynamic_slice(a, (me, 0, 0), (1,) + a.shape[1:]),
                                      (me, 0, 0)) for a in big]
    send_sems, recv_sems, *thru, token = _xchg_start("xchg_start", big, lands, gather=False)
    gx, dmod = finish(mod3 + token[0, 0])
    r_in, r_out = _xchg_wait("xchg_wait", send_sems, recv_sems, thru, gx, gather=False)

    small = jnp.concatenate([g["dlnw"], g["dlnb"], g["dbg"], g["dnw_gla"], g["dnw_gdn"], _pad_cols(g["dalog"], 128),
                             _pad_cols(g["ddtb"], 128), jnp.full((1, 128), g["loss"], F32), jnp.zeros((1, 128), F32)], axis=1)
    sp = jnp.concatenate([dmod.reshape(bsz, SPW), small, jnp.zeros((8 - bsz - 1, SPW), F32)], axis=0)
    blocks = [jnp.transpose(g["dconv_w"].reshape(4, NDEV, W_GQKV // NDEV), (1, 0, 2)),
              jnp.transpose(g["dwgu"].reshape(16, NDEV, 256 // NDEV), (1, 0, 2))]
    r_conv, r_gu, sp_all = _reduce_call(blocks, sp)
    dmod_all, sums = _small_sum_call(sp_all)

    g_in, d_in, nm_in, nv_in = (jnp.swapaxes(a, 0, 1) for a in _adam_sum_call("adam_in", r_in, w_in_t, m_in_t, v_in_t, 256))
    g_out, d_out, nm_out, nv_out = _adam_sum_call("adam_out", r_out, w_out[0], m_w_out[0], v_w_out[0], D)
    g_conv, d_conv, nm_conv, nv_conv = _adam_sum_call("adam_conv", r_conv, gdn_conv_w[0], m_gdn_conv_w[0], v_gdn_conv_w[0],
                                                      W_GQKV // NDEV)
    g_gu, d_gu, nm_gu, nv_gu = _adam_sum_call("adam_gu", r_gu, gla_w_gate_up[0], m_gla_w_gate_up[0], v_gla_w_gate_up[0],
                                              256 // NDEV)
    c16 = c_all[:, :bsz, :].reshape(NDEV * bsz, D)
    g_ada, d_ada, nm_ada, nv_ada = _adam_ada_call(c16, lax.dynamic_slice(dmod_all, (0, me * SHARD_ADA), (NDEV * bsz, SHARD_ADA)),
                                                  w_ada[0], m_w_ada[0], v_w_ada[0])

    def pack(b_a, lw, lb, bgt, n1, n2, al, dt):
        return jnp.concatenate([b_a, lw, lb, bgt, n1, n2, _pad_cols(al, 128), _pad_cols(dt, 128)], axis=1).reshape(-1, 128)

    g_small = _bada_and_pack(dmod_all, sums)
    w_s = pack(b_ada, ln_w, ln_b, gla_b_gate, gla_norm_w, gdn_norm_w, gdn_a_log, gdn_dt_bias)
    m_s = pack(m_b_ada, m_ln_w, m_ln_b, m_gla_b_gate, m_gla_norm_w, m_gdn_norm_w, m_gdn_a_log, m_gdn_dt_bias)
    v_s = pack(v_b_ada, v_ln_w, v_ln_b, v_gla_b_gate, v_gla_norm_w, v_gdn_norm_w, v_gdn_a_log, v_gdn_dt_bias)
    d_s, nm_s, nv_s = _adam_small_call(g_small, w_s, m_s, v_s)

    def unpack(p):
        f = p.reshape(1, -1)
        b_a, lw, lb, bgt, n1, n2, al, dt = jnp.split(f, [3072, 4096, 5120, 5376, 5504, 5632, 5760], axis=1)
        return dict(b_ada=b_a, ln_w=lw, ln_b=lb, b_gate=bgt, gla_nw=n1, gdn_nw=n2, a_log=al[:, :4], dt_bias=dt[:, :4])

    gs, ds, ms, vs = unpack(g_small), unpack(d_s), unpack(nm_s), unpack(nv_s)
    loss = sums[0, SMALL_W]

    def group(t_ada, t_in, t_gu, t_conv, t_out, s):
        return [t_ada[None], s["b_ada"], t_in[None], t_gu[None], s["b_gate"], s["gla_nw"], t_conv[None], s["a_log"],
                s["dt_bias"], s["gdn_nw"], t_out[None], s["ln_w"], s["ln_b"]]

    return (loss, gx, *group(g_ada, g_in, g_gu, g_conv, g_out, gs), *group(d_ada, d_in, d_gu, d_conv, d_out, ds),
            *group(nm_ada, nm_in, nm_gu, nm_conv, nm_out, ms), *group(nv_ada, nv_in, nv_gu, nv_conv, nv_out, vs))


def _bada_and_pack(dmod_all, sums):
    n = dmod_all.shape[0]

    def body(dm_ref, s_ref, o_ref):
        acc = dm_ref[0:1, :]
        for j in range(1, n):
            acc = acc + dm_ref[j:j + 1, :]
        o_ref[:, 0:SPW] = acc
        o_ref[:, SPW:SPW + SMALL_W] = s_ref[:, 0:SMALL_W]

    packed = pl.pallas_call(
        body, name="bada_pack", out_shape=jax.ShapeDtypeStruct((1, SPW + SMALL_W), F32), compiler_params=_params(),
    )(dmod_all, sums)
    return packed.reshape(-1, 128)
```

```python
import functools

import jax
import jax.numpy as jnp
from jax import lax
from jax.experimental import pallas as pl
from jax.experimental.pallas import tpu as pltpu

F32 = jnp.float32
MXU = jnp.bfloat16
WIRE = jnp.bfloat16
HI = lax.Precision.HIGH

D = 1024
NDEV = 8
H = 4
GLA_DK = 64
DV = 128
CHUNK = 64
SUB = 4
LN_EPS = 1e-5
RMS_EPS = 1e-6
ALPHA = 2.0 ** 0.25
GATE_NORM = 16.0

W_GLA, W_GQKV, W_GDN = 1664, 1536, 640
PW = W_GLA + W_GQKV + W_GDN
IN_COLS = 3608
SHARD_IN = IN_COLS // NDEV
SHARD_ADA = 3 * D // NDEV
SPW = 3 * D
SMALL_W = 2816

ADAM_LR, ADAM_B1, ADAM_B2, ADAM_EPS, ADAM_WD, ADAM_STEP = 0.001, 0.9, 0.999, 1e-08, 0.01, 10

VMEM_LIMIT = 56 * 1024 * 1024


def _params(sem=None, **kw):
    if sem is not None:
        kw["dimension_semantics"] = sem
    return pltpu.CompilerParams(vmem_limit_bytes=VMEM_LIMIT, **kw)


_MM = (((2,), (1,)), ((0,), (0,)))
_NT = (((2,), (2,)), ((0,), (0,)))
_TN = (((1,), (1,)), ((0,), (0,)))


def _dg(a, b, dims):
    return lax.dot_general(a.astype(MXU), b.astype(MXU), dims, preferred_element_type=F32)


def _hdg(a, b, dims):
    return lax.dot_general(a, b, dims, precision=HI, preferred_element_type=F32)


@jax.custom_vjp
def bmm(a, b):
    return _dg(a, b, _MM)


bmm.defvjp(lambda a, b: (_dg(a, b, _MM), (a, b)), lambda r, g: (_dg(g, r[1], _NT), _dg(r[0], g, _TN)))


@jax.custom_vjp
def bnt(a, b):
    return _dg(a, b, _NT)


bnt.defvjp(lambda a, b: (_dg(a, b, _NT), (a, b)), lambda r, g: (_dg(g, r[1], _MM), _dg(g, r[0], _TN)))


@jax.custom_vjp
def btn(a, b):
    return _dg(a, b, _TN)


btn.defvjp(lambda a, b: (_dg(a, b, _TN), (a, b)), lambda r, g: (_dg(r[1], g, _NT), _dg(r[0], g, _MM)))


def unit_lower_inverse(a):
    n = a.shape[-1]
    r, c = _iotas(n)
    p = -a
    t = (r == c).astype(F32) + p
    for _ in range(5):
        p = _hdg(p, p, _MM)
        t = t + _hdg(t, p, _MM)
    return t


@jax.custom_vjp
def unit_lower_solve(a, t, r1, r2):
    return _dg(t, r1, _MM), _dg(t, r2, _MM)


def _solve_fwd(a, t, r1, r2):
    s1, s2 = _dg(t, r1, _MM), _dg(t, r2, _MM)
    return (s1, s2), (t, s1, s2)


def _solve_bwd(res, g):
    t, s1, s2 = res
    d1, d2 = _dg(t, g[0], _TN), _dg(t, g[1], _TN)
    return -(_dg(d1, s1, _NT) + _dg(d2, s2, _NT)), jnp.zeros_like(t), d1, d2


unit_lower_solve.defvjp(_solve_fwd, _solve_bwd)


def _iotas(n):
    return lax.broadcasted_iota(jnp.int32, (n, n), 0), lax.broadcasted_iota(jnp.int32, (n, n), 1)


def _col_to_row(col, eye):
    return jnp.sum(jnp.where(eye, col, 0.0), axis=1, keepdims=True)


def _row_to_col(row, eye):
    return jnp.sum(jnp.where(eye, row, 0.0), axis=2, keepdims=True)


def _pick_row(m, i):
    r = lax.broadcasted_iota(jnp.int32, m.shape, 1)
    return jnp.sum(jnp.where(r == i, m, 0.0), axis=1, keepdims=True)


def _rms_gate(o, nw, og):
    on = o * lax.rsqrt(jnp.mean(o * o, axis=-1, keepdims=True) + RMS_EPS) * nw
    return on * jax.nn.silu(og)


def gla_chunk(q, k, v, lr, og, s, wgu, bg, nw):
    n, c, _ = q.shape
    r, cc = _iotas(c)
    causal = r >= cc
    qs = q * (GLA_DK ** -0.5)
    z = bmm(lr, wgu) + bg
    g = jax.nn.log_sigmoid(z) / GATE_NORM
    b = _hdg(jnp.broadcast_to(causal.astype(F32), (n, c, c)), g, _MM)
    bref = _pick_row(b, c // 2 - 1)
    blast = _pick_row(b, c - 1)
    att = jnp.where(causal, bnt(qs * jnp.exp(b - bref), k * jnp.exp(bref - b)), 0.0)
    o = bmm(att, v) + bmm(qs * jnp.exp(b), s)
    rk, ck = _iotas(GLA_DK)
    s_new = _row_to_col(jnp.exp(blast), rk == ck) * s + btn(k * jnp.exp(blast - b), v)
    return _rms_gate(o, nw, og), s_new


def gdn_chunk(cq, ck, cv, a, bb, og, s, alog, dtb, nw, tinv=None):
    c = cq.shape[1]
    r, cc = _iotas(c)
    eye, causal, strict = r == cc, r >= cc, r > cc
    q, k, v = jax.nn.silu(cq), jax.nn.silu(ck), jax.nn.silu(cv)
    q = q * lax.rsqrt(jnp.sum(q * q, axis=-1, keepdims=True) + RMS_EPS) * (DV ** -0.5)
    k = k * lax.rsqrt(jnp.sum(k * k, axis=-1, keepdims=True) + RMS_EPS)
    g = -jnp.exp(alog) * jax.nn.softplus(a + dtb)
    beta = jax.nn.sigmoid(bb)
    d = jnp.sum(jnp.where(causal, _col_to_row(g, eye), 0.0), axis=2, keepdims=True)
    el = jnp.exp(jnp.where(causal, d - _col_to_row(d, eye), -jnp.inf))
    kb = k * beta
    amat = jnp.where(strict, bnt(kb, k) * el, 0.0)
    t = unit_lower_inverse(amat) if tinv is None else tinv
    u, w = unit_lower_solve(amat, t, v * beta, kb * jnp.exp(d))
    qk = jnp.where(causal, bnt(q, k) * el, 0.0)
    dlast = _pick_row(d, c - 1)
    v_new = u - bmm(w, s)
    o = bmm(q * jnp.exp(d), s) + bmm(qk, v_new)
    s_new = jnp.exp(dlast) * s + btn(k * jnp.exp(dlast - d), v_new)
    y = _rms_gate(o, nw, og)
    return (y, s_new, t) if tinv is None else (y, s_new)


def head_fn(x, y, gate, lnw, lnb, tgt):
    u = ALPHA * x + (1.0 + gate) * y
    mu = jnp.mean(u, axis=-1, keepdims=True)
    var = jnp.mean(jnp.square(u - mu), axis=-1, keepdims=True)
    out = (u - mu) * lax.rsqrt(var + LN_EPS) * lnw + lnb
    err = jnp.square(out - tgt)
    return 0.5 * jnp.sum(jnp.mean(err, axis=-1, keepdims=True), axis=0, keepdims=True)


def _proj_call(x, mod3, wpt):
    bsz, t, _ = x.shape
    tm = min(512, t)

    def body(x_ref, mod_ref, w_ref, pg_ref, pq_ref, pd_ref):
        h = (x_ref[0] * (1.0 + mod_ref[0, 1:2, :]) + mod_ref[0, 0:1, :]).astype(MXU)
        nt = lambda lo, hi: lax.dot_general(h, w_ref[lo:hi, :], (((1,), (1,)), ((), ())), preferred_element_type=F32)
        pg_ref[0] = nt(0, W_GLA)
        pq_ref[0] = nt(W_GLA, W_GLA + W_GQKV)
        pd_ref[0] = nt(W_GLA + W_GQKV, PW)

    tok = lambda w: pl.BlockSpec((1, tm, w), lambda b, i: (b, i, 0))
    return pl.pallas_call(
        body, name="proj", grid=(bsz, t // tm),
        in_specs=[tok(D), pl.BlockSpec((1, 3, D), lambda b, i: (b, 0, 0)), pl.BlockSpec((PW, D), lambda b, i: (0, 0))],
        out_specs=[tok(W_GLA), tok(W_GQKV), tok(W_GDN)],
        out_shape=[jax.ShapeDtypeStruct((bsz, t, w), F32) for w in (W_GLA, W_GQKV, W_GDN)],
        compiler_params=_params(("parallel", "parallel")),
    )(x, mod3, wpt)


_CONV_ROWS = 16


def _conv_fwd_call(pq, conv_w):
    bsz, t, _ = pq.shape
    tt = min(512, t)
    hb = tt // 8

    def body(x_ref, halo_ref, w_ref, o_ref, buf):
        i = pl.program_id(1)
        buf[0:8, :] = jnp.where(i > 0, halo_ref[0], 0.0)
        buf[8:, :] = x_ref[0]
        for j in range(W_GQKV // 128):
            ls = slice(128 * j, 128 * j + 128)
            wj = [w_ref[k:k + 1, ls] for k in range(4)]

            def rows(r, carry):
                base = pl.multiple_of(r * _CONV_ROWS, _CONV_ROWS)
                win = buf[pl.ds(base, _CONV_ROWS + 8), ls]
                acc = wj[0] * win[5:5 + _CONV_ROWS, :]
                for k in range(1, 4):
                    acc = acc + wj[k] * win[5 + k:5 + k + _CONV_ROWS, :]
                o_ref[0, pl.ds(base, _CONV_ROWS), ls] = acc
                return carry

            lax.fori_loop(0, tt // _CONV_ROWS, rows, 0, unroll=4)

    return pl.pallas_call(
        body, name="conv_fwd", grid=(bsz, t // tt),
        in_specs=[pl.BlockSpec((1, tt, W_GQKV), lambda b, i: (b, i, 0)),
                  pl.BlockSpec((1, 8, W_GQKV), lambda b, i: (b, jnp.maximum(i * hb - 1, 0), 0)),
                  pl.BlockSpec((4, W_GQKV), lambda b, i: (0, 0))],
        out_specs=pl.BlockSpec((1, tt, W_GQKV), lambda b, i: (b, i, 0)),
        out_shape=jax.ShapeDtypeStruct(pq.shape, F32),
        scratch_shapes=[pltpu.VMEM((tt + 8, W_GQKV), F32)],
        compiler_params=_params(("parallel", "parallel")),
    )(pq, pq, conv_w)


def _conv_bwd_call(dconv, pq, conv_w):
    bsz, t, _ = pq.shape
    tt = min(512, t)
    hb = tt // 8
    nt_ = t // tt

    def body(d_ref, dnext_ref, x_ref, w_ref, din_ref, dw_ref, dbuf):
        b, i = pl.program_id(0), pl.program_id(1)

        @pl.when((b == 0) & (i == 0))
        def _():
            dw_ref[...] = jnp.zeros_like(dw_ref)

        dbuf[0:tt, :] = d_ref[0]
        dbuf[tt:, :] = jnp.where(i < nt_ - 1, dnext_ref[0], 0.0)
        xin = x_ref[0]
        acc = None
        for k in range(4):
            dsh = dbuf[pl.ds(3 - k, tt), :]
            acc = w_ref[k:k + 1, :] * dsh if acc is None else acc + w_ref[k:k + 1, :] * dsh
            dw_ref[k:k + 1, :] += jnp.sum(xin * dsh, axis=0, keepdims=True)
        din_ref[0] = acc.astype(MXU)

    tile = pl.BlockSpec((1, tt, W_GQKV), lambda b, i: (b, i, 0))
    return pl.pallas_call(
        body, name="conv_bwd", grid=(bsz, nt_),
        in_specs=[tile, pl.BlockSpec((1, 8, W_GQKV), lambda b, i: (b, jnp.minimum((i + 1) * hb, t // 8 - 1), 0)),
                  tile, pl.BlockSpec((4, W_GQKV), lambda b, i: (0, 0))],
        out_specs=[tile, pl.BlockSpec((8, W_GQKV), lambda b, i: (0, 0))],
        out_shape=[jax.ShapeDtypeStruct(pq.shape, MXU), jax.ShapeDtypeStruct((8, W_GQKV), F32)],
        scratch_shapes=[pltpu.VMEM((tt + 8, W_GQKV), F32)],
        compiler_params=_params(("arbitrary", "arbitrary")),
    )(dconv, dconv, pq, conv_w)


def _chunk_specs(nc, rev, bsz, cols):
    steps = nc // SUB
    n_of = (lambda n: steps - 1 - n) if rev else (lambda n: n)
    return n_of, [pl.BlockSpec((bsz, SUB * CHUNK, w), lambda n, j=j: (0, n_of(n), j)) for w, j in cols]


def _full(shape):
    return pl.BlockSpec(shape, lambda n: (0,) * len(shape))


def _heads(ref, bsz, rows, width, off=0):
    return jnp.stack([ref[b, rows, off + width * h:off + width * (h + 1)] for b in range(bsz) for h in range(H)])


def _chunk_rows(sub):
    return slice(CHUNK * sub, CHUNK * (sub + 1))


def _per_head(ref, bsz, width, rows=slice(None)):
    return jnp.stack([ref[rows, width * h:width * (h + 1)] for _ in range(bsz) for h in range(H)])


_GLA_COLS = [(256, 0), (256, 1), (512, 1), (512, 2), (128, 12)]


def _gla_args(refs, bsz, rows):
    q_ref, k_ref, v_ref, og_ref, lr_ref, wgu_ref, bg_ref, nw_ref = refs
    lr = jnp.stack([lr_ref[b, rows, :] for b in range(bsz) for _ in range(H)])
    return (_heads(q_ref, bsz, rows, 64), _heads(k_ref, bsz, rows, 64), _heads(v_ref, bsz, rows, 128), lr,
            _heads(og_ref, bsz, rows, 128), _per_head(wgu_ref, bsz, 64), _per_head(bg_ref, bsz, 64), nw_ref[...])


def _gla_fwd_call(pg, wgu, bg, nw):
    bsz, t, _ = pg.shape
    nc = t // CHUNK
    nh = bsz * H
    _, specs = _chunk_specs(nc, False, bsz, _GLA_COLS)

    def body(q_ref, k_ref, v_ref, og_ref, lr_ref, wgu_ref, bg_ref, nw_ref, y_ref, sh_ref, s_ref):
        @pl.when(pl.program_id(0) == 0)
        def _():
            s_ref[...] = jnp.zeros_like(s_ref)

        s = s_ref[...]
        for sub in range(SUB):
            rows = _chunk_rows(sub)
            q, k, v, lr, og, w, b_, nw_ = _gla_args((q_ref, k_ref, v_ref, og_ref, lr_ref, wgu_ref, bg_ref, nw_ref), bsz, rows)
            sh_ref[sub] = s
            y, s = gla_chunk(q, k, v, lr, og, s, w, b_, nw_)
            for b in range(bsz):
                for h in range(H):
                    y_ref[b, rows, 128 * h:128 * h + 128] = y[H * b + h].astype(MXU)
        s_ref[...] = s

    return pl.pallas_call(
        body, name="gla_fwd", grid=(nc // SUB,),
        in_specs=specs + [_full((128, 256)), _full((1, 256)), _full((1, 128))],
        out_specs=[pl.BlockSpec((bsz, SUB * CHUNK, 512), lambda n: (0, n, 0)),
                   pl.BlockSpec((SUB, nh, GLA_DK, DV), lambda n: (n, 0, 0, 0))],
        out_shape=[jax.ShapeDtypeStruct((bsz, t, 512), MXU), jax.ShapeDtypeStruct((nc, nh, GLA_DK, DV), F32)],
        scratch_shapes=[pltpu.VMEM((nh, GLA_DK, DV), F32)],
        compiler_params=_params(("arbitrary",)),
    )(pg, pg, pg, pg, pg, wgu, bg, nw)


def _gla_bwd_call(pg, s_hist, dyin, wgu, bg, nw):
    bsz, t, _ = pg.shape
    nc = t // CHUNK
    nh = bsz * H
    n_of, specs = _chunk_specs(nc, True, bsz, _GLA_COLS)

    def body(q_ref, k_ref, v_ref, og_ref, lr_ref, sh_ref, dy_ref, wgu_ref, bg_ref, nw_ref,
             dp_ref, dwgu_ref, dbg_ref, dnw_ref, ds_ref):
        @pl.when(pl.program_id(0) == 0)
        def _():
            dwgu_ref[...] = jnp.zeros_like(dwgu_ref)
            dbg_ref[...] = jnp.zeros_like(dbg_ref)
            dnw_ref[...] = jnp.zeros_like(dnw_ref)
            ds_ref[...] = jnp.zeros_like(ds_ref)

        ds = ds_ref[...]
        for sub in reversed(range(SUB)):
            rows = _chunk_rows(sub)
            q, k, v, lr, og, w, b_, nw_ = _gla_args((q_ref, k_ref, v_ref, og_ref, lr_ref, wgu_ref, bg_ref, nw_ref), bsz, rows)
            _, vjp = jax.vjp(gla_chunk, q, k, v, lr, og, sh_ref[sub], w, b_, nw_)
            dq, dk, dv, dlr, dog, ds, dwgu, dbg, dnw = vjp((_heads(dy_ref, bsz, rows, 128), ds))
            dnw_ref[...] += dnw
            for b in range(bsz):
                for h in range(H):
                    i = H * b + h
                    dp_ref[b, rows, 512 + 128 * h:512 + 128 * h + 128] = dv[i].astype(MXU)
                    dp_ref[b, rows, 1024 + 128 * h:1024 + 128 * h + 128] = dog[i].astype(MXU)
                    dwgu_ref[:, 64 * h:64 * h + 64] += dwgu[i]
                    dbg_ref[:, 64 * h:64 * h + 64] += dbg[i]
                for j in range(H // 2):
                    dp_ref[b, rows, 128 * j:128 * j + 128] = jnp.concatenate(
                        [dq[H * b + 2 * j], dq[H * b + 2 * j + 1]], axis=-1).astype(MXU)
                    dp_ref[b, rows, 256 + 128 * j:256 + 128 * j + 128] = jnp.concatenate(
                        [dk[H * b + 2 * j], dk[H * b + 2 * j + 1]], axis=-1).astype(MXU)
                dp_ref[b, rows, 1536:1664] = (dlr[H * b] + dlr[H * b + 1] + dlr[H * b + 2] + dlr[H * b + 3]).astype(MXU)
        ds_ref[...] = ds

    return pl.pallas_call(
        body, name="gla_bwd", grid=(nc // SUB,),
        in_specs=specs + [pl.BlockSpec((SUB, nh, GLA_DK, DV), lambda n: (n_of(n), 0, 0, 0)),
                          pl.BlockSpec((bsz, SUB * CHUNK, 512), lambda n: (0, n_of(n), 0)),
                          _full((128, 256)), _full((1, 256)), _full((1, 128))],
        out_specs=[pl.BlockSpec((bsz, SUB * CHUNK, W_GLA), lambda n: (0, n_of(n), 0)),
                   _full((128, 256)), _full((1, 256)), _full((1, 128))],
        out_shape=[jax.ShapeDtypeStruct(pg.shape, MXU), jax.ShapeDtypeStruct((128, 256), F32),
                   jax.ShapeDtypeStruct((1, 256), F32), jax.ShapeDtypeStruct((1, 128), F32)],
        scratch_shapes=[pltpu.VMEM((nh, GLA_DK, DV), F32)],
        compiler_params=_params(("arbitrary",)),
    )(pg, pg, pg, pg, pg, s_hist, dyin, wgu, bg, nw)


_GDN_COLS = [(512, 0), (512, 1), (512, 2), (512, 0), (128, 4)]


def _gdn_args(refs, bsz, rows):
    q_ref, k_ref, v_ref, og_ref, ab_ref, sc_ref, nw_ref = refs
    return (_heads(q_ref, bsz, rows, 128), _heads(k_ref, bsz, rows, 128), _heads(v_ref, bsz, rows, 128),
            _heads(ab_ref, bsz, rows, 1), _heads(ab_ref, bsz, rows, 1, off=H), _heads(og_ref, bsz, rows, 128),
            _per_head(sc_ref, bsz, 1, slice(0, 1)), _per_head(sc_ref, bsz, 1, slice(1, 2)), nw_ref[...])


def _gdn_fwd_call(conv, pd, sc, nw):
    bsz, t, _ = conv.shape
    nc = t // CHUNK
    nh = bsz * H
    _, specs = _chunk_specs(nc, False, bsz, _GDN_COLS)

    def body(q_ref, k_ref, v_ref, og_ref, ab_ref, sc_ref, nw_ref, y_ref, sh_ref, th_ref, s_ref):
        @pl.when(pl.program_id(0) == 0)
        def _():
            s_ref[...] = jnp.zeros_like(s_ref)

        s = s_ref[...]
        for sub in range(SUB):
            rows = _chunk_rows(sub)
            q, k, v, a, bb, og, alog, dtb, nw_ = _gdn_args((q_ref, k_ref, v_ref, og_ref, ab_ref, sc_ref, nw_ref), bsz, rows)
            sh_ref[sub] = s
            y, s, tinv = gdn_chunk(q, k, v, a, bb, og, s, alog, dtb, nw_)
            th_ref[sub] = tinv
            for b in range(bsz):
                for h in range(H):
                    y_ref[b, rows, 128 * h:128 * h + 128] = y[H * b + h].astype(MXU)
        s_ref[...] = s

    return pl.pallas_call(
        body, name="gdn_fwd", grid=(nc // SUB,),
        in_specs=specs + [_full((2, 128)), _full((1, 128))],
        out_specs=[pl.BlockSpec((bsz, SUB * CHUNK, 512), lambda n: (0, n, 0)),
                   pl.BlockSpec((SUB, nh, DV, DV), lambda n: (n, 0, 0, 0)),
                   pl.BlockSpec((SUB, nh, CHUNK, CHUNK), lambda n: (n, 0, 0, 0))],
        out_shape=[jax.ShapeDtypeStruct((bsz, t, 512), MXU), jax.ShapeDtypeStruct((nc, nh, DV, DV), F32),
                   jax.ShapeDtypeStruct((nc, nh, CHUNK, CHUNK), F32)],
        scratch_shapes=[pltpu.VMEM((nh, DV, DV), F32)],
        compiler_params=_params(("arbitrary",)),
    )(conv, conv, conv, pd, pd, sc, nw)


def _gdn_bwd_call(conv, pd, s_hist, t_hist, dyin, sc, nw):
    bsz, t, _ = conv.shape
    nc = t // CHUNK
    nh = bsz * H
    n_of, specs = _chunk_specs(nc, True, bsz, _GDN_COLS)

    def body(q_ref, k_ref, v_ref, og_ref, ab_ref, sh_ref, th_ref, dy_ref, sc_ref, nw_ref,
             dc_ref, dpd_ref, dsc_ref, dnw_ref, ds_ref):
        @pl.when(pl.program_id(0) == 0)
        def _():
            dsc_ref[...] = jnp.zeros_like(dsc_ref)
            dnw_ref[...] = jnp.zeros_like(dnw_ref)
            ds_ref[...] = jnp.zeros_like(ds_ref)

        lane = lax.broadcasted_iota(jnp.int32, (CHUNK, 128), 1)
        ds = ds_ref[...]
        for sub in reversed(range(SUB)):
            rows = _chunk_rows(sub)
            q, k, v, a, bb, og, alog, dtb, nw_ = _gdn_args((q_ref, k_ref, v_ref, og_ref, ab_ref, sc_ref, nw_ref), bsz, rows)
            _, vjp = jax.vjp(functools.partial(gdn_chunk, tinv=th_ref[sub]), q, k, v, a, bb, og, sh_ref[sub], alog, dtb, nw_)
            dq, dk, dv, da, db, dog, ds, dalog, ddtb, dnw = vjp((_heads(dy_ref, bsz, rows, 128), ds))
            dnw_ref[...] += dnw
            for b in range(bsz):
                dab = jnp.zeros((CHUNK, 128), F32)
                for h in range(H):
                    i = H * b + h
                    dc_ref[b, rows, 128 * h:128 * h + 128] = dq[i]
                    dc_ref[b, rows, 512 + 128 * h:512 + 128 * h + 128] = dk[i]
                    dc_ref[b, rows, 1024 + 128 * h:1024 + 128 * h + 128] = dv[i]
                    dpd_ref[b, rows, 128 * h:128 * h + 128] = dog[i].astype(MXU)
                    dab = dab + jnp.where(lane == h, da[i], 0.0) + jnp.where(lane == H + h, db[i], 0.0)
                    dsc_ref[0:1, h:h + 1] += dalog[i]
                    dsc_ref[1:2, h:h + 1] += ddtb[i]
                dpd_ref[b, rows, 512:640] = dab.astype(MXU)
        ds_ref[...] = ds

    return pl.pallas_call(
        body, name="gdn_bwd", grid=(nc // SUB,),
        in_specs=specs + [pl.BlockSpec((SUB, nh, DV, DV), lambda n: (n_of(n), 0, 0, 0)),
                          pl.BlockSpec((SUB, nh, CHUNK, CHUNK), lambda n: (n_of(n), 0, 0, 0)),
                          pl.BlockSpec((bsz, SUB * CHUNK, 512), lambda n: (0, n_of(n), 1)),
                          _full((2, 128)), _full((1, 128))],
        out_specs=[pl.BlockSpec((bsz, SUB * CHUNK, W_GQKV), lambda n: (0, n_of(n), 0)),
                   pl.BlockSpec((bsz, SUB * CHUNK, W_GDN), lambda n: (0, n_of(n), 0)),
                   _full((2, 128)), _full((1, 128))],
        out_shape=[jax.ShapeDtypeStruct(conv.shape, F32), jax.ShapeDtypeStruct(pd.shape, MXU),
                   jax.ShapeDtypeStruct((2, 128), F32), jax.ShapeDtypeStruct((1, 128), F32)],
        scratch_shapes=[pltpu.VMEM((nh, DV, DV), F32)],
        compiler_params=_params(("arbitrary",)),
    )(conv, conv, conv, pd, pd, s_hist, t_hist, dyin, sc, nw)


def _head_call(x, ya, yb, wout, mod3, lnw, lnb, tgt):
    bsz, t, _ = x.shape
    tm = min(512, t)
    rows = min(256, tm)

    def body(x_ref, ya_ref, yb_ref, w_ref, mod_ref, lnw_ref, lnb_ref, t_ref,
             dyin_ref, dxa_ref, dgate_ref, dw_ref, dlnw_ref, dlnb_ref, loss_ref):
        b, i = pl.program_id(0), pl.program_id(1)

        @pl.when((b == 0) & (i == 0))
        def _():
            dw_ref[...] = jnp.zeros_like(dw_ref)
            dlnw_ref[...] = jnp.zeros_like(dlnw_ref)
            dlnb_ref[...] = jnp.zeros_like(dlnb_ref)
            loss_ref[...] = jnp.zeros_like(loss_ref)

        @pl.when(i == 0)
        def _():
            dgate_ref[...] = jnp.zeros_like(dgate_ref)

        w = w_ref[...]
        parts = [slice(p * rows, (p + 1) * rows) for p in range(tm // rows)]
        yin = [jnp.concatenate([ya_ref[0, rs, :], yb_ref[0, rs, :]], axis=-1).astype(MXU) for rs in parts]
        y = [jnp.dot(yi, w, preferred_element_type=F32) for yi in yin]
        for rs, yi, y_p in zip(parts, yin, y):
            loss, vjp = jax.vjp(head_fn, x_ref[0, rs, :], y_p, mod_ref[0, 2:3, :], lnw_ref[...], lnb_ref[...], t_ref[0, rs, :])
            dx, dy, dgate, dlnw, dlnb, _ = vjp(jnp.ones((1, 1), F32))
            dyb = dy.astype(MXU)
            dyin_ref[0, rs, :] = lax.dot_general(dyb, w, (((1,), (1,)), ((), ())), preferred_element_type=F32)
            dw_ref[...] += lax.dot_general(yi, dyb, (((0,), (0,)), ((), ())), preferred_element_type=F32)
            dxa_ref[0, rs, :] = dx
            dgate_ref[0] += dgate
            dlnw_ref[...] += dlnw
            dlnb_ref[...] += dlnb
            loss_ref[...] += jnp.broadcast_to(loss, (1, 128))

    tok = lambda w, j=0: pl.BlockSpec((1, tm, w), lambda b, i: (b, i, j))
    row = pl.BlockSpec((1, D), lambda b, i: (0, 0))
    return pl.pallas_call(
        body, name="head", grid=(bsz, t // tm),
        in_specs=[tok(D), tok(512), tok(512), pl.BlockSpec((D, D), lambda b, i: (0, 0)),
                  pl.BlockSpec((1, 3, D), lambda b, i: (b, 0, 0)), row, row, tok(D)],
        out_specs=[tok(D), tok(D), pl.BlockSpec((1, 1, D), lambda b, i: (b, 0, 0)),
                   pl.BlockSpec((D, D), lambda b, i: (0, 0)), row, row, pl.BlockSpec((1, 128), lambda b, i: (0, 0))],
        out_shape=[jax.ShapeDtypeStruct(x.shape, F32), jax.ShapeDtypeStruct(x.shape, F32),
                   jax.ShapeDtypeStruct((bsz, 1, D), F32), jax.ShapeDtypeStruct((D, D), F32),
                   jax.ShapeDtypeStruct((1, D), F32), jax.ShapeDtypeStruct((1, D), F32),
                   jax.ShapeDtypeStruct((1, 128), F32)],
        compiler_params=_params(("arbitrary", "arbitrary")),
    )(x, ya, yb, wout, mod3, lnw, lnb, tgt)


def _dh_call(dpg, dpq, dpd, wpt, x, mod3, dxa):
    bsz, t, _ = x.shape
    tm = min(512, t)

    def body(dg_ref, dq_ref, dd_ref, w_ref, x_ref, mod_ref, dxa_ref, gx_ref, dmod_ref):
        @pl.when(pl.program_id(1) == 0)
        def _():
            dmod_ref[...] = jnp.zeros_like(dmod_ref)

        mm = lambda a, lo, hi: jnp.dot(a.astype(MXU), w_ref[lo:hi, :], preferred_element_type=F32)
        dh = mm(dg_ref[0], 0, W_GLA) + mm(dq_ref[0], W_GLA, W_GLA + W_GQKV) + mm(dd_ref[0], W_GLA + W_GQKV, PW)
        gx_ref[0] = dh * (1.0 + mod_ref[0, 1:2, :]) + dxa_ref[0]
        dmod_ref[0, 0:1, :] += jnp.sum(dh, axis=0, keepdims=True)
        dmod_ref[0, 1:2, :] += jnp.sum(dh * x_ref[0], axis=0, keepdims=True)

    tok = lambda w: pl.BlockSpec((1, tm, w), lambda b, i: (b, i, 0))
    return pl.pallas_call(
        body, name="dh", grid=(bsz, t // tm),
        in_specs=[tok(W_GLA), tok(W_GQKV), tok(W_GDN), pl.BlockSpec((PW, D), lambda b, i: (0, 0)), tok(D),
                  pl.BlockSpec((1, 3, D), lambda b, i: (b, 0, 0)), tok(D)],
        out_specs=[tok(D), pl.BlockSpec((1, 2, D), lambda b, i: (b, 0, 0))],
        out_shape=[jax.ShapeDtypeStruct(x.shape, F32), jax.ShapeDtypeStruct((bsz, 2, D), F32)],
        compiler_params=_params(("parallel", "arbitrary")),
    )(dpg, dpq, dpd, wpt, x, mod3, dxa)


def _dw_call(x, mod3, dp):
    bsz, t, _ = x.shape
    width = dp.shape[-1]
    tm = min(512, t)

    def body(x_ref, mod_ref, dp_ref, dw_ref):
        @pl.when((pl.program_id(0) == 0) & (pl.program_id(1) == 0))
        def _():
            dw_ref[...] = jnp.zeros_like(dw_ref)

        h = x_ref[0] * (1.0 + mod_ref[0, 1:2, :]) + mod_ref[0, 0:1, :]
        dw_ref[...] += lax.dot_general(dp_ref[0].astype(MXU), h.astype(MXU), (((0,), (0,)), ((), ())),
                                       preferred_element_type=F32)

    tok = lambda w: pl.BlockSpec((1, tm, w), lambda b, i: (b, i, 0))
    return pl.pallas_call(
        body, name=f"dw{width}", grid=(bsz, t // tm),
        in_specs=[tok(D), pl.BlockSpec((1, 3, D), lambda b, i: (b, 0, 0)), tok(width)],
        out_specs=pl.BlockSpec((width, D), lambda b, i: (0, 0)),
        out_shape=jax.ShapeDtypeStruct((width, D), F32),
        compiler_params=_params(("arbitrary", "arbitrary")),
    )(x, mod3, dp)


def _adamw(w, g, m, v):
    m = ADAM_B1 * m + (1.0 - ADAM_B1) * g
    v = ADAM_B2 * v + (1.0 - ADAM_B2) * jnp.square(g)
    m_hat = m / (1.0 - ADAM_B1 ** ADAM_STEP)
    v_hat = v / (1.0 - ADAM_B2 ** ADAM_STEP)
    delta = -ADAM_LR * (m_hat / (jnp.sqrt(v_hat) + ADAM_EPS) + ADAM_WD * w)
    return delta, m, v


def _sum8(ref):
    g = ref[0].astype(F32)
    for j in range(1, NDEV):
        g = g + ref[j].astype(F32)
    return g


def _adam_sum_call(name, g8, w, m, v, cols):
    r, c = w.shape

    def body(g_ref, w_ref, m_ref, v_ref, go_ref, d_ref, mo_ref, vo_ref):
        g = _sum8(g_ref)
        go_ref[...] = g
        d_ref[...], mo_ref[...], vo_ref[...] = _adamw(w_ref[...], g, m_ref[...], v_ref[...])

    blk = pl.BlockSpec((r, cols), lambda i: (0, i))
    return pl.pallas_call(
        body, name=name, grid=(c // cols,),
        in_specs=[pl.BlockSpec((NDEV, r, cols), lambda i: (0, 0, i)), blk, blk, blk],
        out_specs=[blk] * 4, out_shape=[jax.ShapeDtypeStruct((r, c), F32)] * 4,
        compiler_params=_params(("parallel",)),
    )(g8, w, m, v)


def _adam_ada_call(c_all, dmod_cols, w, m, v):
    def body(c_ref, dm_ref, w_ref, m_ref, v_ref, go_ref, d_ref, mo_ref, vo_ref):
        g = lax.dot_general(c_ref[...].astype(MXU), dm_ref[...].astype(MXU), (((0,), (0,)), ((), ())),
                            preferred_element_type=F32)
        go_ref[...] = g
        d_ref[...], mo_ref[...], vo_ref[...] = _adamw(w_ref[...], g, m_ref[...], v_ref[...])

    return pl.pallas_call(
        body, name="adam_ada", out_shape=[jax.ShapeDtypeStruct(w.shape, F32)] * 4, compiler_params=_params(),
    )(c_all, dmod_cols, w, m, v)


def _adam_small_call(g, w, m, v):
    def body(g_ref, w_ref, m_ref, v_ref, d_ref, mo_ref, vo_ref):
        d_ref[...], mo_ref[...], vo_ref[...] = _adamw(w_ref[...], g_ref[...], m_ref[...], v_ref[...])

    return pl.pallas_call(
        body, name="adam_small", out_shape=[jax.ShapeDtypeStruct(w.shape, F32)] * 3, compiler_params=_params(),
    )(g, w, m, v)


def _small_sum_call(sp_all):
    def body(sp_ref, dmod_ref, sum_ref):
        acc = sp_ref[0, 2:3, :]
        for j in range(1, NDEV):
            acc = acc + sp_ref[j, 2:3, :]
        sum_ref[...] = acc
        for j in range(NDEV):
            dmod_ref[2 * j:2 * j + 2, :] = sp_ref[j, 0:2, :]

    return pl.pallas_call(
        body, name="small_sum", out_shape=[jax.ShapeDtypeStruct((2 * NDEV, SPW), F32), jax.ShapeDtypeStruct((1, SPW), F32)],
        compiler_params=_params(),
    )(sp_all)


def _mesh_pos():
    x, y, c = lax.axis_index("x"), lax.axis_index("y"), lax.axis_index("c")
    return x, y, c, 4 * x + 2 * y + c


def _peer(x, y, c, k):
    px = 1 - x if k & 4 else x
    py = 1 - y if k & 2 else y
    pc = 1 - c if k & 1 else c
    return (px, py, pc), 4 * px + 2 * py + pc


_ANY = pl.BlockSpec(memory_space=pl.ANY)
_VMEM = pl.BlockSpec(memory_space=pltpu.VMEM)


def _gather_call(c8, w_ada, b_sh, w_in_t):
    C_SEM, W_SEM, MOD_SEM = 0, 1, 2

    def body(c_ref, wada_ref, b_ref, win_ref, wall_ref, call_ref, mod_ref, modp, send_sems, recv_sems, loc_sem):
        x, y, c, me = _mesh_pos()

        def remote(src, dst, a, k, to):
            return pltpu.make_async_remote_copy(src_ref=src, dst_ref=dst, send_sem=send_sems.at[a, k],
                                                recv_sem=recv_sems.at[a, k], device_id=_peer(x, y, c, to)[0],
                                                device_id_type=pl.DeviceIdType.MESH)

        idx = lambda k: _peer(x, y, c, k)[1]
        sends = []
        call_ref[me] = c_ref[...]
        for k in range(1, NDEV):
            sends.append(remote(c_ref, call_ref.at[me], C_SEM, k, k))
            sends[-1].start()
        local = pltpu.make_async_copy(win_ref, wall_ref.at[me], loc_sem)
        local.start()
        for k in (1, 2, 4, 6):
            sends.append(remote(win_ref, wall_ref.at[me], W_SEM, k, k))
            sends[-1].start()
        for k in range(1, NDEV):
            remote(c_ref, call_ref.at[idx(k)], C_SEM, k, k).wait_recv()
        modp[...] = jnp.dot(call_ref[...].reshape(NDEV * 8, D).astype(MXU), wada_ref[...].astype(MXU),
                            preferred_element_type=F32) + b_ref[...]
        mod_ref[me] = modp[pl.ds(pl.multiple_of(me * 8, 8), 8), :]
        for k in range(1, NDEV):
            sends.append(remote(modp.at[pl.ds(pl.multiple_of(idx(k) * 8, 8), 8), :], mod_ref.at[me], MOD_SEM, k, k))
            sends[-1].start()
        for k in (2, 4, 6):
            remote(win_ref, wall_ref.at[idx(k)], W_SEM, k, k).wait_recv()
            sends.append(remote(wall_ref.at[idx(k)], wall_ref.at[idx(k)], W_SEM, k + 1, 1))
            sends[-1].start()
        for k in (1, 3, 5, 7):
            remote(win_ref, wall_ref.at[idx(k)], W_SEM, k, 1).wait_recv()
        for k in range(1, NDEV):
            remote(modp.at[pl.ds(0, 8), :], mod_ref.at[idx(k)], MOD_SEM, k, k).wait_recv()
        for cp in sends:
            cp.wait_send()
        local.wait()

    return pl.pallas_call(
        body, name="gather",
        out_shape=[jax.ShapeDtypeStruct((NDEV,) + w_in_t.shape, w_in_t.dtype), jax.ShapeDtypeStruct((NDEV, 8, D), F32),
                   jax.ShapeDtypeStruct((NDEV, 8, SHARD_ADA), F32)],
        in_specs=[_VMEM, _VMEM, _VMEM, _ANY], out_specs=[_ANY, _VMEM, _VMEM],
        scratch_shapes=[pltpu.VMEM((NDEV * 8, SHARD_ADA), F32), pltpu.SemaphoreType.DMA((3, NDEV)),
                        pltpu.SemaphoreType.DMA((3, NDEV)), pltpu.SemaphoreType.DMA],
        compiler_params=_params(),
    )(c8, w_ada, b_sh, w_in_t)


def _reduce_call(blocks, sp):
    nb = len(blocks)

    def body(sp_ref, *rest):
        srcs, outs = rest[:nb], rest[nb:2 * nb]
        spall_ref, send_sems, recv_sems, loc_sems = rest[2 * nb:]
        x, y, c, me = _mesh_pos()

        def remote(src, dst, a, k, dev):
            return pltpu.make_async_remote_copy(src_ref=src, dst_ref=dst, send_sem=send_sems.at[a, k],
                                                recv_sem=recv_sems.at[a, k], device_id=dev,
                                                device_id_type=pl.DeviceIdType.MESH)

        sends = []
        spall_ref[me] = sp_ref[...]
        local = [pltpu.make_async_copy(srcs[a].at[me], outs[a].at[me], loc_sems.at[a]) for a in range(nb)]
        for cp in local:
            cp.start()
        for k in range(1, NDEV):
            dev, pidx = _peer(x, y, c, k)
            sends.append(remote(sp_ref, spall_ref.at[me], nb, k, dev))
            sends[-1].start()
            for a in range(nb):
                sends.append(remote(srcs[a].at[pidx], outs[a].at[me], a, k, dev))
                sends[-1].start()
        for k in range(1, NDEV):
            dev, pidx = _peer(x, y, c, k)
            remote(sp_ref, spall_ref.at[pidx], nb, k, dev).wait_recv()
            for a in range(nb):
                remote(srcs[a].at[pidx], outs[a].at[pidx], a, k, dev).wait_recv()
        for cp in sends:
            cp.wait_send()
        for cp in local:
            cp.wait()

    return pl.pallas_call(
        body, name="reduce", out_shape=[jax.ShapeDtypeStruct(a.shape, a.dtype) for a in blocks]
        + [jax.ShapeDtypeStruct((NDEV, 8, SPW), F32)],
        in_specs=[_VMEM] + [_ANY] * nb, out_specs=[_ANY] * nb + [_VMEM],
        scratch_shapes=[pltpu.SemaphoreType.DMA((nb + 1, NDEV)), pltpu.SemaphoreType.DMA((nb + 1, NDEV)),
                        pltpu.SemaphoreType.DMA((nb,))],
        compiler_params=_params(),
    )(sp, *blocks)


_HBM = pl.BlockSpec(memory_space=pltpu.HBM)
_SEM = pl.BlockSpec(memory_space=pltpu.SEMAPHORE)
_EFFECT = pltpu.SideEffectType.DATAFLOW_SIDE_EFFECTING


def _xchg_start(name, blocks, lands, gather):
    nb = len(blocks)

    def body(*refs):
        srcs, dsts = refs[:nb], refs[nb:2 * nb]
        send_sems, recv_sems = refs[2 * nb], refs[2 * nb + 1]
        token = refs[-1]
        x, y, c, me = _mesh_pos()
        for k in range(1, NDEV):
            dev, pidx = _peer(x, y, c, k)
            for a in range(nb):
                pltpu.make_async_remote_copy(src_ref=srcs[a] if gather else srcs[a].at[pidx], dst_ref=dsts[a].at[me],
                                             send_sem=send_sems.at[NDEV * a + k], recv_sem=recv_sems.at[NDEV * a + k],
                                             device_id=dev, device_id_type=pl.DeviceIdType.MESH).start()
        token[...] = jnp.zeros_like(token)

    thru = [pltpu.HBM(a.shape, a.dtype) for a in list(blocks) + list(lands)]
    return pl.pallas_call(
        body, name=name,
        out_shape=(pltpu.SemaphoreType.DMA((nb * NDEV,)), pltpu.SemaphoreType.DMA((nb * NDEV,)), *thru,
                   jax.ShapeDtypeStruct((8, 128), F32)),
        in_specs=[_HBM] * (2 * nb), out_specs=(_SEM, _SEM, *([_HBM] * (2 * nb)), _VMEM),
        input_output_aliases={i: 2 + i for i in range(2 * nb)},
        compiler_params=pltpu.CompilerParams(has_side_effects=_EFFECT),
    )(*[pltpu.with_memory_space_constraint(a, pltpu.HBM) for a in list(blocks) + list(lands)])


def _xchg_wait(name, send_sems, recv_sems, thru, after, gather):
    nb = len(thru) // 2

    def body(*refs):
        srcs, dsts = refs[:nb], refs[nb:2 * nb]
        send_sems, recv_sems = refs[2 * nb], refs[2 * nb + 1]
        x, y, c, me = _mesh_pos()
        for k in range(1, NDEV):
            dev, pidx = _peer(x, y, c, k)
            for a in range(nb):
                cp = pltpu.make_async_remote_copy(src_ref=srcs[a] if gather else srcs[a].at[pidx], dst_ref=dsts[a].at[pidx],
                                                  send_sem=send_sems.at[NDEV * a + k], recv_sem=recv_sems.at[NDEV * a + k],
                                                  device_id=dev, device_id_type=pl.DeviceIdType.MESH)
                cp.wait_send()
                cp.wait_recv()

    out = pl.pallas_call(
        body, name=name, out_shape=tuple(pltpu.HBM(a.shape, a.dtype) for a in thru),
        in_specs=[_HBM] * (2 * nb) + [_SEM, _SEM, pl.BlockSpec(memory_space=pl.ANY)], out_specs=tuple([_HBM] * (2 * nb)),
        input_output_aliases={i: i for i in range(2 * nb)},
        compiler_params=pltpu.CompilerParams(has_side_effects=_EFFECT),
    )(*thru, send_sems, recv_sems, after)
    return out[nb:]


def _pad_cols(a, n):
    return jnp.pad(a, ((0, 0), (0, n - a.shape[1])))


def _assemble_wt(wt_full):
    q, k, v, lr, og, gqkv, ab, dog = jnp.split(wt_full, [256, 512, 1024, 1040, 1552, 3088, 3096], axis=0)
    z = lambda n: jnp.zeros((n, wt_full.shape[1]), wt_full.dtype)
    return jnp.concatenate([q, k, v, og, lr, z(112), gqkv, dog, ab, z(120)], axis=0)


def _disassemble_dwt(dw_gla, dw_gqkv, dw_gdn):
    return jnp.concatenate([dw_gla[:1024], dw_gla[1536:1552], dw_gla[1024:1536], dw_gqkv,
                            dw_gdn[512:520], dw_gdn[:512]], axis=0)


def local_grads(x, mod3, wp, late_weights, bg, gla_nw, sc, gdn_nw, lnw, lnb, tgt):
    pg, pq, pd = _proj_call(x, mod3, wp)
    wout, conv_w, wgu_p = late_weights(pq)
    conv = _conv_fwd_call(pq, conv_w)
    ya, s_gla = _gla_fwd_call(pg, wgu_p, bg, gla_nw)
    yb, s_gdn, t_gdn = _gdn_fwd_call(conv, pd, sc, gdn_nw)
    dyin, dxa, dgate, dwout, dlnw, dlnb, loss = _head_call(x, ya, yb, wout, mod3, lnw, lnb, tgt)
    dpg, dwgu, dbg, dnw_gla = _gla_bwd_call(pg, s_gla, dyin, wgu_p, bg, gla_nw)
    dconv, dpd, dsc, dnw_gdn = _gdn_bwd_call(conv, pd, s_gdn, t_gdn, dyin, sc, gdn_nw)
    dpq, dconv_w = _conv_bwd_call(dconv, pq, conv_w)
    dw_in = _disassemble_dwt(_dw_call(x, mod3, dpg), _dw_call(x, mod3, dpq), _dw_call(x, mod3, dpd))
    g = dict(loss=loss[0, 0], dw_in=dw_in, dwout=dwout, dconv_w=dconv_w[:4], dwgu=dwgu[:16], dbg=dbg, dnw_gla=dnw_gla,
             dalog=dsc[0:1, :4], ddtb=dsc[1:2, :4], dnw_gdn=dnw_gdn, dlnw=dlnw, dlnb=dlnb)
    return g, functools.partial(_finish_grad_x, dpg, dpq, dpd, wp, x, dxa, dgate)


def _finish_grad_x(dpg, dpq, dpd, wp, x, dxa, dgate, mod3):
    gx, dmod2 = _dh_call(dpg, dpq, dpd, wp, x, mod3, dxa)
    return gx, jnp.concatenate([dmod2, dgate], axis=1)


def local_step(x, mod3, wp, wout, conv_w, wgu_p, *args):
    g, finish = local_grads(x, mod3, wp, lambda _: (wout, conv_w, wgu_p), *args)
    g["gx"], g["dmod"] = finish(mod3)
    return g


def kernel(x, c, w_ada, b_ada, w_in, gla_w_gate_up, gla_b_gate, gla_norm_w, gdn_conv_w, gdn_a_log, gdn_dt_bias, gdn_norm_w, w_out, ln_w, ln_b, loss_target, m_w_ada, m_b_ada, m_w_in, m_gla_w_gate_up, m_gla_b_gate, m_gla_norm_w, m_gdn_conv_w, m_gdn_a_log, m_gdn_dt_bias, m_gdn_norm_w, m_w_out, m_ln_w, m_ln_b, v_w_ada, v_b_ada, v_w_in, v_gla_w_gate_up, v_gla_b_gate, v_gla_norm_w, v_gdn_conv_w, v_gdn_a_log, v_gdn_dt_bias, v_gdn_norm_w, v_w_out, v_ln_w, v_ln_b):
    me = 4 * lax.axis_index("x") + 2 * lax.axis_index("y") + lax.axis_index("c")
    bsz = x.shape[0]

    b_sh = lax.dynamic_slice(b_ada, (0, me * SHARD_ADA), (1, SHARD_ADA))
    c8 = jnp.pad(c, ((0, 8 - bsz), (0, 0)))
    w_in_t, m_in_t, v_in_t = (jnp.swapaxes(a[0], 0, 1) for a in (w_in, m_w_in, v_w_in))
    win_all, c_all, mod_blk = _gather_call(c8, w_ada[0], b_sh, w_in_t.astype(WIRE))
    wp = _assemble_wt(win_all.reshape(IN_COLS, D))
    mod = jnp.transpose(mod_blk[:, :bsz, :], (1, 0, 2)).reshape(bsz, 3 * D)
    mod3 = mod.reshape(bsz, 3, D)
    sc = jnp.concatenate([_pad_cols(gdn_a_log, 128), _pad_cols(gdn_dt_bias, 128)], axis=0)

    own = lambda a: lax.dynamic_update_slice(lax.empty((NDEV,) + a.shape, a.dtype), a[None], (me,) + (0,) * a.ndim)
    late = [w_out[0].astype(WIRE), gdn_conv_w[0], gla_w_gate_up[0] + 0.0 * mod_blk[0, 0, 0]]
    w_send, w_recv, *w_thru, w_token = _xchg_start("wgather_start", late, [own(a) for a in late], gather=True)

    def late_weights(pq):
        wout_all, conv_all, wgu_all = _xchg_wait("wgather_wait", w_send, w_recv, w_thru, pq, gather=True)
        return (wout_all.reshape(D, D), jnp.transpose(conv_all, (1, 0, 2)).reshape(4, W_GQKV),
                jnp.pad(jnp.transpose(wgu_all, (1, 0, 2)).reshape(16, 256), ((0, 112), (0, 0))))

    g, finish = local_grads(x, mod3 + w_token[0, 0], wp, late_weights, gla_b_gate, gla_norm_w, sc, gdn_norm_w, ln_w, ln_b,
                            loss_target)

    big = [g["dw_in"].reshape(NDEV, SHARD_IN, D).astype(WIRE), g["dwout"].reshape(NDEV, D // NDEV, D).astype(WIRE)]
    lands = [lax.dynamic_update_slice(lax.empty(a.shape, a.dtype), lax.dynamic_slice(a, (me, 0, 0), (1,) + a.shape[1:]),
                                      (me, 0, 0)) for a in big]
    send_sems, recv_sems, *thru, token = _xchg_start("xchg_start", big, lands, gather=False)
    gx, dmod = finish(mod3 + token[0, 0])
    r_in, r_out = _xchg_wait("xchg_wait", send_sems, recv_sems, thru, gx, gather=False)

    small = jnp.concatenate([g["dlnw"], g["dlnb"], g["dbg"], g["dnw_gla"], g["dnw_gdn"], _pad_cols(g["dalog"], 128),
                             _pad_cols(g["ddtb"], 128), jnp.full((1, 128), g["loss"], F32), jnp.zeros((1, 128), F32)], axis=1)
    sp = jnp.concatenate([dmod.reshape(bsz, SPW), small, jnp.zeros((8 - bsz - 1, SPW), F32)], axis=0)
    blocks = [jnp.transpose(g["dconv_w"].reshape(4, NDEV, W_GQKV // NDEV), (1, 0, 2)),
              jnp.transpose(g["dwgu"].reshape(16, NDEV, 256 // NDEV), (1, 0, 2))]
    r_conv, r_gu, sp_all = _reduce_call(blocks, sp)
    dmod_all, sums = _small_sum_call(sp_all)

    g_in, d_in, nm_in, nv_in = (jnp.swapaxes(a, 0, 1) for a in _adam_sum_call("adam_in", r_in, w_in_t, m_in_t, v_in_t, 256))
    g_out, d_out, nm_out, nv_out = _adam_sum_call("adam_out", r_out, w_out[0], m_w_out[0], v_w_out[0], D)
    g_conv, d_conv, nm_conv, nv_conv = _adam_sum_call("adam_conv", r_conv, gdn_conv_w[0], m_gdn_conv_w[0], v_gdn_conv_w[0],
                                                      W_GQKV // NDEV)
    g_gu, d_gu, nm_gu, nv_gu = _adam_sum_call("adam_gu", r_gu, gla_w_gate_up[0], m_gla_w_gate_up[0], v_gla_w_gate_up[0],
                                              256 // NDEV)
    c16 = c_all[:, :bsz, :].reshape(NDEV * bsz, D)
    g_ada, d_ada, nm_ada, nv_ada = _adam_ada_call(c16, lax.dynamic_slice(dmod_all, (0, me * SHARD_ADA), (NDEV * bsz, SHARD_ADA)),
                                                  w_ada[0], m_w_ada[0], v_w_ada[0])

    def pack(b_a, lw, lb, bgt, n1, n2, al, dt):
        return jnp.concatenate([b_a, lw, lb, bgt, n1, n2, _pad_cols(al, 128), _pad_cols(dt, 128)], axis=1).reshape(-1, 128)

    g_small = _bada_and_pack(dmod_all, sums)
    w_s = pack(b_ada, ln_w, ln_b, gla_b_gate, gla_norm_w, gdn_norm_w, gdn_a_log, gdn_dt_bias)
    m_s = pack(m_b_ada, m_ln_w, m_ln_b, m_gla_b_gate, m_gla_norm_w, m_gdn_norm_w, m_gdn_a_log, m_gdn_dt_bias)
    v_s = pack(v_b_ada, v_ln_w, v_ln_b, v_gla_b_gate, v_gla_norm_w, v_gdn_norm_w, v_gdn_a_log, v_gdn_dt_bias)
    d_s, nm_s, nv_s = _adam_small_call(g_small, w_s, m_s, v_s)

    def unpack(p):
        f = p.reshape(1, -1)
        b_a, lw, lb, bgt, n1, n2, al, dt = jnp.split(f, [3072, 4096, 5120, 5376, 5504, 5632, 5760], axis=1)
        return dict(b_ada=b_a, ln_w=lw, ln_b=lb, b_gate=bgt, gla_nw=n1, gdn_nw=n2, a_log=al[:, :4], dt_bias=dt[:, :4])

    gs, ds, ms, vs = unpack(g_small), unpack(d_s), unpack(nm_s), unpack(nv_s)
    loss = sums[0, SMALL_W]

    def group(t_ada, t_in, t_gu, t_conv, t_out, s):
        return [t_ada[None], s["b_ada"], t_in[None], t_gu[None], s["b_gate"], s["gla_nw"], t_conv[None], s["a_log"],
                s["dt_bias"], s["gdn_nw"], t_out[None], s["ln_w"], s["ln_b"]]

    return (loss, gx, *group(g_ada, g_in, g_gu, g_conv, g_out, gs), *group(d_ada, d_in, d_gu, d_conv, d_out, ds),
            *group(nm_ada, nm_in, nm_gu, nm_conv, nm_out, ms), *group(nv_ada, nv_in, nv_gu, nv_conv, nv_out, vs))


def _bada_and_pack(dmod_all, sums):
    n = dmod_all.shape[0]

    def body(dm_ref, s_ref, o_ref):
        acc = dm_ref[0:1, :]
        for j in range(1, n):
            acc = acc + dm_ref[j:j + 1, :]
        o_ref[:, 0:SPW] = acc
        o_ref[:, SPW:SPW + SMALL_W] = s_ref[:, 0:SMALL_W]

    packed = pl.pallas_call(
        body, name="bada_pack", out_shape=jax.ShapeDtypeStruct((1, SPW + SMALL_W), F32), compiler_params=_params(),
    )(dmod_all, sums)
    return packed.reshape(-1, 128)
```

```python
import functools

import jax
import jax.numpy as jnp
from jax import lax
from jax.experimental import pallas as pl
from jax.experimental.pallas import tpu as pltpu

F32 = jnp.float32
MXU = jnp.bfloat16
WIRE = jnp.bfloat16
HI = lax.Precision.HIGH

D = 1024
NDEV = 8
H = 4
GLA_DK = 64
DV = 128
CHUNK = 64
SUB = 4
LN_EPS = 1e-5
RMS_EPS = 1e-6
ALPHA = 2.0 ** 0.25
GATE_NORM = 16.0

W_GLA, W_GQKV, W_GDN = 1664, 1536, 640
PW = W_GLA + W_GQKV + W_GDN
IN_COLS = 3608
SHARD_IN = IN_COLS // NDEV
SHARD_ADA = 3 * D // NDEV
SPW = 3 * D
SMALL_W = 2816

ADAM_LR, ADAM_B1, ADAM_B2, ADAM_EPS, ADAM_WD, ADAM_STEP = 0.001, 0.9, 0.999, 1e-08, 0.01, 10

VMEM_LIMIT = 56 * 1024 * 1024


def _params(sem=None, **kw):
    if sem is not None:
        kw["dimension_semantics"] = sem
    return pltpu.CompilerParams(vmem_limit_bytes=VMEM_LIMIT, **kw)


_MM = (((2,), (1,)), ((0,), (0,)))
_NT = (((2,), (2,)), ((0,), (0,)))
_TN = (((1,), (1,)), ((0,), (0,)))


def _dg(a, b, dims):
    return lax.dot_general(a.astype(MXU), b.astype(MXU), dims, preferred_element_type=F32)


def _hdg(a, b, dims):
    return lax.dot_general(a, b, dims, precision=HI, preferred_element_type=F32)


@jax.custom_vjp
def bmm(a, b):
    return _dg(a, b, _MM)


bmm.defvjp(lambda a, b: (_dg(a, b, _MM), (a, b)), lambda r, g: (_dg(g, r[1], _NT), _dg(r[0], g, _TN)))


@jax.custom_vjp
def bnt(a, b):
    return _dg(a, b, _NT)


bnt.defvjp(lambda a, b: (_dg(a, b, _NT), (a, b)), lambda r, g: (_dg(g, r[1], _MM), _dg(g, r[0], _TN)))


@jax.custom_vjp
def btn(a, b):
    return _dg(a, b, _TN)


btn.defvjp(lambda a, b: (_dg(a, b, _TN), (a, b)), lambda r, g: (_dg(r[1], g, _NT), _dg(r[0], g, _MM)))


def unit_lower_inverse(a):
    n = a.shape[-1]
    r, c = _iotas(n)
    p = -a
    t = (r == c).astype(F32) + p
    for _ in range(5):
        p = _hdg(p, p, _MM)
        t = t + _hdg(t, p, _MM)
    return t


@jax.custom_vjp
def unit_lower_solve(a, t, r1, r2):
    return _dg(t, r1, _MM), _dg(t, r2, _MM)


def _solve_fwd(a, t, r1, r2):
    s1, s2 = _dg(t, r1, _MM), _dg(t, r2, _MM)
    return (s1, s2), (t, s1, s2)


def _solve_bwd(res, g):
    t, s1, s2 = res
    d1, d2 = _dg(t, g[0], _TN), _dg(t, g[1], _TN)
    return -(_dg(d1, s1, _NT) + _dg(d2, s2, _NT)), jnp.zeros_like(t), d1, d2


unit_lower_solve.defvjp(_solve_fwd, _solve_bwd)


def _iotas(n):
    return lax.broadcasted_iota(jnp.int32, (n, n), 0), lax.broadcasted_iota(jnp.int32, (n, n), 1)


def _col_to_row(col, eye):
    return jnp.sum(jnp.where(eye, col, 0.0), axis=1, keepdims=True)


def _row_to_col(row, eye):
    return jnp.sum(jnp.where(eye, row, 0.0), axis=2, keepdims=True)


def _pick_row(m, i):
    r = lax.broadcasted_iota(jnp.int32, m.shape, 1)
    return jnp.sum(jnp.where(r == i, m, 0.0), axis=1, keepdims=True)


def _rms_gate(o, nw, og):
    on = o * lax.rsqrt(jnp.mean(o * o, axis=-1, keepdims=True) + RMS_EPS) * nw
    return on * jax.nn.silu(og)


def gla_chunk(q, k, v, lr, og, s, wgu, bg, nw):
    n, c, _ = q.shape
    r, cc = _iotas(c)
    causal = r >= cc
    qs = q * (GLA_DK ** -0.5)
    z = bmm(lr, wgu) + bg
    g = jax.nn.log_sigmoid(z) / GATE_NORM
    b = _hdg(jnp.broadcast_to(causal.astype(F32), (n, c, c)), g, _MM)
    bref = _pick_row(b, c // 2 - 1)
    blast = _pick_row(b, c - 1)
    att = jnp.where(causal, bnt(qs * jnp.exp(b - bref), k * jnp.exp(bref - b)), 0.0)
    o = bmm(att, v) + bmm(qs * jnp.exp(b), s)
    rk, ck = _iotas(GLA_DK)
    s_new = _row_to_col(jnp.exp(blast), rk == ck) * s + btn(k * jnp.exp(blast - b), v)
    return _rms_gate(o, nw, og), s_new


def gdn_chunk(cq, ck, cv, a, bb, og, s, alog, dtb, nw, tinv=None):
    c = cq.shape[1]
    r, cc = _iotas(c)
    eye, causal, strict = r == cc, r >= cc, r > cc
    q, k, v = jax.nn.silu(cq), jax.nn.silu(ck), jax.nn.silu(cv)
    q = q * lax.rsqrt(jnp.sum(q * q, axis=-1, keepdims=True) + RMS_EPS) * (DV ** -0.5)
    k = k * lax.rsqrt(jnp.sum(k * k, axis=-1, keepdims=True) + RMS_EPS)
    g = -jnp.exp(alog) * jax.nn.softplus(a + dtb)
    beta = jax.nn.sigmoid(bb)
    d = jnp.sum(jnp.where(causal, _col_to_row(g, eye), 0.0), axis=2, keepdims=True)
    el = jnp.exp(jnp.where(causal, d - _col_to_row(d, eye), -jnp.inf))
    kb = k * beta
    amat = jnp.where(strict, bnt(kb, k) * el, 0.0)
    t = unit_lower_inverse(amat) if tinv is None else tinv
    u, w = unit_lower_solve(amat, t, v * beta, kb * jnp.exp(d))
    qk = jnp.where(causal, bnt(q, k) * el, 0.0)
    dlast = _pick_row(d, c - 1)
    v_new = u - bmm(w, s)
    o = bmm(q * jnp.exp(d), s) + bmm(qk, v_new)
    s_new = jnp.exp(dlast) * s + btn(k * jnp.exp(dlast - d), v_new)
    y = _rms_gate(o, nw, og)
    return (y, s_new, t) if tinv is None else (y, s_new)


def head_fn(x, y, gate, lnw, lnb, tgt):
    u = ALPHA * x + (1.0 + gate) * y
    mu = jnp.mean(u, axis=-1, keepdims=True)
    var = jnp.mean(jnp.square(u - mu), axis=-1, keepdims=True)
    out = (u - mu) * lax.rsqrt(var + LN_EPS) * lnw + lnb
    err = jnp.square(out - tgt)
    return 0.5 * jnp.sum(jnp.mean(err, axis=-1, keepdims=True), axis=0, keepdims=True)


def _proj_call(x, mod3, wpt):
    bsz, t, _ = x.shape
    tm = min(512, t)

    def body(x_ref, mod_ref, w_ref, pg_ref, pq_ref, pd_ref):
        h = (x_ref[0] * (1.0 + mod_ref[0, 1:2, :]) + mod_ref[0, 0:1, :]).astype(MXU)
        nt = lambda lo, hi: lax.dot_general(h, w_ref[lo:hi, :], (((1,), (1,)), ((), ())), preferred_element_type=F32)
        pg_ref[0] = nt(0, W_GLA)
        pq_ref[0] = nt(W_GLA, W_GLA + W_GQKV)
        pd_ref[0] = nt(W_GLA + W_GQKV, PW)

    tok = lambda w: pl.BlockSpec((1, tm, w), lambda b, i: (b, i, 0))
    return pl.pallas_call(
        body, name="proj", grid=(bsz, t // tm),
        in_specs=[tok(D), pl.BlockSpec((1, 3, D), lambda b, i: (b, 0, 0)), pl.BlockSpec((PW, D), lambda b, i: (0, 0))],
        out_specs=[tok(W_GLA), tok(W_GQKV), tok(W_GDN)],
        out_shape=[jax.ShapeDtypeStruct((bsz, t, w), F32) for w in (W_GLA, W_GQKV, W_GDN)],
        compiler_params=_params(("parallel", "parallel")),
    )(x, mod3, wpt)


_CONV_ROWS = 16


def _conv_fwd_call(pq, conv_w):
    bsz, t, _ = pq.shape
    tt = min(512, t)
    hb = tt // 8

    def body(x_ref, halo_ref, w_ref, o_ref, buf):
        i = pl.program_id(1)
        buf[0:8, :] = jnp.where(i > 0, halo_ref[0], 0.0)
        buf[8:, :] = x_ref[0]
        for j in range(W_GQKV // 128):
            ls = slice(128 * j, 128 * j + 128)
            wj = [w_ref[k:k + 1, ls] for k in range(4)]

            def rows(r, carry):
                base = pl.multiple_of(r * _CONV_ROWS, _CONV_ROWS)
                win = buf[pl.ds(base, _CONV_ROWS + 8), ls]
                acc = wj[0] * win[5:5 + _CONV_ROWS, :]
                for k in range(1, 4):
                    acc = acc + wj[k] * win[5 + k:5 + k + _CONV_ROWS, :]
                o_ref[0, pl.ds(base, _CONV_ROWS), ls] = acc
                return carry

            lax.fori_loop(0, tt // _CONV_ROWS, rows, 0, unroll=4)

    return pl.pallas_call(
        body, name="conv_fwd", grid=(bsz, t // tt),
        in_specs=[pl.BlockSpec((1, tt, W_GQKV), lambda b, i: (b, i, 0)),
                  pl.BlockSpec((1, 8, W_GQKV), lambda b, i: (b, jnp.maximum(i * hb - 1, 0), 0)),
                  pl.BlockSpec((4, W_GQKV), lambda b, i: (0, 0))],
        out_specs=pl.BlockSpec((1, tt, W_GQKV), lambda b, i: (b, i, 0)),
        out_shape=jax.ShapeDtypeStruct(pq.shape, F32),
        scratch_shapes=[pltpu.VMEM((tt + 8, W_GQKV), F32)],
        compiler_params=_params(("parallel", "parallel")),
    )(pq, pq, conv_w)


def _conv_bwd_call(dconv, pq, conv_w):
    bsz, t, _ = pq.shape
    tt = min(512, t)
    hb = tt // 8
    nt_ = t // tt

    def body(d_ref, dnext_ref, x_ref, w_ref, din_ref, dw_ref, dbuf):
        b, i = pl.program_id(0), pl.program_id(1)

        @pl.when((b == 0) & (i == 0))
        def _():
            dw_ref[...] = jnp.zeros_like(dw_ref)

        dbuf[0:tt, :] = d_ref[0]
        dbuf[tt:, :] = jnp.where(i < nt_ - 1, dnext_ref[0], 0.0)
        xin = x_ref[0]
        acc = None
        for k in range(4):
            dsh = dbuf[pl.ds(3 - k, tt), :]
            acc = w_ref[k:k + 1, :] * dsh if acc is None else acc + w_ref[k:k + 1, :] * dsh
            dw_ref[k:k + 1, :] += jnp.sum(xin * dsh, axis=0, keepdims=True)
        din_ref[0] = acc.astype(MXU)

    tile = pl.BlockSpec((1, tt, W_GQKV), lambda b, i: (b, i, 0))
    return pl.pallas_call(
        body, name="conv_bwd", grid=(bsz, nt_),
        in_specs=[tile, pl.BlockSpec((1, 8, W_GQKV), lambda b, i: (b, jnp.minimum((i + 1) * hb, t // 8 - 1), 0)),
                  tile, pl.BlockSpec((4, W_GQKV), lambda b, i: (0, 0))],
        out_specs=[tile, pl.BlockSpec((8, W_GQKV), lambda b, i: (0, 0))],
        out_shape=[jax.ShapeDtypeStruct(pq.shape, MXU), jax.ShapeDtypeStruct((8, W_GQKV), F32)],
        scratch_shapes=[pltpu.VMEM((tt + 8, W_GQKV), F32)],
        compiler_params=_params(("arbitrary", "arbitrary")),
    )(dconv, dconv, pq, conv_w)


def _chunk_specs(nc, rev, bsz, cols):
    steps = nc // SUB
    n_of = (lambda n: steps - 1 - n) if rev else (lambda n: n)
    return n_of, [pl.BlockSpec((bsz, SUB * CHUNK, w), lambda n, j=j: (0, n_of(n), j)) for w, j in cols]


def _full(shape):
    return pl.BlockSpec(shape, lambda n: (0,) * len(shape))


def _heads(ref, bsz, rows, width, off=0):
    return jnp.stack([ref[b, rows, off + width * h:off + width * (h + 1)] for b in range(bsz) for h in range(H)])


def _chunk_rows(sub):
    return slice(CHUNK * sub, CHUNK * (sub + 1))


def _per_head(ref, bsz, width, rows=slice(None)):
    return jnp.stack([ref[rows, width * h:width * (h + 1)] for _ in range(bsz) for h in range(H)])


_GLA_COLS = [(256, 0), (256, 1), (512, 1), (512, 2), (128, 12)]


def _gla_args(refs, bsz, rows):
    q_ref, k_ref, v_ref, og_ref, lr_ref, wgu_ref, bg_ref, nw_ref = refs
    lr = jnp.stack([lr_ref[b, rows, :] for b in range(bsz) for _ in range(H)])
    return (_heads(q_ref, bsz, rows, 64), _heads(k_ref, bsz, rows, 64), _heads(v_ref, bsz, rows, 128), lr,
            _heads(og_ref, bsz, rows, 128), _per_head(wgu_ref, bsz, 64), _per_head(bg_ref, bsz, 64), nw_ref[...])


def _gla_fwd_call(pg, wgu, bg, nw):
    bsz, t, _ = pg.shape
    nc = t // CHUNK
    nh = bsz * H
    _, specs = _chunk_specs(nc, False, bsz, _GLA_COLS)

    def body(q_ref, k_ref, v_ref, og_ref, lr_ref, wgu_ref, bg_ref, nw_ref, y_ref, sh_ref, s_ref):
        @pl.when(pl.program_id(0) == 0)
        def _():
            s_ref[...] = jnp.zeros_like(s_ref)

        s = s_ref[...]
        for sub in range(SUB):
            rows = _chunk_rows(sub)
            q, k, v, lr, og, w, b_, nw_ = _gla_args((q_ref, k_ref, v_ref, og_ref, lr_ref, wgu_ref, bg_ref, nw_ref), bsz, rows)
            sh_ref[sub] = s
            y, s = gla_chunk(q, k, v, lr, og, s, w, b_, nw_)
            for b in range(bsz):
                for h in range(H):
                    y_ref[b, rows, 128 * h:128 * h + 128] = y[H * b + h].astype(MXU)
        s_ref[...] = s

    return pl.pallas_call(
        body, name="gla_fwd", grid=(nc // SUB,),
        in_specs=specs + [_full((128, 256)), _full((1, 256)), _full((1, 128))],
        out_specs=[pl.BlockSpec((bsz, SUB * CHUNK, 512), lambda n: (0, n, 0)),
                   pl.BlockSpec((SUB, nh, GLA_DK, DV), lambda n: (n, 0, 0, 0))],
        out_shape=[jax.ShapeDtypeStruct((bsz, t, 512), MXU), jax.ShapeDtypeStruct((nc, nh, GLA_DK, DV), F32)],
        scratch_shapes=[pltpu.VMEM((nh, GLA_DK, DV), F32)],
        compiler_params=_params(("arbitrary",)),
    )(pg, pg, pg, pg, pg, wgu, bg, nw)


def _gla_bwd_call(pg, s_hist, dyin, wgu, bg, nw):
    bsz, t, _ = pg.shape
    nc = t // CHUNK
    nh = bsz * H
    n_of, specs = _chunk_specs(nc, True, bsz, _GLA_COLS)

    def body(q_ref, k_ref, v_ref, og_ref, lr_ref, sh_ref, dy_ref, wgu_ref, bg_ref, nw_ref,
             dp_ref, dwgu_ref, dbg_ref, dnw_ref, ds_ref):
        @pl.when(pl.program_id(0) == 0)
        def _():
            dwgu_ref[...] = jnp.zeros_like(dwgu_ref)
            dbg_ref[...] = jnp.zeros_like(dbg_ref)
            dnw_ref[...] = jnp.zeros_like(dnw_ref)
            ds_ref[...] = jnp.zeros_like(ds_ref)

        ds = ds_ref[...]
        for sub in reversed(range(SUB)):
            rows = _chunk_rows(sub)
            q, k, v, lr, og, w, b_, nw_ = _gla_args((q_ref, k_ref, v_ref, og_ref, lr_ref, wgu_ref, bg_ref, nw_ref), bsz, rows)
            _, vjp = jax.vjp(gla_chunk, q, k, v, lr, og, sh_ref[sub], w, b_, nw_)
            dq, dk, dv, dlr, dog, ds, dwgu, dbg, dnw = vjp((_heads(dy_ref, bsz, rows, 128), ds))
            dnw_ref[...] += dnw
            for b in range(bsz):
                for h in range(H):
                    i = H * b + h
                    dp_ref[b, rows, 512 + 128 * h:512 + 128 * h + 128] = dv[i].astype(MXU)
                    dp_ref[b, rows, 1024 + 128 * h:1024 + 128 * h + 128] = dog[i].astype(MXU)
                    dwgu_ref[:, 64 * h:64 * h + 64] += dwgu[i]
                    dbg_ref[:, 64 * h:64 * h + 64] += dbg[i]
                for j in range(H // 2):
                    dp_ref[b, rows, 128 * j:128 * j + 128] = jnp.concatenate(
                        [dq[H * b + 2 * j], dq[H * b + 2 * j + 1]], axis=-1).astype(MXU)
                    dp_ref[b, rows, 256 + 128 * j:256 + 128 * j + 128] = jnp.concatenate(
                        [dk[H * b + 2 * j], dk[H * b + 2 * j + 1]], axis=-1).astype(MXU)
                dp_ref[b, rows, 1536:1664] = (dlr[H * b] + dlr[H * b + 1] + dlr[H * b + 2] + dlr[H * b + 3]).astype(MXU)
        ds_ref[...] = ds

    return pl.pallas_call(
        body, name="gla_bwd", grid=(nc // SUB,),
        in_specs=specs + [pl.BlockSpec((SUB, nh, GLA_DK, DV), lambda n: (n_of(n), 0, 0, 0)),
                          pl.BlockSpec((bsz, SUB * CHUNK, 512), lambda n: (0, n_of(n), 0)),
                          _full((128, 256)), _full((1, 256)), _full((1, 128))],
        out_specs=[pl.BlockSpec((bsz, SUB * CHUNK, W_GLA), lambda n: (0, n_of(n), 0)),
                   _full((128, 256)), _full((1, 256)), _full((1, 128))],
        out_shape=[jax.ShapeDtypeStruct(pg.shape, MXU), jax.ShapeDtypeStruct((128, 256), F32),
                   jax.ShapeDtypeStruct((1, 256), F32), jax.ShapeDtypeStruct((1, 128), F32)],
        scratch_shapes=[pltpu.VMEM((nh, GLA_DK, DV), F32)],
        compiler_params=_params(("arbitrary",)),
    )(pg, pg, pg, pg, pg, s_hist, dyin, wgu, bg, nw)


_GDN_COLS = [(512, 0), (512, 1), (512, 2), (512, 0), (128, 4)]


def _gdn_args(refs, bsz, rows):
    q_ref, k_ref, v_ref, og_ref, ab_ref, sc_ref, nw_ref = refs
    return (_heads(q_ref, bsz, rows, 128), _heads(k_ref, bsz, rows, 128), _heads(v_ref, bsz, rows, 128),
            _heads(ab_ref, bsz, rows, 1), _heads(ab_ref, bsz, rows, 1, off=H), _heads(og_ref, bsz, rows, 128),
            _per_head(sc_ref, bsz, 1, slice(0, 1)), _per_head(sc_ref, bsz, 1, slice(1, 2)), nw_ref[...])


def _gdn_fwd_call(conv, pd, sc, nw):
    bsz, t, _ = conv.shape
    nc = t // CHUNK
    nh = bsz * H
    _, specs = _chunk_specs(nc, False, bsz, _GDN_COLS)

    def body(q_ref, k_ref, v_ref, og_ref, ab_ref, sc_ref, nw_ref, y_ref, sh_ref, th_ref, s_ref):
        @pl.when(pl.program_id(0) == 0)
        def _():
            s_ref[...] = jnp.zeros_like(s_ref)

        s = s_ref[...]
        for sub in range(SUB):
            rows = _chunk_rows(sub)
            q, k, v, a, bb, og, alog, dtb, nw_ = _gdn_args((q_ref, k_ref, v_ref, og_ref, ab_ref, sc_ref, nw_ref), bsz, rows)
            sh_ref[sub] = s
            y, s, tinv = gdn_chunk(q, k, v, a, bb, og, s, alog, dtb, nw_)
            th_ref[sub] = tinv
            for b in range(bsz):
                for h in range(H):
                    y_ref[b, rows, 128 * h:128 * h + 128] = y[H * b + h].astype(MXU)
        s_ref[...] = s

    return pl.pallas_call(
        body, name="gdn_fwd", grid=(nc // SUB,),
        in_specs=specs + [_full((2, 128)), _full((1, 128))],
        out_specs=[pl.BlockSpec((bsz, SUB * CHUNK, 512), lambda n: (0, n, 0)),
                   pl.BlockSpec((SUB, nh, DV, DV), lambda n: (n, 0, 0, 0)),
                   pl.BlockSpec((SUB, nh, CHUNK, CHUNK), lambda n: (n, 0, 0, 0))],
        out_shape=[jax.ShapeDtypeStruct((bsz, t, 512), MXU), jax.ShapeDtypeStruct((nc, nh, DV, DV), F32),
                   jax.ShapeDtypeStruct((nc, nh, CHUNK, CHUNK), F32)],
        scratch_shapes=[pltpu.VMEM((nh, DV, DV), F32)],
        compiler_params=_params(("arbitrary",)),
    )(conv, conv, conv, pd, pd, sc, nw)


def _gdn_bwd_call(conv, pd, s_hist, t_hist, dyin, sc, nw):
    bsz, t, _ = conv.shape
    nc = t // CHUNK
    nh = bsz * H
    n_of, specs = _chunk_specs(nc, True, bsz, _GDN_COLS)

    def body(q_ref, k_ref, v_ref, og_ref, ab_ref, sh_ref, th_ref, dy_ref, sc_ref, nw_ref,
             dc_ref, dpd_ref, dsc_ref, dnw_ref, ds_ref):
        @pl.when(pl.program_id(0) == 0)
        def _():
            dsc_ref[...] = jnp.zeros_like(dsc_ref)
            dnw_ref[...] = jnp.zeros_like(dnw_ref)
            ds_ref[...] = jnp.zeros_like(ds_ref)

        lane = lax.broadcasted_iota(jnp.int32, (CHUNK, 128), 1)
        ds = ds_ref[...]
        for sub in reversed(range(SUB)):
            rows = _chunk_rows(sub)
            q, k, v, a, bb, og, alog, dtb, nw_ = _gdn_args((q_ref, k_ref, v_ref, og_ref, ab_ref, sc_ref, nw_ref), bsz, rows)
            _, vjp = jax.vjp(functools.partial(gdn_chunk, tinv=th_ref[sub]), q, k, v, a, bb, og, sh_ref[sub], alog, dtb, nw_)
            dq, dk, dv, da, db, dog, ds, dalog, ddtb, dnw = vjp((_heads(dy_ref, bsz, rows, 128), ds))
            dnw_ref[...] += dnw
            for b in range(bsz):
                dab = jnp.zeros((CHUNK, 128), F32)
                for h in range(H):
                    i = H * b + h
                    dc_ref[b, rows, 128 * h:128 * h + 128] = dq[i]
                    dc_ref[b, rows, 512 + 128 * h:512 + 128 * h + 128] = dk[i]
                    dc_ref[b, rows, 1024 + 128 * h:1024 + 128 * h + 128] = dv[i]
                    dpd_ref[b, rows, 128 * h:128 * h + 128] = dog[i].astype(MXU)
                    dab = dab + jnp.where(lane == h, da[i], 0.0) + jnp.where(lane == H + h, db[i], 0.0)
                    dsc_ref[0:1, h:h + 1] += dalog[i]
                    dsc_ref[1:2, h:h + 1] += ddtb[i]
                dpd_ref[b, rows, 512:640] = dab.astype(MXU)
        ds_ref[...] = ds

    return pl.pallas_call(
        body, name="gdn_bwd", grid=(nc // SUB,),
        in_specs=specs + [pl.BlockSpec((SUB, nh, DV, DV), lambda n: (n_of(n), 0, 0, 0)),
                          pl.BlockSpec((SUB, nh, CHUNK, CHUNK), lambda n: (n_of(n), 0, 0, 0)),
                          pl.BlockSpec((bsz, SUB * CHUNK, 512), lambda n: (0, n_of(n), 1)),
                          _full((2, 128)), _full((1, 128))],
        out_specs=[pl.BlockSpec((bsz, SUB * CHUNK, W_GQKV), lambda n: (0, n_of(n), 0)),
                   pl.BlockSpec((bsz, SUB * CHUNK, W_GDN), lambda n: (0, n_of(n), 0)),
                   _full((2, 128)), _full((1, 128))],
        out_shape=[jax.ShapeDtypeStruct(conv.shape, F32), jax.ShapeDtypeStruct(pd.shape, MXU),
                   jax.ShapeDtypeStruct((2, 128), F32), jax.ShapeDtypeStruct((1, 128), F32)],
        scratch_shapes=[pltpu.VMEM((nh, DV, DV), F32)],
        compiler_params=_params(("arbitrary",)),
    )(conv, conv, conv, pd, pd, s_hist, t_hist, dyin, sc, nw)


def _head_call(x, ya, yb, wout, mod3, lnw, lnb, tgt):
    bsz, t, _ = x.shape
    tm = min(512, t)
    rows = min(256, tm)

    def body(x_ref, ya_ref, yb_ref, w_ref, mod_ref, lnw_ref, lnb_ref, t_ref,
             dyin_ref, dxa_ref, dgate_ref, dw_ref, dlnw_ref, dlnb_ref, loss_ref):
        b, i = pl.program_id(0), pl.program_id(1)

        @pl.when((b == 0) & (i == 0))
        def _():
            dw_ref[...] = jnp.zeros_like(dw_ref)
            dlnw_ref[...] = jnp.zeros_like(dlnw_ref)
            dlnb_ref[...] = jnp.zeros_like(dlnb_ref)
            loss_ref[...] = jnp.zeros_like(loss_ref)

        @pl.when(i == 0)
        def _():
            dgate_ref[...] = jnp.zeros_like(dgate_ref)

        w = w_ref[...]
        parts = [slice(p * rows, (p + 1) * rows) for p in range(tm // rows)]
        yin = [jnp.concatenate([ya_ref[0, rs, :], yb_ref[0, rs, :]], axis=-1).astype(MXU) for rs in parts]
        y = [jnp.dot(yi, w, preferred_element_type=F32) for yi in yin]
        for rs, yi, y_p in zip(parts, yin, y):
            loss, vjp = jax.vjp(head_fn, x_ref[0, rs, :], y_p, mod_ref[0, 2:3, :], lnw_ref[...], lnb_ref[...], t_ref[0, rs, :])
            dx, dy, dgate, dlnw, dlnb, _ = vjp(jnp.ones((1, 1), F32))
            dyb = dy.astype(MXU)
            dyin_ref[0, rs, :] = lax.dot_general(dyb, w, (((1,), (1,)), ((), ())), preferred_element_type=F32)
            dw_ref[...] += lax.dot_general(yi, dyb, (((0,), (0,)), ((), ())), preferred_element_type=F32)
            dxa_ref[0, rs, :] = dx
            dgate_ref[0] += dgate
            dlnw_ref[...] += dlnw
            dlnb_ref[...] += dlnb
            loss_ref[...] += jnp.broadcast_to(loss, (1, 128))

    tok = lambda w, j=0: pl.BlockSpec((1, tm, w), lambda b, i: (b, i, j))
    row = pl.BlockSpec((1, D), lambda b, i: (0, 0))
    return pl.pallas_call(
        body, name="head", grid=(bsz, t // tm),
        in_specs=[tok(D), tok(512), tok(512), pl.BlockSpec((D, D), lambda b, i: (0, 0)),
                  pl.BlockSpec((1, 3, D), lambda b, i: (b, 0, 0)), row, row, tok(D)],
        out_specs=[tok(D), tok(D), pl.BlockSpec((1, 1, D), lambda b, i: (b, 0, 0)),
                   pl.BlockSpec((D, D), lambda b, i: (0, 0)), row, row, pl.BlockSpec((1, 128), lambda b, i: (0, 0))],
        out_shape=[jax.ShapeDtypeStruct(x.shape, F32), jax.ShapeDtypeStruct(x.shape, F32),
                   jax.ShapeDtypeStruct((bsz, 1, D), F32), jax.ShapeDtypeStruct((D, D), F32),
                   jax.ShapeDtypeStruct((1, D), F32), jax.ShapeDtypeStruct((1, D), F32),
                   jax.ShapeDtypeStruct((1, 128), F32)],
        compiler_params=_params(("arbitrary", "arbitrary")),
    )(x, ya, yb, wout, mod3, lnw, lnb, tgt)


def _dh_call(dpg, dpq, dpd, wpt, x, mod3, dxa, smalls):
    bsz, t, _ = x.shape
    tm = min(512, t)
    assert bsz + 1 <= 8

    def body(dg_ref, dq_ref, dd_ref, w_ref, x_ref, mod_ref, dxa_ref, dgate_ref, dlnw_ref, dlnb_ref, dbg_ref, n1_ref, n2_ref,
             dsc_ref, loss_ref, gx_ref, sp_ref):
        b = pl.program_id(0)

        @pl.when((b == 0) & (pl.program_id(1) == 0))
        def _():
            sp_ref[...] = jnp.zeros_like(sp_ref)
            for e in range(bsz):
                sp_ref[e:e + 1, 2 * D:3 * D] = dgate_ref[e]
            off = 0
            for ref in (dlnw_ref, dlnb_ref, dbg_ref, n1_ref, n2_ref):
                sp_ref[bsz:bsz + 1, off:off + ref.shape[1]] = ref[...]
                off += ref.shape[1]
            sp_ref[bsz:bsz + 1, off:off + 128] = dsc_ref[0:1, :]
            sp_ref[bsz:bsz + 1, off + 128:off + 256] = dsc_ref[1:2, :]
            sp_ref[bsz:bsz + 1, SMALL_W:SMALL_W + 128] = loss_ref[...]

        mm = lambda a, lo, hi: jnp.dot(a.astype(MXU), w_ref[lo:hi, :], preferred_element_type=F32)
        dh = mm(dg_ref[0], 0, W_GLA) + mm(dq_ref[0], W_GLA, W_GLA + W_GQKV) + mm(dd_ref[0], W_GLA + W_GQKV, PW)
        gx_ref[0] = dh * (1.0 + mod_ref[0, 1:2, :]) + dxa_ref[0]
        dshift = jnp.sum(dh, axis=0, keepdims=True)
        dscale = jnp.sum(dh * x_ref[0], axis=0, keepdims=True)
        for e in range(bsz):
            @pl.when(b == e)
            def _():
                sp_ref[e:e + 1, 0:D] += dshift
                sp_ref[e:e + 1, D:2 * D] += dscale

    tok = lambda w: pl.BlockSpec((1, tm, w), lambda b, i: (b, i, 0))
    whole = lambda a: pl.BlockSpec(a.shape, lambda b, i: (0,) * a.ndim)
    return pl.pallas_call(
        body, name="dh", grid=(bsz, t // tm),
        in_specs=[tok(W_GLA), tok(W_GQKV), tok(W_GDN), pl.BlockSpec((PW, D), lambda b, i: (0, 0)), tok(D),
                  pl.BlockSpec((1, 3, D), lambda b, i: (b, 0, 0)), tok(D)] + [whole(a) for a in smalls],
        out_specs=[tok(D), pl.BlockSpec((8, SPW), lambda b, i: (0, 0))],
        out_shape=[jax.ShapeDtypeStruct(x.shape, F32), jax.ShapeDtypeStruct((8, SPW), F32)],
        compiler_params=_params(("arbitrary", "arbitrary")),
    )(dpg, dpq, dpd, wpt, x, mod3, dxa, *smalls)


def _dw_call(x, mod3, dpg, dpq, dpd):
    bsz, t, _ = x.shape
    tm = min(512, t)
    nsteps = bsz * (t // tm)

    def body(x_ref, mod_ref, dg_ref, dq_ref, dd_ref, dw_ref, acc):
        step = pl.program_id(0) * (t // tm) + pl.program_id(1)

        @pl.when(step == 0)
        def _():
            acc[...] = jnp.zeros_like(acc)

        h = (x_ref[0] * (1.0 + mod_ref[0, 1:2, :]) + mod_ref[0, 0:1, :]).astype(MXU)
        for ref, lo, hi in ((dg_ref, 0, W_GLA), (dq_ref, W_GLA, W_GLA + W_GQKV), (dd_ref, W_GLA + W_GQKV, PW)):
            acc[lo:hi, :] += lax.dot_general(ref[0].astype(MXU), h, (((0,), (0,)), ((), ())), preferred_element_type=F32)

        @pl.when(step == nsteps - 1)
        def _():
            dw_ref[...] = acc[...].astype(dw_ref.dtype)

    tok = lambda w: pl.BlockSpec((1, tm, w), lambda b, i: (b, i, 0))
    return pl.pallas_call(
        body, name="dw", grid=(bsz, t // tm),
        in_specs=[tok(D), pl.BlockSpec((1, 3, D), lambda b, i: (b, 0, 0)), tok(W_GLA), tok(W_GQKV), tok(W_GDN)],
        out_specs=pl.BlockSpec((PW, D), lambda b, i: (0, 0)),
        out_shape=jax.ShapeDtypeStruct((PW, D), WIRE),
        scratch_shapes=[pltpu.VMEM((PW, D), F32)],
        compiler_params=_params(("arbitrary", "arbitrary")),
    )(x, mod3, dpg, dpq, dpd)


def _adamw(w, g, m, v):
    m = ADAM_B1 * m + (1.0 - ADAM_B1) * g
    v = ADAM_B2 * v + (1.0 - ADAM_B2) * jnp.square(g)
    m_hat = m / (1.0 - ADAM_B1 ** ADAM_STEP)
    v_hat = v / (1.0 - ADAM_B2 ** ADAM_STEP)
    delta = -ADAM_LR * (m_hat / (jnp.sqrt(v_hat) + ADAM_EPS) + ADAM_WD * w)
    return delta, m, v


def _sum8(ref):
    g = ref[0].astype(F32)
    for j in range(1, NDEV):
        g = g + ref[j].astype(F32)
    return g


def _adam_sum_call(name, g8, w, m, v, cols):
    r, c = w.shape

    def body(g_ref, w_ref, m_ref, v_ref, go_ref, d_ref, mo_ref, vo_ref):
        g = _sum8(g_ref)
        go_ref[...] = g
        d_ref[...], mo_ref[...], vo_ref[...] = _adamw(w_ref[...], g, m_ref[...], v_ref[...])

    blk = pl.BlockSpec((r, cols), lambda i: (0, i))
    return pl.pallas_call(
        body, name=name, grid=(c // cols,),
        in_specs=[pl.BlockSpec((NDEV, r, cols), lambda i: (0, 0, i)), blk, blk, blk],
        out_specs=[blk] * 4, out_shape=[jax.ShapeDtypeStruct((r, c), F32)] * 4,
        compiler_params=_params(("parallel",)),
    )(g8, w, m, v)


def _adam_ada_call(c_all, dmod_cols, w, m, v):
    def body(c_ref, dm_ref, w_ref, m_ref, v_ref, go_ref, d_ref, mo_ref, vo_ref):
        g = lax.dot_general(c_ref[...].astype(MXU), dm_ref[...].astype(MXU), (((0,), (0,)), ((), ())),
                            preferred_element_type=F32)
        go_ref[...] = g
        d_ref[...], mo_ref[...], vo_ref[...] = _adamw(w_ref[...], g, m_ref[...], v_ref[...])

    return pl.pallas_call(
        body, name="adam_ada", out_shape=[jax.ShapeDtypeStruct(w.shape, F32)] * 4, compiler_params=_params(),
    )(c_all, dmod_cols, w, m, v)


_SMALL_AT = dict(ln_w=(0, 1024), ln_b=(1024, 1024), b_gate=(2048, 256), gla_nw=(2304, 128), gdn_nw=(2432, 128),
                 a_log=(2560, 4), dt_bias=(2688, 4))


def _adam_small_call(sp_all, bsz, params):
    names = list(params)

    def body(sp_ref, *refs):
        ins, outs = refs[:3 * len(names)], refs[3 * len(names):]
        dmod_ref, loss_ref, outs = outs[0], outs[1], outs[2:]
        packed = sp_ref[0, bsz:bsz + 1, :]
        for j in range(1, NDEV):
            packed = packed + sp_ref[j, bsz:bsz + 1, :]
        gb = None
        for j in range(NDEV):
            dmod_ref[bsz * j:bsz * j + bsz, :] = sp_ref[j, 0:bsz, :]
            for e in range(bsz):
                gb = sp_ref[j, e:e + 1, :] if gb is None else gb + sp_ref[j, e:e + 1, :]
        loss_ref[...] = packed[:, SMALL_W:SMALL_W + 128]
        for i, name in enumerate(names):
            if name == "b_ada":
                g = gb
            else:
                lo, n = _SMALL_AT[name]
                g = packed[:, lo:lo + n]
            w_ref, m_ref, v_ref = ins[3 * i:3 * i + 3]
            g_ref, d_ref, mo_ref, vo_ref = outs[4 * i:4 * i + 4]
            g_ref[...] = g
            d_ref[...], mo_ref[...], vo_ref[...] = _adamw(w_ref[...], g, m_ref[...], v_ref[...])

    flat = [a for name in names for a in params[name]]
    out_shape = [jax.ShapeDtypeStruct((NDEV * bsz, SPW), F32), jax.ShapeDtypeStruct((1, 128), F32)]
    out_shape += [jax.ShapeDtypeStruct(params[name][0].shape, F32) for name in names for _ in range(4)]
    res = pl.pallas_call(body, name="adam_small", out_shape=out_shape, compiler_params=_params())(sp_all, *flat)
    return res[0], res[1], {name: res[2 + 4 * i:6 + 4 * i] for i, name in enumerate(names)}


def _mesh_pos():
    x, y, c = lax.axis_index("x"), lax.axis_index("y"), lax.axis_index("c")
    return x, y, c, 4 * x + 2 * y + c


def _peer(x, y, c, k):
    px = 1 - x if k & 4 else x
    py = 1 - y if k & 2 else y
    pc = 1 - c if k & 1 else c
    return (px, py, pc), 4 * px + 2 * py + pc


_ANY = pl.BlockSpec(memory_space=pl.ANY)
_VMEM = pl.BlockSpec(memory_space=pltpu.VMEM)


def _gather_call(c8, w_ada, b_sh, w_in_t):
    C_SEM, W_SEM, MOD_SEM = 0, 1, 2

    def body(c_ref, wada_ref, b_ref, win_ref, wall_ref, call_ref, mod_ref, modp, send_sems, recv_sems, loc_sem):
        x, y, c, me = _mesh_pos()

        def remote(src, dst, a, k, to):
            return pltpu.make_async_remote_copy(src_ref=src, dst_ref=dst, send_sem=send_sems.at[a, k],
                                                recv_sem=recv_sems.at[a, k], device_id=_peer(x, y, c, to)[0],
                                                device_id_type=pl.DeviceIdType.MESH)

        idx = lambda k: _peer(x, y, c, k)[1]
        sends = []
        call_ref[me] = c_ref[...]
        for k in range(1, NDEV):
            sends.append(remote(c_ref, call_ref.at[me], C_SEM, k, k))
            sends[-1].start()
        local = pltpu.make_async_copy(win_ref, wall_ref.at[me], loc_sem)
        local.start()
        for k in (1, 2, 4, 6):
            sends.append(remote(win_ref, wall_ref.at[me], W_SEM, k, k))
            sends[-1].start()
        for k in range(1, NDEV):
            remote(c_ref, call_ref.at[idx(k)], C_SEM, k, k).wait_recv()
        modp[...] = jnp.dot(call_ref[...].reshape(NDEV * 8, D).astype(MXU), wada_ref[...].astype(MXU),
                            preferred_element_type=F32) + b_ref[...]
        mod_ref[me] = modp[pl.ds(pl.multiple_of(me * 8, 8), 8), :]
        for k in range(1, NDEV):
            sends.append(remote(modp.at[pl.ds(pl.multiple_of(idx(k) * 8, 8), 8), :], mod_ref.at[me], MOD_SEM, k, k))
            sends[-1].start()
        for k in (2, 4, 6):
            remote(win_ref, wall_ref.at[idx(k)], W_SEM, k, k).wait_recv()
            sends.append(remote(wall_ref.at[idx(k)], wall_ref.at[idx(k)], W_SEM, k + 1, 1))
            sends[-1].start()
        for k in (1, 3, 5, 7):
            remote(win_ref, wall_ref.at[idx(k)], W_SEM, k, 1).wait_recv()
        for k in range(1, NDEV):
            remote(modp.at[pl.ds(0, 8), :], mod_ref.at[idx(k)], MOD_SEM, k, k).wait_recv()
        for cp in sends:
            cp.wait_send()
        local.wait()

    return pl.pallas_call(
        body, name="gather",
        out_shape=[jax.ShapeDtypeStruct((NDEV,) + w_in_t.shape, w_in_t.dtype), jax.ShapeDtypeStruct((NDEV, 8, D), F32),
                   jax.ShapeDtypeStruct((NDEV, 8, SHARD_ADA), F32)],
        in_specs=[_VMEM, _VMEM, _VMEM, _ANY], out_specs=[_ANY, _VMEM, _VMEM],
        scratch_shapes=[pltpu.VMEM((NDEV * 8, SHARD_ADA), F32), pltpu.SemaphoreType.DMA((3, NDEV)),
                        pltpu.SemaphoreType.DMA((3, NDEV)), pltpu.SemaphoreType.DMA],
        compiler_params=_params(),
    )(c8, w_ada, b_sh, w_in_t)


def _reduce_call(blocks, sp):
    nb = len(blocks)

    def body(sp_ref, *rest):
        srcs, outs = rest[:nb], rest[nb:2 * nb]
        spall_ref, send_sems, recv_sems, loc_sems = rest[2 * nb:]
        x, y, c, me = _mesh_pos()

        def remote(src, dst, a, k, dev):
            return pltpu.make_async_remote_copy(src_ref=src, dst_ref=dst, send_sem=send_sems.at[a, k],
                                                recv_sem=recv_sems.at[a, k], device_id=dev,
                                                device_id_type=pl.DeviceIdType.MESH)

        sends = []
        spall_ref[me] = sp_ref[...]
        local = [pltpu.make_async_copy(srcs[a].at[me], outs[a].at[me], loc_sems.at[a]) for a in range(nb)]
        for cp in local:
            cp.start()
        for k in range(1, NDEV):
            dev, pidx = _peer(x, y, c, k)
            sends.append(remote(sp_ref, spall_ref.at[me], nb, k, dev))
            sends[-1].start()
            for a in range(nb):
                sends.append(remote(srcs[a].at[pidx], outs[a].at[me], a, k, dev))
                sends[-1].start()
        for k in range(1, NDEV):
            dev, pidx = _peer(x, y, c, k)
            remote(sp_ref, spall_ref.at[pidx], nb, k, dev).wait_recv()
            for a in range(nb):
                remote(srcs[a].at[pidx], outs[a].at[pidx], a, k, dev).wait_recv()
        for cp in sends:
            cp.wait_send()
        for cp in local:
            cp.wait()

    return pl.pallas_call(
        body, name="reduce", out_shape=[jax.ShapeDtypeStruct(a.shape, a.dtype) for a in blocks]
        + [jax.ShapeDtypeStruct((NDEV, 8, SPW), F32)],
        in_specs=[_VMEM] + [_ANY] * nb, out_specs=[_ANY] * nb + [_VMEM],
        scratch_shapes=[pltpu.SemaphoreType.DMA((nb + 1, NDEV)), pltpu.SemaphoreType.DMA((nb + 1, NDEV)),
                        pltpu.SemaphoreType.DMA((nb,))],
        compiler_params=_params(),
    )(sp, *blocks)


_HBM = pl.BlockSpec(memory_space=pltpu.HBM)
_SEM = pl.BlockSpec(memory_space=pltpu.SEMAPHORE)
_EFFECT = pltpu.SideEffectType.DATAFLOW_SIDE_EFFECTING


def _xchg_start(name, blocks, lands, gather):
    nb = len(blocks)

    def body(*refs):
        srcs, dsts = refs[:nb], refs[nb:2 * nb]
        send_sems, recv_sems = refs[2 * nb], refs[2 * nb + 1]
        token = refs[-1]
        x, y, c, me = _mesh_pos()
        for k in range(1, NDEV):
            dev, pidx = _peer(x, y, c, k)
            for a in range(nb):
                pltpu.make_async_remote_copy(src_ref=srcs[a] if gather else srcs[a].at[pidx], dst_ref=dsts[a].at[me],
                                             send_sem=send_sems.at[NDEV * a + k], recv_sem=recv_sems.at[NDEV * a + k],
                                             device_id=dev, device_id_type=pl.DeviceIdType.MESH).start()
        token[...] = jnp.zeros_like(token)

    thru = [pltpu.HBM(a.shape, a.dtype) for a in list(blocks) + list(lands)]
    return pl.pallas_call(
        body, name=name,
        out_shape=(pltpu.SemaphoreType.DMA((nb * NDEV,)), pltpu.SemaphoreType.DMA((nb * NDEV,)), *thru,
                   jax.ShapeDtypeStruct((8, 128), F32)),
        in_specs=[_HBM] * (2 * nb), out_specs=(_SEM, _SEM, *([_HBM] * (2 * nb)), _VMEM),
        input_output_aliases={i: 2 + i for i in range(2 * nb)},
        compiler_params=pltpu.CompilerParams(has_side_effects=_EFFECT),
    )(*[pltpu.with_memory_space_constraint(a, pltpu.HBM) for a in list(blocks) + list(lands)])


def _xchg_wait(name, send_sems, recv_sems, thru, after, gather):
    nb = len(thru) // 2

    def body(*refs):
        srcs, dsts = refs[:nb], refs[nb:2 * nb]
        send_sems, recv_sems = refs[2 * nb], refs[2 * nb + 1]
        x, y, c, me = _mesh_pos()
        for k in range(1, NDEV):
            dev, pidx = _peer(x, y, c, k)
            for a in range(nb):
                cp = pltpu.make_async_remote_copy(src_ref=srcs[a] if gather else srcs[a].at[pidx], dst_ref=dsts[a].at[pidx],
                                                  send_sem=send_sems.at[NDEV * a + k], recv_sem=recv_sems.at[NDEV * a + k],
                                                  device_id=dev, device_id_type=pl.DeviceIdType.MESH)
                cp.wait_send()
                cp.wait_recv()

    out = pl.pallas_call(
        body, name=name, out_shape=tuple(pltpu.HBM(a.shape, a.dtype) for a in thru),
        in_specs=[_HBM] * (2 * nb) + [_SEM, _SEM, pl.BlockSpec(memory_space=pl.ANY)], out_specs=tuple([_HBM] * (2 * nb)),
        input_output_aliases={i: i for i in range(2 * nb)},
        compiler_params=pltpu.CompilerParams(has_side_effects=_EFFECT),
    )(*thru, send_sems, recv_sems, after)
    return out[nb:]


def _pad_cols(a, n):
    return jnp.pad(a, ((0, 0), (0, n - a.shape[1])))


def _assemble_wt(wt_full):
    q, k, v, lr, og, gqkv, ab, dog = jnp.split(wt_full, [256, 512, 1024, 1040, 1552, 3088, 3096], axis=0)
    z = lambda n: jnp.zeros((n, wt_full.shape[1]), wt_full.dtype)
    return jnp.concatenate([q, k, v, og, lr, z(112), gqkv, dog, ab, z(120)], axis=0)


def _disassemble_dwt(dwt):
    g0, q0 = W_GLA, W_GLA + W_GQKV
    return jnp.concatenate([dwt[:1024], dwt[1536:1552], dwt[1024:1536], dwt[g0:q0], dwt[q0 + 512:q0 + 520],
                            dwt[q0:q0 + 512]], axis=0)


def local_grads(x, mod3, wp, late_weights, bg, gla_nw, sc, gdn_nw, lnw, lnb, tgt):
    pg, pq, pd = _proj_call(x, mod3, wp)
    wout, conv_w, wgu_p = late_weights(pq)
    conv = _conv_fwd_call(pq, conv_w)
    ya, s_gla = _gla_fwd_call(pg, wgu_p, bg, gla_nw)
    yb, s_gdn, t_gdn = _gdn_fwd_call(conv, pd, sc, gdn_nw)
    dyin, dxa, dgate, dwout, dlnw, dlnb, loss = _head_call(x, ya, yb, wout, mod3, lnw, lnb, tgt)
    dpg, dwgu, dbg, dnw_gla = _gla_bwd_call(pg, s_gla, dyin, wgu_p, bg, gla_nw)
    dconv, dpd, dsc, dnw_gdn = _gdn_bwd_call(conv, pd, s_gdn, t_gdn, dyin, sc, gdn_nw)
    dpq, dconv_w = _conv_bwd_call(dconv, pq, conv_w)
    dw_in = _disassemble_dwt(_dw_call(x, mod3, dpg, dpq, dpd))
    g = dict(dw_in=dw_in, dwout=dwout, dconv_w=dconv_w[:4], dwgu=dwgu[:16])
    smalls = (dgate, dlnw, dlnb, dbg, dnw_gla, dnw_gdn, dsc, loss)
    return g, lambda mod3_: _dh_call(dpg, dpq, dpd, wp, x, mod3_, dxa, smalls)


def local_step(x, mod3, wp, wout, conv_w, wgu_p, *args):
    g, finish = local_grads(x, mod3, wp, lambda _: (wout, conv_w, wgu_p), *args)
    g["gx"], sp = finish(mod3)
    bsz = x.shape[0]
    g["dmod"] = sp[:bsz].reshape(bsz, 3, D)
    g["loss"] = sp[bsz, SMALL_W]
    for name, (lo, n) in _SMALL_AT.items():
        g[name] = sp[bsz:bsz + 1, lo:lo + n]
    return g


def kernel(x, c, w_ada, b_ada, w_in, gla_w_gate_up, gla_b_gate, gla_norm_w, gdn_conv_w, gdn_a_log, gdn_dt_bias, gdn_norm_w, w_out, ln_w, ln_b, loss_target, m_w_ada, m_b_ada, m_w_in, m_gla_w_gate_up, m_gla_b_gate, m_gla_norm_w, m_gdn_conv_w, m_gdn_a_log, m_gdn_dt_bias, m_gdn_norm_w, m_w_out, m_ln_w, m_ln_b, v_w_ada, v_b_ada, v_w_in, v_gla_w_gate_up, v_gla_b_gate, v_gla_norm_w, v_gdn_conv_w, v_gdn_a_log, v_gdn_dt_bias, v_gdn_norm_w, v_w_out, v_ln_w, v_ln_b):
    me = 4 * lax.axis_index("x") + 2 * lax.axis_index("y") + lax.axis_index("c")
    bsz = x.shape[0]

    b_sh = lax.dynamic_slice(b_ada, (0, me * SHARD_ADA), (1, SHARD_ADA))
    c8 = jnp.pad(c, ((0, 8 - bsz), (0, 0)))
    w_in_t, m_in_t, v_in_t = (jnp.swapaxes(a[0], 0, 1) for a in (w_in, m_w_in, v_w_in))
    win_all, c_all, mod_blk = _gather_call(c8, w_ada[0], b_sh, w_in_t.astype(WIRE))
    wp = _assemble_wt(win_all.reshape(IN_COLS, D))
    mod = jnp.transpose(mod_blk[:, :bsz, :], (1, 0, 2)).reshape(bsz, 3 * D)
    mod3 = mod.reshape(bsz, 3, D)
    sc = jnp.concatenate([_pad_cols(gdn_a_log, 128), _pad_cols(gdn_dt_bias, 128)], axis=0)

    own = lambda a: lax.dynamic_update_slice(lax.empty((NDEV,) + a.shape, a.dtype), a[None], (me,) + (0,) * a.ndim)
    late = [w_out[0].astype(WIRE), gdn_conv_w[0], gla_w_gate_up[0] + 0.0 * mod_blk[0, 0, 0]]
    w_send, w_recv, *w_thru, w_token = _xchg_start("wgather_start", late, [own(a) for a in late], gather=True)

    def late_weights(pq):
        wout_all, conv_all, wgu_all = _xchg_wait("wgather_wait", w_send, w_recv, w_thru, pq, gather=True)
        return (wout_all.reshape(D, D), jnp.transpose(conv_all, (1, 0, 2)).reshape(4, W_GQKV),
                jnp.pad(jnp.transpose(wgu_all, (1, 0, 2)).reshape(16, 256), ((0, 112), (0, 0))))

    g, finish = local_grads(x, mod3 + w_token[0, 0], wp, late_weights, gla_b_gate, gla_norm_w, sc, gdn_norm_w, ln_w, ln_b,
                            loss_target)

    big = [g["dw_in"].reshape(NDEV, SHARD_IN, D), g["dwout"].reshape(NDEV, D // NDEV, D).astype(WIRE)]
    lands = [lax.dynamic_update_slice(lax.empty(a.shape, a.dtype), lax.dynamic_slice(a, (me, 0, 0), (1,) + a.shape[1:]),
                                      (me, 0, 0)) for a in big]
    send_sems, recv_sems, *thru, token = _xchg_start("xchg_start", big, lands, gather=False)
    gx, sp = finish(mod3 + token[0, 0])
    r_in, r_out = _xchg_wait("xchg_wait", send_sems, recv_sems, thru, gx, gather=False)
    blocks = [jnp.transpose(g["dconv_w"].reshape(4, NDEV, W_GQKV // NDEV), (1, 0, 2)),
              jnp.transpose(g["dwgu"].reshape(16, NDEV, 256 // NDEV), (1, 0, 2))]
    r_conv, r_gu, sp_all = _reduce_call(blocks, sp)

    t_in = [jnp.swapaxes(a, 0, 1) for a in _adam_sum_call("adam_in", r_in, w_in_t, m_in_t, v_in_t, 256)]
    t_out = _adam_sum_call("adam_out", r_out, w_out[0], m_w_out[0], v_w_out[0], D)
    t_conv = _adam_sum_call("adam_conv", r_conv, gdn_conv_w[0], m_gdn_conv_w[0], v_gdn_conv_w[0], W_GQKV // NDEV)
    t_gu = _adam_sum_call("adam_gu", r_gu, gla_w_gate_up[0], m_gla_w_gate_up[0], v_gla_w_gate_up[0], 256 // NDEV)
    dmod_all, loss, small = _adam_small_call(sp_all, bsz, dict(
        b_ada=(b_ada, m_b_ada, v_b_ada), ln_w=(ln_w, m_ln_w, v_ln_w), ln_b=(ln_b, m_ln_b, v_ln_b),
        b_gate=(gla_b_gate, m_gla_b_gate, v_gla_b_gate), gla_nw=(gla_norm_w, m_gla_norm_w, v_gla_norm_w),
        gdn_nw=(gdn_norm_w, m_gdn_norm_w, v_gdn_norm_w), a_log=(gdn_a_log, m_gdn_a_log, v_gdn_a_log),
        dt_bias=(gdn_dt_bias, m_gdn_dt_bias, v_gdn_dt_bias)))
    c16 = c_all[:, :bsz, :].reshape(NDEV * bsz, D)
    t_ada = _adam_ada_call(c16, lax.dynamic_slice(dmod_all, (0, me * SHARD_ADA), (NDEV * bsz, SHARD_ADA)),
                           w_ada[0], m_w_ada[0], v_w_ada[0])

    def group(i):
        s = lambda name: small[name][i]
        return [t_ada[i][None], s("b_ada"), t_in[i][None], t_gu[i][None], s("b_gate"), s("gla_nw"), t_conv[i][None],
                s("a_log"), s("dt_bias"), s("gdn_nw"), t_out[i][None], s("ln_w"), s("ln_b")]

    return (loss[0, 0], gx, *group(0), *group(1), *group(2), *group(3))
```

```python
import functools

import jax
import jax.numpy as jnp
from jax import lax
from jax.experimental import pallas as pl
from jax.experimental.pallas import tpu as pltpu

F32 = jnp.float32
MXU = jnp.bfloat16
WIRE = jnp.bfloat16
HI = lax.Precision.HIGH

D = 1024
NDEV = 8
H = 4
GLA_DK = 64
DV = 128
CHUNK = 64
SUB = 4
LN_EPS = 1e-5
RMS_EPS = 1e-6
ALPHA = 2.0 ** 0.25
GATE_NORM = 16.0

W_GLA, W_GQKV, W_GDN = 1664, 1536, 640
PW = W_GLA + W_GQKV + W_GDN
IN_COLS = 3608
SHARD_IN = IN_COLS // NDEV
SHARD_ADA = 3 * D // NDEV
SPW = 3 * D
SMALL_W = 2816

ADAM_LR, ADAM_B1, ADAM_B2, ADAM_EPS, ADAM_WD, ADAM_STEP = 0.001, 0.9, 0.999, 1e-08, 0.01, 10

VMEM_LIMIT = 56 * 1024 * 1024


def _params(sem=None, **kw):
    if sem is not None:
        kw["dimension_semantics"] = sem
    return pltpu.CompilerParams(vmem_limit_bytes=VMEM_LIMIT, **kw)


_MM = (((2,), (1,)), ((0,), (0,)))
_NT = (((2,), (2,)), ((0,), (0,)))
_TN = (((1,), (1,)), ((0,), (0,)))


def _dg(a, b, dims):
    return lax.dot_general(a.astype(MXU), b.astype(MXU), dims, preferred_element_type=F32)


def _hdg(a, b, dims):
    return lax.dot_general(a, b, dims, precision=HI, preferred_element_type=F32)


@jax.custom_vjp
def bmm(a, b):
    return _dg(a, b, _MM)


bmm.defvjp(lambda a, b: (_dg(a, b, _MM), (a, b)), lambda r, g: (_dg(g, r[1], _NT), _dg(r[0], g, _TN)))


@jax.custom_vjp
def bnt(a, b):
    return _dg(a, b, _NT)


bnt.defvjp(lambda a, b: (_dg(a, b, _NT), (a, b)), lambda r, g: (_dg(g, r[1], _MM), _dg(g, r[0], _TN)))


@jax.custom_vjp
def btn(a, b):
    return _dg(a, b, _TN)


btn.defvjp(lambda a, b: (_dg(a, b, _TN), (a, b)), lambda r, g: (_dg(r[1], g, _NT), _dg(r[0], g, _MM)))


def unit_lower_inverse(a):
    n = a.shape[-1]
    r, c = _iotas(n)
    p = -a
    t = (r == c).astype(F32) + p
    for _ in range(5):
        p = _dg(p, p, _MM)
        t = t + _dg(t, p, _MM)
    return t


@jax.custom_vjp
def unit_lower_solve(a, t, r1, r2):
    return _dg(t, r1, _MM), _dg(t, r2, _MM)


def _solve_fwd(a, t, r1, r2):
    s1, s2 = _dg(t, r1, _MM), _dg(t, r2, _MM)
    return (s1, s2), (t, s1, s2)


def _solve_bwd(res, g):
    t, s1, s2 = res
    d1, d2 = _dg(t, g[0], _TN), _dg(t, g[1], _TN)
    return -(_dg(d1, s1, _NT) + _dg(d2, s2, _NT)), jnp.zeros_like(t), d1, d2


unit_lower_solve.defvjp(_solve_fwd, _solve_bwd)


def _iotas(n):
    return lax.broadcasted_iota(jnp.int32, (n, n), 0), lax.broadcasted_iota(jnp.int32, (n, n), 1)


def _col_to_row(col, eye):
    return jnp.sum(jnp.where(eye, col, 0.0), axis=1, keepdims=True)


def _row_to_col(row, eye):
    return jnp.sum(jnp.where(eye, row, 0.0), axis=2, keepdims=True)


def _pick_row(m, i):
    r = lax.broadcasted_iota(jnp.int32, m.shape, 1)
    return jnp.sum(jnp.where(r == i, m, 0.0), axis=1, keepdims=True)


def _rms_gate(o, nw, og):
    on = o * lax.rsqrt(jnp.mean(o * o, axis=-1, keepdims=True) + RMS_EPS) * nw
    return on * jax.nn.silu(og)


def gla_chunk(q, k, v, lr, og, s, wgu, bg, nw):
    n, c, _ = q.shape
    r, cc = _iotas(c)
    causal = r >= cc
    qs = q * (GLA_DK ** -0.5)
    z = bmm(lr, wgu) + bg
    g = jax.nn.log_sigmoid(z) / GATE_NORM
    b = _hdg(jnp.broadcast_to(causal.astype(F32), (n, c, c)), g, _MM)
    bref = _pick_row(b, c // 2 - 1)
    blast = _pick_row(b, c - 1)
    att = jnp.where(causal, bnt(qs * jnp.exp(b - bref), k * jnp.exp(bref - b)), 0.0)
    o = bmm(att, v) + bmm(qs * jnp.exp(b), s)
    rk, ck = _iotas(GLA_DK)
    s_new = _row_to_col(jnp.exp(blast), rk == ck) * s + btn(k * jnp.exp(blast - b), v)
    return _rms_gate(o, nw, og), s_new


def gdn_chunk(cq, ck, cv, a, bb, og, s, alog, dtb, nw, tinv=None):
    c = cq.shape[1]
    r, cc = _iotas(c)
    eye, causal, strict = r == cc, r >= cc, r > cc
    q, k, v = jax.nn.silu(cq), jax.nn.silu(ck), jax.nn.silu(cv)
    q = q * lax.rsqrt(jnp.sum(q * q, axis=-1, keepdims=True) + RMS_EPS) * (DV ** -0.5)
    k = k * lax.rsqrt(jnp.sum(k * k, axis=-1, keepdims=True) + RMS_EPS)
    g = -jnp.exp(alog) * jax.nn.softplus(a + dtb)
    beta = jax.nn.sigmoid(bb)
    d = jnp.sum(jnp.where(causal, _col_to_row(g, eye), 0.0), axis=2, keepdims=True)
    el = jnp.exp(jnp.where(causal, d - _col_to_row(d, eye), -jnp.inf))
    kb = k * beta
    amat = jnp.where(strict, bnt(kb, k) * el, 0.0)
    t = unit_lower_inverse(amat) if tinv is None else tinv
    u, w = unit_lower_solve(amat, t, v * beta, kb * jnp.exp(d))
    qk = jnp.where(causal, bnt(q, k) * el, 0.0)
    dlast = _pick_row(d, c - 1)
    v_new = u - bmm(w, s)
    o = bmm(q * jnp.exp(d), s) + bmm(qk, v_new)
    s_new = jnp.exp(dlast) * s + btn(k * jnp.exp(dlast - d), v_new)
    y = _rms_gate(o, nw, og)
    return (y, s_new, t) if tinv is None else (y, s_new)


def head_fn(x, y, gate, lnw, lnb, tgt):
    u = ALPHA * x + (1.0 + gate) * y
    mu = jnp.mean(u, axis=-1, keepdims=True)
    var = jnp.mean(jnp.square(u - mu), axis=-1, keepdims=True)
    out = (u - mu) * lax.rsqrt(var + LN_EPS) * lnw + lnb
    err = jnp.square(out - tgt)
    return 0.5 * jnp.sum(jnp.mean(err, axis=-1, keepdims=True), axis=0, keepdims=True)


def _proj_call(x, mod3, wpt):
    bsz, t, _ = x.shape
    tm = min(512, t)

    def body(x_ref, mod_ref, w_ref, pg_ref, pq_ref, pd_ref):
        h = (x_ref[0] * (1.0 + mod_ref[0, 1:2, :]) + mod_ref[0, 0:1, :]).astype(MXU)
        nt = lambda lo, hi: lax.dot_general(h, w_ref[lo:hi, :], (((1,), (1,)), ((), ())), preferred_element_type=F32)
        pg_ref[0] = nt(0, W_GLA)
        pq_ref[0] = nt(W_GLA, W_GLA + W_GQKV)
        pd_ref[0] = nt(W_GLA + W_GQKV, PW)

    tok = lambda w: pl.BlockSpec((1, tm, w), lambda b, i: (b, i, 0))
    return pl.pallas_call(
        body, name="proj", grid=(bsz, t // tm),
        in_specs=[tok(D), pl.BlockSpec((1, 3, D), lambda b, i: (b, 0, 0)), pl.BlockSpec((PW, D), lambda b, i: (0, 0))],
        out_specs=[tok(W_GLA), tok(W_GQKV), tok(W_GDN)],
        out_shape=[jax.ShapeDtypeStruct((bsz, t, w), F32) for w in (W_GLA, W_GQKV, W_GDN)],
        compiler_params=_params(("parallel", "parallel")),
    )(x, mod3, wpt)


_CONV_ROWS = 16


def _conv_fwd_call(pq, conv_w):
    bsz, t, _ = pq.shape
    tt = min(512, t)
    hb = tt // 8

    def body(x_ref, halo_ref, w_ref, o_ref, buf):
        i = pl.program_id(1)
        buf[0:8, :] = jnp.where(i > 0, halo_ref[0], 0.0)
        buf[8:, :] = x_ref[0]
        for j in range(W_GQKV // 128):
            ls = slice(128 * j, 128 * j + 128)
            wj = [w_ref[k:k + 1, ls] for k in range(4)]

            def rows(r, carry):
                base = pl.multiple_of(r * _CONV_ROWS, _CONV_ROWS)
                win = buf[pl.ds(base, _CONV_ROWS + 8), ls]
                acc = wj[0] * win[5:5 + _CONV_ROWS, :]
                for k in range(1, 4):
                    acc = acc + wj[k] * win[5 + k:5 + k + _CONV_ROWS, :]
                o_ref[0, pl.ds(base, _CONV_ROWS), ls] = acc
                return carry

            lax.fori_loop(0, tt // _CONV_ROWS, rows, 0, unroll=4)

    return pl.pallas_call(
        body, name="conv_fwd", grid=(bsz, t // tt),
        in_specs=[pl.BlockSpec((1, tt, W_GQKV), lambda b, i: (b, i, 0)),
                  pl.BlockSpec((1, 8, W_GQKV), lambda b, i: (b, jnp.maximum(i * hb - 1, 0), 0)),
                  pl.BlockSpec((4, W_GQKV), lambda b, i: (0, 0))],
        out_specs=pl.BlockSpec((1, tt, W_GQKV), lambda b, i: (b, i, 0)),
        out_shape=jax.ShapeDtypeStruct(pq.shape, F32),
        scratch_shapes=[pltpu.VMEM((tt + 8, W_GQKV), F32)],
        compiler_params=_params(("parallel", "parallel")),
    )(pq, pq, conv_w)


def _conv_bwd_call(dconv, pq, conv_w):
    bsz, t, _ = pq.shape
    tt = min(512, t)
    hb = tt // 8
    nt_ = t // tt

    def body(d_ref, dnext_ref, x_ref, w_ref, din_ref, dw_ref, dbuf):
        b, i = pl.program_id(0), pl.program_id(1)

        @pl.when((b == 0) & (i == 0))
        def _():
            dw_ref[...] = jnp.zeros_like(dw_ref)

        dbuf[0:tt, :] = d_ref[0]
        dbuf[tt:, :] = jnp.where(i < nt_ - 1, dnext_ref[0], 0.0)
        xin = x_ref[0]
        acc = None
        for k in range(4):
            dsh = dbuf[pl.ds(3 - k, tt), :]
            acc = w_ref[k:k + 1, :] * dsh if acc is None else acc + w_ref[k:k + 1, :] * dsh
            dw_ref[k:k + 1, :] += jnp.sum(xin * dsh, axis=0, keepdims=True)
        din_ref[0] = acc.astype(MXU)

    tile = pl.BlockSpec((1, tt, W_GQKV), lambda b, i: (b, i, 0))
    return pl.pallas_call(
        body, name="conv_bwd", grid=(bsz, nt_),
        in_specs=[tile, pl.BlockSpec((1, 8, W_GQKV), lambda b, i: (b, jnp.minimum((i + 1) * hb, t // 8 - 1), 0)),
                  tile, pl.BlockSpec((4, W_GQKV), lambda b, i: (0, 0))],
        out_specs=[tile, pl.BlockSpec((8, W_GQKV), lambda b, i: (0, 0))],
        out_shape=[jax.ShapeDtypeStruct(pq.shape, MXU), jax.ShapeDtypeStruct((8, W_GQKV), F32)],
        scratch_shapes=[pltpu.VMEM((tt + 8, W_GQKV), F32)],
        compiler_params=_params(("arbitrary", "arbitrary")),
    )(dconv, dconv, pq, conv_w)


def _chunk_specs(nc, rev, bsz, cols):
    steps = nc // SUB
    n_of = (lambda n: steps - 1 - n) if rev else (lambda n: n)
    return n_of, [pl.BlockSpec((bsz, SUB * CHUNK, w), lambda n, j=j: (0, n_of(n), j)) for w, j in cols]


def _full(shape):
    return pl.BlockSpec(shape, lambda n: (0,) * len(shape))


def _heads(ref, bsz, rows, width, off=0):
    return jnp.stack([ref[b, rows, off + width * h:off + width * (h + 1)] for b in range(bsz) for h in range(H)])


def _chunk_rows(sub):
    return slice(CHUNK * sub, CHUNK * (sub + 1))


def _per_head(ref, bsz, width, rows=slice(None)):
    return jnp.stack([ref[rows, width * h:width * (h + 1)] for _ in range(bsz) for h in range(H)])


_GLA_COLS = [(256, 0), (256, 1), (512, 1), (512, 2), (128, 12)]


def _gla_args(refs, bsz, rows):
    q_ref, k_ref, v_ref, og_ref, lr_ref, wgu_ref, bg_ref, nw_ref = refs
    lr = jnp.stack([lr_ref[b, rows, :] for b in range(bsz) for _ in range(H)])
    return (_heads(q_ref, bsz, rows, 64), _heads(k_ref, bsz, rows, 64), _heads(v_ref, bsz, rows, 128), lr,
            _heads(og_ref, bsz, rows, 128), _per_head(wgu_ref, bsz, 64), _per_head(bg_ref, bsz, 64), nw_ref[...])


def _gla_fwd_call(pg, wgu, bg, nw):
    bsz, t, _ = pg.shape
    nc = t // CHUNK
    nh = bsz * H
    _, specs = _chunk_specs(nc, False, bsz, _GLA_COLS)

    def body(q_ref, k_ref, v_ref, og_ref, lr_ref, wgu_ref, bg_ref, nw_ref, y_ref, sh_ref, s_ref):
        @pl.when(pl.program_id(0) == 0)
        def _():
            s_ref[...] = jnp.zeros_like(s_ref)

        s = s_ref[...]
        for sub in range(SUB):
            rows = _chunk_rows(sub)
            q, k, v, lr, og, w, b_, nw_ = _gla_args((q_ref, k_ref, v_ref, og_ref, lr_ref, wgu_ref, bg_ref, nw_ref), bsz, rows)
            sh_ref[sub] = s
            y, s = gla_chunk(q, k, v, lr, og, s, w, b_, nw_)
            for b in range(bsz):
                for h in range(H):
                    y_ref[b, rows, 128 * h:128 * h + 128] = y[H * b + h].astype(MXU)
        s_ref[...] = s

    return pl.pallas_call(
        body, name="gla_fwd", grid=(nc // SUB,),
        in_specs=specs + [_full((128, 256)), _full((1, 256)), _full((1, 128))],
        out_specs=[pl.BlockSpec((bsz, SUB * CHUNK, 512), lambda n: (0, n, 0)),
                   pl.BlockSpec((SUB, nh, GLA_DK, DV), lambda n: (n, 0, 0, 0))],
        out_shape=[jax.ShapeDtypeStruct((bsz, t, 512), MXU), jax.ShapeDtypeStruct((nc, nh, GLA_DK, DV), F32)],
        scratch_shapes=[pltpu.VMEM((nh, GLA_DK, DV), F32)],
        compiler_params=_params(("arbitrary",)),
    )(pg, pg, pg, pg, pg, wgu, bg, nw)


def _gla_bwd_call(pg, s_hist, dyin, wgu, bg, nw):
    bsz, t, _ = pg.shape
    nc = t // CHUNK
    nh = bsz * H
    n_of, specs = _chunk_specs(nc, True, bsz, _GLA_COLS)

    def body(q_ref, k_ref, v_ref, og_ref, lr_ref, sh_ref, dy_ref, wgu_ref, bg_ref, nw_ref,
             dp_ref, dwgu_ref, dbg_ref, dnw_ref, ds_ref):
        @pl.when(pl.program_id(0) == 0)
        def _():
            dwgu_ref[...] = jnp.zeros_like(dwgu_ref)
            dbg_ref[...] = jnp.zeros_like(dbg_ref)
            dnw_ref[...] = jnp.zeros_like(dnw_ref)
            ds_ref[...] = jnp.zeros_like(ds_ref)

        ds = ds_ref[...]
        for sub in reversed(range(SUB)):
            rows = _chunk_rows(sub)
            q, k, v, lr, og, w, b_, nw_ = _gla_args((q_ref, k_ref, v_ref, og_ref, lr_ref, wgu_ref, bg_ref, nw_ref), bsz, rows)
            _, vjp = jax.vjp(gla_chunk, q, k, v, lr, og, sh_ref[sub], w, b_, nw_)
            dq, dk, dv, dlr, dog, ds, dwgu, dbg, dnw = vjp((_heads(dy_ref, bsz, rows, 128), ds))
            dnw_ref[...] += dnw
            for b in range(bsz):
                for h in range(H):
                    i = H * b + h
                    dp_ref[b, rows, 512 + 128 * h:512 + 128 * h + 128] = dv[i].astype(MXU)
                    dp_ref[b, rows, 1024 + 128 * h:1024 + 128 * h + 128] = dog[i].astype(MXU)
                    dwgu_ref[:, 64 * h:64 * h + 64] += dwgu[i]
                    dbg_ref[:, 64 * h:64 * h + 64] += dbg[i]
                for j in range(H // 2):
                    dp_ref[b, rows, 128 * j:128 * j + 128] = jnp.concatenate(
                        [dq[H * b + 2 * j], dq[H * b + 2 * j + 1]], axis=-1).astype(MXU)
                    dp_ref[b, rows, 256 + 128 * j:256 + 128 * j + 128] = jnp.concatenate(
                        [dk[H * b + 2 * j], dk[H * b + 2 * j + 1]], axis=-1).astype(MXU)
                dp_ref[b, rows, 1536:1664] = (dlr[H * b] + dlr[H * b + 1] + dlr[H * b + 2] + dlr[H * b + 3]).astype(MXU)
        ds_ref[...] = ds

    return pl.pallas_call(
        body, name="gla_bwd", grid=(nc // SUB,),
        in_specs=specs + [pl.BlockSpec((SUB, nh, GLA_DK, DV), lambda n: (n_of(n), 0, 0, 0)),
                          pl.BlockSpec((bsz, SUB * CHUNK, 512), lambda n: (0, n_of(n), 0)),
                          _full((128, 256)), _full((1, 256)), _full((1, 128))],
        out_specs=[pl.BlockSpec((bsz, SUB * CHUNK, W_GLA), lambda n: (0, n_of(n), 0)),
                   _full((128, 256)), _full((1, 256)), _full((1, 128))],
        out_shape=[jax.ShapeDtypeStruct(pg.shape, MXU), jax.ShapeDtypeStruct((128, 256), F32),
                   jax.ShapeDtypeStruct((1, 256), F32), jax.ShapeDtypeStruct((1, 128), F32)],
        scratch_shapes=[pltpu.VMEM((nh, GLA_DK, DV), F32)],
        compiler_params=_params(("arbitrary",)),
    )(pg, pg, pg, pg, pg, s_hist, dyin, wgu, bg, nw)


_GDN_COLS = [(512, 0), (512, 1), (512, 2), (512, 0), (128, 4)]


def _gdn_args(refs, bsz, rows):
    q_ref, k_ref, v_ref, og_ref, ab_ref, sc_ref, nw_ref = refs
    return (_heads(q_ref, bsz, rows, 128), _heads(k_ref, bsz, rows, 128), _heads(v_ref, bsz, rows, 128),
            _heads(ab_ref, bsz, rows, 1), _heads(ab_ref, bsz, rows, 1, off=H), _heads(og_ref, bsz, rows, 128),
            _per_head(sc_ref, bsz, 1, slice(0, 1)), _per_head(sc_ref, bsz, 1, slice(1, 2)), nw_ref[...])


def _gdn_fwd_call(conv, pd, sc, nw):
    bsz, t, _ = conv.shape
    nc = t // CHUNK
    nh = bsz * H
    _, specs = _chunk_specs(nc, False, bsz, _GDN_COLS)

    def body(q_ref, k_ref, v_ref, og_ref, ab_ref, sc_ref, nw_ref, y_ref, sh_ref, th_ref, s_ref):
        @pl.when(pl.program_id(0) == 0)
        def _():
            s_ref[...] = jnp.zeros_like(s_ref)

        s = s_ref[...]
        for sub in range(SUB):
            rows = _chunk_rows(sub)
            q, k, v, a, bb, og, alog, dtb, nw_ = _gdn_args((q_ref, k_ref, v_ref, og_ref, ab_ref, sc_ref, nw_ref), bsz, rows)
            sh_ref[sub] = s
            y, s, tinv = gdn_chunk(q, k, v, a, bb, og, s, alog, dtb, nw_)
            th_ref[sub] = tinv.astype(MXU)
            for b in range(bsz):
                for h in range(H):
                    y_ref[b, rows, 128 * h:128 * h + 128] = y[H * b + h].astype(MXU)
        s_ref[...] = s

    return pl.pallas_call(
        body, name="gdn_fwd", grid=(nc // SUB,),
        in_specs=specs + [_full((2, 128)), _full((1, 128))],
        out_specs=[pl.BlockSpec((bsz, SUB * CHUNK, 512), lambda n: (0, n, 0)),
                   pl.BlockSpec((SUB, nh, DV, DV), lambda n: (n, 0, 0, 0)),
                   pl.BlockSpec((SUB, nh, CHUNK, CHUNK), lambda n: (n, 0, 0, 0))],
        out_shape=[jax.ShapeDtypeStruct((bsz, t, 512), MXU), jax.ShapeDtypeStruct((nc, nh, DV, DV), F32),
                   jax.ShapeDtypeStruct((nc, nh, CHUNK, CHUNK), MXU)],
        scratch_shapes=[pltpu.VMEM((nh, DV, DV), F32)],
        compiler_params=_params(("arbitrary",)),
    )(conv, conv, conv, pd, pd, sc, nw)


def _gdn_bwd_call(conv, pd, s_hist, t_hist, dyin, sc, nw):
    bsz, t, _ = conv.shape
    nc = t // CHUNK
    nh = bsz * H
    n_of, specs = _chunk_specs(nc, True, bsz, _GDN_COLS)

    def body(q_ref, k_ref, v_ref, og_ref, ab_ref, sh_ref, th_ref, dy_ref, sc_ref, nw_ref,
             dc_ref, dpd_ref, dsc_ref, dnw_ref, ds_ref):
        @pl.when(pl.program_id(0) == 0)
        def _():
            dsc_ref[...] = jnp.zeros_like(dsc_ref)
            dnw_ref[...] = jnp.zeros_like(dnw_ref)
            ds_ref[...] = jnp.zeros_like(ds_ref)

        lane = lax.broadcasted_iota(jnp.int32, (CHUNK, 128), 1)
        ds = ds_ref[...]
        for sub in reversed(range(SUB)):
            rows = _chunk_rows(sub)
            q, k, v, a, bb, og, alog, dtb, nw_ = _gdn_args((q_ref, k_ref, v_ref, og_ref, ab_ref, sc_ref, nw_ref), bsz, rows)
            _, vjp = jax.vjp(functools.partial(gdn_chunk, tinv=th_ref[sub]), q, k, v, a, bb, og, sh_ref[sub], alog, dtb, nw_)
            dq, dk, dv, da, db, dog, ds, dalog, ddtb, dnw = vjp((_heads(dy_ref, bsz, rows, 128), ds))
            dnw_ref[...] += dnw
            for b in range(bsz):
                dab = jnp.zeros((CHUNK, 128), F32)
                for h in range(H):
                    i = H * b + h
                    dc_ref[b, rows, 128 * h:128 * h + 128] = dq[i]
                    dc_ref[b, rows, 512 + 128 * h:512 + 128 * h + 128] = dk[i]
                    dc_ref[b, rows, 1024 + 128 * h:1024 + 128 * h + 128] = dv[i]
                    dpd_ref[b, rows, 128 * h:128 * h + 128] = dog[i].astype(MXU)
                    dab = dab + jnp.where(lane == h, da[i], 0.0) + jnp.where(lane == H + h, db[i], 0.0)
                    dsc_ref[0:1, h:h + 1] += dalog[i]
                    dsc_ref[1:2, h:h + 1] += ddtb[i]
                dpd_ref[b, rows, 512:640] = dab.astype(MXU)
        ds_ref[...] = ds

    return pl.pallas_call(
        body, name="gdn_bwd", grid=(nc // SUB,),
        in_specs=specs + [pl.BlockSpec((SUB, nh, DV, DV), lambda n: (n_of(n), 0, 0, 0)),
                          pl.BlockSpec((SUB, nh, CHUNK, CHUNK), lambda n: (n_of(n), 0, 0, 0)),
                          pl.BlockSpec((bsz, SUB * CHUNK, 512), lambda n: (0, n_of(n), 1)),
                          _full((2, 128)), _full((1, 128))],
        out_specs=[pl.BlockSpec((bsz, SUB * CHUNK, W_GQKV), lambda n: (0, n_of(n), 0)),
                   pl.BlockSpec((bsz, SUB * CHUNK, W_GDN), lambda n: (0, n_of(n), 0)),
                   _full((2, 128)), _full((1, 128))],
        out_shape=[jax.ShapeDtypeStruct(conv.shape, F32), jax.ShapeDtypeStruct(pd.shape, MXU),
                   jax.ShapeDtypeStruct((2, 128), F32), jax.ShapeDtypeStruct((1, 128), F32)],
        scratch_shapes=[pltpu.VMEM((nh, DV, DV), F32)],
        compiler_params=_params(("arbitrary",)),
    )(conv, conv, conv, pd, pd, s_hist, t_hist, dyin, sc, nw)


def _head_call(x, ya, yb, wout, mod3, lnw, lnb, tgt):
    bsz, t, _ = x.shape
    tm = min(512, t)
    rows = min(256, tm)

    def body(x_ref, ya_ref, yb_ref, w_ref, mod_ref, lnw_ref, lnb_ref, t_ref,
             dyin_ref, dxa_ref, dgate_ref, dw_ref, dlnw_ref, dlnb_ref, loss_ref):
        b, i = pl.program_id(0), pl.program_id(1)

        @pl.when((b == 0) & (i == 0))
        def _():
            dw_ref[...] = jnp.zeros_like(dw_ref)
            dlnw_ref[...] = jnp.zeros_like(dlnw_ref)
            dlnb_ref[...] = jnp.zeros_like(dlnb_ref)
            loss_ref[...] = jnp.zeros_like(loss_ref)

        @pl.when(i == 0)
        def _():
            dgate_ref[...] = jnp.zeros_like(dgate_ref)

        w = w_ref[...]
        parts = [slice(p * rows, (p + 1) * rows) for p in range(tm // rows)]
        yin = [jnp.concatenate([ya_ref[0, rs, :], yb_ref[0, rs, :]], axis=-1).astype(MXU) for rs in parts]
        y = [jnp.dot(yi, w, preferred_element_type=F32) for yi in yin]
        for rs, yi, y_p in zip(parts, yin, y):
            loss, vjp = jax.vjp(head_fn, x_ref[0, rs, :], y_p, mod_ref[0, 2:3, :], lnw_ref[...], lnb_ref[...], t_ref[0, rs, :])
            dx, dy, dgate, dlnw, dlnb, _ = vjp(jnp.ones((1, 1), F32))
            dyb = dy.astype(MXU)
            dyin_ref[0, rs, :] = lax.dot_general(dyb, w, (((1,), (1,)), ((), ())), preferred_element_type=F32)
            dw_ref[...] += lax.dot_general(yi, dyb, (((0,), (0,)), ((), ())), preferred_element_type=F32)
            dxa_ref[0, rs, :] = dx
            dgate_ref[0] += dgate
            dlnw_ref[...] += dlnw
            dlnb_ref[...] += dlnb
            loss_ref[...] += jnp.broadcast_to(loss, (1, 128))

    tok = lambda w, j=0: pl.BlockSpec((1, tm, w), lambda b, i: (b, i, j))
    row = pl.BlockSpec((1, D), lambda b, i: (0, 0))
    return pl.pallas_call(
        body, name="head", grid=(bsz, t // tm),
        in_specs=[tok(D), tok(512), tok(512), pl.BlockSpec((D, D), lambda b, i: (0, 0)),
                  pl.BlockSpec((1, 3, D), lambda b, i: (b, 0, 0)), row, row, tok(D)],
        out_specs=[tok(D), tok(D), pl.BlockSpec((1, 1, D), lambda b, i: (b, 0, 0)),
                   pl.BlockSpec((D, D), lambda b, i: (0, 0)), row, row, pl.BlockSpec((1, 128), lambda b, i: (0, 0))],
        out_shape=[jax.ShapeDtypeStruct(x.shape, F32), jax.ShapeDtypeStruct(x.shape, F32),
                   jax.ShapeDtypeStruct((bsz, 1, D), F32), jax.ShapeDtypeStruct((D, D), F32),
                   jax.ShapeDtypeStruct((1, D), F32), jax.ShapeDtypeStruct((1, D), F32),
                   jax.ShapeDtypeStruct((1, 128), F32)],
        compiler_params=_params(("arbitrary", "arbitrary")),
    )(x, ya, yb, wout, mod3, lnw, lnb, tgt)


def _dh_call(dpg, dpq, dpd, wpt, x, mod3, dxa, smalls):
    bsz, t, _ = x.shape
    tm = min(512, t)
    assert bsz + 1 <= 8

    def body(dg_ref, dq_ref, dd_ref, w_ref, x_ref, mod_ref, dxa_ref, dgate_ref, dlnw_ref, dlnb_ref, dbg_ref, n1_ref, n2_ref,
             dsc_ref, loss_ref, gx_ref, sp_ref):
        b = pl.program_id(0)

        @pl.when((b == 0) & (pl.program_id(1) == 0))
        def _():
            sp_ref[...] = jnp.zeros_like(sp_ref)
            for e in range(bsz):
                sp_ref[e:e + 1, 2 * D:3 * D] = dgate_ref[e]
            off = 0
            for ref in (dlnw_ref, dlnb_ref, dbg_ref, n1_ref, n2_ref):
                sp_ref[bsz:bsz + 1, off:off + ref.shape[1]] = ref[...]
                off += ref.shape[1]
            sp_ref[bsz:bsz + 1, off:off + 128] = dsc_ref[0:1, :]
            sp_ref[bsz:bsz + 1, off + 128:off + 256] = dsc_ref[1:2, :]
            sp_ref[bsz:bsz + 1, SMALL_W:SMALL_W + 128] = loss_ref[...]

        mm = lambda a, lo, hi: jnp.dot(a.astype(MXU), w_ref[lo:hi, :], preferred_element_type=F32)
        dh = mm(dg_ref[0], 0, W_GLA) + mm(dq_ref[0], W_GLA, W_GLA + W_GQKV) + mm(dd_ref[0], W_GLA + W_GQKV, PW)
        gx_ref[0] = dh * (1.0 + mod_ref[0, 1:2, :]) + dxa_ref[0]
        dshift = jnp.sum(dh, axis=0, keepdims=True)
        dscale = jnp.sum(dh * x_ref[0], axis=0, keepdims=True)
        for e in range(bsz):
            @pl.when(b == e)
            def _():
                sp_ref[e:e + 1, 0:D] += dshift
                sp_ref[e:e + 1, D:2 * D] += dscale

    tok = lambda w: pl.BlockSpec((1, tm, w), lambda b, i: (b, i, 0))
    whole = lambda a: pl.BlockSpec(a.shape, lambda b, i: (0,) * a.ndim)
    return pl.pallas_call(
        body, name="dh", grid=(bsz, t // tm),
        in_specs=[tok(W_GLA), tok(W_GQKV), tok(W_GDN), pl.BlockSpec((PW, D), lambda b, i: (0, 0)), tok(D),
                  pl.BlockSpec((1, 3, D), lambda b, i: (b, 0, 0)), tok(D)] + [whole(a) for a in smalls],
        out_specs=[tok(D), pl.BlockSpec((8, SPW), lambda b, i: (0, 0))],
        out_shape=[jax.ShapeDtypeStruct(x.shape, F32), jax.ShapeDtypeStruct((8, SPW), F32)],
        compiler_params=_params(("arbitrary", "arbitrary")),
    )(dpg, dpq, dpd, wpt, x, mod3, dxa, *smalls)


def _dw_call(x, mod3, dpg, dpq, dpd):
    bsz, t, _ = x.shape
    tm = min(512, t)
    nsteps = bsz * (t // tm)

    def body(x_ref, mod_ref, dg_ref, dq_ref, dd_ref, dw_ref, acc):
        step = pl.program_id(0) * (t // tm) + pl.program_id(1)

        @pl.when(step == 0)
        def _():
            acc[...] = jnp.zeros_like(acc)

        h = (x_ref[0] * (1.0 + mod_ref[0, 1:2, :]) + mod_ref[0, 0:1, :]).astype(MXU)
        for ref, lo, hi in ((dg_ref, 0, W_GLA), (dq_ref, W_GLA, W_GLA + W_GQKV), (dd_ref, W_GLA + W_GQKV, PW)):
            acc[lo:hi, :] += lax.dot_general(ref[0].astype(MXU), h, (((0,), (0,)), ((), ())), preferred_element_type=F32)

        @pl.when(step == nsteps - 1)
        def _():
            dw_ref[...] = acc[...].astype(dw_ref.dtype)

    tok = lambda w: pl.BlockSpec((1, tm, w), lambda b, i: (b, i, 0))
    return pl.pallas_call(
        body, name="dw", grid=(bsz, t // tm),
        in_specs=[tok(D), pl.BlockSpec((1, 3, D), lambda b, i: (b, 0, 0)), tok(W_GLA), tok(W_GQKV), tok(W_GDN)],
        out_specs=pl.BlockSpec((PW, D), lambda b, i: (0, 0)),
        out_shape=jax.ShapeDtypeStruct((PW, D), WIRE),
        scratch_shapes=[pltpu.VMEM((PW, D), F32)],
        compiler_params=_params(("arbitrary", "arbitrary")),
    )(x, mod3, dpg, dpq, dpd)


def _adamw(w, g, m, v):
    m = ADAM_B1 * m + (1.0 - ADAM_B1) * g
    v = ADAM_B2 * v + (1.0 - ADAM_B2) * jnp.square(g)
    m_hat = m / (1.0 - ADAM_B1 ** ADAM_STEP)
    v_hat = v / (1.0 - ADAM_B2 ** ADAM_STEP)
    delta = -ADAM_LR * (m_hat / (jnp.sqrt(v_hat) + ADAM_EPS) + ADAM_WD * w)
    return delta, m, v


def _sum8(ref):
    g = ref[0].astype(F32)
    for j in range(1, NDEV):
        g = g + ref[j].astype(F32)
    return g


def _adam_sum_call(name, g8, w, m, v, cols):
    r, c = w.shape

    def body(g_ref, w_ref, m_ref, v_ref, go_ref, d_ref, mo_ref, vo_ref):
        g = _sum8(g_ref)
        go_ref[...] = g
        d_ref[...], mo_ref[...], vo_ref[...] = _adamw(w_ref[...], g, m_ref[...], v_ref[...])

    blk = pl.BlockSpec((r, cols), lambda i: (0, i))
    return pl.pallas_call(
        body, name=name, grid=(c // cols,),
        in_specs=[pl.BlockSpec((NDEV, r, cols), lambda i: (0, 0, i)), blk, blk, blk],
        out_specs=[blk] * 4, out_shape=[jax.ShapeDtypeStruct((r, c), F32)] * 4,
        compiler_params=_params(("parallel",)),
    )(g8, w, m, v)


def _adam_ada_call(c_all, dmod_cols, w, m, v):
    def body(c_ref, dm_ref, w_ref, m_ref, v_ref, go_ref, d_ref, mo_ref, vo_ref):
        g = lax.dot_general(c_ref[...].astype(MXU), dm_ref[...].astype(MXU), (((0,), (0,)), ((), ())),
                            preferred_element_type=F32)
        go_ref[...] = g
        d_ref[...], mo_ref[...], vo_ref[...] = _adamw(w_ref[...], g, m_ref[...], v_ref[...])

    return pl.pallas_call(
        body, name="adam_ada", out_shape=[jax.ShapeDtypeStruct(w.shape, F32)] * 4, compiler_params=_params(),
    )(c_all, dmod_cols, w, m, v)


_SMALL_AT = dict(ln_w=(0, 1024), ln_b=(1024, 1024), b_gate=(2048, 256), gla_nw=(2304, 128), gdn_nw=(2432, 128),
                 a_log=(2560, 4), dt_bias=(2688, 4))


def _adam_small_call(sp_all, bsz, params):
    names = list(params)

    def body(sp_ref, *refs):
        ins, outs = refs[:3 * len(names)], refs[3 * len(names):]
        dmod_ref, loss_ref, outs = outs[0], outs[1], outs[2:]
        packed = sp_ref[0, bsz:bsz + 1, :]
        for j in range(1, NDEV):
            packed = packed + sp_ref[j, bsz:bsz + 1, :]
        gb = None
        for j in range(NDEV):
            dmod_ref[bsz * j:bsz * j + bsz, :] = sp_ref[j, 0:bsz, :]
            for e in range(bsz):
                gb = sp_ref[j, e:e + 1, :] if gb is None else gb + sp_ref[j, e:e + 1, :]
        loss_ref[...] = packed[:, SMALL_W:SMALL_W + 128]
        for i, name in enumerate(names):
            if name == "b_ada":
                g = gb
            else:
                lo, n = _SMALL_AT[name]
                g = packed[:, lo:lo + n]
            w_ref, m_ref, v_ref = ins[3 * i:3 * i + 3]
            g_ref, d_ref, mo_ref, vo_ref = outs[4 * i:4 * i + 4]
            g_ref[...] = g
            d_ref[...], mo_ref[...], vo_ref[...] = _adamw(w_ref[...], g, m_ref[...], v_ref[...])

    flat = [a for name in names for a in params[name]]
    out_shape = [jax.ShapeDtypeStruct((NDEV * bsz, SPW), F32), jax.ShapeDtypeStruct((1, 128), F32)]
    out_shape += [jax.ShapeDtypeStruct(params[name][0].shape, F32) for name in names for _ in range(4)]
    res = pl.pallas_call(body, name="adam_small", out_shape=out_shape, compiler_params=_params())(sp_all, *flat)
    return res[0], res[1], {name: res[2 + 4 * i:6 + 4 * i] for i, name in enumerate(names)}


def _mesh_pos():
    x, y, c = lax.axis_index("x"), lax.axis_index("y"), lax.axis_index("c")
    return x, y, c, 4 * x + 2 * y + c


def _peer(x, y, c, k):
    px = 1 - x if k & 4 else x
    py = 1 - y if k & 2 else y
    pc = 1 - c if k & 1 else c
    return (px, py, pc), 4 * px + 2 * py + pc


_ANY = pl.BlockSpec(memory_space=pl.ANY)
_VMEM = pl.BlockSpec(memory_space=pltpu.VMEM)


def _gather_call(c8, w_ada, b_sh, w_in_t):
    C_SEM, W_SEM, MOD_SEM = 0, 1, 2

    def body(c_ref, wada_ref, b_ref, win_ref, wall_ref, call_ref, mod_ref, modp, send_sems, recv_sems, loc_sem):
        x, y, c, me = _mesh_pos()

        def remote(src, dst, a, k, to):
            return pltpu.make_async_remote_copy(src_ref=src, dst_ref=dst, send_sem=send_sems.at[a, k],
                                                recv_sem=recv_sems.at[a, k], device_id=_peer(x, y, c, to)[0],
                                                device_id_type=pl.DeviceIdType.MESH)

        idx = lambda k: _peer(x, y, c, k)[1]
        sends = []
        call_ref[me] = c_ref[...]
        for k in range(1, NDEV):
            sends.append(remote(c_ref, call_ref.at[me], C_SEM, k, k))
            sends[-1].start()
        local = pltpu.make_async_copy(win_ref, wall_ref.at[me], loc_sem)
        local.start()
        for k in (1, 2, 4, 6):
            sends.append(remote(win_ref, wall_ref.at[me], W_SEM, k, k))
            sends[-1].start()
        for k in range(1, NDEV):
            remote(c_ref, call_ref.at[idx(k)], C_SEM, k, k).wait_recv()
        modp[...] = jnp.dot(call_ref[...].reshape(NDEV * 8, D).astype(MXU), wada_ref[...].astype(MXU),
                            preferred_element_type=F32) + b_ref[...]
        mod_ref[me] = modp[pl.ds(pl.multiple_of(me * 8, 8), 8), :]
        for k in range(1, NDEV):
            sends.append(remote(modp.at[pl.ds(pl.multiple_of(idx(k) * 8, 8), 8), :], mod_ref.at[me], MOD_SEM, k, k))
            sends[-1].start()
        for k in (2, 4, 6):
            remote(win_ref, wall_ref.at[idx(k)], W_SEM, k, k).wait_recv()
            sends.append(remote(wall_ref.at[idx(k)], wall_ref.at[idx(k)], W_SEM, k + 1, 1))
            sends[-1].start()
        for k in (1, 3, 5, 7):
            remote(win_ref, wall_ref.at[idx(k)], W_SEM, k, 1).wait_recv()
        for k in range(1, NDEV):
            remote(modp.at[pl.ds(0, 8), :], mod_ref.at[idx(k)], MOD_SEM, k, k).wait_recv()
        for cp in sends:
            cp.wait_send()
        local.wait()

    return pl.pallas_call(
        body, name="gather",
        out_shape=[jax.ShapeDtypeStruct((NDEV,) + w_in_t.shape, w_in_t.dtype), jax.ShapeDtypeStruct((NDEV, 8, D), F32),
                   jax.ShapeDtypeStruct((NDEV, 8, SHARD_ADA), F32)],
        in_specs=[_VMEM, _VMEM, _VMEM, _ANY], out_specs=[_ANY, _VMEM, _VMEM],
        scratch_shapes=[pltpu.VMEM((NDEV * 8, SHARD_ADA), F32), pltpu.SemaphoreType.DMA((3, NDEV)),
                        pltpu.SemaphoreType.DMA((3, NDEV)), pltpu.SemaphoreType.DMA],
        compiler_params=_params(),
    )(c8, w_ada, b_sh, w_in_t)


def _reduce_call(blocks, sp):
    nb = len(blocks)

    def body(sp_ref, *rest):
        srcs, outs = rest[:nb], rest[nb:2 * nb]
        spall_ref, send_sems, recv_sems, loc_sems = rest[2 * nb:]
        x, y, c, me = _mesh_pos()

        def remote(src, dst, a, k, dev):
            return pltpu.make_async_remote_copy(src_ref=src, dst_ref=dst, send_sem=send_sems.at[a, k],
                                                recv_sem=recv_sems.at[a, k], device_id=dev,
                                                device_id_type=pl.DeviceIdType.MESH)

        sends = []
        spall_ref[me] = sp_ref[...]
        local = [pltpu.make_async_copy(srcs[a].at[me], outs[a].at[me], loc_sems.at[a]) for a in range(nb)]
        for cp in local:
            cp.start()
        for k in range(1, NDEV):
            dev, pidx = _peer(x, y, c, k)
            sends.append(remote(sp_ref, spall_ref.at[me], nb, k, dev))
            sends[-1].start()
            for a in range(nb):
                sends.append(remote(srcs[a].at[pidx], outs[a].at[me], a, k, dev))
                sends[-1].start()
        for k in range(1, NDEV):
            dev, pidx = _peer(x, y, c, k)
            remote(sp_ref, spall_ref.at[pidx], nb, k, dev).wait_recv()
            for a in range(nb):
                remote(srcs[a].at[pidx], outs[a].at[pidx], a, k, dev).wait_recv()
        for cp in sends:
            cp.wait_send()
        for cp in local:
            cp.wait()

    return pl.pallas_call(
        body, name="reduce", out_shape=[jax.ShapeDtypeStruct(a.shape, a.dtype) for a in blocks]
        + [jax.ShapeDtypeStruct((NDEV, 8, SPW), F32)],
        in_specs=[_VMEM] + [_ANY] * nb, out_specs=[_ANY] * nb + [_VMEM],
        scratch_shapes=[pltpu.SemaphoreType.DMA((nb + 1, NDEV)), pltpu.SemaphoreType.DMA((nb + 1, NDEV)),
                        pltpu.SemaphoreType.DMA((nb,))],
        compiler_params=_params(),
    )(sp, *blocks)


_HBM = pl.BlockSpec(memory_space=pltpu.HBM)
_SEM = pl.BlockSpec(memory_space=pltpu.SEMAPHORE)
_EFFECT = pltpu.SideEffectType.DATAFLOW_SIDE_EFFECTING


def _xchg_start(name, blocks, lands, gather):
    nb = len(blocks)

    def body(*refs):
        srcs, dsts = refs[:nb], refs[nb:2 * nb]
        send_sems, recv_sems = refs[2 * nb], refs[2 * nb + 1]
        token = refs[-1]
        x, y, c, me = _mesh_pos()
        for k in range(1, NDEV):
            dev, pidx = _peer(x, y, c, k)
            for a in range(nb):
                pltpu.make_async_remote_copy(src_ref=srcs[a] if gather else srcs[a].at[pidx], dst_ref=dsts[a].at[me],
                                             send_sem=send_sems.at[NDEV * a + k], recv_sem=recv_sems.at[NDEV * a + k],
                                             device_id=dev, device_id_type=pl.DeviceIdType.MESH).start()
        token[...] = jnp.zeros_like(token)

    thru = [pltpu.HBM(a.shape, a.dtype) for a in list(blocks) + list(lands)]
    return pl.pallas_call(
        body, name=name,
        out_shape=(pltpu.SemaphoreType.DMA((nb * NDEV,)), pltpu.SemaphoreType.DMA((nb * NDEV,)), *thru,
                   jax.ShapeDtypeStruct((8, 128), F32)),
        in_specs=[_HBM] * (2 * nb), out_specs=(_SEM, _SEM, *([_HBM] * (2 * nb)), _VMEM),
        input_output_aliases={i: 2 + i for i in range(2 * nb)},
        compiler_params=pltpu.CompilerParams(has_side_effects=_EFFECT),
    )(*[pltpu.with_memory_space_constraint(a, pltpu.HBM) for a in list(blocks) + list(lands)])


def _xchg_wait(name, send_sems, recv_sems, thru, after, gather):
    nb = len(thru) // 2

    def body(*refs):
        srcs, dsts = refs[:nb], refs[nb:2 * nb]
        send_sems, recv_sems = refs[2 * nb], refs[2 * nb + 1]
        x, y, c, me = _mesh_pos()
        for k in range(1, NDEV):
            dev, pidx = _peer(x, y, c, k)
            for a in range(nb):
                cp = pltpu.make_async_remote_copy(src_ref=srcs[a] if gather else srcs[a].at[pidx], dst_ref=dsts[a].at[pidx],
                                                  send_sem=send_sems.at[NDEV * a + k], recv_sem=recv_sems.at[NDEV * a + k],
                                                  device_id=dev, device_id_type=pl.DeviceIdType.MESH)
                cp.wait_send()
                cp.wait_recv()

    out = pl.pallas_call(
        body, name=name, out_shape=tuple(pltpu.HBM(a.shape, a.dtype) for a in thru),
        in_specs=[_HBM] * (2 * nb) + [_SEM, _SEM, pl.BlockSpec(memory_space=pl.ANY)], out_specs=tuple([_HBM] * (2 * nb)),
        input_output_aliases={i: i for i in range(2 * nb)},
        compiler_params=pltpu.CompilerParams(has_side_effects=_EFFECT),
    )(*thru, send_sems, recv_sems, after)
    return out[nb:]


def _pad_cols(a, n):
    return jnp.pad(a, ((0, 0), (0, n - a.shape[1])))


def _assemble_wt(wt_full):
    q, k, v, lr, og, gqkv, ab, dog = jnp.split(wt_full, [256, 512, 1024, 1040, 1552, 3088, 3096], axis=0)
    z = lambda n: jnp.zeros((n, wt_full.shape[1]), wt_full.dtype)
    return jnp.concatenate([q, k, v, og, lr, z(112), gqkv, dog, ab, z(120)], axis=0)


def _disassemble_dwt(dwt):
    g0, q0 = W_GLA, W_GLA + W_GQKV
    return jnp.concatenate([dwt[:1024], dwt[1536:1552], dwt[1024:1536], dwt[g0:q0], dwt[q0 + 512:q0 + 520],
                            dwt[q0:q0 + 512]], axis=0)


def local_grads(x, mod3, wp, late_weights, bg, gla_nw, sc, gdn_nw, lnw, lnb, tgt):
    pg, pq, pd = _proj_call(x, mod3, wp)
    wout, conv_w, wgu_p = late_weights(pq)
    conv = _conv_fwd_call(pq, conv_w)
    ya, s_gla = _gla_fwd_call(pg, wgu_p, bg, gla_nw)
    yb, s_gdn, t_gdn = _gdn_fwd_call(conv, pd, sc, gdn_nw)
    dyin, dxa, dgate, dwout, dlnw, dlnb, loss = _head_call(x, ya, yb, wout, mod3, lnw, lnb, tgt)
    dpg, dwgu, dbg, dnw_gla = _gla_bwd_call(pg, s_gla, dyin, wgu_p, bg, gla_nw)
    dconv, dpd, dsc, dnw_gdn = _gdn_bwd_call(conv, pd, s_gdn, t_gdn, dyin, sc, gdn_nw)
    dpq, dconv_w = _conv_bwd_call(dconv, pq, conv_w)
    dw_in = _disassemble_dwt(_dw_call(x, mod3, dpg, dpq, dpd))
    g = dict(dw_in=dw_in, dwout=dwout, dconv_w=dconv_w[:4], dwgu=dwgu[:16])
    smalls = (dgate, dlnw, dlnb, dbg, dnw_gla, dnw_gdn, dsc, loss)
    return g, lambda mod3_: _dh_call(dpg, dpq, dpd, wp, x, mod3_, dxa, smalls)


def local_step(x, mod3, wp, wout, conv_w, wgu_p, *args):
    g, finish = local_grads(x, mod3, wp, lambda _: (wout, conv_w, wgu_p), *args)
    g["gx"], sp = finish(mod3)
    bsz = x.shape[0]
    g["dmod"] = sp[:bsz].reshape(bsz, 3, D)
    g["loss"] = sp[bsz, SMALL_W]
    for name, (lo, n) in _SMALL_AT.items():
        g[name] = sp[bsz:bsz + 1, lo:lo + n]
    return g


def kernel(x, c, w_ada, b_ada, w_in, gla_w_gate_up, gla_b_gate, gla_norm_w, gdn_conv_w, gdn_a_log, gdn_dt_bias, gdn_norm_w, w_out, ln_w, ln_b, loss_target, m_w_ada, m_b_ada, m_w_in, m_gla_w_gate_up, m_gla_b_gate, m_gla_norm_w, m_gdn_conv_w, m_gdn_a_log, m_gdn_dt_bias, m_gdn_norm_w, m_w_out, m_ln_w, m_ln_b, v_w_ada, v_b_ada, v_w_in, v_gla_w_gate_up, v_gla_b_gate, v_gla_norm_w, v_gdn_conv_w, v_gdn_a_log, v_gdn_dt_bias, v_gdn_norm_w, v_w_out, v_ln_w, v_ln_b):
    me = 4 * lax.axis_index("x") + 2 * lax.axis_index("y") + lax.axis_index("c")
    bsz = x.shape[0]

    b_sh = lax.dynamic_slice(b_ada, (0, me * SHARD_ADA), (1, SHARD_ADA))
    c8 = jnp.pad(c, ((0, 8 - bsz), (0, 0)))
    w_in_t, m_in_t, v_in_t = (jnp.swapaxes(a[0], 0, 1) for a in (w_in, m_w_in, v_w_in))
    win_all, c_all, mod_blk = _gather_call(c8, w_ada[0], b_sh, w_in_t.astype(WIRE))
    wp = _assemble_wt(win_all.reshape(IN_COLS, D))
    mod = jnp.transpose(mod_blk[:, :bsz, :], (1, 0, 2)).reshape(bsz, 3 * D)
    mod3 = mod.reshape(bsz, 3, D)
    sc = jnp.concatenate([_pad_cols(gdn_a_log, 128), _pad_cols(gdn_dt_bias, 128)], axis=0)

    own = lambda a: lax.dynamic_update_slice(lax.empty((NDEV,) + a.shape, a.dtype), a[None], (me,) + (0,) * a.ndim)
    late = [w_out[0].astype(WIRE), gdn_conv_w[0], gla_w_gate_up[0] + 0.0 * mod_blk[0, 0, 0]]
    w_send, w_recv, *w_thru, w_token = _xchg_start("wgather_start", late, [own(a) for a in late], gather=True)

    def late_weights(pq):
        wout_all, conv_all, wgu_all = _xchg_wait("wgather_wait", w_send, w_recv, w_thru, pq, gather=True)
        return (wout_all.reshape(D, D), jnp.transpose(conv_all, (1, 0, 2)).reshape(4, W_GQKV),
                jnp.pad(jnp.transpose(wgu_all, (1, 0, 2)).reshape(16, 256), ((0, 112), (0, 0))))

    g, finish = local_grads(x, mod3 + w_token[0, 0], wp, late_weights, gla_b_gate, gla_norm_w, sc, gdn_norm_w, ln_w, ln_b,
                            loss_target)

    big = [g["dw_in"].reshape(NDEV, SHARD_IN, D), g["dwout"].reshape(NDEV, D // NDEV, D).astype(WIRE)]
    lands = [lax.dynamic_update_slice(lax.empty(a.shape, a.dtype), lax.dynamic_slice(a, (me, 0, 0), (1,) + a.shape[1:]),
                                      (me, 0, 0)) for a in big]
    send_sems, recv_sems, *thru, token = _xchg_start("xchg_start", big, lands, gather=False)
    gx, sp = finish(mod3 + token[0, 0])
    r_in, r_out = _xchg_wait("xchg_wait", send_sems, recv_sems, thru, gx, gather=False)
    blocks = [jnp.transpose(g["dconv_w"].reshape(4, NDEV, W_GQKV // NDEV), (1, 0, 2)),
              jnp.transpose(g["dwgu"].reshape(16, NDEV, 256 // NDEV), (1, 0, 2))]
    r_conv, r_gu, sp_all = _reduce_call(blocks, sp)

    t_in = [jnp.swapaxes(a, 0, 1) for a in _adam_sum_call("adam_in", r_in, w_in_t, m_in_t, v_in_t, 256)]
    t_out = _adam_sum_call("adam_out", r_out, w_out[0], m_w_out[0], v_w_out[0], D)
    t_conv = _adam_sum_call("adam_conv", r_conv, gdn_conv_w[0], m_gdn_conv_w[0], v_gdn_conv_w[0], W_GQKV // NDEV)
    t_gu = _adam_sum_call("adam_gu", r_gu, gla_w_gate_up[0], m_gla_w_gate_up[0], v_gla_w_gate_up[0], 256 // NDEV)
    dmod_all, loss, small = _adam_small_call(sp_all, bsz, dict(
        b_ada=(b_ada, m_b_ada, v_b_ada), ln_w=(ln_w, m_ln_w, v_ln_w), ln_b=(ln_b, m_ln_b, v_ln_b),
        b_gate=(gla_b_gate, m_gla_b_gate, v_gla_b_gate), gla_nw=(gla_norm_w, m_gla_norm_w, v_gla_norm_w),
        gdn_nw=(gdn_norm_w, m_gdn_norm_w, v_gdn_norm_w), a_log=(gdn_a_log, m_gdn_a_log, v_gdn_a_log),
        dt_bias=(gdn_dt_bias, m_gdn_dt_bias, v_gdn_dt_bias)))
    c16 = c_all[:, :bsz, :].reshape(NDEV * bsz, D)
    t_ada = _adam_ada_call(c16, lax.dynamic_slice(dmod_all, (0, me * SHARD_ADA), (NDEV * bsz, SHARD_ADA)),
                           w_ada[0], m_w_ada[0], v_w_ada[0])

    def group(i):
        s = lambda name: small[name][i]
        return [t_ada[i][None], s("b_ada"), t_in[i][None], t_gu[i][None], s("b_gate"), s("gla_nw"), t_conv[i][None],
                s("a_log"), s("dt_bias"), s("gdn_nw"), t_out[i][None], s("ln_w"), s("ln_b")]

    return (loss[0, 0], gx, *group(0), *group(1), *group(2), *group(3))
```

```python
import functools

import jax
import jax.numpy as jnp
from jax import lax
from jax.experimental import pallas as pl
from jax.experimental.pallas import tpu as pltpu

F32 = jnp.float32
MXU = jnp.bfloat16
WIRE = jnp.bfloat16
HI = lax.Precision.HIGH

D = 1024
NDEV = 8
H = 4
GLA_DK = 64
DV = 128
CHUNK = 64
SUB = 8
SUB_GDN_BWD = 4
LN_EPS = 1e-5
RMS_EPS = 1e-6
ALPHA = 2.0 ** 0.25
GATE_NORM = 16.0

W_GLA, W_GQKV, W_GDN = 1664, 1536, 640
PW = W_GLA + W_GQKV + W_GDN
IN_COLS = 3608
SHARD_IN = IN_COLS // NDEV
SHARD_ADA = 3 * D // NDEV
SPW = 3 * D
SMALL_W = 2816

ADAM_LR, ADAM_B1, ADAM_B2, ADAM_EPS, ADAM_WD, ADAM_STEP = 0.001, 0.9, 0.999, 1e-08, 0.01, 10

VMEM_LIMIT = 56 * 1024 * 1024


def _params(sem=None, **kw):
    if sem is not None:
        kw["dimension_semantics"] = sem
    return pltpu.CompilerParams(vmem_limit_bytes=VMEM_LIMIT, **kw)


_MM = (((2,), (1,)), ((0,), (0,)))
_NT = (((2,), (2,)), ((0,), (0,)))
_TN = (((1,), (1,)), ((0,), (0,)))


def _dg(a, b, dims):
    return lax.dot_general(a.astype(MXU), b.astype(MXU), dims, preferred_element_type=F32)


def _hdg(a, b, dims):
    return lax.dot_general(a, b, dims, precision=HI, preferred_element_type=F32)


@jax.custom_vjp
def bmm(a, b):
    return _dg(a, b, _MM)


bmm.defvjp(lambda a, b: (_dg(a, b, _MM), (a, b)), lambda r, g: (_dg(g, r[1], _NT), _dg(r[0], g, _TN)))


@jax.custom_vjp
def bnt(a, b):
    return _dg(a, b, _NT)


bnt.defvjp(lambda a, b: (_dg(a, b, _NT), (a, b)), lambda r, g: (_dg(g, r[1], _MM), _dg(g, r[0], _TN)))


@jax.custom_vjp
def btn(a, b):
    return _dg(a, b, _TN)


btn.defvjp(lambda a, b: (_dg(a, b, _TN), (a, b)), lambda r, g: (_dg(r[1], g, _NT), _dg(r[0], g, _MM)))


def unit_lower_inverse(a):
    n = a.shape[-1]
    r, c = _iotas(n)
    p = -a
    t = (r == c).astype(F32) + p
    for _ in range(5):
        p = _dg(p, p, _MM)
        t = t + _dg(t, p, _MM)
    return t


@jax.custom_vjp
def unit_lower_solve(a, t, r1, r2):
    return _dg(t, r1, _MM), _dg(t, r2, _MM)


def _solve_fwd(a, t, r1, r2):
    s1, s2 = _dg(t, r1, _MM), _dg(t, r2, _MM)
    return (s1, s2), (t, s1, s2)


def _solve_bwd(res, g):
    t, s1, s2 = res
    d1, d2 = _dg(t, g[0], _TN), _dg(t, g[1], _TN)
    return -(_dg(d1, s1, _NT) + _dg(d2, s2, _NT)), jnp.zeros_like(t), d1, d2


unit_lower_solve.defvjp(_solve_fwd, _solve_bwd)


def _iotas(n):
    return lax.broadcasted_iota(jnp.int32, (n, n), 0), lax.broadcasted_iota(jnp.int32, (n, n), 1)


def _col_to_row(col, eye):
    return jnp.sum(jnp.where(eye, col, 0.0), axis=1, keepdims=True)


def _row_to_col(row, eye):
    return jnp.sum(jnp.where(eye, row, 0.0), axis=2, keepdims=True)


def _pick_row(m, i):
    r = lax.broadcasted_iota(jnp.int32, m.shape, 1)
    return jnp.sum(jnp.where(r == i, m, 0.0), axis=1, keepdims=True)


def _rms_gate(o, nw, og):
    on = o * lax.rsqrt(jnp.mean(o * o, axis=-1, keepdims=True) + RMS_EPS) * nw
    return on * jax.nn.silu(og)


def gla_chunk(q, k, v, lr, og, s, wgu, bg, nw):
    n, c, _ = q.shape
    r, cc = _iotas(c)
    causal = r >= cc
    qs = q * (GLA_DK ** -0.5)
    z = bmm(lr, wgu) + bg
    g = jax.nn.log_sigmoid(z) / GATE_NORM
    b = _hdg(jnp.broadcast_to(causal.astype(F32), (n, c, c)), g, _MM)
    bref = _pick_row(b, c // 2 - 1)
    blast = _pick_row(b, c - 1)
    att = jnp.where(causal, bnt(qs * jnp.exp(b - bref), k * jnp.exp(bref - b)), 0.0)
    o = bmm(att, v) + bmm(qs * jnp.exp(b), s)
    rk, ck = _iotas(GLA_DK)
    s_new = _row_to_col(jnp.exp(blast), rk == ck) * s + btn(k * jnp.exp(blast - b), v)
    return _rms_gate(o, nw, og), s_new


def gdn_chunk(cq, ck, cv, a, bb, og, s, alog, dtb, nw, tinv=None):
    c = cq.shape[1]
    r, cc = _iotas(c)
    eye, causal, strict = r == cc, r >= cc, r > cc
    q, k, v = jax.nn.silu(cq), jax.nn.silu(ck), jax.nn.silu(cv)
    q = q * lax.rsqrt(jnp.sum(q * q, axis=-1, keepdims=True) + RMS_EPS) * (DV ** -0.5)
    k = k * lax.rsqrt(jnp.sum(k * k, axis=-1, keepdims=True) + RMS_EPS)
    g = -jnp.exp(alog) * jax.nn.softplus(a + dtb)
    beta = jax.nn.sigmoid(bb)
    d = jnp.sum(jnp.where(causal, _col_to_row(g, eye), 0.0), axis=2, keepdims=True)
    el = jnp.exp(jnp.where(causal, d - _col_to_row(d, eye), -jnp.inf))
    kb = k * beta
    amat = jnp.where(strict, bnt(kb, k) * el, 0.0)
    t = unit_lower_inverse(amat) if tinv is None else tinv
    u, w = unit_lower_solve(amat, t, v * beta, kb * jnp.exp(d))
    qk = jnp.where(causal, bnt(q, k) * el, 0.0)
    dlast = _pick_row(d, c - 1)
    v_new = u - bmm(w, s)
    o = bmm(q * jnp.exp(d), s) + bmm(qk, v_new)
    s_new = jnp.exp(dlast) * s + btn(k * jnp.exp(dlast - d), v_new)
    y = _rms_gate(o, nw, og)
    return (y, s_new, t) if tinv is None else (y, s_new)


def head_fn(x, y, gate, lnw, lnb, tgt):
    u = ALPHA * x + (1.0 + gate) * y
    mu = jnp.mean(u, axis=-1, keepdims=True)
    var = jnp.mean(jnp.square(u - mu), axis=-1, keepdims=True)
    out = (u - mu) * lax.rsqrt(var + LN_EPS) * lnw + lnb
    err = jnp.square(out - tgt)
    return 0.5 * jnp.sum(jnp.mean(err, axis=-1, keepdims=True), axis=0, keepdims=True)


def _proj_call(x, mod3, wpt):
    bsz, t, _ = x.shape
    tm = min(512, t)

    def body(x_ref, mod_ref, w_ref, pg_ref, pq_ref, pd_ref):
        h = (x_ref[0] * (1.0 + mod_ref[0, 1:2, :]) + mod_ref[0, 0:1, :]).astype(MXU)
        nt = lambda lo, hi: lax.dot_general(h, w_ref[lo:hi, :], (((1,), (1,)), ((), ())), preferred_element_type=F32)
        pg_ref[0] = nt(0, W_GLA)
        pq_ref[0] = nt(W_GLA, W_GLA + W_GQKV)
        pd_ref[0] = nt(W_GLA + W_GQKV, PW)

    tok = lambda w: pl.BlockSpec((1, tm, w), lambda b, i: (b, i, 0))
    return pl.pallas_call(
        body, name="proj", grid=(bsz, t // tm),
        in_specs=[tok(D), pl.BlockSpec((1, 3, D), lambda b, i: (b, 0, 0)), pl.BlockSpec((PW, D), lambda b, i: (0, 0))],
        out_specs=[tok(W_GLA), tok(W_GQKV), tok(W_GDN)],
        out_shape=[jax.ShapeDtypeStruct((bsz, t, w), F32) for w in (W_GLA, W_GQKV, W_GDN)],
        compiler_params=_params(("parallel", "parallel")),
    )(x, mod3, wpt)


_CONV_ROWS = 16


def _conv_fwd_call(pq, conv_w):
    bsz, t, _ = pq.shape
    tt = min(512, t)
    hb = tt // 8

    def body(x_ref, halo_ref, w_ref, o_ref, buf):
        i = pl.program_id(1)
        buf[0:8, :] = jnp.where(i > 0, halo_ref[0], 0.0)
        buf[8:, :] = x_ref[0]
        for j in range(W_GQKV // 128):
            ls = slice(128 * j, 128 * j + 128)
            wj = [w_ref[k:k + 1, ls] for k in range(4)]

            def rows(r, carry):
                base = pl.multiple_of(r * _CONV_ROWS, _CONV_ROWS)
                win = buf[pl.ds(base, _CONV_ROWS + 8), ls]
                acc = wj[0] * win[5:5 + _CONV_ROWS, :]
                for k in range(1, 4):
                    acc = acc + wj[k] * win[5 + k:5 + k + _CONV_ROWS, :]
                o_ref[0, pl.ds(base, _CONV_ROWS), ls] = acc
                return carry

            lax.fori_loop(0, tt // _CONV_ROWS, rows, 0, unroll=4)

    return pl.pallas_call(
        body, name="conv_fwd", grid=(bsz, t // tt),
        in_specs=[pl.BlockSpec((1, tt, W_GQKV), lambda b, i: (b, i, 0)),
                  pl.BlockSpec((1, 8, W_GQKV), lambda b, i: (b, jnp.maximum(i * hb - 1, 0), 0)),
                  pl.BlockSpec((4, W_GQKV), lambda b, i: (0, 0))],
        out_specs=pl.BlockSpec((1, tt, W_GQKV), lambda b, i: (b, i, 0)),
        out_shape=jax.ShapeDtypeStruct(pq.shape, F32),
        scratch_shapes=[pltpu.VMEM((tt + 8, W_GQKV), F32)],
        compiler_params=_params(("parallel", "parallel")),
    )(pq, pq, conv_w)


def _conv_bwd_call(dconv, pq, conv_w):
    bsz, t, _ = pq.shape
    tt = min(512, t)
    hb = tt // 8
    nt_ = t // tt

    def body(d_ref, dnext_ref, x_ref, w_ref, din_ref, dw_ref, dbuf):
        b, i = pl.program_id(0), pl.program_id(1)

        @pl.when((b == 0) & (i == 0))
        def _():
            dw_ref[...] = jnp.zeros_like(dw_ref)

        dbuf[0:tt, :] = d_ref[0]
        dbuf[tt:, :] = jnp.where(i < nt_ - 1, dnext_ref[0], 0.0)
        xin = x_ref[0]
        acc = None
        for k in range(4):
            dsh = dbuf[pl.ds(3 - k, tt), :]
            acc = w_ref[k:k + 1, :] * dsh if acc is None else acc + w_ref[k:k + 1, :] * dsh
            dw_ref[k:k + 1, :] += jnp.sum(xin * dsh, axis=0, keepdims=True)
        din_ref[0] = acc.astype(MXU)

    tile = pl.BlockSpec((1, tt, W_GQKV), lambda b, i: (b, i, 0))
    return pl.pallas_call(
        body, name="conv_bwd", grid=(bsz, nt_),
        in_specs=[tile, pl.BlockSpec((1, 8, W_GQKV), lambda b, i: (b, jnp.minimum((i + 1) * hb, t // 8 - 1), 0)),
                  tile, pl.BlockSpec((4, W_GQKV), lambda b, i: (0, 0))],
        out_specs=[tile, pl.BlockSpec((8, W_GQKV), lambda b, i: (0, 0))],
        out_shape=[jax.ShapeDtypeStruct(pq.shape, MXU), jax.ShapeDtypeStruct((8, W_GQKV), F32)],
        scratch_shapes=[pltpu.VMEM((tt + 8, W_GQKV), F32)],
        compiler_params=_params(("arbitrary", "arbitrary")),
    )(dconv, dconv, pq, conv_w)


def _chunk_specs(nc, rev, bsz, cols, sub=None):
    sub = SUB if sub is None else sub
    steps = nc // sub
    n_of = (lambda n: steps - 1 - n) if rev else (lambda n: n)
    return n_of, [pl.BlockSpec((bsz, sub * CHUNK, w), lambda n, j=j: (0, n_of(n), j)) for w, j in cols]


def _full(shape):
    return pl.BlockSpec(shape, lambda n: (0,) * len(shape))


def _heads(ref, bsz, rows, width, off=0):
    return jnp.stack([ref[b, rows, off + width * h:off + width * (h + 1)] for b in range(bsz) for h in range(H)])


def _chunk_rows(sub):
    return slice(CHUNK * sub, CHUNK * (sub + 1))


def _per_head(ref, bsz, width, rows=slice(None)):
    return jnp.stack([ref[rows, width * h:width * (h + 1)] for _ in range(bsz) for h in range(H)])


_GLA_COLS = [(256, 0), (256, 1), (512, 1), (512, 2), (128, 12)]


def _gla_args(refs, bsz, rows):
    q_ref, k_ref, v_ref, og_ref, lr_ref, wgu_ref, bg_ref, nw_ref = refs
    lr = jnp.stack([lr_ref[b, rows, :] for b in range(bsz) for _ in range(H)])
    return (_heads(q_ref, bsz, rows, 64), _heads(k_ref, bsz, rows, 64), _heads(v_ref, bsz, rows, 128), lr,
            _heads(og_ref, bsz, rows, 128), _per_head(wgu_ref, bsz, 64), _per_head(bg_ref, bsz, 64), nw_ref[...])


def _gla_fwd_call(pg, wgu, bg, nw):
    bsz, t, _ = pg.shape
    nc = t // CHUNK
    nh = bsz * H
    _, specs = _chunk_specs(nc, False, bsz, _GLA_COLS)

    def body(q_ref, k_ref, v_ref, og_ref, lr_ref, wgu_ref, bg_ref, nw_ref, y_ref, sh_ref, s_ref):
        @pl.when(pl.program_id(0) == 0)
        def _():
            s_ref[...] = jnp.zeros_like(s_ref)

        s = s_ref[...]
        for sub in range(SUB):
            rows = _chunk_rows(sub)
            q, k, v, lr, og, w, b_, nw_ = _gla_args((q_ref, k_ref, v_ref, og_ref, lr_ref, wgu_ref, bg_ref, nw_ref), bsz, rows)
            sh_ref[sub] = s
            y, s = gla_chunk(q, k, v, lr, og, s, w, b_, nw_)
            for b in range(bsz):
                for h in range(H):
                    y_ref[b, rows, 128 * h:128 * h + 128] = y[H * b + h].astype(MXU)
        s_ref[...] = s

    return pl.pallas_call(
        body, name="gla_fwd", grid=(nc // SUB,),
        in_specs=specs + [_full((128, 256)), _full((1, 256)), _full((1, 128))],
        out_specs=[pl.BlockSpec((bsz, SUB * CHUNK, 512), lambda n: (0, n, 0)),
                   pl.BlockSpec((SUB, nh, GLA_DK, DV), lambda n: (n, 0, 0, 0))],
        out_shape=[jax.ShapeDtypeStruct((bsz, t, 512), MXU), jax.ShapeDtypeStruct((nc, nh, GLA_DK, DV), F32)],
        scratch_shapes=[pltpu.VMEM((nh, GLA_DK, DV), F32)],
        compiler_params=_params(("arbitrary",)),
    )(pg, pg, pg, pg, pg, wgu, bg, nw)


def _gla_bwd_call(pg, s_hist, dyin, wgu, bg, nw):
    bsz, t, _ = pg.shape
    nc = t // CHUNK
    nh = bsz * H
    n_of, specs = _chunk_specs(nc, True, bsz, _GLA_COLS)

    def body(q_ref, k_ref, v_ref, og_ref, lr_ref, sh_ref, dy_ref, wgu_ref, bg_ref, nw_ref,
             dp_ref, dwgu_ref, dbg_ref, dnw_ref, ds_ref):
        @pl.when(pl.program_id(0) == 0)
        def _():
            dwgu_ref[...] = jnp.zeros_like(dwgu_ref)
            dbg_ref[...] = jnp.zeros_like(dbg_ref)
            dnw_ref[...] = jnp.zeros_like(dnw_ref)
            ds_ref[...] = jnp.zeros_like(ds_ref)

        ds = ds_ref[...]
        for sub in reversed(range(SUB)):
            rows = _chunk_rows(sub)
            q, k, v, lr, og, w, b_, nw_ = _gla_args((q_ref, k_ref, v_ref, og_ref, lr_ref, wgu_ref, bg_ref, nw_ref), bsz, rows)
            _, vjp = jax.vjp(gla_chunk, q, k, v, lr, og, sh_ref[sub], w, b_, nw_)
            dq, dk, dv, dlr, dog, ds, dwgu, dbg, dnw = vjp((_heads(dy_ref, bsz, rows, 128), ds))
            dnw_ref[...] += dnw
            for b in range(bsz):
                for h in range(H):
                    i = H * b + h
                    dp_ref[b, rows, 512 + 128 * h:512 + 128 * h + 128] = dv[i].astype(MXU)
                    dp_ref[b, rows, 1024 + 128 * h:1024 + 128 * h + 128] = dog[i].astype(MXU)
                    dwgu_ref[:, 64 * h:64 * h + 64] += dwgu[i]
                    dbg_ref[:, 64 * h:64 * h + 64] += dbg[i]
                for j in range(H // 2):
                    dp_ref[b, rows, 128 * j:128 * j + 128] = jnp.concatenate(
                        [dq[H * b + 2 * j], dq[H * b + 2 * j + 1]], axis=-1).astype(MXU)
                    dp_ref[b, rows, 256 + 128 * j:256 + 128 * j + 128] = jnp.concatenate(
                        [dk[H * b + 2 * j], dk[H * b + 2 * j + 1]], axis=-1).astype(MXU)
                dp_ref[b, rows, 1536:1664] = (dlr[H * b] + dlr[H * b + 1] + dlr[H * b + 2] + dlr[H * b + 3]).astype(MXU)
        ds_ref[...] = ds

    return pl.pallas_call(
        body, name="gla_bwd", grid=(nc // SUB,),
        in_specs=specs + [pl.BlockSpec((SUB, nh, GLA_DK, DV), lambda n: (n_of(n), 0, 0, 0)),
                          pl.BlockSpec((bsz, SUB * CHUNK, 512), lambda n: (0, n_of(n), 0)),
                          _full((128, 256)), _full((1, 256)), _full((1, 128))],
        out_specs=[pl.BlockSpec((bsz, SUB * CHUNK, W_GLA), lambda n: (0, n_of(n), 0)),
                   _full((128, 256)), _full((1, 256)), _full((1, 128))],
        out_shape=[jax.ShapeDtypeStruct(pg.shape, MXU), jax.ShapeDtypeStruct((128, 256), F32),
                   jax.ShapeDtypeStruct((1, 256), F32), jax.ShapeDtypeStruct((1, 128), F32)],
        scratch_shapes=[pltpu.VMEM((nh, GLA_DK, DV), F32)],
        compiler_params=_params(("arbitrary",)),
    )(pg, pg, pg, pg, pg, s_hist, dyin, wgu, bg, nw)


_GDN_COLS = [(512, 0), (512, 1), (512, 2), (512, 0), (128, 4)]


def _gdn_args(refs, bsz, rows):
    q_ref, k_ref, v_ref, og_ref, ab_ref, sc_ref, nw_ref = refs
    return (_heads(q_ref, bsz, rows, 128), _heads(k_ref, bsz, rows, 128), _heads(v_ref, bsz, rows, 128),
            _heads(ab_ref, bsz, rows, 1), _heads(ab_ref, bsz, rows, 1, off=H), _heads(og_ref, bsz, rows, 128),
            _per_head(sc_ref, bsz, 1, slice(0, 1)), _per_head(sc_ref, bsz, 1, slice(1, 2)), nw_ref[...])


def _gdn_fwd_call(conv, pd, sc, nw):
    bsz, t, _ = conv.shape
    nc = t // CHUNK
    nh = bsz * H
    _, specs = _chunk_specs(nc, False, bsz, _GDN_COLS)

    def body(q_ref, k_ref, v_ref, og_ref, ab_ref, sc_ref, nw_ref, y_ref, sh_ref, th_ref, s_ref):
        @pl.when(pl.program_id(0) == 0)
        def _():
            s_ref[...] = jnp.zeros_like(s_ref)

        s = s_ref[...]
        for sub in range(SUB):
            rows = _chunk_rows(sub)
            q, k, v, a, bb, og, alog, dtb, nw_ = _gdn_args((q_ref, k_ref, v_ref, og_ref, ab_ref, sc_ref, nw_ref), bsz, rows)
            sh_ref[sub] = s
            y, s, tinv = gdn_chunk(q, k, v, a, bb, og, s, alog, dtb, nw_)
            th_ref[sub] = tinv.astype(MXU)
            for b in range(bsz):
                for h in range(H):
                    y_ref[b, rows, 128 * h:128 * h + 128] = y[H * b + h].astype(MXU)
        s_ref[...] = s

    return pl.pallas_call(
        body, name="gdn_fwd", grid=(nc // SUB,),
        in_specs=specs + [_full((2, 128)), _full((1, 128))],
        out_specs=[pl.BlockSpec((bsz, SUB * CHUNK, 512), lambda n: (0, n, 0)),
                   pl.BlockSpec((SUB, nh, DV, DV), lambda n: (n, 0, 0, 0)),
                   pl.BlockSpec((SUB, nh, CHUNK, CHUNK), lambda n: (n, 0, 0, 0))],
        out_shape=[jax.ShapeDtypeStruct((bsz, t, 512), MXU), jax.ShapeDtypeStruct((nc, nh, DV, DV), F32),
                   jax.ShapeDtypeStruct((nc, nh, CHUNK, CHUNK), MXU)],
        scratch_shapes=[pltpu.VMEM((nh, DV, DV), F32)],
        compiler_params=_params(("arbitrary",)),
    )(conv, conv, conv, pd, pd, sc, nw)


def _gdn_bwd_call(conv, pd, s_hist, t_hist, dyin, sc, nw):
    bsz, t, _ = conv.shape
    nc = t // CHUNK
    nh = bsz * H
    n_of, specs = _chunk_specs(nc, True, bsz, _GDN_COLS, SUB_GDN_BWD)

    def body(q_ref, k_ref, v_ref, og_ref, ab_ref, sh_ref, th_ref, dy_ref, sc_ref, nw_ref,
             dc_ref, dpd_ref, dsc_ref, dnw_ref, ds_ref):
        @pl.when(pl.program_id(0) == 0)
        def _():
            dsc_ref[...] = jnp.zeros_like(dsc_ref)
            dnw_ref[...] = jnp.zeros_like(dnw_ref)
            ds_ref[...] = jnp.zeros_like(ds_ref)

        lane = lax.broadcasted_iota(jnp.int32, (CHUNK, 128), 1)
        ds = ds_ref[...]
        for sub in reversed(range(SUB_GDN_BWD)):
            rows = _chunk_rows(sub)
            q, k, v, a, bb, og, alog, dtb, nw_ = _gdn_args((q_ref, k_ref, v_ref, og_ref, ab_ref, sc_ref, nw_ref), bsz, rows)
            _, vjp = jax.vjp(functools.partial(gdn_chunk, tinv=th_ref[sub]), q, k, v, a, bb, og, sh_ref[sub], alog, dtb, nw_)
            dq, dk, dv, da, db, dog, ds, dalog, ddtb, dnw = vjp((_heads(dy_ref, bsz, rows, 128), ds))
            dnw_ref[...] += dnw
            for b in range(bsz):
                dab = jnp.zeros((CHUNK, 128), F32)
                for h in range(H):
                    i = H * b + h
                    dc_ref[b, rows, 128 * h:128 * h + 128] = dq[i]
                    dc_ref[b, rows, 512 + 128 * h:512 + 128 * h + 128] = dk[i]
                    dc_ref[b, rows, 1024 + 128 * h:1024 + 128 * h + 128] = dv[i]
                    dpd_ref[b, rows, 128 * h:128 * h + 128] = dog[i].astype(MXU)
                    dab = dab + jnp.where(lane == h, da[i], 0.0) + jnp.where(lane == H + h, db[i], 0.0)
                    dsc_ref[0:1, h:h + 1] += dalog[i]
                    dsc_ref[1:2, h:h + 1] += ddtb[i]
                dpd_ref[b, rows, 512:640] = dab.astype(MXU)
        ds_ref[...] = ds

    return pl.pallas_call(
        body, name="gdn_bwd", grid=(nc // SUB_GDN_BWD,),
        in_specs=specs + [pl.BlockSpec((SUB_GDN_BWD, nh, DV, DV), lambda n: (n_of(n), 0, 0, 0)),
                          pl.BlockSpec((SUB_GDN_BWD, nh, CHUNK, CHUNK), lambda n: (n_of(n), 0, 0, 0)),
                          pl.BlockSpec((bsz, SUB_GDN_BWD * CHUNK, 512), lambda n: (0, n_of(n), 1)),
                          _full((2, 128)), _full((1, 128))],
        out_specs=[pl.BlockSpec((bsz, SUB_GDN_BWD * CHUNK, W_GQKV), lambda n: (0, n_of(n), 0)),
                   pl.BlockSpec((bsz, SUB_GDN_BWD * CHUNK, W_GDN), lambda n: (0, n_of(n), 0)),
                   _full((2, 128)), _full((1, 128))],
        out_shape=[jax.ShapeDtypeStruct(conv.shape, F32), jax.ShapeDtypeStruct(pd.shape, MXU),
                   jax.ShapeDtypeStruct((2, 128), F32), jax.ShapeDtypeStruct((1, 128), F32)],
        scratch_shapes=[pltpu.VMEM((nh, DV, DV), F32)],
        compiler_params=_params(("arbitrary",)),
    )(conv, conv, conv, pd, pd, s_hist, t_hist, dyin, sc, nw)


def _head_call(x, ya, yb, wout, mod3, lnw, lnb, tgt):
    bsz, t, _ = x.shape
    tm = min(512, t)
    rows = min(256, tm)

    def body(x_ref, ya_ref, yb_ref, w_ref, mod_ref, lnw_ref, lnb_ref, t_ref,
             dyin_ref, dxa_ref, dgate_ref, dw_ref, dlnw_ref, dlnb_ref, loss_ref):
        b, i = pl.program_id(0), pl.program_id(1)

        @pl.when((b == 0) & (i == 0))
        def _():
            dw_ref[...] = jnp.zeros_like(dw_ref)
            dlnw_ref[...] = jnp.zeros_like(dlnw_ref)
            dlnb_ref[...] = jnp.zeros_like(dlnb_ref)
            loss_ref[...] = jnp.zeros_like(loss_ref)

        @pl.when(i == 0)
        def _():
            dgate_ref[...] = jnp.zeros_like(dgate_ref)

        w = w_ref[...]
        parts = [slice(p * rows, (p + 1) * rows) for p in range(tm // rows)]
        yin = [jnp.concatenate([ya_ref[0, rs, :], yb_ref[0, rs, :]], axis=-1).astype(MXU) for rs in parts]
        y = [jnp.dot(yi, w, preferred_element_type=F32) for yi in yin]
        for rs, yi, y_p in zip(parts, yin, y):
            loss, vjp = jax.vjp(head_fn, x_ref[0, rs, :], y_p, mod_ref[0, 2:3, :], lnw_ref[...], lnb_ref[...], t_ref[0, rs, :])
            dx, dy, dgate, dlnw, dlnb, _ = vjp(jnp.ones((1, 1), F32))
            dyb = dy.astype(MXU)
            dyin_ref[0, rs, :] = lax.dot_general(dyb, w, (((1,), (1,)), ((), ())), preferred_element_type=F32)
            dw_ref[...] += lax.dot_general(yi, dyb, (((0,), (0,)), ((), ())), preferred_element_type=F32)
            dxa_ref[0, rs, :] = dx
            dgate_ref[0] += dgate
            dlnw_ref[...] += dlnw
            dlnb_ref[...] += dlnb
            loss_ref[...] += jnp.broadcast_to(loss, (1, 128))

    tok = lambda w, j=0: pl.BlockSpec((1, tm, w), lambda b, i: (b, i, j))
    row = pl.BlockSpec((1, D), lambda b, i: (0, 0))
    return pl.pallas_call(
        body, name="head", grid=(bsz, t // tm),
        in_specs=[tok(D), tok(512), tok(512), pl.BlockSpec((D, D), lambda b, i: (0, 0)),
                  pl.BlockSpec((1, 3, D), lambda b, i: (b, 0, 0)), row, row, tok(D)],
        out_specs=[tok(D), tok(D), pl.BlockSpec((1, 1, D), lambda b, i: (b, 0, 0)),
                   pl.BlockSpec((D, D), lambda b, i: (0, 0)), row, row, pl.BlockSpec((1, 128), lambda b, i: (0, 0))],
        out_shape=[jax.ShapeDtypeStruct(x.shape, F32), jax.ShapeDtypeStruct(x.shape, F32),
                   jax.ShapeDtypeStruct((bsz, 1, D), F32), jax.ShapeDtypeStruct((D, D), F32),
                   jax.ShapeDtypeStruct((1, D), F32), jax.ShapeDtypeStruct((1, D), F32),
                   jax.ShapeDtypeStruct((1, 128), F32)],
        compiler_params=_params(("arbitrary", "arbitrary")),
    )(x, ya, yb, wout, mod3, lnw, lnb, tgt)


def _dh_call(dpg, dpq, dpd, wpt, x, mod3, dxa, smalls):
    bsz, t, _ = x.shape
    tm = min(512, t)
    assert bsz + 1 <= 8

    def body(dg_ref, dq_ref, dd_ref, w_ref, x_ref, mod_ref, dxa_ref, dgate_ref, dlnw_ref, dlnb_ref, dbg_ref, n1_ref, n2_ref,
             dsc_ref, loss_ref, gx_ref, sp_ref):
        b = pl.program_id(0)

        @pl.when((b == 0) & (pl.program_id(1) == 0))
        def _():
            sp_ref[...] = jnp.zeros_like(sp_ref)
            for e in range(bsz):
                sp_ref[e:e + 1, 2 * D:3 * D] = dgate_ref[e]
            off = 0
            for ref in (dlnw_ref, dlnb_ref, dbg_ref, n1_ref, n2_ref):
                sp_ref[bsz:bsz + 1, off:off + ref.shape[1]] = ref[...]
                off += ref.shape[1]
            sp_ref[bsz:bsz + 1, off:off + 128] = dsc_ref[0:1, :]
            sp_ref[bsz:bsz + 1, off + 128:off + 256] = dsc_ref[1:2, :]
            sp_ref[bsz:bsz + 1, SMALL_W:SMALL_W + 128] = loss_ref[...]

        mm = lambda a, lo, hi: jnp.dot(a.astype(MXU), w_ref[lo:hi, :], preferred_element_type=F32)
        dh = mm(dg_ref[0], 0, W_GLA) + mm(dq_ref[0], W_GLA, W_GLA + W_GQKV) + mm(dd_ref[0], W_GLA + W_GQKV, PW)
        gx_ref[0] = dh * (1.0 + mod_ref[0, 1:2, :]) + dxa_ref[0]
        dshift = jnp.sum(dh, axis=0, keepdims=True)
        dscale = jnp.sum(dh * x_ref[0], axis=0, keepdims=True)
        for e in range(bsz):
            @pl.when(b == e)
            def _():
                sp_ref[e:e + 1, 0:D] += dshift
                sp_ref[e:e + 1, D:2 * D] += dscale

    tok = lambda w: pl.BlockSpec((1, tm, w), lambda b, i: (b, i, 0))
    whole = lambda a: pl.BlockSpec(a.shape, lambda b, i: (0,) * a.ndim)
    return pl.pallas_call(
        body, name="dh", grid=(bsz, t // tm),
        in_specs=[tok(W_GLA), tok(W_GQKV), tok(W_GDN), pl.BlockSpec((PW, D), lambda b, i: (0, 0)), tok(D),
                  pl.BlockSpec((1, 3, D), lambda b, i: (b, 0, 0)), tok(D)] + [whole(a) for a in smalls],
        out_specs=[tok(D), pl.BlockSpec((8, SPW), lambda b, i: (0, 0))],
        out_shape=[jax.ShapeDtypeStruct(x.shape, F32), jax.ShapeDtypeStruct((8, SPW), F32)],
        compiler_params=_params(("arbitrary", "arbitrary")),
    )(dpg, dpq, dpd, wpt, x, mod3, dxa, *smalls)


def _dw_call(x, mod3, dpg, dpq, dpd):
    bsz, t, _ = x.shape
    tm = min(512, t)
    nsteps = bsz * (t // tm)

    def body(x_ref, mod_ref, dg_ref, dq_ref, dd_ref, dw_ref, acc):
        step = pl.program_id(0) * (t // tm) + pl.program_id(1)

        @pl.when(step == 0)
        def _():
            acc[...] = jnp.zeros_like(acc)

        h = (x_ref[0] * (1.0 + mod_ref[0, 1:2, :]) + mod_ref[0, 0:1, :]).astype(MXU)
        for ref, lo, hi in ((dg_ref, 0, W_GLA), (dq_ref, W_GLA, W_GLA + W_GQKV), (dd_ref, W_GLA + W_GQKV, PW)):
            acc[lo:hi, :] += lax.dot_general(ref[0].astype(MXU), h, (((0,), (0,)), ((), ())), preferred_element_type=F32)

        @pl.when(step == nsteps - 1)
        def _():
            dw_ref[...] = acc[...].astype(dw_ref.dtype)

    tok = lambda w: pl.BlockSpec((1, tm, w), lambda b, i: (b, i, 0))
    return pl.pallas_call(
        body, name="dw", grid=(bsz, t // tm),
        in_specs=[tok(D), pl.BlockSpec((1, 3, D), lambda b, i: (b, 0, 0)), tok(W_GLA), tok(W_GQKV), tok(W_GDN)],
        out_specs=pl.BlockSpec((PW, D), lambda b, i: (0, 0)),
        out_shape=jax.ShapeDtypeStruct((PW, D), WIRE),
        scratch_shapes=[pltpu.VMEM((PW, D), F32)],
        compiler_params=_params(("arbitrary", "arbitrary")),
    )(x, mod3, dpg, dpq, dpd)


def _adamw(w, g, m, v):
    m = ADAM_B1 * m + (1.0 - ADAM_B1) * g
    v = ADAM_B2 * v + (1.0 - ADAM_B2) * jnp.square(g)
    m_hat = m / (1.0 - ADAM_B1 ** ADAM_STEP)
    v_hat = v / (1.0 - ADAM_B2 ** ADAM_STEP)
    delta = -ADAM_LR * (m_hat / (jnp.sqrt(v_hat) + ADAM_EPS) + ADAM_WD * w)
    return delta, m, v


def _sum8(ref):
    g = ref[0].astype(F32)
    for j in range(1, NDEV):
        g = g + ref[j].astype(F32)
    return g


def _adam_sum_call(name, g8, w, m, v, cols):
    r, c = w.shape

    def body(g_ref, w_ref, m_ref, v_ref, go_ref, d_ref, mo_ref, vo_ref):
        g = _sum8(g_ref)
        go_ref[...] = g
        d_ref[...], mo_ref[...], vo_ref[...] = _adamw(w_ref[...], g, m_ref[...], v_ref[...])

    blk = pl.BlockSpec((r, cols), lambda i: (0, i))
    return pl.pallas_call(
        body, name=name, grid=(c // cols,),
        in_specs=[pl.BlockSpec((NDEV, r, cols), lambda i: (0, 0, i)), blk, blk, blk],
        out_specs=[blk] * 4, out_shape=[jax.ShapeDtypeStruct((r, c), F32)] * 4,
        compiler_params=_params(("parallel",)),
    )(g8, w, m, v)


def _adam_ada_call(c_all, dmod_cols, w, m, v):
    def body(c_ref, dm_ref, w_ref, m_ref, v_ref, go_ref, d_ref, mo_ref, vo_ref):
        g = lax.dot_general(c_ref[...].astype(MXU), dm_ref[...].astype(MXU), (((0,), (0,)), ((), ())),
                            preferred_element_type=F32)
        go_ref[...] = g
        d_ref[...], mo_ref[...], vo_ref[...] = _adamw(w_ref[...], g, m_ref[...], v_ref[...])

    return pl.pallas_call(
        body, name="adam_ada", out_shape=[jax.ShapeDtypeStruct(w.shape, F32)] * 4, compiler_params=_params(),
    )(c_all, dmod_cols, w, m, v)


_SMALL_AT = dict(ln_w=(0, 1024), ln_b=(1024, 1024), b_gate=(2048, 256), gla_nw=(2304, 128), gdn_nw=(2432, 128),
                 a_log=(2560, 4), dt_bias=(2688, 4))


def _adam_small_call(sp_all, bsz, params):
    names = list(params)

    def body(sp_ref, *refs):
        ins, outs = refs[:3 * len(names)], refs[3 * len(names):]
        dmod_ref, loss_ref, outs = outs[0], outs[1], outs[2:]
        packed = sp_ref[0, bsz:bsz + 1, :]
        for j in range(1, NDEV):
            packed = packed + sp_ref[j, bsz:bsz + 1, :]
        gb = None
        for j in range(NDEV):
            dmod_ref[bsz * j:bsz * j + bsz, :] = sp_ref[j, 0:bsz, :]
            for e in range(bsz):
                gb = sp_ref[j, e:e + 1, :] if gb is None else gb + sp_ref[j, e:e + 1, :]
        loss_ref[...] = packed[:, SMALL_W:SMALL_W + 128]
        for i, name in enumerate(names):
            if name == "b_ada":
                g = gb
            else:
                lo, n = _SMALL_AT[name]
                g = packed[:, lo:lo + n]
            w_ref, m_ref, v_ref = ins[3 * i:3 * i + 3]
            g_ref, d_ref, mo_ref, vo_ref = outs[4 * i:4 * i + 4]
            g_ref[...] = g
            d_ref[...], mo_ref[...], vo_ref[...] = _adamw(w_ref[...], g, m_ref[...], v_ref[...])

    flat = [a for name in names for a in params[name]]
    out_shape = [jax.ShapeDtypeStruct((NDEV * bsz, SPW), F32), jax.ShapeDtypeStruct((1, 128), F32)]
    out_shape += [jax.ShapeDtypeStruct(params[name][0].shape, F32) for name in names for _ in range(4)]
    res = pl.pallas_call(body, name="adam_small", out_shape=out_shape, compiler_params=_params())(sp_all, *flat)
    return res[0], res[1], {name: res[2 + 4 * i:6 + 4 * i] for i, name in enumerate(names)}


def _mesh_pos():
    x, y, c = lax.axis_index("x"), lax.axis_index("y"), lax.axis_index("c")
    return x, y, c, 4 * x + 2 * y + c


def _peer(x, y, c, k):
    px = 1 - x if k & 4 else x
    py = 1 - y if k & 2 else y
    pc = 1 - c if k & 1 else c
    return (px, py, pc), 4 * px + 2 * py + pc


_ANY = pl.BlockSpec(memory_space=pl.ANY)
_VMEM = pl.BlockSpec(memory_space=pltpu.VMEM)


def _gather_call(c8, w_ada, b_sh, w_in_t):
    C_SEM, W_SEM, MOD_SEM = 0, 1, 2

    def body(c_ref, wada_ref, b_ref, win_ref, wall_ref, call_ref, mod_ref, modp, send_sems, recv_sems, loc_sem):
        x, y, c, me = _mesh_pos()

        def remote(src, dst, a, k, to):
            return pltpu.make_async_remote_copy(src_ref=src, dst_ref=dst, send_sem=send_sems.at[a, k],
                                                recv_sem=recv_sems.at[a, k], device_id=_peer(x, y, c, to)[0],
                                                device_id_type=pl.DeviceIdType.MESH)

        idx = lambda k: _peer(x, y, c, k)[1]
        sends = []
        call_ref[me] = c_ref[...]
        for k in range(1, NDEV):
            sends.append(remote(c_ref, call_ref.at[me], C_SEM, k, k))
            sends[-1].start()
        local = pltpu.make_async_copy(win_ref, wall_ref.at[me], loc_sem)
        local.start()
        for k in (1, 2, 4, 6):
            sends.append(remote(win_ref, wall_ref.at[me], W_SEM, k, k))
            sends[-1].start()
        for k in range(1, NDEV):
            remote(c_ref, call_ref.at[idx(k)], C_SEM, k, k).wait_recv()
        modp[...] = jnp.dot(call_ref[...].reshape(NDEV * 8, D).astype(MXU), wada_ref[...].astype(MXU),
                            preferred_element_type=F32) + b_ref[...]
        mod_ref[me] = modp[pl.ds(pl.multiple_of(me * 8, 8), 8), :]
        for k in range(1, NDEV):
            sends.append(remote(modp.at[pl.ds(pl.multiple_of(idx(k) * 8, 8), 8), :], mod_ref.at[me], MOD_SEM, k, k))
            sends[-1].start()
        for k in (2, 4, 6):
            remote(win_ref, wall_ref.at[idx(k)], W_SEM, k, k).wait_recv()
            sends.append(remote(wall_ref.at[idx(k)], wall_ref.at[idx(k)], W_SEM, k + 1, 1))
            sends[-1].start()
        for k in (1, 3, 5, 7):
            remote(win_ref, wall_ref.at[idx(k)], W_SEM, k, 1).wait_recv()
        for k in range(1, NDEV):
            remote(modp.at[pl.ds(0, 8), :], mod_ref.at[idx(k)], MOD_SEM, k, k).wait_recv()
        for cp in sends:
            cp.wait_send()
        local.wait()

    return pl.pallas_call(
        body, name="gather",
        out_shape=[jax.ShapeDtypeStruct((NDEV,) + w_in_t.shape, w_in_t.dtype), jax.ShapeDtypeStruct((NDEV, 8, D), F32),
                   jax.ShapeDtypeStruct((NDEV, 8, SHARD_ADA), F32)],
        in_specs=[_VMEM, _VMEM, _VMEM, _ANY], out_specs=[_ANY, _VMEM, _VMEM],
        scratch_shapes=[pltpu.VMEM((NDEV * 8, SHARD_ADA), F32), pltpu.SemaphoreType.DMA((3, NDEV)),
                        pltpu.SemaphoreType.DMA((3, NDEV)), pltpu.SemaphoreType.DMA],
        compiler_params=_params(),
    )(c8, w_ada, b_sh, w_in_t)


def _reduce_call(blocks, sp):
    nb = len(blocks)

    def body(sp_ref, *rest):
        srcs, outs = rest[:nb], rest[nb:2 * nb]
        spall_ref, send_sems, recv_sems, loc_sems = rest[2 * nb:]
        x, y, c, me = _mesh_pos()

        def remote(src, dst, a, k, dev):
            return pltpu.make_async_remote_copy(src_ref=src, dst_ref=dst, send_sem=send_sems.at[a, k],
                                                recv_sem=recv_sems.at[a, k], device_id=dev,
                                                device_id_type=pl.DeviceIdType.MESH)

        sends = []
        spall_ref[me] = sp_ref[...]
        local = [pltpu.make_async_copy(srcs[a].at[me], outs[a].at[me], loc_sems.at[a]) for a in range(nb)]
        for cp in local:
            cp.start()
        for k in range(1, NDEV):
            dev, pidx = _peer(x, y, c, k)
            sends.append(remote(sp_ref, spall_ref.at[me], nb, k, dev))
            sends[-1].start()
            for a in range(nb):
                sends.append(remote(srcs[a].at[pidx], outs[a].at[me], a, k, dev))
                sends[-1].start()
        for k in range(1, NDEV):
            dev, pidx = _peer(x, y, c, k)
            remote(sp_ref, spall_ref.at[pidx], nb, k, dev).wait_recv()
            for a in range(nb):
                remote(srcs[a].at[pidx], outs[a].at[pidx], a, k, dev).wait_recv()
        for cp in sends:
            cp.wait_send()
        for cp in local:
            cp.wait()

    return pl.pallas_call(
        body, name="reduce", out_shape=[jax.ShapeDtypeStruct(a.shape, a.dtype) for a in blocks]
        + [jax.ShapeDtypeStruct((NDEV, 8, SPW), F32)],
        in_specs=[_VMEM] + [_ANY] * nb, out_specs=[_ANY] * nb + [_VMEM],
        scratch_shapes=[pltpu.SemaphoreType.DMA((nb + 1, NDEV)), pltpu.SemaphoreType.DMA((nb + 1, NDEV)),
                        pltpu.SemaphoreType.DMA((nb,))],
        compiler_params=_params(),
    )(sp, *blocks)


_HBM = pl.BlockSpec(memory_space=pltpu.HBM)
_SEM = pl.BlockSpec(memory_space=pltpu.SEMAPHORE)
_EFFECT = pltpu.SideEffectType.DATAFLOW_SIDE_EFFECTING


def _xchg_start(name, blocks, lands, gather):
    nb = len(blocks)

    def body(*refs):
        srcs, dsts = refs[:nb], refs[nb:2 * nb]
        send_sems, recv_sems = refs[2 * nb], refs[2 * nb + 1]
        token = refs[-1]
        x, y, c, me = _mesh_pos()
        for k in range(1, NDEV):
            dev, pidx = _peer(x, y, c, k)
            for a in range(nb):
                pltpu.make_async_remote_copy(src_ref=srcs[a] if gather else srcs[a].at[pidx], dst_ref=dsts[a].at[me],
                                             send_sem=send_sems.at[NDEV * a + k], recv_sem=recv_sems.at[NDEV * a + k],
                                             device_id=dev, device_id_type=pl.DeviceIdType.MESH).start()
        token[...] = jnp.zeros_like(token)

    thru = [pltpu.HBM(a.shape, a.dtype) for a in list(blocks) + list(lands)]
    return pl.pallas_call(
        body, name=name,
        out_shape=(pltpu.SemaphoreType.DMA((nb * NDEV,)), pltpu.SemaphoreType.DMA((nb * NDEV,)), *thru,
                   jax.ShapeDtypeStruct((8, 128), F32)),
        in_specs=[_HBM] * (2 * nb), out_specs=(_SEM, _SEM, *([_HBM] * (2 * nb)), _VMEM),
        input_output_aliases={i: 2 + i for i in range(2 * nb)},
        compiler_params=pltpu.CompilerParams(has_side_effects=_EFFECT),
    )(*[pltpu.with_memory_space_constraint(a, pltpu.HBM) for a in list(blocks) + list(lands)])


def _xchg_wait(name, send_sems, recv_sems, thru, after, gather):
    nb = len(thru) // 2

    def body(*refs):
        srcs, dsts = refs[:nb], refs[nb:2 * nb]
        send_sems, recv_sems = refs[2 * nb], refs[2 * nb + 1]
        x, y, c, me = _mesh_pos()
        for k in range(1, NDEV):
            dev, pidx = _peer(x, y, c, k)
            for a in range(nb):
                cp = pltpu.make_async_remote_copy(src_ref=srcs[a] if gather else srcs[a].at[pidx], dst_ref=dsts[a].at[pidx],
                                                  send_sem=send_sems.at[NDEV * a + k], recv_sem=recv_sems.at[NDEV * a + k],
                                                  device_id=dev, device_id_type=pl.DeviceIdType.MESH)
                cp.wait_send()
                cp.wait_recv()

    out = pl.pallas_call(
        body, name=name, out_shape=tuple(pltpu.HBM(a.shape, a.dtype) for a in thru),
        in_specs=[_HBM] * (2 * nb) + [_SEM, _SEM, pl.BlockSpec(memory_space=pl.ANY)], out_specs=tuple([_HBM] * (2 * nb)),
        input_output_aliases={i: i for i in range(2 * nb)},
        compiler_params=pltpu.CompilerParams(has_side_effects=_EFFECT),
    )(*thru, send_sems, recv_sems, after)
    return out[nb:]


def _pad_cols(a, n):
    return jnp.pad(a, ((0, 0), (0, n - a.shape[1])))


def _assemble_wt(wt_full):
    q, k, v, lr, og, gqkv, ab, dog = jnp.split(wt_full, [256, 512, 1024, 1040, 1552, 3088, 3096], axis=0)
    z = lambda n: jnp.zeros((n, wt_full.shape[1]), wt_full.dtype)
    return jnp.concatenate([q, k, v, og, lr, z(112), gqkv, dog, ab, z(120)], axis=0)


def _disassemble_dwt(dwt):
    g0, q0 = W_GLA, W_GLA + W_GQKV
    return jnp.concatenate([dwt[:1024], dwt[1536:1552], dwt[1024:1536], dwt[g0:q0], dwt[q0 + 512:q0 + 520],
                            dwt[q0:q0 + 512]], axis=0)


def local_grads(x, mod3, wp, late_weights, bg, gla_nw, sc, gdn_nw, lnw, lnb, tgt):
    pg, pq, pd = _proj_call(x, mod3, wp)
    wout, conv_w, wgu_p = late_weights(pq)
    conv = _conv_fwd_call(pq, conv_w)
    ya, s_gla = _gla_fwd_call(pg, wgu_p, bg, gla_nw)
    yb, s_gdn, t_gdn = _gdn_fwd_call(conv, pd, sc, gdn_nw)
    dyin, dxa, dgate, dwout, dlnw, dlnb, loss = _head_call(x, ya, yb, wout, mod3, lnw, lnb, tgt)
    dpg, dwgu, dbg, dnw_gla = _gla_bwd_call(pg, s_gla, dyin, wgu_p, bg, gla_nw)
    dconv, dpd, dsc, dnw_gdn = _gdn_bwd_call(conv, pd, s_gdn, t_gdn, dyin, sc, gdn_nw)
    dpq, dconv_w = _conv_bwd_call(dconv, pq, conv_w)
    dw_in = _disassemble_dwt(_dw_call(x, mod3, dpg, dpq, dpd))
    g = dict(dw_in=dw_in, dwout=dwout, dconv_w=dconv_w[:4], dwgu=dwgu[:16])
    smalls = (dgate, dlnw, dlnb, dbg, dnw_gla, dnw_gdn, dsc, loss)
    return g, lambda mod3_: _dh_call(dpg, dpq, dpd, wp, x, mod3_, dxa, smalls)


def local_step(x, mod3, wp, wout, conv_w, wgu_p, *args):
    g, finish = local_grads(x, mod3, wp, lambda _: (wout, conv_w, wgu_p), *args)
    g["gx"], sp = finish(mod3)
    bsz = x.shape[0]
    g["dmod"] = sp[:bsz].reshape(bsz, 3, D)
    g["loss"] = sp[bsz, SMALL_W]
    for name, (lo, n) in _SMALL_AT.items():
        g[name] = sp[bsz:bsz + 1, lo:lo + n]
    return g


def kernel(x, c, w_ada, b_ada, w_in, gla_w_gate_up, gla_b_gate, gla_norm_w, gdn_conv_w, gdn_a_log, gdn_dt_bias, gdn_norm_w, w_out, ln_w, ln_b, loss_target, m_w_ada, m_b_ada, m_w_in, m_gla_w_gate_up, m_gla_b_gate, m_gla_norm_w, m_gdn_conv_w, m_gdn_a_log, m_gdn_dt_bias, m_gdn_norm_w, m_w_out, m_ln_w, m_ln_b, v_w_ada, v_b_ada, v_w_in, v_gla_w_gate_up, v_gla_b_gate, v_gla_norm_w, v_gdn_conv_w, v_gdn_a_log, v_gdn_dt_bias, v_gdn_norm_w, v_w_out, v_ln_w, v_ln_b):
    me = 4 * lax.axis_index("x") + 2 * lax.axis_index("y") + lax.axis_index("c")
    bsz = x.shape[0]

    b_sh = lax.dynamic_slice(b_ada, (0, me * SHARD_ADA), (1, SHARD_ADA))
    c8 = jnp.pad(c, ((0, 8 - bsz), (0, 0)))
    w_in_t, m_in_t, v_in_t = (jnp.swapaxes(a[0], 0, 1) for a in (w_in, m_w_in, v_w_in))
    win_all, c_all, mod_blk = _gather_call(c8, w_ada[0], b_sh, w_in_t.astype(WIRE))
    wp = _assemble_wt(win_all.reshape(IN_COLS, D))
    mod = jnp.transpose(mod_blk[:, :bsz, :], (1, 0, 2)).reshape(bsz, 3 * D)
    mod3 = mod.reshape(bsz, 3, D)
    sc = jnp.concatenate([_pad_cols(gdn_a_log, 128), _pad_cols(gdn_dt_bias, 128)], axis=0)

    own = lambda a: lax.dynamic_update_slice(lax.empty((NDEV,) + a.shape, a.dtype), a[None], (me,) + (0,) * a.ndim)
    late = [w_out[0].astype(WIRE), gdn_conv_w[0], gla_w_gate_up[0] + 0.0 * mod_blk[0, 0, 0]]
    w_send, w_recv, *w_thru, w_token = _xchg_start("wgather_start", late, [own(a) for a in late], gather=True)

    def late_weights(pq):
        wout_all, conv_all, wgu_all = _xchg_wait("wgather_wait", w_send, w_recv, w_thru, pq, gather=True)
        return (wout_all.reshape(D, D), jnp.transpose(conv_all, (1, 0, 2)).reshape(4, W_GQKV),
                jnp.pad(jnp.transpose(wgu_all, (1, 0, 2)).reshape(16, 256), ((0, 112), (0, 0))))

    g, finish = local_grads(x, mod3 + w_token[0, 0], wp, late_weights, gla_b_gate, gla_norm_w, sc, gdn_norm_w, ln_w, ln_b,
                            loss_target)

    big = [g["dw_in"].reshape(NDEV, SHARD_IN, D), g["dwout"].reshape(NDEV, D // NDEV, D).astype(WIRE)]
    lands = [lax.dynamic_update_slice(lax.empty(a.shape, a.dtype), lax.dynamic_slice(a, (me, 0, 0), (1,) + a.shape[1:]),
                                      (me, 0, 0)) for a in big]
    send_sems, recv_sems, *thru, token = _xchg_start("xchg_start", big, lands, gather=False)
    gx, sp = finish(mod3 + token[0, 0])
    r_in, r_out = _xchg_wait("xchg_wait", send_sems, recv_sems, thru, gx, gather=False)
    blocks = [jnp.transpose(g["dconv_w"].reshape(4, NDEV, W_GQKV // NDEV), (1, 0, 2)),
              jnp.transpose(g["dwgu"].reshape(16, NDEV, 256 // NDEV), (1, 0, 2))]
    r_conv, r_gu, sp_all = _reduce_call(blocks, sp)

    t_in = [jnp.swapaxes(a, 0, 1) for a in _adam_sum_call("adam_in", r_in, w_in_t, m_in_t, v_in_t, 256)]
    t_out = _adam_sum_call("adam_out", r_out, w_out[0], m_w_out[0], v_w_out[0], D)
    t_conv = _adam_sum_call("adam_conv", r_conv, gdn_conv_w[0], m_gdn_conv_w[0], v_gdn_conv_w[0], W_GQKV // NDEV)
    t_gu = _adam_sum_call("adam_gu", r_gu, gla_w_gate_up[0], m_gla_w_gate_up[0], v_gla_w_gate_up[0], 256 // NDEV)
    dmod_all, loss, small = _adam_small_call(sp_all, bsz, dict(
        b_ada=(b_ada, m_b_ada, v_b_ada), ln_w=(ln_w, m_ln_w, v_ln_w), ln_b=(ln_b, m_ln_b, v_ln_b),
        b_gate=(gla_b_gate, m_gla_b_gate, v_gla_b_gate), gla_nw=(gla_norm_w, m_gla_norm_w, v_gla_norm_w),
        gdn_nw=(gdn_norm_w, m_gdn_norm_w, v_gdn_norm_w), a_log=(gdn_a_log, m_gdn_a_log, v_gdn_a_log),
        dt_bias=(gdn_dt_bias, m_gdn_dt_bias, v_gdn_dt_bias)))
    c16 = c_all[:, :bsz, :].reshape(NDEV * bsz, D)
    t_ada = _adam_ada_call(c16, lax.dynamic_slice(dmod_all, (0, me * SHARD_ADA), (NDEV * bsz, SHARD_ADA)),
                           w_ada[0], m_w_ada[0], v_w_ada[0])

    def group(i):
        s = lambda name: small[name][i]
        return [t_ada[i][None], s("b_ada"), t_in[i][None], t_gu[i][None], s("b_gate"), s("gla_nw"), t_conv[i][None],
                s("a_log"), s("dt_bias"), s("gdn_nw"), t_out[i][None], s("ln_w"), s("ln_b")]

    return (loss[0, 0], gx, *group(0), *group(1), *group(2), *group(3))
```

```python
import functools

import jax
import jax.numpy as jnp
from jax import lax
from jax.experimental import pallas as pl
from jax.experimental.pallas import tpu as pltpu

F32 = jnp.float32
MXU = jnp.bfloat16
WIRE = jnp.bfloat16
HI = lax.Precision.HIGH

D = 1024
NDEV = 8
H = 4
GLA_DK = 64
DV = 128
CHUNK = 64
SUB = 8
SUB_GDN_BWD = 4
LN_EPS = 1e-5
RMS_EPS = 1e-6
ALPHA = 2.0 ** 0.25
GATE_NORM = 16.0

W_GLA, W_GQKV, W_GDN = 1664, 1536, 640
PW = W_GLA + W_GQKV + W_GDN
IN_COLS = 3608
SHARD_IN = IN_COLS // NDEV
SHARD_ADA = 3 * D // NDEV
SPW = 3 * D
SMALL_W = 2816

ADAM_LR, ADAM_B1, ADAM_B2, ADAM_EPS, ADAM_WD, ADAM_STEP = 0.001, 0.9, 0.999, 1e-08, 0.01, 10

VMEM_LIMIT = 56 * 1024 * 1024


def _params(sem=None, **kw):
    if sem is not None:
        kw["dimension_semantics"] = sem
    return pltpu.CompilerParams(vmem_limit_bytes=VMEM_LIMIT, **kw)


_MM = (((2,), (1,)), ((0,), (0,)))
_NT = (((2,), (2,)), ((0,), (0,)))
_TN = (((1,), (1,)), ((0,), (0,)))


def _dg(a, b, dims):
    return lax.dot_general(a.astype(MXU), b.astype(MXU), dims, preferred_element_type=F32)


def _hdg(a, b, dims):
    return lax.dot_general(a, b, dims, precision=HI, preferred_element_type=F32)


@jax.custom_vjp
def bmm(a, b):
    return _dg(a, b, _MM)


bmm.defvjp(lambda a, b: (_dg(a, b, _MM), (a, b)), lambda r, g: (_dg(g, r[1], _NT), _dg(r[0], g, _TN)))


@jax.custom_vjp
def bnt(a, b):
    return _dg(a, b, _NT)


bnt.defvjp(lambda a, b: (_dg(a, b, _NT), (a, b)), lambda r, g: (_dg(g, r[1], _MM), _dg(g, r[0], _TN)))


@jax.custom_vjp
def btn(a, b):
    return _dg(a, b, _TN)


btn.defvjp(lambda a, b: (_dg(a, b, _TN), (a, b)), lambda r, g: (_dg(r[1], g, _NT), _dg(r[0], g, _MM)))


def unit_lower_inverse(a):
    n = a.shape[-1]
    r, c = _iotas(n)
    p = -a
    t = (r == c).astype(F32) + p
    for _ in range(5):
        p = _dg(p, p, _MM)
        t = t + _dg(t, p, _MM)
    return t


@jax.custom_vjp
def unit_lower_solve(a, t, r1, r2):
    return _dg(t, r1, _MM), _dg(t, r2, _MM)


def _solve_fwd(a, t, r1, r2):
    s1, s2 = _dg(t, r1, _MM), _dg(t, r2, _MM)
    return (s1, s2), (t, s1, s2)


def _solve_bwd(res, g):
    t, s1, s2 = res
    d1, d2 = _dg(t, g[0], _TN), _dg(t, g[1], _TN)
    return -(_dg(d1, s1, _NT) + _dg(d2, s2, _NT)), jnp.zeros_like(t), d1, d2


unit_lower_solve.defvjp(_solve_fwd, _solve_bwd)


def _iotas(n):
    return lax.broadcasted_iota(jnp.int32, (n, n), 0), lax.broadcasted_iota(jnp.int32, (n, n), 1)


def _col_to_row(col, eye):
    return jnp.sum(jnp.where(eye, col, 0.0), axis=1, keepdims=True)


def _row_to_col(row, eye):
    return jnp.sum(jnp.where(eye, row, 0.0), axis=2, keepdims=True)


def _pick_row(m, i):
    r = lax.broadcasted_iota(jnp.int32, m.shape, 1)
    return jnp.sum(jnp.where(r == i, m, 0.0), axis=1, keepdims=True)


def _rms_gate(o, nw, og):
    on = o * lax.rsqrt(jnp.mean(o * o, axis=-1, keepdims=True) + RMS_EPS) * nw
    return on * jax.nn.silu(og)


def gla_chunk(q, k, v, lr, og, s, wgu, bg, nw):
    n, c, _ = q.shape
    r, cc = _iotas(c)
    causal = r >= cc
    qs = q * (GLA_DK ** -0.5)
    z = bmm(lr, wgu) + bg
    g = jax.nn.log_sigmoid(z) / GATE_NORM
    b = _hdg(jnp.broadcast_to(causal.astype(F32), (n, c, c)), g, _MM)
    bref = _pick_row(b, c // 2 - 1)
    blast = _pick_row(b, c - 1)
    att = jnp.where(causal, bnt(qs * jnp.exp(b - bref), k * jnp.exp(bref - b)), 0.0)
    o = bmm(att, v) + bmm(qs * jnp.exp(b), s)
    rk, ck = _iotas(GLA_DK)
    s_new = _row_to_col(jnp.exp(blast), rk == ck) * s + btn(k * jnp.exp(blast - b), v)
    return _rms_gate(o, nw, og), s_new


def gdn_chunk(cq, ck, cv, a, bb, og, s, alog, dtb, nw, tinv=None):
    c = cq.shape[1]
    r, cc = _iotas(c)
    eye, causal, strict = r == cc, r >= cc, r > cc
    q, k, v = jax.nn.silu(cq), jax.nn.silu(ck), jax.nn.silu(cv)
    q = q * lax.rsqrt(jnp.sum(q * q, axis=-1, keepdims=True) + RMS_EPS) * (DV ** -0.5)
    k = k * lax.rsqrt(jnp.sum(k * k, axis=-1, keepdims=True) + RMS_EPS)
    g = -jnp.exp(alog) * jax.nn.softplus(a + dtb)
    beta = jax.nn.sigmoid(bb)
    d = jnp.sum(jnp.where(causal, _col_to_row(g, eye), 0.0), axis=2, keepdims=True)
    el = jnp.exp(jnp.where(causal, d - _col_to_row(d, eye), -jnp.inf))
    kb = k * beta
    amat = jnp.where(strict, bnt(kb, k) * el, 0.0)
    t = unit_lower_inverse(amat) if tinv is None else tinv
    u, w = unit_lower_solve(amat, t, v * beta, kb * jnp.exp(d))
    qk = jnp.where(causal, bnt(q, k) * el, 0.0)
    dlast = _pick_row(d, c - 1)
    v_new = u - bmm(w, s)
    o = bmm(q * jnp.exp(d), s) + bmm(qk, v_new)
    s_new = jnp.exp(dlast) * s + btn(k * jnp.exp(dlast - d), v_new)
    y = _rms_gate(o, nw, og)
    return (y, s_new, t) if tinv is None else (y, s_new)


def head_fn(x, y, gate, lnw, lnb, tgt):
    u = ALPHA * x + (1.0 + gate) * y
    mu = jnp.mean(u, axis=-1, keepdims=True)
    var = jnp.mean(jnp.square(u - mu), axis=-1, keepdims=True)
    out = (u - mu) * lax.rsqrt(var + LN_EPS) * lnw + lnb
    err = jnp.square(out - tgt)
    return 0.5 * jnp.sum(jnp.mean(err, axis=-1, keepdims=True), axis=0, keepdims=True)


def _proj_call(x, mod3, wpt):
    bsz, t, _ = x.shape
    tm = min(512, t)

    def body(x_ref, mod_ref, w_ref, pg_ref, pq_ref, pd_ref):
        h = (x_ref[0] * (1.0 + mod_ref[0, 1:2, :]) + mod_ref[0, 0:1, :]).astype(MXU)
        nt = lambda lo, hi: lax.dot_general(h, w_ref[lo:hi, :], (((1,), (1,)), ((), ())), preferred_element_type=F32)
        pg_ref[0] = nt(0, W_GLA)
        pq_ref[0] = nt(W_GLA, W_GLA + W_GQKV)
        pd_ref[0] = nt(W_GLA + W_GQKV, PW)

    tok = lambda w: pl.BlockSpec((1, tm, w), lambda b, i: (b, i, 0))
    return pl.pallas_call(
        body, name="proj", grid=(bsz, t // tm),
        in_specs=[tok(D), pl.BlockSpec((1, 3, D), lambda b, i: (b, 0, 0)), pl.BlockSpec((PW, D), lambda b, i: (0, 0))],
        out_specs=[tok(W_GLA), tok(W_GQKV), tok(W_GDN)],
        out_shape=[jax.ShapeDtypeStruct((bsz, t, w), F32) for w in (W_GLA, W_GQKV, W_GDN)],
        compiler_params=_params(("parallel", "parallel")),
    )(x, mod3, wpt)


_CONV_ROWS = 16


def _conv_fwd_call(pq, conv_w):
    bsz, t, _ = pq.shape
    tt = min(512, t)
    hb = tt // 8

    def body(x_ref, halo_ref, w_ref, o_ref, buf):
        i = pl.program_id(1)
        buf[0:8, :] = jnp.where(i > 0, halo_ref[0], 0.0)
        buf[8:, :] = x_ref[0]
        for j in range(W_GQKV // 128):
            ls = slice(128 * j, 128 * j + 128)
            wj = [w_ref[k:k + 1, ls] for k in range(4)]

            def rows(r, carry):
                base = pl.multiple_of(r * _CONV_ROWS, _CONV_ROWS)
                win = buf[pl.ds(base, _CONV_ROWS + 8), ls]
                acc = wj[0] * win[5:5 + _CONV_ROWS, :]
                for k in range(1, 4):
                    acc = acc + wj[k] * win[5 + k:5 + k + _CONV_ROWS, :]
                o_ref[0, pl.ds(base, _CONV_ROWS), ls] = acc
                return carry

            lax.fori_loop(0, tt // _CONV_ROWS, rows, 0, unroll=4)

    return pl.pallas_call(
        body, name="conv_fwd", grid=(bsz, t // tt),
        in_specs=[pl.BlockSpec((1, tt, W_GQKV), lambda b, i: (b, i, 0)),
                  pl.BlockSpec((1, 8, W_GQKV), lambda b, i: (b, jnp.maximum(i * hb - 1, 0), 0)),
                  pl.BlockSpec((4, W_GQKV), lambda b, i: (0, 0))],
        out_specs=pl.BlockSpec((1, tt, W_GQKV), lambda b, i: (b, i, 0)),
        out_shape=jax.ShapeDtypeStruct(pq.shape, F32),
        scratch_shapes=[pltpu.VMEM((tt + 8, W_GQKV), F32)],
        compiler_params=_params(("parallel", "parallel")),
    )(pq, pq, conv_w)


def _conv_bwd_call(dconv, pq, conv_w):
    bsz, t, _ = pq.shape
    tt = min(512, t)
    hb = tt // 8
    nt_ = t // tt

    def body(d_ref, dnext_ref, x_ref, w_ref, din_ref, dw_ref, dbuf):
        b, i = pl.program_id(0), pl.program_id(1)

        @pl.when((b == 0) & (i == 0))
        def _():
            dw_ref[...] = jnp.zeros_like(dw_ref)

        dbuf[0:tt, :] = d_ref[0]
        dbuf[tt:, :] = jnp.where(i < nt_ - 1, dnext_ref[0], 0.0)
        xin = x_ref[0]
        acc = None
        for k in range(4):
            dsh = dbuf[pl.ds(3 - k, tt), :]
            acc = w_ref[k:k + 1, :] * dsh if acc is None else acc + w_ref[k:k + 1, :] * dsh
            dw_ref[k:k + 1, :] += jnp.sum(xin * dsh, axis=0, keepdims=True)
        din_ref[0] = acc.astype(MXU)

    tile = pl.BlockSpec((1, tt, W_GQKV), lambda b, i: (b, i, 0))
    return pl.pallas_call(
        body, name="conv_bwd", grid=(bsz, nt_),
        in_specs=[tile, pl.BlockSpec((1, 8, W_GQKV), lambda b, i: (b, jnp.minimum((i + 1) * hb, t // 8 - 1), 0)),
                  tile, pl.BlockSpec((4, W_GQKV), lambda b, i: (0, 0))],
        out_specs=[tile, pl.BlockSpec((8, W_GQKV), lambda b, i: (0, 0))],
        out_shape=[jax.ShapeDtypeStruct(pq.shape, MXU), jax.ShapeDtypeStruct((8, W_GQKV), F32)],
        scratch_shapes=[pltpu.VMEM((tt + 8, W_GQKV), F32)],
        compiler_params=_params(("arbitrary", "arbitrary")),
    )(dconv, dconv, pq, conv_w)


def _chunk_specs(nc, rev, bsz, cols, sub=None):
    sub = SUB if sub is None else sub
    steps = nc // sub
    n_of = (lambda n: steps - 1 - n) if rev else (lambda n: n)
    return n_of, [pl.BlockSpec((bsz, sub * CHUNK, w), lambda n, j=j: (0, n_of(n), j)) for w, j in cols]


def _full(shape):
    return pl.BlockSpec(shape, lambda n: (0,) * len(shape))


def _heads(ref, bsz, rows, width, off=0):
    return jnp.stack([ref[b, rows, off + width * h:off + width * (h + 1)] for b in range(bsz) for h in range(H)])


def _chunk_rows(sub):
    return slice(CHUNK * sub, CHUNK * (sub + 1))


def _per_head(ref, bsz, width, rows=slice(None)):
    return jnp.stack([ref[rows, width * h:width * (h + 1)] for _ in range(bsz) for h in range(H)])


_GLA_COLS = [(256, 0), (256, 1), (512, 1), (512, 2), (128, 12)]


def _gla_args(refs, bsz, rows):
    q_ref, k_ref, v_ref, og_ref, lr_ref, wgu_ref, bg_ref, nw_ref = refs
    lr = jnp.stack([lr_ref[b, rows, :] for b in range(bsz) for _ in range(H)])
    return (_heads(q_ref, bsz, rows, 64), _heads(k_ref, bsz, rows, 64), _heads(v_ref, bsz, rows, 128), lr,
            _heads(og_ref, bsz, rows, 128), _per_head(wgu_ref, bsz, 64), _per_head(bg_ref, bsz, 64), nw_ref[...])


def _gla_fwd_call(pg, wgu, bg, nw):
    bsz, t, _ = pg.shape
    nc = t // CHUNK
    nh = bsz * H
    _, specs = _chunk_specs(nc, False, bsz, _GLA_COLS)

    def body(q_ref, k_ref, v_ref, og_ref, lr_ref, wgu_ref, bg_ref, nw_ref, y_ref, sh_ref, s_ref):
        @pl.when(pl.program_id(0) == 0)
        def _():
            s_ref[...] = jnp.zeros_like(s_ref)

        s = s_ref[...]
        for sub in range(SUB):
            rows = _chunk_rows(sub)
            q, k, v, lr, og, w, b_, nw_ = _gla_args((q_ref, k_ref, v_ref, og_ref, lr_ref, wgu_ref, bg_ref, nw_ref), bsz, rows)
            sh_ref[sub] = s
            y, s = gla_chunk(q, k, v, lr, og, s, w, b_, nw_)
            for b in range(bsz):
                for h in range(H):
                    y_ref[b, rows, 128 * h:128 * h + 128] = y[H * b + h].astype(MXU)
        s_ref[...] = s

    return pl.pallas_call(
        body, name="gla_fwd", grid=(nc // SUB,),
        in_specs=specs + [_full((128, 256)), _full((1, 256)), _full((1, 128))],
        out_specs=[pl.BlockSpec((bsz, SUB * CHUNK, 512), lambda n: (0, n, 0)),
                   pl.BlockSpec((SUB, nh, GLA_DK, DV), lambda n: (n, 0, 0, 0))],
        out_shape=[jax.ShapeDtypeStruct((bsz, t, 512), MXU), jax.ShapeDtypeStruct((nc, nh, GLA_DK, DV), F32)],
        scratch_shapes=[pltpu.VMEM((nh, GLA_DK, DV), F32)],
        compiler_params=_params(("arbitrary",)),
    )(pg, pg, pg, pg, pg, wgu, bg, nw)


def _gla_bwd_call(pg, s_hist, dyin, wgu, bg, nw):
    bsz, t, _ = pg.shape
    nc = t // CHUNK
    nh = bsz * H
    n_of, specs = _chunk_specs(nc, True, bsz, _GLA_COLS)

    def body(q_ref, k_ref, v_ref, og_ref, lr_ref, sh_ref, dy_ref, wgu_ref, bg_ref, nw_ref,
             dp_ref, dwgu_ref, dbg_ref, dnw_ref, ds_ref):
        @pl.when(pl.program_id(0) == 0)
        def _():
            dwgu_ref[...] = jnp.zeros_like(dwgu_ref)
            dbg_ref[...] = jnp.zeros_like(dbg_ref)
            dnw_ref[...] = jnp.zeros_like(dnw_ref)
            ds_ref[...] = jnp.zeros_like(ds_ref)

        ds = ds_ref[...]
        for sub in reversed(range(SUB)):
            rows = _chunk_rows(sub)
            q, k, v, lr, og, w, b_, nw_ = _gla_args((q_ref, k_ref, v_ref, og_ref, lr_ref, wgu_ref, bg_ref, nw_ref), bsz, rows)
            _, vjp = jax.vjp(gla_chunk, q, k, v, lr, og, sh_ref[sub], w, b_, nw_)
            dq, dk, dv, dlr, dog, ds, dwgu, dbg, dnw = vjp((_heads(dy_ref, bsz, rows, 128), ds))
            dnw_ref[...] += dnw
            for b in range(bsz):
                for h in range(H):
                    i = H * b + h
                    dp_ref[b, rows, 512 + 128 * h:512 + 128 * h + 128] = dv[i].astype(MXU)
                    dp_ref[b, rows, 1024 + 128 * h:1024 + 128 * h + 128] = dog[i].astype(MXU)
                    dwgu_ref[:, 64 * h:64 * h + 64] += dwgu[i]
                    dbg_ref[:, 64 * h:64 * h + 64] += dbg[i]
                for j in range(H // 2):
                    dp_ref[b, rows, 128 * j:128 * j + 128] = jnp.concatenate(
                        [dq[H * b + 2 * j], dq[H * b + 2 * j + 1]], axis=-1).astype(MXU)
                    dp_ref[b, rows, 256 + 128 * j:256 + 128 * j + 128] = jnp.concatenate(
                        [dk[H * b + 2 * j], dk[H * b + 2 * j + 1]], axis=-1).astype(MXU)
                dp_ref[b, rows, 1536:1664] = (dlr[H * b] + dlr[H * b + 1] + dlr[H * b + 2] + dlr[H * b + 3]).astype(MXU)
        ds_ref[...] = ds

    return pl.pallas_call(
        body, name="gla_bwd", grid=(nc // SUB,),
        in_specs=specs + [pl.BlockSpec((SUB, nh, GLA_DK, DV), lambda n: (n_of(n), 0, 0, 0)),
                          pl.BlockSpec((bsz, SUB * CHUNK, 512), lambda n: (0, n_of(n), 0)),
                          _full((128, 256)), _full((1, 256)), _full((1, 128))],
        out_specs=[pl.BlockSpec((bsz, SUB * CHUNK, W_GLA), lambda n: (0, n_of(n), 0)),
                   _full((128, 256)), _full((1, 256)), _full((1, 128))],
        out_shape=[jax.ShapeDtypeStruct(pg.shape, MXU), jax.ShapeDtypeStruct((128, 256), F32),
                   jax.ShapeDtypeStruct((1, 256), F32), jax.ShapeDtypeStruct((1, 128), F32)],
        scratch_shapes=[pltpu.VMEM((nh, GLA_DK, DV), F32)],
        compiler_params=_params(("arbitrary",)),
    )(pg, pg, pg, pg, pg, s_hist, dyin, wgu, bg, nw)


_GDN_COLS = [(512, 0), (512, 1), (512, 2), (512, 0), (128, 4)]


def _gdn_args(refs, bsz, rows):
    q_ref, k_ref, v_ref, og_ref, ab_ref, sc_ref, nw_ref = refs
    return (_heads(q_ref, bsz, rows, 128), _heads(k_ref, bsz, rows, 128), _heads(v_ref, bsz, rows, 128),
            _heads(ab_ref, bsz, rows, 1), _heads(ab_ref, bsz, rows, 1, off=H), _heads(og_ref, bsz, rows, 128),
            _per_head(sc_ref, bsz, 1, slice(0, 1)), _per_head(sc_ref, bsz, 1, slice(1, 2)), nw_ref[...])


def _gdn_fwd_call(conv, pd, sc, nw):
    bsz, t, _ = conv.shape
    nc = t // CHUNK
    nh = bsz * H
    _, specs = _chunk_specs(nc, False, bsz, _GDN_COLS)

    def body(q_ref, k_ref, v_ref, og_ref, ab_ref, sc_ref, nw_ref, y_ref, sh_ref, th_ref, s_ref):
        @pl.when(pl.program_id(0) == 0)
        def _():
            s_ref[...] = jnp.zeros_like(s_ref)

        s = s_ref[...]
        for sub in range(SUB):
            rows = _chunk_rows(sub)
            q, k, v, a, bb, og, alog, dtb, nw_ = _gdn_args((q_ref, k_ref, v_ref, og_ref, ab_ref, sc_ref, nw_ref), bsz, rows)
            sh_ref[sub] = s
            y, s, tinv = gdn_chunk(q, k, v, a, bb, og, s, alog, dtb, nw_)
            th_ref[sub] = tinv.astype(MXU)
            for b in range(bsz):
                for h in range(H):
                    y_ref[b, rows, 128 * h:128 * h + 128] = y[H * b + h].astype(MXU)
        s_ref[...] = s

    return pl.pallas_call(
        body, name="gdn_fwd", grid=(nc // SUB,),
        in_specs=specs + [_full((2, 128)), _full((1, 128))],
        out_specs=[pl.BlockSpec((bsz, SUB * CHUNK, 512), lambda n: (0, n, 0)),
                   pl.BlockSpec((SUB, nh, DV, DV), lambda n: (n, 0, 0, 0)),
                   pl.BlockSpec((SUB, nh, CHUNK, CHUNK), lambda n: (n, 0, 0, 0))],
        out_shape=[jax.ShapeDtypeStruct((bsz, t, 512), MXU), jax.ShapeDtypeStruct((nc, nh, DV, DV), F32),
                   jax.ShapeDtypeStruct((nc, nh, CHUNK, CHUNK), MXU)],
        scratch_shapes=[pltpu.VMEM((nh, DV, DV), F32)],
        compiler_params=_params(("arbitrary",)),
    )(conv, conv, conv, pd, pd, sc, nw)


def _gdn_bwd_call(conv, pd, s_hist, t_hist, dyin, sc, nw):
    bsz, t, _ = conv.shape
    nc = t // CHUNK
    nh = bsz * H
    n_of, specs = _chunk_specs(nc, True, bsz, _GDN_COLS, SUB_GDN_BWD)

    def body(q_ref, k_ref, v_ref, og_ref, ab_ref, sh_ref, th_ref, dy_ref, sc_ref, nw_ref,
             dc_ref, dpd_ref, dsc_ref, dnw_ref, ds_ref):
        @pl.when(pl.program_id(0) == 0)
        def _():
            dsc_ref[...] = jnp.zeros_like(dsc_ref)
            dnw_ref[...] = jnp.zeros_like(dnw_ref)
            ds_ref[...] = jnp.zeros_like(ds_ref)

        lane = lax.broadcasted_iota(jnp.int32, (CHUNK, 128), 1)
        ds = ds_ref[...]
        for sub in reversed(range(SUB_GDN_BWD)):
            rows = _chunk_rows(sub)
            q, k, v, a, bb, og, alog, dtb, nw_ = _gdn_args((q_ref, k_ref, v_ref, og_ref, ab_ref, sc_ref, nw_ref), bsz, rows)
            _, vjp = jax.vjp(functools.partial(gdn_chunk, tinv=th_ref[sub]), q, k, v, a, bb, og, sh_ref[sub], alog, dtb, nw_)
            dq, dk, dv, da, db, dog, ds, dalog, ddtb, dnw = vjp((_heads(dy_ref, bsz, rows, 128), ds))
            dnw_ref[...] += dnw
            for b in range(bsz):
                dab = jnp.zeros((CHUNK, 128), F32)
                for h in range(H):
                    i = H * b + h
                    dc_ref[b, rows, 128 * h:128 * h + 128] = dq[i]
                    dc_ref[b, rows, 512 + 128 * h:512 + 128 * h + 128] = dk[i]
                    dc_ref[b, rows, 1024 + 128 * h:1024 + 128 * h + 128] = dv[i]
                    dpd_ref[b, rows, 128 * h:128 * h + 128] = dog[i].astype(MXU)
                    dab = dab + jnp.where(lane == h, da[i], 0.0) + jnp.where(lane == H + h, db[i], 0.0)
                    dsc_ref[0:1, h:h + 1] += dalog[i]
                    dsc_ref[1:2, h:h + 1] += ddtb[i]
                dpd_ref[b, rows, 512:640] = dab.astype(MXU)
        ds_ref[...] = ds

    return pl.pallas_call(
        body, name="gdn_bwd", grid=(nc // SUB_GDN_BWD,),
        in_specs=specs + [pl.BlockSpec((SUB_GDN_BWD, nh, DV, DV), lambda n: (n_of(n), 0, 0, 0)),
                          pl.BlockSpec((SUB_GDN_BWD, nh, CHUNK, CHUNK), lambda n: (n_of(n), 0, 0, 0)),
                          pl.BlockSpec((bsz, SUB_GDN_BWD * CHUNK, 512), lambda n: (0, n_of(n), 1)),
                          _full((2, 128)), _full((1, 128))],
        out_specs=[pl.BlockSpec((bsz, SUB_GDN_BWD * CHUNK, W_GQKV), lambda n: (0, n_of(n), 0)),
                   pl.BlockSpec((bsz, SUB_GDN_BWD * CHUNK, W_GDN), lambda n: (0, n_of(n), 0)),
                   _full((2, 128)), _full((1, 128))],
        out_shape=[jax.ShapeDtypeStruct(conv.shape, F32), jax.ShapeDtypeStruct(pd.shape, MXU),
                   jax.ShapeDtypeStruct((2, 128), F32), jax.ShapeDtypeStruct((1, 128), F32)],
        scratch_shapes=[pltpu.VMEM((nh, DV, DV), F32)],
        compiler_params=_params(("arbitrary",)),
    )(conv, conv, conv, pd, pd, s_hist, t_hist, dyin, sc, nw)


def _head_call(x, ya, yb, wout, mod3, lnw, lnb, tgt):
    bsz, t, _ = x.shape
    tm = min(512, t)
    rows = min(256, tm)

    def body(x_ref, ya_ref, yb_ref, w_ref, mod_ref, lnw_ref, lnb_ref, t_ref,
             dyin_ref, dxa_ref, dgate_ref, dw_ref, dlnw_ref, dlnb_ref, loss_ref):
        b, i = pl.program_id(0), pl.program_id(1)

        @pl.when((b == 0) & (i == 0))
        def _():
            dw_ref[...] = jnp.zeros_like(dw_ref)
            dlnw_ref[...] = jnp.zeros_like(dlnw_ref)
            dlnb_ref[...] = jnp.zeros_like(dlnb_ref)
            loss_ref[...] = jnp.zeros_like(loss_ref)

        @pl.when(i == 0)
        def _():
            dgate_ref[...] = jnp.zeros_like(dgate_ref)

        w = w_ref[...]
        parts = [slice(p * rows, (p + 1) * rows) for p in range(tm // rows)]
        yin = [jnp.concatenate([ya_ref[0, rs, :], yb_ref[0, rs, :]], axis=-1).astype(MXU) for rs in parts]
        y = [jnp.dot(yi, w, preferred_element_type=F32) for yi in yin]
        for rs, yi, y_p in zip(parts, yin, y):
            loss, vjp = jax.vjp(head_fn, x_ref[0, rs, :], y_p, mod_ref[0, 2:3, :], lnw_ref[...], lnb_ref[...], t_ref[0, rs, :])
            dx, dy, dgate, dlnw, dlnb, _ = vjp(jnp.ones((1, 1), F32))
            dyb = dy.astype(MXU)
            dyin_ref[0, rs, :] = lax.dot_general(dyb, w, (((1,), (1,)), ((), ())), preferred_element_type=F32)
            dw_ref[...] += lax.dot_general(yi, dyb, (((0,), (0,)), ((), ())), preferred_element_type=F32)
            dxa_ref[0, rs, :] = dx
            dgate_ref[0] += dgate
            dlnw_ref[...] += dlnw
            dlnb_ref[...] += dlnb
            loss_ref[...] += jnp.broadcast_to(loss, (1, 128))

    tok = lambda w, j=0: pl.BlockSpec((1, tm, w), lambda b, i: (b, i, j))
    row = pl.BlockSpec((1, D), lambda b, i: (0, 0))
    return pl.pallas_call(
        body, name="head", grid=(bsz, t // tm),
        in_specs=[tok(D), tok(512), tok(512), pl.BlockSpec((D, D), lambda b, i: (0, 0)),
                  pl.BlockSpec((1, 3, D), lambda b, i: (b, 0, 0)), row, row, tok(D)],
        out_specs=[tok(D), tok(D), pl.BlockSpec((1, 1, D), lambda b, i: (b, 0, 0)),
                   pl.BlockSpec((D, D), lambda b, i: (0, 0)), row, row, pl.BlockSpec((1, 128), lambda b, i: (0, 0))],
        out_shape=[jax.ShapeDtypeStruct(x.shape, F32), jax.ShapeDtypeStruct(x.shape, F32),
                   jax.ShapeDtypeStruct((bsz, 1, D), F32), jax.ShapeDtypeStruct((D, D), F32),
                   jax.ShapeDtypeStruct((1, D), F32), jax.ShapeDtypeStruct((1, D), F32),
                   jax.ShapeDtypeStruct((1, 128), F32)],
        compiler_params=_params(("arbitrary", "arbitrary")),
    )(x, ya, yb, wout, mod3, lnw, lnb, tgt)


def _dh_call(dpg, dpq, dpd, wpt, x, mod3, dxa, smalls):
    bsz, t, _ = x.shape
    tm = min(512, t)
    assert bsz + 1 <= 8

    def body(dg_ref, dq_ref, dd_ref, w_ref, x_ref, mod_ref, dxa_ref, dgate_ref, dlnw_ref, dlnb_ref, dbg_ref, n1_ref, n2_ref,
             dsc_ref, loss_ref, gx_ref, sp_ref):
        b = pl.program_id(0)

        @pl.when((b == 0) & (pl.program_id(1) == 0))
        def _():
            sp_ref[...] = jnp.zeros_like(sp_ref)
            for e in range(bsz):
                sp_ref[e:e + 1, 2 * D:3 * D] = dgate_ref[e]
            off = 0
            for ref in (dlnw_ref, dlnb_ref, dbg_ref, n1_ref, n2_ref):
                sp_ref[bsz:bsz + 1, off:off + ref.shape[1]] = ref[...]
                off += ref.shape[1]
            sp_ref[bsz:bsz + 1, off:off + 128] = dsc_ref[0:1, :]
            sp_ref[bsz:bsz + 1, off + 128:off + 256] = dsc_ref[1:2, :]
            sp_ref[bsz:bsz + 1, SMALL_W:SMALL_W + 128] = loss_ref[...]

        mm = lambda a, lo, hi: jnp.dot(a.astype(MXU), w_ref[lo:hi, :], preferred_element_type=F32)
        dh = mm(dg_ref[0], 0, W_GLA) + mm(dq_ref[0], W_GLA, W_GLA + W_GQKV) + mm(dd_ref[0], W_GLA + W_GQKV, PW)
        gx_ref[0] = dh * (1.0 + mod_ref[0, 1:2, :]) + dxa_ref[0]
        dshift = jnp.sum(dh, axis=0, keepdims=True)
        dscale = jnp.sum(dh * x_ref[0], axis=0, keepdims=True)
        for e in range(bsz):
            @pl.when(b == e)
            def _():
                sp_ref[e:e + 1, 0:D] += dshift
                sp_ref[e:e + 1, D:2 * D] += dscale

    tok = lambda w: pl.BlockSpec((1, tm, w), lambda b, i: (b, i, 0))
    whole = lambda a: pl.BlockSpec(a.shape, lambda b, i: (0,) * a.ndim)
    return pl.pallas_call(
        body, name="dh", grid=(bsz, t // tm),
        in_specs=[tok(W_GLA), tok(W_GQKV), tok(W_GDN), pl.BlockSpec((PW, D), lambda b, i: (0, 0)), tok(D),
                  pl.BlockSpec((1, 3, D), lambda b, i: (b, 0, 0)), tok(D)] + [whole(a) for a in smalls],
        out_specs=[tok(D), pl.BlockSpec((8, SPW), lambda b, i: (0, 0))],
        out_shape=[jax.ShapeDtypeStruct(x.shape, F32), jax.ShapeDtypeStruct((8, SPW), F32)],
        compiler_params=_params(("arbitrary", "arbitrary")),
    )(dpg, dpq, dpd, wpt, x, mod3, dxa, *smalls)


def _dw_call(x, mod3, dpg, dpq, dpd):
    bsz, t, _ = x.shape
    tm = min(512, t)
    nsteps = bsz * (t // tm)

    def body(x_ref, mod_ref, dg_ref, dq_ref, dd_ref, dw_ref, acc):
        step = pl.program_id(0) * (t // tm) + pl.program_id(1)

        @pl.when(step == 0)
        def _():
            acc[...] = jnp.zeros_like(acc)

        h = (x_ref[0] * (1.0 + mod_ref[0, 1:2, :]) + mod_ref[0, 0:1, :]).astype(MXU)
        for ref, lo, hi in ((dg_ref, 0, W_GLA), (dq_ref, W_GLA, W_GLA + W_GQKV), (dd_ref, W_GLA + W_GQKV, PW)):
            acc[lo:hi, :] += lax.dot_general(ref[0].astype(MXU), h, (((0,), (0,)), ((), ())), preferred_element_type=F32)

        @pl.when(step == nsteps - 1)
        def _():
            dw_ref[...] = acc[...].astype(dw_ref.dtype)

    tok = lambda w: pl.BlockSpec((1, tm, w), lambda b, i: (b, i, 0))
    return pl.pallas_call(
        body, name="dw", grid=(bsz, t // tm),
        in_specs=[tok(D), pl.BlockSpec((1, 3, D), lambda b, i: (b, 0, 0)), tok(W_GLA), tok(W_GQKV), tok(W_GDN)],
        out_specs=pl.BlockSpec((PW, D), lambda b, i: (0, 0)),
        out_shape=jax.ShapeDtypeStruct((PW, D), WIRE),
        scratch_shapes=[pltpu.VMEM((PW, D), F32)],
        compiler_params=_params(("arbitrary", "arbitrary")),
    )(x, mod3, dpg, dpq, dpd)


def _adamw(w, g, m, v):
    m = ADAM_B1 * m + (1.0 - ADAM_B1) * g
    v = ADAM_B2 * v + (1.0 - ADAM_B2) * jnp.square(g)
    m_hat = m / (1.0 - ADAM_B1 ** ADAM_STEP)
    v_hat = v / (1.0 - ADAM_B2 ** ADAM_STEP)
    delta = -ADAM_LR * (m_hat / (jnp.sqrt(v_hat) + ADAM_EPS) + ADAM_WD * w)
    return delta, m, v


def _sum8(ref):
    g = ref[0].astype(F32)
    for j in range(1, NDEV):
        g = g + ref[j].astype(F32)
    return g


def _adam_sum_call(name, g8, w, m, v, cols):
    r, c = w.shape

    def body(g_ref, w_ref, m_ref, v_ref, go_ref, d_ref, mo_ref, vo_ref):
        g = _sum8(g_ref)
        go_ref[...] = g
        d_ref[...], mo_ref[...], vo_ref[...] = _adamw(w_ref[...], g, m_ref[...], v_ref[...])

    blk = pl.BlockSpec((r, cols), lambda i: (0, i))
    return pl.pallas_call(
        body, name=name, grid=(c // cols,),
        in_specs=[pl.BlockSpec((NDEV, r, cols), lambda i: (0, 0, i)), blk, blk, blk],
        out_specs=[blk] * 4, out_shape=[jax.ShapeDtypeStruct((r, c), F32)] * 4,
        compiler_params=_params(("parallel",)),
    )(g8, w, m, v)


def _adam_ada_call(c_all, dmod_cols, w, m, v):
    def body(c_ref, dm_ref, w_ref, m_ref, v_ref, go_ref, d_ref, mo_ref, vo_ref):
        g = lax.dot_general(c_ref[...].astype(MXU), dm_ref[...].astype(MXU), (((0,), (0,)), ((), ())),
                            preferred_element_type=F32)
        go_ref[...] = g
        d_ref[...], mo_ref[...], vo_ref[...] = _adamw(w_ref[...], g, m_ref[...], v_ref[...])

    return pl.pallas_call(
        body, name="adam_ada", out_shape=[jax.ShapeDtypeStruct(w.shape, F32)] * 4, compiler_params=_params(),
    )(c_all, dmod_cols, w, m, v)


_SMALL_AT = dict(ln_w=(0, 1024), ln_b=(1024, 1024), b_gate=(2048, 256), gla_nw=(2304, 128), gdn_nw=(2432, 128),
                 a_log=(2560, 4), dt_bias=(2688, 4))


def _adam_small_call(sp_all, bsz, params):
    names = list(params)

    def body(sp_ref, *refs):
        ins, outs = refs[:3 * len(names)], refs[3 * len(names):]
        dmod_ref, loss_ref, outs = outs[0], outs[1], outs[2:]
        packed = sp_ref[0, bsz:bsz + 1, :]
        for j in range(1, NDEV):
            packed = packed + sp_ref[j, bsz:bsz + 1, :]
        gb = None
        for j in range(NDEV):
            dmod_ref[bsz * j:bsz * j + bsz, :] = sp_ref[j, 0:bsz, :]
            for e in range(bsz):
                gb = sp_ref[j, e:e + 1, :] if gb is None else gb + sp_ref[j, e:e + 1, :]
        loss_ref[...] = packed[:, SMALL_W:SMALL_W + 128]
        for i, name in enumerate(names):
            if name == "b_ada":
                g = gb
            else:
                lo, n = _SMALL_AT[name]
                g = packed[:, lo:lo + n]
            w_ref, m_ref, v_ref = ins[3 * i:3 * i + 3]
            g_ref, d_ref, mo_ref, vo_ref = outs[4 * i:4 * i + 4]
            g_ref[...] = g
            d_ref[...], mo_ref[...], vo_ref[...] = _adamw(w_ref[...], g, m_ref[...], v_ref[...])

    flat = [a for name in names for a in params[name]]
    out_shape = [jax.ShapeDtypeStruct((NDEV * bsz, SPW), F32), jax.ShapeDtypeStruct((1, 128), F32)]
    out_shape += [jax.ShapeDtypeStruct(params[name][0].shape, F32) for name in names for _ in range(4)]
    res = pl.pallas_call(body, name="adam_small", out_shape=out_shape, compiler_params=_params())(sp_all, *flat)
    return res[0], res[1], {name: res[2 + 4 * i:6 + 4 * i] for i, name in enumerate(names)}


def _mesh_pos():
    x, y, c = lax.axis_index("x"), lax.axis_index("y"), lax.axis_index("c")
    return x, y, c, 4 * x + 2 * y + c


def _peer(x, y, c, k):
    px = 1 - x if k & 4 else x
    py = 1 - y if k & 2 else y
    pc = 1 - c if k & 1 else c
    return (px, py, pc), 4 * px + 2 * py + pc


_ANY = pl.BlockSpec(memory_space=pl.ANY)
_VMEM = pl.BlockSpec(memory_space=pltpu.VMEM)


def _gather_call(c8, w_ada, b_sh, w_in_t):
    C_SEM, W_SEM, MOD_SEM = 0, 1, 2

    def body(c_ref, wada_ref, b_ref, win_ref, wall_ref, call_ref, mod_ref, modp, send_sems, recv_sems, loc_sem):
        x, y, c, me = _mesh_pos()

        def remote(src, dst, a, k, to):
            return pltpu.make_async_remote_copy(src_ref=src, dst_ref=dst, send_sem=send_sems.at[a, k],
                                                recv_sem=recv_sems.at[a, k], device_id=_peer(x, y, c, to)[0],
                                                device_id_type=pl.DeviceIdType.MESH)

        idx = lambda k: _peer(x, y, c, k)[1]
        sends = []
        call_ref[me] = c_ref[...]
        for k in range(1, NDEV):
            sends.append(remote(c_ref, call_ref.at[me], C_SEM, k, k))
            sends[-1].start()
        local = pltpu.make_async_copy(win_ref, wall_ref.at[me], loc_sem)
        local.start()
        for k in (1, 2, 4, 6):
            sends.append(remote(win_ref, wall_ref.at[me], W_SEM, k, k))
            sends[-1].start()
        for k in range(1, NDEV):
            remote(c_ref, call_ref.at[idx(k)], C_SEM, k, k).wait_recv()
        modp[...] = jnp.dot(call_ref[...].reshape(NDEV * 8, D).astype(MXU), wada_ref[...].astype(MXU),
                            preferred_element_type=F32) + b_ref[...]
        mod_ref[me] = modp[pl.ds(pl.multiple_of(me * 8, 8), 8), :]
        for k in range(1, NDEV):
            sends.append(remote(modp.at[pl.ds(pl.multiple_of(idx(k) * 8, 8), 8), :], mod_ref.at[me], MOD_SEM, k, k))
            sends[-1].start()
        for k in (2, 4, 6):
            remote(win_ref, wall_ref.at[idx(k)], W_SEM, k, k).wait_recv()
            sends.append(remote(wall_ref.at[idx(k)], wall_ref.at[idx(k)], W_SEM, k + 1, 1))
            sends[-1].start()
        for k in (1, 3, 5, 7):
            remote(win_ref, wall_ref.at[idx(k)], W_SEM, k, 1).wait_recv()
        for k in range(1, NDEV):
            remote(modp.at[pl.ds(0, 8), :], mod_ref.at[idx(k)], MOD_SEM, k, k).wait_recv()
        for cp in sends:
            cp.wait_send()
        local.wait()

    return pl.pallas_call(
        body, name="gather",
        out_shape=[jax.ShapeDtypeStruct((NDEV,) + w_in_t.shape, w_in_t.dtype), jax.ShapeDtypeStruct((NDEV, 8, D), F32),
                   jax.ShapeDtypeStruct((NDEV, 8, SHARD_ADA), F32)],
        in_specs=[_VMEM, _VMEM, _VMEM, _ANY], out_specs=[_ANY, _VMEM, _VMEM],
        scratch_shapes=[pltpu.VMEM((NDEV * 8, SHARD_ADA), F32), pltpu.SemaphoreType.DMA((3, NDEV)),
                        pltpu.SemaphoreType.DMA((3, NDEV)), pltpu.SemaphoreType.DMA],
        compiler_params=_params(),
    )(c8, w_ada, b_sh, w_in_t)


_HBM = pl.BlockSpec(memory_space=pltpu.HBM)
_SEM = pl.BlockSpec(memory_space=pltpu.SEMAPHORE)
_EFFECT = pltpu.SideEffectType.DATAFLOW_SIDE_EFFECTING


def _whole(gather, a):
    return gather[a] if isinstance(gather, (list, tuple)) else gather


def _xchg_start(name, blocks, lands, gather):
    nb = len(blocks)

    def body(*refs):
        srcs, dsts = refs[:nb], refs[nb:2 * nb]
        send_sems, recv_sems = refs[2 * nb], refs[2 * nb + 1]
        token = refs[-1]
        x, y, c, me = _mesh_pos()
        for k in range(1, NDEV):
            dev, pidx = _peer(x, y, c, k)
            for a in range(nb):
                pltpu.make_async_remote_copy(src_ref=srcs[a] if _whole(gather, a) else srcs[a].at[pidx], dst_ref=dsts[a].at[me],
                                             send_sem=send_sems.at[NDEV * a + k], recv_sem=recv_sems.at[NDEV * a + k],
                                             device_id=dev, device_id_type=pl.DeviceIdType.MESH).start()
        token[...] = jnp.zeros_like(token)

    thru = [pltpu.HBM(a.shape, a.dtype) for a in list(blocks) + list(lands)]
    return pl.pallas_call(
        body, name=name,
        out_shape=(pltpu.SemaphoreType.DMA((nb * NDEV,)), pltpu.SemaphoreType.DMA((nb * NDEV,)), *thru,
                   jax.ShapeDtypeStruct((8, 128), F32)),
        in_specs=[_HBM] * (2 * nb), out_specs=(_SEM, _SEM, *([_HBM] * (2 * nb)), _VMEM),
        input_output_aliases={i: 2 + i for i in range(2 * nb)},
        compiler_params=pltpu.CompilerParams(has_side_effects=_EFFECT),
    )(*[pltpu.with_memory_space_constraint(a, pltpu.HBM) for a in list(blocks) + list(lands)])


def _xchg_wait(name, send_sems, recv_sems, thru, after, gather):
    nb = len(thru) // 2

    def body(*refs):
        srcs, dsts = refs[:nb], refs[nb:2 * nb]
        send_sems, recv_sems = refs[2 * nb], refs[2 * nb + 1]
        x, y, c, me = _mesh_pos()
        for k in range(1, NDEV):
            dev, pidx = _peer(x, y, c, k)
            for a in range(nb):
                cp = pltpu.make_async_remote_copy(src_ref=srcs[a] if _whole(gather, a) else srcs[a].at[pidx], dst_ref=dsts[a].at[pidx],
                                                  send_sem=send_sems.at[NDEV * a + k], recv_sem=recv_sems.at[NDEV * a + k],
                                                  device_id=dev, device_id_type=pl.DeviceIdType.MESH)
                cp.wait_send()
                cp.wait_recv()

    out = pl.pallas_call(
        body, name=name, out_shape=tuple(pltpu.HBM(a.shape, a.dtype) for a in thru),
        in_specs=[_HBM] * (2 * nb) + [_SEM, _SEM, pl.BlockSpec(memory_space=pl.ANY)], out_specs=tuple([_HBM] * (2 * nb)),
        input_output_aliases={i: i for i in range(2 * nb)},
        compiler_params=pltpu.CompilerParams(has_side_effects=_EFFECT),
    )(*thru, send_sems, recv_sems, after)
    return out[nb:]


def _pad_cols(a, n):
    return jnp.pad(a, ((0, 0), (0, n - a.shape[1])))


def _assemble_wt(wt_full):
    q, k, v, lr, og, gqkv, ab, dog = jnp.split(wt_full, [256, 512, 1024, 1040, 1552, 3088, 3096], axis=0)
    z = lambda n: jnp.zeros((n, wt_full.shape[1]), wt_full.dtype)
    return jnp.concatenate([q, k, v, og, lr, z(112), gqkv, dog, ab, z(120)], axis=0)


def _disassemble_dwt(dwt):
    g0, q0 = W_GLA, W_GLA + W_GQKV
    return jnp.concatenate([dwt[:1024], dwt[1536:1552], dwt[1024:1536], dwt[g0:q0], dwt[q0 + 512:q0 + 520],
                            dwt[q0:q0 + 512]], axis=0)


def local_grads(x, mod3, wp, late_weights, bg, gla_nw, sc, gdn_nw, lnw, lnb, tgt):
    pg, pq, pd = _proj_call(x, mod3, wp)
    wout, conv_w, wgu_p = late_weights(pq)
    conv = _conv_fwd_call(pq, conv_w)
    ya, s_gla = _gla_fwd_call(pg, wgu_p, bg, gla_nw)
    yb, s_gdn, t_gdn = _gdn_fwd_call(conv, pd, sc, gdn_nw)
    dyin, dxa, dgate, dwout, dlnw, dlnb, loss = _head_call(x, ya, yb, wout, mod3, lnw, lnb, tgt)
    dpg, dwgu, dbg, dnw_gla = _gla_bwd_call(pg, s_gla, dyin, wgu_p, bg, gla_nw)
    dconv, dpd, dsc, dnw_gdn = _gdn_bwd_call(conv, pd, s_gdn, t_gdn, dyin, sc, gdn_nw)
    dpq, dconv_w = _conv_bwd_call(dconv, pq, conv_w)
    dw_in = _disassemble_dwt(_dw_call(x, mod3, dpg, dpq, dpd))
    g = dict(dw_in=dw_in, dwout=dwout, dconv_w=dconv_w[:4], dwgu=dwgu[:16])
    smalls = (dgate, dlnw, dlnb, dbg, dnw_gla, dnw_gdn, dsc, loss)
    return g, lambda mod3_: _dh_call(dpg, dpq, dpd, wp, x, mod3_, dxa, smalls)


def local_step(x, mod3, wp, wout, conv_w, wgu_p, *args):
    g, finish = local_grads(x, mod3, wp, lambda _: (wout, conv_w, wgu_p), *args)
    g["gx"], sp = finish(mod3)
    bsz = x.shape[0]
    g["dmod"] = sp[:bsz].reshape(bsz, 3, D)
    g["loss"] = sp[bsz, SMALL_W]
    for name, (lo, n) in _SMALL_AT.items():
        g[name] = sp[bsz:bsz + 1, lo:lo + n]
    return g


def kernel(x, c, w_ada, b_ada, w_in, gla_w_gate_up, gla_b_gate, gla_norm_w, gdn_conv_w, gdn_a_log, gdn_dt_bias, gdn_norm_w, w_out, ln_w, ln_b, loss_target, m_w_ada, m_b_ada, m_w_in, m_gla_w_gate_up, m_gla_b_gate, m_gla_norm_w, m_gdn_conv_w, m_gdn_a_log, m_gdn_dt_bias, m_gdn_norm_w, m_w_out, m_ln_w, m_ln_b, v_w_ada, v_b_ada, v_w_in, v_gla_w_gate_up, v_gla_b_gate, v_gla_norm_w, v_gdn_conv_w, v_gdn_a_log, v_gdn_dt_bias, v_gdn_norm_w, v_w_out, v_ln_w, v_ln_b):
    me = 4 * lax.axis_index("x") + 2 * lax.axis_index("y") + lax.axis_index("c")
    bsz = x.shape[0]

    b_sh = lax.dynamic_slice(b_ada, (0, me * SHARD_ADA), (1, SHARD_ADA))
    c8 = jnp.pad(c, ((0, 8 - bsz), (0, 0)))
    w_in_t, m_in_t, v_in_t = (jnp.swapaxes(a[0], 0, 1) for a in (w_in, m_w_in, v_w_in))
    win_all, c_all, mod_blk = _gather_call(c8, w_ada[0], b_sh, w_in_t.astype(WIRE))
    wp = _assemble_wt(win_all.reshape(IN_COLS, D))
    mod = jnp.transpose(mod_blk[:, :bsz, :], (1, 0, 2)).reshape(bsz, 3 * D)
    mod3 = mod.reshape(bsz, 3, D)
    sc = jnp.concatenate([_pad_cols(gdn_a_log, 128), _pad_cols(gdn_dt_bias, 128)], axis=0)

    own = lambda a: lax.dynamic_update_slice(lax.empty((NDEV,) + a.shape, a.dtype), a[None], (me,) + (0,) * a.ndim)
    late = [w_out[0].astype(WIRE), gdn_conv_w[0], gla_w_gate_up[0] + 0.0 * mod_blk[0, 0, 0]]
    w_send, w_recv, *w_thru, w_token = _xchg_start("wgather_start", late, [own(a) for a in late], gather=True)

    def late_weights(pq):
        wout_all, conv_all, wgu_all = _xchg_wait("wgather_wait", w_send, w_recv, w_thru, pq, gather=True)
        return (wout_all.reshape(D, D), jnp.transpose(conv_all, (1, 0, 2)).reshape(4, W_GQKV),
                jnp.pad(jnp.transpose(wgu_all, (1, 0, 2)).reshape(16, 256), ((0, 112), (0, 0))))

    g, finish = local_grads(x, mod3 + w_token[0, 0], wp, late_weights, gla_b_gate, gla_norm_w, sc, gdn_norm_w, ln_w, ln_b,
                            loss_target)

    big = [g["dw_in"].reshape(NDEV, SHARD_IN, D), g["dwout"].reshape(NDEV, D // NDEV, D).astype(WIRE)]
    lands = [lax.dynamic_update_slice(lax.empty(a.shape, a.dtype), lax.dynamic_slice(a, (me, 0, 0), (1,) + a.shape[1:]),
                                      (me, 0, 0)) for a in big]
    send_sems, recv_sems, *thru, token = _xchg_start("xchg_start", big, lands, gather=False)
    gx, sp = finish(mod3 + token[0, 0])
    little = [jnp.transpose(g["dconv_w"].reshape(4, NDEV, W_GQKV // NDEV), (1, 0, 2)),
              jnp.transpose(g["dwgu"].reshape(16, NDEV, 256 // NDEV), (1, 0, 2)), sp]
    modes = [False, False, True]
    l_lands = [lax.dynamic_update_slice(lax.empty(a.shape, a.dtype), lax.dynamic_slice(a, (me, 0, 0), (1,) + a.shape[1:]),
                                        (me, 0, 0)) for a in little[:2]] + [own(sp)]
    l_send, l_recv, *l_thru, l_token = _xchg_start("small_start", little, l_lands, gather=modes)
    r_in, r_out = _xchg_wait("xchg_wait", send_sems, recv_sems, thru, l_token, gather=False)

    t_in = [jnp.swapaxes(a, 0, 1) for a in _adam_sum_call("adam_in", r_in, w_in_t, m_in_t, v_in_t, 256)]
    t_out = _adam_sum_call("adam_out", r_out, w_out[0], m_w_out[0], v_w_out[0], D)
    r_conv, r_gu, sp_all = _xchg_wait("small_wait", l_send, l_recv, l_thru, t_out[3], gather=modes)
    t_conv = _adam_sum_call("adam_conv", r_conv, gdn_conv_w[0], m_gdn_conv_w[0], v_gdn_conv_w[0], W_GQKV // NDEV)
    t_gu = _adam_sum_call("adam_gu", r_gu, gla_w_gate_up[0], m_gla_w_gate_up[0], v_gla_w_gate_up[0], 256 // NDEV)
    dmod_all, loss, small = _adam_small_call(sp_all, bsz, dict(
        b_ada=(b_ada, m_b_ada, v_b_ada), ln_w=(ln_w, m_ln_w, v_ln_w), ln_b=(ln_b, m_ln_b, v_ln_b),
        b_gate=(gla_b_gate, m_gla_b_gate, v_gla_b_gate), gla_nw=(gla_norm_w, m_gla_norm_w, v_gla_norm_w),
        gdn_nw=(gdn_norm_w, m_gdn_norm_w, v_gdn_norm_w), a_log=(gdn_a_log, m_gdn_a_log, v_gdn_a_log),
        dt_bias=(gdn_dt_bias, m_gdn_dt_bias, v_gdn_dt_bias)))
    c16 = c_all[:, :bsz, :].reshape(NDEV * bsz, D)
    t_ada = _adam_ada_call(c16, lax.dynamic_slice(dmod_all, (0, me * SHARD_ADA), (NDEV * bsz, SHARD_ADA)),
                           w_ada[0], m_w_ada[0], v_w_ada[0])

    def group(i):
        s = lambda name: small[name][i]
        return [t_ada[i][None], s("b_ada"), t_in[i][None], t_gu[i][None], s("b_gate"), s("gla_nw"), t_conv[i][None],
                s("a_log"), s("dt_bias"), s("gdn_nw"), t_out[i][None], s("ln_w"), s("ln_b")]

    return (loss[0, 0], gx, *group(0), *group(1), *group(2), *group(3))
```

```python
import functools

import jax
import jax.numpy as jnp
from jax import lax
from jax.experimental import pallas as pl
from jax.experimental.pallas import tpu as pltpu

F32 = jnp.float32
MXU = jnp.bfloat16
WIRE = jnp.bfloat16
HI = lax.Precision.HIGH

D = 1024
NDEV = 8
H = 4
GLA_DK = 64
DV = 128
CHUNK = 64
SUB = 8
SUB_GDN_BWD = 4
LN_EPS = 1e-5
RMS_EPS = 1e-6
ALPHA = 2.0 ** 0.25
GATE_NORM = 16.0

W_GLA, W_GQKV, W_GDN = 1664, 1536, 640
PW = W_GLA + W_GQKV + W_GDN
IN_COLS = 3608
SHARD_IN = IN_COLS // NDEV
SHARD_ADA = 3 * D // NDEV
SPW = 3 * D
SMALL_W = 2816

ADAM_LR, ADAM_B1, ADAM_B2, ADAM_EPS, ADAM_WD, ADAM_STEP = 0.001, 0.9, 0.999, 1e-08, 0.01, 10

VMEM_LIMIT = 56 * 1024 * 1024


def _params(sem=None, **kw):
    if sem is not None:
        kw["dimension_semantics"] = sem
    return pltpu.CompilerParams(vmem_limit_bytes=VMEM_LIMIT, **kw)


_MM = (((2,), (1,)), ((0,), (0,)))
_NT = (((2,), (2,)), ((0,), (0,)))
_TN = (((1,), (1,)), ((0,), (0,)))


def _dg(a, b, dims):
    return lax.dot_general(a.astype(MXU), b.astype(MXU), dims, preferred_element_type=F32)


def _hdg(a, b, dims):
    return lax.dot_general(a, b, dims, precision=HI, preferred_element_type=F32)


@jax.custom_vjp
def bmm(a, b):
    return _dg(a, b, _MM)


bmm.defvjp(lambda a, b: (_dg(a, b, _MM), (a, b)), lambda r, g: (_dg(g, r[1], _NT), _dg(r[0], g, _TN)))


@jax.custom_vjp
def bnt(a, b):
    return _dg(a, b, _NT)


bnt.defvjp(lambda a, b: (_dg(a, b, _NT), (a, b)), lambda r, g: (_dg(g, r[1], _MM), _dg(g, r[0], _TN)))


@jax.custom_vjp
def btn(a, b):
    return _dg(a, b, _TN)


btn.defvjp(lambda a, b: (_dg(a, b, _TN), (a, b)), lambda r, g: (_dg(r[1], g, _NT), _dg(r[0], g, _MM)))


def unit_lower_inverse(a):
    n = a.shape[-1]
    r, c = _iotas(n)
    p = -a
    t = (r == c).astype(F32) + p
    for _ in range(5):
        p = _dg(p, p, _MM)
        t = t + _dg(t, p, _MM)
    return t


@jax.custom_vjp
def unit_lower_solve(a, t, r1, r2):
    return _dg(t, r1, _MM), _dg(t, r2, _MM)


def _solve_fwd(a, t, r1, r2):
    s1, s2 = _dg(t, r1, _MM), _dg(t, r2, _MM)
    return (s1, s2), (t, s1, s2)


def _solve_bwd(res, g):
    t, s1, s2 = res
    d1, d2 = _dg(t, g[0], _TN), _dg(t, g[1], _TN)
    return -(_dg(d1, s1, _NT) + _dg(d2, s2, _NT)), jnp.zeros_like(t), d1, d2


unit_lower_solve.defvjp(_solve_fwd, _solve_bwd)


def _iotas(n):
    return lax.broadcasted_iota(jnp.int32, (n, n), 0), lax.broadcasted_iota(jnp.int32, (n, n), 1)


def _col_to_row(col, eye):
    return jnp.sum(jnp.where(eye, col, 0.0), axis=1, keepdims=True)


def _row_to_col(row, eye):
    return jnp.sum(jnp.where(eye, row, 0.0), axis=2, keepdims=True)


def _pick_row(m, i):
    r = lax.broadcasted_iota(jnp.int32, m.shape, 1)
    return jnp.sum(jnp.where(r == i, m, 0.0), axis=1, keepdims=True)


def _rms_gate(o, nw, og):
    on = o * lax.rsqrt(jnp.mean(o * o, axis=-1, keepdims=True) + RMS_EPS) * nw
    return on * jax.nn.silu(og)


def gla_chunk(q, k, v, lr, og, s, wgu, bg, nw):
    n, c, _ = q.shape
    r, cc = _iotas(c)
    causal = r >= cc
    qs = q * (GLA_DK ** -0.5)
    z = bmm(lr, wgu) + bg
    g = jax.nn.log_sigmoid(z) / GATE_NORM
    b = _hdg(jnp.broadcast_to(causal.astype(F32), (n, c, c)), g, _MM)
    bref = _pick_row(b, c // 2 - 1)
    blast = _pick_row(b, c - 1)
    att = jnp.where(causal, bnt(qs * jnp.exp(b - bref), k * jnp.exp(bref - b)), 0.0)
    o = bmm(att, v) + bmm(qs * jnp.exp(b), s)
    rk, ck = _iotas(GLA_DK)
    s_new = _row_to_col(jnp.exp(blast), rk == ck) * s + btn(k * jnp.exp(blast - b), v)
    return _rms_gate(o, nw, og), s_new


def gdn_chunk(cq, ck, cv, a, bb, og, s, alog, dtb, nw, tinv=None):
    c = cq.shape[1]
    r, cc = _iotas(c)
    eye, causal, strict = r == cc, r >= cc, r > cc
    q, k, v = jax.nn.silu(cq), jax.nn.silu(ck), jax.nn.silu(cv)
    q = q * lax.rsqrt(jnp.sum(q * q, axis=-1, keepdims=True) + RMS_EPS) * (DV ** -0.5)
    k = k * lax.rsqrt(jnp.sum(k * k, axis=-1, keepdims=True) + RMS_EPS)
    g = -jnp.exp(alog) * jax.nn.softplus(a + dtb)
    beta = jax.nn.sigmoid(bb)
    d = jnp.sum(jnp.where(causal, _col_to_row(g, eye), 0.0), axis=2, keepdims=True)
    el = jnp.exp(jnp.where(causal, d - _col_to_row(d, eye), -jnp.inf))
    kb = k * beta
    amat = jnp.where(strict, bnt(kb, k) * el, 0.0)
    t = unit_lower_inverse(amat) if tinv is None else tinv
    u, w = unit_lower_solve(amat, t, v * beta, kb * jnp.exp(d))
    qk = jnp.where(causal, bnt(q, k) * el, 0.0)
    dlast = _pick_row(d, c - 1)
    v_new = u - bmm(w, s)
    o = bmm(q * jnp.exp(d), s) + bmm(qk, v_new)
    s_new = jnp.exp(dlast) * s + btn(k * jnp.exp(dlast - d), v_new)
    y = _rms_gate(o, nw, og)
    return (y, s_new, t) if tinv is None else (y, s_new)


def head_fn(x, y, gate, lnw, lnb, tgt):
    u = ALPHA * x + (1.0 + gate) * y
    mu = jnp.mean(u, axis=-1, keepdims=True)
    var = jnp.mean(jnp.square(u - mu), axis=-1, keepdims=True)
    out = (u - mu) * lax.rsqrt(var + LN_EPS) * lnw + lnb
    err = jnp.square(out - tgt)
    return 0.5 * jnp.sum(jnp.mean(err, axis=-1, keepdims=True), axis=0, keepdims=True)


def _proj_call(x, mod3, wpt, conv_w):
    bsz, t, _ = x.shape
    tm = min(512, t)

    def body(x_ref, mod_ref, w_ref, cw_ref, pg_ref, pq_ref, pd_ref, conv_ref, buf):
        i = pl.program_id(1)
        h = (x_ref[0] * (1.0 + mod_ref[0, 1:2, :]) + mod_ref[0, 0:1, :]).astype(MXU)
        nt = lambda lo, hi: lax.dot_general(h, w_ref[lo:hi, :], (((1,), (1,)), ((), ())), preferred_element_type=F32)
        pq = nt(W_GLA, W_GLA + W_GQKV)
        pq_ref[0] = pq
        @pl.when(i == 0)
        def _():
            buf[0:8, :] = jnp.zeros((8, W_GQKV), F32)

        @pl.when(i > 0)
        def _():
            buf[0:8, :] = buf[tm:tm + 8, :]

        buf[8:, :] = pq
        acc = cw_ref[0:1, :] * buf[pl.ds(5, tm), :]
        for k in range(1, 4):
            acc = acc + cw_ref[k:k + 1, :] * buf[pl.ds(5 + k, tm), :]
        conv_ref[0] = acc
        pg_ref[0] = nt(0, W_GLA)
        pd_ref[0] = nt(W_GLA + W_GQKV, PW)

    tok = lambda w: pl.BlockSpec((1, tm, w), lambda b, i: (b, i, 0))
    return pl.pallas_call(
        body, name="proj", grid=(bsz, t // tm),
        in_specs=[tok(D), pl.BlockSpec((1, 3, D), lambda b, i: (b, 0, 0)), pl.BlockSpec((PW, D), lambda b, i: (0, 0)),
                  pl.BlockSpec((4, W_GQKV), lambda b, i: (0, 0))],
        out_specs=[tok(W_GLA), tok(W_GQKV), tok(W_GDN), tok(W_GQKV)],
        out_shape=[jax.ShapeDtypeStruct((bsz, t, w), F32) for w in (W_GLA, W_GQKV, W_GDN, W_GQKV)],
        scratch_shapes=[pltpu.VMEM((tm + 8, W_GQKV), F32)],
        compiler_params=_params(("arbitrary", "arbitrary")),
    )(x, mod3, wpt, conv_w)


def _conv_bwd_call(dconv, pq, conv_w):
    bsz, t, _ = pq.shape
    tt = min(512, t)
    hb = tt // 8
    nt_ = t // tt

    def body(d_ref, dnext_ref, x_ref, w_ref, din_ref, dw_ref, dbuf):
        b, i = pl.program_id(0), pl.program_id(1)

        @pl.when((b == 0) & (i == 0))
        def _():
            dw_ref[...] = jnp.zeros_like(dw_ref)

        dbuf[0:tt, :] = d_ref[0]
        dbuf[tt:, :] = jnp.where(i < nt_ - 1, dnext_ref[0], 0.0)
        xin = x_ref[0]
        acc = None
        for k in range(4):
            dsh = dbuf[pl.ds(3 - k, tt), :]
            acc = w_ref[k:k + 1, :] * dsh if acc is None else acc + w_ref[k:k + 1, :] * dsh
            dw_ref[k:k + 1, :] += jnp.sum(xin * dsh, axis=0, keepdims=True)
        din_ref[0] = acc.astype(MXU)

    tile = pl.BlockSpec((1, tt, W_GQKV), lambda b, i: (b, i, 0))
    return pl.pallas_call(
        body, name="conv_bwd", grid=(bsz, nt_),
        in_specs=[tile, pl.BlockSpec((1, 8, W_GQKV), lambda b, i: (b, jnp.minimum((i + 1) * hb, t // 8 - 1), 0)),
                  tile, pl.BlockSpec((4, W_GQKV), lambda b, i: (0, 0))],
        out_specs=[tile, pl.BlockSpec((8, W_GQKV), lambda b, i: (0, 0))],
        out_shape=[jax.ShapeDtypeStruct(pq.shape, MXU), jax.ShapeDtypeStruct((8, W_GQKV), F32)],
        scratch_shapes=[pltpu.VMEM((tt + 8, W_GQKV), F32)],
        compiler_params=_params(("arbitrary", "arbitrary")),
    )(dconv, dconv, pq, conv_w)


def _chunk_specs(nc, rev, bsz, cols, sub=None):
    sub = SUB if sub is None else sub
    steps = nc // sub
    n_of = (lambda n: steps - 1 - n) if rev else (lambda n: n)
    return n_of, [pl.BlockSpec((bsz, sub * CHUNK, w), lambda n, j=j: (0, n_of(n), j)) for w, j in cols]


def _full(shape):
    return pl.BlockSpec(shape, lambda n: (0,) * len(shape))


def _heads(ref, bsz, rows, width, off=0):
    return jnp.stack([ref[b, rows, off + width * h:off + width * (h + 1)] for b in range(bsz) for h in range(H)])


def _chunk_rows(sub):
    return slice(CHUNK * sub, CHUNK * (sub + 1))


def _per_head(ref, bsz, width, rows=slice(None)):
    return jnp.stack([ref[rows, width * h:width * (h + 1)] for _ in range(bsz) for h in range(H)])


_GLA_COLS = [(256, 0), (256, 1), (512, 1), (512, 2), (128, 12)]


def _gla_args(refs, bsz, rows):
    q_ref, k_ref, v_ref, og_ref, lr_ref, wgu_ref, bg_ref, nw_ref = refs
    lr = jnp.stack([lr_ref[b, rows, :] for b in range(bsz) for _ in range(H)])
    return (_heads(q_ref, bsz, rows, 64), _heads(k_ref, bsz, rows, 64), _heads(v_ref, bsz, rows, 128), lr,
            _heads(og_ref, bsz, rows, 128), _per_head(wgu_ref, bsz, 64), _per_head(bg_ref, bsz, 64), nw_ref[...])


def _gla_fwd_call(pg, wgu, bg, nw):
    bsz, t, _ = pg.shape
    nc = t // CHUNK
    nh = bsz * H
    _, specs = _chunk_specs(nc, False, bsz, _GLA_COLS)

    def body(q_ref, k_ref, v_ref, og_ref, lr_ref, wgu_ref, bg_ref, nw_ref, y_ref, sh_ref, s_ref):
        @pl.when(pl.program_id(0) == 0)
        def _():
            s_ref[...] = jnp.zeros_like(s_ref)

        s = s_ref[...]
        for sub in range(SUB):
            rows = _chunk_rows(sub)
            q, k, v, lr, og, w, b_, nw_ = _gla_args((q_ref, k_ref, v_ref, og_ref, lr_ref, wgu_ref, bg_ref, nw_ref), bsz, rows)
            sh_ref[sub] = s
            y, s = gla_chunk(q, k, v, lr, og, s, w, b_, nw_)
            for b in range(bsz):
                for h in range(H):
                    y_ref[b, rows, 128 * h:128 * h + 128] = y[H * b + h].astype(MXU)
        s_ref[...] = s

    return pl.pallas_call(
        body, name="gla_fwd", grid=(nc // SUB,),
        in_specs=specs + [_full((128, 256)), _full((1, 256)), _full((1, 128))],
        out_specs=[pl.BlockSpec((bsz, SUB * CHUNK, 512), lambda n: (0, n, 0)),
                   pl.BlockSpec((SUB, nh, GLA_DK, DV), lambda n: (n, 0, 0, 0))],
        out_shape=[jax.ShapeDtypeStruct((bsz, t, 512), MXU), jax.ShapeDtypeStruct((nc, nh, GLA_DK, DV), F32)],
        scratch_shapes=[pltpu.VMEM((nh, GLA_DK, DV), F32)],
        compiler_params=_params(("arbitrary",)),
    )(pg, pg, pg, pg, pg, wgu, bg, nw)


def _gla_bwd_call(pg, s_hist, dyin, wgu, bg, nw):
    bsz, t, _ = pg.shape
    nc = t // CHUNK
    nh = bsz * H
    n_of, specs = _chunk_specs(nc, True, bsz, _GLA_COLS)

    def body(q_ref, k_ref, v_ref, og_ref, lr_ref, sh_ref, dy_ref, wgu_ref, bg_ref, nw_ref,
             dp_ref, dwgu_ref, dbg_ref, dnw_ref, ds_ref):
        @pl.when(pl.program_id(0) == 0)
        def _():
            dwgu_ref[...] = jnp.zeros_like(dwgu_ref)
            dbg_ref[...] = jnp.zeros_like(dbg_ref)
            dnw_ref[...] = jnp.zeros_like(dnw_ref)
            ds_ref[...] = jnp.zeros_like(ds_ref)

        ds = ds_ref[...]
        for sub in reversed(range(SUB)):
            rows = _chunk_rows(sub)
            q, k, v, lr, og, w, b_, nw_ = _gla_args((q_ref, k_ref, v_ref, og_ref, lr_ref, wgu_ref, bg_ref, nw_ref), bsz, rows)
            _, vjp = jax.vjp(gla_chunk, q, k, v, lr, og, sh_ref[sub], w, b_, nw_)
            dq, dk, dv, dlr, dog, ds, dwgu, dbg, dnw = vjp((_heads(dy_ref, bsz, rows, 128), ds))
            dnw_ref[...] += dnw
            for b in range(bsz):
                for h in range(H):
                    i = H * b + h
                    dp_ref[b, rows, 512 + 128 * h:512 + 128 * h + 128] = dv[i].astype(MXU)
                    dp_ref[b, rows, 1024 + 128 * h:1024 + 128 * h + 128] = dog[i].astype(MXU)
                    dwgu_ref[:, 64 * h:64 * h + 64] += dwgu[i]
                    dbg_ref[:, 64 * h:64 * h + 64] += dbg[i]
                for j in range(H // 2):
                    dp_ref[b, rows, 128 * j:128 * j + 128] = jnp.concatenate(
                        [dq[H * b + 2 * j], dq[H * b + 2 * j + 1]], axis=-1).astype(MXU)
                    dp_ref[b, rows, 256 + 128 * j:256 + 128 * j + 128] = jnp.concatenate(
                        [dk[H * b + 2 * j], dk[H * b + 2 * j + 1]], axis=-1).astype(MXU)
                dp_ref[b, rows, 1536:1664] = (dlr[H * b] + dlr[H * b + 1] + dlr[H * b + 2] + dlr[H * b + 3]).astype(MXU)
        ds_ref[...] = ds

    return pl.pallas_call(
        body, name="gla_bwd", grid=(nc // SUB,),
        in_specs=specs + [pl.BlockSpec((SUB, nh, GLA_DK, DV), lambda n: (n_of(n), 0, 0, 0)),
                          pl.BlockSpec((bsz, SUB * CHUNK, 512), lambda n: (0, n_of(n), 0)),
                          _full((128, 256)), _full((1, 256)), _full((1, 128))],
        out_specs=[pl.BlockSpec((bsz, SUB * CHUNK, W_GLA), lambda n: (0, n_of(n), 0)),
                   _full((128, 256)), _full((1, 256)), _full((1, 128))],
        out_shape=[jax.ShapeDtypeStruct(pg.shape, MXU), jax.ShapeDtypeStruct((128, 256), F32),
                   jax.ShapeDtypeStruct((1, 256), F32), jax.ShapeDtypeStruct((1, 128), F32)],
        scratch_shapes=[pltpu.VMEM((nh, GLA_DK, DV), F32)],
        compiler_params=_params(("arbitrary",)),
    )(pg, pg, pg, pg, pg, s_hist, dyin, wgu, bg, nw)


_GDN_COLS = [(512, 0), (512, 1), (512, 2), (512, 0), (128, 4)]


def _gdn_args(refs, bsz, rows):
    q_ref, k_ref, v_ref, og_ref, ab_ref, sc_ref, nw_ref = refs
    return (_heads(q_ref, bsz, rows, 128), _heads(k_ref, bsz, rows, 128), _heads(v_ref, bsz, rows, 128),
            _heads(ab_ref, bsz, rows, 1), _heads(ab_ref, bsz, rows, 1, off=H), _heads(og_ref, bsz, rows, 128),
            _per_head(sc_ref, bsz, 1, slice(0, 1)), _per_head(sc_ref, bsz, 1, slice(1, 2)), nw_ref[...])


def _gdn_fwd_call(conv, pd, sc, nw):
    bsz, t, _ = conv.shape
    nc = t // CHUNK
    nh = bsz * H
    _, specs = _chunk_specs(nc, False, bsz, _GDN_COLS)

    def body(q_ref, k_ref, v_ref, og_ref, ab_ref, sc_ref, nw_ref, y_ref, sh_ref, th_ref, s_ref):
        @pl.when(pl.program_id(0) == 0)
        def _():
            s_ref[...] = jnp.zeros_like(s_ref)

        s = s_ref[...]
        for sub in range(SUB):
            rows = _chunk_rows(sub)
            q, k, v, a, bb, og, alog, dtb, nw_ = _gdn_args((q_ref, k_ref, v_ref, og_ref, ab_ref, sc_ref, nw_ref), bsz, rows)
            sh_ref[sub] = s
            y, s, tinv = gdn_chunk(q, k, v, a, bb, og, s, alog, dtb, nw_)
            th_ref[sub] = tinv.astype(MXU)
            for b in range(bsz):
                for h in range(H):
                    y_ref[b, rows, 128 * h:128 * h + 128] = y[H * b + h].astype(MXU)
        s_ref[...] = s

    return pl.pallas_call(
        body, name="gdn_fwd", grid=(nc // SUB,),
        in_specs=specs + [_full((2, 128)), _full((1, 128))],
        out_specs=[pl.BlockSpec((bsz, SUB * CHUNK, 512), lambda n: (0, n, 0)),
                   pl.BlockSpec((SUB, nh, DV, DV), lambda n: (n, 0, 0, 0)),
                   pl.BlockSpec((SUB, nh, CHUNK, CHUNK), lambda n: (n, 0, 0, 0))],
        out_shape=[jax.ShapeDtypeStruct((bsz, t, 512), MXU), jax.ShapeDtypeStruct((nc, nh, DV, DV), F32),
                   jax.ShapeDtypeStruct((nc, nh, CHUNK, CHUNK), MXU)],
        scratch_shapes=[pltpu.VMEM((nh, DV, DV), F32)],
        compiler_params=_params(("arbitrary",)),
    )(conv, conv, conv, pd, pd, sc, nw)


def _gdn_bwd_call(conv, pd, s_hist, t_hist, dyin, sc, nw):
    bsz, t, _ = conv.shape
    nc = t // CHUNK
    nh = bsz * H
    n_of, specs = _chunk_specs(nc, True, bsz, _GDN_COLS, SUB_GDN_BWD)

    def body(q_ref, k_ref, v_ref, og_ref, ab_ref, sh_ref, th_ref, dy_ref, sc_ref, nw_ref,
             dc_ref, dpd_ref, dsc_ref, dnw_ref, ds_ref):
        @pl.when(pl.program_id(0) == 0)
        def _():
            dsc_ref[...] = jnp.zeros_like(dsc_ref)
            dnw_ref[...] = jnp.zeros_like(dnw_ref)
            ds_ref[...] = jnp.zeros_like(ds_ref)

        lane = lax.broadcasted_iota(jnp.int32, (CHUNK, 128), 1)
        ds = ds_ref[...]
        for sub in reversed(range(SUB_GDN_BWD)):
            rows = _chunk_rows(sub)
            q, k, v, a, bb, og, alog, dtb, nw_ = _gdn_args((q_ref, k_ref, v_ref, og_ref, ab_ref, sc_ref, nw_ref), bsz, rows)
            _, vjp = jax.vjp(functools.partial(gdn_chunk, tinv=th_ref[sub]), q, k, v, a, bb, og, sh_ref[sub], alog, dtb, nw_)
            dq, dk, dv, da, db, dog, ds, dalog, ddtb, dnw = vjp((_heads(dy_ref, bsz, rows, 128), ds))
            dnw_ref[...] += dnw
            for b in range(bsz):
                dab = jnp.zeros((CHUNK, 128), F32)
                for h in range(H):
                    i = H * b + h
                    dc_ref[b, rows, 128 * h:128 * h + 128] = dq[i]
                    dc_ref[b, rows, 512 + 128 * h:512 + 128 * h + 128] = dk[i]
                    dc_ref[b, rows, 1024 + 128 * h:1024 + 128 * h + 128] = dv[i]
                    dpd_ref[b, rows, 128 * h:128 * h + 128] = dog[i].astype(MXU)
                    dab = dab + jnp.where(lane == h, da[i], 0.0) + jnp.where(lane == H + h, db[i], 0.0)
                    dsc_ref[0:1, h:h + 1] += dalog[i]
                    dsc_ref[1:2, h:h + 1] += ddtb[i]
                dpd_ref[b, rows, 512:640] = dab.astype(MXU)
        ds_ref[...] = ds

    return pl.pallas_call(
        body, name="gdn_bwd", grid=(nc // SUB_GDN_BWD,),
        in_specs=specs + [pl.BlockSpec((SUB_GDN_BWD, nh, DV, DV), lambda n: (n_of(n), 0, 0, 0)),
                          pl.BlockSpec((SUB_GDN_BWD, nh, CHUNK, CHUNK), lambda n: (n_of(n), 0, 0, 0)),
                          pl.BlockSpec((bsz, SUB_GDN_BWD * CHUNK, 512), lambda n: (0, n_of(n), 1)),
                          _full((2, 128)), _full((1, 128))],
        out_specs=[pl.BlockSpec((bsz, SUB_GDN_BWD * CHUNK, W_GQKV), lambda n: (0, n_of(n), 0)),
                   pl.BlockSpec((bsz, SUB_GDN_BWD * CHUNK, W_GDN), lambda n: (0, n_of(n), 0)),
                   _full((2, 128)), _full((1, 128))],
        out_shape=[jax.ShapeDtypeStruct(conv.shape, F32), jax.ShapeDtypeStruct(pd.shape, MXU),
                   jax.ShapeDtypeStruct((2, 128), F32), jax.ShapeDtypeStruct((1, 128), F32)],
        scratch_shapes=[pltpu.VMEM((nh, DV, DV), F32)],
        compiler_params=_params(("arbitrary",)),
    )(conv, conv, conv, pd, pd, s_hist, t_hist, dyin, sc, nw)


def _head_call(x, ya, yb, wout, mod3, lnw, lnb, tgt):
    bsz, t, _ = x.shape
    tm = min(512, t)
    rows = min(256, tm)

    def body(x_ref, ya_ref, yb_ref, w_ref, mod_ref, lnw_ref, lnb_ref, t_ref,
             dyin_ref, dxa_ref, dgate_ref, dw_ref, dlnw_ref, dlnb_ref, loss_ref):
        b, i = pl.program_id(0), pl.program_id(1)

        @pl.when((b == 0) & (i == 0))
        def _():
            dw_ref[...] = jnp.zeros_like(dw_ref)
            dlnw_ref[...] = jnp.zeros_like(dlnw_ref)
            dlnb_ref[...] = jnp.zeros_like(dlnb_ref)
            loss_ref[...] = jnp.zeros_like(loss_ref)

        @pl.when(i == 0)
        def _():
            dgate_ref[...] = jnp.zeros_like(dgate_ref)

        w = w_ref[...]
        parts = [slice(p * rows, (p + 1) * rows) for p in range(tm // rows)]
        yin = [jnp.concatenate([ya_ref[0, rs, :], yb_ref[0, rs, :]], axis=-1).astype(MXU) for rs in parts]
        y = [jnp.dot(yi, w, preferred_element_type=F32) for yi in yin]
        for rs, yi, y_p in zip(parts, yin, y):
            loss, vjp = jax.vjp(head_fn, x_ref[0, rs, :], y_p, mod_ref[0, 2:3, :], lnw_ref[...], lnb_ref[...], t_ref[0, rs, :])
            dx, dy, dgate, dlnw, dlnb, _ = vjp(jnp.ones((1, 1), F32))
            dyb = dy.astype(MXU)
            dyin_ref[0, rs, :] = lax.dot_general(dyb, w, (((1,), (1,)), ((), ())), preferred_element_type=F32)
            dw_ref[...] += lax.dot_general(yi, dyb, (((0,), (0,)), ((), ())), preferred_element_type=F32)
            dxa_ref[0, rs, :] = dx
            dgate_ref[0] += dgate
            dlnw_ref[...] += dlnw
            dlnb_ref[...] += dlnb
            loss_ref[...] += jnp.broadcast_to(loss, (1, 128))

    tok = lambda w, j=0: pl.BlockSpec((1, tm, w), lambda b, i: (b, i, j))
    row = pl.BlockSpec((1, D), lambda b, i: (0, 0))
    return pl.pallas_call(
        body, name="head", grid=(bsz, t // tm),
        in_specs=[tok(D), tok(512), tok(512), pl.BlockSpec((D, D), lambda b, i: (0, 0)),
                  pl.BlockSpec((1, 3, D), lambda b, i: (b, 0, 0)), row, row, tok(D)],
        out_specs=[tok(D), tok(D), pl.BlockSpec((1, 1, D), lambda b, i: (b, 0, 0)),
                   pl.BlockSpec((D, D), lambda b, i: (0, 0)), row, row, pl.BlockSpec((1, 128), lambda b, i: (0, 0))],
        out_shape=[jax.ShapeDtypeStruct(x.shape, F32), jax.ShapeDtypeStruct(x.shape, F32),
                   jax.ShapeDtypeStruct((bsz, 1, D), F32), jax.ShapeDtypeStruct((D, D), F32),
                   jax.ShapeDtypeStruct((1, D), F32), jax.ShapeDtypeStruct((1, D), F32),
                   jax.ShapeDtypeStruct((1, 128), F32)],
        compiler_params=_params(("arbitrary", "arbitrary")),
    )(x, ya, yb, wout, mod3, lnw, lnb, tgt)


def _dh_call(dpg, dpq, dpd, wpt, x, mod3, dxa, smalls):
    bsz, t, _ = x.shape
    tm = min(512, t)
    assert bsz + 1 <= 8

    def body(dg_ref, dq_ref, dd_ref, w_ref, x_ref, mod_ref, dxa_ref, dgate_ref, dlnw_ref, dlnb_ref, dbg_ref, n1_ref, n2_ref,
             dsc_ref, loss_ref, gx_ref, sp_ref):
        b = pl.program_id(0)

        @pl.when((b == 0) & (pl.program_id(1) == 0))
        def _():
            sp_ref[...] = jnp.zeros_like(sp_ref)
            for e in range(bsz):
                sp_ref[e:e + 1, 2 * D:3 * D] = dgate_ref[e]
            off = 0
            for ref in (dlnw_ref, dlnb_ref, dbg_ref, n1_ref, n2_ref):
                sp_ref[bsz:bsz + 1, off:off + ref.shape[1]] = ref[...]
                off += ref.shape[1]
            sp_ref[bsz:bsz + 1, off:off + 128] = dsc_ref[0:1, :]
            sp_ref[bsz:bsz + 1, off + 128:off + 256] = dsc_ref[1:2, :]
            sp_ref[bsz:bsz + 1, SMALL_W:SMALL_W + 128] = loss_ref[...]

        mm = lambda a, lo, hi: jnp.dot(a.astype(MXU), w_ref[lo:hi, :], preferred_element_type=F32)
        dh = mm(dg_ref[0], 0, W_GLA) + mm(dq_ref[0], W_GLA, W_GLA + W_GQKV) + mm(dd_ref[0], W_GLA + W_GQKV, PW)
        gx_ref[0] = dh * (1.0 + mod_ref[0, 1:2, :]) + dxa_ref[0]
        dshift = jnp.sum(dh, axis=0, keepdims=True)
        dscale = jnp.sum(dh * x_ref[0], axis=0, keepdims=True)
        for e in range(bsz):
            @pl.when(b == e)
            def _():
                sp_ref[e:e + 1, 0:D] += dshift
                sp_ref[e:e + 1, D:2 * D] += dscale

    tok = lambda w: pl.BlockSpec((1, tm, w), lambda b, i: (b, i, 0))
    whole = lambda a: pl.BlockSpec(a.shape, lambda b, i: (0,) * a.ndim)
    return pl.pallas_call(
        body, name="dh", grid=(bsz, t // tm),
        in_specs=[tok(W_GLA), tok(W_GQKV), tok(W_GDN), pl.BlockSpec((PW, D), lambda b, i: (0, 0)), tok(D),
                  pl.BlockSpec((1, 3, D), lambda b, i: (b, 0, 0)), tok(D)] + [whole(a) for a in smalls],
        out_specs=[tok(D), pl.BlockSpec((8, SPW), lambda b, i: (0, 0))],
        out_shape=[jax.ShapeDtypeStruct(x.shape, F32), jax.ShapeDtypeStruct((8, SPW), F32)],
        compiler_params=_params(("arbitrary", "arbitrary")),
    )(dpg, dpq, dpd, wpt, x, mod3, dxa, *smalls)


def _dw_call(x, mod3, dpg, dpq, dpd):
    bsz, t, _ = x.shape
    tm = min(512, t)
    nsteps = bsz * (t // tm)

    def body(x_ref, mod_ref, dg_ref, dq_ref, dd_ref, dw_ref, acc):
        step = pl.program_id(0) * (t // tm) + pl.program_id(1)

        @pl.when(step == 0)
        def _():
            acc[...] = jnp.zeros_like(acc)

        h = (x_ref[0] * (1.0 + mod_ref[0, 1:2, :]) + mod_ref[0, 0:1, :]).astype(MXU)
        for ref, lo, hi in ((dg_ref, 0, W_GLA), (dq_ref, W_GLA, W_GLA + W_GQKV), (dd_ref, W_GLA + W_GQKV, PW)):
            acc[lo:hi, :] += lax.dot_general(ref[0].astype(MXU), h, (((0,), (0,)), ((), ())), preferred_element_type=F32)

        @pl.when(step == nsteps - 1)
        def _():
            dw_ref[...] = acc[...].astype(dw_ref.dtype)

    tok = lambda w: pl.BlockSpec((1, tm, w), lambda b, i: (b, i, 0))
    return pl.pallas_call(
        body, name="dw", grid=(bsz, t // tm),
        in_specs=[tok(D), pl.BlockSpec((1, 3, D), lambda b, i: (b, 0, 0)), tok(W_GLA), tok(W_GQKV), tok(W_GDN)],
        out_specs=pl.BlockSpec((PW, D), lambda b, i: (0, 0)),
        out_shape=jax.ShapeDtypeStruct((PW, D), WIRE),
        scratch_shapes=[pltpu.VMEM((PW, D), F32)],
        compiler_params=_params(("arbitrary", "arbitrary")),
    )(x, mod3, dpg, dpq, dpd)


def _adamw(w, g, m, v):
    m = ADAM_B1 * m + (1.0 - ADAM_B1) * g
    v = ADAM_B2 * v + (1.0 - ADAM_B2) * jnp.square(g)
    m_hat = m / (1.0 - ADAM_B1 ** ADAM_STEP)
    v_hat = v / (1.0 - ADAM_B2 ** ADAM_STEP)
    delta = -ADAM_LR * (m_hat / (jnp.sqrt(v_hat) + ADAM_EPS) + ADAM_WD * w)
    return delta, m, v


def _sum8(ref):
    g = ref[0].astype(F32)
    for j in range(1, NDEV):
        g = g + ref[j].astype(F32)
    return g


def _adam_sum_call(name, g8, w, m, v, cols):
    r, c = w.shape

    def body(g_ref, w_ref, m_ref, v_ref, go_ref, d_ref, mo_ref, vo_ref):
        g = _sum8(g_ref)
        go_ref[...] = g
        d_ref[...], mo_ref[...], vo_ref[...] = _adamw(w_ref[...], g, m_ref[...], v_ref[...])

    blk = pl.BlockSpec((r, cols), lambda i: (0, i))
    return pl.pallas_call(
        body, name=name, grid=(c // cols,),
        in_specs=[pl.BlockSpec((NDEV, r, cols), lambda i: (0, 0, i)), blk, blk, blk],
        out_specs=[blk] * 4, out_shape=[jax.ShapeDtypeStruct((r, c), F32)] * 4,
        compiler_params=_params(("parallel",)),
    )(g8, w, m, v)


def _adam_ada_call(c_all, dmod_cols, w, m, v):
    def body(c_ref, dm_ref, w_ref, m_ref, v_ref, go_ref, d_ref, mo_ref, vo_ref):
        g = lax.dot_general(c_ref[...].astype(MXU), dm_ref[...].astype(MXU), (((0,), (0,)), ((), ())),
                            preferred_element_type=F32)
        go_ref[...] = g
        d_ref[...], mo_ref[...], vo_ref[...] = _adamw(w_ref[...], g, m_ref[...], v_ref[...])

    return pl.pallas_call(
        body, name="adam_ada", out_shape=[jax.ShapeDtypeStruct(w.shape, F32)] * 4, compiler_params=_params(),
    )(c_all, dmod_cols, w, m, v)


_SMALL_AT = dict(ln_w=(0, 1024), ln_b=(1024, 1024), b_gate=(2048, 256), gla_nw=(2304, 128), gdn_nw=(2432, 128),
                 a_log=(2560, 4), dt_bias=(2688, 4))


def _adam_small_call(sp_all, bsz, params):
    names = list(params)

    def body(sp_ref, *refs):
        ins, outs = refs[:3 * len(names)], refs[3 * len(names):]
        dmod_ref, loss_ref, outs = outs[0], outs[1], outs[2:]
        packed = sp_ref[0, bsz:bsz + 1, :]
        for j in range(1, NDEV):
            packed = packed + sp_ref[j, bsz:bsz + 1, :]
        gb = None
        for j in range(NDEV):
            dmod_ref[bsz * j:bsz * j + bsz, :] = sp_ref[j, 0:bsz, :]
            for e in range(bsz):
                gb = sp_ref[j, e:e + 1, :] if gb is None else gb + sp_ref[j, e:e + 1, :]
        loss_ref[...] = packed[:, SMALL_W:SMALL_W + 128]
        for i, name in enumerate(names):
            if name == "b_ada":
                g = gb
            else:
                lo, n = _SMALL_AT[name]
                g = packed[:, lo:lo + n]
            w_ref, m_ref, v_ref = ins[3 * i:3 * i + 3]
            g_ref, d_ref, mo_ref, vo_ref = outs[4 * i:4 * i + 4]
            g_ref[...] = g
            d_ref[...], mo_ref[...], vo_ref[...] = _adamw(w_ref[...], g, m_ref[...], v_ref[...])

    flat = [a for name in names for a in params[name]]
    out_shape = [jax.ShapeDtypeStruct((NDEV * bsz, SPW), F32), jax.ShapeDtypeStruct((1, 128), F32)]
    out_shape += [jax.ShapeDtypeStruct(params[name][0].shape, F32) for name in names for _ in range(4)]
    res = pl.pallas_call(body, name="adam_small", out_shape=out_shape, compiler_params=_params())(sp_all, *flat)
    return res[0], res[1], {name: res[2 + 4 * i:6 + 4 * i] for i, name in enumerate(names)}


def _mesh_pos():
    x, y, c = lax.axis_index("x"), lax.axis_index("y"), lax.axis_index("c")
    return x, y, c, 4 * x + 2 * y + c


def _peer(x, y, c, k):
    px = 1 - x if k & 4 else x
    py = 1 - y if k & 2 else y
    pc = 1 - c if k & 1 else c
    return (px, py, pc), 4 * px + 2 * py + pc


_ANY = pl.BlockSpec(memory_space=pl.ANY)
_VMEM = pl.BlockSpec(memory_space=pltpu.VMEM)


def _gather_call(c8, w_ada, b_sh, w_in_t, conv_w):
    C_SEM, W_SEM, MOD_SEM, CONV_SEM = 0, 1, 2, 3

    def body(c_ref, wada_ref, b_ref, win_ref, cw_ref, wall_ref, call_ref, mod_ref, cwall_ref, modp, send_sems, recv_sems, loc_sem):
        x, y, c, me = _mesh_pos()

        def remote(src, dst, a, k, to):
            return pltpu.make_async_remote_copy(src_ref=src, dst_ref=dst, send_sem=send_sems.at[a, k],
                                                recv_sem=recv_sems.at[a, k], device_id=_peer(x, y, c, to)[0],
                                                device_id_type=pl.DeviceIdType.MESH)

        idx = lambda k: _peer(x, y, c, k)[1]
        sends = []
        call_ref[me] = c_ref[...]
        cwall_ref[me] = cw_ref[...]
        for k in range(1, NDEV):
            sends.append(remote(c_ref, call_ref.at[me], C_SEM, k, k))
            sends[-1].start()
            sends.append(remote(cw_ref, cwall_ref.at[me], CONV_SEM, k, k))
            sends[-1].start()
        local = pltpu.make_async_copy(win_ref, wall_ref.at[me], loc_sem)
        local.start()
        for k in (1, 2, 4, 6):
            sends.append(remote(win_ref, wall_ref.at[me], W_SEM, k, k))
            sends[-1].start()
        for k in range(1, NDEV):
            remote(c_ref, call_ref.at[idx(k)], C_SEM, k, k).wait_recv()
        modp[...] = jnp.dot(call_ref[...].reshape(NDEV * 8, D).astype(MXU), wada_ref[...].astype(MXU),
                            preferred_element_type=F32) + b_ref[...]
        mod_ref[me] = modp[pl.ds(pl.multiple_of(me * 8, 8), 8), :]
        for k in range(1, NDEV):
            sends.append(remote(modp.at[pl.ds(pl.multiple_of(idx(k) * 8, 8), 8), :], mod_ref.at[me], MOD_SEM, k, k))
            sends[-1].start()
        for k in (2, 4, 6):
            remote(win_ref, wall_ref.at[idx(k)], W_SEM, k, k).wait_recv()
            sends.append(remote(wall_ref.at[idx(k)], wall_ref.at[idx(k)], W_SEM, k + 1, 1))
            sends[-1].start()
        for k in (1, 3, 5, 7):
            remote(win_ref, wall_ref.at[idx(k)], W_SEM, k, 1).wait_recv()
        for k in range(1, NDEV):
            remote(modp.at[pl.ds(0, 8), :], mod_ref.at[idx(k)], MOD_SEM, k, k).wait_recv()
            remote(cw_ref, cwall_ref.at[idx(k)], CONV_SEM, k, k).wait_recv()
        for cp in sends:
            cp.wait_send()
        local.wait()

    return pl.pallas_call(
        body, name="gather",
        out_shape=[jax.ShapeDtypeStruct((NDEV,) + w_in_t.shape, w_in_t.dtype), jax.ShapeDtypeStruct((NDEV, 8, D), F32),
                   jax.ShapeDtypeStruct((NDEV, 8, SHARD_ADA), F32), jax.ShapeDtypeStruct((NDEV,) + conv_w.shape, F32)],
        in_specs=[_VMEM, _VMEM, _VMEM, _ANY, _VMEM], out_specs=[_ANY, _VMEM, _VMEM, _VMEM],
        scratch_shapes=[pltpu.VMEM((NDEV * 8, SHARD_ADA), F32), pltpu.SemaphoreType.DMA((4, NDEV)),
                        pltpu.SemaphoreType.DMA((4, NDEV)), pltpu.SemaphoreType.DMA],
        compiler_params=_params(),
    )(c8, w_ada, b_sh, w_in_t, conv_w)


_HBM = pl.BlockSpec(memory_space=pltpu.HBM)
_SEM = pl.BlockSpec(memory_space=pltpu.SEMAPHORE)
_EFFECT = pltpu.SideEffectType.DATAFLOW_SIDE_EFFECTING


def _whole(gather, a):
    return gather[a] if isinstance(gather, (list, tuple)) else gather


def _xchg_start(name, blocks, lands, gather):
    nb = len(blocks)

    def body(*refs):
        srcs, dsts = refs[:nb], refs[nb:2 * nb]
        send_sems, recv_sems = refs[2 * nb], refs[2 * nb + 1]
        token = refs[-1]
        x, y, c, me = _mesh_pos()
        for k in range(1, NDEV):
            dev, pidx = _peer(x, y, c, k)
            for a in range(nb):
                pltpu.make_async_remote_copy(src_ref=srcs[a] if _whole(gather, a) else srcs[a].at[pidx], dst_ref=dsts[a].at[me],
                                             send_sem=send_sems.at[NDEV * a + k], recv_sem=recv_sems.at[NDEV * a + k],
                                             device_id=dev, device_id_type=pl.DeviceIdType.MESH).start()
        token[...] = jnp.zeros_like(token)

    thru = [pltpu.HBM(a.shape, a.dtype) for a in list(blocks) + list(lands)]
    return pl.pallas_call(
        body, name=name,
        out_shape=(pltpu.SemaphoreType.DMA((nb * NDEV,)), pltpu.SemaphoreType.DMA((nb * NDEV,)), *thru,
                   jax.ShapeDtypeStruct((8, 128), F32)),
        in_specs=[_HBM] * (2 * nb), out_specs=(_SEM, _SEM, *([_HBM] * (2 * nb)), _VMEM),
        input_output_aliases={i: 2 + i for i in range(2 * nb)},
        compiler_params=pltpu.CompilerParams(has_side_effects=_EFFECT),
    )(*[pltpu.with_memory_space_constraint(a, pltpu.HBM) for a in list(blocks) + list(lands)])


def _xchg_wait(name, send_sems, recv_sems, thru, after, gather):
    nb = len(thru) // 2

    def body(*refs):
        srcs, dsts = refs[:nb], refs[nb:2 * nb]
        send_sems, recv_sems = refs[2 * nb], refs[2 * nb + 1]
        x, y, c, me = _mesh_pos()
        for k in range(1, NDEV):
            dev, pidx = _peer(x, y, c, k)
            for a in range(nb):
                cp = pltpu.make_async_remote_copy(src_ref=srcs[a] if _whole(gather, a) else srcs[a].at[pidx], dst_ref=dsts[a].at[pidx],
                                                  send_sem=send_sems.at[NDEV * a + k], recv_sem=recv_sems.at[NDEV * a + k],
                                                  device_id=dev, device_id_type=pl.DeviceIdType.MESH)
                cp.wait_send()
                cp.wait_recv()

    out = pl.pallas_call(
        body, name=name, out_shape=tuple(pltpu.HBM(a.shape, a.dtype) for a in thru),
        in_specs=[_HBM] * (2 * nb) + [_SEM, _SEM, pl.BlockSpec(memory_space=pl.ANY)], out_specs=tuple([_HBM] * (2 * nb)),
        input_output_aliases={i: i for i in range(2 * nb)},
        compiler_params=pltpu.CompilerParams(has_side_effects=_EFFECT),
    )(*thru, send_sems, recv_sems, after)
    return out[nb:]


def _pad_cols(a, n):
    return jnp.pad(a, ((0, 0), (0, n - a.shape[1])))


def _assemble_wt(wt_full):
    q, k, v, lr, og, gqkv, ab, dog = jnp.split(wt_full, [256, 512, 1024, 1040, 1552, 3088, 3096], axis=0)
    z = lambda n: jnp.zeros((n, wt_full.shape[1]), wt_full.dtype)
    return jnp.concatenate([q, k, v, og, lr, z(112), gqkv, dog, ab, z(120)], axis=0)


def _disassemble_dwt(dwt):
    g0, q0 = W_GLA, W_GLA + W_GQKV
    return jnp.concatenate([dwt[:1024], dwt[1536:1552], dwt[1024:1536], dwt[g0:q0], dwt[q0 + 512:q0 + 520],
                            dwt[q0:q0 + 512]], axis=0)


def local_grads(x, mod3, wp, conv_w, late_weights, bg, gla_nw, sc, gdn_nw, lnw, lnb, tgt):
    pg, pq, pd, conv = _proj_call(x, mod3, wp, conv_w)
    wout, wgu_p = late_weights(pq)
    ya, s_gla = _gla_fwd_call(pg, wgu_p, bg, gla_nw)
    yb, s_gdn, t_gdn = _gdn_fwd_call(conv, pd, sc, gdn_nw)
    dyin, dxa, dgate, dwout, dlnw, dlnb, loss = _head_call(x, ya, yb, wout, mod3, lnw, lnb, tgt)
    dpg, dwgu, dbg, dnw_gla = _gla_bwd_call(pg, s_gla, dyin, wgu_p, bg, gla_nw)
    dconv, dpd, dsc, dnw_gdn = _gdn_bwd_call(conv, pd, s_gdn, t_gdn, dyin, sc, gdn_nw)
    dpq, dconv_w = _conv_bwd_call(dconv, pq, conv_w)
    dw_in = _disassemble_dwt(_dw_call(x, mod3, dpg, dpq, dpd))
    g = dict(dw_in=dw_in, dwout=dwout, dconv_w=dconv_w[:4], dwgu=dwgu[:16])
    smalls = (dgate, dlnw, dlnb, dbg, dnw_gla, dnw_gdn, dsc, loss)
    return g, lambda mod3_: _dh_call(dpg, dpq, dpd, wp, x, mod3_, dxa, smalls)


def local_step(x, mod3, wp, wout, conv_w, wgu_p, *args):
    g, finish = local_grads(x, mod3, wp, conv_w, lambda _: (wout, wgu_p), *args)
    g["gx"], sp = finish(mod3)
    bsz = x.shape[0]
    g["dmod"] = sp[:bsz].reshape(bsz, 3, D)
    g["loss"] = sp[bsz, SMALL_W]
    for name, (lo, n) in _SMALL_AT.items():
        g[name] = sp[bsz:bsz + 1, lo:lo + n]
    return g


def kernel(x, c, w_ada, b_ada, w_in, gla_w_gate_up, gla_b_gate, gla_norm_w, gdn_conv_w, gdn_a_log, gdn_dt_bias, gdn_norm_w, w_out, ln_w, ln_b, loss_target, m_w_ada, m_b_ada, m_w_in, m_gla_w_gate_up, m_gla_b_gate, m_gla_norm_w, m_gdn_conv_w, m_gdn_a_log, m_gdn_dt_bias, m_gdn_norm_w, m_w_out, m_ln_w, m_ln_b, v_w_ada, v_b_ada, v_w_in, v_gla_w_gate_up, v_gla_b_gate, v_gla_norm_w, v_gdn_conv_w, v_gdn_a_log, v_gdn_dt_bias, v_gdn_norm_w, v_w_out, v_ln_w, v_ln_b):
    me = 4 * lax.axis_index("x") + 2 * lax.axis_index("y") + lax.axis_index("c")
    bsz = x.shape[0]

    b_sh = lax.dynamic_slice(b_ada, (0, me * SHARD_ADA), (1, SHARD_ADA))
    c8 = jnp.pad(c, ((0, 8 - bsz), (0, 0)))
    w_in_t, m_in_t, v_in_t = (jnp.swapaxes(a[0], 0, 1) for a in (w_in, m_w_in, v_w_in))
    win_all, c_all, mod_blk, conv_all = _gather_call(c8, w_ada[0], b_sh, w_in_t.astype(WIRE), gdn_conv_w[0])
    conv_w = jnp.transpose(conv_all, (1, 0, 2)).reshape(4, W_GQKV)
    wp = _assemble_wt(win_all.reshape(IN_COLS, D))
    mod = jnp.transpose(mod_blk[:, :bsz, :], (1, 0, 2)).reshape(bsz, 3 * D)
    mod3 = mod.reshape(bsz, 3, D)
    sc = jnp.concatenate([_pad_cols(gdn_a_log, 128), _pad_cols(gdn_dt_bias, 128)], axis=0)

    own = lambda a: lax.dynamic_update_slice(lax.empty((NDEV,) + a.shape, a.dtype), a[None], (me,) + (0,) * a.ndim)
    late = [w_out[0].astype(WIRE), gla_w_gate_up[0] + 0.0 * mod_blk[0, 0, 0]]
    w_send, w_recv, *w_thru, w_token = _xchg_start("wgather_start", late, [own(a) for a in late], gather=True)

    def late_weights(pq):
        wout_all, wgu_all = _xchg_wait("wgather_wait", w_send, w_recv, w_thru, pq, gather=True)
        return wout_all.reshape(D, D), jnp.pad(jnp.transpose(wgu_all, (1, 0, 2)).reshape(16, 256), ((0, 112), (0, 0)))

    g, finish = local_grads(x, mod3 + w_token[0, 0], wp, conv_w, late_weights, gla_b_gate, gla_norm_w, sc, gdn_norm_w, ln_w, ln_b,
                            loss_target)

    big = [g["dw_in"].reshape(NDEV, SHARD_IN, D), g["dwout"].reshape(NDEV, D // NDEV, D).astype(WIRE)]
    lands = [lax.dynamic_update_slice(lax.empty(a.shape, a.dtype), lax.dynamic_slice(a, (me, 0, 0), (1,) + a.shape[1:]),
                                      (me, 0, 0)) for a in big]
    send_sems, recv_sems, *thru, token = _xchg_start("xchg_start", big, lands, gather=False)
    gx, sp = finish(mod3 + token[0, 0])
    little = [jnp.transpose(g["dconv_w"].reshape(4, NDEV, W_GQKV // NDEV), (1, 0, 2)),
              jnp.transpose(g["dwgu"].reshape(16, NDEV, 256 // NDEV), (1, 0, 2)), sp]
    modes = [False, False, True]
    l_lands = [lax.dynamic_update_slice(lax.empty(a.shape, a.dtype), lax.dynamic_slice(a, (me, 0, 0), (1,) + a.shape[1:]),
                                        (me, 0, 0)) for a in little[:2]] + [own(sp)]
    l_send, l_recv, *l_thru, l_token = _xchg_start("small_start", little, l_lands, gather=modes)
    r_in, r_out = _xchg_wait("xchg_wait", send_sems, recv_sems, thru, l_token, gather=False)

    t_in = [jnp.swapaxes(a, 0, 1) for a in _adam_sum_call("adam_in", r_in, w_in_t, m_in_t, v_in_t, 256)]
    t_out = _adam_sum_call("adam_out", r_out, w_out[0], m_w_out[0], v_w_out[0], D)
    r_conv, r_gu, sp_all = _xchg_wait("small_wait", l_send, l_recv, l_thru, t_out[3], gather=modes)
    t_conv = _adam_sum_call("adam_conv", r_conv, gdn_conv_w[0], m_gdn_conv_w[0], v_gdn_conv_w[0], W_GQKV // NDEV)
    t_gu = _adam_sum_call("adam_gu", r_gu, gla_w_gate_up[0], m_gla_w_gate_up[0], v_gla_w_gate_up[0], 256 // NDEV)
    dmod_all, loss, small = _adam_small_call(sp_all, bsz, dict(
        b_ada=(b_ada, m_b_ada, v_b_ada), ln_w=(ln_w, m_ln_w, v_ln_w), ln_b=(ln_b, m_ln_b, v_ln_b),
        b_gate=(gla_b_gate, m_gla_b_gate, v_gla_b_gate), gla_nw=(gla_norm_w, m_gla_norm_w, v_gla_norm_w),
        gdn_nw=(gdn_norm_w, m_gdn_norm_w, v_gdn_norm_w), a_log=(gdn_a_log, m_gdn_a_log, v_gdn_a_log),
        dt_bias=(gdn_dt_bias, m_gdn_dt_bias, v_gdn_dt_bias)))
    c16 = c_all[:, :bsz, :].reshape(NDEV * bsz, D)
    t_ada = _adam_ada_call(c16, lax.dynamic_slice(dmod_all, (0, me * SHARD_ADA), (NDEV * bsz, SHARD_ADA)),
                           w_ada[0], m_w_ada[0], v_w_ada[0])

    def group(i):
        s = lambda name: small[name][i]
        return [t_ada[i][None], s("b_ada"), t_in[i][None], t_gu[i][None], s("b_gate"), s("gla_nw"), t_conv[i][None],
                s("a_log"), s("dt_bias"), s("gdn_nw"), t_out[i][None], s("ln_w"), s("ln_b")]

    return (loss[0, 0], gx, *group(0), *group(1), *group(2), *group(3))
```

```python
import functools

import jax
import jax.numpy as jnp
from jax import lax
from jax.experimental import pallas as pl
from jax.experimental.pallas import tpu as pltpu

F32 = jnp.float32
MXU = jnp.bfloat16
WIRE = jnp.bfloat16
HI = lax.Precision.HIGH

D = 1024
NDEV = 8
H = 4
GLA_DK = 64
DV = 128
CHUNK = 64
SUB = 8
SUB_GDN_BWD = 4
LN_EPS = 1e-5
RMS_EPS = 1e-6
ALPHA = 2.0 ** 0.25
GATE_NORM = 16.0

W_GLA, W_GQKV, W_GDN = 1664, 1536, 640
PW = W_GLA + W_GQKV + W_GDN
IN_COLS = 3608
SHARD_IN = IN_COLS // NDEV
SHARD_ADA = 3 * D // NDEV
SPW = 3 * D
SMALL_W = 2816

ADAM_LR, ADAM_B1, ADAM_B2, ADAM_EPS, ADAM_WD, ADAM_STEP = 0.001, 0.9, 0.999, 1e-08, 0.01, 10

VMEM_LIMIT = 56 * 1024 * 1024


def _params(sem=None, **kw):
    if sem is not None:
        kw["dimension_semantics"] = sem
    return pltpu.CompilerParams(vmem_limit_bytes=VMEM_LIMIT, **kw)


_MM = (((2,), (1,)), ((0,), (0,)))
_NT = (((2,), (2,)), ((0,), (0,)))
_TN = (((1,), (1,)), ((0,), (0,)))


def _dg(a, b, dims):
    return lax.dot_general(a.astype(MXU), b.astype(MXU), dims, preferred_element_type=F32)


def _hdg(a, b, dims):
    return lax.dot_general(a, b, dims, precision=HI, preferred_element_type=F32)


@jax.custom_vjp
def bmm(a, b):
    return _dg(a, b, _MM)


bmm.defvjp(lambda a, b: (_dg(a, b, _MM), (a, b)), lambda r, g: (_dg(g, r[1], _NT), _dg(r[0], g, _TN)))


@jax.custom_vjp
def bnt(a, b):
    return _dg(a, b, _NT)


bnt.defvjp(lambda a, b: (_dg(a, b, _NT), (a, b)), lambda r, g: (_dg(g, r[1], _MM), _dg(g, r[0], _TN)))


@jax.custom_vjp
def btn(a, b):
    return _dg(a, b, _TN)


btn.defvjp(lambda a, b: (_dg(a, b, _TN), (a, b)), lambda r, g: (_dg(r[1], g, _NT), _dg(r[0], g, _MM)))


def unit_lower_inverse(a):
    n = a.shape[-1]
    r, c = _iotas(n)
    p = -a
    t = (r == c).astype(F32) + p
    for _ in range(5):
        p = _dg(p, p, _MM)
        t = t + _dg(t, p, _MM)
    return t


@jax.custom_vjp
def unit_lower_solve(a, t, r1, r2):
    return _dg(t, r1, _MM), _dg(t, r2, _MM)


def _solve_fwd(a, t, r1, r2):
    s1, s2 = _dg(t, r1, _MM), _dg(t, r2, _MM)
    return (s1, s2), (t, s1, s2)


def _solve_bwd(res, g):
    t, s1, s2 = res
    d1, d2 = _dg(t, g[0], _TN), _dg(t, g[1], _TN)
    return -(_dg(d1, s1, _NT) + _dg(d2, s2, _NT)), jnp.zeros_like(t), d1, d2


unit_lower_solve.defvjp(_solve_fwd, _solve_bwd)


def _iotas(n):
    return lax.broadcasted_iota(jnp.int32, (n, n), 0), lax.broadcasted_iota(jnp.int32, (n, n), 1)


def _col_to_row(col, eye):
    return jnp.sum(jnp.where(eye, col, 0.0), axis=1, keepdims=True)


def _row_to_col(row, eye):
    return jnp.sum(jnp.where(eye, row, 0.0), axis=2, keepdims=True)


def _pick_row(m, i):
    r = lax.broadcasted_iota(jnp.int32, m.shape, 1)
    return jnp.sum(jnp.where(r == i, m, 0.0), axis=1, keepdims=True)


def _rms_gate(o, nw, og):
    on = o * lax.rsqrt(jnp.mean(o * o, axis=-1, keepdims=True) + RMS_EPS) * nw
    return on * jax.nn.silu(og)


def gla_chunk(q, k, v, lr, og, s, wgu, bg, nw):
    n, c, _ = q.shape
    r, cc = _iotas(c)
    causal = r >= cc
    qs = q * (GLA_DK ** -0.5)
    z = bmm(lr, wgu) + bg
    g = jax.nn.log_sigmoid(z) / GATE_NORM
    b = _hdg(jnp.broadcast_to(causal.astype(F32), (n, c, c)), g, _MM)
    bref = _pick_row(b, c // 2 - 1)
    blast = _pick_row(b, c - 1)
    att = jnp.where(causal, bnt(qs * jnp.exp(b - bref), k * jnp.exp(bref - b)), 0.0)
    o = bmm(att, v) + bmm(qs * jnp.exp(b), s)
    rk, ck = _iotas(GLA_DK)
    s_new = _row_to_col(jnp.exp(blast), rk == ck) * s + btn(k * jnp.exp(blast - b), v)
    return _rms_gate(o, nw, og), s_new


def gdn_chunk(cq, ck, cv, a, bb, og, s, alog, dtb, nw, tinv=None):
    c = cq.shape[1]
    r, cc = _iotas(c)
    eye, causal, strict = r == cc, r >= cc, r > cc
    q, k, v = jax.nn.silu(cq), jax.nn.silu(ck), jax.nn.silu(cv)
    q = q * lax.rsqrt(jnp.sum(q * q, axis=-1, keepdims=True) + RMS_EPS) * (DV ** -0.5)
    k = k * lax.rsqrt(jnp.sum(k * k, axis=-1, keepdims=True) + RMS_EPS)
    g = -jnp.exp(alog) * jax.nn.softplus(a + dtb)
    beta = jax.nn.sigmoid(bb)
    d = jnp.sum(jnp.where(causal, _col_to_row(g, eye), 0.0), axis=2, keepdims=True)
    el = jnp.exp(jnp.where(causal, d - _col_to_row(d, eye), -jnp.inf))
    kb = k * beta
    amat = jnp.where(strict, bnt(kb, k) * el, 0.0)
    t = unit_lower_inverse(amat) if tinv is None else tinv
    u, w = unit_lower_solve(amat, t, v * beta, kb * jnp.exp(d))
    qk = jnp.where(causal, bnt(q, k) * el, 0.0)
    dlast = _pick_row(d, c - 1)
    v_new = u - bmm(w, s)
    o = bmm(q * jnp.exp(d), s) + bmm(qk, v_new)
    s_new = jnp.exp(dlast) * s + btn(k * jnp.exp(dlast - d), v_new)
    y = _rms_gate(o, nw, og)
    return (y, s_new, t) if tinv is None else (y, s_new)


def head_fn(x, y, gate, lnw, lnb, tgt):
    u = ALPHA * x + (1.0 + gate) * y
    mu = jnp.mean(u, axis=-1, keepdims=True)
    var = jnp.mean(jnp.square(u - mu), axis=-1, keepdims=True)
    out = (u - mu) * lax.rsqrt(var + LN_EPS) * lnw + lnb
    err = jnp.square(out - tgt)
    return 0.5 * jnp.sum(jnp.mean(err, axis=-1, keepdims=True), axis=0, keepdims=True)


def _proj_call(x, mod3, wpt, conv_w):
    bsz, t, _ = x.shape
    tm = min(512, t)

    def body(x_ref, mod_ref, w_ref, cw_ref, pg_ref, pq_ref, pd_ref, conv_ref, buf):
        i = pl.program_id(1)
        h = (x_ref[0] * (1.0 + mod_ref[0, 1:2, :]) + mod_ref[0, 0:1, :]).astype(MXU)
        nt = lambda lo, hi: lax.dot_general(h, w_ref[lo:hi, :], (((1,), (1,)), ((), ())), preferred_element_type=F32)
        pq = nt(W_GLA, W_GLA + W_GQKV)
        pq_ref[0] = pq
        @pl.when(i == 0)
        def _():
            buf[0:8, :] = jnp.zeros((8, W_GQKV), F32)

        @pl.when(i > 0)
        def _():
            buf[0:8, :] = buf[tm:tm + 8, :]

        buf[8:, :] = pq
        acc = cw_ref[0:1, :] * buf[pl.ds(5, tm), :]
        for k in range(1, 4):
            acc = acc + cw_ref[k:k + 1, :] * buf[pl.ds(5 + k, tm), :]
        conv_ref[0] = acc
        pg_ref[0] = nt(0, W_GLA)
        pd_ref[0] = nt(W_GLA + W_GQKV, PW)

    tok = lambda w: pl.BlockSpec((1, tm, w), lambda b, i: (b, i, 0))
    return pl.pallas_call(
        body, name="proj", grid=(bsz, t // tm),
        in_specs=[tok(D), pl.BlockSpec((1, 3, D), lambda b, i: (b, 0, 0)), pl.BlockSpec((PW, D), lambda b, i: (0, 0)),
                  pl.BlockSpec((4, W_GQKV), lambda b, i: (0, 0))],
        out_specs=[tok(W_GLA), tok(W_GQKV), tok(W_GDN), tok(W_GQKV)],
        out_shape=[jax.ShapeDtypeStruct((bsz, t, w), F32) for w in (W_GLA, W_GQKV, W_GDN, W_GQKV)],
        scratch_shapes=[pltpu.VMEM((tm + 8, W_GQKV), F32)],
        compiler_params=_params(("arbitrary", "arbitrary")),
    )(x, mod3, wpt, conv_w)


def _conv_bwd_call(dconv, pq, conv_w):
    bsz, t, _ = pq.shape
    tt = min(512, t)
    hb = tt // 8
    nt_ = t // tt

    def body(d_ref, dnext_ref, x_ref, w_ref, din_ref, dw_ref, dbuf):
        b, i = pl.program_id(0), pl.program_id(1)

        @pl.when((b == 0) & (i == 0))
        def _():
            dw_ref[...] = jnp.zeros_like(dw_ref)

        dbuf[0:tt, :] = d_ref[0]
        dbuf[tt:, :] = jnp.where(i < nt_ - 1, dnext_ref[0], 0.0)
        rows, unroll = 16, 4
        for j in range(W_GQKV // 128):
            ls = slice(128 * j, 128 * j + 128)
            wj = [jnp.broadcast_to(w_ref[k:k + 1, ls], (rows, 128)) for k in range(4)]

            def trip(r, dw4):
                dw4 = list(dw4)
                for u in range(unroll):
                    base = pl.multiple_of(r * (rows * unroll), rows * unroll) + rows * u
                    win = dbuf[pl.ds(base, rows + 8), ls]
                    xin = x_ref[0, pl.ds(base, rows), ls]
                    acc = None
                    for k in range(4):
                        dsh = win[3 - k:3 - k + rows, :]
                        acc = wj[k] * dsh if acc is None else acc + wj[k] * dsh
                        dw4[k] = dw4[k] + xin * dsh
                    din_ref[0, pl.ds(base, rows), ls] = acc
                return tuple(dw4)

            dw4 = lax.fori_loop(0, tt // (rows * unroll), trip, tuple(jnp.zeros((rows, 128), F32) for _ in range(4)))
            for k in range(4):
                dw_ref[k:k + 1, ls] += jnp.sum(dw4[k], axis=0, keepdims=True)

    tile = pl.BlockSpec((1, tt, W_GQKV), lambda b, i: (b, i, 0))
    return pl.pallas_call(
        body, name="conv_bwd", grid=(bsz, nt_),
        in_specs=[tile, pl.BlockSpec((1, 8, W_GQKV), lambda b, i: (b, jnp.minimum((i + 1) * hb, t // 8 - 1), 0)),
                  tile, pl.BlockSpec((4, W_GQKV), lambda b, i: (0, 0))],
        out_specs=[tile, pl.BlockSpec((8, W_GQKV), lambda b, i: (0, 0))],
        out_shape=[jax.ShapeDtypeStruct(pq.shape, F32), jax.ShapeDtypeStruct((8, W_GQKV), F32)],
        scratch_shapes=[pltpu.VMEM((tt + 8, W_GQKV), F32)],
        compiler_params=_params(("arbitrary", "arbitrary")),
    )(dconv, dconv, pq, conv_w)


def _chunk_specs(nc, rev, bsz, cols, sub=None):
    sub = SUB if sub is None else sub
    steps = nc // sub
    n_of = (lambda n: steps - 1 - n) if rev else (lambda n: n)
    return n_of, [pl.BlockSpec((bsz, sub * CHUNK, w), lambda n, j=j: (0, n_of(n), j)) for w, j in cols]


def _full(shape):
    return pl.BlockSpec(shape, lambda n: (0,) * len(shape))


def _heads(ref, bsz, rows, width, off=0):
    return jnp.stack([ref[b, rows, off + width * h:off + width * (h + 1)] for b in range(bsz) for h in range(H)])


def _chunk_rows(sub):
    return slice(CHUNK * sub, CHUNK * (sub + 1))


def _per_head(ref, bsz, width, rows=slice(None)):
    return jnp.stack([ref[rows, width * h:width * (h + 1)] for _ in range(bsz) for h in range(H)])


_GLA_COLS = [(256, 0), (256, 1), (512, 1), (512, 2), (128, 12)]


def _gla_args(refs, bsz, rows):
    q_ref, k_ref, v_ref, og_ref, lr_ref, wgu_ref, bg_ref, nw_ref = refs
    lr = jnp.stack([lr_ref[b, rows, :] for b in range(bsz) for _ in range(H)])
    return (_heads(q_ref, bsz, rows, 64), _heads(k_ref, bsz, rows, 64), _heads(v_ref, bsz, rows, 128), lr,
            _heads(og_ref, bsz, rows, 128), _per_head(wgu_ref, bsz, 64), _per_head(bg_ref, bsz, 64), nw_ref[...])


def _gla_fwd_call(pg, wgu, bg, nw):
    bsz, t, _ = pg.shape
    nc = t // CHUNK
    nh = bsz * H
    _, specs = _chunk_specs(nc, False, bsz, _GLA_COLS)

    def body(q_ref, k_ref, v_ref, og_ref, lr_ref, wgu_ref, bg_ref, nw_ref, y_ref, sh_ref, s_ref):
        @pl.when(pl.program_id(0) == 0)
        def _():
            s_ref[...] = jnp.zeros_like(s_ref)

        s = s_ref[...]
        for sub in range(SUB):
            rows = _chunk_rows(sub)
            q, k, v, lr, og, w, b_, nw_ = _gla_args((q_ref, k_ref, v_ref, og_ref, lr_ref, wgu_ref, bg_ref, nw_ref), bsz, rows)
            sh_ref[sub] = s
            y, s = gla_chunk(q, k, v, lr, og, s, w, b_, nw_)
            for b in range(bsz):
                for h in range(H):
                    y_ref[b, rows, 128 * h:128 * h + 128] = y[H * b + h].astype(MXU)
        s_ref[...] = s

    return pl.pallas_call(
        body, name="gla_fwd", grid=(nc // SUB,),
        in_specs=specs + [_full((128, 256)), _full((1, 256)), _full((1, 128))],
        out_specs=[pl.BlockSpec((bsz, SUB * CHUNK, 512), lambda n: (0, n, 0)),
                   pl.BlockSpec((SUB, nh, GLA_DK, DV), lambda n: (n, 0, 0, 0))],
        out_shape=[jax.ShapeDtypeStruct((bsz, t, 512), MXU), jax.ShapeDtypeStruct((nc, nh, GLA_DK, DV), F32)],
        scratch_shapes=[pltpu.VMEM((nh, GLA_DK, DV), F32)],
        compiler_params=_params(("arbitrary",)),
    )(pg, pg, pg, pg, pg, wgu, bg, nw)


def _gla_bwd_call(pg, s_hist, dyin, wgu, bg, nw):
    bsz, t, _ = pg.shape
    nc = t // CHUNK
    nh = bsz * H
    n_of, specs = _chunk_specs(nc, True, bsz, _GLA_COLS)

    def body(q_ref, k_ref, v_ref, og_ref, lr_ref, sh_ref, dy_ref, wgu_ref, bg_ref, nw_ref,
             dp_ref, dwgu_ref, dbg_ref, dnw_ref, ds_ref):
        @pl.when(pl.program_id(0) == 0)
        def _():
            dwgu_ref[...] = jnp.zeros_like(dwgu_ref)
            dbg_ref[...] = jnp.zeros_like(dbg_ref)
            dnw_ref[...] = jnp.zeros_like(dnw_ref)
            ds_ref[...] = jnp.zeros_like(ds_ref)

        ds = ds_ref[...]
        for sub in reversed(range(SUB)):
            rows = _chunk_rows(sub)
            q, k, v, lr, og, w, b_, nw_ = _gla_args((q_ref, k_ref, v_ref, og_ref, lr_ref, wgu_ref, bg_ref, nw_ref), bsz, rows)
            _, vjp = jax.vjp(gla_chunk, q, k, v, lr, og, sh_ref[sub], w, b_, nw_)
            dq, dk, dv, dlr, dog, ds, dwgu, dbg, dnw = vjp((_heads(dy_ref, bsz, rows, 128), ds))
            dnw_ref[...] += dnw
            for b in range(bsz):
                for h in range(H):
                    i = H * b + h
                    dp_ref[b, rows, 512 + 128 * h:512 + 128 * h + 128] = dv[i].astype(MXU)
                    dp_ref[b, rows, 1024 + 128 * h:1024 + 128 * h + 128] = dog[i].astype(MXU)
                    dwgu_ref[:, 64 * h:64 * h + 64] += dwgu[i]
                    dbg_ref[:, 64 * h:64 * h + 64] += dbg[i]
                for j in range(H // 2):
                    dp_ref[b, rows, 128 * j:128 * j + 128] = jnp.concatenate(
                        [dq[H * b + 2 * j], dq[H * b + 2 * j + 1]], axis=-1).astype(MXU)
                    dp_ref[b, rows, 256 + 128 * j:256 + 128 * j + 128] = jnp.concatenate(
                        [dk[H * b + 2 * j], dk[H * b + 2 * j + 1]], axis=-1).astype(MXU)
                dp_ref[b, rows, 1536:1664] = (dlr[H * b] + dlr[H * b + 1] + dlr[H * b + 2] + dlr[H * b + 3]).astype(MXU)
        ds_ref[...] = ds

    return pl.pallas_call(
        body, name="gla_bwd", grid=(nc // SUB,),
        in_specs=specs + [pl.BlockSpec((SUB, nh, GLA_DK, DV), lambda n: (n_of(n), 0, 0, 0)),
                          pl.BlockSpec((bsz, SUB * CHUNK, 512), lambda n: (0, n_of(n), 0)),
                          _full((128, 256)), _full((1, 256)), _full((1, 128))],
        out_specs=[pl.BlockSpec((bsz, SUB * CHUNK, W_GLA), lambda n: (0, n_of(n), 0)),
                   _full((128, 256)), _full((1, 256)), _full((1, 128))],
        out_shape=[jax.ShapeDtypeStruct(pg.shape, MXU), jax.ShapeDtypeStruct((128, 256), F32),
                   jax.ShapeDtypeStruct((1, 256), F32), jax.ShapeDtypeStruct((1, 128), F32)],
        scratch_shapes=[pltpu.VMEM((nh, GLA_DK, DV), F32)],
        compiler_params=_params(("arbitrary",)),
    )(pg, pg, pg, pg, pg, s_hist, dyin, wgu, bg, nw)


_GDN_COLS = [(512, 0), (512, 1), (512, 2), (512, 0), (128, 4)]


def _gdn_args(refs, bsz, rows):
    q_ref, k_ref, v_ref, og_ref, ab_ref, sc_ref, nw_ref = refs
    return (_heads(q_ref, bsz, rows, 128), _heads(k_ref, bsz, rows, 128), _heads(v_ref, bsz, rows, 128),
            _heads(ab_ref, bsz, rows, 1), _heads(ab_ref, bsz, rows, 1, off=H), _heads(og_ref, bsz, rows, 128),
            _per_head(sc_ref, bsz, 1, slice(0, 1)), _per_head(sc_ref, bsz, 1, slice(1, 2)), nw_ref[...])


def _gdn_fwd_call(conv, pd, sc, nw):
    bsz, t, _ = conv.shape
    nc = t // CHUNK
    nh = bsz * H
    _, specs = _chunk_specs(nc, False, bsz, _GDN_COLS)

    def body(q_ref, k_ref, v_ref, og_ref, ab_ref, sc_ref, nw_ref, y_ref, sh_ref, th_ref, s_ref):
        @pl.when(pl.program_id(0) == 0)
        def _():
            s_ref[...] = jnp.zeros_like(s_ref)

        s = s_ref[...]
        for sub in range(SUB):
            rows = _chunk_rows(sub)
            q, k, v, a, bb, og, alog, dtb, nw_ = _gdn_args((q_ref, k_ref, v_ref, og_ref, ab_ref, sc_ref, nw_ref), bsz, rows)
            sh_ref[sub] = s
            y, s, tinv = gdn_chunk(q, k, v, a, bb, og, s, alog, dtb, nw_)
            th_ref[sub] = tinv.astype(MXU)
            for b in range(bsz):
                for h in range(H):
                    y_ref[b, rows, 128 * h:128 * h + 128] = y[H * b + h].astype(MXU)
        s_ref[...] = s

    return pl.pallas_call(
        body, name="gdn_fwd", grid=(nc // SUB,),
        in_specs=specs + [_full((2, 128)), _full((1, 128))],
        out_specs=[pl.BlockSpec((bsz, SUB * CHUNK, 512), lambda n: (0, n, 0)),
                   pl.BlockSpec((SUB, nh, DV, DV), lambda n: (n, 0, 0, 0)),
                   pl.BlockSpec((SUB, nh, CHUNK, CHUNK), lambda n: (n, 0, 0, 0))],
        out_shape=[jax.ShapeDtypeStruct((bsz, t, 512), MXU), jax.ShapeDtypeStruct((nc, nh, DV, DV), F32),
                   jax.ShapeDtypeStruct((nc, nh, CHUNK, CHUNK), MXU)],
        scratch_shapes=[pltpu.VMEM((nh, DV, DV), F32)],
        compiler_params=_params(("arbitrary",)),
    )(conv, conv, conv, pd, pd, sc, nw)


def _gdn_bwd_call(conv, pd, s_hist, t_hist, dyin, sc, nw):
    bsz, t, _ = conv.shape
    nc = t // CHUNK
    nh = bsz * H
    n_of, specs = _chunk_specs(nc, True, bsz, _GDN_COLS, SUB_GDN_BWD)

    def body(q_ref, k_ref, v_ref, og_ref, ab_ref, sh_ref, th_ref, dy_ref, sc_ref, nw_ref,
             dc_ref, dpd_ref, dsc_ref, dnw_ref, ds_ref):
        @pl.when(pl.program_id(0) == 0)
        def _():
            dsc_ref[...] = jnp.zeros_like(dsc_ref)
            dnw_ref[...] = jnp.zeros_like(dnw_ref)
            ds_ref[...] = jnp.zeros_like(ds_ref)

        lane = lax.broadcasted_iota(jnp.int32, (CHUNK, 128), 1)
        ds = ds_ref[...]
        for sub in reversed(range(SUB_GDN_BWD)):
            rows = _chunk_rows(sub)
            q, k, v, a, bb, og, alog, dtb, nw_ = _gdn_args((q_ref, k_ref, v_ref, og_ref, ab_ref, sc_ref, nw_ref), bsz, rows)
            _, vjp = jax.vjp(functools.partial(gdn_chunk, tinv=th_ref[sub]), q, k, v, a, bb, og, sh_ref[sub], alog, dtb, nw_)
            dq, dk, dv, da, db, dog, ds, dalog, ddtb, dnw = vjp((_heads(dy_ref, bsz, rows, 128), ds))
            dnw_ref[...] += dnw
            for b in range(bsz):
                dab = jnp.zeros((CHUNK, 128), F32)
                for h in range(H):
                    i = H * b + h
                    dc_ref[b, rows, 128 * h:128 * h + 128] = dq[i]
                    dc_ref[b, rows, 512 + 128 * h:512 + 128 * h + 128] = dk[i]
                    dc_ref[b, rows, 1024 + 128 * h:1024 + 128 * h + 128] = dv[i]
                    dpd_ref[b, rows, 128 * h:128 * h + 128] = dog[i].astype(MXU)
                    dab = dab + jnp.where(lane == h, da[i], 0.0) + jnp.where(lane == H + h, db[i], 0.0)
                    dsc_ref[0:1, h:h + 1] += dalog[i]
                    dsc_ref[1:2, h:h + 1] += ddtb[i]
                dpd_ref[b, rows, 512:640] = dab.astype(MXU)
        ds_ref[...] = ds

    return pl.pallas_call(
        body, name="gdn_bwd", grid=(nc // SUB_GDN_BWD,),
        in_specs=specs + [pl.BlockSpec((SUB_GDN_BWD, nh, DV, DV), lambda n: (n_of(n), 0, 0, 0)),
                          pl.BlockSpec((SUB_GDN_BWD, nh, CHUNK, CHUNK), lambda n: (n_of(n), 0, 0, 0)),
                          pl.BlockSpec((bsz, SUB_GDN_BWD * CHUNK, 512), lambda n: (0, n_of(n), 1)),
                          _full((2, 128)), _full((1, 128))],
        out_specs=[pl.BlockSpec((bsz, SUB_GDN_BWD * CHUNK, W_GQKV), lambda n: (0, n_of(n), 0)),
                   pl.BlockSpec((bsz, SUB_GDN_BWD * CHUNK, W_GDN), lambda n: (0, n_of(n), 0)),
                   _full((2, 128)), _full((1, 128))],
        out_shape=[jax.ShapeDtypeStruct(conv.shape, F32), jax.ShapeDtypeStruct(pd.shape, MXU),
                   jax.ShapeDtypeStruct((2, 128), F32), jax.ShapeDtypeStruct((1, 128), F32)],
        scratch_shapes=[pltpu.VMEM((nh, DV, DV), F32)],
        compiler_params=_params(("arbitrary",)),
    )(conv, conv, conv, pd, pd, s_hist, t_hist, dyin, sc, nw)


def _head_call(x, ya, yb, wout, mod3, lnw, lnb, tgt):
    bsz, t, _ = x.shape
    tm = min(512, t)
    rows = min(256, tm)

    def body(x_ref, ya_ref, yb_ref, w_ref, mod_ref, lnw_ref, lnb_ref, t_ref,
             dyin_ref, dxa_ref, dgate_ref, dw_ref, dlnw_ref, dlnb_ref, loss_ref):
        b, i = pl.program_id(0), pl.program_id(1)

        @pl.when((b == 0) & (i == 0))
        def _():
            dw_ref[...] = jnp.zeros_like(dw_ref)
            dlnw_ref[...] = jnp.zeros_like(dlnw_ref)
            dlnb_ref[...] = jnp.zeros_like(dlnb_ref)
            loss_ref[...] = jnp.zeros_like(loss_ref)

        @pl.when(i == 0)
        def _():
            dgate_ref[...] = jnp.zeros_like(dgate_ref)

        w = w_ref[...]
        parts = [slice(p * rows, (p + 1) * rows) for p in range(tm // rows)]
        yin = [jnp.concatenate([ya_ref[0, rs, :], yb_ref[0, rs, :]], axis=-1).astype(MXU) for rs in parts]
        y = [jnp.dot(yi, w, preferred_element_type=F32) for yi in yin]
        for rs, yi, y_p in zip(parts, yin, y):
            loss, vjp = jax.vjp(head_fn, x_ref[0, rs, :], y_p, mod_ref[0, 2:3, :], lnw_ref[...], lnb_ref[...], t_ref[0, rs, :])
            dx, dy, dgate, dlnw, dlnb, _ = vjp(jnp.ones((1, 1), F32))
            dyb = dy.astype(MXU)
            dyin_ref[0, rs, :] = lax.dot_general(dyb, w, (((1,), (1,)), ((), ())), preferred_element_type=F32)
            dw_ref[...] += lax.dot_general(yi, dyb, (((0,), (0,)), ((), ())), preferred_element_type=F32)
            dxa_ref[0, rs, :] = dx
            dgate_ref[0] += dgate
            dlnw_ref[...] += dlnw
            dlnb_ref[...] += dlnb
            loss_ref[...] += jnp.broadcast_to(loss, (1, 128))

    tok = lambda w, j=0: pl.BlockSpec((1, tm, w), lambda b, i: (b, i, j))
    row = pl.BlockSpec((1, D), lambda b, i: (0, 0))
    return pl.pallas_call(
        body, name="head", grid=(bsz, t // tm),
        in_specs=[tok(D), tok(512), tok(512), pl.BlockSpec((D, D), lambda b, i: (0, 0)),
                  pl.BlockSpec((1, 3, D), lambda b, i: (b, 0, 0)), row, row, tok(D)],
        out_specs=[tok(D), tok(D), pl.BlockSpec((1, 1, D), lambda b, i: (b, 0, 0)),
                   pl.BlockSpec((D, D), lambda b, i: (0, 0)), row, row, pl.BlockSpec((1, 128), lambda b, i: (0, 0))],
        out_shape=[jax.ShapeDtypeStruct(x.shape, F32), jax.ShapeDtypeStruct(x.shape, F32),
                   jax.ShapeDtypeStruct((bsz, 1, D), F32), jax.ShapeDtypeStruct((D, D), F32),
                   jax.ShapeDtypeStruct((1, D), F32), jax.ShapeDtypeStruct((1, D), F32),
                   jax.ShapeDtypeStruct((1, 128), F32)],
        compiler_params=_params(("arbitrary", "arbitrary")),
    )(x, ya, yb, wout, mod3, lnw, lnb, tgt)


def _dh_call(dpg, dpq, dpd, wpt, x, mod3, dxa, smalls):
    bsz, t, _ = x.shape
    tm = min(512, t)
    assert bsz + 1 <= 8

    def body(dg_ref, dq_ref, dd_ref, w_ref, x_ref, mod_ref, dxa_ref, dgate_ref, dlnw_ref, dlnb_ref, dbg_ref, n1_ref, n2_ref,
             dsc_ref, loss_ref, gx_ref, sp_ref):
        b = pl.program_id(0)

        @pl.when((b == 0) & (pl.program_id(1) == 0))
        def _():
            sp_ref[...] = jnp.zeros_like(sp_ref)
            for e in range(bsz):
                sp_ref[e:e + 1, 2 * D:3 * D] = dgate_ref[e]
            off = 0
            for ref in (dlnw_ref, dlnb_ref, dbg_ref, n1_ref, n2_ref):
                sp_ref[bsz:bsz + 1, off:off + ref.shape[1]] = ref[...]
                off += ref.shape[1]
            sp_ref[bsz:bsz + 1, off:off + 128] = dsc_ref[0:1, :]
            sp_ref[bsz:bsz + 1, off + 128:off + 256] = dsc_ref[1:2, :]
            sp_ref[bsz:bsz + 1, SMALL_W:SMALL_W + 128] = loss_ref[...]

        mm = lambda a, lo, hi: jnp.dot(a.astype(MXU), w_ref[lo:hi, :], preferred_element_type=F32)
        dh = mm(dg_ref[0], 0, W_GLA) + mm(dq_ref[0], W_GLA, W_GLA + W_GQKV) + mm(dd_ref[0], W_GLA + W_GQKV, PW)
        gx_ref[0] = dh * (1.0 + mod_ref[0, 1:2, :]) + dxa_ref[0]
        dshift = jnp.sum(dh, axis=0, keepdims=True)
        dscale = jnp.sum(dh * x_ref[0], axis=0, keepdims=True)
        for e in range(bsz):
            @pl.when(b == e)
            def _():
                sp_ref[e:e + 1, 0:D] += dshift
                sp_ref[e:e + 1, D:2 * D] += dscale

    tok = lambda w: pl.BlockSpec((1, tm, w), lambda b, i: (b, i, 0))
    whole = lambda a: pl.BlockSpec(a.shape, lambda b, i: (0,) * a.ndim)
    return pl.pallas_call(
        body, name="dh", grid=(bsz, t // tm),
        in_specs=[tok(W_GLA), tok(W_GQKV), tok(W_GDN), pl.BlockSpec((PW, D), lambda b, i: (0, 0)), tok(D),
                  pl.BlockSpec((1, 3, D), lambda b, i: (b, 0, 0)), tok(D)] + [whole(a) for a in smalls],
        out_specs=[tok(D), pl.BlockSpec((8, SPW), lambda b, i: (0, 0))],
        out_shape=[jax.ShapeDtypeStruct(x.shape, F32), jax.ShapeDtypeStruct((8, SPW), F32)],
        compiler_params=_params(("arbitrary", "arbitrary")),
    )(dpg, dpq, dpd, wpt, x, mod3, dxa, *smalls)


def _dw_call(x, mod3, dpg, dpq, dpd):
    bsz, t, _ = x.shape
    tm = min(512, t)
    nsteps = bsz * (t // tm)

    def body(x_ref, mod_ref, dg_ref, dq_ref, dd_ref, dw_ref, acc):
        step = pl.program_id(0) * (t // tm) + pl.program_id(1)

        @pl.when(step == 0)
        def _():
            acc[...] = jnp.zeros_like(acc)

        h = (x_ref[0] * (1.0 + mod_ref[0, 1:2, :]) + mod_ref[0, 0:1, :]).astype(MXU)
        for ref, lo, hi in ((dg_ref, 0, W_GLA), (dq_ref, W_GLA, W_GLA + W_GQKV), (dd_ref, W_GLA + W_GQKV, PW)):
            acc[lo:hi, :] += lax.dot_general(ref[0].astype(MXU), h, (((0,), (0,)), ((), ())), preferred_element_type=F32)

        @pl.when(step == nsteps - 1)
        def _():
            dw_ref[...] = acc[...].astype(dw_ref.dtype)

    tok = lambda w: pl.BlockSpec((1, tm, w), lambda b, i: (b, i, 0))
    return pl.pallas_call(
        body, name="dw", grid=(bsz, t // tm),
        in_specs=[tok(D), pl.BlockSpec((1, 3, D), lambda b, i: (b, 0, 0)), tok(W_GLA), tok(W_GQKV), tok(W_GDN)],
        out_specs=pl.BlockSpec((PW, D), lambda b, i: (0, 0)),
        out_shape=jax.ShapeDtypeStruct((PW, D), WIRE),
        scratch_shapes=[pltpu.VMEM((PW, D), F32)],
        compiler_params=_params(("arbitrary", "arbitrary")),
    )(x, mod3, dpg, dpq, dpd)


def _adamw(w, g, m, v):
    m = ADAM_B1 * m + (1.0 - ADAM_B1) * g
    v = ADAM_B2 * v + (1.0 - ADAM_B2) * jnp.square(g)
    m_hat = m / (1.0 - ADAM_B1 ** ADAM_STEP)
    v_hat = v / (1.0 - ADAM_B2 ** ADAM_STEP)
    delta = -ADAM_LR * (m_hat / (jnp.sqrt(v_hat) + ADAM_EPS) + ADAM_WD * w)
    return delta, m, v


def _sum8(ref):
    g = ref[0].astype(F32)
    for j in range(1, NDEV):
        g = g + ref[j].astype(F32)
    return g


def _adam_sum_call(name, g8, w, m, v, cols):
    r, c = w.shape

    def body(g_ref, w_ref, m_ref, v_ref, go_ref, d_ref, mo_ref, vo_ref):
        g = _sum8(g_ref)
        go_ref[...] = g
        d_ref[...], mo_ref[...], vo_ref[...] = _adamw(w_ref[...], g, m_ref[...], v_ref[...])

    blk = pl.BlockSpec((r, cols), lambda i: (0, i))
    return pl.pallas_call(
        body, name=name, grid=(c // cols,),
        in_specs=[pl.BlockSpec((NDEV, r, cols), lambda i: (0, 0, i)), blk, blk, blk],
        out_specs=[blk] * 4, out_shape=[jax.ShapeDtypeStruct((r, c), F32)] * 4,
        compiler_params=_params(("parallel",)),
    )(g8, w, m, v)


def _adam_ada_call(c_all, dmod_cols, w, m, v):
    def body(c_ref, dm_ref, w_ref, m_ref, v_ref, go_ref, d_ref, mo_ref, vo_ref):
        g = lax.dot_general(c_ref[...].astype(MXU), dm_ref[...].astype(MXU), (((0,), (0,)), ((), ())),
                            preferred_element_type=F32)
        go_ref[...] = g
        d_ref[...], mo_ref[...], vo_ref[...] = _adamw(w_ref[...], g, m_ref[...], v_ref[...])

    return pl.pallas_call(
        body, name="adam_ada", out_shape=[jax.ShapeDtypeStruct(w.shape, F32)] * 4, compiler_params=_params(),
    )(c_all, dmod_cols, w, m, v)


_SMALL_AT = dict(ln_w=(0, 1024), ln_b=(1024, 1024), b_gate=(2048, 256), gla_nw=(2304, 128), gdn_nw=(2432, 128),
                 a_log=(2560, 4), dt_bias=(2688, 4))


def _adam_small_call(sp_all, bsz, params):
    names = list(params)

    def body(sp_ref, *refs):
        ins, outs = refs[:3 * len(names)], refs[3 * len(names):]
        dmod_ref, loss_ref, outs = outs[0], outs[1], outs[2:]
        packed = sp_ref[0, bsz:bsz + 1, :]
        for j in range(1, NDEV):
            packed = packed + sp_ref[j, bsz:bsz + 1, :]
        gb = None
        for j in range(NDEV):
            dmod_ref[bsz * j:bsz * j + bsz, :] = sp_ref[j, 0:bsz, :]
            for e in range(bsz):
                gb = sp_ref[j, e:e + 1, :] if gb is None else gb + sp_ref[j, e:e + 1, :]
        loss_ref[...] = packed[:, SMALL_W:SMALL_W + 128]
        for i, name in enumerate(names):
            if name == "b_ada":
                g = gb
            else:
                lo, n = _SMALL_AT[name]
                g = packed[:, lo:lo + n]
            w_ref, m_ref, v_ref = ins[3 * i:3 * i + 3]
            g_ref, d_ref, mo_ref, vo_ref = outs[4 * i:4 * i + 4]
            g_ref[...] = g
            d_ref[...], mo_ref[...], vo_ref[...] = _adamw(w_ref[...], g, m_ref[...], v_ref[...])

    flat = [a for name in names for a in params[name]]
    out_shape = [jax.ShapeDtypeStruct((NDEV * bsz, SPW), F32), jax.ShapeDtypeStruct((1, 128), F32)]
    out_shape += [jax.ShapeDtypeStruct(params[name][0].shape, F32) for name in names for _ in range(4)]
    res = pl.pallas_call(body, name="adam_small", out_shape=out_shape, compiler_params=_params())(sp_all, *flat)
    return res[0], res[1], {name: res[2 + 4 * i:6 + 4 * i] for i, name in enumerate(names)}


def _mesh_pos():
    x, y, c = lax.axis_index("x"), lax.axis_index("y"), lax.axis_index("c")
    return x, y, c, 4 * x + 2 * y + c


def _peer(x, y, c, k):
    px = 1 - x if k & 4 else x
    py = 1 - y if k & 2 else y
    pc = 1 - c if k & 1 else c
    return (px, py, pc), 4 * px + 2 * py + pc


_ANY = pl.BlockSpec(memory_space=pl.ANY)
_VMEM = pl.BlockSpec(memory_space=pltpu.VMEM)


def _gather_call(c8, w_ada, b_sh, w_in_t, conv_w):
    C_SEM, W_SEM, MOD_SEM, CONV_SEM = 0, 1, 2, 3

    def body(c_ref, wada_ref, b_ref, win_ref, cw_ref, wall_ref, call_ref, mod_ref, cwall_ref, modp, send_sems, recv_sems, loc_sem):
        x, y, c, me = _mesh_pos()

        def remote(src, dst, a, k, to):
            return pltpu.make_async_remote_copy(src_ref=src, dst_ref=dst, send_sem=send_sems.at[a, k],
                                                recv_sem=recv_sems.at[a, k], device_id=_peer(x, y, c, to)[0],
                                                device_id_type=pl.DeviceIdType.MESH)

        idx = lambda k: _peer(x, y, c, k)[1]
        sends = []
        call_ref[me] = c_ref[...]
        cwall_ref[me] = cw_ref[...]
        for k in range(1, NDEV):
            sends.append(remote(c_ref, call_ref.at[me], C_SEM, k, k))
            sends[-1].start()
            sends.append(remote(cw_ref, cwall_ref.at[me], CONV_SEM, k, k))
            sends[-1].start()
        local = pltpu.make_async_copy(win_ref, wall_ref.at[me], loc_sem)
        local.start()
        for k in (1, 2, 4, 6):
            sends.append(remote(win_ref, wall_ref.at[me], W_SEM, k, k))
            sends[-1].start()
        for k in range(1, NDEV):
            remote(c_ref, call_ref.at[idx(k)], C_SEM, k, k).wait_recv()
        modp[...] = jnp.dot(call_ref[...].reshape(NDEV * 8, D).astype(MXU), wada_ref[...].astype(MXU),
                            preferred_element_type=F32) + b_ref[...]
        mod_ref[me] = modp[pl.ds(pl.multiple_of(me * 8, 8), 8), :]
        for k in range(1, NDEV):
            sends.append(remote(modp.at[pl.ds(pl.multiple_of(idx(k) * 8, 8), 8), :], mod_ref.at[me], MOD_SEM, k, k))
            sends[-1].start()
        for k in (2, 4, 6):
            remote(win_ref, wall_ref.at[idx(k)], W_SEM, k, k).wait_recv()
            sends.append(remote(wall_ref.at[idx(k)], wall_ref.at[idx(k)], W_SEM, k + 1, 1))
            sends[-1].start()
        for k in (1, 3, 5, 7):
            remote(win_ref, wall_ref.at[idx(k)], W_SEM, k, 1).wait_recv()
        for k in range(1, NDEV):
            remote(modp.at[pl.ds(0, 8), :], mod_ref.at[idx(k)], MOD_SEM, k, k).wait_recv()
            remote(cw_ref, cwall_ref.at[idx(k)], CONV_SEM, k, k).wait_recv()
        for cp in sends:
            cp.wait_send()
        local.wait()

    return pl.pallas_call(
        body, name="gather",
        out_shape=[jax.ShapeDtypeStruct((NDEV,) + w_in_t.shape, w_in_t.dtype), jax.ShapeDtypeStruct((NDEV, 8, D), F32),
                   jax.ShapeDtypeStruct((NDEV, 8, SHARD_ADA), F32), jax.ShapeDtypeStruct((NDEV,) + conv_w.shape, F32)],
        in_specs=[_VMEM, _VMEM, _VMEM, _ANY, _VMEM], out_specs=[_ANY, _VMEM, _VMEM, _VMEM],
        scratch_shapes=[pltpu.VMEM((NDEV * 8, SHARD_ADA), F32), pltpu.SemaphoreType.DMA((4, NDEV)),
                        pltpu.SemaphoreType.DMA((4, NDEV)), pltpu.SemaphoreType.DMA],
        compiler_params=_params(),
    )(c8, w_ada, b_sh, w_in_t, conv_w)


_HBM = pl.BlockSpec(memory_space=pltpu.HBM)
_SEM = pl.BlockSpec(memory_space=pltpu.SEMAPHORE)
_EFFECT = pltpu.SideEffectType.DATAFLOW_SIDE_EFFECTING


def _whole(gather, a):
    return gather[a] if isinstance(gather, (list, tuple)) else gather


def _xchg_start(name, blocks, lands, gather):
    nb = len(blocks)

    def body(*refs):
        srcs, dsts = refs[:nb], refs[nb:2 * nb]
        send_sems, recv_sems = refs[2 * nb], refs[2 * nb + 1]
        token = refs[-1]
        x, y, c, me = _mesh_pos()
        for k in range(1, NDEV):
            dev, pidx = _peer(x, y, c, k)
            for a in range(nb):
                pltpu.make_async_remote_copy(src_ref=srcs[a] if _whole(gather, a) else srcs[a].at[pidx], dst_ref=dsts[a].at[me],
                                             send_sem=send_sems.at[NDEV * a + k], recv_sem=recv_sems.at[NDEV * a + k],
                                             device_id=dev, device_id_type=pl.DeviceIdType.MESH).start()
        token[...] = jnp.zeros_like(token)

    thru = [pltpu.HBM(a.shape, a.dtype) for a in list(blocks) + list(lands)]
    return pl.pallas_call(
        body, name=name,
        out_shape=(pltpu.SemaphoreType.DMA((nb * NDEV,)), pltpu.SemaphoreType.DMA((nb * NDEV,)), *thru,
                   jax.ShapeDtypeStruct((8, 128), F32)),
        in_specs=[_HBM] * (2 * nb), out_specs=(_SEM, _SEM, *([_HBM] * (2 * nb)), _VMEM),
        input_output_aliases={i: 2 + i for i in range(2 * nb)},
        compiler_params=pltpu.CompilerParams(has_side_effects=_EFFECT),
    )(*[pltpu.with_memory_space_constraint(a, pltpu.HBM) for a in list(blocks) + list(lands)])


def _xchg_wait(name, send_sems, recv_sems, thru, after, gather):
    nb = len(thru) // 2

    def body(*refs):
        srcs, dsts = refs[:nb], refs[nb:2 * nb]
        send_sems, recv_sems = refs[2 * nb], refs[2 * nb + 1]
        x, y, c, me = _mesh_pos()
        for k in range(1, NDEV):
            dev, pidx = _peer(x, y, c, k)
            for a in range(nb):
                cp = pltpu.make_async_remote_copy(src_ref=srcs[a] if _whole(gather, a) else srcs[a].at[pidx], dst_ref=dsts[a].at[pidx],
                                                  send_sem=send_sems.at[NDEV * a + k], recv_sem=recv_sems.at[NDEV * a + k],
                                                  device_id=dev, device_id_type=pl.DeviceIdType.MESH)
                cp.wait_send()
                cp.wait_recv()

    out = pl.pallas_call(
        body, name=name, out_shape=tuple(pltpu.HBM(a.shape, a.dtype) for a in thru),
        in_specs=[_HBM] * (2 * nb) + [_SEM, _SEM, pl.BlockSpec(memory_space=pl.ANY)], out_specs=tuple([_HBM] * (2 * nb)),
        input_output_aliases={i: i for i in range(2 * nb)},
        compiler_params=pltpu.CompilerParams(has_side_effects=_EFFECT),
    )(*thru, send_sems, recv_sems, after)
    return out[nb:]


def _pad_cols(a, n):
    return jnp.pad(a, ((0, 0), (0, n - a.shape[1])))


def _assemble_wt(wt_full):
    q, k, v, lr, og, gqkv, ab, dog = jnp.split(wt_full, [256, 512, 1024, 1040, 1552, 3088, 3096], axis=0)
    z = lambda n: jnp.zeros((n, wt_full.shape[1]), wt_full.dtype)
    return jnp.concatenate([q, k, v, og, lr, z(112), gqkv, dog, ab, z(120)], axis=0)


def _disassemble_dwt(dwt):
    g0, q0 = W_GLA, W_GLA + W_GQKV
    return jnp.concatenate([dwt[:1024], dwt[1536:1552], dwt[1024:1536], dwt[g0:q0], dwt[q0 + 512:q0 + 520],
                            dwt[q0:q0 + 512]], axis=0)


def local_grads(x, mod3, wp, conv_w, late_weights, bg, gla_nw, sc, gdn_nw, lnw, lnb, tgt):
    pg, pq, pd, conv = _proj_call(x, mod3, wp, conv_w)
    wout, wgu_p = late_weights(pq)
    ya, s_gla = _gla_fwd_call(pg, wgu_p, bg, gla_nw)
    yb, s_gdn, t_gdn = _gdn_fwd_call(conv, pd, sc, gdn_nw)
    dyin, dxa, dgate, dwout, dlnw, dlnb, loss = _head_call(x, ya, yb, wout, mod3, lnw, lnb, tgt)
    dpg, dwgu, dbg, dnw_gla = _gla_bwd_call(pg, s_gla, dyin, wgu_p, bg, gla_nw)
    dconv, dpd, dsc, dnw_gdn = _gdn_bwd_call(conv, pd, s_gdn, t_gdn, dyin, sc, gdn_nw)
    dpq, dconv_w = _conv_bwd_call(dconv, pq, conv_w)
    dw_in = _disassemble_dwt(_dw_call(x, mod3, dpg, dpq, dpd))
    g = dict(dw_in=dw_in, dwout=dwout, dconv_w=dconv_w[:4], dwgu=dwgu[:16])
    smalls = (dgate, dlnw, dlnb, dbg, dnw_gla, dnw_gdn, dsc, loss)
    return g, lambda mod3_: _dh_call(dpg, dpq, dpd, wp, x, mod3_, dxa, smalls)


def local_step(x, mod3, wp, wout, conv_w, wgu_p, *args):
    g, finish = local_grads(x, mod3, wp, conv_w, lambda _: (wout, wgu_p), *args)
    g["gx"], sp = finish(mod3)
    bsz = x.shape[0]
    g["dmod"] = sp[:bsz].reshape(bsz, 3, D)
    g["loss"] = sp[bsz, SMALL_W]
    for name, (lo, n) in _SMALL_AT.items():
        g[name] = sp[bsz:bsz + 1, lo:lo + n]
    return g


def kernel(x, c, w_ada, b_ada, w_in, gla_w_gate_up, gla_b_gate, gla_norm_w, gdn_conv_w, gdn_a_log, gdn_dt_bias, gdn_norm_w, w_out, ln_w, ln_b, loss_target, m_w_ada, m_b_ada, m_w_in, m_gla_w_gate_up, m_gla_b_gate, m_gla_norm_w, m_gdn_conv_w, m_gdn_a_log, m_gdn_dt_bias, m_gdn_norm_w, m_w_out, m_ln_w, m_ln_b, v_w_ada, v_b_ada, v_w_in, v_gla_w_gate_up, v_gla_b_gate, v_gla_norm_w, v_gdn_conv_w, v_gdn_a_log, v_gdn_dt_bias, v_gdn_norm_w, v_w_out, v_ln_w, v_ln_b):
    me = 4 * lax.axis_index("x") + 2 * lax.axis_index("y") + lax.axis_index("c")
    bsz = x.shape[0]

    b_sh = lax.dynamic_slice(b_ada, (0, me * SHARD_ADA), (1, SHARD_ADA))
    c8 = jnp.pad(c, ((0, 8 - bsz), (0, 0)))
    w_in_t, m_in_t, v_in_t = (jnp.swapaxes(a[0], 0, 1) for a in (w_in, m_w_in, v_w_in))
    win_all, c_all, mod_blk, conv_all = _gather_call(c8, w_ada[0], b_sh, w_in_t.astype(WIRE), gdn_conv_w[0])
    conv_w = jnp.transpose(conv_all, (1, 0, 2)).reshape(4, W_GQKV)
    wp = _assemble_wt(win_all.reshape(IN_COLS, D))
    mod = jnp.transpose(mod_blk[:, :bsz, :], (1, 0, 2)).reshape(bsz, 3 * D)
    mod3 = mod.reshape(bsz, 3, D)
    sc = jnp.concatenate([_pad_cols(gdn_a_log, 128), _pad_cols(gdn_dt_bias, 128)], axis=0)

    own = lambda a: lax.dynamic_update_slice(lax.empty((NDEV,) + a.shape, a.dtype), a[None], (me,) + (0,) * a.ndim)
    late = [w_out[0].astype(WIRE), gla_w_gate_up[0] + 0.0 * mod_blk[0, 0, 0]]
    w_send, w_recv, *w_thru, w_token = _xchg_start("wgather_start", late, [own(a) for a in late], gather=True)

    def late_weights(pq):
        wout_all, wgu_all = _xchg_wait("wgather_wait", w_send, w_recv, w_thru, pq, gather=True)
        return wout_all.reshape(D, D), jnp.pad(jnp.transpose(wgu_all, (1, 0, 2)).reshape(16, 256), ((0, 112), (0, 0)))

    g, finish = local_grads(x, mod3 + w_token[0, 0], wp, conv_w, late_weights, gla_b_gate, gla_norm_w, sc, gdn_norm_w, ln_w, ln_b,
                            loss_target)

    big = [g["dw_in"].reshape(NDEV, SHARD_IN, D), g["dwout"].reshape(NDEV, D // NDEV, D).astype(WIRE)]
    lands = [lax.dynamic_update_slice(lax.empty(a.shape, a.dtype), lax.dynamic_slice(a, (me, 0, 0), (1,) + a.shape[1:]),
                                      (me, 0, 0)) for a in big]
    send_sems, recv_sems, *thru, token = _xchg_start("xchg_start", big, lands, gather=False)
    gx, sp = finish(mod3 + token[0, 0])
    little = [jnp.transpose(g["dconv_w"].reshape(4, NDEV, W_GQKV // NDEV), (1, 0, 2)),
              jnp.transpose(g["dwgu"].reshape(16, NDEV, 256 // NDEV), (1, 0, 2)), sp]
    modes = [False, False, True]
    l_lands = [lax.dynamic_update_slice(lax.empty(a.shape, a.dtype), lax.dynamic_slice(a, (me, 0, 0), (1,) + a.shape[1:]),
                                        (me, 0, 0)) for a in little[:2]] + [own(sp)]
    l_send, l_recv, *l_thru, l_token = _xchg_start("small_start", little, l_lands, gather=modes)
    r_in, r_out = _xchg_wait("xchg_wait", send_sems, recv_sems, thru, l_token, gather=False)

    t_in = [jnp.swapaxes(a, 0, 1) for a in _adam_sum_call("adam_in", r_in, w_in_t, m_in_t, v_in_t, 256)]
    t_out = _adam_sum_call("adam_out", r_out, w_out[0], m_w_out[0], v_w_out[0], D)
    r_conv, r_gu, sp_all = _xchg_wait("small_wait", l_send, l_recv, l_thru, t_out[3], gather=modes)
    t_conv = _adam_sum_call("adam_conv", r_conv, gdn_conv_w[0], m_gdn_conv_w[0], v_gdn_conv_w[0], W_GQKV // NDEV)
    t_gu = _adam_sum_call("adam_gu", r_gu, gla_w_gate_up[0], m_gla_w_gate_up[0], v_gla_w_gate_up[0], 256 // NDEV)
    dmod_all, loss, small = _adam_small_call(sp_all, bsz, dict(
        b_ada=(b_ada, m_b_ada, v_b_ada), ln_w=(ln_w, m_ln_w, v_ln_w), ln_b=(ln_b, m_ln_b, v_ln_b),
        b_gate=(gla_b_gate, m_gla_b_gate, v_gla_b_gate), gla_nw=(gla_norm_w, m_gla_norm_w, v_gla_norm_w),
        gdn_nw=(gdn_norm_w, m_gdn_norm_w, v_gdn_norm_w), a_log=(gdn_a_log, m_gdn_a_log, v_gdn_a_log),
        dt_bias=(gdn_dt_bias, m_gdn_dt_bias, v_gdn_dt_bias)))
    c16 = c_all[:, :bsz, :].reshape(NDEV * bsz, D)
    t_ada = _adam_ada_call(c16, lax.dynamic_slice(dmod_all, (0, me * SHARD_ADA), (NDEV * bsz, SHARD_ADA)),
                           w_ada[0], m_w_ada[0], v_w_ada[0])

    def group(i):
        s = lambda name: small[name][i]
        return [t_ada[i][None], s("b_ada"), t_in[i][None], t_gu[i][None], s("b_gate"), s("gla_nw"), t_conv[i][None],
                s("a_log"), s("dt_bias"), s("gdn_nw"), t_out[i][None], s("ln_w"), s("ln_b")]

    return (loss[0, 0], gx, *group(0), *group(1), *group(2), *group(3))
```

```python
import functools

import jax
import jax.numpy as jnp
from jax import lax
from jax.experimental import pallas as pl
from jax.experimental.pallas import tpu as pltpu

F32 = jnp.float32
MXU = jnp.bfloat16
WIRE = jnp.bfloat16
HI = lax.Precision.HIGH

D = 1024
NDEV = 8
H = 4
GLA_DK = 64
DV = 128
CHUNK = 64
SUB = 8
SUB_GDN_BWD = 4
LN_EPS = 1e-5
RMS_EPS = 1e-6
ALPHA = 2.0 ** 0.25
GATE_NORM = 16.0

W_GLA, W_GQKV, W_GDN = 1664, 1536, 640
PW = W_GLA + W_GQKV + W_GDN
IN_COLS = 3608
SHARD_IN = IN_COLS // NDEV
SHARD_ADA = 3 * D // NDEV
SPW = 3 * D
SMALL_W = 2816

ADAM_LR, ADAM_B1, ADAM_B2, ADAM_EPS, ADAM_WD, ADAM_STEP = 0.001, 0.9, 0.999, 1e-08, 0.01, 10

VMEM_LIMIT = 56 * 1024 * 1024


def _params(sem=None, **kw):
    if sem is not None:
        kw["dimension_semantics"] = sem
    return pltpu.CompilerParams(vmem_limit_bytes=VMEM_LIMIT, **kw)


_MM = (((2,), (1,)), ((0,), (0,)))
_NT = (((2,), (2,)), ((0,), (0,)))
_TN = (((1,), (1,)), ((0,), (0,)))


def _dg(a, b, dims):
    return lax.dot_general(a.astype(MXU), b.astype(MXU), dims, preferred_element_type=F32)


def _hdg(a, b, dims):
    return lax.dot_general(a, b, dims, precision=HI, preferred_element_type=F32)


@jax.custom_vjp
def bmm(a, b):
    return _dg(a, b, _MM)


bmm.defvjp(lambda a, b: (_dg(a, b, _MM), (a, b)), lambda r, g: (_dg(g, r[1], _NT), _dg(r[0], g, _TN)))


@jax.custom_vjp
def bnt(a, b):
    return _dg(a, b, _NT)


bnt.defvjp(lambda a, b: (_dg(a, b, _NT), (a, b)), lambda r, g: (_dg(g, r[1], _MM), _dg(g, r[0], _TN)))


@jax.custom_vjp
def btn(a, b):
    return _dg(a, b, _TN)


btn.defvjp(lambda a, b: (_dg(a, b, _TN), (a, b)), lambda r, g: (_dg(r[1], g, _NT), _dg(r[0], g, _MM)))


def unit_lower_inverse(a):
    n = a.shape[-1]
    r, c = _iotas(n)
    p = -a
    t = (r == c).astype(F32) + p
    for _ in range(5):
        p = _dg(p, p, _MM)
        t = t + _dg(t, p, _MM)
    return t


@jax.custom_vjp
def unit_lower_solve(a, t, r1, r2):
    return _dg(t, r1, _MM), _dg(t, r2, _MM)


def _solve_fwd(a, t, r1, r2):
    s1, s2 = _dg(t, r1, _MM), _dg(t, r2, _MM)
    return (s1, s2), (t, s1, s2)


def _solve_bwd(res, g):
    t, s1, s2 = res
    d1, d2 = _dg(t, g[0], _TN), _dg(t, g[1], _TN)
    return -(_dg(d1, s1, _NT) + _dg(d2, s2, _NT)), jnp.zeros_like(t), d1, d2


unit_lower_solve.defvjp(_solve_fwd, _solve_bwd)


def _iotas(n):
    return lax.broadcasted_iota(jnp.int32, (n, n), 0), lax.broadcasted_iota(jnp.int32, (n, n), 1)


def _col_to_row(col, eye):
    return jnp.sum(jnp.where(eye, col, 0.0), axis=1, keepdims=True)


def _row_to_col(row, eye):
    return jnp.sum(jnp.where(eye, row, 0.0), axis=2, keepdims=True)


def _pick_row(m, i):
    r = lax.broadcasted_iota(jnp.int32, m.shape, 1)
    return jnp.sum(jnp.where(r == i, m, 0.0), axis=1, keepdims=True)


def _rms_gate(o, nw, og):
    on = o * lax.rsqrt(jnp.mean(o * o, axis=-1, keepdims=True) + RMS_EPS) * nw
    return on * jax.nn.silu(og)


def gla_chunk(q, k, v, lr, og, s, wgu, bg, nw):
    n, c, _ = q.shape
    r, cc = _iotas(c)
    causal = r >= cc
    qs = q * (GLA_DK ** -0.5)
    z = bmm(lr, wgu) + bg
    g = jax.nn.log_sigmoid(z) / GATE_NORM
    b = _hdg(jnp.broadcast_to(causal.astype(F32), (n, c, c)), g, _MM)
    bref = _pick_row(b, c // 2 - 1)
    blast = _pick_row(b, c - 1)
    att = jnp.where(causal, bnt(qs * jnp.exp(b - bref), k * jnp.exp(bref - b)), 0.0)
    o = bmm(att, v) + bmm(qs * jnp.exp(b), s)
    rk, ck = _iotas(GLA_DK)
    s_new = _row_to_col(jnp.exp(blast), rk == ck) * s + btn(k * jnp.exp(blast - b), v)
    return _rms_gate(o, nw, og), s_new


def gdn_chunk(cq, ck, cv, a, bb, og, s, alog, dtb, nw, tinv=None):
    c = cq.shape[1]
    r, cc = _iotas(c)
    eye, causal, strict = r == cc, r >= cc, r > cc
    q, k, v = jax.nn.silu(cq), jax.nn.silu(ck), jax.nn.silu(cv)
    q = q * lax.rsqrt(jnp.sum(q * q, axis=-1, keepdims=True) + RMS_EPS) * (DV ** -0.5)
    k = k * lax.rsqrt(jnp.sum(k * k, axis=-1, keepdims=True) + RMS_EPS)
    g = -jnp.exp(alog) * jax.nn.softplus(a + dtb)
    beta = jax.nn.sigmoid(bb)
    d = jnp.sum(jnp.where(causal, _col_to_row(g, eye), 0.0), axis=2, keepdims=True)
    el = jnp.exp(jnp.where(causal, d - _col_to_row(d, eye), -jnp.inf))
    kb = k * beta
    amat = jnp.where(strict, bnt(kb, k) * el, 0.0)
    t = unit_lower_inverse(amat) if tinv is None else tinv
    u, w = unit_lower_solve(amat, t, v * beta, kb * jnp.exp(d))
    qk = jnp.where(causal, bnt(q, k) * el, 0.0)
    dlast = _pick_row(d, c - 1)
    v_new = u - bmm(w, s)
    o = bmm(q * jnp.exp(d), s) + bmm(qk, v_new)
    s_new = jnp.exp(dlast) * s + btn(k * jnp.exp(dlast - d), v_new)
    y = _rms_gate(o, nw, og)
    return (y, s_new, t) if tinv is None else (y, s_new)


def head_fn(x, y, gate, lnw, lnb, tgt):
    u = ALPHA * x + (1.0 + gate) * y
    mu = jnp.mean(u, axis=-1, keepdims=True)
    var = jnp.mean(jnp.square(u - mu), axis=-1, keepdims=True)
    out = (u - mu) * lax.rsqrt(var + LN_EPS) * lnw + lnb
    err = jnp.square(out - tgt)
    return 0.5 * jnp.sum(jnp.mean(err, axis=-1, keepdims=True), axis=0, keepdims=True)


def _proj_call(x, mod3, wpt, conv_w):
    bsz, t, _ = x.shape
    tm = min(512, t)

    def body(x_ref, mod_ref, w_ref, cw_ref, pg_ref, pq_ref, pd_ref, conv_ref, buf):
        i = pl.program_id(1)
        h = (x_ref[0] * (1.0 + mod_ref[0, 1:2, :]) + mod_ref[0, 0:1, :]).astype(MXU)
        nt = lambda lo, hi: lax.dot_general(h, w_ref[lo:hi, :], (((1,), (1,)), ((), ())), preferred_element_type=F32)
        pq = nt(W_GLA, W_GLA + W_GQKV)
        pq_ref[0] = pq
        @pl.when(i == 0)
        def _():
            buf[0:8, :] = jnp.zeros((8, W_GQKV), F32)

        @pl.when(i > 0)
        def _():
            buf[0:8, :] = buf[tm:tm + 8, :]

        buf[8:, :] = pq
        acc = cw_ref[0:1, :] * buf[pl.ds(5, tm), :]
        for k in range(1, 4):
            acc = acc + cw_ref[k:k + 1, :] * buf[pl.ds(5 + k, tm), :]
        conv_ref[0] = acc
        pg_ref[0] = nt(0, W_GLA)
        pd_ref[0] = nt(W_GLA + W_GQKV, PW)

    tok = lambda w: pl.BlockSpec((1, tm, w), lambda b, i: (b, i, 0))
    return pl.pallas_call(
        body, name="proj", grid=(bsz, t // tm),
        in_specs=[tok(D), pl.BlockSpec((1, 3, D), lambda b, i: (b, 0, 0)), pl.BlockSpec((PW, D), lambda b, i: (0, 0)),
                  pl.BlockSpec((4, W_GQKV), lambda b, i: (0, 0))],
        out_specs=[tok(W_GLA), tok(W_GQKV), tok(W_GDN), tok(W_GQKV)],
        out_shape=[jax.ShapeDtypeStruct((bsz, t, w), F32) for w in (W_GLA, W_GQKV, W_GDN, W_GQKV)],
        scratch_shapes=[pltpu.VMEM((tm + 8, W_GQKV), F32)],
        compiler_params=_params(("arbitrary", "arbitrary")),
    )(x, mod3, wpt, conv_w)


def _conv_bwd_call(dconv, pq, conv_w):
    bsz, t, _ = pq.shape
    tt = min(512, t)
    hb = tt // 8
    nt_ = t // tt

    def body(d_ref, dnext_ref, x_ref, w_ref, din_ref, dw_ref, dbuf):
        b, i = pl.program_id(0), pl.program_id(1)

        @pl.when((b == 0) & (i == 0))
        def _():
            dw_ref[...] = jnp.zeros_like(dw_ref)

        dbuf[0:tt, :] = d_ref[0]
        dbuf[tt:, :] = jnp.where(i < nt_ - 1, dnext_ref[0], 0.0)
        rows, unroll = 16, 4
        for j in range(W_GQKV // 128):
            ls = slice(128 * j, 128 * j + 128)
            wj = [jnp.broadcast_to(w_ref[k:k + 1, ls], (rows, 128)) for k in range(4)]

            def trip(r, dw4):
                dw4 = list(dw4)
                for u in range(unroll):
                    base = pl.multiple_of(r * (rows * unroll), rows * unroll) + rows * u
                    win = dbuf[pl.ds(base, rows + 8), ls]
                    xin = x_ref[0, pl.ds(base, rows), ls]
                    acc = None
                    for k in range(4):
                        dsh = win[3 - k:3 - k + rows, :]
                        acc = wj[k] * dsh if acc is None else acc + wj[k] * dsh
                        dw4[k] = dw4[k] + xin * dsh
                    din_ref[0, pl.ds(base, rows), ls] = acc
                return tuple(dw4)

            dw4 = lax.fori_loop(0, tt // (rows * unroll), trip, tuple(jnp.zeros((rows, 128), F32) for _ in range(4)))
            for k in range(4):
                dw_ref[k:k + 1, ls] += jnp.sum(dw4[k], axis=0, keepdims=True)

    tile = pl.BlockSpec((1, tt, W_GQKV), lambda b, i: (b, i, 0))
    return pl.pallas_call(
        body, name="conv_bwd", grid=(bsz, nt_),
        in_specs=[tile, pl.BlockSpec((1, 8, W_GQKV), lambda b, i: (b, jnp.minimum((i + 1) * hb, t // 8 - 1), 0)),
                  tile, pl.BlockSpec((4, W_GQKV), lambda b, i: (0, 0))],
        out_specs=[tile, pl.BlockSpec((8, W_GQKV), lambda b, i: (0, 0))],
        out_shape=[jax.ShapeDtypeStruct(pq.shape, F32), jax.ShapeDtypeStruct((8, W_GQKV), F32)],
        scratch_shapes=[pltpu.VMEM((tt + 8, W_GQKV), F32)],
        compiler_params=_params(("arbitrary", "arbitrary")),
    )(dconv, dconv, pq, conv_w)


def _chunk_specs(nc, rev, bsz, cols, sub=None):
    sub = SUB if sub is None else sub
    steps = nc // sub
    n_of = (lambda n: steps - 1 - n) if rev else (lambda n: n)
    return n_of, [pl.BlockSpec((bsz, sub * CHUNK, w), lambda n, j=j: (0, n_of(n), j)) for w, j in cols]


def _full(shape):
    return pl.BlockSpec(shape, lambda n: (0,) * len(shape))


def _heads(ref, bsz, rows, width, off=0):
    return jnp.stack([ref[b, rows, off + width * h:off + width * (h + 1)] for b in range(bsz) for h in range(H)])


def _chunk_rows(sub):
    return slice(CHUNK * sub, CHUNK * (sub + 1))


def _per_head(ref, bsz, width, rows=slice(None)):
    return jnp.stack([ref[rows, width * h:width * (h + 1)] for _ in range(bsz) for h in range(H)])


_GLA_COLS = [(256, 0), (256, 1), (512, 1), (512, 2), (128, 12)]


def _gla_args(refs, bsz, rows):
    q_ref, k_ref, v_ref, og_ref, lr_ref, wgu_ref, bg_ref, nw_ref = refs
    lr = jnp.stack([lr_ref[b, rows, :] for b in range(bsz) for _ in range(H)])
    return (_heads(q_ref, bsz, rows, 64), _heads(k_ref, bsz, rows, 64), _heads(v_ref, bsz, rows, 128), lr,
            _heads(og_ref, bsz, rows, 128), _per_head(wgu_ref, bsz, 64), _per_head(bg_ref, bsz, 64), nw_ref[...])


def _gla_fwd_call(pg, wgu, bg, nw):
    bsz, t, _ = pg.shape
    nc = t // CHUNK
    nh = bsz * H
    _, specs = _chunk_specs(nc, False, bsz, _GLA_COLS)

    def body(q_ref, k_ref, v_ref, og_ref, lr_ref, wgu_ref, bg_ref, nw_ref, y_ref, sh_ref, s_ref):
        @pl.when(pl.program_id(0) == 0)
        def _():
            s_ref[...] = jnp.zeros_like(s_ref)

        s = s_ref[...]
        for sub in range(SUB):
            rows = _chunk_rows(sub)
            q, k, v, lr, og, w, b_, nw_ = _gla_args((q_ref, k_ref, v_ref, og_ref, lr_ref, wgu_ref, bg_ref, nw_ref), bsz, rows)
            sh_ref[sub] = s
            y, s = gla_chunk(q, k, v, lr, og, s, w, b_, nw_)
            for b in range(bsz):
                for h in range(H):
                    y_ref[b, rows, 128 * h:128 * h + 128] = y[H * b + h].astype(MXU)
        s_ref[...] = s

    return pl.pallas_call(
        body, name="gla_fwd", grid=(nc // SUB,),
        in_specs=specs + [_full((128, 256)), _full((1, 256)), _full((1, 128))],
        out_specs=[pl.BlockSpec((bsz, SUB * CHUNK, 512), lambda n: (0, n, 0)),
                   pl.BlockSpec((SUB, nh, GLA_DK, DV), lambda n: (n, 0, 0, 0))],
        out_shape=[jax.ShapeDtypeStruct((bsz, t, 512), MXU), jax.ShapeDtypeStruct((nc, nh, GLA_DK, DV), F32)],
        scratch_shapes=[pltpu.VMEM((nh, GLA_DK, DV), F32)],
        compiler_params=_params(("arbitrary",)),
    )(pg, pg, pg, pg, pg, wgu, bg, nw)


def _gla_bwd_call(pg, s_hist, dyin, wgu, bg, nw):
    bsz, t, _ = pg.shape
    nc = t // CHUNK
    nh = bsz * H
    n_of, specs = _chunk_specs(nc, True, bsz, _GLA_COLS)

    def body(q_ref, k_ref, v_ref, og_ref, lr_ref, sh_ref, dy_ref, wgu_ref, bg_ref, nw_ref,
             dp_ref, dwgu_ref, dbg_ref, dnw_ref, ds_ref):
        @pl.when(pl.program_id(0) == 0)
        def _():
            dwgu_ref[...] = jnp.zeros_like(dwgu_ref)
            dbg_ref[...] = jnp.zeros_like(dbg_ref)
            dnw_ref[...] = jnp.zeros_like(dnw_ref)
            ds_ref[...] = jnp.zeros_like(ds_ref)

        ds = ds_ref[...]
        for sub in reversed(range(SUB)):
            rows = _chunk_rows(sub)
            q, k, v, lr, og, w, b_, nw_ = _gla_args((q_ref, k_ref, v_ref, og_ref, lr_ref, wgu_ref, bg_ref, nw_ref), bsz, rows)
            _, vjp = jax.vjp(gla_chunk, q, k, v, lr, og, sh_ref[sub], w, b_, nw_)
            dq, dk, dv, dlr, dog, ds, dwgu, dbg, dnw = vjp((_heads(dy_ref, bsz, rows, 128), ds))
            dnw_ref[...] += dnw
            for b in range(bsz):
                for h in range(H):
                    i = H * b + h
                    dp_ref[b, rows, 512 + 128 * h:512 + 128 * h + 128] = dv[i].astype(MXU)
                    dp_ref[b, rows, 1024 + 128 * h:1024 + 128 * h + 128] = dog[i].astype(MXU)
                    dwgu_ref[:, 64 * h:64 * h + 64] += dwgu[i]
                    dbg_ref[:, 64 * h:64 * h + 64] += dbg[i]
                for j in range(H // 2):
                    dp_ref[b, rows, 128 * j:128 * j + 128] = jnp.concatenate(
                        [dq[H * b + 2 * j], dq[H * b + 2 * j + 1]], axis=-1).astype(MXU)
                    dp_ref[b, rows, 256 + 128 * j:256 + 128 * j + 128] = jnp.concatenate(
                        [dk[H * b + 2 * j], dk[H * b + 2 * j + 1]], axis=-1).astype(MXU)
                dp_ref[b, rows, 1536:1664] = (dlr[H * b] + dlr[H * b + 1] + dlr[H * b + 2] + dlr[H * b + 3]).astype(MXU)
        ds_ref[...] = ds

    return pl.pallas_call(
        body, name="gla_bwd", grid=(nc // SUB,),
        in_specs=specs + [pl.BlockSpec((SUB, nh, GLA_DK, DV), lambda n: (n_of(n), 0, 0, 0)),
                          pl.BlockSpec((bsz, SUB * CHUNK, 512), lambda n: (0, n_of(n), 0)),
                          _full((128, 256)), _full((1, 256)), _full((1, 128))],
        out_specs=[pl.BlockSpec((bsz, SUB * CHUNK, W_GLA), lambda n: (0, n_of(n), 0)),
                   _full((128, 256)), _full((1, 256)), _full((1, 128))],
        out_shape=[jax.ShapeDtypeStruct(pg.shape, MXU), jax.ShapeDtypeStruct((128, 256), F32),
                   jax.ShapeDtypeStruct((1, 256), F32), jax.ShapeDtypeStruct((1, 128), F32)],
        scratch_shapes=[pltpu.VMEM((nh, GLA_DK, DV), F32)],
        compiler_params=_params(("arbitrary",)),
    )(pg, pg, pg, pg, pg, s_hist, dyin, wgu, bg, nw)


_GDN_COLS = [(512, 0), (512, 1), (512, 2), (512, 0), (128, 4)]


def _gdn_args(refs, bsz, rows):
    q_ref, k_ref, v_ref, og_ref, ab_ref, sc_ref, nw_ref = refs
    return (_heads(q_ref, bsz, rows, 128), _heads(k_ref, bsz, rows, 128), _heads(v_ref, bsz, rows, 128),
            _heads(ab_ref, bsz, rows, 1), _heads(ab_ref, bsz, rows, 1, off=H), _heads(og_ref, bsz, rows, 128),
            _per_head(sc_ref, bsz, 1, slice(0, 1)), _per_head(sc_ref, bsz, 1, slice(1, 2)), nw_ref[...])


def _gdn_fwd_call(conv, pd, sc, nw):
    bsz, t, _ = conv.shape
    nc = t // CHUNK
    nh = bsz * H
    _, specs = _chunk_specs(nc, False, bsz, _GDN_COLS)

    def body(q_ref, k_ref, v_ref, og_ref, ab_ref, sc_ref, nw_ref, y_ref, sh_ref, th_ref, s_ref):
        @pl.when(pl.program_id(0) == 0)
        def _():
            s_ref[...] = jnp.zeros_like(s_ref)

        s = s_ref[...]
        for sub in range(SUB):
            rows = _chunk_rows(sub)
            q, k, v, a, bb, og, alog, dtb, nw_ = _gdn_args((q_ref, k_ref, v_ref, og_ref, ab_ref, sc_ref, nw_ref), bsz, rows)
            sh_ref[sub] = s
            y, s, tinv = gdn_chunk(q, k, v, a, bb, og, s, alog, dtb, nw_)
            th_ref[sub] = tinv.astype(MXU)
            for b in range(bsz):
                for h in range(H):
                    y_ref[b, rows, 128 * h:128 * h + 128] = y[H * b + h].astype(MXU)
        s_ref[...] = s

    return pl.pallas_call(
        body, name="gdn_fwd", grid=(nc // SUB,),
        in_specs=specs + [_full((2, 128)), _full((1, 128))],
        out_specs=[pl.BlockSpec((bsz, SUB * CHUNK, 512), lambda n: (0, n, 0)),
                   pl.BlockSpec((SUB, nh, DV, DV), lambda n: (n, 0, 0, 0)),
                   pl.BlockSpec((SUB, nh, CHUNK, CHUNK), lambda n: (n, 0, 0, 0))],
        out_shape=[jax.ShapeDtypeStruct((bsz, t, 512), MXU), jax.ShapeDtypeStruct((nc, nh, DV, DV), F32),
                   jax.ShapeDtypeStruct((nc, nh, CHUNK, CHUNK), MXU)],
        scratch_shapes=[pltpu.VMEM((nh, DV, DV), F32)],
        compiler_params=_params(("arbitrary",)),
    )(conv, conv, conv, pd, pd, sc, nw)


def _gdn_bwd_call(conv, pd, s_hist, t_hist, dyin, sc, nw):
    bsz, t, _ = conv.shape
    nc = t // CHUNK
    nh = bsz * H
    n_of, specs = _chunk_specs(nc, True, bsz, _GDN_COLS, SUB_GDN_BWD)

    def body(q_ref, k_ref, v_ref, og_ref, ab_ref, sh_ref, th_ref, dy_ref, sc_ref, nw_ref,
             dc_ref, dpd_ref, dsc_ref, dnw_ref, ds_ref):
        @pl.when(pl.program_id(0) == 0)
        def _():
            dsc_ref[...] = jnp.zeros_like(dsc_ref)
            dnw_ref[...] = jnp.zeros_like(dnw_ref)
            ds_ref[...] = jnp.zeros_like(ds_ref)

        lane = lax.broadcasted_iota(jnp.int32, (CHUNK, 128), 1)
        ds = ds_ref[...]
        for sub in reversed(range(SUB_GDN_BWD)):
            rows = _chunk_rows(sub)
            q, k, v, a, bb, og, alog, dtb, nw_ = _gdn_args((q_ref, k_ref, v_ref, og_ref, ab_ref, sc_ref, nw_ref), bsz, rows)
            _, vjp = jax.vjp(functools.partial(gdn_chunk, tinv=th_ref[sub]), q, k, v, a, bb, og, sh_ref[sub], alog, dtb, nw_)
            dq, dk, dv, da, db, dog, ds, dalog, ddtb, dnw = vjp((_heads(dy_ref, bsz, rows, 128), ds))
            dnw_ref[...] += dnw
            for b in range(bsz):
                dab = jnp.zeros((CHUNK, 128), F32)
                for h in range(H):
                    i = H * b + h
                    dc_ref[b, rows, 128 * h:128 * h + 128] = dq[i]
                    dc_ref[b, rows, 512 + 128 * h:512 + 128 * h + 128] = dk[i]
                    dc_ref[b, rows, 1024 + 128 * h:1024 + 128 * h + 128] = dv[i]
                    dpd_ref[b, rows, 128 * h:128 * h + 128] = dog[i].astype(MXU)
                    dab = dab + jnp.where(lane == h, da[i], 0.0) + jnp.where(lane == H + h, db[i], 0.0)
                    dsc_ref[0:1, h:h + 1] += dalog[i]
                    dsc_ref[1:2, h:h + 1] += ddtb[i]
                dpd_ref[b, rows, 512:640] = dab.astype(MXU)
        ds_ref[...] = ds

    return pl.pallas_call(
        body, name="gdn_bwd", grid=(nc // SUB_GDN_BWD,),
        in_specs=specs + [pl.BlockSpec((SUB_GDN_BWD, nh, DV, DV), lambda n: (n_of(n), 0, 0, 0)),
                          pl.BlockSpec((SUB_GDN_BWD, nh, CHUNK, CHUNK), lambda n: (n_of(n), 0, 0, 0)),
                          pl.BlockSpec((bsz, SUB_GDN_BWD * CHUNK, 512), lambda n: (0, n_of(n), 1)),
                          _full((2, 128)), _full((1, 128))],
        out_specs=[pl.BlockSpec((bsz, SUB_GDN_BWD * CHUNK, W_GQKV), lambda n: (0, n_of(n), 0)),
                   pl.BlockSpec((bsz, SUB_GDN_BWD * CHUNK, W_GDN), lambda n: (0, n_of(n), 0)),
                   _full((2, 128)), _full((1, 128))],
        out_shape=[jax.ShapeDtypeStruct(conv.shape, F32), jax.ShapeDtypeStruct(pd.shape, MXU),
                   jax.ShapeDtypeStruct((2, 128), F32), jax.ShapeDtypeStruct((1, 128), F32)],
        scratch_shapes=[pltpu.VMEM((nh, DV, DV), F32)],
        compiler_params=_params(("arbitrary",)),
    )(conv, conv, conv, pd, pd, s_hist, t_hist, dyin, sc, nw)


def _head_call(x, ya, yb, wout, mod3, lnw, lnb, tgt):
    bsz, t, _ = x.shape
    tm = min(512, t)
    rows = min(256, tm)

    def body(x_ref, ya_ref, yb_ref, w_ref, mod_ref, lnw_ref, lnb_ref, t_ref,
             dyin_ref, dxa_ref, dgate_ref, dw_ref, dlnw_ref, dlnb_ref, loss_ref):
        b, i = pl.program_id(0), pl.program_id(1)

        @pl.when((b == 0) & (i == 0))
        def _():
            dw_ref[...] = jnp.zeros_like(dw_ref)
            dlnw_ref[...] = jnp.zeros_like(dlnw_ref)
            dlnb_ref[...] = jnp.zeros_like(dlnb_ref)
            loss_ref[...] = jnp.zeros_like(loss_ref)

        @pl.when(i == 0)
        def _():
            dgate_ref[...] = jnp.zeros_like(dgate_ref)

        w = w_ref[...]
        parts = [slice(p * rows, (p + 1) * rows) for p in range(tm // rows)]
        yin = [jnp.concatenate([ya_ref[0, rs, :], yb_ref[0, rs, :]], axis=-1).astype(MXU) for rs in parts]
        y = [jnp.dot(yi, w, preferred_element_type=F32) for yi in yin]
        for rs, yi, y_p in zip(parts, yin, y):
            loss, vjp = jax.vjp(head_fn, x_ref[0, rs, :], y_p, mod_ref[0, 2:3, :], lnw_ref[...], lnb_ref[...], t_ref[0, rs, :])
            dx, dy, dgate, dlnw, dlnb, _ = vjp(jnp.ones((1, 1), F32))
            dyb = dy.astype(MXU)
            dyin_ref[0, rs, :] = lax.dot_general(dyb, w, (((1,), (1,)), ((), ())), preferred_element_type=F32)
            dw_ref[...] += lax.dot_general(yi, dyb, (((0,), (0,)), ((), ())), preferred_element_type=F32)
            dxa_ref[0, rs, :] = dx
            dgate_ref[0] += dgate
            dlnw_ref[...] += dlnw
            dlnb_ref[...] += dlnb
            loss_ref[...] += jnp.broadcast_to(loss, (1, 128))

    tok = lambda w, j=0: pl.BlockSpec((1, tm, w), lambda b, i: (b, i, j))
    row = pl.BlockSpec((1, D), lambda b, i: (0, 0))
    return pl.pallas_call(
        body, name="head", grid=(bsz, t // tm),
        in_specs=[tok(D), tok(512), tok(512), pl.BlockSpec((D, D), lambda b, i: (0, 0)),
                  pl.BlockSpec((1, 3, D), lambda b, i: (b, 0, 0)), row, row, tok(D)],
        out_specs=[tok(D), tok(D), pl.BlockSpec((1, 1, D), lambda b, i: (b, 0, 0)),
                   pl.BlockSpec((D, D), lambda b, i: (0, 0)), row, row, pl.BlockSpec((1, 128), lambda b, i: (0, 0))],
        out_shape=[jax.ShapeDtypeStruct(x.shape, F32), jax.ShapeDtypeStruct(x.shape, F32),
                   jax.ShapeDtypeStruct((bsz, 1, D), F32), jax.ShapeDtypeStruct((D, D), F32),
                   jax.ShapeDtypeStruct((1, D), F32), jax.ShapeDtypeStruct((1, D), F32),
                   jax.ShapeDtypeStruct((1, 128), F32)],
        compiler_params=_params(("arbitrary", "arbitrary")),
    )(x, ya, yb, wout, mod3, lnw, lnb, tgt)


def _dh_call(dpg, dpq, dpd, wpt, x, mod3, dxa, smalls):
    bsz, t, _ = x.shape
    tm = min(512, t)
    assert bsz + 1 <= 8

    def body(dg_ref, dq_ref, dd_ref, w_ref, x_ref, mod_ref, dxa_ref, dgate_ref, dlnw_ref, dlnb_ref, dbg_ref, n1_ref, n2_ref,
             dsc_ref, loss_ref, gx_ref, sp_ref):
        b = pl.program_id(0)

        @pl.when((b == 0) & (pl.program_id(1) == 0))
        def _():
            sp_ref[...] = jnp.zeros_like(sp_ref)
            for e in range(bsz):
                sp_ref[e:e + 1, 2 * D:3 * D] = dgate_ref[e]
            off = 0
            for ref in (dlnw_ref, dlnb_ref, dbg_ref, n1_ref, n2_ref):
                sp_ref[bsz:bsz + 1, off:off + ref.shape[1]] = ref[...]
                off += ref.shape[1]
            sp_ref[bsz:bsz + 1, off:off + 128] = dsc_ref[0:1, :]
            sp_ref[bsz:bsz + 1, off + 128:off + 256] = dsc_ref[1:2, :]
            sp_ref[bsz:bsz + 1, SMALL_W:SMALL_W + 128] = loss_ref[...]

        mm = lambda a, lo, hi: jnp.dot(a.astype(MXU), w_ref[lo:hi, :], preferred_element_type=F32)
        dh = mm(dg_ref[0], 0, W_GLA) + mm(dq_ref[0], W_GLA, W_GLA + W_GQKV) + mm(dd_ref[0], W_GLA + W_GQKV, PW)
        gx_ref[0] = dh * (1.0 + mod_ref[0, 1:2, :]) + dxa_ref[0]
        dshift = jnp.sum(dh, axis=0, keepdims=True)
        dscale = jnp.sum(dh * x_ref[0], axis=0, keepdims=True)
        for e in range(bsz):
            @pl.when(b == e)
            def _():
                sp_ref[e:e + 1, 0:D] += dshift
                sp_ref[e:e + 1, D:2 * D] += dscale

    tok = lambda w: pl.BlockSpec((1, tm, w), lambda b, i: (b, i, 0))
    whole = lambda a: pl.BlockSpec(a.shape, lambda b, i: (0,) * a.ndim)
    return pl.pallas_call(
        body, name="dh", grid=(bsz, t // tm),
        in_specs=[tok(W_GLA), tok(W_GQKV), tok(W_GDN), pl.BlockSpec((PW, D), lambda b, i: (0, 0)), tok(D),
                  pl.BlockSpec((1, 3, D), lambda b, i: (b, 0, 0)), tok(D)] + [whole(a) for a in smalls],
        out_specs=[tok(D), pl.BlockSpec((8, SPW), lambda b, i: (0, 0))],
        out_shape=[jax.ShapeDtypeStruct(x.shape, F32), jax.ShapeDtypeStruct((8, SPW), F32)],
        compiler_params=_params(("arbitrary", "arbitrary")),
    )(dpg, dpq, dpd, wpt, x, mod3, dxa, *smalls)


def _dw_call(x, mod3, dpg, dpq, dpd):
    bsz, t, _ = x.shape
    tm = min(512, t)
    nsteps = bsz * (t // tm)

    def body(x_ref, mod_ref, dg_ref, dq_ref, dd_ref, dw_ref, acc):
        step = pl.program_id(0) * (t // tm) + pl.program_id(1)

        @pl.when(step == 0)
        def _():
            acc[...] = jnp.zeros_like(acc)

        h = (x_ref[0] * (1.0 + mod_ref[0, 1:2, :]) + mod_ref[0, 0:1, :]).astype(MXU)
        for ref, lo, hi in ((dg_ref, 0, W_GLA), (dq_ref, W_GLA, W_GLA + W_GQKV), (dd_ref, W_GLA + W_GQKV, PW)):
            acc[lo:hi, :] += lax.dot_general(ref[0].astype(MXU), h, (((0,), (0,)), ((), ())), preferred_element_type=F32)

        @pl.when(step == nsteps - 1)
        def _():
            dw_ref[...] = acc[...].astype(dw_ref.dtype)

    tok = lambda w: pl.BlockSpec((1, tm, w), lambda b, i: (b, i, 0))
    return pl.pallas_call(
        body, name="dw", grid=(bsz, t // tm),
        in_specs=[tok(D), pl.BlockSpec((1, 3, D), lambda b, i: (b, 0, 0)), tok(W_GLA), tok(W_GQKV), tok(W_GDN)],
        out_specs=pl.BlockSpec((PW, D), lambda b, i: (0, 0)),
        out_shape=jax.ShapeDtypeStruct((PW, D), WIRE),
        scratch_shapes=[pltpu.VMEM((PW, D), F32)],
        compiler_params=_params(("arbitrary", "arbitrary")),
    )(x, mod3, dpg, dpq, dpd)


def _adamw(w, g, m, v):
    m = ADAM_B1 * m + (1.0 - ADAM_B1) * g
    v = ADAM_B2 * v + (1.0 - ADAM_B2) * jnp.square(g)
    m_hat = m / (1.0 - ADAM_B1 ** ADAM_STEP)
    v_hat = v / (1.0 - ADAM_B2 ** ADAM_STEP)
    delta = -ADAM_LR * (m_hat / (jnp.sqrt(v_hat) + ADAM_EPS) + ADAM_WD * w)
    return delta, m, v


def _sum8(ref):
    g = ref[0].astype(F32)
    for j in range(1, NDEV):
        g = g + ref[j].astype(F32)
    return g


def _adam_sum_call(name, g8, w, m, v, cols):
    r, c = w.shape

    def body(g_ref, w_ref, m_ref, v_ref, go_ref, d_ref, mo_ref, vo_ref):
        g = _sum8(g_ref)
        go_ref[...] = g
        d_ref[...], mo_ref[...], vo_ref[...] = _adamw(w_ref[...], g, m_ref[...], v_ref[...])

    blk = pl.BlockSpec((r, cols), lambda i: (0, i))
    return pl.pallas_call(
        body, name=name, grid=(c // cols,),
        in_specs=[pl.BlockSpec((NDEV, r, cols), lambda i: (0, 0, i)), blk, blk, blk],
        out_specs=[blk] * 4, out_shape=[jax.ShapeDtypeStruct((r, c), F32)] * 4,
        compiler_params=_params(("parallel",)),
    )(g8, w, m, v)


def _adam_ada_call(c_all, dmod_cols, w, m, v):
    def body(c_ref, dm_ref, w_ref, m_ref, v_ref, go_ref, d_ref, mo_ref, vo_ref):
        g = lax.dot_general(c_ref[...].astype(MXU), dm_ref[...].astype(MXU), (((0,), (0,)), ((), ())),
                            preferred_element_type=F32)
        go_ref[...] = g
        d_ref[...], mo_ref[...], vo_ref[...] = _adamw(w_ref[...], g, m_ref[...], v_ref[...])

    return pl.pallas_call(
        body, name="adam_ada", out_shape=[jax.ShapeDtypeStruct(w.shape, F32)] * 4, compiler_params=_params(),
    )(c_all, dmod_cols, w, m, v)


_SMALL_AT = dict(ln_w=(0, 1024), ln_b=(1024, 1024), b_gate=(2048, 256), gla_nw=(2304, 128), gdn_nw=(2432, 128),
                 a_log=(2560, 4), dt_bias=(2688, 4))


def _adam_small_call(sp_all, bsz, params):
    names = list(params)

    def body(sp_ref, *refs):
        ins, outs = refs[:3 * len(names)], refs[3 * len(names):]
        dmod_ref, loss_ref, outs = outs[0], outs[1], outs[2:]
        packed = sp_ref[0, bsz:bsz + 1, :]
        for j in range(1, NDEV):
            packed = packed + sp_ref[j, bsz:bsz + 1, :]
        gb = None
        for j in range(NDEV):
            dmod_ref[bsz * j:bsz * j + bsz, :] = sp_ref[j, 0:bsz, :]
            for e in range(bsz):
                gb = sp_ref[j, e:e + 1, :] if gb is None else gb + sp_ref[j, e:e + 1, :]
        loss_ref[...] = packed[:, SMALL_W:SMALL_W + 128]
        for i, name in enumerate(names):
            if name == "b_ada":
                g = gb
            else:
                lo, n = _SMALL_AT[name]
                g = packed[:, lo:lo + n]
            w_ref, m_ref, v_ref = ins[3 * i:3 * i + 3]
            g_ref, d_ref, mo_ref, vo_ref = outs[4 * i:4 * i + 4]
            g_ref[...] = g
            d_ref[...], mo_ref[...], vo_ref[...] = _adamw(w_ref[...], g, m_ref[...], v_ref[...])

    flat = [a for name in names for a in params[name]]
    out_shape = [jax.ShapeDtypeStruct((NDEV * bsz, SPW), F32), jax.ShapeDtypeStruct((1, 128), F32)]
    out_shape += [jax.ShapeDtypeStruct(params[name][0].shape, F32) for name in names for _ in range(4)]
    res = pl.pallas_call(body, name="adam_small", out_shape=out_shape, compiler_params=_params())(sp_all, *flat)
    return res[0], res[1], {name: res[2 + 4 * i:6 + 4 * i] for i, name in enumerate(names)}


def _mesh_pos():
    x, y, c = lax.axis_index("x"), lax.axis_index("y"), lax.axis_index("c")
    return x, y, c, 4 * x + 2 * y + c


def _peer(x, y, c, k):
    px = 1 - x if k & 4 else x
    py = 1 - y if k & 2 else y
    pc = 1 - c if k & 1 else c
    return (px, py, pc), 4 * px + 2 * py + pc


_ANY = pl.BlockSpec(memory_space=pl.ANY)
_VMEM = pl.BlockSpec(memory_space=pltpu.VMEM)


def _gather_call(c8, w_ada, b_sh, w_in_t, conv_w):
    C_SEM, W_SEM, MOD_SEM, CONV_SEM = 0, 1, 2, 3

    def body(c_ref, wada_ref, b_ref, win_ref, cw_ref, wall_ref, call_ref, mod_ref, cwall_ref, modp, send_sems, recv_sems, loc_sem):
        x, y, c, me = _mesh_pos()

        def remote(src, dst, a, k, to):
            return pltpu.make_async_remote_copy(src_ref=src, dst_ref=dst, send_sem=send_sems.at[a, k],
                                                recv_sem=recv_sems.at[a, k], device_id=_peer(x, y, c, to)[0],
                                                device_id_type=pl.DeviceIdType.MESH)

        idx = lambda k: _peer(x, y, c, k)[1]
        sends = []
        call_ref[me] = c_ref[...]
        cwall_ref[me] = cw_ref[...]
        for k in range(1, NDEV):
            sends.append(remote(c_ref, call_ref.at[me], C_SEM, k, k))
            sends[-1].start()
            sends.append(remote(cw_ref, cwall_ref.at[me], CONV_SEM, k, k))
            sends[-1].start()
        local = pltpu.make_async_copy(win_ref, wall_ref.at[me], loc_sem)
        local.start()
        for k in (1, 2, 4, 6):
            sends.append(remote(win_ref, wall_ref.at[me], W_SEM, k, k))
            sends[-1].start()
        for k in range(1, NDEV):
            remote(c_ref, call_ref.at[idx(k)], C_SEM, k, k).wait_recv()
        modp[...] = jnp.dot(call_ref[...].reshape(NDEV * 8, D).astype(MXU), wada_ref[...].astype(MXU),
                            preferred_element_type=F32) + b_ref[...]
        mod_ref[me] = modp[pl.ds(pl.multiple_of(me * 8, 8), 8), :]
        for k in range(1, NDEV):
            sends.append(remote(modp.at[pl.ds(pl.multiple_of(idx(k) * 8, 8), 8), :], mod_ref.at[me], MOD_SEM, k, k))
            sends[-1].start()
        for k in (2, 4, 6):
            remote(win_ref, wall_ref.at[idx(k)], W_SEM, k, k).wait_recv()
            sends.append(remote(wall_ref.at[idx(k)], wall_ref.at[idx(k)], W_SEM, k + 1, 1))
            sends[-1].start()
        for k in (1, 3, 5, 7):
            remote(win_ref, wall_ref.at[idx(k)], W_SEM, k, 1).wait_recv()
        for k in range(1, NDEV):
            remote(modp.at[pl.ds(0, 8), :], mod_ref.at[idx(k)], MOD_SEM, k, k).wait_recv()
            remote(cw_ref, cwall_ref.at[idx(k)], CONV_SEM, k, k).wait_recv()
        for cp in sends:
            cp.wait_send()
        local.wait()

    return pl.pallas_call(
        body, name="gather",
        out_shape=[jax.ShapeDtypeStruct((NDEV,) + w_in_t.shape, w_in_t.dtype), jax.ShapeDtypeStruct((NDEV, 8, D), F32),
                   jax.ShapeDtypeStruct((NDEV, 8, SHARD_ADA), F32), jax.ShapeDtypeStruct((NDEV,) + conv_w.shape, F32)],
        in_specs=[_VMEM, _VMEM, _VMEM, _ANY, _VMEM], out_specs=[_ANY, _VMEM, _VMEM, _VMEM],
        scratch_shapes=[pltpu.VMEM((NDEV * 8, SHARD_ADA), F32), pltpu.SemaphoreType.DMA((4, NDEV)),
                        pltpu.SemaphoreType.DMA((4, NDEV)), pltpu.SemaphoreType.DMA],
        compiler_params=_params(),
    )(c8, w_ada, b_sh, w_in_t, conv_w)


_HBM = pl.BlockSpec(memory_space=pltpu.HBM)
_SEM = pl.BlockSpec(memory_space=pltpu.SEMAPHORE)
_EFFECT = pltpu.SideEffectType.DATAFLOW_SIDE_EFFECTING


def _whole(gather, a):
    return gather[a] if isinstance(gather, (list, tuple)) else gather


def _xchg_start(name, blocks, lands, gather):
    nb = len(blocks)

    def body(*refs):
        srcs, dsts = refs[:nb], refs[nb:2 * nb]
        send_sems, recv_sems = refs[2 * nb], refs[2 * nb + 1]
        token = refs[-1]
        x, y, c, me = _mesh_pos()
        for k in range(1, NDEV):
            dev, pidx = _peer(x, y, c, k)
            for a in range(nb):
                pltpu.make_async_remote_copy(src_ref=srcs[a] if _whole(gather, a) else srcs[a].at[pidx], dst_ref=dsts[a].at[me],
                                             send_sem=send_sems.at[NDEV * a + k], recv_sem=recv_sems.at[NDEV * a + k],
                                             device_id=dev, device_id_type=pl.DeviceIdType.MESH).start()
        token[...] = jnp.zeros_like(token)

    thru = [pltpu.HBM(a.shape, a.dtype) for a in list(blocks) + list(lands)]
    return pl.pallas_call(
        body, name=name,
        out_shape=(pltpu.SemaphoreType.DMA((nb * NDEV,)), pltpu.SemaphoreType.DMA((nb * NDEV,)), *thru,
                   jax.ShapeDtypeStruct((8, 128), F32)),
        in_specs=[_HBM] * (2 * nb), out_specs=(_SEM, _SEM, *([_HBM] * (2 * nb)), _VMEM),
        input_output_aliases={i: 2 + i for i in range(2 * nb)},
        compiler_params=pltpu.CompilerParams(has_side_effects=_EFFECT),
    )(*[pltpu.with_memory_space_constraint(a, pltpu.HBM) for a in list(blocks) + list(lands)])


def _xchg_wait(name, send_sems, recv_sems, thru, after, gather):
    nb = len(thru) // 2

    def body(*refs):
        srcs, dsts = refs[:nb], refs[nb:2 * nb]
        send_sems, recv_sems = refs[2 * nb], refs[2 * nb + 1]
        x, y, c, me = _mesh_pos()
        for k in range(1, NDEV):
            dev, pidx = _peer(x, y, c, k)
            for a in range(nb):
                cp = pltpu.make_async_remote_copy(src_ref=srcs[a] if _whole(gather, a) else srcs[a].at[pidx], dst_ref=dsts[a].at[pidx],
                                                  send_sem=send_sems.at[NDEV * a + k], recv_sem=recv_sems.at[NDEV * a + k],
                                                  device_id=dev, device_id_type=pl.DeviceIdType.MESH)
                cp.wait_send()
                cp.wait_recv()

    out = pl.pallas_call(
        body, name=name, out_shape=tuple(pltpu.HBM(a.shape, a.dtype) for a in thru),
        in_specs=[_HBM] * (2 * nb) + [_SEM, _SEM, pl.BlockSpec(memory_space=pl.ANY)], out_specs=tuple([_HBM] * (2 * nb)),
        input_output_aliases={i: i for i in range(2 * nb)},
        compiler_params=pltpu.CompilerParams(has_side_effects=_EFFECT),
    )(*thru, send_sems, recv_sems, after)
    return out[nb:]


def _pad_cols(a, n):
    return jnp.pad(a, ((0, 0), (0, n - a.shape[1])))


_SEGMENTS = ((0, 256, 0), (256, 512, 256), (512, 1024, 512), (1040, 1552, 1024), (1024, 1040, 1536),
             (1552, 3088, W_GLA), (3096, 3608, W_GLA + W_GQKV), (3088, 3096, W_GLA + W_GQKV + 512))


def _row_pieces():
    out = []
    for lo, hi, dst in _SEGMENTS:
        while lo < hi:
            j, off = divmod(lo, SHARD_IN)
            n = min(hi - lo, SHARD_IN - off)
            out.append((j, off, n, dst))
            lo, dst = lo + n, dst + n
    return out


def _relayout_call(a, to_layout):
    cols = 256
    dst_shape = (PW, D) if to_layout else (NDEV, SHARD_IN, D)

    def body(i_ref, o_ref, scr):
        if to_layout:
            scr[...] = jnp.zeros_like(scr)
        for j, off, n, at in _row_pieces():
            if to_layout:
                scr[at:at + n, :] = i_ref[j, off:off + n, :].astype(F32)
            else:
                scr[j, off:off + n, :] = i_ref[at:at + n, :].astype(F32)
        o_ref[...] = scr[...].astype(o_ref.dtype)

    blk = lambda shape: pl.BlockSpec(shape[:-1] + (cols,), lambda i: (0,) * (len(shape) - 1) + (i,))
    return pl.pallas_call(
        body, name="to_layout" if to_layout else "to_shards", grid=(D // cols,),
        in_specs=[blk(a.shape)], out_specs=blk(dst_shape), out_shape=jax.ShapeDtypeStruct(dst_shape, a.dtype),
        scratch_shapes=[pltpu.VMEM(dst_shape[:-1] + (cols,), F32)],
        compiler_params=_params(("parallel",)),
    )(a)


def _assemble_wt(blocks):
    return _relayout_call(blocks, True)


def _disassemble_dwt(dwt):
    return _relayout_call(dwt, False)


def local_grads(x, mod3, wp, conv_w, late_weights, bg, gla_nw, sc, gdn_nw, lnw, lnb, tgt):
    pg, pq, pd, conv = _proj_call(x, mod3, wp, conv_w)
    wout, wgu_p = late_weights(pq)
    ya, s_gla = _gla_fwd_call(pg, wgu_p, bg, gla_nw)
    yb, s_gdn, t_gdn = _gdn_fwd_call(conv, pd, sc, gdn_nw)
    dyin, dxa, dgate, dwout, dlnw, dlnb, loss = _head_call(x, ya, yb, wout, mod3, lnw, lnb, tgt)
    dpg, dwgu, dbg, dnw_gla = _gla_bwd_call(pg, s_gla, dyin, wgu_p, bg, gla_nw)
    dconv, dpd, dsc, dnw_gdn = _gdn_bwd_call(conv, pd, s_gdn, t_gdn, dyin, sc, gdn_nw)
    dpq, dconv_w = _conv_bwd_call(dconv, pq, conv_w)
    dw_in = _disassemble_dwt(_dw_call(x, mod3, dpg, dpq, dpd))
    g = dict(dw_in=dw_in, dwout=dwout, dconv_w=dconv_w[:4], dwgu=dwgu[:16])
    smalls = (dgate, dlnw, dlnb, dbg, dnw_gla, dnw_gdn, dsc, loss)
    return g, lambda mod3_: _dh_call(dpg, dpq, dpd, wp, x, mod3_, dxa, smalls)


def local_step(x, mod3, wp, wout, conv_w, wgu_p, *args):
    g, finish = local_grads(x, mod3, wp, conv_w, lambda _: (wout, wgu_p), *args)
    g["gx"], sp = finish(mod3)
    bsz = x.shape[0]
    g["dmod"] = sp[:bsz].reshape(bsz, 3, D)
    g["loss"] = sp[bsz, SMALL_W]
    for name, (lo, n) in _SMALL_AT.items():
        g[name] = sp[bsz:bsz + 1, lo:lo + n]
    return g


def kernel(x, c, w_ada, b_ada, w_in, gla_w_gate_up, gla_b_gate, gla_norm_w, gdn_conv_w, gdn_a_log, gdn_dt_bias, gdn_norm_w, w_out, ln_w, ln_b, loss_target, m_w_ada, m_b_ada, m_w_in, m_gla_w_gate_up, m_gla_b_gate, m_gla_norm_w, m_gdn_conv_w, m_gdn_a_log, m_gdn_dt_bias, m_gdn_norm_w, m_w_out, m_ln_w, m_ln_b, v_w_ada, v_b_ada, v_w_in, v_gla_w_gate_up, v_gla_b_gate, v_gla_norm_w, v_gdn_conv_w, v_gdn_a_log, v_gdn_dt_bias, v_gdn_norm_w, v_w_out, v_ln_w, v_ln_b):
    me = 4 * lax.axis_index("x") + 2 * lax.axis_index("y") + lax.axis_index("c")
    bsz = x.shape[0]

    b_sh = lax.dynamic_slice(b_ada, (0, me * SHARD_ADA), (1, SHARD_ADA))
    c8 = jnp.pad(c, ((0, 8 - bsz), (0, 0)))
    w_in_t, m_in_t, v_in_t = (jnp.swapaxes(a[0], 0, 1) for a in (w_in, m_w_in, v_w_in))
    win_all, c_all, mod_blk, conv_all = _gather_call(c8, w_ada[0], b_sh, w_in_t.astype(WIRE), gdn_conv_w[0])
    conv_w = jnp.transpose(conv_all, (1, 0, 2)).reshape(4, W_GQKV)
    wp = _assemble_wt(win_all)
    mod = jnp.transpose(mod_blk[:, :bsz, :], (1, 0, 2)).reshape(bsz, 3 * D)
    mod3 = mod.reshape(bsz, 3, D)
    sc = jnp.concatenate([_pad_cols(gdn_a_log, 128), _pad_cols(gdn_dt_bias, 128)], axis=0)

    own = lambda a: lax.dynamic_update_slice(lax.empty((NDEV,) + a.shape, a.dtype), a[None], (me,) + (0,) * a.ndim)
    late = [w_out[0].astype(WIRE), gla_w_gate_up[0] + 0.0 * mod_blk[0, 0, 0]]
    w_send, w_recv, *w_thru, w_token = _xchg_start("wgather_start", late, [own(a) for a in late], gather=True)

    def late_weights(pq):
        wout_all, wgu_all = _xchg_wait("wgather_wait", w_send, w_recv, w_thru, pq, gather=True)
        return wout_all.reshape(D, D), jnp.pad(jnp.transpose(wgu_all, (1, 0, 2)).reshape(16, 256), ((0, 112), (0, 0)))

    g, finish = local_grads(x, mod3 + w_token[0, 0], wp, conv_w, late_weights, gla_b_gate, gla_norm_w, sc, gdn_norm_w, ln_w, ln_b,
                            loss_target)

    big = [g["dw_in"], g["dwout"].reshape(NDEV, D // NDEV, D).astype(WIRE)]
    lands = [lax.dynamic_update_slice(lax.empty(a.shape, a.dtype), lax.dynamic_slice(a, (me, 0, 0), (1,) + a.shape[1:]),
                                      (me, 0, 0)) for a in big]
    send_sems, recv_sems, *thru, token = _xchg_start("xchg_start", big, lands, gather=False)
    gx, sp = finish(mod3 + token[0, 0])
    little = [jnp.transpose(g["dconv_w"].reshape(4, NDEV, W_GQKV // NDEV), (1, 0, 2)),
              jnp.transpose(g["dwgu"].reshape(16, NDEV, 256 // NDEV), (1, 0, 2)), sp]
    modes = [False, False, True]
    l_lands = [lax.dynamic_update_slice(lax.empty(a.shape, a.dtype), lax.dynamic_slice(a, (me, 0, 0), (1,) + a.shape[1:]),
                                        (me, 0, 0)) for a in little[:2]] + [own(sp)]
    l_send, l_recv, *l_thru, l_token = _xchg_start("small_start", little, l_lands, gather=modes)
    r_in, r_out = _xchg_wait("xchg_wait", send_sems, recv_sems, thru, l_token, gather=False)

    t_in = [jnp.swapaxes(a, 0, 1) for a in _adam_sum_call("adam_in", r_in, w_in_t, m_in_t, v_in_t, 256)]
    t_out = _adam_sum_call("adam_out", r_out, w_out[0], m_w_out[0], v_w_out[0], D)
    r_conv, r_gu, sp_all = _xchg_wait("small_wait", l_send, l_recv, l_thru, t_out[3], gather=modes)
    t_conv = _adam_sum_call("adam_conv", r_conv, gdn_conv_w[0], m_gdn_conv_w[0], v_gdn_conv_w[0], W_GQKV // NDEV)
    t_gu = _adam_sum_call("adam_gu", r_gu, gla_w_gate_up[0], m_gla_w_gate_up[0], v_gla_w_gate_up[0], 256 // NDEV)
    dmod_all, loss, small = _adam_small_call(sp_all, bsz, dict(
        b_ada=(b_ada, m_b_ada, v_b_ada), ln_w=(ln_w, m_ln_w, v_ln_w), ln_b=(ln_b, m_ln_b, v_ln_b),
        b_gate=(gla_b_gate, m_gla_b_gate, v_gla_b_gate), gla_nw=(gla_norm_w, m_gla_norm_w, v_gla_norm_w),
        gdn_nw=(gdn_norm_w, m_gdn_norm_w, v_gdn_norm_w), a_log=(gdn_a_log, m_gdn_a_log, v_gdn_a_log),
        dt_bias=(gdn_dt_bias, m_gdn_dt_bias, v_gdn_dt_bias)))
    c16 = c_all[:, :bsz, :].reshape(NDEV * bsz, D)
    t_ada = _adam_ada_call(c16, lax.dynamic_slice(dmod_all, (0, me * SHARD_ADA), (NDEV * bsz, SHARD_ADA)),
                           w_ada[0], m_w_ada[0], v_w_ada[0])

    def group(i):
        s = lambda name: small[name][i]
        return [t_ada[i][None], s("b_ada"), t_in[i][None], t_gu[i][None], s("b_gate"), s("gla_nw"), t_conv[i][None],
                s("a_log"), s("dt_bias"), s("gdn_nw"), t_out[i][None], s("ln_w"), s("ln_b")]

    return (loss[0, 0], gx, *group(0), *group(1), *group(2), *group(3))
```

```python
import functools

import jax
import jax.numpy as jnp
from jax import lax
from jax.experimental import pallas as pl
from jax.experimental.pallas import tpu as pltpu
from jax.experimental.pallas import tpu_sc as plsc

F32 = jnp.float32
MXU = jnp.bfloat16
WIRE = jnp.bfloat16
HI = lax.Precision.HIGH

D = 1024
NDEV = 8
H = 4
GLA_DK = 64
DV = 128
CHUNK = 64
SUB = 8
SUB_GDN_BWD = 4
LN_EPS = 1e-5
RMS_EPS = 1e-6
ALPHA = 2.0 ** 0.25
GATE_NORM = 16.0

W_GLA, W_GQKV, W_GDN = 1664, 1536, 640
PW = W_GLA + W_GQKV + W_GDN
IN_COLS = 3608
SHARD_IN = IN_COLS // NDEV
SHARD_ADA = 3 * D // NDEV
SPW = 3 * D
SMALL_W = 2816

ADAM_LR, ADAM_B1, ADAM_B2, ADAM_EPS, ADAM_WD, ADAM_STEP = 0.001, 0.9, 0.999, 1e-08, 0.01, 10

VMEM_LIMIT = 56 * 1024 * 1024


def _params(sem=None, **kw):
    if sem is not None:
        kw["dimension_semantics"] = sem
    return pltpu.CompilerParams(vmem_limit_bytes=VMEM_LIMIT, **kw)


_MM = (((2,), (1,)), ((0,), (0,)))
_NT = (((2,), (2,)), ((0,), (0,)))
_TN = (((1,), (1,)), ((0,), (0,)))


def _dg(a, b, dims):
    return lax.dot_general(a.astype(MXU), b.astype(MXU), dims, preferred_element_type=F32)


def _hdg(a, b, dims):
    return lax.dot_general(a, b, dims, precision=HI, preferred_element_type=F32)


@jax.custom_vjp
def bmm(a, b):
    return _dg(a, b, _MM)


bmm.defvjp(lambda a, b: (_dg(a, b, _MM), (a, b)), lambda r, g: (_dg(g, r[1], _NT), _dg(r[0], g, _TN)))


@jax.custom_vjp
def bnt(a, b):
    return _dg(a, b, _NT)


bnt.defvjp(lambda a, b: (_dg(a, b, _NT), (a, b)), lambda r, g: (_dg(g, r[1], _MM), _dg(g, r[0], _TN)))


@jax.custom_vjp
def btn(a, b):
    return _dg(a, b, _TN)


btn.defvjp(lambda a, b: (_dg(a, b, _TN), (a, b)), lambda r, g: (_dg(r[1], g, _NT), _dg(r[0], g, _MM)))


def unit_lower_inverse(a):
    n = a.shape[-1]
    r, c = _iotas(n)
    p = -a
    t = (r == c).astype(F32) + p
    for _ in range(5):
        p = _dg(p, p, _MM)
        t = t + _dg(t, p, _MM)
    return t


@jax.custom_vjp
def unit_lower_solve(a, t, r1, r2):
    return _dg(t, r1, _MM), _dg(t, r2, _MM)


def _solve_fwd(a, t, r1, r2):
    s1, s2 = _dg(t, r1, _MM), _dg(t, r2, _MM)
    return (s1, s2), (t, s1, s2)


def _solve_bwd(res, g):
    t, s1, s2 = res
    d1, d2 = _dg(t, g[0], _TN), _dg(t, g[1], _TN)
    return -(_dg(d1, s1, _NT) + _dg(d2, s2, _NT)), jnp.zeros_like(t), d1, d2


unit_lower_solve.defvjp(_solve_fwd, _solve_bwd)


def _iotas(n):
    return lax.broadcasted_iota(jnp.int32, (n, n), 0), lax.broadcasted_iota(jnp.int32, (n, n), 1)


def _col_to_row(col, eye):
    return jnp.sum(jnp.where(eye, col, 0.0), axis=1, keepdims=True)


def _row_to_col(row, eye):
    return jnp.sum(jnp.where(eye, row, 0.0), axis=2, keepdims=True)


def _pick_row(m, i):
    r = lax.broadcasted_iota(jnp.int32, m.shape, 1)
    return jnp.sum(jnp.where(r == i, m, 0.0), axis=1, keepdims=True)


def _rms_gate(o, nw, og):
    on = o * lax.rsqrt(jnp.mean(o * o, axis=-1, keepdims=True) + RMS_EPS) * nw
    return on * jax.nn.silu(og)


def gla_chunk(q, k, v, lr, og, s, wgu, bg, nw):
    n, c, _ = q.shape
    r, cc = _iotas(c)
    causal = r >= cc
    qs = q * (GLA_DK ** -0.5)
    z = bmm(lr, wgu) + bg
    g = jax.nn.log_sigmoid(z) / GATE_NORM
    b = _hdg(jnp.broadcast_to(causal.astype(F32), (n, c, c)), g, _MM)
    bref = _pick_row(b, c // 2 - 1)
    blast = _pick_row(b, c - 1)
    att = jnp.where(causal, bnt(qs * jnp.exp(b - bref), k * jnp.exp(bref - b)), 0.0)
    o = bmm(att, v) + bmm(qs * jnp.exp(b), s)
    rk, ck = _iotas(GLA_DK)
    s_new = _row_to_col(jnp.exp(blast), rk == ck) * s + btn(k * jnp.exp(blast - b), v)
    return _rms_gate(o, nw, og), s_new


def gdn_chunk(cq, ck, cv, a, bb, og, s, alog, dtb, nw, tinv=None):
    c = cq.shape[1]
    r, cc = _iotas(c)
    eye, causal, strict = r == cc, r >= cc, r > cc
    q, k, v = jax.nn.silu(cq), jax.nn.silu(ck), jax.nn.silu(cv)
    q = q * lax.rsqrt(jnp.sum(q * q, axis=-1, keepdims=True) + RMS_EPS) * (DV ** -0.5)
    k = k * lax.rsqrt(jnp.sum(k * k, axis=-1, keepdims=True) + RMS_EPS)
    g = -jnp.exp(alog) * jax.nn.softplus(a + dtb)
    beta = jax.nn.sigmoid(bb)
    d = jnp.sum(jnp.where(causal, _col_to_row(g, eye), 0.0), axis=2, keepdims=True)
    el = jnp.exp(jnp.where(causal, d - _col_to_row(d, eye), -jnp.inf))
    kb = k * beta
    amat = jnp.where(strict, bnt(kb, k) * el, 0.0)
    t = unit_lower_inverse(amat) if tinv is None else tinv
    u, w = unit_lower_solve(amat, t, v * beta, kb * jnp.exp(d))
    qk = jnp.where(causal, bnt(q, k) * el, 0.0)
    dlast = _pick_row(d, c - 1)
    v_new = u - bmm(w, s)
    o = bmm(q * jnp.exp(d), s) + bmm(qk, v_new)
    s_new = jnp.exp(dlast) * s + btn(k * jnp.exp(dlast - d), v_new)
    y = _rms_gate(o, nw, og)
    return (y, s_new, t) if tinv is None else (y, s_new)


def head_fn(x, y, gate, lnw, lnb, tgt):
    u = ALPHA * x + (1.0 + gate) * y
    mu = jnp.mean(u, axis=-1, keepdims=True)
    var = jnp.mean(jnp.square(u - mu), axis=-1, keepdims=True)
    out = (u - mu) * lax.rsqrt(var + LN_EPS) * lnw + lnb
    err = jnp.square(out - tgt)
    return 0.5 * jnp.sum(jnp.mean(err, axis=-1, keepdims=True), axis=0, keepdims=True)


def _proj_call(x, mod3, wpt, conv_w):
    bsz, t, _ = x.shape
    tm = min(512, t)

    def body(x_ref, mod_ref, w_ref, cw_ref, pg_ref, pq_ref, pd_ref, conv_ref, buf):
        i = pl.program_id(1)
        h = (x_ref[0] * (1.0 + mod_ref[0, 1:2, :]) + mod_ref[0, 0:1, :]).astype(MXU)
        nt = lambda lo, hi: lax.dot_general(h, w_ref[lo:hi, :], (((1,), (1,)), ((), ())), preferred_element_type=F32)
        pq = nt(W_GLA, W_GLA + W_GQKV)
        pq_ref[0] = pq
        @pl.when(i == 0)
        def _():
            buf[0:8, :] = jnp.zeros((8, W_GQKV), F32)

        @pl.when(i > 0)
        def _():
            buf[0:8, :] = buf[tm:tm + 8, :]

        buf[8:, :] = pq
        acc = cw_ref[0:1, :] * buf[pl.ds(5, tm), :]
        for k in range(1, 4):
            acc = acc + cw_ref[k:k + 1, :] * buf[pl.ds(5 + k, tm), :]
        conv_ref[0] = acc
        pg_ref[0] = nt(0, W_GLA)
        pd_ref[0] = nt(W_GLA + W_GQKV, PW)

    tok = lambda w: pl.BlockSpec((1, tm, w), lambda b, i: (b, i, 0))
    return pl.pallas_call(
        body, name="proj", grid=(bsz, t // tm),
        in_specs=[tok(D), pl.BlockSpec((1, 3, D), lambda b, i: (b, 0, 0)), pl.BlockSpec((PW, D), lambda b, i: (0, 0)),
                  pl.BlockSpec((4, W_GQKV), lambda b, i: (0, 0))],
        out_specs=[tok(W_GLA), tok(W_GQKV), tok(W_GDN), tok(W_GQKV)],
        out_shape=[jax.ShapeDtypeStruct((bsz, t, w), F32) for w in (W_GLA, W_GQKV, W_GDN, W_GQKV)],
        scratch_shapes=[pltpu.VMEM((tm + 8, W_GQKV), F32)],
        compiler_params=_params(("arbitrary", "arbitrary")),
    )(x, mod3, wpt, conv_w)


def _conv_bwd_call(dconv, pq, conv_w):
    bsz, t, _ = pq.shape
    tt = min(512, t)
    hb = tt // 8
    nt_ = t // tt

    def body(d_ref, dnext_ref, x_ref, w_ref, din_ref, dw_ref, dbuf):
        b, i = pl.program_id(0), pl.program_id(1)

        @pl.when((b == 0) & (i == 0))
        def _():
            dw_ref[...] = jnp.zeros_like(dw_ref)

        dbuf[0:tt, :] = d_ref[0]
        dbuf[tt:, :] = jnp.where(i < nt_ - 1, dnext_ref[0], 0.0)
        rows, unroll = 16, 4
        for j in range(W_GQKV // 128):
            ls = slice(128 * j, 128 * j + 128)
            wj = [jnp.broadcast_to(w_ref[k:k + 1, ls], (rows, 128)) for k in range(4)]

            def trip(r, dw4):
                dw4 = list(dw4)
                for u in range(unroll):
                    base = pl.multiple_of(r * (rows * unroll), rows * unroll) + rows * u
                    win = dbuf[pl.ds(base, rows + 8), ls]
                    xin = x_ref[0, pl.ds(base, rows), ls]
                    acc = None
                    for k in range(4):
                        dsh = win[3 - k:3 - k + rows, :]
                        acc = wj[k] * dsh if acc is None else acc + wj[k] * dsh
                        dw4[k] = dw4[k] + xin * dsh
                    din_ref[0, pl.ds(base, rows), ls] = acc
                return tuple(dw4)

            dw4 = lax.fori_loop(0, tt // (rows * unroll), trip, tuple(jnp.zeros((rows, 128), F32) for _ in range(4)))
            for k in range(4):
                dw_ref[k:k + 1, ls] += jnp.sum(dw4[k], axis=0, keepdims=True)

    tile = pl.BlockSpec((1, tt, W_GQKV), lambda b, i: (b, i, 0))
    return pl.pallas_call(
        body, name="conv_bwd", grid=(bsz, nt_),
        in_specs=[tile, pl.BlockSpec((1, 8, W_GQKV), lambda b, i: (b, jnp.minimum((i + 1) * hb, t // 8 - 1), 0)),
                  tile, pl.BlockSpec((4, W_GQKV), lambda b, i: (0, 0))],
        out_specs=[tile, pl.BlockSpec((8, W_GQKV), lambda b, i: (0, 0))],
        out_shape=[jax.ShapeDtypeStruct(pq.shape, F32), jax.ShapeDtypeStruct((8, W_GQKV), F32)],
        scratch_shapes=[pltpu.VMEM((tt + 8, W_GQKV), F32)],
        compiler_params=_params(("arbitrary", "arbitrary")),
    )(dconv, dconv, pq, conv_w)


SC_TILES = 32
SC_ROWS = 16


def _conv_bwd_sc_call(dconv, pq, conv_w):
    bsz, t, w_ = pq.shape
    n = bsz * t
    per_tile = n // SC_TILES
    pieces = per_tile // SC_ROWS
    rw = SC_ROWS * w_
    assert t % per_tile == 0 and per_tile % SC_ROWS == 0

    def body(d_hbm, x_hbm, w_hbm, din_hbm, dwp_hbm, dbuf, xbuf, obuf, wbuf, dwacc):
        tile = lax.axis_index("sc_tile") * 2 + lax.axis_index("sc_core")
        pltpu.sync_copy(w_hbm, wbuf)

        @pl.loop(0, 4 * w_, step=16)
        def _(i):
            dwacc[pl.ds(i, 16)] = jnp.zeros((16,), F32)

        @pl.loop(0, pieces)
        def _(p):
            r0 = tile * per_tile + p * SC_ROWS
            pltpu.sync_copy(d_hbm.at[pl.ds(r0 * w_, rw)], dbuf.at[pl.ds(0, rw)])
            pltpu.sync_copy(x_hbm.at[pl.ds(r0 * w_, rw)], xbuf)
            nxt = jnp.minimum(r0 + SC_ROWS, n - 8)
            pltpu.sync_copy(d_hbm.at[pl.ds(nxt * w_, 8 * w_)], dbuf.at[pl.ds(rw, 8 * w_)])

            @pl.when((r0 + SC_ROWS) % t == 0)
            def _():
                @pl.loop(0, 8 * w_, step=16)
                def _(i):
                    dbuf[pl.ds(rw + i, 16)] = jnp.zeros((16,), F32)

            @pl.loop(0, w_, step=16)
            def _(c):
                w0, w1, w2, w3 = (wbuf[pl.ds(k * w_ + c, 16)] for k in range(4))

                def row(s, carry):
                    a, b, cc, s0, s1, s2, s3 = carry
                    e = dbuf[pl.ds((s + 3) * w_ + c, 16)]
                    xv = xbuf[pl.ds(s * w_ + c, 16)]
                    obuf[pl.ds(s * w_ + c, 16)] = w3 * a + w2 * b + w1 * cc + w0 * e
                    return b, cc, e, s0 + xv * e, s1 + xv * cc, s2 + xv * b, s3 + xv * a

                init = tuple(dbuf[pl.ds(k * w_ + c, 16)] for k in range(3)) + tuple(dwacc[pl.ds(k * w_ + c, 16)] for k in range(4))
                out = lax.fori_loop(0, SC_ROWS, row, init)
                for k in range(4):
                    dwacc[pl.ds(k * w_ + c, 16)] = out[3 + k]

            pltpu.sync_copy(obuf, din_hbm.at[pl.ds(r0 * w_, rw)])

        pltpu.sync_copy(dwacc, dwp_hbm.at[pl.ds(tile * 4 * w_, 4 * w_)])

    din, dwp = pl.kernel(
        body, name="conv_bwd_sc",
        out_type=(jax.ShapeDtypeStruct((n * w_,), F32), jax.ShapeDtypeStruct((SC_TILES * 4 * w_,), F32)),
        mesh=plsc.VectorSubcoreMesh(core_axis_name="sc_core", subcore_axis_name="sc_tile"),
        scratch_types=[pltpu.VMEM((rw + 8 * w_,), F32), pltpu.VMEM((rw,), F32), pltpu.VMEM((rw,), F32),
                       pltpu.VMEM((4 * w_,), F32), pltpu.VMEM((4 * w_,), F32)],
    )(dconv.reshape(-1), pq.reshape(-1), conv_w.reshape(-1))
    return din.reshape(pq.shape), jnp.sum(dwp.reshape(SC_TILES, 4, w_), axis=0)


def _chunk_specs(nc, rev, bsz, cols, sub=None):
    sub = SUB if sub is None else sub
    steps = nc // sub
    n_of = (lambda n: steps - 1 - n) if rev else (lambda n: n)
    return n_of, [pl.BlockSpec((bsz, sub * CHUNK, w), lambda n, j=j: (0, n_of(n), j)) for w, j in cols]


def _full(shape):
    return pl.BlockSpec(shape, lambda n: (0,) * len(shape))


def _heads(ref, bsz, rows, width, off=0):
    return jnp.stack([ref[b, rows, off + width * h:off + width * (h + 1)] for b in range(bsz) for h in range(H)])


def _chunk_rows(sub):
    return slice(CHUNK * sub, CHUNK * (sub + 1))


def _per_head(ref, bsz, width, rows=slice(None)):
    return jnp.stack([ref[rows, width * h:width * (h + 1)] for _ in range(bsz) for h in range(H)])


_GLA_COLS = [(256, 0), (256, 1), (512, 1), (512, 2), (128, 12)]


def _gla_args(refs, bsz, rows):
    q_ref, k_ref, v_ref, og_ref, lr_ref, wgu_ref, bg_ref, nw_ref = refs
    lr = jnp.stack([lr_ref[b, rows, :] for b in range(bsz) for _ in range(H)])
    return (_heads(q_ref, bsz, rows, 64), _heads(k_ref, bsz, rows, 64), _heads(v_ref, bsz, rows, 128), lr,
            _heads(og_ref, bsz, rows, 128), _per_head(wgu_ref, bsz, 64), _per_head(bg_ref, bsz, 64), nw_ref[...])


def _gla_fwd_call(pg, wgu, bg, nw):
    bsz, t, _ = pg.shape
    nc = t // CHUNK
    nh = bsz * H
    _, specs = _chunk_specs(nc, False, bsz, _GLA_COLS)

    def body(q_ref, k_ref, v_ref, og_ref, lr_ref, wgu_ref, bg_ref, nw_ref, y_ref, sh_ref, s_ref):
        @pl.when(pl.program_id(0) == 0)
        def _():
            s_ref[...] = jnp.zeros_like(s_ref)

        s = s_ref[...]
        for sub in range(SUB):
            rows = _chunk_rows(sub)
            q, k, v, lr, og, w, b_, nw_ = _gla_args((q_ref, k_ref, v_ref, og_ref, lr_ref, wgu_ref, bg_ref, nw_ref), bsz, rows)
            sh_ref[sub] = s
            y, s = gla_chunk(q, k, v, lr, og, s, w, b_, nw_)
            for b in range(bsz):
                for h in range(H):
                    y_ref[b, rows, 128 * h:128 * h + 128] = y[H * b + h].astype(MXU)
        s_ref[...] = s

    return pl.pallas_call(
        body, name="gla_fwd", grid=(nc // SUB,),
        in_specs=specs + [_full((128, 256)), _full((1, 256)), _full((1, 128))],
        out_specs=[pl.BlockSpec((bsz, SUB * CHUNK, 512), lambda n: (0, n, 0)),
                   pl.BlockSpec((SUB, nh, GLA_DK, DV), lambda n: (n, 0, 0, 0))],
        out_shape=[jax.ShapeDtypeStruct((bsz, t, 512), MXU), jax.ShapeDtypeStruct((nc, nh, GLA_DK, DV), F32)],
        scratch_shapes=[pltpu.VMEM((nh, GLA_DK, DV), F32)],
        compiler_params=_params(("arbitrary",)),
    )(pg, pg, pg, pg, pg, wgu, bg, nw)


def _gla_bwd_call(pg, s_hist, dyin, wgu, bg, nw):
    bsz, t, _ = pg.shape
    nc = t // CHUNK
    nh = bsz * H
    n_of, specs = _chunk_specs(nc, True, bsz, _GLA_COLS)

    def body(q_ref, k_ref, v_ref, og_ref, lr_ref, sh_ref, dy_ref, wgu_ref, bg_ref, nw_ref,
             dp_ref, dwgu_ref, dbg_ref, dnw_ref, ds_ref):
        @pl.when(pl.program_id(0) == 0)
        def _():
            dwgu_ref[...] = jnp.zeros_like(dwgu_ref)
            dbg_ref[...] = jnp.zeros_like(dbg_ref)
            dnw_ref[...] = jnp.zeros_like(dnw_ref)
            ds_ref[...] = jnp.zeros_like(ds_ref)

        ds = ds_ref[...]
        for sub in reversed(range(SUB)):
            rows = _chunk_rows(sub)
            q, k, v, lr, og, w, b_, nw_ = _gla_args((q_ref, k_ref, v_ref, og_ref, lr_ref, wgu_ref, bg_ref, nw_ref), bsz, rows)
            _, vjp = jax.vjp(gla_chunk, q, k, v, lr, og, sh_ref[sub], w, b_, nw_)
            dq, dk, dv, dlr, dog, ds, dwgu, dbg, dnw = vjp((_heads(dy_ref, bsz, rows, 128), ds))
            dnw_ref[...] += dnw
            for b in range(bsz):
                for h in range(H):
                    i = H * b + h
                    dp_ref[b, rows, 512 + 128 * h:512 + 128 * h + 128] = dv[i].astype(MXU)
                    dp_ref[b, rows, 1024 + 128 * h:1024 + 128 * h + 128] = dog[i].astype(MXU)
                    dwgu_ref[:, 64 * h:64 * h + 64] += dwgu[i]
                    dbg_ref[:, 64 * h:64 * h + 64] += dbg[i]
                for j in range(H // 2):
                    dp_ref[b, rows, 128 * j:128 * j + 128] = jnp.concatenate(
                        [dq[H * b + 2 * j], dq[H * b + 2 * j + 1]], axis=-1).astype(MXU)
                    dp_ref[b, rows, 256 + 128 * j:256 + 128 * j + 128] = jnp.concatenate(
                        [dk[H * b + 2 * j], dk[H * b + 2 * j + 1]], axis=-1).astype(MXU)
                dp_ref[b, rows, 1536:1664] = (dlr[H * b] + dlr[H * b + 1] + dlr[H * b + 2] + dlr[H * b + 3]).astype(MXU)
        ds_ref[...] = ds

    return pl.pallas_call(
        body, name="gla_bwd", grid=(nc // SUB,),
        in_specs=specs + [pl.BlockSpec((SUB, nh, GLA_DK, DV), lambda n: (n_of(n), 0, 0, 0)),
                          pl.BlockSpec((bsz, SUB * CHUNK, 512), lambda n: (0, n_of(n), 0)),
                          _full((128, 256)), _full((1, 256)), _full((1, 128))],
        out_specs=[pl.BlockSpec((bsz, SUB * CHUNK, W_GLA), lambda n: (0, n_of(n), 0)),
                   _full((128, 256)), _full((1, 256)), _full((1, 128))],
        out_shape=[jax.ShapeDtypeStruct(pg.shape, MXU), jax.ShapeDtypeStruct((128, 256), F32),
                   jax.ShapeDtypeStruct((1, 256), F32), jax.ShapeDtypeStruct((1, 128), F32)],
        scratch_shapes=[pltpu.VMEM((nh, GLA_DK, DV), F32)],
        compiler_params=_params(("arbitrary",)),
    )(pg, pg, pg, pg, pg, s_hist, dyin, wgu, bg, nw)


_GDN_COLS = [(512, 0), (512, 1), (512, 2), (512, 0), (128, 4)]


def _gdn_args(refs, bsz, rows):
    q_ref, k_ref, v_ref, og_ref, ab_ref, sc_ref, nw_ref = refs
    return (_heads(q_ref, bsz, rows, 128), _heads(k_ref, bsz, rows, 128), _heads(v_ref, bsz, rows, 128),
            _heads(ab_ref, bsz, rows, 1), _heads(ab_ref, bsz, rows, 1, off=H), _heads(og_ref, bsz, rows, 128),
            _per_head(sc_ref, bsz, 1, slice(0, 1)), _per_head(sc_ref, bsz, 1, slice(1, 2)), nw_ref[...])


def _gdn_fwd_call(conv, pd, sc, nw):
    bsz, t, _ = conv.shape
    nc = t // CHUNK
    nh = bsz * H
    _, specs = _chunk_specs(nc, False, bsz, _GDN_COLS)

    def body(q_ref, k_ref, v_ref, og_ref, ab_ref, sc_ref, nw_ref, y_ref, sh_ref, th_ref, s_ref):
        @pl.when(pl.program_id(0) == 0)
        def _():
            s_ref[...] = jnp.zeros_like(s_ref)

        s = s_ref[...]
        for sub in range(SUB):
            rows = _chunk_rows(sub)
            q, k, v, a, bb, og, alog, dtb, nw_ = _gdn_args((q_ref, k_ref, v_ref, og_ref, ab_ref, sc_ref, nw_ref), bsz, rows)
            sh_ref[sub] = s
            y, s, tinv = gdn_chunk(q, k, v, a, bb, og, s, alog, dtb, nw_)
            th_ref[sub] = tinv.astype(MXU)
            for b in range(bsz):
                for h in range(H):
                    y_ref[b, rows, 128 * h:128 * h + 128] = y[H * b + h].astype(MXU)
        s_ref[...] = s

    return pl.pallas_call(
        body, name="gdn_fwd", grid=(nc // SUB,),
        in_specs=specs + [_full((2, 128)), _full((1, 128))],
        out_specs=[pl.BlockSpec((bsz, SUB * CHUNK, 512), lambda n: (0, n, 0)),
                   pl.BlockSpec((SUB, nh, DV, DV), lambda n: (n, 0, 0, 0)),
                   pl.BlockSpec((SUB, nh, CHUNK, CHUNK), lambda n: (n, 0, 0, 0))],
        out_shape=[jax.ShapeDtypeStruct((bsz, t, 512), MXU), jax.ShapeDtypeStruct((nc, nh, DV, DV), F32),
                   jax.ShapeDtypeStruct((nc, nh, CHUNK, CHUNK), MXU)],
        scratch_shapes=[pltpu.VMEM((nh, DV, DV), F32)],
        compiler_params=_params(("arbitrary",)),
    )(conv, conv, conv, pd, pd, sc, nw)


def _gdn_bwd_call(conv, pd, s_hist, t_hist, dyin, sc, nw):
    bsz, t, _ = conv.shape
    nc = t // CHUNK
    nh = bsz * H
    n_of, specs = _chunk_specs(nc, True, bsz, _GDN_COLS, SUB_GDN_BWD)

    def body(q_ref, k_ref, v_ref, og_ref, ab_ref, sh_ref, th_ref, dy_ref, sc_ref, nw_ref,
             dc_ref, dpd_ref, dsc_ref, dnw_ref, ds_ref):
        @pl.when(pl.program_id(0) == 0)
        def _():
            dsc_ref[...] = jnp.zeros_like(dsc_ref)
            dnw_ref[...] = jnp.zeros_like(dnw_ref)
            ds_ref[...] = jnp.zeros_like(ds_ref)

        lane = lax.broadcasted_iota(jnp.int32, (CHUNK, 128), 1)
        ds = ds_ref[...]
        for sub in reversed(range(SUB_GDN_BWD)):
            rows = _chunk_rows(sub)
            q, k, v, a, bb, og, alog, dtb, nw_ = _gdn_args((q_ref, k_ref, v_ref, og_ref, ab_ref, sc_ref, nw_ref), bsz, rows)
            _, vjp = jax.vjp(functools.partial(gdn_chunk, tinv=th_ref[sub]), q, k, v, a, bb, og, sh_ref[sub], alog, dtb, nw_)
            dq, dk, dv, da, db, dog, ds, dalog, ddtb, dnw = vjp((_heads(dy_ref, bsz, rows, 128), ds))
            dnw_ref[...] += dnw
            for b in range(bsz):
                dab = jnp.zeros((CHUNK, 128), F32)
                for h in range(H):
                    i = H * b + h
                    dc_ref[b, rows, 128 * h:128 * h + 128] = dq[i]
                    dc_ref[b, rows, 512 + 128 * h:512 + 128 * h + 128] = dk[i]
                    dc_ref[b, rows, 1024 + 128 * h:1024 + 128 * h + 128] = dv[i]
                    dpd_ref[b, rows, 128 * h:128 * h + 128] = dog[i].astype(MXU)
                    dab = dab + jnp.where(lane == h, da[i], 0.0) + jnp.where(lane == H + h, db[i], 0.0)
                    dsc_ref[0:1, h:h + 1] += dalog[i]
                    dsc_ref[1:2, h:h + 1] += ddtb[i]
                dpd_ref[b, rows, 512:640] = dab.astype(MXU)
        ds_ref[...] = ds

    return pl.pallas_call(
        body, name="gdn_bwd", grid=(nc // SUB_GDN_BWD,),
        in_specs=specs + [pl.BlockSpec((SUB_GDN_BWD, nh, DV, DV), lambda n: (n_of(n), 0, 0, 0)),
                          pl.BlockSpec((SUB_GDN_BWD, nh, CHUNK, CHUNK), lambda n: (n_of(n), 0, 0, 0)),
                          pl.BlockSpec((bsz, SUB_GDN_BWD * CHUNK, 512), lambda n: (0, n_of(n), 1)),
                          _full((2, 128)), _full((1, 128))],
        out_specs=[pl.BlockSpec((bsz, SUB_GDN_BWD * CHUNK, W_GQKV), lambda n: (0, n_of(n), 0)),
                   pl.BlockSpec((bsz, SUB_GDN_BWD * CHUNK, W_GDN), lambda n: (0, n_of(n), 0)),
                   _full((2, 128)), _full((1, 128))],
        out_shape=[jax.ShapeDtypeStruct(conv.shape, F32), jax.ShapeDtypeStruct(pd.shape, MXU),
                   jax.ShapeDtypeStruct((2, 128), F32), jax.ShapeDtypeStruct((1, 128), F32)],
        scratch_shapes=[pltpu.VMEM((nh, DV, DV), F32)],
        compiler_params=_params(("arbitrary",)),
    )(conv, conv, conv, pd, pd, s_hist, t_hist, dyin, sc, nw)


def _head_call(x, ya, yb, wout, mod3, lnw, lnb, tgt):
    bsz, t, _ = x.shape
    tm = min(512, t)
    rows = min(256, tm)

    def body(x_ref, ya_ref, yb_ref, w_ref, mod_ref, lnw_ref, lnb_ref, t_ref,
             dyin_ref, dxa_ref, dgate_ref, dw_ref, dlnw_ref, dlnb_ref, loss_ref):
        b, i = pl.program_id(0), pl.program_id(1)

        @pl.when((b == 0) & (i == 0))
        def _():
            dw_ref[...] = jnp.zeros_like(dw_ref)
            dlnw_ref[...] = jnp.zeros_like(dlnw_ref)
            dlnb_ref[...] = jnp.zeros_like(dlnb_ref)
            loss_ref[...] = jnp.zeros_like(loss_ref)

        @pl.when(i == 0)
        def _():
            dgate_ref[...] = jnp.zeros_like(dgate_ref)

        w = w_ref[...]
        parts = [slice(p * rows, (p + 1) * rows) for p in range(tm // rows)]
        yin = [jnp.concatenate([ya_ref[0, rs, :], yb_ref[0, rs, :]], axis=-1).astype(MXU) for rs in parts]
        y = [jnp.dot(yi, w, preferred_element_type=F32) for yi in yin]
        for rs, yi, y_p in zip(parts, yin, y):
            loss, vjp = jax.vjp(head_fn, x_ref[0, rs, :], y_p, mod_ref[0, 2:3, :], lnw_ref[...], lnb_ref[...], t_ref[0, rs, :])
            dx, dy, dgate, dlnw, dlnb, _ = vjp(jnp.ones((1, 1), F32))
            dyb = dy.astype(MXU)
            dyin_ref[0, rs, :] = lax.dot_general(dyb, w, (((1,), (1,)), ((), ())), preferred_element_type=F32)
            dw_ref[...] += lax.dot_general(yi, dyb, (((0,), (0,)), ((), ())), preferred_element_type=F32)
            dxa_ref[0, rs, :] = dx
            dgate_ref[0] += dgate
            dlnw_ref[...] += dlnw
            dlnb_ref[...] += dlnb
            loss_ref[...] += jnp.broadcast_to(loss, (1, 128))

    tok = lambda w, j=0: pl.BlockSpec((1, tm, w), lambda b, i: (b, i, j))
    row = pl.BlockSpec((1, D), lambda b, i: (0, 0))
    return pl.pallas_call(
        body, name="head", grid=(bsz, t // tm),
        in_specs=[tok(D), tok(512), tok(512), pl.BlockSpec((D, D), lambda b, i: (0, 0)),
                  pl.BlockSpec((1, 3, D), lambda b, i: (b, 0, 0)), row, row, tok(D)],
        out_specs=[tok(D), tok(D), pl.BlockSpec((1, 1, D), lambda b, i: (b, 0, 0)),
                   pl.BlockSpec((D, D), lambda b, i: (0, 0)), row, row, pl.BlockSpec((1, 128), lambda b, i: (0, 0))],
        out_shape=[jax.ShapeDtypeStruct(x.shape, F32), jax.ShapeDtypeStruct(x.shape, F32),
                   jax.ShapeDtypeStruct((bsz, 1, D), F32), jax.ShapeDtypeStruct((D, D), F32),
                   jax.ShapeDtypeStruct((1, D), F32), jax.ShapeDtypeStruct((1, D), F32),
                   jax.ShapeDtypeStruct((1, 128), F32)],
        compiler_params=_params(("arbitrary", "arbitrary")),
    )(x, ya, yb, wout, mod3, lnw, lnb, tgt)


def _dh_call(dpg, dpq, dpd, wpt, x, mod3, dxa, smalls):
    bsz, t, _ = x.shape
    tm = min(512, t)
    assert bsz + 1 <= 8

    def body(dg_ref, dq_ref, dd_ref, w_ref, x_ref, mod_ref, dxa_ref, dgate_ref, dlnw_ref, dlnb_ref, dbg_ref, n1_ref, n2_ref,
             dsc_ref, loss_ref, gx_ref, sp_ref):
        b = pl.program_id(0)

        @pl.when((b == 0) & (pl.program_id(1) == 0))
        def _():
            sp_ref[...] = jnp.zeros_like(sp_ref)
            for e in range(bsz):
                sp_ref[e:e + 1, 2 * D:3 * D] = dgate_ref[e]
            off = 0
            for ref in (dlnw_ref, dlnb_ref, dbg_ref, n1_ref, n2_ref):
                sp_ref[bsz:bsz + 1, off:off + ref.shape[1]] = ref[...]
                off += ref.shape[1]
            sp_ref[bsz:bsz + 1, off:off + 128] = dsc_ref[0:1, :]
            sp_ref[bsz:bsz + 1, off + 128:off + 256] = dsc_ref[1:2, :]
            sp_ref[bsz:bsz + 1, SMALL_W:SMALL_W + 128] = loss_ref[...]

        mm = lambda a, lo, hi: jnp.dot(a.astype(MXU), w_ref[lo:hi, :], preferred_element_type=F32)
        dh = mm(dg_ref[0], 0, W_GLA) + mm(dq_ref[0], W_GLA, W_GLA + W_GQKV) + mm(dd_ref[0], W_GLA + W_GQKV, PW)
        gx_ref[0] = dh * (1.0 + mod_ref[0, 1:2, :]) + dxa_ref[0]
        dshift = jnp.sum(dh, axis=0, keepdims=True)
        dscale = jnp.sum(dh * x_ref[0], axis=0, keepdims=True)
        for e in range(bsz):
            @pl.when(b == e)
            def _():
                sp_ref[e:e + 1, 0:D] += dshift
                sp_ref[e:e + 1, D:2 * D] += dscale

    tok = lambda w: pl.BlockSpec((1, tm, w), lambda b, i: (b, i, 0))
    whole = lambda a: pl.BlockSpec(a.shape, lambda b, i: (0,) * a.ndim)
    return pl.pallas_call(
        body, name="dh", grid=(bsz, t // tm),
        in_specs=[tok(W_GLA), tok(W_GQKV), tok(W_GDN), pl.BlockSpec((PW, D), lambda b, i: (0, 0)), tok(D),
                  pl.BlockSpec((1, 3, D), lambda b, i: (b, 0, 0)), tok(D)] + [whole(a) for a in smalls],
        out_specs=[tok(D), pl.BlockSpec((8, SPW), lambda b, i: (0, 0))],
        out_shape=[jax.ShapeDtypeStruct(x.shape, F32), jax.ShapeDtypeStruct((8, SPW), F32)],
        compiler_params=_params(("arbitrary", "arbitrary")),
    )(dpg, dpq, dpd, wpt, x, mod3, dxa, *smalls)


def _dw_call(x, mod3, dpg, dpq, dpd):
    bsz, t, _ = x.shape
    tm = min(512, t)
    nsteps = bsz * (t // tm)

    def body(x_ref, mod_ref, dg_ref, dq_ref, dd_ref, dw_ref, acc):
        step = pl.program_id(0) * (t // tm) + pl.program_id(1)

        @pl.when(step == 0)
        def _():
            acc[...] = jnp.zeros_like(acc)

        h = (x_ref[0] * (1.0 + mod_ref[0, 1:2, :]) + mod_ref[0, 0:1, :]).astype(MXU)
        for ref, lo, hi in ((dg_ref, 0, W_GLA), (dq_ref, W_GLA, W_GLA + W_GQKV), (dd_ref, W_GLA + W_GQKV, PW)):
            acc[lo:hi, :] += lax.dot_general(ref[0].astype(MXU), h, (((0,), (0,)), ((), ())), preferred_element_type=F32)

        @pl.when(step == nsteps - 1)
        def _():
            dw_ref[...] = acc[...].astype(dw_ref.dtype)

    tok = lambda w: pl.BlockSpec((1, tm, w), lambda b, i: (b, i, 0))
    return pl.pallas_call(
        body, name="dw", grid=(bsz, t // tm),
        in_specs=[tok(D), pl.BlockSpec((1, 3, D), lambda b, i: (b, 0, 0)), tok(W_GLA), tok(W_GQKV), tok(W_GDN)],
        out_specs=pl.BlockSpec((PW, D), lambda b, i: (0, 0)),
        out_shape=jax.ShapeDtypeStruct((PW, D), WIRE),
        scratch_shapes=[pltpu.VMEM((PW, D), F32)],
        compiler_params=_params(("arbitrary", "arbitrary")),
    )(x, mod3, dpg, dpq, dpd)


def _adamw(w, g, m, v):
    m = ADAM_B1 * m + (1.0 - ADAM_B1) * g
    v = ADAM_B2 * v + (1.0 - ADAM_B2) * jnp.square(g)
    m_hat = m / (1.0 - ADAM_B1 ** ADAM_STEP)
    v_hat = v / (1.0 - ADAM_B2 ** ADAM_STEP)
    delta = -ADAM_LR * (m_hat / (jnp.sqrt(v_hat) + ADAM_EPS) + ADAM_WD * w)
    return delta, m, v


def _sum8(ref):
    g = ref[0].astype(F32)
    for j in range(1, NDEV):
        g = g + ref[j].astype(F32)
    return g


def _adam_sum_call(name, g8, w, m, v, cols):
    r, c = w.shape

    def body(g_ref, w_ref, m_ref, v_ref, go_ref, d_ref, mo_ref, vo_ref):
        g = _sum8(g_ref)
        go_ref[...] = g
        d_ref[...], mo_ref[...], vo_ref[...] = _adamw(w_ref[...], g, m_ref[...], v_ref[...])

    blk = pl.BlockSpec((r, cols), lambda i: (0, i))
    return pl.pallas_call(
        body, name=name, grid=(c // cols,),
        in_specs=[pl.BlockSpec((NDEV, r, cols), lambda i: (0, 0, i)), blk, blk, blk],
        out_specs=[blk] * 4, out_shape=[jax.ShapeDtypeStruct((r, c), F32)] * 4,
        compiler_params=_params(("parallel",)),
    )(g8, w, m, v)


def _adam_ada_call(c_all, dmod_cols, w, m, v):
    def body(c_ref, dm_ref, w_ref, m_ref, v_ref, go_ref, d_ref, mo_ref, vo_ref):
        g = lax.dot_general(c_ref[...].astype(MXU), dm_ref[...].astype(MXU), (((0,), (0,)), ((), ())),
                            preferred_element_type=F32)
        go_ref[...] = g
        d_ref[...], mo_ref[...], vo_ref[...] = _adamw(w_ref[...], g, m_ref[...], v_ref[...])

    return pl.pallas_call(
        body, name="adam_ada", out_shape=[jax.ShapeDtypeStruct(w.shape, F32)] * 4, compiler_params=_params(),
    )(c_all, dmod_cols, w, m, v)


_SMALL_AT = dict(ln_w=(0, 1024), ln_b=(1024, 1024), b_gate=(2048, 256), gla_nw=(2304, 128), gdn_nw=(2432, 128),
                 a_log=(2560, 4), dt_bias=(2688, 4))


def _adam_small_call(sp_all, bsz, params):
    names = list(params)

    def body(sp_ref, *refs):
        ins, outs = refs[:3 * len(names)], refs[3 * len(names):]
        dmod_ref, loss_ref, outs = outs[0], outs[1], outs[2:]
        packed = sp_ref[0, bsz:bsz + 1, :]
        for j in range(1, NDEV):
            packed = packed + sp_ref[j, bsz:bsz + 1, :]
        gb = None
        for j in range(NDEV):
            dmod_ref[bsz * j:bsz * j + bsz, :] = sp_ref[j, 0:bsz, :]
            for e in range(bsz):
                gb = sp_ref[j, e:e + 1, :] if gb is None else gb + sp_ref[j, e:e + 1, :]
        loss_ref[...] = packed[:, SMALL_W:SMALL_W + 128]
        for i, name in enumerate(names):
            if name == "b_ada":
                g = gb
            else:
                lo, n = _SMALL_AT[name]
                g = packed[:, lo:lo + n]
            w_ref, m_ref, v_ref = ins[3 * i:3 * i + 3]
            g_ref, d_ref, mo_ref, vo_ref = outs[4 * i:4 * i + 4]
            g_ref[...] = g
            d_ref[...], mo_ref[...], vo_ref[...] = _adamw(w_ref[...], g, m_ref[...], v_ref[...])

    flat = [a for name in names for a in params[name]]
    out_shape = [jax.ShapeDtypeStruct((NDEV * bsz, SPW), F32), jax.ShapeDtypeStruct((1, 128), F32)]
    out_shape += [jax.ShapeDtypeStruct(params[name][0].shape, F32) for name in names for _ in range(4)]
    res = pl.pallas_call(body, name="adam_small", out_shape=out_shape, compiler_params=_params())(sp_all, *flat)
    return res[0], res[1], {name: res[2 + 4 * i:6 + 4 * i] for i, name in enumerate(names)}


def _mesh_pos():
    x, y, c = lax.axis_index("x"), lax.axis_index("y"), lax.axis_index("c")
    return x, y, c, 4 * x + 2 * y + c


def _peer(x, y, c, k):
    px = 1 - x if k & 4 else x
    py = 1 - y if k & 2 else y
    pc = 1 - c if k & 1 else c
    return (px, py, pc), 4 * px + 2 * py + pc


_ANY = pl.BlockSpec(memory_space=pl.ANY)
_VMEM = pl.BlockSpec(memory_space=pltpu.VMEM)


def _gather_call(c8, w_ada, b_sh, w_in_t, conv_w):
    C_SEM, W_SEM, MOD_SEM, CONV_SEM = 0, 1, 2, 3

    def body(c_ref, wada_ref, b_ref, win_ref, cw_ref, wall_ref, call_ref, mod_ref, cwall_ref, modp, send_sems, recv_sems, loc_sem):
        x, y, c, me = _mesh_pos()

        def remote(src, dst, a, k, to):
            return pltpu.make_async_remote_copy(src_ref=src, dst_ref=dst, send_sem=send_sems.at[a, k],
                                                recv_sem=recv_sems.at[a, k], device_id=_peer(x, y, c, to)[0],
                                                device_id_type=pl.DeviceIdType.MESH)

        idx = lambda k: _peer(x, y, c, k)[1]
        sends = []
        call_ref[me] = c_ref[...]
        cwall_ref[me] = cw_ref[...]
        for k in range(1, NDEV):
            sends.append(remote(c_ref, call_ref.at[me], C_SEM, k, k))
            sends[-1].start()
            sends.append(remote(cw_ref, cwall_ref.at[me], CONV_SEM, k, k))
            sends[-1].start()
        local = pltpu.make_async_copy(win_ref, wall_ref.at[me], loc_sem)
        local.start()
        for k in (1, 2, 4, 6):
            sends.append(remote(win_ref, wall_ref.at[me], W_SEM, k, k))
            sends[-1].start()
        for k in range(1, NDEV):
            remote(c_ref, call_ref.at[idx(k)], C_SEM, k, k).wait_recv()
        modp[...] = jnp.dot(call_ref[...].reshape(NDEV * 8, D).astype(MXU), wada_ref[...].astype(MXU),
                            preferred_element_type=F32) + b_ref[...]
        mod_ref[me] = modp[pl.ds(pl.multiple_of(me * 8, 8), 8), :]
        for k in range(1, NDEV):
            sends.append(remote(modp.at[pl.ds(pl.multiple_of(idx(k) * 8, 8), 8), :], mod_ref.at[me], MOD_SEM, k, k))
            sends[-1].start()
        for k in (2, 4, 6):
            remote(win_ref, wall_ref.at[idx(k)], W_SEM, k, k).wait_recv()
            sends.append(remote(wall_ref.at[idx(k)], wall_ref.at[idx(k)], W_SEM, k + 1, 1))
            sends[-1].start()
        for k in (1, 3, 5, 7):
            remote(win_ref, wall_ref.at[idx(k)], W_SEM, k, 1).wait_recv()
        for k in range(1, NDEV):
            remote(modp.at[pl.ds(0, 8), :], mod_ref.at[idx(k)], MOD_SEM, k, k).wait_recv()
            remote(cw_ref, cwall_ref.at[idx(k)], CONV_SEM, k, k).wait_recv()
        for cp in sends:
            cp.wait_send()
        local.wait()

    return pl.pallas_call(
        body, name="gather",
        out_shape=[jax.ShapeDtypeStruct((NDEV,) + w_in_t.shape, w_in_t.dtype), jax.ShapeDtypeStruct((NDEV, 8, D), F32),
                   jax.ShapeDtypeStruct((NDEV, 8, SHARD_ADA), F32), jax.ShapeDtypeStruct((NDEV,) + conv_w.shape, F32)],
        in_specs=[_VMEM, _VMEM, _VMEM, _ANY, _VMEM], out_specs=[_ANY, _VMEM, _VMEM, _VMEM],
        scratch_shapes=[pltpu.VMEM((NDEV * 8, SHARD_ADA), F32), pltpu.SemaphoreType.DMA((4, NDEV)),
                        pltpu.SemaphoreType.DMA((4, NDEV)), pltpu.SemaphoreType.DMA],
        compiler_params=_params(),
    )(c8, w_ada, b_sh, w_in_t, conv_w)


_HBM = pl.BlockSpec(memory_space=pltpu.HBM)
_SEM = pl.BlockSpec(memory_space=pltpu.SEMAPHORE)
_EFFECT = pltpu.SideEffectType.DATAFLOW_SIDE_EFFECTING


def _whole(gather, a):
    return gather[a] if isinstance(gather, (list, tuple)) else gather


def _xchg_start(name, blocks, lands, gather):
    nb = len(blocks)

    def body(*refs):
        srcs, dsts = refs[:nb], refs[nb:2 * nb]
        send_sems, recv_sems = refs[2 * nb], refs[2 * nb + 1]
        token = refs[-1]
        x, y, c, me = _mesh_pos()
        for k in range(1, NDEV):
            dev, pidx = _peer(x, y, c, k)
            for a in range(nb):
                pltpu.make_async_remote_copy(src_ref=srcs[a] if _whole(gather, a) else srcs[a].at[pidx], dst_ref=dsts[a].at[me],
                                             send_sem=send_sems.at[NDEV * a + k], recv_sem=recv_sems.at[NDEV * a + k],
                                             device_id=dev, device_id_type=pl.DeviceIdType.MESH).start()
        token[...] = jnp.zeros_like(token)

    thru = [pltpu.HBM(a.shape, a.dtype) for a in list(blocks) + list(lands)]
    return pl.pallas_call(
        body, name=name,
        out_shape=(pltpu.SemaphoreType.DMA((nb * NDEV,)), pltpu.SemaphoreType.DMA((nb * NDEV,)), *thru,
                   jax.ShapeDtypeStruct((8, 128), F32)),
        in_specs=[_HBM] * (2 * nb), out_specs=(_SEM, _SEM, *([_HBM] * (2 * nb)), _VMEM),
        input_output_aliases={i: 2 + i for i in range(2 * nb)},
        compiler_params=pltpu.CompilerParams(has_side_effects=_EFFECT),
    )(*[pltpu.with_memory_space_constraint(a, pltpu.HBM) for a in list(blocks) + list(lands)])


def _xchg_wait(name, send_sems, recv_sems, thru, after, gather):
    nb = len(thru) // 2

    def body(*refs):
        srcs, dsts = refs[:nb], refs[nb:2 * nb]
        send_sems, recv_sems = refs[2 * nb], refs[2 * nb + 1]
        x, y, c, me = _mesh_pos()
        for k in range(1, NDEV):
            dev, pidx = _peer(x, y, c, k)
            for a in range(nb):
                cp = pltpu.make_async_remote_copy(src_ref=srcs[a] if _whole(gather, a) else srcs[a].at[pidx], dst_ref=dsts[a].at[pidx],
                                                  send_sem=send_sems.at[NDEV * a + k], recv_sem=recv_sems.at[NDEV * a + k],
                                                  device_id=dev, device_id_type=pl.DeviceIdType.MESH)
                cp.wait_send()
                cp.wait_recv()

    out = pl.pallas_call(
        body, name=name, out_shape=tuple(pltpu.HBM(a.shape, a.dtype) for a in thru),
        in_specs=[_HBM] * (2 * nb) + [_SEM, _SEM, pl.BlockSpec(memory_space=pl.ANY)], out_specs=tuple([_HBM] * (2 * nb)),
        input_output_aliases={i: i for i in range(2 * nb)},
        compiler_params=pltpu.CompilerParams(has_side_effects=_EFFECT),
    )(*thru, send_sems, recv_sems, after)
    return out[nb:]


def _pad_cols(a, n):
    return jnp.pad(a, ((0, 0), (0, n - a.shape[1])))


_SEGMENTS = ((0, 256, 0), (256, 512, 256), (512, 1024, 512), (1040, 1552, 1024), (1024, 1040, 1536),
             (1552, 3088, W_GLA), (3096, 3608, W_GLA + W_GQKV), (3088, 3096, W_GLA + W_GQKV + 512))


def _row_pieces():
    out = []
    for lo, hi, dst in _SEGMENTS:
        while lo < hi:
            j, off = divmod(lo, SHARD_IN)
            n = min(hi - lo, SHARD_IN - off)
            out.append((j, off, n, dst))
            lo, dst = lo + n, dst + n
    return out


def _relayout_call(a, to_layout):
    cols = 256
    dst_shape = (PW, D) if to_layout else (NDEV, SHARD_IN, D)

    def body(i_ref, o_ref, scr):
        if to_layout:
            scr[...] = jnp.zeros_like(scr)
        for j, off, n, at in _row_pieces():
            if to_layout:
                scr[at:at + n, :] = i_ref[j, off:off + n, :].astype(F32)
            else:
                scr[j, off:off + n, :] = i_ref[at:at + n, :].astype(F32)
        o_ref[...] = scr[...].astype(o_ref.dtype)

    blk = lambda shape: pl.BlockSpec(shape[:-1] + (cols,), lambda i: (0,) * (len(shape) - 1) + (i,))
    return pl.pallas_call(
        body, name="to_layout" if to_layout else "to_shards", grid=(D // cols,),
        in_specs=[blk(a.shape)], out_specs=blk(dst_shape), out_shape=jax.ShapeDtypeStruct(dst_shape, a.dtype),
        scratch_shapes=[pltpu.VMEM(dst_shape[:-1] + (cols,), F32)],
        compiler_params=_params(("parallel",)),
    )(a)


def _assemble_wt(blocks):
    return _relayout_call(blocks, True)


def _disassemble_dwt(dwt):
    return _relayout_call(dwt, False)


def local_grads(x, mod3, wp, conv_w, late_weights, bg, gla_nw, sc, gdn_nw, lnw, lnb, tgt):
    pg, pq, pd, conv = _proj_call(x, mod3, wp, conv_w)
    wout, wgu_p = late_weights(pq)
    ya, s_gla = _gla_fwd_call(pg, wgu_p, bg, gla_nw)
    yb, s_gdn, t_gdn = _gdn_fwd_call(conv, pd, sc, gdn_nw)
    dyin, dxa, dgate, dwout, dlnw, dlnb, loss = _head_call(x, ya, yb, wout, mod3, lnw, lnb, tgt)
    dconv, dpd, dsc, dnw_gdn = _gdn_bwd_call(conv, pd, s_gdn, t_gdn, dyin, sc, gdn_nw)
    dpq, dconv_w = _conv_bwd_sc_call(dconv, pq, conv_w)
    dpg, dwgu, dbg, dnw_gla = _gla_bwd_call(pg, s_gla, dyin, wgu_p, bg, gla_nw)
    dw_in = _disassemble_dwt(_dw_call(x, mod3, dpg, dpq, dpd))
    g = dict(dw_in=dw_in, dwout=dwout, dconv_w=dconv_w, dwgu=dwgu[:16])
    smalls = (dgate, dlnw, dlnb, dbg, dnw_gla, dnw_gdn, dsc, loss)
    return g, lambda mod3_: _dh_call(dpg, dpq, dpd, wp, x, mod3_, dxa, smalls)


def local_step(x, mod3, wp, wout, conv_w, wgu_p, *args):
    g, finish = local_grads(x, mod3, wp, conv_w, lambda _: (wout, wgu_p), *args)
    g["gx"], sp = finish(mod3)
    bsz = x.shape[0]
    g["dmod"] = sp[:bsz].reshape(bsz, 3, D)
    g["loss"] = sp[bsz, SMALL_W]
    for name, (lo, n) in _SMALL_AT.items():
        g[name] = sp[bsz:bsz + 1, lo:lo + n]
    return g


def kernel(x, c, w_ada, b_ada, w_in, gla_w_gate_up, gla_b_gate, gla_norm_w, gdn_conv_w, gdn_a_log, gdn_dt_bias, gdn_norm_w, w_out, ln_w, ln_b, loss_target, m_w_ada, m_b_ada, m_w_in, m_gla_w_gate_up, m_gla_b_gate, m_gla_norm_w, m_gdn_conv_w, m_gdn_a_log, m_gdn_dt_bias, m_gdn_norm_w, m_w_out, m_ln_w, m_ln_b, v_w_ada, v_b_ada, v_w_in, v_gla_w_gate_up, v_gla_b_gate, v_gla_norm_w, v_gdn_conv_w, v_gdn_a_log, v_gdn_dt_bias, v_gdn_norm_w, v_w_out, v_ln_w, v_ln_b):
    me = 4 * lax.axis_index("x") + 2 * lax.axis_index("y") + lax.axis_index("c")
    bsz = x.shape[0]

    b_sh = lax.dynamic_slice(b_ada, (0, me * SHARD_ADA), (1, SHARD_ADA))
    c8 = jnp.pad(c, ((0, 8 - bsz), (0, 0)))
    w_in_t, m_in_t, v_in_t = (jnp.swapaxes(a[0], 0, 1) for a in (w_in, m_w_in, v_w_in))
    win_all, c_all, mod_blk, conv_all = _gather_call(c8, w_ada[0], b_sh, w_in_t.astype(WIRE), gdn_conv_w[0])
    conv_w = jnp.transpose(conv_all, (1, 0, 2)).reshape(4, W_GQKV)
    wp = _assemble_wt(win_all)
    mod = jnp.transpose(mod_blk[:, :bsz, :], (1, 0, 2)).reshape(bsz, 3 * D)
    mod3 = mod.reshape(bsz, 3, D)
    sc = jnp.concatenate([_pad_cols(gdn_a_log, 128), _pad_cols(gdn_dt_bias, 128)], axis=0)

    own = lambda a: lax.dynamic_update_slice(lax.empty((NDEV,) + a.shape, a.dtype), a[None], (me,) + (0,) * a.ndim)
    late = [w_out[0].astype(WIRE), gla_w_gate_up[0] + 0.0 * mod_blk[0, 0, 0]]
    w_send, w_recv, *w_thru, w_token = _xchg_start("wgather_start", late, [own(a) for a in late], gather=True)

    def late_weights(pq):
        wout_all, wgu_all = _xchg_wait("wgather_wait", w_send, w_recv, w_thru, pq, gather=True)
        return wout_all.reshape(D, D), jnp.pad(jnp.transpose(wgu_all, (1, 0, 2)).reshape(16, 256), ((0, 112), (0, 0)))

    g, finish = local_grads(x, mod3 + w_token[0, 0], wp, conv_w, late_weights, gla_b_gate, gla_norm_w, sc, gdn_norm_w, ln_w, ln_b,
                            loss_target)

    big = [g["dw_in"], g["dwout"].reshape(NDEV, D // NDEV, D).astype(WIRE)]
    lands = [lax.dynamic_update_slice(lax.empty(a.shape, a.dtype), lax.dynamic_slice(a, (me, 0, 0), (1,) + a.shape[1:]),
                                      (me, 0, 0)) for a in big]
    send_sems, recv_sems, *thru, token = _xchg_start("xchg_start", big, lands, gather=False)
    gx, sp = finish(mod3 + token[0, 0])
    little = [jnp.transpose(g["dconv_w"].reshape(4, NDEV, W_GQKV // NDEV), (1, 0, 2)),
              jnp.transpose(g["dwgu"].reshape(16, NDEV, 256 // NDEV), (1, 0, 2)), sp]
    modes = [False, False, True]
    l_lands = [lax.dynamic_update_slice(lax.empty(a.shape, a.dtype), lax.dynamic_slice(a, (me, 0, 0), (1,) + a.shape[1:]),
                                        (me, 0, 0)) for a in little[:2]] + [own(sp)]
    l_send, l_recv, *l_thru, l_token = _xchg_start("small_start", little, l_lands, gather=modes)
    r_in, r_out = _xchg_wait("xchg_wait", send_sems, recv_sems, thru, l_token, gather=False)

    t_in = [jnp.swapaxes(a, 0, 1) for a in _adam_sum_call("adam_in", r_in, w_in_t, m_in_t, v_in_t, 256)]
    t_out = _adam_sum_call("adam_out", r_out, w_out[0], m_w_out[0], v_w_out[0], D)
    r_conv, r_gu, sp_all = _xchg_wait("small_wait", l_send, l_recv, l_thru, t_out[3], gather=modes)
    t_conv = _adam_sum_call("adam_conv", r_conv, gdn_conv_w[0], m_gdn_conv_w[0], v_gdn_conv_w[0], W_GQKV // NDEV)
    t_gu = _adam_sum_call("adam_gu", r_gu, gla_w_gate_up[0], m_gla_w_gate_up[0], v_gla_w_gate_up[0], 256 // NDEV)
    dmod_all, loss, small = _adam_small_call(sp_all, bsz, dict(
        b_ada=(b_ada, m_b_ada, v_b_ada), ln_w=(ln_w, m_ln_w, v_ln_w), ln_b=(ln_b, m_ln_b, v_ln_b),
        b_gate=(gla_b_gate, m_gla_b_gate, v_gla_b_gate), gla_nw=(gla_norm_w, m_gla_norm_w, v_gla_norm_w),
        gdn_nw=(gdn_norm_w, m_gdn_norm_w, v_gdn_norm_w), a_log=(gdn_a_log, m_gdn_a_log, v_gdn_a_log),
        dt_bias=(gdn_dt_bias, m_gdn_dt_bias, v_gdn_dt_bias)))
    c16 = c_all[:, :bsz, :].reshape(NDEV * bsz, D)
    t_ada = _adam_ada_call(c16, lax.dynamic_slice(dmod_all, (0, me * SHARD_ADA), (NDEV * bsz, SHARD_ADA)),
                           w_ada[0], m_w_ada[0], v_w_ada[0])

    def group(i):
        s = lambda name: small[name][i]
        return [t_ada[i][None], s("b_ada"), t_in[i][None], t_gu[i][None], s("b_gate"), s("gla_nw"), t_conv[i][None],
                s("a_log"), s("dt_bias"), s("gdn_nw"), t_out[i][None], s("ln_w"), s("ln_b")]

    return (loss[0, 0], gx, *group(0), *group(1), *group(2), *group(3))
```

```python
import functools

import jax
import jax.numpy as jnp
from jax import lax
from jax.experimental import pallas as pl
from jax.experimental.pallas import tpu as pltpu
from jax.experimental.pallas import tpu_sc as plsc

F32 = jnp.float32
MXU = jnp.bfloat16
WIRE = jnp.bfloat16
HI = lax.Precision.HIGH

D = 1024
NDEV = 8
H = 4
GLA_DK = 64
DV = 128
CHUNK = 64
SUB = 8
SUB_GDN_BWD = 4
LN_EPS = 1e-5
RMS_EPS = 1e-6
ALPHA = 2.0 ** 0.25
GATE_NORM = 16.0

W_GLA, W_GQKV, W_GDN = 1664, 1536, 640
PW = W_GLA + W_GQKV + W_GDN
IN_COLS = 3608
SHARD_IN = IN_COLS // NDEV
SHARD_ADA = 3 * D // NDEV
SPW = 3 * D
SMALL_W = 2816

ADAM_LR, ADAM_B1, ADAM_B2, ADAM_EPS, ADAM_WD, ADAM_STEP = 0.001, 0.9, 0.999, 1e-08, 0.01, 10

VMEM_LIMIT = 56 * 1024 * 1024


def _params(sem=None, **kw):
    if sem is not None:
        kw["dimension_semantics"] = sem
    return pltpu.CompilerParams(vmem_limit_bytes=VMEM_LIMIT, **kw)


_MM = (((2,), (1,)), ((0,), (0,)))
_NT = (((2,), (2,)), ((0,), (0,)))
_TN = (((1,), (1,)), ((0,), (0,)))


def _dg(a, b, dims):
    return lax.dot_general(a.astype(MXU), b.astype(MXU), dims, preferred_element_type=F32)


def _hdg(a, b, dims):
    return lax.dot_general(a, b, dims, precision=HI, preferred_element_type=F32)


@jax.custom_vjp
def bmm(a, b):
    return _dg(a, b, _MM)


bmm.defvjp(lambda a, b: (_dg(a, b, _MM), (a, b)), lambda r, g: (_dg(g, r[1], _NT), _dg(r[0], g, _TN)))


@jax.custom_vjp
def bnt(a, b):
    return _dg(a, b, _NT)


bnt.defvjp(lambda a, b: (_dg(a, b, _NT), (a, b)), lambda r, g: (_dg(g, r[1], _MM), _dg(g, r[0], _TN)))


@jax.custom_vjp
def btn(a, b):
    return _dg(a, b, _TN)


btn.defvjp(lambda a, b: (_dg(a, b, _TN), (a, b)), lambda r, g: (_dg(r[1], g, _NT), _dg(r[0], g, _MM)))


def unit_lower_inverse(a):
    n = a.shape[-1]
    r, c = _iotas(n)
    p = -a
    t = (r == c).astype(F32) + p
    for _ in range(5):
        p = _dg(p, p, _MM)
        t = t + _dg(t, p, _MM)
    return t


@jax.custom_vjp
def unit_lower_solve(a, t, r1, r2):
    return _dg(t, r1, _MM), _dg(t, r2, _MM)


def _solve_fwd(a, t, r1, r2):
    s1, s2 = _dg(t, r1, _MM), _dg(t, r2, _MM)
    return (s1, s2), (t, s1, s2)


def _solve_bwd(res, g):
    t, s1, s2 = res
    d1, d2 = _dg(t, g[0], _TN), _dg(t, g[1], _TN)
    return -(_dg(d1, s1, _NT) + _dg(d2, s2, _NT)), jnp.zeros_like(t), d1, d2


unit_lower_solve.defvjp(_solve_fwd, _solve_bwd)


def _iotas(n):
    return lax.broadcasted_iota(jnp.int32, (n, n), 0), lax.broadcasted_iota(jnp.int32, (n, n), 1)


def _col_to_row(col, eye):
    return jnp.sum(jnp.where(eye, col, 0.0), axis=1, keepdims=True)


def _row_to_col(row, eye):
    return jnp.sum(jnp.where(eye, row, 0.0), axis=2, keepdims=True)


def _pick_row(m, i):
    r = lax.broadcasted_iota(jnp.int32, m.shape, 1)
    return jnp.sum(jnp.where(r == i, m, 0.0), axis=1, keepdims=True)


def _rms_gate(o, nw, og):
    on = o * lax.rsqrt(jnp.mean(o * o, axis=-1, keepdims=True) + RMS_EPS) * nw
    return on * jax.nn.silu(og)


def gla_chunk(q, k, v, lr, og, s, wgu, bg, nw):
    n, c, _ = q.shape
    r, cc = _iotas(c)
    causal = r >= cc
    qs = q * (GLA_DK ** -0.5)
    z = bmm(lr, wgu) + bg
    g = jax.nn.log_sigmoid(z) / GATE_NORM
    b = _hdg(jnp.broadcast_to(causal.astype(F32), (n, c, c)), g, _MM)
    bref = _pick_row(b, c // 2 - 1)
    blast = _pick_row(b, c - 1)
    att = jnp.where(causal, bnt(qs * jnp.exp(b - bref), k * jnp.exp(bref - b)), 0.0)
    o = bmm(att, v) + bmm(qs * jnp.exp(b), s)
    rk, ck = _iotas(GLA_DK)
    s_new = _row_to_col(jnp.exp(blast), rk == ck) * s + btn(k * jnp.exp(blast - b), v)
    return _rms_gate(o, nw, og), s_new


def gdn_chunk(cq, ck, cv, a, bb, og, s, alog, dtb, nw, tinv=None):
    c = cq.shape[1]
    r, cc = _iotas(c)
    eye, causal, strict = r == cc, r >= cc, r > cc
    q, k, v = jax.nn.silu(cq), jax.nn.silu(ck), jax.nn.silu(cv)
    q = q * lax.rsqrt(jnp.sum(q * q, axis=-1, keepdims=True) + RMS_EPS) * (DV ** -0.5)
    k = k * lax.rsqrt(jnp.sum(k * k, axis=-1, keepdims=True) + RMS_EPS)
    g = -jnp.exp(alog) * jax.nn.softplus(a + dtb)
    beta = jax.nn.sigmoid(bb)
    d = jnp.sum(jnp.where(causal, _col_to_row(g, eye), 0.0), axis=2, keepdims=True)
    el = jnp.exp(jnp.where(causal, d - _col_to_row(d, eye), -jnp.inf))
    kb = k * beta
    amat = jnp.where(strict, bnt(kb, k) * el, 0.0)
    t = unit_lower_inverse(amat) if tinv is None else tinv
    u, w = unit_lower_solve(amat, t, v * beta, kb * jnp.exp(d))
    qk = jnp.where(causal, bnt(q, k) * el, 0.0)
    dlast = _pick_row(d, c - 1)
    v_new = u - bmm(w, s)
    o = bmm(q * jnp.exp(d), s) + bmm(qk, v_new)
    s_new = jnp.exp(dlast) * s + btn(k * jnp.exp(dlast - d), v_new)
    y = _rms_gate(o, nw, og)
    return (y, s_new, t) if tinv is None else (y, s_new)


def head_fn(x, y, gate, lnw, lnb, tgt):
    u = ALPHA * x + (1.0 + gate) * y
    mu = jnp.mean(u, axis=-1, keepdims=True)
    var = jnp.mean(jnp.square(u - mu), axis=-1, keepdims=True)
    out = (u - mu) * lax.rsqrt(var + LN_EPS) * lnw + lnb
    err = jnp.square(out - tgt)
    return 0.5 * jnp.sum(jnp.mean(err, axis=-1, keepdims=True), axis=0, keepdims=True)


def _proj_call(x, mod3, wpt, conv_w):
    bsz, t, _ = x.shape
    tm = min(512, t)

    def body(x_ref, mod_ref, w_ref, cw_ref, pg_ref, pq_ref, pd_ref, conv_ref, buf):
        i = pl.program_id(1)
        h = (x_ref[0] * (1.0 + mod_ref[0, 1:2, :]) + mod_ref[0, 0:1, :]).astype(MXU)
        nt = lambda lo, hi: lax.dot_general(h, w_ref[lo:hi, :], (((1,), (1,)), ((), ())), preferred_element_type=F32)
        pq = nt(W_GLA, W_GLA + W_GQKV)
        pq_ref[0] = pq
        @pl.when(i == 0)
        def _():
            buf[0:8, :] = jnp.zeros((8, W_GQKV), F32)

        @pl.when(i > 0)
        def _():
            buf[0:8, :] = buf[tm:tm + 8, :]

        buf[8:, :] = pq
        acc = cw_ref[0:1, :] * buf[pl.ds(5, tm), :]
        for k in range(1, 4):
            acc = acc + cw_ref[k:k + 1, :] * buf[pl.ds(5 + k, tm), :]
        conv_ref[0] = acc
        pg_ref[0] = nt(0, W_GLA)
        pd_ref[0] = nt(W_GLA + W_GQKV, PW)

    tok = lambda w: pl.BlockSpec((1, tm, w), lambda b, i: (b, i, 0))
    return pl.pallas_call(
        body, name="proj", grid=(bsz, t // tm),
        in_specs=[tok(D), pl.BlockSpec((1, 3, D), lambda b, i: (b, 0, 0)), pl.BlockSpec((PW, D), lambda b, i: (0, 0)),
                  pl.BlockSpec((4, W_GQKV), lambda b, i: (0, 0))],
        out_specs=[tok(W_GLA), tok(W_GQKV), tok(W_GDN), tok(W_GQKV)],
        out_shape=[jax.ShapeDtypeStruct((bsz, t, w), F32) for w in (W_GLA, W_GQKV, W_GDN, W_GQKV)],
        scratch_shapes=[pltpu.VMEM((tm + 8, W_GQKV), F32)],
        compiler_params=_params(("arbitrary", "arbitrary")),
    )(x, mod3, wpt, conv_w)


def _conv_bwd_call(dconv, pq, conv_w):
    bsz, t, _ = pq.shape
    tt = min(512, t)
    hb = tt // 8
    nt_ = t // tt

    def body(d_ref, dnext_ref, x_ref, w_ref, din_ref, dw_ref, dbuf):
        b, i = pl.program_id(0), pl.program_id(1)

        @pl.when((b == 0) & (i == 0))
        def _():
            dw_ref[...] = jnp.zeros_like(dw_ref)

        dbuf[0:tt, :] = d_ref[0]
        dbuf[tt:, :] = jnp.where(i < nt_ - 1, dnext_ref[0], 0.0)
        rows, unroll = 16, 4
        for j in range(W_GQKV // 128):
            ls = slice(128 * j, 128 * j + 128)
            wj = [jnp.broadcast_to(w_ref[k:k + 1, ls], (rows, 128)) for k in range(4)]

            def trip(r, dw4):
                dw4 = list(dw4)
                for u in range(unroll):
                    base = pl.multiple_of(r * (rows * unroll), rows * unroll) + rows * u
                    win = dbuf[pl.ds(base, rows + 8), ls]
                    xin = x_ref[0, pl.ds(base, rows), ls]
                    acc = None
                    for k in range(4):
                        dsh = win[3 - k:3 - k + rows, :]
                        acc = wj[k] * dsh if acc is None else acc + wj[k] * dsh
                        dw4[k] = dw4[k] + xin * dsh
                    din_ref[0, pl.ds(base, rows), ls] = acc
                return tuple(dw4)

            dw4 = lax.fori_loop(0, tt // (rows * unroll), trip, tuple(jnp.zeros((rows, 128), F32) for _ in range(4)))
            for k in range(4):
                dw_ref[k:k + 1, ls] += jnp.sum(dw4[k], axis=0, keepdims=True)

    tile = pl.BlockSpec((1, tt, W_GQKV), lambda b, i: (b, i, 0))
    return pl.pallas_call(
        body, name="conv_bwd", grid=(bsz, nt_),
        in_specs=[tile, pl.BlockSpec((1, 8, W_GQKV), lambda b, i: (b, jnp.minimum((i + 1) * hb, t // 8 - 1), 0)),
                  tile, pl.BlockSpec((4, W_GQKV), lambda b, i: (0, 0))],
        out_specs=[tile, pl.BlockSpec((8, W_GQKV), lambda b, i: (0, 0))],
        out_shape=[jax.ShapeDtypeStruct(pq.shape, F32), jax.ShapeDtypeStruct((8, W_GQKV), F32)],
        scratch_shapes=[pltpu.VMEM((tt + 8, W_GQKV), F32)],
        compiler_params=_params(("arbitrary", "arbitrary")),
    )(dconv, dconv, pq, conv_w)


SC_TILES = 32
SC_ROWS = 16


def _conv_bwd_sc_call(dconv, pq, conv_w):
    bsz, t, w_ = pq.shape
    n = bsz * t
    per_tile = n // SC_TILES
    pieces = per_tile // SC_ROWS
    R = SC_ROWS
    assert t % per_tile == 0 and per_tile % R == 0

    def body(d_hbm, x_hbm, w_hbm, din_hbm, dwp_hbm, dbuf, xbuf, obuf, wbuf, dwacc):
        tile = lax.axis_index("sc_tile") * 2 + lax.axis_index("sc_core")
        pltpu.sync_copy(w_hbm, wbuf)

        @pl.loop(0, w_, step=16)
        def _(c):
            for k in range(8):
                dwacc[k, pl.ds(c, 16)] = jnp.zeros((16,), F32)

        @pl.loop(0, pieces)
        def _(p):
            r0 = pl.multiple_of(tile * per_tile + p * R, R)
            pltpu.sync_copy(d_hbm.at[pl.ds(r0, R), :], dbuf.at[pl.ds(0, R), :])
            pltpu.sync_copy(x_hbm.at[pl.ds(r0, R), :], xbuf)
            nxt = pl.multiple_of(jnp.minimum(r0 + R, n - 8), 8)
            pltpu.sync_copy(d_hbm.at[pl.ds(nxt, 8), :], dbuf.at[pl.ds(R, 8), :])

            @pl.when((r0 + R) % t == 0)
            def _():
                @pl.loop(0, w_, step=16)
                def _(c):
                    for k in range(3):
                        dbuf[R + k, pl.ds(c, 16)] = jnp.zeros((16,), F32)

            @pl.loop(0, w_, step=16)
            def _(c):
                cs = pl.ds(c, 16)
                w0, w1, w2, w3 = (wbuf[k, cs] for k in range(4))
                a, b, cc = dbuf[0, cs], dbuf[1, cs], dbuf[2, cs]
                s0, s1, s2, s3 = (dwacc[k, cs] for k in range(4))
                for s in range(R):
                    e = dbuf[s + 3, cs]
                    xv = xbuf[s, cs]
                    obuf[s, cs] = w3 * a + w2 * b + w1 * cc + w0 * e
                    s0, s1, s2, s3 = s0 + xv * e, s1 + xv * cc, s2 + xv * b, s3 + xv * a
                    a, b, cc = b, cc, e
                dwacc[0, cs], dwacc[1, cs], dwacc[2, cs], dwacc[3, cs] = s0, s1, s2, s3

            pltpu.sync_copy(obuf, din_hbm.at[pl.ds(r0, R), :])

        pltpu.sync_copy(dwacc, dwp_hbm.at[pl.ds(pl.multiple_of(tile * 8, 8), 8), :])

    din, dwp = pl.kernel(
        body, name="conv_bwd_sc",
        out_type=(jax.ShapeDtypeStruct((n, w_), F32), jax.ShapeDtypeStruct((SC_TILES * 8, w_), F32)),
        mesh=plsc.VectorSubcoreMesh(core_axis_name="sc_core", subcore_axis_name="sc_tile"),
        scratch_types=[pltpu.VMEM((R + 8, w_), F32), pltpu.VMEM((R, w_), F32), pltpu.VMEM((R, w_), F32),
                       pltpu.VMEM((4, w_), F32), pltpu.VMEM((8, w_), F32)],
    )(dconv.reshape(n, w_), pq.reshape(n, w_), conv_w)
    return din.reshape(pq.shape), jnp.sum(dwp.reshape(SC_TILES, 8, w_)[:, :4], axis=0)


def _chunk_specs(nc, rev, bsz, cols, sub=None):
    sub = SUB if sub is None else sub
    steps = nc // sub
    n_of = (lambda n: steps - 1 - n) if rev else (lambda n: n)
    return n_of, [pl.BlockSpec((bsz, sub * CHUNK, w), lambda n, j=j: (0, n_of(n), j)) for w, j in cols]


def _full(shape):
    return pl.BlockSpec(shape, lambda n: (0,) * len(shape))


def _heads(ref, bsz, rows, width, off=0):
    return jnp.stack([ref[b, rows, off + width * h:off + width * (h + 1)] for b in range(bsz) for h in range(H)])


def _chunk_rows(sub):
    return slice(CHUNK * sub, CHUNK * (sub + 1))


def _per_head(ref, bsz, width, rows=slice(None)):
    return jnp.stack([ref[rows, width * h:width * (h + 1)] for _ in range(bsz) for h in range(H)])


_GLA_COLS = [(256, 0), (256, 1), (512, 1), (512, 2), (128, 12)]


def _gla_args(refs, bsz, rows):
    q_ref, k_ref, v_ref, og_ref, lr_ref, wgu_ref, bg_ref, nw_ref = refs
    lr = jnp.stack([lr_ref[b, rows, :] for b in range(bsz) for _ in range(H)])
    return (_heads(q_ref, bsz, rows, 64), _heads(k_ref, bsz, rows, 64), _heads(v_ref, bsz, rows, 128), lr,
            _heads(og_ref, bsz, rows, 128), _per_head(wgu_ref, bsz, 64), _per_head(bg_ref, bsz, 64), nw_ref[...])


def _gla_fwd_call(pg, wgu, bg, nw):
    bsz, t, _ = pg.shape
    nc = t // CHUNK
    nh = bsz * H
    _, specs = _chunk_specs(nc, False, bsz, _GLA_COLS)

    def body(q_ref, k_ref, v_ref, og_ref, lr_ref, wgu_ref, bg_ref, nw_ref, y_ref, sh_ref, s_ref):
        @pl.when(pl.program_id(0) == 0)
        def _():
            s_ref[...] = jnp.zeros_like(s_ref)

        s = s_ref[...]
        for sub in range(SUB):
            rows = _chunk_rows(sub)
            q, k, v, lr, og, w, b_, nw_ = _gla_args((q_ref, k_ref, v_ref, og_ref, lr_ref, wgu_ref, bg_ref, nw_ref), bsz, rows)
            sh_ref[sub] = s
            y, s = gla_chunk(q, k, v, lr, og, s, w, b_, nw_)
            for b in range(bsz):
                for h in range(H):
                    y_ref[b, rows, 128 * h:128 * h + 128] = y[H * b + h].astype(MXU)
        s_ref[...] = s

    return pl.pallas_call(
        body, name="gla_fwd", grid=(nc // SUB,),
        in_specs=specs + [_full((128, 256)), _full((1, 256)), _full((1, 128))],
        out_specs=[pl.BlockSpec((bsz, SUB * CHUNK, 512), lambda n: (0, n, 0)),
                   pl.BlockSpec((SUB, nh, GLA_DK, DV), lambda n: (n, 0, 0, 0))],
        out_shape=[jax.ShapeDtypeStruct((bsz, t, 512), MXU), jax.ShapeDtypeStruct((nc, nh, GLA_DK, DV), F32)],
        scratch_shapes=[pltpu.VMEM((nh, GLA_DK, DV), F32)],
        compiler_params=_params(("arbitrary",)),
    )(pg, pg, pg, pg, pg, wgu, bg, nw)


def _gla_bwd_call(pg, s_hist, dyin, wgu, bg, nw):
    bsz, t, _ = pg.shape
    nc = t // CHUNK
    nh = bsz * H
    n_of, specs = _chunk_specs(nc, True, bsz, _GLA_COLS)

    def body(q_ref, k_ref, v_ref, og_ref, lr_ref, sh_ref, dy_ref, wgu_ref, bg_ref, nw_ref,
             dp_ref, dwgu_ref, dbg_ref, dnw_ref, ds_ref):
        @pl.when(pl.program_id(0) == 0)
        def _():
            dwgu_ref[...] = jnp.zeros_like(dwgu_ref)
            dbg_ref[...] = jnp.zeros_like(dbg_ref)
            dnw_ref[...] = jnp.zeros_like(dnw_ref)
            ds_ref[...] = jnp.zeros_like(ds_ref)

        ds = ds_ref[...]
        for sub in reversed(range(SUB)):
            rows = _chunk_rows(sub)
            q, k, v, lr, og, w, b_, nw_ = _gla_args((q_ref, k_ref, v_ref, og_ref, lr_ref, wgu_ref, bg_ref, nw_ref), bsz, rows)
            _, vjp = jax.vjp(gla_chunk, q, k, v, lr, og, sh_ref[sub], w, b_, nw_)
            dq, dk, dv, dlr, dog, ds, dwgu, dbg, dnw = vjp((_heads(dy_ref, bsz, rows, 128), ds))
            dnw_ref[...] += dnw
            for b in range(bsz):
                for h in range(H):
                    i = H * b + h
                    dp_ref[b, rows, 512 + 128 * h:512 + 128 * h + 128] = dv[i].astype(MXU)
                    dp_ref[b, rows, 1024 + 128 * h:1024 + 128 * h + 128] = dog[i].astype(MXU)
                    dwgu_ref[:, 64 * h:64 * h + 64] += dwgu[i]
                    dbg_ref[:, 64 * h:64 * h + 64] += dbg[i]
                for j in range(H // 2):
                    dp_ref[b, rows, 128 * j:128 * j + 128] = jnp.concatenate(
                        [dq[H * b + 2 * j], dq[H * b + 2 * j + 1]], axis=-1).astype(MXU)
                    dp_ref[b, rows, 256 + 128 * j:256 + 128 * j + 128] = jnp.concatenate(
                        [dk[H * b + 2 * j], dk[H * b + 2 * j + 1]], axis=-1).astype(MXU)
                dp_ref[b, rows, 1536:1664] = (dlr[H * b] + dlr[H * b + 1] + dlr[H * b + 2] + dlr[H * b + 3]).astype(MXU)
        ds_ref[...] = ds

    return pl.pallas_call(
        body, name="gla_bwd", grid=(nc // SUB,),
        in_specs=specs + [pl.BlockSpec((SUB, nh, GLA_DK, DV), lambda n: (n_of(n), 0, 0, 0)),
                          pl.BlockSpec((bsz, SUB * CHUNK, 512), lambda n: (0, n_of(n), 0)),
                          _full((128, 256)), _full((1, 256)), _full((1, 128))],
        out_specs=[pl.BlockSpec((bsz, SUB * CHUNK, W_GLA), lambda n: (0, n_of(n), 0)),
                   _full((128, 256)), _full((1, 256)), _full((1, 128))],
        out_shape=[jax.ShapeDtypeStruct(pg.shape, MXU), jax.ShapeDtypeStruct((128, 256), F32),
                   jax.ShapeDtypeStruct((1, 256), F32), jax.ShapeDtypeStruct((1, 128), F32)],
        scratch_shapes=[pltpu.VMEM((nh, GLA_DK, DV), F32)],
        compiler_params=_params(("arbitrary",)),
    )(pg, pg, pg, pg, pg, s_hist, dyin, wgu, bg, nw)


_GDN_COLS = [(512, 0), (512, 1), (512, 2), (512, 0), (128, 4)]


def _gdn_args(refs, bsz, rows):
    q_ref, k_ref, v_ref, og_ref, ab_ref, sc_ref, nw_ref = refs
    return (_heads(q_ref, bsz, rows, 128), _heads(k_ref, bsz, rows, 128), _heads(v_ref, bsz, rows, 128),
            _heads(ab_ref, bsz, rows, 1), _heads(ab_ref, bsz, rows, 1, off=H), _heads(og_ref, bsz, rows, 128),
            _per_head(sc_ref, bsz, 1, slice(0, 1)), _per_head(sc_ref, bsz, 1, slice(1, 2)), nw_ref[...])


def _gdn_fwd_call(conv, pd, sc, nw):
    bsz, t, _ = conv.shape
    nc = t // CHUNK
    nh = bsz * H
    _, specs = _chunk_specs(nc, False, bsz, _GDN_COLS)

    def body(q_ref, k_ref, v_ref, og_ref, ab_ref, sc_ref, nw_ref, y_ref, sh_ref, th_ref, s_ref):
        @pl.when(pl.program_id(0) == 0)
        def _():
            s_ref[...] = jnp.zeros_like(s_ref)

        s = s_ref[...]
        for sub in range(SUB):
            rows = _chunk_rows(sub)
            q, k, v, a, bb, og, alog, dtb, nw_ = _gdn_args((q_ref, k_ref, v_ref, og_ref, ab_ref, sc_ref, nw_ref), bsz, rows)
            sh_ref[sub] = s
            y, s, tinv = gdn_chunk(q, k, v, a, bb, og, s, alog, dtb, nw_)
            th_ref[sub] = tinv.astype(MXU)
            for b in range(bsz):
                for h in range(H):
                    y_ref[b, rows, 128 * h:128 * h + 128] = y[H * b + h].astype(MXU)
        s_ref[...] = s

    return pl.pallas_call(
        body, name="gdn_fwd", grid=(nc // SUB,),
        in_specs=specs + [_full((2, 128)), _full((1, 128))],
        out_specs=[pl.BlockSpec((bsz, SUB * CHUNK, 512), lambda n: (0, n, 0)),
                   pl.BlockSpec((SUB, nh, DV, DV), lambda n: (n, 0, 0, 0)),
                   pl.BlockSpec((SUB, nh, CHUNK, CHUNK), lambda n: (n, 0, 0, 0))],
        out_shape=[jax.ShapeDtypeStruct((bsz, t, 512), MXU), jax.ShapeDtypeStruct((nc, nh, DV, DV), F32),
                   jax.ShapeDtypeStruct((nc, nh, CHUNK, CHUNK), MXU)],
        scratch_shapes=[pltpu.VMEM((nh, DV, DV), F32)],
        compiler_params=_params(("arbitrary",)),
    )(conv, conv, conv, pd, pd, sc, nw)


def _gdn_bwd_call(conv, pd, s_hist, t_hist, dyin, sc, nw):
    bsz, t, _ = conv.shape
    nc = t // CHUNK
    nh = bsz * H
    n_of, specs = _chunk_specs(nc, True, bsz, _GDN_COLS, SUB_GDN_BWD)

    def body(q_ref, k_ref, v_ref, og_ref, ab_ref, sh_ref, th_ref, dy_ref, sc_ref, nw_ref,
             dc_ref, dpd_ref, dsc_ref, dnw_ref, ds_ref):
        @pl.when(pl.program_id(0) == 0)
        def _():
            dsc_ref[...] = jnp.zeros_like(dsc_ref)
            dnw_ref[...] = jnp.zeros_like(dnw_ref)
            ds_ref[...] = jnp.zeros_like(ds_ref)

        lane = lax.broadcasted_iota(jnp.int32, (CHUNK, 128), 1)
        ds = ds_ref[...]
        for sub in reversed(range(SUB_GDN_BWD)):
            rows = _chunk_rows(sub)
            q, k, v, a, bb, og, alog, dtb, nw_ = _gdn_args((q_ref, k_ref, v_ref, og_ref, ab_ref, sc_ref, nw_ref), bsz, rows)
            _, vjp = jax.vjp(functools.partial(gdn_chunk, tinv=th_ref[sub]), q, k, v, a, bb, og, sh_ref[sub], alog, dtb, nw_)
            dq, dk, dv, da, db, dog, ds, dalog, ddtb, dnw = vjp((_heads(dy_ref, bsz, rows, 128), ds))
            dnw_ref[...] += dnw
            for b in range(bsz):
                dab = jnp.zeros((CHUNK, 128), F32)
                for h in range(H):
                    i = H * b + h
                    dc_ref[b, rows, 128 * h:128 * h + 128] = dq[i]
                    dc_ref[b, rows, 512 + 128 * h:512 + 128 * h + 128] = dk[i]
                    dc_ref[b, rows, 1024 + 128 * h:1024 + 128 * h + 128] = dv[i]
                    dpd_ref[b, rows, 128 * h:128 * h + 128] = dog[i].astype(MXU)
                    dab = dab + jnp.where(lane == h, da[i], 0.0) + jnp.where(lane == H + h, db[i], 0.0)
                    dsc_ref[0:1, h:h + 1] += dalog[i]
                    dsc_ref[1:2, h:h + 1] += ddtb[i]
                dpd_ref[b, rows, 512:640] = dab.astype(MXU)
        ds_ref[...] = ds

    return pl.pallas_call(
        body, name="gdn_bwd", grid=(nc // SUB_GDN_BWD,),
        in_specs=specs + [pl.BlockSpec((SUB_GDN_BWD, nh, DV, DV), lambda n: (n_of(n), 0, 0, 0)),
                          pl.BlockSpec((SUB_GDN_BWD, nh, CHUNK, CHUNK), lambda n: (n_of(n), 0, 0, 0)),
                          pl.BlockSpec((bsz, SUB_GDN_BWD * CHUNK, 512), lambda n: (0, n_of(n), 1)),
                          _full((2, 128)), _full((1, 128))],
        out_specs=[pl.BlockSpec((bsz, SUB_GDN_BWD * CHUNK, W_GQKV), lambda n: (0, n_of(n), 0)),
                   pl.BlockSpec((bsz, SUB_GDN_BWD * CHUNK, W_GDN), lambda n: (0, n_of(n), 0)),
                   _full((2, 128)), _full((1, 128))],
        out_shape=[jax.ShapeDtypeStruct(conv.shape, F32), jax.ShapeDtypeStruct(pd.shape, MXU),
                   jax.ShapeDtypeStruct((2, 128), F32), jax.ShapeDtypeStruct((1, 128), F32)],
        scratch_shapes=[pltpu.VMEM((nh, DV, DV), F32)],
        compiler_params=_params(("arbitrary",)),
    )(conv, conv, conv, pd, pd, s_hist, t_hist, dyin, sc, nw)


def _head_call(x, ya, yb, wout, mod3, lnw, lnb, tgt):
    bsz, t, _ = x.shape
    tm = min(512, t)
    rows = min(256, tm)

    def body(x_ref, ya_ref, yb_ref, w_ref, mod_ref, lnw_ref, lnb_ref, t_ref,
             dyin_ref, dxa_ref, dgate_ref, dw_ref, dlnw_ref, dlnb_ref, loss_ref):
        b, i = pl.program_id(0), pl.program_id(1)

        @pl.when((b == 0) & (i == 0))
        def _():
            dw_ref[...] = jnp.zeros_like(dw_ref)
            dlnw_ref[...] = jnp.zeros_like(dlnw_ref)
            dlnb_ref[...] = jnp.zeros_like(dlnb_ref)
            loss_ref[...] = jnp.zeros_like(loss_ref)

        @pl.when(i == 0)
        def _():
            dgate_ref[...] = jnp.zeros_like(dgate_ref)

        w = w_ref[...]
        parts = [slice(p * rows, (p + 1) * rows) for p in range(tm // rows)]
        yin = [jnp.concatenate([ya_ref[0, rs, :], yb_ref[0, rs, :]], axis=-1).astype(MXU) for rs in parts]
        y = [jnp.dot(yi, w, preferred_element_type=F32) for yi in yin]
        for rs, yi, y_p in zip(parts, yin, y):
            loss, vjp = jax.vjp(head_fn, x_ref[0, rs, :], y_p, mod_ref[0, 2:3, :], lnw_ref[...], lnb_ref[...], t_ref[0, rs, :])
            dx, dy, dgate, dlnw, dlnb, _ = vjp(jnp.ones((1, 1), F32))
            dyb = dy.astype(MXU)
            dyin_ref[0, rs, :] = lax.dot_general(dyb, w, (((1,), (1,)), ((), ())), preferred_element_type=F32)
            dw_ref[...] += lax.dot_general(yi, dyb, (((0,), (0,)), ((), ())), preferred_element_type=F32)
            dxa_ref[0, rs, :] = dx
            dgate_ref[0] += dgate
            dlnw_ref[...] += dlnw
            dlnb_ref[...] += dlnb
            loss_ref[...] += jnp.broadcast_to(loss, (1, 128))

    tok = lambda w, j=0: pl.BlockSpec((1, tm, w), lambda b, i: (b, i, j))
    row = pl.BlockSpec((1, D), lambda b, i: (0, 0))
    return pl.pallas_call(
        body, name="head", grid=(bsz, t // tm),
        in_specs=[tok(D), tok(512), tok(512), pl.BlockSpec((D, D), lambda b, i: (0, 0)),
                  pl.BlockSpec((1, 3, D), lambda b, i: (b, 0, 0)), row, row, tok(D)],
        out_specs=[tok(D), tok(D), pl.BlockSpec((1, 1, D), lambda b, i: (b, 0, 0)),
                   pl.BlockSpec((D, D), lambda b, i: (0, 0)), row, row, pl.BlockSpec((1, 128), lambda b, i: (0, 0))],
        out_shape=[jax.ShapeDtypeStruct(x.shape, F32), jax.ShapeDtypeStruct(x.shape, F32),
                   jax.ShapeDtypeStruct((bsz, 1, D), F32), jax.ShapeDtypeStruct((D, D), F32),
                   jax.ShapeDtypeStruct((1, D), F32), jax.ShapeDtypeStruct((1, D), F32),
                   jax.ShapeDtypeStruct((1, 128), F32)],
        compiler_params=_params(("arbitrary", "arbitrary")),
    )(x, ya, yb, wout, mod3, lnw, lnb, tgt)


def _dh_call(dpg, dpq, dpd, wpt, x, mod3, dxa, smalls):
    bsz, t, _ = x.shape
    tm = min(512, t)
    assert bsz + 1 <= 8

    def body(dg_ref, dq_ref, dd_ref, w_ref, x_ref, mod_ref, dxa_ref, dgate_ref, dlnw_ref, dlnb_ref, dbg_ref, n1_ref, n2_ref,
             dsc_ref, loss_ref, gx_ref, sp_ref):
        b = pl.program_id(0)

        @pl.when((b == 0) & (pl.program_id(1) == 0))
        def _():
            sp_ref[...] = jnp.zeros_like(sp_ref)
            for e in range(bsz):
                sp_ref[e:e + 1, 2 * D:3 * D] = dgate_ref[e]
            off = 0
            for ref in (dlnw_ref, dlnb_ref, dbg_ref, n1_ref, n2_ref):
                sp_ref[bsz:bsz + 1, off:off + ref.shape[1]] = ref[...]
                off += ref.shape[1]
            sp_ref[bsz:bsz + 1, off:off + 128] = dsc_ref[0:1, :]
            sp_ref[bsz:bsz + 1, off + 128:off + 256] = dsc_ref[1:2, :]
            sp_ref[bsz:bsz + 1, SMALL_W:SMALL_W + 128] = loss_ref[...]

        mm = lambda a, lo, hi: jnp.dot(a.astype(MXU), w_ref[lo:hi, :], preferred_element_type=F32)
        dh = mm(dg_ref[0], 0, W_GLA) + mm(dq_ref[0], W_GLA, W_GLA + W_GQKV) + mm(dd_ref[0], W_GLA + W_GQKV, PW)
        gx_ref[0] = dh * (1.0 + mod_ref[0, 1:2, :]) + dxa_ref[0]
        dshift = jnp.sum(dh, axis=0, keepdims=True)
        dscale = jnp.sum(dh * x_ref[0], axis=0, keepdims=True)
        for e in range(bsz):
            @pl.when(b == e)
            def _():
                sp_ref[e:e + 1, 0:D] += dshift
                sp_ref[e:e + 1, D:2 * D] += dscale

    tok = lambda w: pl.BlockSpec((1, tm, w), lambda b, i: (b, i, 0))
    whole = lambda a: pl.BlockSpec(a.shape, lambda b, i: (0,) * a.ndim)
    return pl.pallas_call(
        body, name="dh", grid=(bsz, t // tm),
        in_specs=[tok(W_GLA), tok(W_GQKV), tok(W_GDN), pl.BlockSpec((PW, D), lambda b, i: (0, 0)), tok(D),
                  pl.BlockSpec((1, 3, D), lambda b, i: (b, 0, 0)), tok(D)] + [whole(a) for a in smalls],
        out_specs=[tok(D), pl.BlockSpec((8, SPW), lambda b, i: (0, 0))],
        out_shape=[jax.ShapeDtypeStruct(x.shape, F32), jax.ShapeDtypeStruct((8, SPW), F32)],
        compiler_params=_params(("arbitrary", "arbitrary")),
    )(dpg, dpq, dpd, wpt, x, mod3, dxa, *smalls)


def _dw_call(x, mod3, dpg, dpq, dpd):
    bsz, t, _ = x.shape
    tm = min(512, t)
    nsteps = bsz * (t // tm)

    def body(x_ref, mod_ref, dg_ref, dq_ref, dd_ref, dw_ref, acc):
        step = pl.program_id(0) * (t // tm) + pl.program_id(1)

        @pl.when(step == 0)
        def _():
            acc[...] = jnp.zeros_like(acc)

        h = (x_ref[0] * (1.0 + mod_ref[0, 1:2, :]) + mod_ref[0, 0:1, :]).astype(MXU)
        for ref, lo, hi in ((dg_ref, 0, W_GLA), (dq_ref, W_GLA, W_GLA + W_GQKV), (dd_ref, W_GLA + W_GQKV, PW)):
            acc[lo:hi, :] += lax.dot_general(ref[0].astype(MXU), h, (((0,), (0,)), ((), ())), preferred_element_type=F32)

        @pl.when(step == nsteps - 1)
        def _():
            dw_ref[...] = acc[...].astype(dw_ref.dtype)

    tok = lambda w: pl.BlockSpec((1, tm, w), lambda b, i: (b, i, 0))
    return pl.pallas_call(
        body, name="dw", grid=(bsz, t // tm),
        in_specs=[tok(D), pl.BlockSpec((1, 3, D), lambda b, i: (b, 0, 0)), tok(W_GLA), tok(W_GQKV), tok(W_GDN)],
        out_specs=pl.BlockSpec((PW, D), lambda b, i: (0, 0)),
        out_shape=jax.ShapeDtypeStruct((PW, D), WIRE),
        scratch_shapes=[pltpu.VMEM((PW, D), F32)],
        compiler_params=_params(("arbitrary", "arbitrary")),
    )(x, mod3, dpg, dpq, dpd)


def _adamw(w, g, m, v):
    m = ADAM_B1 * m + (1.0 - ADAM_B1) * g
    v = ADAM_B2 * v + (1.0 - ADAM_B2) * jnp.square(g)
    m_hat = m / (1.0 - ADAM_B1 ** ADAM_STEP)
    v_hat = v / (1.0 - ADAM_B2 ** ADAM_STEP)
    delta = -ADAM_LR * (m_hat / (jnp.sqrt(v_hat) + ADAM_EPS) + ADAM_WD * w)
    return delta, m, v


def _sum8(ref):
    g = ref[0].astype(F32)
    for j in range(1, NDEV):
        g = g + ref[j].astype(F32)
    return g


def _adam_sum_call(name, g8, w, m, v, cols):
    r, c = w.shape

    def body(g_ref, w_ref, m_ref, v_ref, go_ref, d_ref, mo_ref, vo_ref):
        g = _sum8(g_ref)
        go_ref[...] = g
        d_ref[...], mo_ref[...], vo_ref[...] = _adamw(w_ref[...], g, m_ref[...], v_ref[...])

    blk = pl.BlockSpec((r, cols), lambda i: (0, i))
    return pl.pallas_call(
        body, name=name, grid=(c // cols,),
        in_specs=[pl.BlockSpec((NDEV, r, cols), lambda i: (0, 0, i)), blk, blk, blk],
        out_specs=[blk] * 4, out_shape=[jax.ShapeDtypeStruct((r, c), F32)] * 4,
        compiler_params=_params(("parallel",)),
    )(g8, w, m, v)


def _adam_ada_call(c_all, dmod_cols, w, m, v):
    def body(c_ref, dm_ref, w_ref, m_ref, v_ref, go_ref, d_ref, mo_ref, vo_ref):
        g = lax.dot_general(c_ref[...].astype(MXU), dm_ref[...].astype(MXU), (((0,), (0,)), ((), ())),
                            preferred_element_type=F32)
        go_ref[...] = g
        d_ref[...], mo_ref[...], vo_ref[...] = _adamw(w_ref[...], g, m_ref[...], v_ref[...])

    return pl.pallas_call(
        body, name="adam_ada", out_shape=[jax.ShapeDtypeStruct(w.shape, F32)] * 4, compiler_params=_params(),
    )(c_all, dmod_cols, w, m, v)


_SMALL_AT = dict(ln_w=(0, 1024), ln_b=(1024, 1024), b_gate=(2048, 256), gla_nw=(2304, 128), gdn_nw=(2432, 128),
                 a_log=(2560, 4), dt_bias=(2688, 4))


def _adam_small_call(sp_all, bsz, params):
    names = list(params)

    def body(sp_ref, *refs):
        ins, outs = refs[:3 * len(names)], refs[3 * len(names):]
        dmod_ref, loss_ref, outs = outs[0], outs[1], outs[2:]
        packed = sp_ref[0, bsz:bsz + 1, :]
        for j in range(1, NDEV):
            packed = packed + sp_ref[j, bsz:bsz + 1, :]
        gb = None
        for j in range(NDEV):
            dmod_ref[bsz * j:bsz * j + bsz, :] = sp_ref[j, 0:bsz, :]
            for e in range(bsz):
                gb = sp_ref[j, e:e + 1, :] if gb is None else gb + sp_ref[j, e:e + 1, :]
        loss_ref[...] = packed[:, SMALL_W:SMALL_W + 128]
        for i, name in enumerate(names):
            if name == "b_ada":
                g = gb
            else:
                lo, n = _SMALL_AT[name]
                g = packed[:, lo:lo + n]
            w_ref, m_ref, v_ref = ins[3 * i:3 * i + 3]
            g_ref, d_ref, mo_ref, vo_ref = outs[4 * i:4 * i + 4]
            g_ref[...] = g
            d_ref[...], mo_ref[...], vo_ref[...] = _adamw(w_ref[...], g, m_ref[...], v_ref[...])

    flat = [a for name in names for a in params[name]]
    out_shape = [jax.ShapeDtypeStruct((NDEV * bsz, SPW), F32), jax.ShapeDtypeStruct((1, 128), F32)]
    out_shape += [jax.ShapeDtypeStruct(params[name][0].shape, F32) for name in names for _ in range(4)]
    res = pl.pallas_call(body, name="adam_small", out_shape=out_shape, compiler_params=_params())(sp_all, *flat)
    return res[0], res[1], {name: res[2 + 4 * i:6 + 4 * i] for i, name in enumerate(names)}


def _mesh_pos():
    x, y, c = lax.axis_index("x"), lax.axis_index("y"), lax.axis_index("c")
    return x, y, c, 4 * x + 2 * y + c


def _peer(x, y, c, k):
    px = 1 - x if k & 4 else x
    py = 1 - y if k & 2 else y
    pc = 1 - c if k & 1 else c
    return (px, py, pc), 4 * px + 2 * py + pc


_ANY = pl.BlockSpec(memory_space=pl.ANY)
_VMEM = pl.BlockSpec(memory_space=pltpu.VMEM)


def _gather_call(c8, w_ada, b_sh, w_in_t, conv_w):
    C_SEM, W_SEM, MOD_SEM, CONV_SEM = 0, 1, 2, 3

    def body(c_ref, wada_ref, b_ref, win_ref, cw_ref, wall_ref, call_ref, mod_ref, cwall_ref, modp, send_sems, recv_sems, loc_sem):
        x, y, c, me = _mesh_pos()

        def remote(src, dst, a, k, to):
            return pltpu.make_async_remote_copy(src_ref=src, dst_ref=dst, send_sem=send_sems.at[a, k],
                                                recv_sem=recv_sems.at[a, k], device_id=_peer(x, y, c, to)[0],
                                                device_id_type=pl.DeviceIdType.MESH)

        idx = lambda k: _peer(x, y, c, k)[1]
        sends = []
        call_ref[me] = c_ref[...]
        cwall_ref[me] = cw_ref[...]
        for k in range(1, NDEV):
            sends.append(remote(c_ref, call_ref.at[me], C_SEM, k, k))
            sends[-1].start()
            sends.append(remote(cw_ref, cwall_ref.at[me], CONV_SEM, k, k))
            sends[-1].start()
        local = pltpu.make_async_copy(win_ref, wall_ref.at[me], loc_sem)
        local.start()
        for k in (1, 2, 4, 6):
            sends.append(remote(win_ref, wall_ref.at[me], W_SEM, k, k))
            sends[-1].start()
        for k in range(1, NDEV):
            remote(c_ref, call_ref.at[idx(k)], C_SEM, k, k).wait_recv()
        modp[...] = jnp.dot(call_ref[...].reshape(NDEV * 8, D).astype(MXU), wada_ref[...].astype(MXU),
                            preferred_element_type=F32) + b_ref[...]
        mod_ref[me] = modp[pl.ds(pl.multiple_of(me * 8, 8), 8), :]
        for k in range(1, NDEV):
            sends.append(remote(modp.at[pl.ds(pl.multiple_of(idx(k) * 8, 8), 8), :], mod_ref.at[me], MOD_SEM, k, k))
            sends[-1].start()
        for k in (2, 4, 6):
            remote(win_ref, wall_ref.at[idx(k)], W_SEM, k, k).wait_recv()
            sends.append(remote(wall_ref.at[idx(k)], wall_ref.at[idx(k)], W_SEM, k + 1, 1))
            sends[-1].start()
        for k in (1, 3, 5, 7):
            remote(win_ref, wall_ref.at[idx(k)], W_SEM, k, 1).wait_recv()
        for k in range(1, NDEV):
            remote(modp.at[pl.ds(0, 8), :], mod_ref.at[idx(k)], MOD_SEM, k, k).wait_recv()
            remote(cw_ref, cwall_ref.at[idx(k)], CONV_SEM, k, k).wait_recv()
        for cp in sends:
            cp.wait_send()
        local.wait()

    return pl.pallas_call(
        body, name="gather",
        out_shape=[jax.ShapeDtypeStruct((NDEV,) + w_in_t.shape, w_in_t.dtype), jax.ShapeDtypeStruct((NDEV, 8, D), F32),
                   jax.ShapeDtypeStruct((NDEV, 8, SHARD_ADA), F32), jax.ShapeDtypeStruct((NDEV,) + conv_w.shape, F32)],
        in_specs=[_VMEM, _VMEM, _VMEM, _ANY, _VMEM], out_specs=[_ANY, _VMEM, _VMEM, _VMEM],
        scratch_shapes=[pltpu.VMEM((NDEV * 8, SHARD_ADA), F32), pltpu.SemaphoreType.DMA((4, NDEV)),
                        pltpu.SemaphoreType.DMA((4, NDEV)), pltpu.SemaphoreType.DMA],
        compiler_params=_params(),
    )(c8, w_ada, b_sh, w_in_t, conv_w)


_HBM = pl.BlockSpec(memory_space=pltpu.HBM)
_SEM = pl.BlockSpec(memory_space=pltpu.SEMAPHORE)
_EFFECT = pltpu.SideEffectType.DATAFLOW_SIDE_EFFECTING


def _whole(gather, a):
    return gather[a] if isinstance(gather, (list, tuple)) else gather


def _xchg_start(name, blocks, lands, gather):
    nb = len(blocks)

    def body(*refs):
        srcs, dsts = refs[:nb], refs[nb:2 * nb]
        send_sems, recv_sems = refs[2 * nb], refs[2 * nb + 1]
        token = refs[-1]
        x, y, c, me = _mesh_pos()
        for k in range(1, NDEV):
            dev, pidx = _peer(x, y, c, k)
            for a in range(nb):
                pltpu.make_async_remote_copy(src_ref=srcs[a] if _whole(gather, a) else srcs[a].at[pidx], dst_ref=dsts[a].at[me],
                                             send_sem=send_sems.at[NDEV * a + k], recv_sem=recv_sems.at[NDEV * a + k],
                                             device_id=dev, device_id_type=pl.DeviceIdType.MESH).start()
        token[...] = jnp.zeros_like(token)

    thru = [pltpu.HBM(a.shape, a.dtype) for a in list(blocks) + list(lands)]
    return pl.pallas_call(
        body, name=name,
        out_shape=(pltpu.SemaphoreType.DMA((nb * NDEV,)), pltpu.SemaphoreType.DMA((nb * NDEV,)), *thru,
                   jax.ShapeDtypeStruct((8, 128), F32)),
        in_specs=[_HBM] * (2 * nb), out_specs=(_SEM, _SEM, *([_HBM] * (2 * nb)), _VMEM),
        input_output_aliases={i: 2 + i for i in range(2 * nb)},
        compiler_params=pltpu.CompilerParams(has_side_effects=_EFFECT),
    )(*[pltpu.with_memory_space_constraint(a, pltpu.HBM) for a in list(blocks) + list(lands)])


def _xchg_wait(name, send_sems, recv_sems, thru, after, gather):
    nb = len(thru) // 2

    def body(*refs):
        srcs, dsts = refs[:nb], refs[nb:2 * nb]
        send_sems, recv_sems = refs[2 * nb], refs[2 * nb + 1]
        x, y, c, me = _mesh_pos()
        for k in range(1, NDEV):
            dev, pidx = _peer(x, y, c, k)
            for a in range(nb):
                cp = pltpu.make_async_remote_copy(src_ref=srcs[a] if _whole(gather, a) else srcs[a].at[pidx], dst_ref=dsts[a].at[pidx],
                                                  send_sem=send_sems.at[NDEV * a + k], recv_sem=recv_sems.at[NDEV * a + k],
                                                  device_id=dev, device_id_type=pl.DeviceIdType.MESH)
                cp.wait_send()
                cp.wait_recv()

    out = pl.pallas_call(
        body, name=name, out_shape=tuple(pltpu.HBM(a.shape, a.dtype) for a in thru),
        in_specs=[_HBM] * (2 * nb) + [_SEM, _SEM, pl.BlockSpec(memory_space=pl.ANY)], out_specs=tuple([_HBM] * (2 * nb)),
        input_output_aliases={i: i for i in range(2 * nb)},
        compiler_params=pltpu.CompilerParams(has_side_effects=_EFFECT),
    )(*thru, send_sems, recv_sems, after)
    return out[nb:]


def _pad_cols(a, n):
    return jnp.pad(a, ((0, 0), (0, n - a.shape[1])))


_SEGMENTS = ((0, 256, 0), (256, 512, 256), (512, 1024, 512), (1040, 1552, 1024), (1024, 1040, 1536),
             (1552, 3088, W_GLA), (3096, 3608, W_GLA + W_GQKV), (3088, 3096, W_GLA + W_GQKV + 512))


def _row_pieces():
    out = []
    for lo, hi, dst in _SEGMENTS:
        while lo < hi:
            j, off = divmod(lo, SHARD_IN)
            n = min(hi - lo, SHARD_IN - off)
            out.append((j, off, n, dst))
            lo, dst = lo + n, dst + n
    return out


def _relayout_call(a, to_layout):
    cols = 256
    dst_shape = (PW, D) if to_layout else (NDEV, SHARD_IN, D)

    def body(i_ref, o_ref, scr):
        if to_layout:
            scr[...] = jnp.zeros_like(scr)
        for j, off, n, at in _row_pieces():
            if to_layout:
                scr[at:at + n, :] = i_ref[j, off:off + n, :].astype(F32)
            else:
                scr[j, off:off + n, :] = i_ref[at:at + n, :].astype(F32)
        o_ref[...] = scr[...].astype(o_ref.dtype)

    blk = lambda shape: pl.BlockSpec(shape[:-1] + (cols,), lambda i: (0,) * (len(shape) - 1) + (i,))
    return pl.pallas_call(
        body, name="to_layout" if to_layout else "to_shards", grid=(D // cols,),
        in_specs=[blk(a.shape)], out_specs=blk(dst_shape), out_shape=jax.ShapeDtypeStruct(dst_shape, a.dtype),
        scratch_shapes=[pltpu.VMEM(dst_shape[:-1] + (cols,), F32)],
        compiler_params=_params(("parallel",)),
    )(a)


def _assemble_wt(blocks):
    return _relayout_call(blocks, True)


def _disassemble_dwt(dwt):
    return _relayout_call(dwt, False)


def local_grads(x, mod3, wp, conv_w, late_weights, bg, gla_nw, sc, gdn_nw, lnw, lnb, tgt):
    pg, pq, pd, conv = _proj_call(x, mod3, wp, conv_w)
    wout, wgu_p = late_weights(pq)
    ya, s_gla = _gla_fwd_call(pg, wgu_p, bg, gla_nw)
    yb, s_gdn, t_gdn = _gdn_fwd_call(conv, pd, sc, gdn_nw)
    dyin, dxa, dgate, dwout, dlnw, dlnb, loss = _head_call(x, ya, yb, wout, mod3, lnw, lnb, tgt)
    dconv, dpd, dsc, dnw_gdn = _gdn_bwd_call(conv, pd, s_gdn, t_gdn, dyin, sc, gdn_nw)
    dpq, dconv_w = _conv_bwd_sc_call(dconv, pq, conv_w)
    dpg, dwgu, dbg, dnw_gla = _gla_bwd_call(pg, s_gla, dyin, wgu_p, bg, gla_nw)
    dw_in = _disassemble_dwt(_dw_call(x, mod3, dpg, dpq, dpd))
    g = dict(dw_in=dw_in, dwout=dwout, dconv_w=dconv_w, dwgu=dwgu[:16])
    smalls = (dgate, dlnw, dlnb, dbg, dnw_gla, dnw_gdn, dsc, loss)
    return g, lambda mod3_: _dh_call(dpg, dpq, dpd, wp, x, mod3_, dxa, smalls)


def local_step(x, mod3, wp, wout, conv_w, wgu_p, *args):
    g, finish = local_grads(x, mod3, wp, conv_w, lambda _: (wout, wgu_p), *args)
    g["gx"], sp = finish(mod3)
    bsz = x.shape[0]
    g["dmod"] = sp[:bsz].reshape(bsz, 3, D)
    g["loss"] = sp[bsz, SMALL_W]
    for name, (lo, n) in _SMALL_AT.items():
        g[name] = sp[bsz:bsz + 1, lo:lo + n]
    return g


def kernel(x, c, w_ada, b_ada, w_in, gla_w_gate_up, gla_b_gate, gla_norm_w, gdn_conv_w, gdn_a_log, gdn_dt_bias, gdn_norm_w, w_out, ln_w, ln_b, loss_target, m_w_ada, m_b_ada, m_w_in, m_gla_w_gate_up, m_gla_b_gate, m_gla_norm_w, m_gdn_conv_w, m_gdn_a_log, m_gdn_dt_bias, m_gdn_norm_w, m_w_out, m_ln_w, m_ln_b, v_w_ada, v_b_ada, v_w_in, v_gla_w_gate_up, v_gla_b_gate, v_gla_norm_w, v_gdn_conv_w, v_gdn_a_log, v_gdn_dt_bias, v_gdn_norm_w, v_w_out, v_ln_w, v_ln_b):
    me = 4 * lax.axis_index("x") + 2 * lax.axis_index("y") + lax.axis_index("c")
    bsz = x.shape[0]

    b_sh = lax.dynamic_slice(b_ada, (0, me * SHARD_ADA), (1, SHARD_ADA))
    c8 = jnp.pad(c, ((0, 8 - bsz), (0, 0)))
    w_in_t, m_in_t, v_in_t = (jnp.swapaxes(a[0], 0, 1) for a in (w_in, m_w_in, v_w_in))
    win_all, c_all, mod_blk, conv_all = _gather_call(c8, w_ada[0], b_sh, w_in_t.astype(WIRE), gdn_conv_w[0])
    conv_w = jnp.transpose(conv_all, (1, 0, 2)).reshape(4, W_GQKV)
    wp = _assemble_wt(win_all)
    mod = jnp.transpose(mod_blk[:, :bsz, :], (1, 0, 2)).reshape(bsz, 3 * D)
    mod3 = mod.reshape(bsz, 3, D)
    sc = jnp.concatenate([_pad_cols(gdn_a_log, 128), _pad_cols(gdn_dt_bias, 128)], axis=0)

    own = lambda a: lax.dynamic_update_slice(lax.empty((NDEV,) + a.shape, a.dtype), a[None], (me,) + (0,) * a.ndim)
    late = [w_out[0].astype(WIRE), gla_w_gate_up[0] + 0.0 * mod_blk[0, 0, 0]]
    w_send, w_recv, *w_thru, w_token = _xchg_start("wgather_start", late, [own(a) for a in late], gather=True)

    def late_weights(pq):
        wout_all, wgu_all = _xchg_wait("wgather_wait", w_send, w_recv, w_thru, pq, gather=True)
        return wout_all.reshape(D, D), jnp.pad(jnp.transpose(wgu_all, (1, 0, 2)).reshape(16, 256), ((0, 112), (0, 0)))

    g, finish = local_grads(x, mod3 + w_token[0, 0], wp, conv_w, late_weights, gla_b_gate, gla_norm_w, sc, gdn_norm_w, ln_w, ln_b,
                            loss_target)

    big = [g["dw_in"], g["dwout"].reshape(NDEV, D // NDEV, D).astype(WIRE)]
    lands = [lax.dynamic_update_slice(lax.empty(a.shape, a.dtype), lax.dynamic_slice(a, (me, 0, 0), (1,) + a.shape[1:]),
                                      (me, 0, 0)) for a in big]
    send_sems, recv_sems, *thru, token = _xchg_start("xchg_start", big, lands, gather=False)
    gx, sp = finish(mod3 + token[0, 0])
    little = [jnp.transpose(g["dconv_w"].reshape(4, NDEV, W_GQKV // NDEV), (1, 0, 2)),
              jnp.transpose(g["dwgu"].reshape(16, NDEV, 256 // NDEV), (1, 0, 2)), sp]
    modes = [False, False, True]
    l_lands = [lax.dynamic_update_slice(lax.empty(a.shape, a.dtype), lax.dynamic_slice(a, (me, 0, 0), (1,) + a.shape[1:]),
                                        (me, 0, 0)) for a in little[:2]] + [own(sp)]
    l_send, l_recv, *l_thru, l_token = _xchg_start("small_start", little, l_lands, gather=modes)
    r_in, r_out = _xchg_wait("xchg_wait", send_sems, recv_sems, thru, l_token, gather=False)

    t_in = [jnp.swapaxes(a, 0, 1) for a in _adam_sum_call("adam_in", r_in, w_in_t, m_in_t, v_in_t, 256)]
    t_out = _adam_sum_call("adam_out", r_out, w_out[0], m_w_out[0], v_w_out[0], D)
    r_conv, r_gu, sp_all = _xchg_wait("small_wait", l_send, l_recv, l_thru, t_out[3], gather=modes)
    t_conv = _adam_sum_call("adam_conv", r_conv, gdn_conv_w[0], m_gdn_conv_w[0], v_gdn_conv_w[0], W_GQKV // NDEV)
    t_gu = _adam_sum_call("adam_gu", r_gu, gla_w_gate_up[0], m_gla_w_gate_up[0], v_gla_w_gate_up[0], 256 // NDEV)
    dmod_all, loss, small = _adam_small_call(sp_all, bsz, dict(
        b_ada=(b_ada, m_b_ada, v_b_ada), ln_w=(ln_w, m_ln_w, v_ln_w), ln_b=(ln_b, m_ln_b, v_ln_b),
        b_gate=(gla_b_gate, m_gla_b_gate, v_gla_b_gate), gla_nw=(gla_norm_w, m_gla_norm_w, v_gla_norm_w),
        gdn_nw=(gdn_norm_w, m_gdn_norm_w, v_gdn_norm_w), a_log=(gdn_a_log, m_gdn_a_log, v_gdn_a_log),
        dt_bias=(gdn_dt_bias, m_gdn_dt_bias, v_gdn_dt_bias)))
    c16 = c_all[:, :bsz, :].reshape(NDEV * bsz, D)
    t_ada = _adam_ada_call(c16, lax.dynamic_slice(dmod_all, (0, me * SHARD_ADA), (NDEV * bsz, SHARD_ADA)),
                           w_ada[0], m_w_ada[0], v_w_ada[0])

    def group(i):
        s = lambda name: small[name][i]
        return [t_ada[i][None], s("b_ada"), t_in[i][None], t_gu[i][None], s("b_gate"), s("gla_nw"), t_conv[i][None],
                s("a_log"), s("dt_bias"), s("gdn_nw"), t_out[i][None], s("ln_w"), s("ln_b")]

    return (loss[0, 0], gx, *group(0), *group(1), *group(2), *group(3))
```

```python
import functools

import jax
import jax.numpy as jnp
from jax import lax
from jax.experimental import pallas as pl
from jax.experimental.pallas import tpu as pltpu
from jax.experimental.pallas import tpu_sc as plsc

F32 = jnp.float32
MXU = jnp.bfloat16
WIRE = jnp.bfloat16
HI = lax.Precision.HIGH

D = 1024
NDEV = 8
H = 4
GLA_DK = 64
DV = 128
CHUNK = 64
SUB = 8
SUB_GDN_BWD = 4
LN_EPS = 1e-5
RMS_EPS = 1e-6
ALPHA = 2.0 ** 0.25
GATE_NORM = 16.0

W_GLA, W_GQKV, W_GDN = 1664, 1536, 640
PW = W_GLA + W_GQKV + W_GDN
IN_COLS = 3608
SHARD_IN = IN_COLS // NDEV
SHARD_ADA = 3 * D // NDEV
SPW = 3 * D
SMALL_W = 2816

ADAM_LR, ADAM_B1, ADAM_B2, ADAM_EPS, ADAM_WD, ADAM_STEP = 0.001, 0.9, 0.999, 1e-08, 0.01, 10

VMEM_LIMIT = 56 * 1024 * 1024


def _params(sem=None, **kw):
    if sem is not None:
        kw["dimension_semantics"] = sem
    return pltpu.CompilerParams(vmem_limit_bytes=VMEM_LIMIT, **kw)


_MM = (((2,), (1,)), ((0,), (0,)))
_NT = (((2,), (2,)), ((0,), (0,)))
_TN = (((1,), (1,)), ((0,), (0,)))


def _dg(a, b, dims):
    return lax.dot_general(a.astype(MXU), b.astype(MXU), dims, preferred_element_type=F32)


def _hdg(a, b, dims):
    return lax.dot_general(a, b, dims, precision=HI, preferred_element_type=F32)


@jax.custom_vjp
def bmm(a, b):
    return _dg(a, b, _MM)


bmm.defvjp(lambda a, b: (_dg(a, b, _MM), (a, b)), lambda r, g: (_dg(g, r[1], _NT), _dg(r[0], g, _TN)))


@jax.custom_vjp
def bnt(a, b):
    return _dg(a, b, _NT)


bnt.defvjp(lambda a, b: (_dg(a, b, _NT), (a, b)), lambda r, g: (_dg(g, r[1], _MM), _dg(g, r[0], _TN)))


@jax.custom_vjp
def btn(a, b):
    return _dg(a, b, _TN)


btn.defvjp(lambda a, b: (_dg(a, b, _TN), (a, b)), lambda r, g: (_dg(r[1], g, _NT), _dg(r[0], g, _MM)))


def unit_lower_inverse(a):
    n = a.shape[-1]
    r, c = _iotas(n)
    p = -a
    t = (r == c).astype(F32) + p
    for _ in range(5):
        p = _dg(p, p, _MM)
        t = t + _dg(t, p, _MM)
    return t


@jax.custom_vjp
def unit_lower_solve(a, t, r1, r2):
    return _dg(t, r1, _MM), _dg(t, r2, _MM)


def _solve_fwd(a, t, r1, r2):
    s1, s2 = _dg(t, r1, _MM), _dg(t, r2, _MM)
    return (s1, s2), (t, s1, s2)


def _solve_bwd(res, g):
    t, s1, s2 = res
    d1, d2 = _dg(t, g[0], _TN), _dg(t, g[1], _TN)
    return -(_dg(d1, s1, _NT) + _dg(d2, s2, _NT)), jnp.zeros_like(t), d1, d2


unit_lower_solve.defvjp(_solve_fwd, _solve_bwd)


def _iotas(n):
    return lax.broadcasted_iota(jnp.int32, (n, n), 0), lax.broadcasted_iota(jnp.int32, (n, n), 1)


def _col_to_row(col, eye):
    return jnp.sum(jnp.where(eye, col, 0.0), axis=1, keepdims=True)


def _row_to_col(row, eye):
    return jnp.sum(jnp.where(eye, row, 0.0), axis=2, keepdims=True)


def _pick_row(m, i):
    r = lax.broadcasted_iota(jnp.int32, m.shape, 1)
    return jnp.sum(jnp.where(r == i, m, 0.0), axis=1, keepdims=True)


def _rms_gate(o, nw, og):
    on = o * lax.rsqrt(jnp.mean(o * o, axis=-1, keepdims=True) + RMS_EPS) * nw
    return on * jax.nn.silu(og)


def gla_chunk(q, k, v, lr, og, s, wgu, bg, nw):
    n, c, _ = q.shape
    r, cc = _iotas(c)
    causal = r >= cc
    qs = q * (GLA_DK ** -0.5)
    z = bmm(lr, wgu) + bg
    g = jax.nn.log_sigmoid(z) / GATE_NORM
    b = _hdg(jnp.broadcast_to(causal.astype(F32), (n, c, c)), g, _MM)
    bref = _pick_row(b, c // 2 - 1)
    blast = _pick_row(b, c - 1)
    att = jnp.where(causal, bnt(qs * jnp.exp(b - bref), k * jnp.exp(bref - b)), 0.0)
    o = bmm(att, v) + bmm(qs * jnp.exp(b), s)
    rk, ck = _iotas(GLA_DK)
    s_new = _row_to_col(jnp.exp(blast), rk == ck) * s + btn(k * jnp.exp(blast - b), v)
    return _rms_gate(o, nw, og), s_new


def gdn_chunk(cq, ck, cv, a, bb, og, s, alog, dtb, nw, tinv=None):
    c = cq.shape[1]
    r, cc = _iotas(c)
    eye, causal, strict = r == cc, r >= cc, r > cc
    q, k, v = jax.nn.silu(cq), jax.nn.silu(ck), jax.nn.silu(cv)
    q = q * lax.rsqrt(jnp.sum(q * q, axis=-1, keepdims=True) + RMS_EPS) * (DV ** -0.5)
    k = k * lax.rsqrt(jnp.sum(k * k, axis=-1, keepdims=True) + RMS_EPS)
    g = -jnp.exp(alog) * jax.nn.softplus(a + dtb)
    beta = jax.nn.sigmoid(bb)
    d = jnp.sum(jnp.where(causal, _col_to_row(g, eye), 0.0), axis=2, keepdims=True)
    el = jnp.exp(jnp.where(causal, d - _col_to_row(d, eye), -jnp.inf))
    kb = k * beta
    amat = jnp.where(strict, bnt(kb, k) * el, 0.0)
    t = unit_lower_inverse(amat) if tinv is None else tinv
    u, w = unit_lower_solve(amat, t, v * beta, kb * jnp.exp(d))
    qk = jnp.where(causal, bnt(q, k) * el, 0.0)
    dlast = _pick_row(d, c - 1)
    v_new = u - bmm(w, s)
    o = bmm(q * jnp.exp(d), s) + bmm(qk, v_new)
    s_new = jnp.exp(dlast) * s + btn(k * jnp.exp(dlast - d), v_new)
    y = _rms_gate(o, nw, og)
    return (y, s_new, t) if tinv is None else (y, s_new)


def head_fn(x, y, gate, lnw, lnb, tgt):
    u = ALPHA * x + (1.0 + gate) * y
    mu = jnp.mean(u, axis=-1, keepdims=True)
    var = jnp.mean(jnp.square(u - mu), axis=-1, keepdims=True)
    out = (u - mu) * lax.rsqrt(var + LN_EPS) * lnw + lnb
    err = jnp.square(out - tgt)
    return 0.5 * jnp.sum(jnp.mean(err, axis=-1, keepdims=True), axis=0, keepdims=True)


def _proj_call(x, mod3, wpt, conv_w):
    bsz, t, _ = x.shape
    tm = min(512, t)

    def body(x_ref, mod_ref, w_ref, cw_ref, pg_ref, pq_ref, pd_ref, conv_ref, buf):
        i = pl.program_id(1)
        h = (x_ref[0] * (1.0 + mod_ref[0, 1:2, :]) + mod_ref[0, 0:1, :]).astype(MXU)
        nt = lambda lo, hi: lax.dot_general(h, w_ref[lo:hi, :], (((1,), (1,)), ((), ())), preferred_element_type=F32)
        pq = nt(W_GLA, W_GLA + W_GQKV)
        pq_ref[0] = pq
        @pl.when(i == 0)
        def _():
            buf[0:8, :] = jnp.zeros((8, W_GQKV), F32)

        @pl.when(i > 0)
        def _():
            buf[0:8, :] = buf[tm:tm + 8, :]

        buf[8:, :] = pq
        acc = cw_ref[0:1, :] * buf[pl.ds(5, tm), :]
        for k in range(1, 4):
            acc = acc + cw_ref[k:k + 1, :] * buf[pl.ds(5 + k, tm), :]
        conv_ref[0] = acc
        pg_ref[0] = nt(0, W_GLA)
        pd_ref[0] = nt(W_GLA + W_GQKV, PW)

    tok = lambda w: pl.BlockSpec((1, tm, w), lambda b, i: (b, i, 0))
    return pl.pallas_call(
        body, name="proj", grid=(bsz, t // tm),
        in_specs=[tok(D), pl.BlockSpec((1, 3, D), lambda b, i: (b, 0, 0)), pl.BlockSpec((PW, D), lambda b, i: (0, 0)),
                  pl.BlockSpec((4, W_GQKV), lambda b, i: (0, 0))],
        out_specs=[tok(W_GLA), tok(W_GQKV), tok(W_GDN), tok(W_GQKV)],
        out_shape=[jax.ShapeDtypeStruct((bsz, t, w), F32) for w in (W_GLA, W_GQKV, W_GDN, W_GQKV)],
        scratch_shapes=[pltpu.VMEM((tm + 8, W_GQKV), F32)],
        compiler_params=_params(("arbitrary", "arbitrary")),
    )(x, mod3, wpt, conv_w)


def _conv_bwd_call(dconv, pq, conv_w):
    bsz, t, _ = pq.shape
    tt = min(512, t)
    hb = tt // 8
    nt_ = t // tt

    def body(d_ref, dnext_ref, x_ref, w_ref, din_ref, dw_ref, dbuf):
        b, i = pl.program_id(0), pl.program_id(1)

        @pl.when((b == 0) & (i == 0))
        def _():
            dw_ref[...] = jnp.zeros_like(dw_ref)

        dbuf[0:tt, :] = d_ref[0]
        dbuf[tt:, :] = jnp.where(i < nt_ - 1, dnext_ref[0], 0.0)
        rows, unroll = 16, 4
        for j in range(W_GQKV // 128):
            ls = slice(128 * j, 128 * j + 128)
            wj = [jnp.broadcast_to(w_ref[k:k + 1, ls], (rows, 128)) for k in range(4)]

            def trip(r, dw4):
                dw4 = list(dw4)
                for u in range(unroll):
                    base = pl.multiple_of(r * (rows * unroll), rows * unroll) + rows * u
                    win = dbuf[pl.ds(base, rows + 8), ls]
                    xin = x_ref[0, pl.ds(base, rows), ls]
                    acc = None
                    for k in range(4):
                        dsh = win[3 - k:3 - k + rows, :]
                        acc = wj[k] * dsh if acc is None else acc + wj[k] * dsh
                        dw4[k] = dw4[k] + xin * dsh
                    din_ref[0, pl.ds(base, rows), ls] = acc
                return tuple(dw4)

            dw4 = lax.fori_loop(0, tt // (rows * unroll), trip, tuple(jnp.zeros((rows, 128), F32) for _ in range(4)))
            for k in range(4):
                dw_ref[k:k + 1, ls] += jnp.sum(dw4[k], axis=0, keepdims=True)

    tile = pl.BlockSpec((1, tt, W_GQKV), lambda b, i: (b, i, 0))
    return pl.pallas_call(
        body, name="conv_bwd", grid=(bsz, nt_),
        in_specs=[tile, pl.BlockSpec((1, 8, W_GQKV), lambda b, i: (b, jnp.minimum((i + 1) * hb, t // 8 - 1), 0)),
                  tile, pl.BlockSpec((4, W_GQKV), lambda b, i: (0, 0))],
        out_specs=[tile, pl.BlockSpec((8, W_GQKV), lambda b, i: (0, 0))],
        out_shape=[jax.ShapeDtypeStruct(pq.shape, F32), jax.ShapeDtypeStruct((8, W_GQKV), F32)],
        scratch_shapes=[pltpu.VMEM((tt + 8, W_GQKV), F32)],
        compiler_params=_params(("arbitrary", "arbitrary")),
    )(dconv, dconv, pq, conv_w)


SC_TILES = 32
SC_ROWS = 16


def _conv_bwd_sc_call(dconv, pq, conv_w):
    bsz, t, w_ = pq.shape
    n = bsz * t
    per_tile = n // SC_TILES
    pieces = per_tile // SC_ROWS
    R = SC_ROWS
    assert t % per_tile == 0 and per_tile % R == 0

    def body(d_hbm, x_hbm, w_hbm, din_hbm, dwp_hbm, dbuf, xbuf, obuf, wbuf, dwacc):
        tile = lax.axis_index("sc_tile") * 2 + lax.axis_index("sc_core")
        pltpu.sync_copy(w_hbm, wbuf)

        @pl.loop(0, w_, step=16)
        def _(c):
            for k in range(8):
                dwacc[k, pl.ds(c, 16)] = jnp.zeros((16,), F32)

        @pl.loop(0, pieces)
        def _(p):
            r0 = pl.multiple_of(tile * per_tile + p * R, R)
            pltpu.sync_copy(d_hbm.at[pl.ds(r0, R), :], dbuf.at[pl.ds(0, R), :])
            pltpu.sync_copy(x_hbm.at[pl.ds(r0, R), :], xbuf)
            nxt = pl.multiple_of(jnp.minimum(r0 + R, n - 8), 8)
            pltpu.sync_copy(d_hbm.at[pl.ds(nxt, 8), :], dbuf.at[pl.ds(R, 8), :])

            @pl.when((r0 + R) % t == 0)
            def _():
                @pl.loop(0, w_, step=16)
                def _(c):
                    for k in range(3):
                        dbuf[R + k, pl.ds(c, 16)] = jnp.zeros((16,), F32)

            @pl.loop(0, w_, step=16)
            def _(c):
                cs = pl.ds(c, 16)
                w0, w1, w2, w3 = (wbuf[k, cs] for k in range(4))
                a, b, cc = dbuf[0, cs], dbuf[1, cs], dbuf[2, cs]
                s0, s1, s2, s3 = (dwacc[k, cs] for k in range(4))
                for s in range(R):
                    e = dbuf[s + 3, cs]
                    xv = xbuf[s, cs]
                    obuf[s, cs] = w3 * a + w2 * b + w1 * cc + w0 * e
                    s0, s1, s2, s3 = s0 + xv * e, s1 + xv * cc, s2 + xv * b, s3 + xv * a
                    a, b, cc = b, cc, e
                dwacc[0, cs], dwacc[1, cs], dwacc[2, cs], dwacc[3, cs] = s0, s1, s2, s3

            pltpu.sync_copy(obuf, din_hbm.at[pl.ds(r0, R), :])

        pltpu.sync_copy(dwacc, dwp_hbm.at[pl.ds(pl.multiple_of(tile * 8, 8), 8), :])

    din, dwp = pl.kernel(
        body, name="conv_bwd_sc",
        out_type=(jax.ShapeDtypeStruct((n, w_), F32), jax.ShapeDtypeStruct((SC_TILES * 8, w_), F32)),
        mesh=plsc.VectorSubcoreMesh(core_axis_name="sc_core", subcore_axis_name="sc_tile"),
        scratch_types=[pltpu.VMEM((R + 8, w_), F32), pltpu.VMEM((R, w_), F32), pltpu.VMEM((R, w_), F32),
                       pltpu.VMEM((4, w_), F32), pltpu.VMEM((8, w_), F32)],
    )(dconv.reshape(n, w_), pq.reshape(n, w_), conv_w)
    return din.reshape(pq.shape), jnp.sum(dwp.reshape(SC_TILES, 8, w_)[:, :4], axis=0)


def _chunk_specs(nc, rev, bsz, cols, sub=None):
    sub = SUB if sub is None else sub
    steps = nc // sub
    n_of = (lambda n: steps - 1 - n) if rev else (lambda n: n)
    return n_of, [pl.BlockSpec((bsz, sub * CHUNK, w), lambda n, j=j: (0, n_of(n), j)) for w, j in cols]


def _full(shape):
    return pl.BlockSpec(shape, lambda n: (0,) * len(shape))


def _heads(ref, bsz, rows, width, off=0):
    return jnp.stack([ref[b, rows, off + width * h:off + width * (h + 1)] for b in range(bsz) for h in range(H)])


def _chunk_rows(sub):
    return slice(CHUNK * sub, CHUNK * (sub + 1))


def _per_head(ref, bsz, width, rows=slice(None)):
    return jnp.stack([ref[rows, width * h:width * (h + 1)] for _ in range(bsz) for h in range(H)])


_GLA_COLS = [(256, 0), (256, 1), (512, 1), (512, 2), (128, 12)]


def _gla_args(refs, bsz, rows):
    q_ref, k_ref, v_ref, og_ref, lr_ref, wgu_ref, bg_ref, nw_ref = refs
    lr = jnp.stack([lr_ref[b, rows, :] for b in range(bsz) for _ in range(H)])
    return (_heads(q_ref, bsz, rows, 64), _heads(k_ref, bsz, rows, 64), _heads(v_ref, bsz, rows, 128), lr,
            _heads(og_ref, bsz, rows, 128), _per_head(wgu_ref, bsz, 64), _per_head(bg_ref, bsz, 64), nw_ref[...])


def _gla_fwd_call(pg, wgu, bg, nw):
    bsz, t, _ = pg.shape
    nc = t // CHUNK
    nh = bsz * H
    _, specs = _chunk_specs(nc, False, bsz, _GLA_COLS)

    def body(q_ref, k_ref, v_ref, og_ref, lr_ref, wgu_ref, bg_ref, nw_ref, y_ref, sh_ref, s_ref):
        @pl.when(pl.program_id(0) == 0)
        def _():
            s_ref[...] = jnp.zeros_like(s_ref)

        s = s_ref[...]
        for sub in range(SUB):
            rows = _chunk_rows(sub)
            q, k, v, lr, og, w, b_, nw_ = _gla_args((q_ref, k_ref, v_ref, og_ref, lr_ref, wgu_ref, bg_ref, nw_ref), bsz, rows)
            sh_ref[sub] = s
            y, s = gla_chunk(q, k, v, lr, og, s, w, b_, nw_)
            for b in range(bsz):
                for h in range(H):
                    y_ref[b, rows, 128 * h:128 * h + 128] = y[H * b + h].astype(MXU)
        s_ref[...] = s

    return pl.pallas_call(
        body, name="gla_fwd", grid=(nc // SUB,),
        in_specs=specs + [_full((128, 256)), _full((1, 256)), _full((1, 128))],
        out_specs=[pl.BlockSpec((bsz, SUB * CHUNK, 512), lambda n: (0, n, 0)),
                   pl.BlockSpec((SUB, nh, GLA_DK, DV), lambda n: (n, 0, 0, 0))],
        out_shape=[jax.ShapeDtypeStruct((bsz, t, 512), MXU), jax.ShapeDtypeStruct((nc, nh, GLA_DK, DV), F32)],
        scratch_shapes=[pltpu.VMEM((nh, GLA_DK, DV), F32)],
        compiler_params=_params(("arbitrary",)),
    )(pg, pg, pg, pg, pg, wgu, bg, nw)


def _gla_bwd_call(pg, s_hist, dyin, wgu, bg, nw):
    bsz, t, _ = pg.shape
    nc = t // CHUNK
    nh = bsz * H
    n_of, specs = _chunk_specs(nc, True, bsz, _GLA_COLS)

    def body(q_ref, k_ref, v_ref, og_ref, lr_ref, sh_ref, dy_ref, wgu_ref, bg_ref, nw_ref,
             dp_ref, dwgu_ref, dbg_ref, dnw_ref, ds_ref):
        @pl.when(pl.program_id(0) == 0)
        def _():
            dwgu_ref[...] = jnp.zeros_like(dwgu_ref)
            dbg_ref[...] = jnp.zeros_like(dbg_ref)
            dnw_ref[...] = jnp.zeros_like(dnw_ref)
            ds_ref[...] = jnp.zeros_like(ds_ref)

        ds = ds_ref[...]
        for sub in reversed(range(SUB)):
            rows = _chunk_rows(sub)
            q, k, v, lr, og, w, b_, nw_ = _gla_args((q_ref, k_ref, v_ref, og_ref, lr_ref, wgu_ref, bg_ref, nw_ref), bsz, rows)
            _, vjp = jax.vjp(gla_chunk, q, k, v, lr, og, sh_ref[sub], w, b_, nw_)
            dq, dk, dv, dlr, dog, ds, dwgu, dbg, dnw = vjp((_heads(dy_ref, bsz, rows, 128), ds))
            dnw_ref[...] += dnw
            for b in range(bsz):
                for h in range(H):
                    i = H * b + h
                    dp_ref[b, rows, 512 + 128 * h:512 + 128 * h + 128] = dv[i].astype(MXU)
                    dp_ref[b, rows, 1024 + 128 * h:1024 + 128 * h + 128] = dog[i].astype(MXU)
                    dwgu_ref[:, 64 * h:64 * h + 64] += dwgu[i]
                    dbg_ref[:, 64 * h:64 * h + 64] += dbg[i]
                for j in range(H // 2):
                    dp_ref[b, rows, 128 * j:128 * j + 128] = jnp.concatenate(
                        [dq[H * b + 2 * j], dq[H * b + 2 * j + 1]], axis=-1).astype(MXU)
                    dp_ref[b, rows, 256 + 128 * j:256 + 128 * j + 128] = jnp.concatenate(
                        [dk[H * b + 2 * j], dk[H * b + 2 * j + 1]], axis=-1).astype(MXU)
                dp_ref[b, rows, 1536:1664] = (dlr[H * b] + dlr[H * b + 1] + dlr[H * b + 2] + dlr[H * b + 3]).astype(MXU)
        ds_ref[...] = ds

    return pl.pallas_call(
        body, name="gla_bwd", grid=(nc // SUB,),
        in_specs=specs + [pl.BlockSpec((SUB, nh, GLA_DK, DV), lambda n: (n_of(n), 0, 0, 0)),
                          pl.BlockSpec((bsz, SUB * CHUNK, 512), lambda n: (0, n_of(n), 0)),
                          _full((128, 256)), _full((1, 256)), _full((1, 128))],
        out_specs=[pl.BlockSpec((bsz, SUB * CHUNK, W_GLA), lambda n: (0, n_of(n), 0)),
                   _full((128, 256)), _full((1, 256)), _full((1, 128))],
        out_shape=[jax.ShapeDtypeStruct(pg.shape, MXU), jax.ShapeDtypeStruct((128, 256), F32),
                   jax.ShapeDtypeStruct((1, 256), F32), jax.ShapeDtypeStruct((1, 128), F32)],
        scratch_shapes=[pltpu.VMEM((nh, GLA_DK, DV), F32)],
        compiler_params=_params(("arbitrary",)),
    )(pg, pg, pg, pg, pg, s_hist, dyin, wgu, bg, nw)


_GDN_COLS = [(512, 0), (512, 1), (512, 2), (512, 0), (128, 4)]


def _gdn_args(refs, bsz, rows):
    q_ref, k_ref, v_ref, og_ref, ab_ref, sc_ref, nw_ref = refs
    return (_heads(q_ref, bsz, rows, 128), _heads(k_ref, bsz, rows, 128), _heads(v_ref, bsz, rows, 128),
            _heads(ab_ref, bsz, rows, 1), _heads(ab_ref, bsz, rows, 1, off=H), _heads(og_ref, bsz, rows, 128),
            _per_head(sc_ref, bsz, 1, slice(0, 1)), _per_head(sc_ref, bsz, 1, slice(1, 2)), nw_ref[...])


def _gdn_fwd_call(conv, pd, sc, nw):
    bsz, t, _ = conv.shape
    nc = t // CHUNK
    nh = bsz * H
    _, specs = _chunk_specs(nc, False, bsz, _GDN_COLS)

    def body(q_ref, k_ref, v_ref, og_ref, ab_ref, sc_ref, nw_ref, y_ref, sh_ref, th_ref, s_ref):
        @pl.when(pl.program_id(0) == 0)
        def _():
            s_ref[...] = jnp.zeros_like(s_ref)

        s = s_ref[...]
        for sub in range(SUB):
            rows = _chunk_rows(sub)
            q, k, v, a, bb, og, alog, dtb, nw_ = _gdn_args((q_ref, k_ref, v_ref, og_ref, ab_ref, sc_ref, nw_ref), bsz, rows)
            sh_ref[sub] = s
            y, s, tinv = gdn_chunk(q, k, v, a, bb, og, s, alog, dtb, nw_)
            th_ref[sub] = tinv.astype(MXU)
            for b in range(bsz):
                for h in range(H):
                    y_ref[b, rows, 128 * h:128 * h + 128] = y[H * b + h].astype(MXU)
        s_ref[...] = s

    return pl.pallas_call(
        body, name="gdn_fwd", grid=(nc // SUB,),
        in_specs=specs + [_full((2, 128)), _full((1, 128))],
        out_specs=[pl.BlockSpec((bsz, SUB * CHUNK, 512), lambda n: (0, n, 0)),
                   pl.BlockSpec((SUB, nh, DV, DV), lambda n: (n, 0, 0, 0)),
                   pl.BlockSpec((SUB, nh, CHUNK, CHUNK), lambda n: (n, 0, 0, 0))],
        out_shape=[jax.ShapeDtypeStruct((bsz, t, 512), MXU), jax.ShapeDtypeStruct((nc, nh, DV, DV), F32),
                   jax.ShapeDtypeStruct((nc, nh, CHUNK, CHUNK), MXU)],
        scratch_shapes=[pltpu.VMEM((nh, DV, DV), F32)],
        compiler_params=_params(("arbitrary",)),
    )(conv, conv, conv, pd, pd, sc, nw)


def _gdn_bwd_call(conv, pd, s_hist, t_hist, dyin, sc, nw):
    bsz, t, _ = conv.shape
    nc = t // CHUNK
    nh = bsz * H
    n_of, specs = _chunk_specs(nc, True, bsz, _GDN_COLS, SUB_GDN_BWD)

    def body(q_ref, k_ref, v_ref, og_ref, ab_ref, sh_ref, th_ref, dy_ref, sc_ref, nw_ref,
             dc_ref, dpd_ref, dsc_ref, dnw_ref, ds_ref):
        @pl.when(pl.program_id(0) == 0)
        def _():
            dsc_ref[...] = jnp.zeros_like(dsc_ref)
            dnw_ref[...] = jnp.zeros_like(dnw_ref)
            ds_ref[...] = jnp.zeros_like(ds_ref)

        lane = lax.broadcasted_iota(jnp.int32, (CHUNK, 128), 1)
        ds = ds_ref[...]
        for sub in reversed(range(SUB_GDN_BWD)):
            rows = _chunk_rows(sub)
            q, k, v, a, bb, og, alog, dtb, nw_ = _gdn_args((q_ref, k_ref, v_ref, og_ref, ab_ref, sc_ref, nw_ref), bsz, rows)
            _, vjp = jax.vjp(functools.partial(gdn_chunk, tinv=th_ref[sub]), q, k, v, a, bb, og, sh_ref[sub], alog, dtb, nw_)
            dq, dk, dv, da, db, dog, ds, dalog, ddtb, dnw = vjp((_heads(dy_ref, bsz, rows, 128), ds))
            dnw_ref[...] += dnw
            for b in range(bsz):
                dab = jnp.zeros((CHUNK, 128), F32)
                for h in range(H):
                    i = H * b + h
                    dc_ref[b, rows, 128 * h:128 * h + 128] = dq[i]
                    dc_ref[b, rows, 512 + 128 * h:512 + 128 * h + 128] = dk[i]
                    dc_ref[b, rows, 1024 + 128 * h:1024 + 128 * h + 128] = dv[i]
                    dpd_ref[b, rows, 128 * h:128 * h + 128] = dog[i].astype(MXU)
                    dab = dab + jnp.where(lane == h, da[i], 0.0) + jnp.where(lane == H + h, db[i], 0.0)
                    dsc_ref[0:1, h:h + 1] += dalog[i]
                    dsc_ref[1:2, h:h + 1] += ddtb[i]
                dpd_ref[b, rows, 512:640] = dab.astype(MXU)
        ds_ref[...] = ds

    return pl.pallas_call(
        body, name="gdn_bwd", grid=(nc // SUB_GDN_BWD,),
        in_specs=specs + [pl.BlockSpec((SUB_GDN_BWD, nh, DV, DV), lambda n: (n_of(n), 0, 0, 0)),
                          pl.BlockSpec((SUB_GDN_BWD, nh, CHUNK, CHUNK), lambda n: (n_of(n), 0, 0, 0)),
                          pl.BlockSpec((bsz, SUB_GDN_BWD * CHUNK, 512), lambda n: (0, n_of(n), 1)),
                          _full((2, 128)), _full((1, 128))],
        out_specs=[pl.BlockSpec((bsz, SUB_GDN_BWD * CHUNK, W_GQKV), lambda n: (0, n_of(n), 0)),
                   pl.BlockSpec((bsz, SUB_GDN_BWD * CHUNK, W_GDN), lambda n: (0, n_of(n), 0)),
                   _full((2, 128)), _full((1, 128))],
        out_shape=[jax.ShapeDtypeStruct(conv.shape, F32), jax.ShapeDtypeStruct(pd.shape, MXU),
                   jax.ShapeDtypeStruct((2, 128), F32), jax.ShapeDtypeStruct((1, 128), F32)],
        scratch_shapes=[pltpu.VMEM((nh, DV, DV), F32)],
        compiler_params=_params(("arbitrary",)),
    )(conv, conv, conv, pd, pd, s_hist, t_hist, dyin, sc, nw)


def _head_call(x, ya, yb, wout, mod3, lnw, lnb, tgt):
    bsz, t, _ = x.shape
    tm = min(512, t)
    rows = min(256, tm)

    def body(x_ref, ya_ref, yb_ref, w_ref, mod_ref, lnw_ref, lnb_ref, t_ref,
             dyin_ref, dxa_ref, dgate_ref, dw_ref, dlnw_ref, dlnb_ref, loss_ref):
        b, i = pl.program_id(0), pl.program_id(1)

        @pl.when((b == 0) & (i == 0))
        def _():
            dw_ref[...] = jnp.zeros_like(dw_ref)
            dlnw_ref[...] = jnp.zeros_like(dlnw_ref)
            dlnb_ref[...] = jnp.zeros_like(dlnb_ref)
            loss_ref[...] = jnp.zeros_like(loss_ref)

        @pl.when(i == 0)
        def _():
            dgate_ref[...] = jnp.zeros_like(dgate_ref)

        w = w_ref[...]
        parts = [slice(p * rows, (p + 1) * rows) for p in range(tm // rows)]
        yin = [jnp.concatenate([ya_ref[0, rs, :], yb_ref[0, rs, :]], axis=-1).astype(MXU) for rs in parts]
        y = [jnp.dot(yi, w, preferred_element_type=F32) for yi in yin]
        for rs, yi, y_p in zip(parts, yin, y):
            loss, vjp = jax.vjp(head_fn, x_ref[0, rs, :], y_p, mod_ref[0, 2:3, :], lnw_ref[...], lnb_ref[...], t_ref[0, rs, :])
            dx, dy, dgate, dlnw, dlnb, _ = vjp(jnp.ones((1, 1), F32))
            dyb = dy.astype(MXU)
            dyin_ref[0, rs, :] = lax.dot_general(dyb, w, (((1,), (1,)), ((), ())), preferred_element_type=F32)
            dw_ref[...] += lax.dot_general(yi, dyb, (((0,), (0,)), ((), ())), preferred_element_type=F32)
            dxa_ref[0, rs, :] = dx
            dgate_ref[0] += dgate
            dlnw_ref[...] += dlnw
            dlnb_ref[...] += dlnb
            loss_ref[...] += jnp.broadcast_to(loss, (1, 128))

    tok = lambda w, j=0: pl.BlockSpec((1, tm, w), lambda b, i: (b, i, j))
    row = pl.BlockSpec((1, D), lambda b, i: (0, 0))
    return pl.pallas_call(
        body, name="head", grid=(bsz, t // tm),
        in_specs=[tok(D), tok(512), tok(512), pl.BlockSpec((D, D), lambda b, i: (0, 0)),
                  pl.BlockSpec((1, 3, D), lambda b, i: (b, 0, 0)), row, row, tok(D)],
        out_specs=[tok(D), tok(D), pl.BlockSpec((1, 1, D), lambda b, i: (b, 0, 0)),
                   pl.BlockSpec((D, D), lambda b, i: (0, 0)), row, row, pl.BlockSpec((1, 128), lambda b, i: (0, 0))],
        out_shape=[jax.ShapeDtypeStruct(x.shape, F32), jax.ShapeDtypeStruct(x.shape, F32),
                   jax.ShapeDtypeStruct((bsz, 1, D), F32), jax.ShapeDtypeStruct((D, D), F32),
                   jax.ShapeDtypeStruct((1, D), F32), jax.ShapeDtypeStruct((1, D), F32),
                   jax.ShapeDtypeStruct((1, 128), F32)],
        compiler_params=_params(("arbitrary", "arbitrary")),
    )(x, ya, yb, wout, mod3, lnw, lnb, tgt)


def _dh_call(dpg, dpq, dpd, wpt, x, mod3, dxa, smalls):
    bsz, t, _ = x.shape
    tm = min(512, t)
    assert bsz + 1 <= 8

    def body(dg_ref, dq_ref, dd_ref, w_ref, x_ref, mod_ref, dxa_ref, dgate_ref, dlnw_ref, dlnb_ref, dbg_ref, n1_ref, n2_ref,
             dsc_ref, loss_ref, gx_ref, sp_ref):
        b = pl.program_id(0)

        @pl.when((b == 0) & (pl.program_id(1) == 0))
        def _():
            sp_ref[...] = jnp.zeros_like(sp_ref)
            for e in range(bsz):
                sp_ref[e:e + 1, 2 * D:3 * D] = dgate_ref[e]
            off = 0
            for ref in (dlnw_ref, dlnb_ref, dbg_ref, n1_ref, n2_ref):
                sp_ref[bsz:bsz + 1, off:off + ref.shape[1]] = ref[...]
                off += ref.shape[1]
            sp_ref[bsz:bsz + 1, off:off + 128] = dsc_ref[0:1, :]
            sp_ref[bsz:bsz + 1, off + 128:off + 256] = dsc_ref[1:2, :]
            sp_ref[bsz:bsz + 1, SMALL_W:SMALL_W + 128] = loss_ref[...]

        mm = lambda a, lo, hi: jnp.dot(a.astype(MXU), w_ref[lo:hi, :], preferred_element_type=F32)
        dh = mm(dg_ref[0], 0, W_GLA) + mm(dq_ref[0], W_GLA, W_GLA + W_GQKV) + mm(dd_ref[0], W_GLA + W_GQKV, PW)
        gx_ref[0] = dh * (1.0 + mod_ref[0, 1:2, :]) + dxa_ref[0]
        dshift = jnp.sum(dh, axis=0, keepdims=True)
        dscale = jnp.sum(dh * x_ref[0], axis=0, keepdims=True)
        for e in range(bsz):
            @pl.when(b == e)
            def _():
                sp_ref[e:e + 1, 0:D] += dshift
                sp_ref[e:e + 1, D:2 * D] += dscale

    tok = lambda w: pl.BlockSpec((1, tm, w), lambda b, i: (b, i, 0))
    whole = lambda a: pl.BlockSpec(a.shape, lambda b, i: (0,) * a.ndim)
    return pl.pallas_call(
        body, name="dh", grid=(bsz, t // tm),
        in_specs=[tok(W_GLA), tok(W_GQKV), tok(W_GDN), pl.BlockSpec((PW, D), lambda b, i: (0, 0)), tok(D),
                  pl.BlockSpec((1, 3, D), lambda b, i: (b, 0, 0)), tok(D)] + [whole(a) for a in smalls],
        out_specs=[tok(D), pl.BlockSpec((8, SPW), lambda b, i: (0, 0))],
        out_shape=[jax.ShapeDtypeStruct(x.shape, F32), jax.ShapeDtypeStruct((8, SPW), F32)],
        compiler_params=_params(("arbitrary", "arbitrary")),
    )(dpg, dpq, dpd, wpt, x, mod3, dxa, *smalls)


def _dw_call(name, x, mod3, parts):
    bsz, t, _ = x.shape
    tm = min(512, t)
    nsteps = bsz * (t // tm)
    widths = [p.shape[-1] for p in parts]
    rows = sum(widths)

    def body(x_ref, mod_ref, *refs):
        dp_refs, dw_ref, acc = refs[:len(parts)], refs[-2], refs[-1]
        step = pl.program_id(0) * (t // tm) + pl.program_id(1)

        @pl.when(step == 0)
        def _():
            acc[...] = jnp.zeros_like(acc)

        h = (x_ref[0] * (1.0 + mod_ref[0, 1:2, :]) + mod_ref[0, 0:1, :]).astype(MXU)
        lo = 0
        for ref, wd in zip(dp_refs, widths):
            acc[lo:lo + wd, :] += lax.dot_general(ref[0].astype(MXU), h, (((0,), (0,)), ((), ())), preferred_element_type=F32)
            lo += wd

        @pl.when(step == nsteps - 1)
        def _():
            dw_ref[...] = acc[...].astype(dw_ref.dtype)

    tok = lambda w: pl.BlockSpec((1, tm, w), lambda b, i: (b, i, 0))
    return pl.pallas_call(
        body, name=name, grid=(bsz, t // tm),
        in_specs=[tok(D), pl.BlockSpec((1, 3, D), lambda b, i: (b, 0, 0))] + [tok(wd) for wd in widths],
        out_specs=pl.BlockSpec((rows, D), lambda b, i: (0, 0)),
        out_shape=jax.ShapeDtypeStruct((rows, D), WIRE),
        scratch_shapes=[pltpu.VMEM((rows, D), F32)],
        compiler_params=_params(("arbitrary", "arbitrary")),
    )(x, mod3, *parts)


def _adamw(w, g, m, v):
    m = ADAM_B1 * m + (1.0 - ADAM_B1) * g
    v = ADAM_B2 * v + (1.0 - ADAM_B2) * jnp.square(g)
    m_hat = m / (1.0 - ADAM_B1 ** ADAM_STEP)
    v_hat = v / (1.0 - ADAM_B2 ** ADAM_STEP)
    delta = -ADAM_LR * (m_hat / (jnp.sqrt(v_hat) + ADAM_EPS) + ADAM_WD * w)
    return delta, m, v


def _sum8(ref):
    g = ref[0].astype(F32)
    for j in range(1, NDEV):
        g = g + ref[j].astype(F32)
    return g


def _adam_sum_call(name, g8, w, m, v, cols):
    r, c = w.shape

    def body(g_ref, w_ref, m_ref, v_ref, go_ref, d_ref, mo_ref, vo_ref):
        g = _sum8(g_ref)
        go_ref[...] = g
        d_ref[...], mo_ref[...], vo_ref[...] = _adamw(w_ref[...], g, m_ref[...], v_ref[...])

    blk = pl.BlockSpec((r, cols), lambda i: (0, i))
    return pl.pallas_call(
        body, name=name, grid=(c // cols,),
        in_specs=[pl.BlockSpec((NDEV, r, cols), lambda i: (0, 0, i)), blk, blk, blk],
        out_specs=[blk] * 4, out_shape=[jax.ShapeDtypeStruct((r, c), F32)] * 4,
        compiler_params=_params(("parallel",)),
    )(g8, w, m, v)


def _adam_ada_call(c_all, dmod_cols, w, m, v):
    def body(c_ref, dm_ref, w_ref, m_ref, v_ref, go_ref, d_ref, mo_ref, vo_ref):
        g = lax.dot_general(c_ref[...].astype(MXU), dm_ref[...].astype(MXU), (((0,), (0,)), ((), ())),
                            preferred_element_type=F32)
        go_ref[...] = g
        d_ref[...], mo_ref[...], vo_ref[...] = _adamw(w_ref[...], g, m_ref[...], v_ref[...])

    return pl.pallas_call(
        body, name="adam_ada", out_shape=[jax.ShapeDtypeStruct(w.shape, F32)] * 4, compiler_params=_params(),
    )(c_all, dmod_cols, w, m, v)


_SMALL_AT = dict(ln_w=(0, 1024), ln_b=(1024, 1024), b_gate=(2048, 256), gla_nw=(2304, 128), gdn_nw=(2432, 128),
                 a_log=(2560, 4), dt_bias=(2688, 4))


def _adam_small_call(sp_all, bsz, params):
    names = list(params)

    def body(sp_ref, *refs):
        ins, outs = refs[:3 * len(names)], refs[3 * len(names):]
        dmod_ref, loss_ref, outs = outs[0], outs[1], outs[2:]
        packed = sp_ref[0, bsz:bsz + 1, :]
        for j in range(1, NDEV):
            packed = packed + sp_ref[j, bsz:bsz + 1, :]
        gb = None
        for j in range(NDEV):
            dmod_ref[bsz * j:bsz * j + bsz, :] = sp_ref[j, 0:bsz, :]
            for e in range(bsz):
                gb = sp_ref[j, e:e + 1, :] if gb is None else gb + sp_ref[j, e:e + 1, :]
        loss_ref[...] = packed[:, SMALL_W:SMALL_W + 128]
        for i, name in enumerate(names):
            if name == "b_ada":
                g = gb
            else:
                lo, n = _SMALL_AT[name]
                g = packed[:, lo:lo + n]
            w_ref, m_ref, v_ref = ins[3 * i:3 * i + 3]
            g_ref, d_ref, mo_ref, vo_ref = outs[4 * i:4 * i + 4]
            g_ref[...] = g
            d_ref[...], mo_ref[...], vo_ref[...] = _adamw(w_ref[...], g, m_ref[...], v_ref[...])

    flat = [a for name in names for a in params[name]]
    out_shape = [jax.ShapeDtypeStruct((NDEV * bsz, SPW), F32), jax.ShapeDtypeStruct((1, 128), F32)]
    out_shape += [jax.ShapeDtypeStruct(params[name][0].shape, F32) for name in names for _ in range(4)]
    res = pl.pallas_call(body, name="adam_small", out_shape=out_shape, compiler_params=_params())(sp_all, *flat)
    return res[0], res[1], {name: res[2 + 4 * i:6 + 4 * i] for i, name in enumerate(names)}


def _mesh_pos():
    x, y, c = lax.axis_index("x"), lax.axis_index("y"), lax.axis_index("c")
    return x, y, c, 4 * x + 2 * y + c


def _peer(x, y, c, k):
    px = 1 - x if k & 4 else x
    py = 1 - y if k & 2 else y
    pc = 1 - c if k & 1 else c
    return (px, py, pc), 4 * px + 2 * py + pc


_ANY = pl.BlockSpec(memory_space=pl.ANY)
_VMEM = pl.BlockSpec(memory_space=pltpu.VMEM)


def _gather_call(c8, w_ada, b_sh, w_in_t, conv_w):
    C_SEM, W_SEM, MOD_SEM, CONV_SEM = 0, 1, 2, 3

    def body(c_ref, wada_ref, b_ref, win_ref, cw_ref, wall_ref, call_ref, mod_ref, cwall_ref, modp, send_sems, recv_sems, loc_sem):
        x, y, c, me = _mesh_pos()

        def remote(src, dst, a, k, to):
            return pltpu.make_async_remote_copy(src_ref=src, dst_ref=dst, send_sem=send_sems.at[a, k],
                                                recv_sem=recv_sems.at[a, k], device_id=_peer(x, y, c, to)[0],
                                                device_id_type=pl.DeviceIdType.MESH)

        idx = lambda k: _peer(x, y, c, k)[1]
        sends = []
        call_ref[me] = c_ref[...]
        cwall_ref[me] = cw_ref[...]
        for k in range(1, NDEV):
            sends.append(remote(c_ref, call_ref.at[me], C_SEM, k, k))
            sends[-1].start()
            sends.append(remote(cw_ref, cwall_ref.at[me], CONV_SEM, k, k))
            sends[-1].start()
        local = pltpu.make_async_copy(win_ref, wall_ref.at[me], loc_sem)
        local.start()
        for k in (1, 2, 4, 6):
            sends.append(remote(win_ref, wall_ref.at[me], W_SEM, k, k))
            sends[-1].start()
        for k in range(1, NDEV):
            remote(c_ref, call_ref.at[idx(k)], C_SEM, k, k).wait_recv()
        modp[...] = jnp.dot(call_ref[...].reshape(NDEV * 8, D).astype(MXU), wada_ref[...].astype(MXU),
                            preferred_element_type=F32) + b_ref[...]
        mod_ref[me] = modp[pl.ds(pl.multiple_of(me * 8, 8), 8), :]
        for k in range(1, NDEV):
            sends.append(remote(modp.at[pl.ds(pl.multiple_of(idx(k) * 8, 8), 8), :], mod_ref.at[me], MOD_SEM, k, k))
            sends[-1].start()
        for k in (2, 4, 6):
            remote(win_ref, wall_ref.at[idx(k)], W_SEM, k, k).wait_recv()
            sends.append(remote(wall_ref.at[idx(k)], wall_ref.at[idx(k)], W_SEM, k + 1, 1))
            sends[-1].start()
        for k in (1, 3, 5, 7):
            remote(win_ref, wall_ref.at[idx(k)], W_SEM, k, 1).wait_recv()
        for k in range(1, NDEV):
            remote(modp.at[pl.ds(0, 8), :], mod_ref.at[idx(k)], MOD_SEM, k, k).wait_recv()
            remote(cw_ref, cwall_ref.at[idx(k)], CONV_SEM, k, k).wait_recv()
        for cp in sends:
            cp.wait_send()
        local.wait()

    return pl.pallas_call(
        body, name="gather",
        out_shape=[jax.ShapeDtypeStruct((NDEV,) + w_in_t.shape, w_in_t.dtype), jax.ShapeDtypeStruct((NDEV, 8, D), F32),
                   jax.ShapeDtypeStruct((NDEV, 8, SHARD_ADA), F32), jax.ShapeDtypeStruct((NDEV,) + conv_w.shape, F32)],
        in_specs=[_VMEM, _VMEM, _VMEM, _ANY, _VMEM], out_specs=[_ANY, _VMEM, _VMEM, _VMEM],
        scratch_shapes=[pltpu.VMEM((NDEV * 8, SHARD_ADA), F32), pltpu.SemaphoreType.DMA((4, NDEV)),
                        pltpu.SemaphoreType.DMA((4, NDEV)), pltpu.SemaphoreType.DMA],
        compiler_params=_params(),
    )(c8, w_ada, b_sh, w_in_t, conv_w)


_HBM = pl.BlockSpec(memory_space=pltpu.HBM)
_SEM = pl.BlockSpec(memory_space=pltpu.SEMAPHORE)
_EFFECT = pltpu.SideEffectType.DATAFLOW_SIDE_EFFECTING


def _whole(gather, a):
    return gather[a] if isinstance(gather, (list, tuple)) else gather


def _xchg_start(name, blocks, lands, gather):
    nb = len(blocks)

    def body(*refs):
        srcs, dsts = refs[:nb], refs[nb:2 * nb]
        send_sems, recv_sems = refs[2 * nb], refs[2 * nb + 1]
        token = refs[-1]
        x, y, c, me = _mesh_pos()
        for k in range(1, NDEV):
            dev, pidx = _peer(x, y, c, k)
            for a in range(nb):
                pltpu.make_async_remote_copy(src_ref=srcs[a] if _whole(gather, a) else srcs[a].at[pidx], dst_ref=dsts[a].at[me],
                                             send_sem=send_sems.at[NDEV * a + k], recv_sem=recv_sems.at[NDEV * a + k],
                                             device_id=dev, device_id_type=pl.DeviceIdType.MESH).start()
        token[...] = jnp.zeros_like(token)

    thru = [pltpu.HBM(a.shape, a.dtype) for a in list(blocks) + list(lands)]
    return pl.pallas_call(
        body, name=name,
        out_shape=(pltpu.SemaphoreType.DMA((nb * NDEV,)), pltpu.SemaphoreType.DMA((nb * NDEV,)), *thru,
                   jax.ShapeDtypeStruct((8, 128), F32)),
        in_specs=[_HBM] * (2 * nb), out_specs=(_SEM, _SEM, *([_HBM] * (2 * nb)), _VMEM),
        input_output_aliases={i: 2 + i for i in range(2 * nb)},
        compiler_params=pltpu.CompilerParams(has_side_effects=_EFFECT),
    )(*[pltpu.with_memory_space_constraint(a, pltpu.HBM) for a in list(blocks) + list(lands)])


def _xchg_wait(name, send_sems, recv_sems, thru, after, gather):
    nb = len(thru) // 2

    def body(*refs):
        srcs, dsts = refs[:nb], refs[nb:2 * nb]
        send_sems, recv_sems = refs[2 * nb], refs[2 * nb + 1]
        x, y, c, me = _mesh_pos()
        for k in range(1, NDEV):
            dev, pidx = _peer(x, y, c, k)
            for a in range(nb):
                cp = pltpu.make_async_remote_copy(src_ref=srcs[a] if _whole(gather, a) else srcs[a].at[pidx], dst_ref=dsts[a].at[pidx],
                                                  send_sem=send_sems.at[NDEV * a + k], recv_sem=recv_sems.at[NDEV * a + k],
                                                  device_id=dev, device_id_type=pl.DeviceIdType.MESH)
                cp.wait_send()
                cp.wait_recv()

    out = pl.pallas_call(
        body, name=name, out_shape=tuple(pltpu.HBM(a.shape, a.dtype) for a in thru),
        in_specs=[_HBM] * (2 * nb) + [_SEM, _SEM, pl.BlockSpec(memory_space=pl.ANY)], out_specs=tuple([_HBM] * (2 * nb)),
        input_output_aliases={i: i for i in range(2 * nb)},
        compiler_params=pltpu.CompilerParams(has_side_effects=_EFFECT),
    )(*thru, send_sems, recv_sems, after)
    return out[nb:]


def _pad_cols(a, n):
    return jnp.pad(a, ((0, 0), (0, n - a.shape[1])))


_SEGMENTS = ((0, 256, 0), (256, 512, 256), (512, 1024, 512), (1040, 1552, 1024), (1024, 1040, 1536),
             (1552, 3088, W_GLA), (3096, 3608, W_GLA + W_GQKV), (3088, 3096, W_GLA + W_GQKV + 512))


def _row_pieces():
    out = []
    for lo, hi, dst in _SEGMENTS:
        while lo < hi:
            j, off = divmod(lo, SHARD_IN)
            n = min(hi - lo, SHARD_IN - off)
            out.append((j, off, n, dst))
            lo, dst = lo + n, dst + n
    return out


def _relayout_call(a, to_layout):
    cols = 256
    dst_shape = (PW, D) if to_layout else (NDEV, SHARD_IN, D)

    def body(i_ref, o_ref, scr):
        if to_layout:
            scr[...] = jnp.zeros_like(scr)
        for j, off, n, at in _row_pieces():
            if to_layout:
                scr[at:at + n, :] = i_ref[j, off:off + n, :].astype(F32)
            else:
                scr[j, off:off + n, :] = i_ref[at:at + n, :].astype(F32)
        o_ref[...] = scr[...].astype(o_ref.dtype)

    blk = lambda shape: pl.BlockSpec(shape[:-1] + (cols,), lambda i: (0,) * (len(shape) - 1) + (i,))
    return pl.pallas_call(
        body, name="to_layout" if to_layout else "to_shards", grid=(D // cols,),
        in_specs=[blk(a.shape)], out_specs=blk(dst_shape), out_shape=jax.ShapeDtypeStruct(dst_shape, a.dtype),
        scratch_shapes=[pltpu.VMEM(dst_shape[:-1] + (cols,), F32)],
        compiler_params=_params(("parallel",)),
    )(a)


def _assemble_wt(blocks):
    return _relayout_call(blocks, True)


def _disassemble_dwt(dwt):
    return _relayout_call(dwt, False)


def local_grads(x, mod3, wp, conv_w, late_weights, bg, gla_nw, sc, gdn_nw, lnw, lnb, tgt):
    pg, pq, pd, conv = _proj_call(x, mod3, wp, conv_w)
    wout, wgu_p = late_weights(pq)
    ya, s_gla = _gla_fwd_call(pg, wgu_p, bg, gla_nw)
    yb, s_gdn, t_gdn = _gdn_fwd_call(conv, pd, sc, gdn_nw)
    dyin, dxa, dgate, dwout, dlnw, dlnb, loss = _head_call(x, ya, yb, wout, mod3, lnw, lnb, tgt)
    dconv, dpd, dsc, dnw_gdn = _gdn_bwd_call(conv, pd, s_gdn, t_gdn, dyin, sc, gdn_nw)
    dpq, dconv_w = _conv_bwd_sc_call(dconv, pq, conv_w)
    dpg, dwgu, dbg, dnw_gla = _gla_bwd_call(pg, s_gla, dyin, wgu_p, bg, gla_nw)
    dwt_a = _dw_call("dw_a", x, mod3, [dpg, dpd])
    dwt_b = _dw_call("dw_b", x, mod3, [dpq])
    dw_in = _disassemble_dwt(jnp.concatenate([dwt_a[:W_GLA], dwt_b, dwt_a[W_GLA:]], axis=0))
    g = dict(dw_in=dw_in, dwout=dwout, dconv_w=dconv_w, dwgu=dwgu[:16])
    smalls = (dgate, dlnw, dlnb, dbg, dnw_gla, dnw_gdn, dsc, loss)
    return g, lambda mod3_: _dh_call(dpg, dpq, dpd, wp, x, mod3_, dxa, smalls)


def local_step(x, mod3, wp, wout, conv_w, wgu_p, *args):
    g, finish = local_grads(x, mod3, wp, conv_w, lambda _: (wout, wgu_p), *args)
    g["gx"], sp = finish(mod3)
    bsz = x.shape[0]
    g["dmod"] = sp[:bsz].reshape(bsz, 3, D)
    g["loss"] = sp[bsz, SMALL_W]
    for name, (lo, n) in _SMALL_AT.items():
        g[name] = sp[bsz:bsz + 1, lo:lo + n]
    return g


def kernel(x, c, w_ada, b_ada, w_in, gla_w_gate_up, gla_b_gate, gla_norm_w, gdn_conv_w, gdn_a_log, gdn_dt_bias, gdn_norm_w, w_out, ln_w, ln_b, loss_target, m_w_ada, m_b_ada, m_w_in, m_gla_w_gate_up, m_gla_b_gate, m_gla_norm_w, m_gdn_conv_w, m_gdn_a_log, m_gdn_dt_bias, m_gdn_norm_w, m_w_out, m_ln_w, m_ln_b, v_w_ada, v_b_ada, v_w_in, v_gla_w_gate_up, v_gla_b_gate, v_gla_norm_w, v_gdn_conv_w, v_gdn_a_log, v_gdn_dt_bias, v_gdn_norm_w, v_w_out, v_ln_w, v_ln_b):
    me = 4 * lax.axis_index("x") + 2 * lax.axis_index("y") + lax.axis_index("c")
    bsz = x.shape[0]

    b_sh = lax.dynamic_slice(b_ada, (0, me * SHARD_ADA), (1, SHARD_ADA))
    c8 = jnp.pad(c, ((0, 8 - bsz), (0, 0)))
    w_in_t, m_in_t, v_in_t = (jnp.swapaxes(a[0], 0, 1) for a in (w_in, m_w_in, v_w_in))
    win_all, c_all, mod_blk, conv_all = _gather_call(c8, w_ada[0], b_sh, w_in_t.astype(WIRE), gdn_conv_w[0])
    conv_w = jnp.transpose(conv_all, (1, 0, 2)).reshape(4, W_GQKV)
    wp = _assemble_wt(win_all)
    mod = jnp.transpose(mod_blk[:, :bsz, :], (1, 0, 2)).reshape(bsz, 3 * D)
    mod3 = mod.reshape(bsz, 3, D)
    sc = jnp.concatenate([_pad_cols(gdn_a_log, 128), _pad_cols(gdn_dt_bias, 128)], axis=0)

    own = lambda a: lax.dynamic_update_slice(lax.empty((NDEV,) + a.shape, a.dtype), a[None], (me,) + (0,) * a.ndim)
    late = [w_out[0].astype(WIRE), gla_w_gate_up[0] + 0.0 * mod_blk[0, 0, 0]]
    w_send, w_recv, *w_thru, w_token = _xchg_start("wgather_start", late, [own(a) for a in late], gather=True)

    def late_weights(pq):
        wout_all, wgu_all = _xchg_wait("wgather_wait", w_send, w_recv, w_thru, pq, gather=True)
        return wout_all.reshape(D, D), jnp.pad(jnp.transpose(wgu_all, (1, 0, 2)).reshape(16, 256), ((0, 112), (0, 0)))

    g, finish = local_grads(x, mod3 + w_token[0, 0], wp, conv_w, late_weights, gla_b_gate, gla_norm_w, sc, gdn_norm_w, ln_w, ln_b,
                            loss_target)

    big = [g["dw_in"], g["dwout"].reshape(NDEV, D // NDEV, D).astype(WIRE)]
    lands = [lax.dynamic_update_slice(lax.empty(a.shape, a.dtype), lax.dynamic_slice(a, (me, 0, 0), (1,) + a.shape[1:]),
                                      (me, 0, 0)) for a in big]
    send_sems, recv_sems, *thru, token = _xchg_start("xchg_start", big, lands, gather=False)
    gx, sp = finish(mod3 + token[0, 0])
    little = [jnp.transpose(g["dconv_w"].reshape(4, NDEV, W_GQKV // NDEV), (1, 0, 2)),
              jnp.transpose(g["dwgu"].reshape(16, NDEV, 256 // NDEV), (1, 0, 2)), sp]
    modes = [False, False, True]
    l_lands = [lax.dynamic_update_slice(lax.empty(a.shape, a.dtype), lax.dynamic_slice(a, (me, 0, 0), (1,) + a.shape[1:]),
                                        (me, 0, 0)) for a in little[:2]] + [own(sp)]
    l_send, l_recv, *l_thru, l_token = _xchg_start("small_start", little, l_lands, gather=modes)
    r_in, r_out = _xchg_wait("xchg_wait", send_sems, recv_sems, thru, l_token, gather=False)

    t_in = [jnp.swapaxes(a, 0, 1) for a in _adam_sum_call("adam_in", r_in, w_in_t, m_in_t, v_in_t, 256)]
    t_out = _adam_sum_call("adam_out", r_out, w_out[0], m_w_out[0], v_w_out[0], D)
    r_conv, r_gu, sp_all = _xchg_wait("small_wait", l_send, l_recv, l_thru, t_out[3], gather=modes)
    t_conv = _adam_sum_call("adam_conv", r_conv, gdn_conv_w[0], m_gdn_conv_w[0], v_gdn_conv_w[0], W_GQKV // NDEV)
    t_gu = _adam_sum_call("adam_gu", r_gu, gla_w_gate_up[0], m_gla_w_gate_up[0], v_gla_w_gate_up[0], 256 // NDEV)
    dmod_all, loss, small = _adam_small_call(sp_all, bsz, dict(
        b_ada=(b_ada, m_b_ada, v_b_ada), ln_w=(ln_w, m_ln_w, v_ln_w), ln_b=(ln_b, m_ln_b, v_ln_b),
        b_gate=(gla_b_gate, m_gla_b_gate, v_gla_b_gate), gla_nw=(gla_norm_w, m_gla_norm_w, v_gla_norm_w),
        gdn_nw=(gdn_norm_w, m_gdn_norm_w, v_gdn_norm_w), a_log=(gdn_a_log, m_gdn_a_log, v_gdn_a_log),
        dt_bias=(gdn_dt_bias, m_gdn_dt_bias, v_gdn_dt_bias)))
    c16 = c_all[:, :bsz, :].reshape(NDEV * bsz, D)
    t_ada = _adam_ada_call(c16, lax.dynamic_slice(dmod_all, (0, me * SHARD_ADA), (NDEV * bsz, SHARD_ADA)),
                           w_ada[0], m_w_ada[0], v_w_ada[0])

    def group(i):
        s = lambda name: small[name][i]
        return [t_ada[i][None], s("b_ada"), t_in[i][None], t_gu[i][None], s("b_gate"), s("gla_nw"), t_conv[i][None],
                s("a_log"), s("dt_bias"), s("gdn_nw"), t_out[i][None], s("ln_w"), s("ln_b")]

    return (loss[0, 0], gx, *group(0), *group(1), *group(2), *group(3))
```

```python
import functools

import jax
import jax.numpy as jnp
from jax import lax
from jax.experimental import pallas as pl
from jax.experimental.pallas import tpu as pltpu
from jax.experimental.pallas import tpu_sc as plsc

F32 = jnp.float32
MXU = jnp.bfloat16
WIRE = jnp.bfloat16
HI = lax.Precision.HIGH

D = 1024
NDEV = 8
H = 4
GLA_DK = 64
DV = 128
CHUNK = 64
SUB = 8
SUB_GDN_BWD = 4
LN_EPS = 1e-5
RMS_EPS = 1e-6
ALPHA = 2.0 ** 0.25
GATE_NORM = 16.0

W_GLA, W_GQKV, W_GDN = 1664, 1536, 640
PW = W_GLA + W_GQKV + W_GDN
IN_COLS = 3608
SHARD_IN = IN_COLS // NDEV
SHARD_ADA = 3 * D // NDEV
SPW = 3 * D
SMALL_W = 2816

ADAM_LR, ADAM_B1, ADAM_B2, ADAM_EPS, ADAM_WD, ADAM_STEP = 0.001, 0.9, 0.999, 1e-08, 0.01, 10

VMEM_LIMIT = 56 * 1024 * 1024


def _params(sem=None, **kw):
    if sem is not None:
        kw["dimension_semantics"] = sem
    return pltpu.CompilerParams(vmem_limit_bytes=VMEM_LIMIT, **kw)


_MM = (((2,), (1,)), ((0,), (0,)))
_NT = (((2,), (2,)), ((0,), (0,)))
_TN = (((1,), (1,)), ((0,), (0,)))


def _dg(a, b, dims):
    return lax.dot_general(a.astype(MXU), b.astype(MXU), dims, preferred_element_type=F32)


def _hdg(a, b, dims):
    return lax.dot_general(a, b, dims, precision=HI, preferred_element_type=F32)


@jax.custom_vjp
def bmm(a, b):
    return _dg(a, b, _MM)


bmm.defvjp(lambda a, b: (_dg(a, b, _MM), (a, b)), lambda r, g: (_dg(g, r[1], _NT), _dg(r[0], g, _TN)))


@jax.custom_vjp
def bnt(a, b):
    return _dg(a, b, _NT)


bnt.defvjp(lambda a, b: (_dg(a, b, _NT), (a, b)), lambda r, g: (_dg(g, r[1], _MM), _dg(g, r[0], _TN)))


@jax.custom_vjp
def btn(a, b):
    return _dg(a, b, _TN)


btn.defvjp(lambda a, b: (_dg(a, b, _TN), (a, b)), lambda r, g: (_dg(r[1], g, _NT), _dg(r[0], g, _MM)))


def unit_lower_inverse(a):
    n = a.shape[-1]
    r, c = _iotas(n)
    p = -a
    t = (r == c).astype(F32) + p
    for _ in range(5):
        p = _dg(p, p, _MM)
        t = t + _dg(t, p, _MM)
    return t


@jax.custom_vjp
def unit_lower_solve(a, t, r1, r2):
    return _dg(t, r1, _MM), _dg(t, r2, _MM)


def _solve_fwd(a, t, r1, r2):
    s1, s2 = _dg(t, r1, _MM), _dg(t, r2, _MM)
    return (s1, s2), (t, s1, s2)


def _solve_bwd(res, g):
    t, s1, s2 = res
    d1, d2 = _dg(t, g[0], _TN), _dg(t, g[1], _TN)
    return -(_dg(d1, s1, _NT) + _dg(d2, s2, _NT)), jnp.zeros_like(t), d1, d2


unit_lower_solve.defvjp(_solve_fwd, _solve_bwd)


def _iotas(n):
    return lax.broadcasted_iota(jnp.int32, (n, n), 0), lax.broadcasted_iota(jnp.int32, (n, n), 1)


def _col_to_row(col, eye):
    return jnp.sum(jnp.where(eye, col, 0.0), axis=1, keepdims=True)


def _row_to_col(row, eye):
    return jnp.sum(jnp.where(eye, row, 0.0), axis=2, keepdims=True)


def _pick_row(m, i):
    r = lax.broadcasted_iota(jnp.int32, m.shape, 1)
    return jnp.sum(jnp.where(r == i, m, 0.0), axis=1, keepdims=True)


def _rms_gate(o, nw, og):
    on = o * lax.rsqrt(jnp.mean(o * o, axis=-1, keepdims=True) + RMS_EPS) * nw
    return on * jax.nn.silu(og)


def gla_chunk(q, k, v, lr, og, s, wgu, bg, nw):
    n, c, _ = q.shape
    r, cc = _iotas(c)
    causal = r >= cc
    qs = q * (GLA_DK ** -0.5)
    z = bmm(lr, wgu) + bg
    g = jax.nn.log_sigmoid(z) / GATE_NORM
    b = _hdg(jnp.broadcast_to(causal.astype(F32), (n, c, c)), g, _MM)
    bref = _pick_row(b, c // 2 - 1)
    blast = _pick_row(b, c - 1)
    att = jnp.where(causal, bnt(qs * jnp.exp(b - bref), k * jnp.exp(bref - b)), 0.0)
    o = bmm(att, v) + bmm(qs * jnp.exp(b), s)
    rk, ck = _iotas(GLA_DK)
    s_new = _row_to_col(jnp.exp(blast), rk == ck) * s + btn(k * jnp.exp(blast - b), v)
    return _rms_gate(o, nw, og), s_new


def gdn_chunk(cq, ck, cv, a, bb, og, s, alog, dtb, nw, tinv=None):
    c = cq.shape[1]
    r, cc = _iotas(c)
    eye, causal, strict = r == cc, r >= cc, r > cc
    q, k, v = jax.nn.silu(cq), jax.nn.silu(ck), jax.nn.silu(cv)
    q = q * lax.rsqrt(jnp.sum(q * q, axis=-1, keepdims=True) + RMS_EPS) * (DV ** -0.5)
    k = k * lax.rsqrt(jnp.sum(k * k, axis=-1, keepdims=True) + RMS_EPS)
    g = -jnp.exp(alog) * jax.nn.softplus(a + dtb)
    beta = jax.nn.sigmoid(bb)
    d = jnp.sum(jnp.where(causal, _col_to_row(g, eye), 0.0), axis=2, keepdims=True)
    el = jnp.exp(jnp.where(causal, d - _col_to_row(d, eye), -jnp.inf))
    kb = k * beta
    amat = jnp.where(strict, bnt(kb, k) * el, 0.0)
    t = unit_lower_inverse(amat) if tinv is None else tinv
    u, w = unit_lower_solve(amat, t, v * beta, kb * jnp.exp(d))
    qk = jnp.where(causal, bnt(q, k) * el, 0.0)
    dlast = _pick_row(d, c - 1)
    v_new = u - bmm(w, s)
    o = bmm(q * jnp.exp(d), s) + bmm(qk, v_new)
    s_new = jnp.exp(dlast) * s + btn(k * jnp.exp(dlast - d), v_new)
    y = _rms_gate(o, nw, og)
    return (y, s_new, t) if tinv is None else (y, s_new)


def head_fn(x, y, gate, lnw, lnb, tgt):
    u = ALPHA * x + (1.0 + gate) * y
    mu = jnp.mean(u, axis=-1, keepdims=True)
    var = jnp.mean(jnp.square(u - mu), axis=-1, keepdims=True)
    out = (u - mu) * lax.rsqrt(var + LN_EPS) * lnw + lnb
    err = jnp.square(out - tgt)
    return 0.5 * jnp.sum(jnp.mean(err, axis=-1, keepdims=True), axis=0, keepdims=True)


def _proj_call(x, mod3, wpt, conv_w):
    bsz, t, _ = x.shape
    tm = min(512, t)

    def body(x_ref, mod_ref, w_ref, pg_ref, pq_ref, pd_ref):
        h = (x_ref[0] * (1.0 + mod_ref[0, 1:2, :]) + mod_ref[0, 0:1, :]).astype(MXU)
        nt = lambda lo, hi: lax.dot_general(h, w_ref[lo:hi, :], (((1,), (1,)), ((), ())), preferred_element_type=F32)
        pq_ref[0] = nt(W_GLA, W_GLA + W_GQKV)
        pg_ref[0] = nt(0, W_GLA)
        pd_ref[0] = nt(W_GLA + W_GQKV, PW)

    tok = lambda w: pl.BlockSpec((1, tm, w), lambda b, i: (b, i, 0))
    pg, pq, pd = pl.pallas_call(
        body, name="proj", grid=(bsz, t // tm),
        in_specs=[tok(D), pl.BlockSpec((1, 3, D), lambda b, i: (b, 0, 0)), pl.BlockSpec((PW, D), lambda b, i: (0, 0))],
        out_specs=[tok(W_GLA), tok(W_GQKV), tok(W_GDN)],
        out_shape=[jax.ShapeDtypeStruct((bsz, t, w), F32) for w in (W_GLA, W_GQKV, W_GDN)],
        compiler_params=_params(("parallel", "parallel")),
    )(x, mod3, wpt)
    return pg, pq, pd, _conv_fwd_sc_call(pq, conv_w)


def _conv_fwd_sc_call(pq, conv_w):
    bsz, t, w_ = pq.shape
    n = bsz * t
    tiles, R = 32, 16
    per_tile = n // tiles
    assert t % per_tile == 0 and per_tile % R == 0

    def body(x_hbm, w_hbm, o_hbm, buf, obuf, wbuf):
        tile = lax.axis_index("sc_tile") * 2 + lax.axis_index("sc_core")
        pltpu.sync_copy(w_hbm, wbuf)

        @pl.loop(0, per_tile // R)
        def _(p):
            r0 = pl.multiple_of(tile * per_tile + p * R, R)
            prev = pl.multiple_of(jnp.maximum(r0 - 8, 0), 8)
            pltpu.sync_copy(x_hbm.at[pl.ds(prev, 8), :], buf.at[pl.ds(0, 8), :])
            pltpu.sync_copy(x_hbm.at[pl.ds(r0, R), :], buf.at[pl.ds(8, R), :])

            @pl.when(r0 % t == 0)
            def _():
                @pl.loop(0, w_, step=16)
                def _(c):
                    for k in range(5, 8):
                        buf[k, pl.ds(c, 16)] = jnp.zeros((16,), F32)

            @pl.loop(0, w_, step=16)
            def _(c):
                cs = pl.ds(c, 16)
                w0, w1, w2, w3 = (wbuf[k, cs] for k in range(4))
                a, b, cc = buf[5, cs], buf[6, cs], buf[7, cs]
                for s in range(R):
                    e = buf[8 + s, cs]
                    obuf[s, cs] = w0 * a + w1 * b + w2 * cc + w3 * e
                    a, b, cc = b, cc, e

            pltpu.sync_copy(obuf, o_hbm.at[pl.ds(r0, R), :])

    out = pl.kernel(
        body, name="conv_fwd_sc", out_type=jax.ShapeDtypeStruct((n, w_), F32),
        mesh=plsc.VectorSubcoreMesh(core_axis_name="sc_core", subcore_axis_name="sc_tile"),
        scratch_types=[pltpu.VMEM((R + 8, w_), F32), pltpu.VMEM((R, w_), F32), pltpu.VMEM((4, w_), F32)],
    )(pq.reshape(n, w_), conv_w)
    return out.reshape(pq.shape)


def _conv_bwd_call(dconv, pq, conv_w):
    bsz, t, _ = pq.shape
    tt = min(512, t)
    hb = tt // 8
    nt_ = t // tt

    def body(d_ref, dnext_ref, x_ref, w_ref, din_ref, dw_ref, dbuf):
        b, i = pl.program_id(0), pl.program_id(1)

        @pl.when((b == 0) & (i == 0))
        def _():
            dw_ref[...] = jnp.zeros_like(dw_ref)

        dbuf[0:tt, :] = d_ref[0]
        dbuf[tt:, :] = jnp.where(i < nt_ - 1, dnext_ref[0], 0.0)
        rows, unroll = 16, 4
        for j in range(W_GQKV // 128):
            ls = slice(128 * j, 128 * j + 128)
            wj = [jnp.broadcast_to(w_ref[k:k + 1, ls], (rows, 128)) for k in range(4)]

            def trip(r, dw4):
                dw4 = list(dw4)
                for u in range(unroll):
                    base = pl.multiple_of(r * (rows * unroll), rows * unroll) + rows * u
                    win = dbuf[pl.ds(base, rows + 8), ls]
                    xin = x_ref[0, pl.ds(base, rows), ls]
                    acc = None
                    for k in range(4):
                        dsh = win[3 - k:3 - k + rows, :]
                        acc = wj[k] * dsh if acc is None else acc + wj[k] * dsh
                        dw4[k] = dw4[k] + xin * dsh
                    din_ref[0, pl.ds(base, rows), ls] = acc
                return tuple(dw4)

            dw4 = lax.fori_loop(0, tt // (rows * unroll), trip, tuple(jnp.zeros((rows, 128), F32) for _ in range(4)))
            for k in range(4):
                dw_ref[k:k + 1, ls] += jnp.sum(dw4[k], axis=0, keepdims=True)

    tile = pl.BlockSpec((1, tt, W_GQKV), lambda b, i: (b, i, 0))
    return pl.pallas_call(
        body, name="conv_bwd", grid=(bsz, nt_),
        in_specs=[tile, pl.BlockSpec((1, 8, W_GQKV), lambda b, i: (b, jnp.minimum((i + 1) * hb, t // 8 - 1), 0)),
                  tile, pl.BlockSpec((4, W_GQKV), lambda b, i: (0, 0))],
        out_specs=[tile, pl.BlockSpec((8, W_GQKV), lambda b, i: (0, 0))],
        out_shape=[jax.ShapeDtypeStruct(pq.shape, F32), jax.ShapeDtypeStruct((8, W_GQKV), F32)],
        scratch_shapes=[pltpu.VMEM((tt + 8, W_GQKV), F32)],
        compiler_params=_params(("arbitrary", "arbitrary")),
    )(dconv, dconv, pq, conv_w)


SC_TILES = 32
SC_ROWS = 16


def _conv_bwd_sc_call(dconv, pq, conv_w):
    bsz, t, w_ = pq.shape
    n = bsz * t
    per_tile = n // SC_TILES
    pieces = per_tile // SC_ROWS
    R = SC_ROWS
    assert t % per_tile == 0 and per_tile % R == 0

    def body(d_hbm, x_hbm, w_hbm, din_hbm, dwp_hbm, dbuf, xbuf, obuf, wbuf, dwacc):
        tile = lax.axis_index("sc_tile") * 2 + lax.axis_index("sc_core")
        pltpu.sync_copy(w_hbm, wbuf)

        @pl.loop(0, w_, step=16)
        def _(c):
            for k in range(8):
                dwacc[k, pl.ds(c, 16)] = jnp.zeros((16,), F32)

        @pl.loop(0, pieces)
        def _(p):
            r0 = pl.multiple_of(tile * per_tile + p * R, R)
            pltpu.sync_copy(d_hbm.at[pl.ds(r0, R), :], dbuf.at[pl.ds(0, R), :])
            pltpu.sync_copy(x_hbm.at[pl.ds(r0, R), :], xbuf)
            nxt = pl.multiple_of(jnp.minimum(r0 + R, n - 8), 8)
            pltpu.sync_copy(d_hbm.at[pl.ds(nxt, 8), :], dbuf.at[pl.ds(R, 8), :])

            @pl.when((r0 + R) % t == 0)
            def _():
                @pl.loop(0, w_, step=16)
                def _(c):
                    for k in range(3):
                        dbuf[R + k, pl.ds(c, 16)] = jnp.zeros((16,), F32)

            @pl.loop(0, w_, step=16)
            def _(c):
                cs = pl.ds(c, 16)
                w0, w1, w2, w3 = (wbuf[k, cs] for k in range(4))
                a, b, cc = dbuf[0, cs], dbuf[1, cs], dbuf[2, cs]
                s0, s1, s2, s3 = (dwacc[k, cs] for k in range(4))
                for s in range(R):
                    e = dbuf[s + 3, cs]
                    xv = xbuf[s, cs]
                    obuf[s, cs] = w3 * a + w2 * b + w1 * cc + w0 * e
                    s0, s1, s2, s3 = s0 + xv * e, s1 + xv * cc, s2 + xv * b, s3 + xv * a
                    a, b, cc = b, cc, e
                dwacc[0, cs], dwacc[1, cs], dwacc[2, cs], dwacc[3, cs] = s0, s1, s2, s3

            pltpu.sync_copy(obuf, din_hbm.at[pl.ds(r0, R), :])

        pltpu.sync_copy(dwacc, dwp_hbm.at[pl.ds(pl.multiple_of(tile * 8, 8), 8), :])

    din, dwp = pl.kernel(
        body, name="conv_bwd_sc",
        out_type=(jax.ShapeDtypeStruct((n, w_), F32), jax.ShapeDtypeStruct((SC_TILES * 8, w_), F32)),
        mesh=plsc.VectorSubcoreMesh(core_axis_name="sc_core", subcore_axis_name="sc_tile"),
        scratch_types=[pltpu.VMEM((R + 8, w_), F32), pltpu.VMEM((R, w_), F32), pltpu.VMEM((R, w_), F32),
                       pltpu.VMEM((4, w_), F32), pltpu.VMEM((8, w_), F32)],
    )(dconv.reshape(n, w_), pq.reshape(n, w_), conv_w)
    return din.reshape(pq.shape), jnp.sum(dwp.reshape(SC_TILES, 8, w_)[:, :4], axis=0)


def _chunk_specs(nc, rev, bsz, cols, sub=None):
    sub = SUB if sub is None else sub
    steps = nc // sub
    n_of = (lambda n: steps - 1 - n) if rev else (lambda n: n)
    return n_of, [pl.BlockSpec((bsz, sub * CHUNK, w), lambda n, j=j: (0, n_of(n), j)) for w, j in cols]


def _full(shape):
    return pl.BlockSpec(shape, lambda n: (0,) * len(shape))


def _heads(ref, bsz, rows, width, off=0):
    return jnp.stack([ref[b, rows, off + width * h:off + width * (h + 1)] for b in range(bsz) for h in range(H)])


def _chunk_rows(sub):
    return slice(CHUNK * sub, CHUNK * (sub + 1))


def _per_head(ref, bsz, width, rows=slice(None)):
    return jnp.stack([ref[rows, width * h:width * (h + 1)] for _ in range(bsz) for h in range(H)])


_GLA_COLS = [(256, 0), (256, 1), (512, 1), (512, 2), (128, 12)]


def _gla_args(refs, bsz, rows):
    q_ref, k_ref, v_ref, og_ref, lr_ref, wgu_ref, bg_ref, nw_ref = refs
    lr = jnp.stack([lr_ref[b, rows, :] for b in range(bsz) for _ in range(H)])
    return (_heads(q_ref, bsz, rows, 64), _heads(k_ref, bsz, rows, 64), _heads(v_ref, bsz, rows, 128), lr,
            _heads(og_ref, bsz, rows, 128), _per_head(wgu_ref, bsz, 64), _per_head(bg_ref, bsz, 64), nw_ref[...])


def _gla_fwd_call(pg, wgu, bg, nw):
    bsz, t, _ = pg.shape
    nc = t // CHUNK
    nh = bsz * H
    _, specs = _chunk_specs(nc, False, bsz, _GLA_COLS)

    def body(q_ref, k_ref, v_ref, og_ref, lr_ref, wgu_ref, bg_ref, nw_ref, y_ref, sh_ref, s_ref):
        @pl.when(pl.program_id(0) == 0)
        def _():
            s_ref[...] = jnp.zeros_like(s_ref)

        s = s_ref[...]
        for sub in range(SUB):
            rows = _chunk_rows(sub)
            q, k, v, lr, og, w, b_, nw_ = _gla_args((q_ref, k_ref, v_ref, og_ref, lr_ref, wgu_ref, bg_ref, nw_ref), bsz, rows)
            sh_ref[sub] = s
            y, s = gla_chunk(q, k, v, lr, og, s, w, b_, nw_)
            for b in range(bsz):
                for h in range(H):
                    y_ref[b, rows, 128 * h:128 * h + 128] = y[H * b + h].astype(MXU)
        s_ref[...] = s

    return pl.pallas_call(
        body, name="gla_fwd", grid=(nc // SUB,),
        in_specs=specs + [_full((128, 256)), _full((1, 256)), _full((1, 128))],
        out_specs=[pl.BlockSpec((bsz, SUB * CHUNK, 512), lambda n: (0, n, 0)),
                   pl.BlockSpec((SUB, nh, GLA_DK, DV), lambda n: (n, 0, 0, 0))],
        out_shape=[jax.ShapeDtypeStruct((bsz, t, 512), MXU), jax.ShapeDtypeStruct((nc, nh, GLA_DK, DV), F32)],
        scratch_shapes=[pltpu.VMEM((nh, GLA_DK, DV), F32)],
        compiler_params=_params(("arbitrary",)),
    )(pg, pg, pg, pg, pg, wgu, bg, nw)


def _gla_bwd_call(pg, s_hist, dyin, wgu, bg, nw):
    bsz, t, _ = pg.shape
    nc = t // CHUNK
    nh = bsz * H
    n_of, specs = _chunk_specs(nc, True, bsz, _GLA_COLS)

    def body(q_ref, k_ref, v_ref, og_ref, lr_ref, sh_ref, dy_ref, wgu_ref, bg_ref, nw_ref,
             dp_ref, dwgu_ref, dbg_ref, dnw_ref, ds_ref):
        @pl.when(pl.program_id(0) == 0)
        def _():
            dwgu_ref[...] = jnp.zeros_like(dwgu_ref)
            dbg_ref[...] = jnp.zeros_like(dbg_ref)
            dnw_ref[...] = jnp.zeros_like(dnw_ref)
            ds_ref[...] = jnp.zeros_like(ds_ref)

        ds = ds_ref[...]
        for sub in reversed(range(SUB)):
            rows = _chunk_rows(sub)
            q, k, v, lr, og, w, b_, nw_ = _gla_args((q_ref, k_ref, v_ref, og_ref, lr_ref, wgu_ref, bg_ref, nw_ref), bsz, rows)
            _, vjp = jax.vjp(gla_chunk, q, k, v, lr, og, sh_ref[sub], w, b_, nw_)
            dq, dk, dv, dlr, dog, ds, dwgu, dbg, dnw = vjp((_heads(dy_ref, bsz, rows, 128), ds))
            dnw_ref[...] += dnw
            for b in range(bsz):
                for h in range(H):
                    i = H * b + h
                    dp_ref[b, rows, 512 + 128 * h:512 + 128 * h + 128] = dv[i].astype(MXU)
                    dp_ref[b, rows, 1024 + 128 * h:1024 + 128 * h + 128] = dog[i].astype(MXU)
                    dwgu_ref[:, 64 * h:64 * h + 64] += dwgu[i]
                    dbg_ref[:, 64 * h:64 * h + 64] += dbg[i]
                for j in range(H // 2):
                    dp_ref[b, rows, 128 * j:128 * j + 128] = jnp.concatenate(
                        [dq[H * b + 2 * j], dq[H * b + 2 * j + 1]], axis=-1).astype(MXU)
                    dp_ref[b, rows, 256 + 128 * j:256 + 128 * j + 128] = jnp.concatenate(
                        [dk[H * b + 2 * j], dk[H * b + 2 * j + 1]], axis=-1).astype(MXU)
                dp_ref[b, rows, 1536:1664] = (dlr[H * b] + dlr[H * b + 1] + dlr[H * b + 2] + dlr[H * b + 3]).astype(MXU)
        ds_ref[...] = ds

    return pl.pallas_call(
        body, name="gla_bwd", grid=(nc // SUB,),
        in_specs=specs + [pl.BlockSpec((SUB, nh, GLA_DK, DV), lambda n: (n_of(n), 0, 0, 0)),
                          pl.BlockSpec((bsz, SUB * CHUNK, 512), lambda n: (0, n_of(n), 0)),
                          _full((128, 256)), _full((1, 256)), _full((1, 128))],
        out_specs=[pl.BlockSpec((bsz, SUB * CHUNK, W_GLA), lambda n: (0, n_of(n), 0)),
                   _full((128, 256)), _full((1, 256)), _full((1, 128))],
        out_shape=[jax.ShapeDtypeStruct(pg.shape, MXU), jax.ShapeDtypeStruct((128, 256), F32),
                   jax.ShapeDtypeStruct((1, 256), F32), jax.ShapeDtypeStruct((1, 128), F32)],
        scratch_shapes=[pltpu.VMEM((nh, GLA_DK, DV), F32)],
        compiler_params=_params(("arbitrary",)),
    )(pg, pg, pg, pg, pg, s_hist, dyin, wgu, bg, nw)


_GDN_COLS = [(512, 0), (512, 1), (512, 2), (512, 0), (128, 4)]


def _gdn_args(refs, bsz, rows):
    q_ref, k_ref, v_ref, og_ref, ab_ref, sc_ref, nw_ref = refs
    return (_heads(q_ref, bsz, rows, 128), _heads(k_ref, bsz, rows, 128), _heads(v_ref, bsz, rows, 128),
            _heads(ab_ref, bsz, rows, 1), _heads(ab_ref, bsz, rows, 1, off=H), _heads(og_ref, bsz, rows, 128),
            _per_head(sc_ref, bsz, 1, slice(0, 1)), _per_head(sc_ref, bsz, 1, slice(1, 2)), nw_ref[...])


def _gdn_fwd_call(conv, pd, sc, nw):
    bsz, t, _ = conv.shape
    nc = t // CHUNK
    nh = bsz * H
    _, specs = _chunk_specs(nc, False, bsz, _GDN_COLS)

    def body(q_ref, k_ref, v_ref, og_ref, ab_ref, sc_ref, nw_ref, y_ref, sh_ref, th_ref, s_ref):
        @pl.when(pl.program_id(0) == 0)
        def _():
            s_ref[...] = jnp.zeros_like(s_ref)

        s = s_ref[...]
        for sub in range(SUB):
            rows = _chunk_rows(sub)
            q, k, v, a, bb, og, alog, dtb, nw_ = _gdn_args((q_ref, k_ref, v_ref, og_ref, ab_ref, sc_ref, nw_ref), bsz, rows)
            sh_ref[sub] = s
            y, s, tinv = gdn_chunk(q, k, v, a, bb, og, s, alog, dtb, nw_)
            th_ref[sub] = tinv.astype(MXU)
            for b in range(bsz):
                for h in range(H):
                    y_ref[b, rows, 128 * h:128 * h + 128] = y[H * b + h].astype(MXU)
        s_ref[...] = s

    return pl.pallas_call(
        body, name="gdn_fwd", grid=(nc // SUB,),
        in_specs=specs + [_full((2, 128)), _full((1, 128))],
        out_specs=[pl.BlockSpec((bsz, SUB * CHUNK, 512), lambda n: (0, n, 0)),
                   pl.BlockSpec((SUB, nh, DV, DV), lambda n: (n, 0, 0, 0)),
                   pl.BlockSpec((SUB, nh, CHUNK, CHUNK), lambda n: (n, 0, 0, 0))],
        out_shape=[jax.ShapeDtypeStruct((bsz, t, 512), MXU), jax.ShapeDtypeStruct((nc, nh, DV, DV), F32),
                   jax.ShapeDtypeStruct((nc, nh, CHUNK, CHUNK), MXU)],
        scratch_shapes=[pltpu.VMEM((nh, DV, DV), F32)],
        compiler_params=_params(("arbitrary",)),
    )(conv, conv, conv, pd, pd, sc, nw)


def _gdn_bwd_call(conv, pd, s_hist, t_hist, dyin, sc, nw):
    bsz, t, _ = conv.shape
    nc = t // CHUNK
    nh = bsz * H
    n_of, specs = _chunk_specs(nc, True, bsz, _GDN_COLS, SUB_GDN_BWD)

    def body(q_ref, k_ref, v_ref, og_ref, ab_ref, sh_ref, th_ref, dy_ref, sc_ref, nw_ref,
             dc_ref, dpd_ref, dsc_ref, dnw_ref, ds_ref):
        @pl.when(pl.program_id(0) == 0)
        def _():
            dsc_ref[...] = jnp.zeros_like(dsc_ref)
            dnw_ref[...] = jnp.zeros_like(dnw_ref)
            ds_ref[...] = jnp.zeros_like(ds_ref)

        lane = lax.broadcasted_iota(jnp.int32, (CHUNK, 128), 1)
        ds = ds_ref[...]
        for sub in reversed(range(SUB_GDN_BWD)):
            rows = _chunk_rows(sub)
            q, k, v, a, bb, og, alog, dtb, nw_ = _gdn_args((q_ref, k_ref, v_ref, og_ref, ab_ref, sc_ref, nw_ref), bsz, rows)
            _, vjp = jax.vjp(functools.partial(gdn_chunk, tinv=th_ref[sub]), q, k, v, a, bb, og, sh_ref[sub], alog, dtb, nw_)
            dq, dk, dv, da, db, dog, ds, dalog, ddtb, dnw = vjp((_heads(dy_ref, bsz, rows, 128), ds))
            dnw_ref[...] += dnw
            for b in range(bsz):
                dab = jnp.zeros((CHUNK, 128), F32)
                for h in range(H):
                    i = H * b + h
                    dc_ref[b, rows, 128 * h:128 * h + 128] = dq[i]
                    dc_ref[b, rows, 512 + 128 * h:512 + 128 * h + 128] = dk[i]
                    dc_ref[b, rows, 1024 + 128 * h:1024 + 128 * h + 128] = dv[i]
                    dpd_ref[b, rows, 128 * h:128 * h + 128] = dog[i].astype(MXU)
                    dab = dab + jnp.where(lane == h, da[i], 0.0) + jnp.where(lane == H + h, db[i], 0.0)
                    dsc_ref[0:1, h:h + 1] += dalog[i]
                    dsc_ref[1:2, h:h + 1] += ddtb[i]
                dpd_ref[b, rows, 512:640] = dab.astype(MXU)
        ds_ref[...] = ds

    return pl.pallas_call(
        body, name="gdn_bwd", grid=(nc // SUB_GDN_BWD,),
        in_specs=specs + [pl.BlockSpec((SUB_GDN_BWD, nh, DV, DV), lambda n: (n_of(n), 0, 0, 0)),
                          pl.BlockSpec((SUB_GDN_BWD, nh, CHUNK, CHUNK), lambda n: (n_of(n), 0, 0, 0)),
                          pl.BlockSpec((bsz, SUB_GDN_BWD * CHUNK, 512), lambda n: (0, n_of(n), 1)),
                          _full((2, 128)), _full((1, 128))],
        out_specs=[pl.BlockSpec((bsz, SUB_GDN_BWD * CHUNK, W_GQKV), lambda n: (0, n_of(n), 0)),
                   pl.BlockSpec((bsz, SUB_GDN_BWD * CHUNK, W_GDN), lambda n: (0, n_of(n), 0)),
                   _full((2, 128)), _full((1, 128))],
        out_shape=[jax.ShapeDtypeStruct(conv.shape, F32), jax.ShapeDtypeStruct(pd.shape, MXU),
                   jax.ShapeDtypeStruct((2, 128), F32), jax.ShapeDtypeStruct((1, 128), F32)],
        scratch_shapes=[pltpu.VMEM((nh, DV, DV), F32)],
        compiler_params=_params(("arbitrary",)),
    )(conv, conv, conv, pd, pd, s_hist, t_hist, dyin, sc, nw)


def _head_call(x, ya, yb, wout, mod3, lnw, lnb, tgt):
    bsz, t, _ = x.shape
    tm = min(512, t)
    rows = min(256, tm)

    def body(x_ref, ya_ref, yb_ref, w_ref, mod_ref, lnw_ref, lnb_ref, t_ref,
             dyin_ref, dxa_ref, dgate_ref, dw_ref, dlnw_ref, dlnb_ref, loss_ref):
        b, i = pl.program_id(0), pl.program_id(1)

        @pl.when((b == 0) & (i == 0))
        def _():
            dw_ref[...] = jnp.zeros_like(dw_ref)
            dlnw_ref[...] = jnp.zeros_like(dlnw_ref)
            dlnb_ref[...] = jnp.zeros_like(dlnb_ref)
            loss_ref[...] = jnp.zeros_like(loss_ref)

        @pl.when(i == 0)
        def _():
            dgate_ref[...] = jnp.zeros_like(dgate_ref)

        w = w_ref[...]
        parts = [slice(p * rows, (p + 1) * rows) for p in range(tm // rows)]
        yin = [jnp.concatenate([ya_ref[0, rs, :], yb_ref[0, rs, :]], axis=-1).astype(MXU) for rs in parts]
        y = [jnp.dot(yi, w, preferred_element_type=F32) for yi in yin]
        for rs, yi, y_p in zip(parts, yin, y):
            loss, vjp = jax.vjp(head_fn, x_ref[0, rs, :], y_p, mod_ref[0, 2:3, :], lnw_ref[...], lnb_ref[...], t_ref[0, rs, :])
            dx, dy, dgate, dlnw, dlnb, _ = vjp(jnp.ones((1, 1), F32))
            dyb = dy.astype(MXU)
            dyin_ref[0, rs, :] = lax.dot_general(dyb, w, (((1,), (1,)), ((), ())), preferred_element_type=F32)
            dw_ref[...] += lax.dot_general(yi, dyb, (((0,), (0,)), ((), ())), preferred_element_type=F32)
            dxa_ref[0, rs, :] = dx
            dgate_ref[0] += dgate
            dlnw_ref[...] += dlnw
            dlnb_ref[...] += dlnb
            loss_ref[...] += jnp.broadcast_to(loss, (1, 128))

    tok = lambda w, j=0: pl.BlockSpec((1, tm, w), lambda b, i: (b, i, j))
    row = pl.BlockSpec((1, D), lambda b, i: (0, 0))
    return pl.pallas_call(
        body, name="head", grid=(bsz, t // tm),
        in_specs=[tok(D), tok(512), tok(512), pl.BlockSpec((D, D), lambda b, i: (0, 0)),
                  pl.BlockSpec((1, 3, D), lambda b, i: (b, 0, 0)), row, row, tok(D)],
        out_specs=[tok(D), tok(D), pl.BlockSpec((1, 1, D), lambda b, i: (b, 0, 0)),
                   pl.BlockSpec((D, D), lambda b, i: (0, 0)), row, row, pl.BlockSpec((1, 128), lambda b, i: (0, 0))],
        out_shape=[jax.ShapeDtypeStruct(x.shape, F32), jax.ShapeDtypeStruct(x.shape, F32),
                   jax.ShapeDtypeStruct((bsz, 1, D), F32), jax.ShapeDtypeStruct((D, D), F32),
                   jax.ShapeDtypeStruct((1, D), F32), jax.ShapeDtypeStruct((1, D), F32),
                   jax.ShapeDtypeStruct((1, 128), F32)],
        compiler_params=_params(("arbitrary", "arbitrary")),
    )(x, ya, yb, wout, mod3, lnw, lnb, tgt)


def _dh_call(dpg, dpq, dpd, wpt, x, mod3, dxa, smalls):
    bsz, t, _ = x.shape
    tm = min(512, t)
    assert bsz + 1 <= 8

    def body(dg_ref, dq_ref, dd_ref, w_ref, x_ref, mod_ref, dxa_ref, dgate_ref, dlnw_ref, dlnb_ref, dbg_ref, n1_ref, n2_ref,
             dsc_ref, loss_ref, gx_ref, sp_ref):
        b = pl.program_id(0)

        @pl.when((b == 0) & (pl.program_id(1) == 0))
        def _():
            sp_ref[...] = jnp.zeros_like(sp_ref)
            for e in range(bsz):
                sp_ref[e:e + 1, 2 * D:3 * D] = dgate_ref[e]
            off = 0
            for ref in (dlnw_ref, dlnb_ref, dbg_ref, n1_ref, n2_ref):
                sp_ref[bsz:bsz + 1, off:off + ref.shape[1]] = ref[...]
                off += ref.shape[1]
            sp_ref[bsz:bsz + 1, off:off + 128] = dsc_ref[0:1, :]
            sp_ref[bsz:bsz + 1, off + 128:off + 256] = dsc_ref[1:2, :]
            sp_ref[bsz:bsz + 1, SMALL_W:SMALL_W + 128] = loss_ref[...]

        mm = lambda a, lo, hi: jnp.dot(a.astype(MXU), w_ref[lo:hi, :], preferred_element_type=F32)
        dh = mm(dg_ref[0], 0, W_GLA) + mm(dq_ref[0], W_GLA, W_GLA + W_GQKV) + mm(dd_ref[0], W_GLA + W_GQKV, PW)
        gx_ref[0] = dh * (1.0 + mod_ref[0, 1:2, :]) + dxa_ref[0]
        dshift = jnp.sum(dh, axis=0, keepdims=True)
        dscale = jnp.sum(dh * x_ref[0], axis=0, keepdims=True)
        for e in range(bsz):
            @pl.when(b == e)
            def _():
                sp_ref[e:e + 1, 0:D] += dshift
                sp_ref[e:e + 1, D:2 * D] += dscale

    tok = lambda w: pl.BlockSpec((1, tm, w), lambda b, i: (b, i, 0))
    whole = lambda a: pl.BlockSpec(a.shape, lambda b, i: (0,) * a.ndim)
    return pl.pallas_call(
        body, name="dh", grid=(bsz, t // tm),
        in_specs=[tok(W_GLA), tok(W_GQKV), tok(W_GDN), pl.BlockSpec((PW, D), lambda b, i: (0, 0)), tok(D),
                  pl.BlockSpec((1, 3, D), lambda b, i: (b, 0, 0)), tok(D)] + [whole(a) for a in smalls],
        out_specs=[tok(D), pl.BlockSpec((8, SPW), lambda b, i: (0, 0))],
        out_shape=[jax.ShapeDtypeStruct(x.shape, F32), jax.ShapeDtypeStruct((8, SPW), F32)],
        compiler_params=_params(("arbitrary", "arbitrary")),
    )(dpg, dpq, dpd, wpt, x, mod3, dxa, *smalls)


def _dw_call(name, x, mod3, parts):
    bsz, t, _ = x.shape
    tm = min(512, t)
    nsteps = bsz * (t // tm)
    widths = [p.shape[-1] for p in parts]
    rows = sum(widths)

    def body(x_ref, mod_ref, *refs):
        dp_refs, dw_ref, acc = refs[:len(parts)], refs[-2], refs[-1]
        step = pl.program_id(0) * (t // tm) + pl.program_id(1)

        @pl.when(step == 0)
        def _():
            acc[...] = jnp.zeros_like(acc)

        h = (x_ref[0] * (1.0 + mod_ref[0, 1:2, :]) + mod_ref[0, 0:1, :]).astype(MXU)
        lo = 0
        for ref, wd in zip(dp_refs, widths):
            acc[lo:lo + wd, :] += lax.dot_general(ref[0].astype(MXU), h, (((0,), (0,)), ((), ())), preferred_element_type=F32)
            lo += wd

        @pl.when(step == nsteps - 1)
        def _():
            dw_ref[...] = acc[...].astype(dw_ref.dtype)

    tok = lambda w: pl.BlockSpec((1, tm, w), lambda b, i: (b, i, 0))
    return pl.pallas_call(
        body, name=name, grid=(bsz, t // tm),
        in_specs=[tok(D), pl.BlockSpec((1, 3, D), lambda b, i: (b, 0, 0))] + [tok(wd) for wd in widths],
        out_specs=pl.BlockSpec((rows, D), lambda b, i: (0, 0)),
        out_shape=jax.ShapeDtypeStruct((rows, D), WIRE),
        scratch_shapes=[pltpu.VMEM((rows, D), F32)],
        compiler_params=_params(("arbitrary", "arbitrary")),
    )(x, mod3, *parts)


def _adamw(w, g, m, v):
    m = ADAM_B1 * m + (1.0 - ADAM_B1) * g
    v = ADAM_B2 * v + (1.0 - ADAM_B2) * jnp.square(g)
    m_hat = m / (1.0 - ADAM_B1 ** ADAM_STEP)
    v_hat = v / (1.0 - ADAM_B2 ** ADAM_STEP)
    delta = -ADAM_LR * (m_hat / (jnp.sqrt(v_hat) + ADAM_EPS) + ADAM_WD * w)
    return delta, m, v


def _sum8(ref):
    g = ref[0].astype(F32)
    for j in range(1, NDEV):
        g = g + ref[j].astype(F32)
    return g


def _adam_sum_call(name, g8, w, m, v, cols):
    r, c = w.shape

    def body(g_ref, w_ref, m_ref, v_ref, go_ref, d_ref, mo_ref, vo_ref):
        g = _sum8(g_ref)
        go_ref[...] = g
        d_ref[...], mo_ref[...], vo_ref[...] = _adamw(w_ref[...], g, m_ref[...], v_ref[...])

    blk = pl.BlockSpec((r, cols), lambda i: (0, i))
    return pl.pallas_call(
        body, name=name, grid=(c // cols,),
        in_specs=[pl.BlockSpec((NDEV, r, cols), lambda i: (0, 0, i)), blk, blk, blk],
        out_specs=[blk] * 4, out_shape=[jax.ShapeDtypeStruct((r, c), F32)] * 4,
        compiler_params=_params(("parallel",)),
    )(g8, w, m, v)


def _adam_ada_call(c_all, dmod_cols, w, m, v):
    def body(c_ref, dm_ref, w_ref, m_ref, v_ref, go_ref, d_ref, mo_ref, vo_ref):
        g = lax.dot_general(c_ref[...].astype(MXU), dm_ref[...].astype(MXU), (((0,), (0,)), ((), ())),
                            preferred_element_type=F32)
        go_ref[...] = g
        d_ref[...], mo_ref[...], vo_ref[...] = _adamw(w_ref[...], g, m_ref[...], v_ref[...])

    return pl.pallas_call(
        body, name="adam_ada", out_shape=[jax.ShapeDtypeStruct(w.shape, F32)] * 4, compiler_params=_params(),
    )(c_all, dmod_cols, w, m, v)


_SMALL_AT = dict(ln_w=(0, 1024), ln_b=(1024, 1024), b_gate=(2048, 256), gla_nw=(2304, 128), gdn_nw=(2432, 128),
                 a_log=(2560, 4), dt_bias=(2688, 4))


def _adam_small_call(sp_all, bsz, params):
    names = list(params)

    def body(sp_ref, *refs):
        ins, outs = refs[:3 * len(names)], refs[3 * len(names):]
        dmod_ref, loss_ref, outs = outs[0], outs[1], outs[2:]
        packed = sp_ref[0, bsz:bsz + 1, :]
        for j in range(1, NDEV):
            packed = packed + sp_ref[j, bsz:bsz + 1, :]
        gb = None
        for j in range(NDEV):
            dmod_ref[bsz * j:bsz * j + bsz, :] = sp_ref[j, 0:bsz, :]
            for e in range(bsz):
                gb = sp_ref[j, e:e + 1, :] if gb is None else gb + sp_ref[j, e:e + 1, :]
        loss_ref[...] = packed[:, SMALL_W:SMALL_W + 128]
        for i, name in enumerate(names):
            if name == "b_ada":
                g = gb
            else:
                lo, n = _SMALL_AT[name]
                g = packed[:, lo:lo + n]
            w_ref, m_ref, v_ref = ins[3 * i:3 * i + 3]
            g_ref, d_ref, mo_ref, vo_ref = outs[4 * i:4 * i + 4]
            g_ref[...] = g
            d_ref[...], mo_ref[...], vo_ref[...] = _adamw(w_ref[...], g, m_ref[...], v_ref[...])

    flat = [a for name in names for a in params[name]]
    out_shape = [jax.ShapeDtypeStruct((NDEV * bsz, SPW), F32), jax.ShapeDtypeStruct((1, 128), F32)]
    out_shape += [jax.ShapeDtypeStruct(params[name][0].shape, F32) for name in names for _ in range(4)]
    res = pl.pallas_call(body, name="adam_small", out_shape=out_shape, compiler_params=_params())(sp_all, *flat)
    return res[0], res[1], {name: res[2 + 4 * i:6 + 4 * i] for i, name in enumerate(names)}


def _mesh_pos():
    x, y, c = lax.axis_index("x"), lax.axis_index("y"), lax.axis_index("c")
    return x, y, c, 4 * x + 2 * y + c


def _peer(x, y, c, k):
    px = 1 - x if k & 4 else x
    py = 1 - y if k & 2 else y
    pc = 1 - c if k & 1 else c
    return (px, py, pc), 4 * px + 2 * py + pc


_ANY = pl.BlockSpec(memory_space=pl.ANY)
_VMEM = pl.BlockSpec(memory_space=pltpu.VMEM)


def _gather_call(c8, w_ada, b_sh, w_in_t, conv_w):
    C_SEM, W_SEM, MOD_SEM, CONV_SEM = 0, 1, 2, 3

    def body(c_ref, wada_ref, b_ref, win_ref, cw_ref, wall_ref, call_ref, mod_ref, cwall_ref, modp, send_sems, recv_sems, loc_sem):
        x, y, c, me = _mesh_pos()

        def remote(src, dst, a, k, to):
            return pltpu.make_async_remote_copy(src_ref=src, dst_ref=dst, send_sem=send_sems.at[a, k],
                                                recv_sem=recv_sems.at[a, k], device_id=_peer(x, y, c, to)[0],
                                                device_id_type=pl.DeviceIdType.MESH)

        idx = lambda k: _peer(x, y, c, k)[1]
        sends = []
        call_ref[me] = c_ref[...]
        cwall_ref[me] = cw_ref[...]
        for k in range(1, NDEV):
            sends.append(remote(c_ref, call_ref.at[me], C_SEM, k, k))
            sends[-1].start()
            sends.append(remote(cw_ref, cwall_ref.at[me], CONV_SEM, k, k))
            sends[-1].start()
        local = pltpu.make_async_copy(win_ref, wall_ref.at[me], loc_sem)
        local.start()
        for k in (1, 2, 4, 6):
            sends.append(remote(win_ref, wall_ref.at[me], W_SEM, k, k))
            sends[-1].start()
        for k in range(1, NDEV):
            remote(c_ref, call_ref.at[idx(k)], C_SEM, k, k).wait_recv()
        modp[...] = jnp.dot(call_ref[...].reshape(NDEV * 8, D).astype(MXU), wada_ref[...].astype(MXU),
                            preferred_element_type=F32) + b_ref[...]
        mod_ref[me] = modp[pl.ds(pl.multiple_of(me * 8, 8), 8), :]
        for k in range(1, NDEV):
            sends.append(remote(modp.at[pl.ds(pl.multiple_of(idx(k) * 8, 8), 8), :], mod_ref.at[me], MOD_SEM, k, k))
            sends[-1].start()
        for k in (2, 4, 6):
            remote(win_ref, wall_ref.at[idx(k)], W_SEM, k, k).wait_recv()
            sends.append(remote(wall_ref.at[idx(k)], wall_ref.at[idx(k)], W_SEM, k + 1, 1))
            sends[-1].start()
        for k in (1, 3, 5, 7):
            remote(win_ref, wall_ref.at[idx(k)], W_SEM, k, 1).wait_recv()
        for k in range(1, NDEV):
            remote(modp.at[pl.ds(0, 8), :], mod_ref.at[idx(k)], MOD_SEM, k, k).wait_recv()
            remote(cw_ref, cwall_ref.at[idx(k)], CONV_SEM, k, k).wait_recv()
        for cp in sends:
            cp.wait_send()
        local.wait()

    return pl.pallas_call(
        body, name="gather",
        out_shape=[jax.ShapeDtypeStruct((NDEV,) + w_in_t.shape, w_in_t.dtype), jax.ShapeDtypeStruct((NDEV, 8, D), F32),
                   jax.ShapeDtypeStruct((NDEV, 8, SHARD_ADA), F32), jax.ShapeDtypeStruct((NDEV,) + conv_w.shape, F32)],
        in_specs=[_VMEM, _VMEM, _VMEM, _ANY, _VMEM], out_specs=[_ANY, _VMEM, _VMEM, _VMEM],
        scratch_shapes=[pltpu.VMEM((NDEV * 8, SHARD_ADA), F32), pltpu.SemaphoreType.DMA((4, NDEV)),
                        pltpu.SemaphoreType.DMA((4, NDEV)), pltpu.SemaphoreType.DMA],
        compiler_params=_params(),
    )(c8, w_ada, b_sh, w_in_t, conv_w)


_HBM = pl.BlockSpec(memory_space=pltpu.HBM)
_SEM = pl.BlockSpec(memory_space=pltpu.SEMAPHORE)
_EFFECT = pltpu.SideEffectType.DATAFLOW_SIDE_EFFECTING


def _whole(gather, a):
    return gather[a] if isinstance(gather, (list, tuple)) else gather


def _xchg_start(name, blocks, lands, gather):
    nb = len(blocks)

    def body(*refs):
        srcs, dsts = refs[:nb], refs[nb:2 * nb]
        send_sems, recv_sems = refs[2 * nb], refs[2 * nb + 1]
        token = refs[-1]
        x, y, c, me = _mesh_pos()
        for k in range(1, NDEV):
            dev, pidx = _peer(x, y, c, k)
            for a in range(nb):
                pltpu.make_async_remote_copy(src_ref=srcs[a] if _whole(gather, a) else srcs[a].at[pidx], dst_ref=dsts[a].at[me],
                                             send_sem=send_sems.at[NDEV * a + k], recv_sem=recv_sems.at[NDEV * a + k],
                                             device_id=dev, device_id_type=pl.DeviceIdType.MESH).start()
        token[...] = jnp.zeros_like(token)

    thru = [pltpu.HBM(a.shape, a.dtype) for a in list(blocks) + list(lands)]
    return pl.pallas_call(
        body, name=name,
        out_shape=(pltpu.SemaphoreType.DMA((nb * NDEV,)), pltpu.SemaphoreType.DMA((nb * NDEV,)), *thru,
                   jax.ShapeDtypeStruct((8, 128), F32)),
        in_specs=[_HBM] * (2 * nb), out_specs=(_SEM, _SEM, *([_HBM] * (2 * nb)), _VMEM),
        input_output_aliases={i: 2 + i for i in range(2 * nb)},
        compiler_params=pltpu.CompilerParams(has_side_effects=_EFFECT),
    )(*[pltpu.with_memory_space_constraint(a, pltpu.HBM) for a in list(blocks) + list(lands)])


def _xchg_wait(name, send_sems, recv_sems, thru, after, gather):
    nb = len(thru) // 2

    def body(*refs):
        srcs, dsts = refs[:nb], refs[nb:2 * nb]
        send_sems, recv_sems = refs[2 * nb], refs[2 * nb + 1]
        x, y, c, me = _mesh_pos()
        for k in range(1, NDEV):
            dev, pidx = _peer(x, y, c, k)
            for a in range(nb):
                cp = pltpu.make_async_remote_copy(src_ref=srcs[a] if _whole(gather, a) else srcs[a].at[pidx], dst_ref=dsts[a].at[pidx],
                                                  send_sem=send_sems.at[NDEV * a + k], recv_sem=recv_sems.at[NDEV * a + k],
                                                  device_id=dev, device_id_type=pl.DeviceIdType.MESH)
                cp.wait_send()
                cp.wait_recv()

    out = pl.pallas_call(
        body, name=name, out_shape=tuple(pltpu.HBM(a.shape, a.dtype) for a in thru),
        in_specs=[_HBM] * (2 * nb) + [_SEM, _SEM, pl.BlockSpec(memory_space=pl.ANY)], out_specs=tuple([_HBM] * (2 * nb)),
        input_output_aliases={i: i for i in range(2 * nb)},
        compiler_params=pltpu.CompilerParams(has_side_effects=_EFFECT),
    )(*thru, send_sems, recv_sems, after)
    return out[nb:]


def _pad_cols(a, n):
    return jnp.pad(a, ((0, 0), (0, n - a.shape[1])))


_SEGMENTS = ((0, 256, 0), (256, 512, 256), (512, 1024, 512), (1040, 1552, 1024), (1024, 1040, 1536),
             (1552, 3088, W_GLA), (3096, 3608, W_GLA + W_GQKV), (3088, 3096, W_GLA + W_GQKV + 512))


def _row_pieces():
    out = []
    for lo, hi, dst in _SEGMENTS:
        while lo < hi:
            j, off = divmod(lo, SHARD_IN)
            n = min(hi - lo, SHARD_IN - off)
            out.append((j, off, n, dst))
            lo, dst = lo + n, dst + n
    return out


def _relayout_call(a, to_layout):
    cols = 256
    dst_shape = (PW, D) if to_layout else (NDEV, SHARD_IN, D)

    def body(i_ref, o_ref, scr):
        if to_layout:
            scr[...] = jnp.zeros_like(scr)
        for j, off, n, at in _row_pieces():
            if to_layout:
                scr[at:at + n, :] = i_ref[j, off:off + n, :].astype(F32)
            else:
                scr[j, off:off + n, :] = i_ref[at:at + n, :].astype(F32)
        o_ref[...] = scr[...].astype(o_ref.dtype)

    blk = lambda shape: pl.BlockSpec(shape[:-1] + (cols,), lambda i: (0,) * (len(shape) - 1) + (i,))
    return pl.pallas_call(
        body, name="to_layout" if to_layout else "to_shards", grid=(D // cols,),
        in_specs=[blk(a.shape)], out_specs=blk(dst_shape), out_shape=jax.ShapeDtypeStruct(dst_shape, a.dtype),
        scratch_shapes=[pltpu.VMEM(dst_shape[:-1] + (cols,), F32)],
        compiler_params=_params(("parallel",)),
    )(a)


def _assemble_wt(blocks):
    return _relayout_call(blocks, True)


def _disassemble_dwt(dwt):
    return _relayout_call(dwt, False)


def local_grads(x, mod3, wp, conv_w, late_weights, bg, gla_nw, sc, gdn_nw, lnw, lnb, tgt):
    pg, pq, pd, conv = _proj_call(x, mod3, wp, conv_w)
    wout, wgu_p = late_weights(pq)
    ya, s_gla = _gla_fwd_call(pg, wgu_p, bg, gla_nw)
    yb, s_gdn, t_gdn = _gdn_fwd_call(conv, pd, sc, gdn_nw)
    dyin, dxa, dgate, dwout, dlnw, dlnb, loss = _head_call(x, ya, yb, wout, mod3, lnw, lnb, tgt)
    dconv, dpd, dsc, dnw_gdn = _gdn_bwd_call(conv, pd, s_gdn, t_gdn, dyin, sc, gdn_nw)
    dpq, dconv_w = _conv_bwd_sc_call(dconv, pq, conv_w)
    dpg, dwgu, dbg, dnw_gla = _gla_bwd_call(pg, s_gla, dyin, wgu_p, bg, gla_nw)
    dwt_a = _dw_call("dw_a", x, mod3, [dpg, dpd])
    dwt_b = _dw_call("dw_b", x, mod3, [dpq])
    dw_in = _disassemble_dwt(jnp.concatenate([dwt_a[:W_GLA], dwt_b, dwt_a[W_GLA:]], axis=0))
    g = dict(dw_in=dw_in, dwout=dwout, dconv_w=dconv_w, dwgu=dwgu[:16])
    smalls = (dgate, dlnw, dlnb, dbg, dnw_gla, dnw_gdn, dsc, loss)
    return g, lambda mod3_: _dh_call(dpg, dpq, dpd, wp, x, mod3_, dxa, smalls)


def local_step(x, mod3, wp, wout, conv_w, wgu_p, *args):
    g, finish = local_grads(x, mod3, wp, conv_w, lambda _: (wout, wgu_p), *args)
    g["gx"], sp = finish(mod3)
    bsz = x.shape[0]
    g["dmod"] = sp[:bsz].reshape(bsz, 3, D)
    g["loss"] = sp[bsz, SMALL_W]
    for name, (lo, n) in _SMALL_AT.items():
        g[name] = sp[bsz:bsz + 1, lo:lo + n]
    return g


def kernel(x, c, w_ada, b_ada, w_in, gla_w_gate_up, gla_b_gate, gla_norm_w, gdn_conv_w, gdn_a_log, gdn_dt_bias, gdn_norm_w, w_out, ln_w, ln_b, loss_target, m_w_ada, m_b_ada, m_w_in, m_gla_w_gate_up, m_gla_b_gate, m_gla_norm_w, m_gdn_conv_w, m_gdn_a_log, m_gdn_dt_bias, m_gdn_norm_w, m_w_out, m_ln_w, m_ln_b, v_w_ada, v_b_ada, v_w_in, v_gla_w_gate_up, v_gla_b_gate, v_gla_norm_w, v_gdn_conv_w, v_gdn_a_log, v_gdn_dt_bias, v_gdn_norm_w, v_w_out, v_ln_w, v_ln_b):
    me = 4 * lax.axis_index("x") + 2 * lax.axis_index("y") + lax.axis_index("c")
    bsz = x.shape[0]

    b_sh = lax.dynamic_slice(b_ada, (0, me * SHARD_ADA), (1, SHARD_ADA))
    c8 = jnp.pad(c, ((0, 8 - bsz), (0, 0)))
    w_in_t, m_in_t, v_in_t = (jnp.swapaxes(a[0], 0, 1) for a in (w_in, m_w_in, v_w_in))
    win_all, c_all, mod_blk, conv_all = _gather_call(c8, w_ada[0], b_sh, w_in_t.astype(WIRE), gdn_conv_w[0])
    conv_w = jnp.transpose(conv_all, (1, 0, 2)).reshape(4, W_GQKV)
    wp = _assemble_wt(win_all)
    mod = jnp.transpose(mod_blk[:, :bsz, :], (1, 0, 2)).reshape(bsz, 3 * D)
    mod3 = mod.reshape(bsz, 3, D)
    sc = jnp.concatenate([_pad_cols(gdn_a_log, 128), _pad_cols(gdn_dt_bias, 128)], axis=0)

    own = lambda a: lax.dynamic_update_slice(lax.empty((NDEV,) + a.shape, a.dtype), a[None], (me,) + (0,) * a.ndim)
    late = [w_out[0].astype(WIRE), gla_w_gate_up[0] + 0.0 * mod_blk[0, 0, 0]]
    w_send, w_recv, *w_thru, w_token = _xchg_start("wgather_start", late, [own(a) for a in late], gather=True)

    def late_weights(pq):
        wout_all, wgu_all = _xchg_wait("wgather_wait", w_send, w_recv, w_thru, pq, gather=True)
        return wout_all.reshape(D, D), jnp.pad(jnp.transpose(wgu_all, (1, 0, 2)).reshape(16, 256), ((0, 112), (0, 0)))

    g, finish = local_grads(x, mod3 + w_token[0, 0], wp, conv_w, late_weights, gla_b_gate, gla_norm_w, sc, gdn_norm_w, ln_w, ln_b,
                            loss_target)

    big = [g["dw_in"], g["dwout"].reshape(NDEV, D // NDEV, D).astype(WIRE)]
    lands = [lax.dynamic_update_slice(lax.empty(a.shape, a.dtype), lax.dynamic_slice(a, (me, 0, 0), (1,) + a.shape[1:]),
                                      (me, 0, 0)) for a in big]
    send_sems, recv_sems, *thru, token = _xchg_start("xchg_start", big, lands, gather=False)
    gx, sp = finish(mod3 + token[0, 0])
    little = [jnp.transpose(g["dconv_w"].reshape(4, NDEV, W_GQKV // NDEV), (1, 0, 2)),
              jnp.transpose(g["dwgu"].reshape(16, NDEV, 256 // NDEV), (1, 0, 2)), sp]
    modes = [False, False, True]
    l_lands = [lax.dynamic_update_slice(lax.empty(a.shape, a.dtype), lax.dynamic_slice(a, (me, 0, 0), (1,) + a.shape[1:]),
                                        (me, 0, 0)) for a in little[:2]] + [own(sp)]
    l_send, l_recv, *l_thru, l_token = _xchg_start("small_start", little, l_lands, gather=modes)
    r_in, r_out = _xchg_wait("xchg_wait", send_sems, recv_sems, thru, l_token, gather=False)

    t_in = [jnp.swapaxes(a, 0, 1) for a in _adam_sum_call("adam_in", r_in, w_in_t, m_in_t, v_in_t, 256)]
    t_out = _adam_sum_call("adam_out", r_out, w_out[0], m_w_out[0], v_w_out[0], D)
    r_conv, r_gu, sp_all = _xchg_wait("small_wait", l_send, l_recv, l_thru, t_out[3], gather=modes)
    t_conv = _adam_sum_call("adam_conv", r_conv, gdn_conv_w[0], m_gdn_conv_w[0], v_gdn_conv_w[0], W_GQKV // NDEV)
    t_gu = _adam_sum_call("adam_gu", r_gu, gla_w_gate_up[0], m_gla_w_gate_up[0], v_gla_w_gate_up[0], 256 // NDEV)
    dmod_all, loss, small = _adam_small_call(sp_all, bsz, dict(
        b_ada=(b_ada, m_b_ada, v_b_ada), ln_w=(ln_w, m_ln_w, v_ln_w), ln_b=(ln_b, m_ln_b, v_ln_b),
        b_gate=(gla_b_gate, m_gla_b_gate, v_gla_b_gate), gla_nw=(gla_norm_w, m_gla_norm_w, v_gla_norm_w),
        gdn_nw=(gdn_norm_w, m_gdn_norm_w, v_gdn_norm_w), a_log=(gdn_a_log, m_gdn_a_log, v_gdn_a_log),
        dt_bias=(gdn_dt_bias, m_gdn_dt_bias, v_gdn_dt_bias)))
    c16 = c_all[:, :bsz, :].reshape(NDEV * bsz, D)
    t_ada = _adam_ada_call(c16, lax.dynamic_slice(dmod_all, (0, me * SHARD_ADA), (NDEV * bsz, SHARD_ADA)),
                           w_ada[0], m_w_ada[0], v_w_ada[0])

    def group(i):
        s = lambda name: small[name][i]
        return [t_ada[i][None], s("b_ada"), t_in[i][None], t_gu[i][None], s("b_gate"), s("gla_nw"), t_conv[i][None],
                s("a_log"), s("dt_bias"), s("gdn_nw"), t_out[i][None], s("ln_w"), s("ln_b")]

    return (loss[0, 0], gx, *group(0), *group(1), *group(2), *group(3))
```
